```python
import jax
import jax.numpy as jnp
from jax import lax
import numpy as np

D_MODEL = 4096
BATCH = 4
SEQ = 2048
DEPTH = 2
DEC_BATCH = 8
DEC_SEQ = 1
PAST_LEN = 16384
PAGE_SIZE = 128

N_A_LAYERS = DEPTH // 2
N_B_LAYERS = DEPTH - N_A_LAYERS
D_FF = 4 * D_MODEL
PLE_DIM = 256
POOL_WINDOWS = (2, 4, 8, 16)
POOL_GROUP = D_MODEL // len(POOL_WINDOWS)
POOL_STATE = max(POOL_WINDOWS) - 1
HEAD_DIM = 128
N_HEADS = D_MODEL // HEAD_DIM
N_KV_HEADS = 4
GROUP_SIZE = N_HEADS // N_KV_HEADS
N_BRANCH = 3
N_KV_TENSORS = 6
CMP_BLOCK = 32
CMP_STRIDE = 16
CMP_HIDDEN = 2 * HEAD_DIM
SEL_BLOCK = 64
N_SEL = 16
WINDOW = 512
SEL_QBLOCK = 32
WIN_QBLOCK = 128
ROPE_THETA = 10000.0
EPS = 1e-6
SCALE = HEAD_DIM ** -0.5
NEG = -1e30
FORCE = 1e9
INVALID_POS = -(2 ** 30)

kernel_name = 'yoco_pool_nsa_decoder_step'


def rmsnorm(x, g):
    xf = x.astype(jnp.float32)
    y = xf * lax.rsqrt(jnp.mean(xf * xf, axis=-1, keepdims=True) + EPS)
    return (y * g.astype(jnp.float32)).astype(x.dtype)


def rope(x, pos):
    half = HEAD_DIM // 2
    inv = ROPE_THETA ** (-jnp.arange(half, dtype=jnp.float32) / half)
    ang = pos.astype(jnp.float32)[:, None] * inv[None, :]
    shape = (1, pos.shape[0]) + (1,) * (x.ndim - 3) + (half,)
    cos = jnp.cos(ang).reshape(shape)
    sin = jnp.sin(ang).reshape(shape)
    xf = x.astype(jnp.float32)
    x1, x2 = xf[..., :half], xf[..., half:]
    return jnp.concatenate([x1 * cos - x2 * sin, x2 * cos + x1 * sin], axis=-1).astype(x.dtype)


def _pad_time(x, mult):
    pad = (-x.shape[1]) % mult
    return jnp.pad(x, ((0, 0), (0, pad)) + ((0, 0),) * (x.ndim - 2))


def _blocks(x, axis, qb, nb):
    widths = [(0, 0)] * x.ndim
    widths[axis] = (0, nb * qb - x.shape[axis])
    x = jnp.pad(x, widths)
    x = x.reshape(x.shape[:axis] + (nb, qb) + x.shape[axis + 1:])
    return jnp.moveaxis(x, axis, 0)


def _unblocks(y, t):
    y = jnp.moveaxis(y, 0, 1)
    y = y.reshape((y.shape[0], y.shape[1] * y.shape[2]) + y.shape[3:])
    return y[:, :t]


def pool_mixer(a, prefix, pos, w_pool, scale):
    B, T, _ = a.shape
    seq = jnp.concatenate([prefix, a], axis=1)
    cs = jnp.pad(jnp.cumsum(seq.astype(jnp.float32), axis=1), ((0, 0), (1, 0), (0, 0)))
    upto = cs[:, POOL_STATE + 1:]
    x_t = seq[:, POOL_STATE:].astype(jnp.float32)
    diffs = []
    for g, w in enumerate(POOL_WINDOWS):
        c = slice(g * POOL_GROUP, (g + 1) * POOL_GROUP)
        before = cs[:, POOL_STATE + 1 - w: POOL_STATE + 1 - w + T, c]
        cnt = jnp.minimum(pos + 1, w).astype(jnp.float32)[None, :, None]
        diffs.append((upto[..., c] - before) / cnt - x_t[..., c])
    d = jnp.stack(diffs, axis=2).astype(a.dtype)
    out = jnp.einsum('btgc,gce->btge', d, w_pool).reshape(B, T, D_MODEL)
    return out * scale, seq[:, -POOL_STATE:]


def shared_kv_rows(s, pos, g_kv, w_kv, g_k_sel, g_k_win):
    hkv = rmsnorm(s, g_kv)
    kv = jnp.einsum('btd,dngk->btngk', hkv, w_kv)
    k_cmp = kv[:, :, 0]
    v_cmp = kv[:, :, 1]
    k_sel = rope(rmsnorm(kv[:, :, 2], g_k_sel), pos)
    v_sel = kv[:, :, 3]
    k_win = rope(rmsnorm(kv[:, :, 4], g_k_win), pos)
    v_win = kv[:, :, 5]
    return (k_cmp, v_cmp, k_sel, v_sel, k_win, v_win)


def compress(raw, w1, w2, pe):
    B, Tk, G, hd = raw.shape
    n_sub = Tk // CMP_STRIDE
    ratio = CMP_BLOCK // CMP_STRIDE
    sub = raw.reshape(B, n_sub, CMP_STRIDE, G, hd).transpose(0, 1, 3, 2, 4).reshape(B, n_sub, G, CMP_STRIDE * hd)
    w1s = w1.reshape(ratio, CMP_STRIDE * hd, CMP_HIDDEN)
    nc = n_sub - ratio + 1
    pre = pe.reshape(-1) @ w1
    for i in range(ratio):
        pre = pre + jnp.einsum('bngc,ch->bngh', sub[:, i:i + nc], w1s[i])
    return jnp.einsum('bngh,hd->bngd', jax.nn.silu(pre), w2)


def cmp_sel_branches(q, qr, q_pos, ck, cv, ksb, vsb):
    Tq = q.shape[1]
    nc = ck.shape[1]
    ns = ksb.shape[2]
    ratio = CMP_BLOCK // CMP_STRIDE
    per_sel = SEL_BLOCK // CMP_STRIDE
    n_sel = min(N_SEL, ns)
    cmp_end = jnp.arange(nc) * CMP_STRIDE + CMP_BLOCK - 1
    blk = jnp.arange(ns)
    qb = min(SEL_QBLOCK, Tq)
    nb = -(-Tq // qb)
    gather = jax.vmap(jax.vmap(lambda blocks, ids: blocks[ids]))

    def body(args):
        qblk, rblk, pblk = args
        ok_c = (cmp_end[None, :] <= pblk[:, None])[None, :, None, None, :]
        s_c = jnp.einsum('bqgrd,bngd->bqgrn', qblk, ck).astype(jnp.float32) * SCALE
        p_c = jnp.where(ok_c, jax.nn.softmax(jnp.where(ok_c, s_c, NEG), axis=-1), 0.0)
        o_cmp = jnp.einsum('bqgrn,bngd->bqgrd', p_c.astype(cv.dtype), cv)
        imp = jnp.pad(p_c.sum(axis=3), ((0, 0), (0, 0), (0, 0), (ratio - 1, ratio - 1)))
        p_slc = 0.0
        for m in range(per_sel):
            for n in range(ratio):
                o0 = m - n + ratio - 1
                p_slc = p_slc + imp[..., o0:o0 + per_sel * (ns - 1) + 1:per_sel]
        cur = pblk // SEL_BLOCK
        vis = blk[None, :] * SEL_BLOCK <= pblk[:, None]
        forced = vis & ((blk[None, :] == 0) | (blk[None, :] == cur[:, None]) | (blk[None, :] == cur[:, None] - 1))
        score = jnp.where(vis[None, :, None, :], p_slc, NEG)
        score = jnp.where(forced[None, :, None, :], FORCE, score)
        top_s, idx = lax.top_k(score, n_sel)
        ids = idx.transpose(0, 2, 1, 3)
        valid = (top_s > 0.5 * NEG).transpose(0, 2, 1, 3)
        kg = gather(ksb, ids)
        vg = gather(vsb, ids)
        s = jnp.einsum('bqgrd,bgqnld->bqgrnl', rblk, kg).astype(jnp.float32) * SCALE
        kpos = ids[..., None] * SEL_BLOCK + jnp.arange(SEL_BLOCK)
        ok = (kpos <= pblk[None, None, :, None, None]) & valid[..., None]
        ok = ok.transpose(0, 2, 1, 3, 4)[:, :, :, None]
        s = jnp.where(ok, s, NEG)
        shp = s.shape
        p = jax.nn.softmax(s.reshape(shp[:4] + (-1,)), axis=-1).reshape(shp)
        p = jnp.where(ok, p, 0.0).astype(vg.dtype)
        o_sel = jnp.einsum('bqgrnl,bgqnld->bqgrd', p, vg)
        return o_cmp, o_sel

    o_cmp, o_sel = lax.map(body, (_blocks(q, 1, qb, nb), _blocks(qr, 1, qb, nb), _blocks(q_pos, 0, qb, nb)))
    return _unblocks(o_cmp, Tq), _unblocks(o_sel, Tq)


def window_branch(qr, q_pos, kw, vw, kw_pos):
    Tq = qr.shape[1]
    off = kw.shape[1] - Tq
    qb = min(WIN_QBLOCK, Tq)
    nb = -(-Tq // qb)
    pad_q = nb * qb - Tq
    kp = jnp.pad(kw, ((0, 0), (WINDOW, pad_q), (0, 0), (0, 0)))
    vp = jnp.pad(vw, ((0, 0), (WINDOW, pad_q), (0, 0), (0, 0)))
    pp = jnp.pad(kw_pos, (WINDOW, pad_q), constant_values=INVALID_POS)
    span = qb + WINDOW

    def body(args):
        i, qblk, pblk = args
        start = off + i * qb
        kb = lax.dynamic_slice_in_dim(kp, start, span, axis=1)
        vb = lax.dynamic_slice_in_dim(vp, start, span, axis=1)
        pk = lax.dynamic_slice_in_dim(pp, start, span, axis=0)
        s = jnp.einsum('bqgrd,bkgd->bqgrk', qblk, kb).astype(jnp.float32) * SCALE
        ok = (pk[None, :] <= pblk[:, None]) & (pk[None, :] > pblk[:, None] - WINDOW)
        ok = ok[None, :, None, None, :]
        p = jnp.where(ok, jax.nn.softmax(jnp.where(ok, s, NEG), axis=-1), 0.0).astype(vb.dtype)
        return jnp.einsum('bqgrk,bkgd->bqgrd', p, vb)

    out = lax.map(body, (jnp.arange(nb), _blocks(qr, 1, qb, nb), _blocks(q_pos, 0, qb, nb)))
    return _unblocks(out, Tq)


def nsa_mixer(a, pos, ck, cv, ksb, vsb, win_ctx, w_qg, g_q, w_o):
    B, T, _ = a.shape
    qg = jnp.einsum('btd,de->bte', a, w_qg)
    q = rmsnorm(qg[..., :N_HEADS * HEAD_DIM].reshape(B, T, N_KV_HEADS, GROUP_SIZE, HEAD_DIM), g_q)
    gates = jax.nn.sigmoid(qg[..., N_HEADS * HEAD_DIM:].astype(jnp.float32)).reshape(B, T, N_KV_HEADS, GROUP_SIZE, N_BRANCH)
    qr = rope(q, pos)
    o_cmp, o_sel = cmp_sel_branches(q, qr, pos, ck, cv, ksb, vsb)
    kw, vw, kw_pos = win_ctx
    o_win = window_branch(qr, pos, kw, vw, kw_pos)
    o = gates[..., 0:1] * o_cmp + gates[..., 1:2] * o_sel + gates[..., 2:3] * o_win
    return jnp.einsum('bte,ed->btd', o.astype(a.dtype).reshape(B, T, N_HEADS * HEAD_DIM), w_o)


def trunk(x, p, pos, pool_prefix, make_ctx, W):
    h = x
    pool_new = []
    kv_rows = None
    win_state = None
    ck = cv = ksb = vsb = win_ctx = None
    for layer in range(DEPTH):
        a = rmsnorm(h, W['g_mix'][layer])
        if layer < N_A_LAYERS:
            mix, st = pool_mixer(a, pool_prefix[layer], pos, W['w_pool'][layer], W['pool_scale'][layer])
            pool_new.append(st)
        else:
            j = layer - N_A_LAYERS
            if j == 0:
                kv_rows = shared_kv_rows(h, pos, W['g_kv'], W['w_kv'], W['g_k_sel'], W['g_k_win'])
                (kc, vc, ksl, vsl), win_ctx, win_state = make_ctx(kv_rows)
                ck = rmsnorm(compress(kc, W['w_cmp_k1'], W['w_cmp_k2'], W['pe_cmp_k']), W['g_k_cmp'])
                cv = compress(vc, W['w_cmp_v1'], W['w_cmp_v2'], W['pe_cmp_v'])
                Bk, Tk = ksl.shape[:2]
                ns = Tk // SEL_BLOCK
                ksb = ksl.reshape(Bk, ns, SEL_BLOCK, N_KV_HEADS, HEAD_DIM).transpose(0, 3, 1, 2, 4)
                vsb = vsl.reshape(Bk, ns, SEL_BLOCK, N_KV_HEADS, HEAD_DIM).transpose(0, 3, 1, 2, 4)
            mix = nsa_mixer(a, pos, ck, cv, ksb, vsb, win_ctx, W['w_qg'][j], W['g_q'][j], W['w_o'][j])
        h = h + mix
        m = rmsnorm(h, W['g_ffn'][layer])
        u = jnp.square(jax.nn.relu(jnp.einsum('btd,df->btf', m, W['w_up'][layer])))
        h = h + jnp.einsum('btf,fd->btd', u, W['w_down'][layer])
        gate = jax.nn.sigmoid(jnp.einsum('btd,de->bte', rmsnorm(h, W['g_ple'][layer]), W['w_ple_gate'][layer]).astype(jnp.float32))
        h = h + (jnp.einsum('btk,kd->btd', p[layer], W['w_ple'][layer]) * gate).astype(h.dtype)
    return h, jnp.stack(pool_new, axis=0), kv_rows, win_state


def setup_inputs(seed: int = 0) -> dict:
    key = jax.random.key(seed)
    keys = iter(jax.random.split(key, 40))

    def nrm(shape, scale=1.0):
        return jax.random.normal(next(keys), shape, jnp.float32) * scale

    def gain(shape):
        return 1.0 + 0.05 * nrm(shape)

    n_pages = PAST_LEN // PAGE_SIZE
    n_phys = (5 * DEC_BATCH * n_pages + 3) // 4
    wb = min(WINDOW, PAST_LEN)
    kv_page = (n_phys, PAGE_SIZE, N_KV_HEADS, HEAD_DIM)
    perm = jax.random.permutation(next(keys), n_phys)[:DEC_BATCH * n_pages]
    return {
        'x_prompt': nrm((BATCH, SEQ, D_MODEL)),
        'x_sample': nrm((DEC_BATCH, DEC_SEQ, D_MODEL)),
        'state_pool': nrm((N_A_LAYERS, DEC_BATCH, POOL_STATE, D_MODEL)),
        'cache_k_cmp': nrm(kv_page),
        'cache_v_cmp': nrm(kv_page),
        'cache_k_sel': nrm(kv_page),
        'cache_v_sel': nrm(kv_page),
        'state_k_win': nrm((DEC_BATCH, wb, N_KV_HEADS, HEAD_DIM)),
        'state_v_win': nrm((DEC_BATCH, wb, N_KV_HEADS, HEAD_DIM)),
        'page_table': perm.reshape(DEC_BATCH, n_pages).astype(jnp.int32),
        'p_prompt': nrm((DEPTH, BATCH, SEQ, PLE_DIM)),
        'p_sample': nrm((DEPTH, DEC_BATCH, DEC_SEQ, PLE_DIM)),
        'g_mix': gain((DEPTH, D_MODEL)),
        'w_pool': nrm((N_A_LAYERS, len(POOL_WINDOWS), POOL_GROUP, POOL_GROUP), POOL_GROUP ** -0.5),
        'pool_scale': gain((N_A_LAYERS, D_MODEL)),
        'g_kv': gain((D_MODEL,)),
        'w_kv': nrm((D_MODEL, N_KV_TENSORS, N_KV_HEADS, HEAD_DIM), D_MODEL ** -0.5),
        'g_k_cmp': gain((HEAD_DIM,)),
        'g_k_sel': gain((HEAD_DIM,)),
        'g_k_win': gain((HEAD_DIM,)),
        'w_cmp_k1': nrm((CMP_BLOCK * HEAD_DIM, CMP_HIDDEN), (CMP_BLOCK * HEAD_DIM) ** -0.5),
        'w_cmp_k2': nrm((CMP_HIDDEN, HEAD_DIM), CMP_HIDDEN ** -0.5),
        'pe_cmp_k': nrm((CMP_BLOCK, HEAD_DIM), 0.5),
        'w_cmp_v1': nrm((CMP_BLOCK * HEAD_DIM, CMP_HIDDEN), (CMP_BLOCK * HEAD_DIM) ** -0.5),
        'w_cmp_v2': nrm((CMP_HIDDEN, HEAD_DIM), CMP_HIDDEN ** -0.5),
        'pe_cmp_v': nrm((CMP_BLOCK, HEAD_DIM), 0.5),
        'w_qg': nrm((N_B_LAYERS, D_MODEL, N_HEADS * HEAD_DIM + N_BRANCH * N_HEADS), D_MODEL ** -0.5),
        'g_q': gain((N_B_LAYERS, HEAD_DIM)),
        'w_o': nrm((N_B_LAYERS, N_HEADS * HEAD_DIM, D_MODEL), (N_HEADS * HEAD_DIM) ** -0.5),
        'g_ffn': gain((DEPTH, D_MODEL)),
        'w_up': nrm((DEPTH, D_MODEL, D_FF), D_MODEL ** -0.5),
        'w_down': nrm((DEPTH, D_FF, D_MODEL), 0.5 * D_FF ** -0.5),
        'g_ple': gain((DEPTH, D_MODEL)),
        'w_ple': nrm((DEPTH, PLE_DIM, D_MODEL), PLE_DIM ** -0.5),
        'w_ple_gate': nrm((DEPTH, D_MODEL, D_MODEL), D_MODEL ** -0.5),
    }


def reference(x_prompt, x_sample, state_pool, cache_k_cmp, cache_v_cmp, cache_k_sel, cache_v_sel, state_k_win, state_v_win, page_table, p_prompt, p_sample, g_mix, w_pool, pool_scale, g_kv, w_kv, g_k_cmp, g_k_sel, g_k_win, w_cmp_k1, w_cmp_k2, pe_cmp_k, w_cmp_v1, w_cmp_v2, pe_cmp_v, w_qg, g_q, w_o, g_ffn, w_up, w_down, g_ple, w_ple, w_ple_gate):
    W = dict(g_mix=g_mix, w_pool=w_pool, pool_scale=pool_scale, g_kv=g_kv, w_kv=w_kv, g_k_cmp=g_k_cmp,
             g_k_sel=g_k_sel, g_k_win=g_k_win, w_cmp_k1=w_cmp_k1, w_cmp_k2=w_cmp_k2, pe_cmp_k=pe_cmp_k,
             w_cmp_v1=w_cmp_v1, w_cmp_v2=w_cmp_v2, pe_cmp_v=pe_cmp_v, w_qg=w_qg, g_q=g_q, w_o=w_o,
             g_ffn=g_ffn, w_up=w_up, w_down=w_down, g_ple=g_ple, w_ple=w_ple, w_ple_gate=w_ple_gate)
    t_p = x_prompt.shape[1]
    t_s = x_sample.shape[1]
    pos_p = jnp.arange(t_p, dtype=jnp.int32)
    pos_s = PAST_LEN + jnp.arange(t_s, dtype=jnp.int32)
    caches = (cache_k_cmp, cache_v_cmp, cache_k_sel, cache_v_sel)

    def prompt_ctx(rows):
        full = tuple(_pad_time(r, SEL_BLOCK) for r in rows[:4])
        nw = min(WINDOW, t_p)
        return full, (rows[4], rows[5], pos_p), (rows[4][:, -nw:], rows[5][:, -nw:])

    def sample_ctx(rows):
        def past(pool):
            g = pool[page_table]
            return g.reshape((g.shape[0], g.shape[1] * g.shape[2]) + g.shape[3:])
        full = tuple(_pad_time(jnp.concatenate([past(c), r], axis=1), SEL_BLOCK) for c, r in zip(caches, rows[:4]))
        wb = state_k_win.shape[1]
        kw = jnp.concatenate([state_k_win, rows[4]], axis=1)
        vw = jnp.concatenate([state_v_win, rows[5]], axis=1)
        kw_pos = jnp.concatenate([jnp.arange(PAST_LEN - wb, PAST_LEN, dtype=jnp.int32), pos_s])
        return full, (kw, vw, kw_pos), (kw[:, -wb:], vw[:, -wb:])

    pool_zero = jnp.zeros((N_A_LAYERS, x_prompt.shape[0], POOL_STATE, D_MODEL), x_prompt.dtype)
    y_p, pool_p, rows_p, win_p = trunk(x_prompt, p_prompt, pos_p, pool_zero, prompt_ctx, W)
    y_s, pool_s, rows_s, win_s = trunk(x_sample, p_sample, pos_s, state_pool, sample_ctx, W)
    return (y_p, y_s, pool_p, pool_s, rows_p[0], rows_p[1], rows_p[2], rows_p[3], win_p[0], win_p[1], rows_s[0], rows_s[1], rows_s[2], rows_s[3], win_s[0], win_s[1])
```

```python
import functools

import jax
import jax.numpy as jnp
import numpy as np
from jax import lax
from jax.experimental import pallas as pl
from jax.experimental.pallas import tpu as pltpu

F32 = jnp.float32
BF16 = jnp.bfloat16

POOL_WINDOWS = (2, 4, 8, 16)
POOL_STATE = max(POOL_WINDOWS) - 1
POOL_PAD = POOL_STATE + 1
HEAD_DIM = 128
N_KV_HEADS = 4
N_BRANCH = 3
CMP_BLOCK = 32
CMP_STRIDE = 16
CMP_HIDDEN = 2 * HEAD_DIM
SEL_BLOCK = 64
N_SEL = 16
WINDOW = 512
ROPE_THETA = 10000.0
EPS = 1e-6
SCALE = HEAD_DIM ** -0.5
NEG = -1e30
FORCE = 1e9
PAD_SCORE = -3e38
KV_W = N_KV_HEADS * HEAD_DIM
LANE = 128
VMEM_LIMIT = 56 * 1024 * 1024


def _params(sem):
    return pltpu.CompilerParams(dimension_semantics=sem, vmem_limit_bytes=VMEM_LIMIT)


def _sigmoid(x):
    return 1.0 / (1.0 + jnp.exp(-x))


def _dot(a, b):
    return jnp.dot(a, b, preferred_element_type=F32)


def _dot_nt(a, b):
    return lax.dot_general(a, b, (((1,), (1,)), ((), ())), preferred_element_type=F32)


def _dot_tn(a, b):
    return lax.dot_general(a, b, (((0,), (0,)), ((), ())), preferred_element_type=F32)


def _head_norm(x, g):
    return x * lax.rsqrt(jnp.mean(x * x, axis=-1, keepdims=True) + EPS) * g


def _rope(x, cosf, sinf):
    return x * cosf + pltpu.roll(x, HEAD_DIM // 2, 1) * sinf


def _rms_kernel(x_ref, g_ref, *o_refs):
    x = x_ref[...]
    y = x * lax.rsqrt(jnp.mean(x * x, axis=-1, keepdims=True) + EPS)
    for i, o_ref in enumerate(o_refs):
        o_ref[...] = (y * g_ref[i:i + 1, :]).astype(o_ref.dtype)


def rmsnorm_rows(x, gains, dtypes):
    M, D = x.shape
    tm = min(M, 256)
    g = jnp.stack(gains).astype(F32)
    n = len(gains)
    return pl.pallas_call(
        _rms_kernel,
        grid=(M // tm,),
        in_specs=[pl.BlockSpec((tm, D), lambda i: (i, 0)), pl.BlockSpec((n, D), lambda i: (0, 0))],
        out_specs=[pl.BlockSpec((tm, D), lambda i: (i, 0)) for _ in range(n)],
        out_shape=[jax.ShapeDtypeStruct((M, D), dt) for dt in dtypes],
        compiler_params=_params(("parallel",)),
        name="rmsnorm_rows",
    )(x, g)


def _mm_kernel(*refs, nk, n_extra, n_out, epilogue):
    x_ref, w_ref = refs[0], refs[1]
    extras = refs[2:2 + n_extra]
    outs = refs[2 + n_extra:2 + n_extra + n_out]

    def finish(acc):
        for o_ref, r in zip(outs, epilogue(acc, *extras)):
            o_ref[...] = r.astype(o_ref.dtype).reshape(o_ref.shape)

    if nk == 1:
        finish(_dot(x_ref[...], w_ref[...]))
    else:
        acc_ref = refs[-1]
        k = pl.program_id(2)

        @pl.when(k == 0)
        def _():
            acc_ref[...] = jnp.zeros_like(acc_ref)

        acc_ref[...] += _dot(x_ref[...], w_ref[...])

        @pl.when(k == nk - 1)
        def _():
            finish(acc_ref[...])


def mm(x, w, *, grid, tm, tn, tk, epilogue, extras=(), extra_specs=(), out_shapes, out_specs,
       x_map=None, w_map=None, name):
    nk = grid[2]
    x_map = x_map or (lambda i, j, k: (i, k))
    w_map = w_map or (lambda i, j, k: (k, j))
    kern = functools.partial(_mm_kernel, nk=nk, n_extra=len(extras), n_out=len(out_shapes), epilogue=epilogue)
    return pl.pallas_call(
        kern,
        grid=grid,
        in_specs=[pl.BlockSpec((tm, tk), x_map), pl.BlockSpec((tk, tn), w_map)] + list(extra_specs),
        out_specs=list(out_specs),
        out_shape=list(out_shapes),
        scratch_shapes=[pltpu.VMEM((tm, tn), F32)] if nk > 1 else [],
        compiler_params=_params(("parallel", "parallel", "arbitrary")),
        name=name,
    )(x, w, *extras)


def _tiles(M, N, K, tm, tn, tk):
    tm, tn, tk = min(tm, M), min(tn, N), min(tk, K)
    return (M // tm, N // tn, K // tk), tm, tn, tk


def _mn_spec(tm, tn):
    return pl.BlockSpec((tm, tn), lambda i, j, k: (i, j))


def _pool_diff_kernel(a_ref, pre_ref, d_ref, seq_ref, *, T, pos0):
    seq_ref[0:POOL_PAD, :] = pre_ref[0]
    seq_ref[POOL_PAD:POOL_PAD + T, :] = a_ref[0]
    x_t = a_ref[0]
    pos = pos0 + lax.broadcasted_iota(jnp.int32, (T, 1), 0)
    for g, w in enumerate(POOL_WINDOWS):
        @pl.when(pl.program_id(1) == g)
        def _(w=w):
            s = x_t
            for j in range(1, w):
                s = s + seq_ref[POOL_PAD - j:POOL_PAD - j + T, :]
            cnt = jnp.minimum(pos + 1, w).astype(F32)
            d_ref[0] = (s / cnt - x_t).astype(d_ref.dtype)


def pool_diff(a, prefix, pos0):
    B, T, D = a.shape
    pg = D // len(POOL_WINDOWS)
    tc = min(pg, 512)
    cpg = pg // tc
    pre = jnp.concatenate([jnp.zeros((B, 1, D), F32), prefix], axis=1)
    return pl.pallas_call(
        functools.partial(_pool_diff_kernel, T=T, pos0=pos0),
        grid=(B, len(POOL_WINDOWS), cpg),
        in_specs=[pl.BlockSpec((1, T, tc), lambda b, g, c: (b, 0, g * cpg + c)),
                  pl.BlockSpec((1, POOL_PAD, tc), lambda b, g, c: (b, 0, g * cpg + c))],
        out_specs=pl.BlockSpec((1, T, tc), lambda b, g, c: (b, 0, g * cpg + c)),
        out_shape=jax.ShapeDtypeStruct((B, T, D), BF16),
        scratch_shapes=[pltpu.VMEM((POOL_PAD + T, tc), F32)],
        compiler_params=_params(("parallel", "parallel", "parallel")),
        name="pool_diff",
    )(a, pre)


def _cmp_partial_kernel(pt_ref, kpage_ref, vpage_ref, wk_ref, wv_ref, abk_ref, abv_ref, xk_ref, xv_ref, *, cp):
    del pt_ref
    p = pl.program_id(2)
    row0 = pl.multiple_of(p * 8, 8)
    for page_ref, x_ref in ((kpage_ref, xk_ref), (vpage_ref, xv_ref)):
        for g in range(N_KV_HEADS):
            for r in range(CMP_STRIDE):
                x_ref[g, pl.ds(row0, 8), r * HEAD_DIM:(r + 1) * HEAD_DIM] = (
                    page_ref[0, pl.ds(r, 8, stride=CMP_STRIDE), g, :])

    @pl.when(p == cp - 1)
    def _():
        for x_ref, w_ref, ab_ref in ((xk_ref, wk_ref, abk_ref), (xv_ref, wv_ref, abv_ref)):
            for g in range(N_KV_HEADS):
                ab_ref[0, g] = _dot(x_ref[g].astype(BF16), w_ref[...])


def cmp_partials(k_pages, v_pages, table, wk_cat, wv_cat, cp):
    B, ppb = table.shape
    page = k_pages.shape[1]
    spp = page // CMP_STRIDE
    assert spp == 8, "one page must fill one 8-sublane group of sub-blocks"
    nch = ppb // cp
    m = cp * spp
    kdim = CMP_STRIDE * HEAD_DIM
    page_spec = pl.BlockSpec((1, page, N_KV_HEADS, HEAD_DIM), lambda b, c, p, pt: (pt[b, c * cp + p], 0, 0, 0))
    w_spec = pl.BlockSpec((kdim, 2 * CMP_HIDDEN), lambda b, c, p, pt: (0, 0))
    out_spec = pl.BlockSpec((1, N_KV_HEADS, m, 2 * CMP_HIDDEN), lambda b, c, p, pt: (b, 0, c, 0))
    out_shape = jax.ShapeDtypeStruct((B, N_KV_HEADS, ppb * spp, 2 * CMP_HIDDEN), F32)
    return pl.pallas_call(
        functools.partial(_cmp_partial_kernel, cp=cp),
        grid_spec=pltpu.PrefetchScalarGridSpec(
            num_scalar_prefetch=1,
            grid=(B, nch, cp),
            in_specs=[page_spec, page_spec, w_spec, w_spec],
            out_specs=[out_spec, out_spec],
            scratch_shapes=[pltpu.VMEM((N_KV_HEADS, m, kdim), F32), pltpu.VMEM((N_KV_HEADS, m, kdim), F32)],
        ),
        out_shape=[out_shape, out_shape],
        compiler_params=_params(("parallel", "parallel", "arbitrary")),
        name="cmp_partials",
    )(table, k_pages, v_pages, wk_cat, wv_cat)


def _cmp_finish_kernel(abk_ref, abv_ref, pek_ref, pev_ref, wk1_ref, wv1_ref, wk2_ref, wv2_ref, gk_ref,
                       ck_ref, cv_ref):
    n = abk_ref.shape[2]
    for ab_ref, pe_ref, w1_ref, w2_ref, o_ref, is_k in ((abk_ref, pek_ref, wk1_ref, wk2_ref, ck_ref, True),
                                                        (abv_ref, pev_ref, wv1_ref, wv2_ref, cv_ref, False)):
        ab = ab_ref[0, 0]
        bias = _dot(jnp.broadcast_to(pe_ref[...], (8, pe_ref.shape[1])), w1_ref[...])[0:1, :]
        pre = ab[:, :CMP_HIDDEN] + pltpu.roll(ab[:, CMP_HIDDEN:], n - 1, 0) + bias
        hid = pre * _sigmoid(pre)
        out = _dot(hid.astype(BF16), w2_ref[...])
        if is_k:
            out = _head_norm(out, gk_ref[...])
        o_ref[0, 0] = out.astype(o_ref.dtype)


def cmp_finish(abk, abv, pe_k, pe_v, wk1, wv1, wk2, wv2, g_k_cmp):
    B, G, n, _ = abk.shape
    ab_spec = pl.BlockSpec((1, 1, n, 2 * CMP_HIDDEN), lambda b, g: (b, g, 0, 0))
    full = lambda a: pl.BlockSpec(a.shape, lambda b, g: (0,) * a.ndim)
    out_spec = pl.BlockSpec((1, 1, n, HEAD_DIM), lambda b, g: (b, g, 0, 0))
    out_shape = jax.ShapeDtypeStruct((B, G, n, HEAD_DIM), BF16)
    args = (pe_k, pe_v, wk1, wv1, wk2, wv2, g_k_cmp)
    return pl.pallas_call(
        _cmp_finish_kernel,
        grid=(B, G),
        in_specs=[ab_spec, ab_spec] + [full(a) for a in args],
        out_specs=[out_spec, out_spec],
        out_shape=[out_shape, out_shape],
        compiler_params=_params(("parallel", "parallel")),
        name="cmp_finish",
    )(abk, abv, *args)


def _select_blocks(score, blk, ns):
    rank = jnp.zeros(score.shape, jnp.int32)
    for j in range(ns):
        sj = score[j:j + 1, :]
        beats = (sj > score) | ((sj == score) & (j < blk))
        rank = rank + beats.astype(jnp.int32)
    return (rank < min(N_SEL, ns)) & (score > 0.5 * NEG)


def _attn_prompt_kernel(q_ref, qr_ref, ck_ref, cv_ref, ks_ref, vs_ref, kw_ref, vw_ref, gate_ref, mapT_ref,
                        expand_ref, o_ref, m_ref, l_ref, acc_ref, part_ref, mask_ref, *, tq, kc, nc, ns, R):
    qi = pl.program_id(2)
    q0 = qi * tq
    pos = q0 + lax.broadcasted_iota(jnp.int32, (tq, 1), 0)
    ncp = ck_ref.shape[2]

    ck = ck_ref[0, 0]
    cv = cv_ref[0, 0]
    cidx = lax.broadcasted_iota(jnp.int32, (1, ncp), 1)
    ok_c = (cidx * CMP_STRIDE + CMP_BLOCK - 1 <= pos) & (cidx < nc)
    imp = jnp.zeros((tq, ncp), F32)
    for r in range(R):
        qh = q_ref[0, :, r * HEAD_DIM:(r + 1) * HEAD_DIM]
        s = jnp.where(ok_c, _dot_nt(qh, ck) * SCALE, NEG)
        e = jnp.exp(s - jnp.max(s, axis=1, keepdims=True))
        p = jnp.where(ok_c, e * (1.0 / jnp.sum(e, axis=1, keepdims=True)), 0.0)
        imp = imp + p
        part_ref[r] = gate_ref[0, :, r * N_BRANCH:r * N_BRANCH + 1] * _dot(p.astype(BF16), cv)

    nsp = mapT_ref.shape[0]
    p_slc = lax.dot_general(mapT_ref[...], imp, (((1,), (1,)), ((), ())), precision=lax.Precision.HIGHEST,
                            preferred_element_type=F32)
    blk = lax.broadcasted_iota(jnp.int32, (nsp, tq), 0)
    pos_l = q0 + lax.broadcasted_iota(jnp.int32, (nsp, tq), 1)
    cur = pos_l // SEL_BLOCK
    vis = blk * SEL_BLOCK <= pos_l
    forced = vis & ((blk == 0) | (blk == cur) | (blk == cur - 1))
    score = jnp.where(forced, FORCE, jnp.where(vis, p_slc, NEG))
    score = jnp.where(blk < ns, score, PAD_SCORE)
    sel = _select_blocks(score, blk, ns).astype(BF16)
    for c in range(mask_ref.shape[0]):
        mask_ref[c] = _dot_tn(sel, expand_ref[:, c * kc:(c + 1) * kc])

    def flash(k_ref, v_ref, c_lo, ok_fn):
        m_ref[...] = jnp.full(m_ref.shape, NEG, F32)
        l_ref[...] = jnp.zeros(l_ref.shape, F32)
        acc_ref[...] = jnp.zeros(acc_ref.shape, F32)

        def body(c, carry):
            k0 = pl.multiple_of(c * kc, kc)
            k = k_ref[0, pl.ds(k0, kc), :]
            v = v_ref[0, pl.ds(k0, kc), :]
            kpos = k0 + lax.broadcasted_iota(jnp.int32, (1, kc), 1)
            ok = ok_fn(c, kpos)
            for r in range(R):
                qh = qr_ref[0, :, r * HEAD_DIM:(r + 1) * HEAD_DIM]
                s = jnp.where(ok, _dot_nt(qh, k) * SCALE, NEG)
                m_prev = m_ref[r]
                m_new = jnp.maximum(m_prev, jnp.max(s, axis=1, keepdims=True))
                alpha = jnp.exp(m_prev - m_new)
                p = jnp.where(ok, jnp.exp(s - m_new), 0.0)
                l_ref[r] = alpha * l_ref[r] + jnp.sum(p, axis=1, keepdims=True)
                acc_ref[r] = alpha * acc_ref[r] + _dot(p.astype(BF16), v)
                m_ref[r] = m_new
            return carry

        lax.fori_loop(c_lo, qi * (tq // kc) + tq // kc, body, 0)

    flash(ks_ref, vs_ref, 0, lambda c, kpos: (mask_ref[c] > 0.5) & (kpos <= pos))
    for r in range(R):
        gate = gate_ref[0, :, r * N_BRANCH + 1:r * N_BRANCH + 2]
        part_ref[r] = part_ref[r] + gate * (acc_ref[r] * (1.0 / l_ref[r]))

    c_lo = jnp.maximum(q0 // kc - WINDOW // kc, 0)
    flash(kw_ref, vw_ref, c_lo, lambda c, kpos: (kpos <= pos) & (kpos > pos - WINDOW))
    for r in range(R):
        gate = gate_ref[0, :, r * N_BRANCH + 2:r * N_BRANCH + 3]
        o = part_ref[r] + gate * (acc_ref[r] * (1.0 / l_ref[r]))
        o_ref[0, :, r * HEAD_DIM:(r + 1) * HEAD_DIM] = o.astype(o_ref.dtype)


def _overlap_map(ncp, nsp, ns):
    ratio = CMP_BLOCK // CMP_STRIDE
    per_sel = SEL_BLOCK // CMP_STRIDE
    m = np.zeros((ncp, nsp), np.float32)
    for b in range(ns):
        for mm_ in range(per_sel):
            for n in range(ratio):
                j = per_sel * b + mm_ - n
                if 0 <= j < ncp:
                    m[j, b] += 1.0
    return m


def attn_prompt(q, qr, ck, cv, ks, vs, kw, vw, gates, *, nc):
    B, T, HD = q.shape
    G = N_KV_HEADS
    R = HD // HEAD_DIM // G
    tq = min(T, 256)
    kc = tq
    ns = T // SEL_BLOCK
    nsp = -(-ns // 8) * 8
    ncp = ck.shape[2]
    mapT = jnp.asarray(_overlap_map(ncp, nsp, ns).T)
    expand = jnp.asarray((np.arange(T)[None, :] // SEL_BLOCK == np.arange(nsp)[:, None]).astype(np.float32), BF16)
    q_spec = pl.BlockSpec((1, tq, R * HEAD_DIM), lambda b, g, i: (b, i, g))
    c_spec = pl.BlockSpec((1, 1, ncp, HEAD_DIM), lambda b, g, i: (b, g, 0, 0))
    kv_spec = pl.BlockSpec((1, T, HEAD_DIM), lambda b, g, i: (b, 0, g))
    return pl.pallas_call(
        functools.partial(_attn_prompt_kernel, tq=tq, kc=kc, nc=nc, ns=ns, R=R),
        grid=(B, G, T // tq),
        in_specs=[q_spec, q_spec, c_spec, c_spec, kv_spec, kv_spec, kv_spec, kv_spec,
                  pl.BlockSpec((1, tq, LANE), lambda b, g, i: (b, i, g)),
                  pl.BlockSpec((nsp, ncp), lambda b, g, i: (0, 0)),
                  pl.BlockSpec((nsp, T), lambda b, g, i: (0, 0))],
        out_specs=q_spec,
        out_shape=jax.ShapeDtypeStruct((B, T, HD), BF16),
        scratch_shapes=[pltpu.VMEM((R, tq, 1), F32), pltpu.VMEM((R, tq, 1), F32),
                        pltpu.VMEM((R, tq, HEAD_DIM), F32), pltpu.VMEM((R, tq, HEAD_DIM), F32),
                        pltpu.VMEM((T // kc, tq, kc), F32)],
        compiler_params=_params(("parallel", "parallel", "arbitrary")),
        name="attn_prompt",
    )(q, qr, ck, cv, ks, vs, kw, vw, gates, mapT, expand)


def _attn_dec_dense_kernel(q_ref, qr_ref, ck_ref, cv_ref, kw_ref, vw_ref, map_ref, ocmp_ref, owin_ref, ids_ref,
                           *, nc, ns, pos):
    G, R = q_ref.shape[1], q_ref.shape[2]
    ncp = ck_ref.shape[2]
    nsl = map_ref.shape[1]
    cidx = lax.broadcasted_iota(jnp.int32, (1, ncp), 1)
    ok_c = (cidx * CMP_STRIDE + CMP_BLOCK - 1 <= pos) & (cidx < nc)
    blk_l = lax.broadcasted_iota(jnp.int32, (1, nsl), 1)
    cur = pos // SEL_BLOCK
    vis = blk_l * SEL_BLOCK <= pos
    forced = vis & ((blk_l == 0) | (blk_l == cur) | (blk_l == cur - 1))
    ii = lax.broadcasted_iota(jnp.int32, (nsl, nsl), 0)
    jj = lax.broadcasted_iota(jnp.int32, (nsl, nsl), 1)
    slot = lax.broadcasted_iota(jnp.int32, (nsl, LANE), 1).astype(F32)
    blk_s = lax.broadcasted_iota(jnp.int32, (nsl, LANE), 0).astype(F32)
    for g in range(G):
        s = jnp.where(ok_c, _dot_nt(q_ref[0, g], ck_ref[0, g]) * SCALE, NEG)
        e = jnp.exp(s - jnp.max(s, axis=1, keepdims=True))
        p = jnp.where(ok_c, e * (1.0 / jnp.sum(e, axis=1, keepdims=True)), 0.0)
        ocmp_ref[0, g] = _dot(p.astype(BF16), cv_ref[0, g])
        imp = jnp.broadcast_to(jnp.sum(p, axis=0, keepdims=True), (R, ncp))
        p_slc = jnp.dot(imp, map_ref[...], precision=lax.Precision.HIGHEST, preferred_element_type=F32)[0:1, :]
        score_l = jnp.where(forced, FORCE, jnp.where(vis, p_slc, NEG))
        score_l = jnp.where(blk_l < ns, score_l, PAD_SCORE)
        score_s = jnp.sum(jnp.where(ii == jj, score_l, 0.0), axis=1, keepdims=True)
        beats = (score_l > score_s) | ((score_l == score_s) & (jj < ii))
        rank = jnp.sum(beats.astype(F32), axis=1, keepdims=True)
        ids = jnp.sum(jnp.where(rank == slot, blk_s, 0.0), axis=0, keepdims=True)
        ids_ref[0, g] = ids[:, :N_SEL].astype(jnp.int32)
        kw = kw_ref[0, :, g, :].astype(BF16)
        vw = vw_ref[0, :, g, :].astype(BF16)
        s = _dot_nt(qr_ref[0, g], kw) * SCALE
        e = jnp.exp(s - jnp.max(s, axis=1, keepdims=True))
        p = e * (1.0 / jnp.sum(e, axis=1, keepdims=True))
        owin_ref[0, g] = _dot(p.astype(BF16), vw)


def attn_dec_dense(q, qr, ck, cv, kw, vw, *, nc, ns, pos):
    B, G, R, _ = q.shape
    ncp = ck.shape[2]
    wb = kw.shape[1]
    nsl = -(-ns // LANE) * LANE
    omap = jnp.asarray(_overlap_map(ncp, nsl, ns))
    q_spec = pl.BlockSpec((1, G, R, HEAD_DIM), lambda b: (b, 0, 0, 0))
    c_spec = pl.BlockSpec((1, G, ncp, HEAD_DIM), lambda b: (b, 0, 0, 0))
    w_spec = pl.BlockSpec((1, wb, G, HEAD_DIM), lambda b: (b, 0, 0, 0))
    return pl.pallas_call(
        functools.partial(_attn_dec_dense_kernel, nc=nc, ns=ns, pos=pos),
        grid=(B,),
        in_specs=[q_spec, q_spec, c_spec, c_spec, w_spec, w_spec, pl.BlockSpec((ncp, nsl), lambda b: (0, 0))],
        out_specs=[q_spec, q_spec, pl.BlockSpec((1, G, 1, N_SEL), lambda b: (b, 0, 0, 0))],
        out_shape=[jax.ShapeDtypeStruct((B, G, R, HEAD_DIM), F32), jax.ShapeDtypeStruct((B, G, R, HEAD_DIM), F32),
                   jax.ShapeDtypeStruct((B, G, 1, N_SEL), jnp.int32)],
        compiler_params=_params(("parallel",)),
        name="attn_dec_dense",
    )(q, qr, ck, cv, kw, vw, omap)


def _attn_dec_sel_kernel(pt_ref, ids_ref, qr_ref, *refs, n_past, pos):
    del pt_ref
    G = qr_ref.shape[1]
    kc_refs, vc_refs = refs[:G], refs[G:2 * G]
    kn_ref, vn_ref, ocmp_ref, owin_ref, gate_ref, o_ref, m_ref, l_ref, acc_ref = refs[2 * G:]
    b, n = pl.program_id(0), pl.program_id(1)

    @pl.when(n == 0)
    def _():
        m_ref[...] = jnp.full(m_ref.shape, NEG, F32)
        l_ref[...] = jnp.zeros(l_ref.shape, F32)
        acc_ref[...] = jnp.zeros(acc_ref.shape, F32)

    row = lax.broadcasted_iota(jnp.int32, (SEL_BLOCK, 1), 0)
    for g in range(G):
        bid = ids_ref[(b * G + g) * N_SEL + n]
        is_new = bid >= n_past
        first = (row == 0) & (bid == n_past)
        sl = slice(g * HEAD_DIM, (g + 1) * HEAD_DIM)
        k = jnp.where(is_new, jnp.where(first, kn_ref[0, :, sl], 0.0), kc_refs[g][0, :, g, :]).astype(BF16)
        v = jnp.where(is_new, jnp.where(first, vn_ref[0, :, sl], 0.0), vc_refs[g][0, :, g, :]).astype(BF16)
        kpos = bid * SEL_BLOCK + lax.broadcasted_iota(jnp.int32, (1, SEL_BLOCK), 1)
        ok = kpos <= pos
        s = jnp.where(ok, _dot_nt(qr_ref[0, g], k) * SCALE, NEG)
        m_prev = m_ref[g]
        m_new = jnp.maximum(m_prev, jnp.max(s, axis=1, keepdims=True))
        alpha = jnp.exp(m_prev - m_new)
        p = jnp.where(ok, jnp.exp(s - m_new), 0.0)
        l_ref[g] = alpha * l_ref[g] + jnp.sum(p, axis=1, keepdims=True)
        acc_ref[g] = alpha * acc_ref[g] + _dot(p.astype(BF16), v)
        m_ref[g] = m_new

    @pl.when(n == N_SEL - 1)
    def _():
        for g in range(G):
            gt = gate_ref[0, g]
            o_sel = acc_ref[g] * (1.0 / l_ref[g])
            o_ref[0, g] = gt[:, 0:1] * ocmp_ref[0, g] + gt[:, 1:2] * o_sel + gt[:, 2:3] * owin_ref[0, g]


def attn_dec_sel(table, ids, qr, k_cache, v_cache, k_new, v_new, ocmp, owin, gates, *, pos):
    B, G, R, _ = qr.shape
    page = k_cache.shape[1]
    bpp = page // SEL_BLOCK
    n_past = table.shape[1] * bpp
    kc = k_cache.reshape(k_cache.shape[0] * bpp, SEL_BLOCK, G, HEAD_DIM)
    vc = v_cache.reshape(v_cache.shape[0] * bpp, SEL_BLOCK, G, HEAD_DIM)

    def cache_spec(g):
        def index(b, n, pt, ids_):
            bid = jnp.minimum(ids_[(b * G + g) * N_SEL + n], n_past - 1)
            return (pt[b, bid // bpp] * bpp + bid % bpp, 0, 0, 0)
        return pl.BlockSpec((1, SEL_BLOCK, G, HEAD_DIM), index)

    q_spec = pl.BlockSpec((1, G, R, HEAD_DIM), lambda b, n, pt, ids_: (b, 0, 0, 0))
    n_spec = pl.BlockSpec((1, 1, G * HEAD_DIM), lambda b, n, pt, ids_: (b, 0, 0))
    g_spec = pl.BlockSpec((1, G, R, LANE), lambda b, n, pt, ids_: (b, 0, 0, 0))
    c_specs = [cache_spec(g) for g in range(G)]
    return pl.pallas_call(
        functools.partial(_attn_dec_sel_kernel, n_past=n_past, pos=pos),
        grid_spec=pltpu.PrefetchScalarGridSpec(
            num_scalar_prefetch=2,
            grid=(B, N_SEL),
            in_specs=[q_spec] + c_specs + c_specs + [n_spec, n_spec, q_spec, q_spec, g_spec],
            out_specs=q_spec,
            scratch_shapes=[pltpu.VMEM((G, R, 1), F32), pltpu.VMEM((G, R, 1), F32),
                            pltpu.VMEM((G, R, HEAD_DIM), F32)],
        ),
        out_shape=jax.ShapeDtypeStruct((B, G, R, HEAD_DIM), F32),
        compiler_params=_params(("parallel", "arbitrary")),
        name="attn_dec_sel",
    )(table, ids.reshape(-1), qr, *([kc] * G), *([vc] * G), k_new, v_new, ocmp, owin, gates)


def _rope_tables(pos):
    half = HEAD_DIM // 2
    inv = ROPE_THETA ** (-jnp.arange(half, dtype=F32) / half)
    ang = pos.astype(F32)[:, None] * inv[None, :]
    cos, sin = jnp.cos(ang), jnp.sin(ang)
    return jnp.concatenate([cos, cos], axis=1), jnp.concatenate([-sin, sin], axis=1)


def _prep_weights(W):
    D = W['w_kv'].shape[0]
    H = D // HEAD_DIM
    R = H // N_KV_HEADS
    pg = D // len(POOL_WINDOWS)
    half = CMP_STRIDE * HEAD_DIM
    P = {}
    P['w_pool'] = W['w_pool'].astype(BF16).reshape(-1, len(POOL_WINDOWS) * pg, pg)
    P['w_kv'] = W['w_kv'].astype(BF16).reshape(D, -1)
    kv_gain = jnp.ones((W['w_kv'].shape[1], KV_W), F32)
    kv_gain = kv_gain.at[2].set(jnp.tile(W['g_k_sel'], N_KV_HEADS)).at[4].set(jnp.tile(W['g_k_win'], N_KV_HEADS))
    P['kv_gain'] = kv_gain
    for t in ('k', 'v'):
        w1 = W['w_cmp_%s1' % t].astype(BF16)
        P['w_cmp_%s1' % t] = w1
        P['w_cmp_%scat' % t] = jnp.concatenate([w1[:half], w1[half:]], axis=1)
        P['w_cmp_%s2' % t] = W['w_cmp_%s2' % t].astype(BF16)
        P['pe_%s' % t] = W['pe_cmp_%s' % t].astype(BF16).reshape(1, -1)
    w_qg = W['w_qg'].astype(BF16)
    P['w_q'] = w_qg[:, :, :H * HEAD_DIM]
    wg = w_qg[:, :, H * HEAD_DIM:].reshape(w_qg.shape[0], D, N_KV_HEADS, R * N_BRANCH)
    wg = jnp.pad(wg, ((0, 0), (0, 0), (0, 0), (0, LANE - R * N_BRANCH)))
    P['w_gate'] = wg.reshape(w_qg.shape[0], D, N_KV_HEADS * LANE)
    for name in ('w_o', 'w_up', 'w_down', 'w_ple', 'w_ple_gate'):
        P[name] = W[name].astype(BF16)
    return P


def _ffn_ple(h, p_l, layer, W, P, tmx):
    M, D = h.shape
    F = P['w_up'].shape[2]
    (m_,) = rmsnorm_rows(h, [W['g_ffn'][layer]], [BF16])
    grid, tm, tn, tk = _tiles(M, F, D, tmx, 1024, D)
    (u,) = mm(m_, P['w_up'][layer], grid=grid, tm=tm, tn=tn, tk=tk,
              epilogue=lambda acc: (jnp.square(jnp.maximum(acc, 0.0)),),
              out_shapes=[jax.ShapeDtypeStruct((M, F), BF16)], out_specs=[_mn_spec(tm, tn)], name="ffn_up")
    grid, tm, tn, tk = _tiles(M, D, F, tmx, 1024, 2048 if M > 8 else 4096)
    (h,) = mm(u, P['w_down'][layer], grid=grid, tm=tm, tn=tn, tk=tk,
              epilogue=lambda acc, r: (r[...] + acc,), extras=[h], extra_specs=[_mn_spec(tm, tn)],
              out_shapes=[jax.ShapeDtypeStruct((M, D), F32)], out_specs=[_mn_spec(tm, tn)], name="ffn_down")
    (e_,) = rmsnorm_rows(h, [W['g_ple'][layer]], [BF16])
    ple_dim = p_l.shape[1]
    grid, tm, tn, tk = _tiles(M, D, D, tmx, 512, D)
    (h,) = mm(e_, P['w_ple_gate'][layer], grid=grid, tm=tm, tn=tn, tk=tk,
              epilogue=lambda acc, r, pp, wp: (r[...] + _dot(pp[...], wp[...]) * _sigmoid(acc),),
              extras=[h, p_l.astype(BF16), P['w_ple'][layer]],
              extra_specs=[_mn_spec(tm, tn), pl.BlockSpec((tm, ple_dim), lambda i, j, k: (i, 0)),
                           pl.BlockSpec((ple_dim, tn), lambda i, j, k: (0, j))],
              out_shapes=[jax.ShapeDtypeStruct((M, D), F32)], out_specs=[_mn_spec(tm, tn)], name="ple")
    return h


def _kv_epilogue(acc, gain_ref, cos_ref, sin_ref):
    j = pl.program_id(1)
    cosf, sinf = cos_ref[...], sin_ref[...]
    heads = []
    for hh in range(N_KV_HEADS):
        sl = slice(hh * HEAD_DIM, (hh + 1) * HEAD_DIM)
        heads.append(_rope(_head_norm(acc[:, sl], gain_ref[0, :, sl]), cosf, sinf))
    out = jnp.where((j == 2) | (j == 4), jnp.concatenate(heads, axis=1), acc)
    return out, out


def _q_epilogue(acc, gq_ref, cos_ref, sin_ref):
    cosf, sinf = cos_ref[...], sin_ref[...]
    qs, qrs = [], []
    for hh in range(acc.shape[1] // HEAD_DIM):
        qn = _head_norm(acc[:, hh * HEAD_DIM:(hh + 1) * HEAD_DIM], gq_ref[...])
        qs.append(qn)
        qrs.append(_rope(qn, cosf, sinf))
    return jnp.concatenate(qs, axis=1), jnp.concatenate(qrs, axis=1)


def _trunk(x, p, pool_prefix, pos0, W, P, attend):
    B, T, D = x.shape
    M = B * T
    tmx = 1024 if M >= 1024 else M
    h = x.reshape(M, D)
    pg = D // len(POOL_WINDOWS)

    (a0,) = rmsnorm_rows(h, [W['g_mix'][0]], [F32])
    a0 = a0.reshape(B, T, D)
    d = pool_diff(a0, pool_prefix[0], pos0).reshape(M, D)
    pool_new = jnp.concatenate([pool_prefix[0], a0], axis=1)[:, -POOL_STATE:][None]
    grid, tm, tn, tk = _tiles(M, D, pg, tmx, pg, pg)
    (h,) = mm(d, P['w_pool'][0], grid=grid, tm=tm, tn=tn, tk=tk,
              x_map=lambda i, j, k: (i, j), w_map=lambda i, j, k: (j, 0),
              epilogue=lambda acc, sc, r: (r[...] + acc * sc[...],),
              extras=[W['pool_scale'][0].reshape(1, D), h],
              extra_specs=[pl.BlockSpec((1, tn), lambda i, j, k: (0, j)), _mn_spec(tm, tn)],
              out_shapes=[jax.ShapeDtypeStruct((M, D), F32)], out_specs=[_mn_spec(tm, tn)], name="pool_mix")
    h = _ffn_ple(h, p[0].reshape(M, -1), 0, W, P, tmx)

    hkv, a1 = rmsnorm_rows(h, [W['g_kv'], W['g_mix'][1]], [BF16, BF16])
    pos = pos0 + jnp.tile(jnp.arange(T, dtype=jnp.int32), B)
    cosf, sinf = _rope_tables(pos)
    n_kv = P['w_kv'].shape[1] // KV_W
    grid, tm, tn, tk = _tiles(M, n_kv * KV_W, D, tmx, KV_W, D)
    rope_spec = pl.BlockSpec((tm, HEAD_DIM), lambda i, j, k: (i, 0))
    kv_spec = pl.BlockSpec((1, tm, KV_W), lambda i, j, k: (j, i, 0))
    kv, kv_b = mm(hkv, P['w_kv'], grid=grid, tm=tm, tn=tn, tk=tk, epilogue=_kv_epilogue,
                  extras=[P['kv_gain'].reshape(n_kv, 1, KV_W), cosf, sinf],
                  extra_specs=[pl.BlockSpec((1, 1, KV_W), lambda i, j, k: (j, 0, 0)), rope_spec, rope_spec],
                  out_shapes=[jax.ShapeDtypeStruct((n_kv, M, KV_W), F32), jax.ShapeDtypeStruct((n_kv, M, KV_W), BF16)],
                  out_specs=[kv_spec, kv_spec], name="kv_proj")
    grid, tm, tn, tk = _tiles(M, D, D, tmx, 512, D)
    q, qr = mm(a1, P['w_q'][0], grid=grid, tm=tm, tn=tn, tk=tk, epilogue=_q_epilogue,
               extras=[W['g_q'][0].reshape(1, HEAD_DIM), cosf, sinf],
               extra_specs=[pl.BlockSpec((1, HEAD_DIM), lambda i, j, k: (0, 0)), rope_spec, rope_spec],
               out_shapes=[jax.ShapeDtypeStruct((M, D), BF16)] * 2, out_specs=[_mn_spec(tm, tn)] * 2, name="q_proj")
    ng = N_KV_HEADS * LANE
    grid, tm, tn, tk = _tiles(M, ng, D, tmx, ng, D)
    (gates,) = mm(a1, P['w_gate'][0], grid=grid, tm=tm, tn=tn, tk=tk, epilogue=lambda acc: (_sigmoid(acc),),
                  out_shapes=[jax.ShapeDtypeStruct((M, ng), F32)], out_specs=[_mn_spec(tm, tn)], name="gate_proj")

    o, win_state = attend(kv, kv_b, q, qr, gates)

    grid, tm, tn, tk = _tiles(M, D, D, tmx, 512, D)
    (h,) = mm(o, P['w_o'][0], grid=grid, tm=tm, tn=tn, tk=tk,
              epilogue=lambda acc, r: (r[...] + acc,), extras=[h], extra_specs=[_mn_spec(tm, tn)],
              out_shapes=[jax.ShapeDtypeStruct((M, D), F32)], out_specs=[_mn_spec(tm, tn)], name="attn_out")
    h = _ffn_ple(h, p[1].reshape(M, -1), 1, W, P, tmx)
    rows = tuple(kv[n].reshape(B, T, N_KV_HEADS, HEAD_DIM) for n in range(4))
    return h.reshape(B, T, D), pool_new, rows, win_state


def _compress(k_pages, v_pages, table, cp, W, P):
    abk, abv = cmp_partials(k_pages, v_pages, table, P['w_cmp_kcat'], P['w_cmp_vcat'], cp)
    return cmp_finish(abk, abv, P['pe_k'], P['pe_v'], P['w_cmp_k1'], P['w_cmp_v1'], P['w_cmp_k2'], P['w_cmp_v2'],
                      W['g_k_cmp'].reshape(1, HEAD_DIM))


def kernel(x_prompt, x_sample, state_pool, cache_k_cmp, cache_v_cmp, cache_k_sel, cache_v_sel, state_k_win, state_v_win, page_table, p_prompt, p_sample, g_mix, w_pool, pool_scale, g_kv, w_kv, g_k_cmp, g_k_sel, g_k_win, w_cmp_k1, w_cmp_k2, pe_cmp_k, w_cmp_v1, w_cmp_v2, pe_cmp_v, w_qg, g_q, w_o, g_ffn, w_up, w_down, g_ple, w_ple, w_ple_gate):
    W = dict(g_mix=g_mix, w_pool=w_pool, pool_scale=pool_scale, g_kv=g_kv, w_kv=w_kv, g_k_cmp=g_k_cmp,
             g_k_sel=g_k_sel, g_k_win=g_k_win, w_cmp_k1=w_cmp_k1, w_cmp_k2=w_cmp_k2, pe_cmp_k=pe_cmp_k,
             w_cmp_v1=w_cmp_v1, w_cmp_v2=w_cmp_v2, pe_cmp_v=pe_cmp_v, w_qg=w_qg, g_q=g_q, w_o=w_o,
             g_ffn=g_ffn, w_up=w_up, w_down=w_down, g_ple=g_ple, w_ple=w_ple, w_ple_gate=w_ple_gate)
    P = _prep_weights(W)
    Bp, Tp, D = x_prompt.shape
    Bs, Ts, _ = x_sample.shape
    assert Ts == 1, "the decode path handles one new token per sequence"
    page = cache_k_cmp.shape[1]
    past_len = page_table.shape[1] * page
    R = D // HEAD_DIM // N_KV_HEADS
    assert Tp % page == 0 and past_len % SEL_BLOCK == 0

    def attend_prompt(kv, kv_b, q, qr, gates):
        ppb = Tp // page
        table = jnp.arange(Bp * ppb, dtype=jnp.int32).reshape(Bp, ppb)
        pages = lambda a: a.reshape(-1, page, N_KV_HEADS, HEAD_DIM)
        ck, cv = _compress(pages(kv[0]), pages(kv[1]), table, ppb, W, P)
        nc = Tp // CMP_STRIDE - CMP_BLOCK // CMP_STRIDE + 1
        seq = lambda a: a.reshape(Bp, Tp, -1)
        o = attn_prompt(seq(q), seq(qr), ck, cv, seq(kv_b[2]), seq(kv_b[3]), seq(kv_b[4]), seq(kv_b[5]),
                        seq(gates), nc=nc)
        nw = min(WINDOW, Tp)
        win = tuple(kv[n].reshape(Bp, Tp, N_KV_HEADS, HEAD_DIM)[:, -nw:] for n in (4, 5))
        return o.reshape(Bp * Tp, D), win

    def attend_sample(kv, kv_b, q, qr, gates):
        del kv_b
        ck, cv = _compress(cache_k_cmp, cache_v_cmp, page_table, min(32, page_table.shape[1]), W, P)
        nc = (past_len - (CMP_BLOCK - 1)) // CMP_STRIDE + 1
        ns = past_len // SEL_BLOCK + 1
        wb = state_k_win.shape[1]
        new_row = lambda a: a.reshape(Bs, 1, N_KV_HEADS, HEAD_DIM)
        kw = jnp.concatenate([state_k_win, new_row(kv[4])], axis=1)[:, -wb:]
        vw = jnp.concatenate([state_v_win, new_row(kv[5])], axis=1)[:, -wb:]
        heads = lambda a: a.reshape(Bs, N_KV_HEADS, R, HEAD_DIM)
        ocmp, owin, ids = attn_dec_dense(heads(q), heads(qr), ck, cv, kw, vw, nc=nc, ns=ns, pos=past_len)
        gt = gates.reshape(Bs, N_KV_HEADS, LANE)[:, :, :R * N_BRANCH].reshape(Bs, N_KV_HEADS, R, N_BRANCH)
        gt = jnp.pad(gt, ((0, 0), (0, 0), (0, 0), (0, LANE - N_BRANCH)))
        o = attn_dec_sel(page_table, ids, heads(qr), cache_k_sel, cache_v_sel,
                         kv[2][:, None], kv[3][:, None], ocmp, owin, gt, pos=past_len)
        return o.reshape(Bs, D).astype(BF16), (kw, vw)

    pool_zero = jnp.zeros((state_pool.shape[0], Bp, POOL_STATE, D), x_prompt.dtype)
    y_p, pool_p, rows_p, win_p = _trunk(x_prompt, p_prompt, pool_zero, 0, W, P, attend_prompt)
    y_s, pool_s, rows_s, win_s = _trunk(x_sample, p_sample, state_pool, past_len, W, P, attend_sample)
    return (y_p, y_s, pool_p, pool_s, rows_p[0], rows_p[1], rows_p[2], rows_p[3], win_p[0], win_p[1],
            rows_s[0], rows_s[1], rows_s[2], rows_s[3], win_s[0], win_s[1])
```

```python
import functools

import jax
import jax.numpy as jnp
import numpy as np
from jax import lax
from jax.experimental import pallas as pl
from jax.experimental.pallas import tpu as pltpu

F32 = jnp.float32
BF16 = jnp.bfloat16

POOL_WINDOWS = (2, 4, 8, 16)
POOL_STATE = max(POOL_WINDOWS) - 1
POOL_PAD = POOL_STATE + 1
HEAD_DIM = 128
N_KV_HEADS = 4
N_BRANCH = 3
CMP_BLOCK = 32
CMP_STRIDE = 16
CMP_HIDDEN = 2 * HEAD_DIM
SEL_BLOCK = 64
N_SEL = 16
WINDOW = 512
ROPE_THETA = 10000.0
EPS = 1e-6
SCALE = HEAD_DIM ** -0.5
NEG = -1e30
FORCE = 1e9
PAD_SCORE = -3e38
KV_W = N_KV_HEADS * HEAD_DIM
LANE = 128
VMEM_LIMIT = 56 * 1024 * 1024


def _params(sem):
    return pltpu.CompilerParams(dimension_semantics=sem, vmem_limit_bytes=VMEM_LIMIT)


def _sigmoid(x):
    return 1.0 / (1.0 + jnp.exp(-x))


def _dot(a, b):
    return jnp.dot(a, b, preferred_element_type=F32)


def _dot_nt(a, b):
    return lax.dot_general(a, b, (((1,), (1,)), ((), ())), preferred_element_type=F32)


def _dot_tn(a, b):
    return lax.dot_general(a, b, (((0,), (0,)), ((), ())), preferred_element_type=F32)


def _head_norm(x, g):
    return x * lax.rsqrt(jnp.mean(x * x, axis=-1, keepdims=True) + EPS) * g


def _rope(x, cosf, sinf):
    return x * cosf + pltpu.roll(x, HEAD_DIM // 2, 1) * sinf


def _rms_kernel(x_ref, g_ref, *o_refs):
    x = x_ref[...]
    y = x * lax.rsqrt(jnp.mean(x * x, axis=-1, keepdims=True) + EPS)
    for i, o_ref in enumerate(o_refs):
        o_ref[...] = (y * g_ref[i:i + 1, :]).astype(o_ref.dtype)


def rmsnorm_rows(x, gains, dtypes):
    M, D = x.shape
    tm = min(M, 256)
    g = jnp.stack(gains).astype(F32)
    n = len(gains)
    return pl.pallas_call(
        _rms_kernel,
        grid=(M // tm,),
        in_specs=[pl.BlockSpec((tm, D), lambda i: (i, 0)), pl.BlockSpec((n, D), lambda i: (0, 0))],
        out_specs=[pl.BlockSpec((tm, D), lambda i: (i, 0)) for _ in range(n)],
        out_shape=[jax.ShapeDtypeStruct((M, D), dt) for dt in dtypes],
        compiler_params=_params(("parallel",)),
        name="rmsnorm_rows",
    )(x, g)


def _mm_kernel(*refs, nk, n_extra, n_out, epilogue):
    x_ref, w_ref = refs[0], refs[1]
    extras = refs[2:2 + n_extra]
    outs = refs[2 + n_extra:2 + n_extra + n_out]

    def finish(acc):
        for o_ref, r in zip(outs, epilogue(acc, *extras)):
            o_ref[...] = r.astype(o_ref.dtype).reshape(o_ref.shape)

    if nk == 1:
        finish(_dot(x_ref[...], w_ref[...]))
    else:
        acc_ref = refs[-1]
        k = pl.program_id(2)

        @pl.when(k == 0)
        def _():
            acc_ref[...] = jnp.zeros_like(acc_ref)

        acc_ref[...] += _dot(x_ref[...], w_ref[...])

        @pl.when(k == nk - 1)
        def _():
            finish(acc_ref[...])


def mm(x, w, *, grid, tm, tn, tk, epilogue, extras=(), extra_specs=(), out_shapes, out_specs,
       x_map=None, w_map=None, name):
    nk = grid[2]
    x_map = x_map or (lambda i, j, k: (i, k))
    w_map = w_map or (lambda i, j, k: (k, j))
    kern = functools.partial(_mm_kernel, nk=nk, n_extra=len(extras), n_out=len(out_shapes), epilogue=epilogue)
    return pl.pallas_call(
        kern,
        grid=grid,
        in_specs=[pl.BlockSpec((tm, tk), x_map), pl.BlockSpec((tk, tn), w_map)] + list(extra_specs),
        out_specs=list(out_specs),
        out_shape=list(out_shapes),
        scratch_shapes=[pltpu.VMEM((tm, tn), F32)] if nk > 1 else [],
        compiler_params=_params(("parallel", "parallel", "arbitrary")),
        name=name,
    )(x, w, *extras)


def _mm_ws_kernel(*refs, nk, n_extra, n_out, epilogue, emit):
    x_ref, w_ref = refs[0], refs[1]
    extras = refs[2:2 + n_extra]
    outs = refs[2 + n_extra:2 + n_extra + n_out]
    rest = refs[2 + n_extra + n_out:]
    wb_ref = rest[1 if emit else 0]
    i, k = pl.program_id(1), pl.program_id(2)

    @pl.when(i == 0)
    def _():
        wb = w_ref[...].astype(BF16)
        wb_ref[k] = wb
        if emit:
            rest[0][...] = wb

    def finish(acc):
        for o_ref, r in zip(outs, epilogue(acc, *extras)):
            o_ref[...] = r.astype(o_ref.dtype).reshape(o_ref.shape)

    if nk == 1:
        finish(_dot(x_ref[...], wb_ref[0]))
    else:
        acc_ref = rest[-1]

        @pl.when(k == 0)
        def _():
            acc_ref[...] = jnp.zeros_like(acc_ref)

        acc_ref[...] += _dot(x_ref[...], wb_ref[k])

        @pl.when(k == nk - 1)
        def _():
            finish(acc_ref[...])


def mm_ws(x, w, *, layer=None, emit=False, grid, tm, tn, tk, epilogue, extras=(), extra_specs=(), out_shapes,
          out_specs, name):
    gm, gn, nk = grid
    swap = lambda f: (lambda j, i, k: f(i, j, k))
    respec = lambda s: pl.BlockSpec(s.block_shape, swap(s.index_map))
    k_once = lambda i, k: jnp.where(i == 0, k, nk - 1)
    if layer is None:
        w_spec = pl.BlockSpec((tk, tn), lambda j, i, k: (k_once(i, k), j))
    else:
        w_spec = pl.BlockSpec((None, tk, tn), lambda j, i, k: (layer, k_once(i, k), j))
    out_shapes, out_specs = list(out_shapes), [respec(s) for s in out_specs]
    if emit:
        out_shapes.append(jax.ShapeDtypeStruct((nk * tk, gn * tn), BF16))
        out_specs.append(pl.BlockSpec((tk, tn), lambda j, i, k: (k_once(i, k), j)))
    kern = functools.partial(_mm_ws_kernel, nk=nk, n_extra=len(extras), n_out=len(out_shapes) - emit,
                             epilogue=epilogue, emit=emit)
    return pl.pallas_call(
        kern,
        grid=(gn, gm, nk),
        in_specs=[pl.BlockSpec((tm, tk), lambda j, i, k: (i, k)), w_spec] + [respec(s) for s in extra_specs],
        out_specs=out_specs,
        out_shape=out_shapes,
        scratch_shapes=[pltpu.VMEM((nk, tk, tn), BF16)] + ([pltpu.VMEM((tm, tn), F32)] if nk > 1 else []),
        compiler_params=_params(("parallel", "arbitrary", "arbitrary")),
        name=name,
    )(x, w, *extras)


def _tiles(M, N, K, tm, tn, tk):
    tm, tn, tk = min(tm, M), min(tn, N), min(tk, K)
    return (M // tm, N // tn, K // tk), tm, tn, tk


def _mn_spec(tm, tn):
    return pl.BlockSpec((tm, tn), lambda i, j, k: (i, j))


def _pool_diff_kernel(a_ref, pre_ref, d_ref, seq_ref, *, T, pos0):
    seq_ref[0:POOL_PAD, :] = pre_ref[0]
    seq_ref[POOL_PAD:POOL_PAD + T, :] = a_ref[0]
    x_t = a_ref[0]
    pos = pos0 + lax.broadcasted_iota(jnp.int32, (T, 1), 0)
    for g, w in enumerate(POOL_WINDOWS):
        @pl.when(pl.program_id(1) == g)
        def _(w=w):
            s = x_t
            for j in range(1, w):
                s = s + seq_ref[POOL_PAD - j:POOL_PAD - j + T, :]
            cnt = jnp.minimum(pos + 1, w).astype(F32)
            d_ref[0] = (s / cnt - x_t).astype(d_ref.dtype)


def pool_diff(a, prefix, pos0):
    B, T, D = a.shape
    pg = D // len(POOL_WINDOWS)
    tc = min(pg, 512)
    cpg = pg // tc
    pre = jnp.concatenate([jnp.zeros((B, 1, D), F32), prefix], axis=1)
    return pl.pallas_call(
        functools.partial(_pool_diff_kernel, T=T, pos0=pos0),
        grid=(B, len(POOL_WINDOWS), cpg),
        in_specs=[pl.BlockSpec((1, T, tc), lambda b, g, c: (b, 0, g * cpg + c)),
                  pl.BlockSpec((1, POOL_PAD, tc), lambda b, g, c: (b, 0, g * cpg + c))],
        out_specs=pl.BlockSpec((1, T, tc), lambda b, g, c: (b, 0, g * cpg + c)),
        out_shape=jax.ShapeDtypeStruct((B, T, D), BF16),
        scratch_shapes=[pltpu.VMEM((POOL_PAD + T, tc), F32)],
        compiler_params=_params(("parallel", "parallel", "parallel")),
        name="pool_diff",
    )(a, pre)


def _cmp_partial_kernel(pt_ref, kpage_ref, vpage_ref, wk_ref, wv_ref, abk_ref, abv_ref, xk_ref, xv_ref, *, cp):
    del pt_ref
    p = pl.program_id(2)
    row0 = pl.multiple_of(p * 8, 8)
    for page_ref, x_ref in ((kpage_ref, xk_ref), (vpage_ref, xv_ref)):
        for g in range(N_KV_HEADS):
            for r in range(CMP_STRIDE):
                x_ref[g, pl.ds(row0, 8), r * HEAD_DIM:(r + 1) * HEAD_DIM] = (
                    page_ref[0, pl.ds(r, 8, stride=CMP_STRIDE), g, :])

    @pl.when(p == cp - 1)
    def _():
        for x_ref, w_ref, ab_ref in ((xk_ref, wk_ref, abk_ref), (xv_ref, wv_ref, abv_ref)):
            for g in range(N_KV_HEADS):
                ab_ref[0, g] = _dot(x_ref[g].astype(BF16), w_ref[...])


def cmp_partials(k_pages, v_pages, table, wk_cat, wv_cat, cp):
    B, ppb = table.shape
    page = k_pages.shape[1]
    spp = page // CMP_STRIDE
    assert spp == 8, "one page must fill one 8-sublane group of sub-blocks"
    nch = ppb // cp
    m = cp * spp
    kdim = CMP_STRIDE * HEAD_DIM
    page_spec = pl.BlockSpec((1, page, N_KV_HEADS, HEAD_DIM), lambda b, c, p, pt: (pt[b, c * cp + p], 0, 0, 0))
    w_spec = pl.BlockSpec((kdim, 2 * CMP_HIDDEN), lambda b, c, p, pt: (0, 0))
    out_spec = pl.BlockSpec((1, N_KV_HEADS, m, 2 * CMP_HIDDEN), lambda b, c, p, pt: (b, 0, c, 0))
    out_shape = jax.ShapeDtypeStruct((B, N_KV_HEADS, ppb * spp, 2 * CMP_HIDDEN), F32)
    return pl.pallas_call(
        functools.partial(_cmp_partial_kernel, cp=cp),
        grid_spec=pltpu.PrefetchScalarGridSpec(
            num_scalar_prefetch=1,
            grid=(B, nch, cp),
            in_specs=[page_spec, page_spec, w_spec, w_spec],
            out_specs=[out_spec, out_spec],
            scratch_shapes=[pltpu.VMEM((N_KV_HEADS, m, kdim), F32), pltpu.VMEM((N_KV_HEADS, m, kdim), F32)],
        ),
        out_shape=[out_shape, out_shape],
        compiler_params=_params(("parallel", "parallel", "arbitrary")),
        name="cmp_partials",
    )(table, k_pages, v_pages, wk_cat, wv_cat)


def _cmp_finish_kernel(abk_ref, abv_ref, pek_ref, pev_ref, wk1_ref, wv1_ref, wk2_ref, wv2_ref, gk_ref,
                       ck_ref, cv_ref):
    n = abk_ref.shape[2]
    for ab_ref, pe_ref, w1_ref, w2_ref, o_ref, is_k in ((abk_ref, pek_ref, wk1_ref, wk2_ref, ck_ref, True),
                                                        (abv_ref, pev_ref, wv1_ref, wv2_ref, cv_ref, False)):
        ab = ab_ref[0, 0]
        bias = _dot(jnp.broadcast_to(pe_ref[...], (8, pe_ref.shape[1])), w1_ref[...])[0:1, :]
        pre = ab[:, :CMP_HIDDEN] + pltpu.roll(ab[:, CMP_HIDDEN:], n - 1, 0) + bias
        hid = pre * _sigmoid(pre)
        out = _dot(hid.astype(BF16), w2_ref[...])
        if is_k:
            out = _head_norm(out, gk_ref[...])
        o_ref[0, 0] = out.astype(o_ref.dtype)


def cmp_finish(abk, abv, pe_k, pe_v, wk1, wv1, wk2, wv2, g_k_cmp):
    B, G, n, _ = abk.shape
    ab_spec = pl.BlockSpec((1, 1, n, 2 * CMP_HIDDEN), lambda b, g: (b, g, 0, 0))
    full = lambda a: pl.BlockSpec(a.shape, lambda b, g: (0,) * a.ndim)
    out_spec = pl.BlockSpec((1, 1, n, HEAD_DIM), lambda b, g: (b, g, 0, 0))
    out_shape = jax.ShapeDtypeStruct((B, G, n, HEAD_DIM), BF16)
    args = (pe_k, pe_v, wk1, wv1, wk2, wv2, g_k_cmp)
    return pl.pallas_call(
        _cmp_finish_kernel,
        grid=(B, G),
        in_specs=[ab_spec, ab_spec] + [full(a) for a in args],
        out_specs=[out_spec, out_spec],
        out_shape=[out_shape, out_shape],
        compiler_params=_params(("parallel", "parallel")),
        name="cmp_finish",
    )(abk, abv, *args)


def _select_blocks(score, blk, ns):
    rank = jnp.zeros(score.shape, jnp.int32)
    for j in range(ns):
        sj = score[j:j + 1, :]
        beats = (sj > score) | ((sj == score) & (j < blk))
        rank = rank + beats.astype(jnp.int32)
    return (rank < min(N_SEL, ns)) & (score > 0.5 * NEG)


def _attn_prompt_kernel(q_ref, qr_ref, ck_ref, cv_ref, ks_ref, vs_ref, kw_ref, vw_ref, gate_ref, mapT_ref,
                        expand_ref, o_ref, part_ref, sbias_ref, wbias_ref, m_ref, l_ref, acc_ref, *, tq, kc, nc, ns, R):
    qi = pl.program_id(2)
    q0 = qi * tq
    pos = q0 + lax.broadcasted_iota(jnp.int32, (tq, 1), 0)
    ncp = ck_ref.shape[2]

    ck = ck_ref[0, 0]
    cv = cv_ref[0, 0]
    cidx = lax.broadcasted_iota(jnp.int32, (1, ncp), 1)
    ok_c = (cidx * CMP_STRIDE + CMP_BLOCK - 1 <= pos) & (cidx < nc)
    imp = jnp.zeros((tq, ncp), F32)
    for r in range(R):
        qh = q_ref[0, :, r * HEAD_DIM:(r + 1) * HEAD_DIM]
        s = jnp.where(ok_c, _dot_nt(qh, ck) * SCALE, NEG)
        e = jnp.exp(s - jnp.max(s, axis=1, keepdims=True))
        p = jnp.where(ok_c, e * (1.0 / jnp.sum(e, axis=1, keepdims=True)), 0.0)
        imp = imp + p
        part_ref[r] = gate_ref[0, :, r * N_BRANCH:r * N_BRANCH + 1] * _dot(p.astype(BF16), cv)

    nsp = mapT_ref.shape[0]
    p_slc = lax.dot_general(mapT_ref[...], imp, (((1,), (1,)), ((), ())), precision=lax.Precision.HIGHEST,
                            preferred_element_type=F32)
    blk = lax.broadcasted_iota(jnp.int32, (nsp, tq), 0)
    pos_l = q0 + lax.broadcasted_iota(jnp.int32, (nsp, tq), 1)
    cur = pos_l // SEL_BLOCK
    vis = blk * SEL_BLOCK <= pos_l
    forced = vis & ((blk == 0) | (blk == cur) | (blk == cur - 1))
    score = jnp.where(forced, FORCE, jnp.where(vis, p_slc, NEG))
    score = jnp.where(blk < ns, score, PAD_SCORE)
    sel = _select_blocks(score, blk, ns).astype(BF16)

    c_hi = (q0 + tq) // kc
    col = lax.broadcasted_iota(jnp.int32, (1, kc), 1)
    for c in range(sbias_ref.shape[0]):
        @pl.when(c < c_hi)
        def _(c=c):
            ok = (_dot_tn(sel, expand_ref[:, c * kc:(c + 1) * kc]) > 0.5) & (c * kc + col <= pos)
            sbias_ref[c] = jnp.where(ok, 0.0, NEG)
    nwc = wbias_ref.shape[0]
    c_w0 = c_hi - nwc
    for d in range(nwc):
        kpos = (c_w0 + d) * kc + col
        wbias_ref[d] = jnp.where((kpos <= pos) & (kpos > pos - WINDOW), 0.0, NEG)

    def fold(t):
        return [t[:, i * LANE:(i + 1) * LANE] for i in range(kc // LANE)]

    def branch(k_ref, v_ref, c_lo, bias_of, gate_col):
        def logits(r, c, bias):
            k = k_ref[0, pl.ds(pl.multiple_of(c * kc, kc), kc), :]
            return _dot_nt(qr_ref[0, :, r * HEAD_DIM:(r + 1) * HEAD_DIM], k) * SCALE + bias

        m_ref[...] = jnp.full(m_ref.shape, NEG, F32)
        l_ref[...] = jnp.zeros(l_ref.shape, F32)
        acc_ref[...] = jnp.zeros(acc_ref.shape, F32)

        def max_body(c, carry):
            bias = bias_of(c)
            for r in range(R):
                mx = m_ref[r]
                for part in fold(logits(r, c, bias)):
                    mx = jnp.maximum(mx, part)
                m_ref[r] = mx
            return carry

        lax.fori_loop(c_lo, c_hi, max_body, 0)
        for r in range(R):
            m_ref[r] = jnp.broadcast_to(jnp.max(m_ref[r], axis=1, keepdims=True), (tq, LANE))

        def sum_body(c, carry):
            bias = bias_of(c)
            v = v_ref[0, pl.ds(pl.multiple_of(c * kc, kc), kc), :]
            for r in range(R):
                t = logits(r, c, bias)
                m = m_ref[r]
                ps = [jnp.exp(part - m) for part in fold(t)]
                acc_ref[r] += _dot(jnp.concatenate(ps, axis=1).astype(BF16), v)
                lsum = l_ref[r]
                for part in ps:
                    lsum = lsum + part
                l_ref[r] = lsum
            return carry

        lax.fori_loop(c_lo, c_hi, sum_body, 0)
        for r in range(R):
            gate = gate_ref[0, :, r * N_BRANCH + gate_col:r * N_BRANCH + gate_col + 1]
            inv_l = 1.0 / jnp.sum(l_ref[r], axis=1, keepdims=True)
            part_ref[r] += gate * (acc_ref[r] * inv_l)

    branch(ks_ref, vs_ref, 0, lambda c: sbias_ref[c], 1)
    branch(kw_ref, vw_ref, jnp.maximum(c_w0, 0), lambda c: wbias_ref[c - c_w0], 2)
    for r in range(R):
        o_ref[0, :, r * HEAD_DIM:(r + 1) * HEAD_DIM] = part_ref[r].astype(o_ref.dtype)


def _overlap_map(ncp, nsp, ns):
    ratio = CMP_BLOCK // CMP_STRIDE
    per_sel = SEL_BLOCK // CMP_STRIDE
    m = np.zeros((ncp, nsp), np.float32)
    for b in range(ns):
        for mm_ in range(per_sel):
            for n in range(ratio):
                j = per_sel * b + mm_ - n
                if 0 <= j < ncp:
                    m[j, b] += 1.0
    return m


def attn_prompt(q, qr, ck, cv, ks, vs, kw, vw, gates, *, nc):
    B, T, HD = q.shape
    G = N_KV_HEADS
    R = HD // HEAD_DIM // G
    tq = min(T, 256)
    kc = tq
    ns = T // SEL_BLOCK
    nsp = -(-ns // 8) * 8
    ncp = ck.shape[2]
    mapT = jnp.asarray(_overlap_map(ncp, nsp, ns).T)
    expand = jnp.asarray((np.arange(T)[None, :] // SEL_BLOCK == np.arange(nsp)[:, None]).astype(np.float32), BF16)
    q_spec = pl.BlockSpec((1, tq, R * HEAD_DIM), lambda b, g, i: (b, i, g))
    c_spec = pl.BlockSpec((1, 1, ncp, HEAD_DIM), lambda b, g, i: (b, g, 0, 0))
    kv_spec = pl.BlockSpec((1, T, HEAD_DIM), lambda b, g, i: (b, 0, g))
    return pl.pallas_call(
        functools.partial(_attn_prompt_kernel, tq=tq, kc=kc, nc=nc, ns=ns, R=R),
        grid=(B, G, T // tq),
        in_specs=[q_spec, q_spec, c_spec, c_spec, kv_spec, kv_spec, kv_spec, kv_spec,
                  pl.BlockSpec((1, tq, LANE), lambda b, g, i: (b, i, g)),
                  pl.BlockSpec((nsp, ncp), lambda b, g, i: (0, 0)),
                  pl.BlockSpec((nsp, T), lambda b, g, i: (0, 0))],
        out_specs=q_spec,
        out_shape=jax.ShapeDtypeStruct((B, T, HD), BF16),
        scratch_shapes=[pltpu.VMEM((R, tq, HEAD_DIM), F32), pltpu.VMEM((T // kc, tq, kc), F32),
                        pltpu.VMEM((min(WINDOW, T) // kc + tq // kc, tq, kc), F32)]
                       + [pltpu.VMEM((R, tq, HEAD_DIM), F32)] * 3,
        compiler_params=_params(("parallel", "parallel", "arbitrary")),
        name="attn_prompt",
    )(q, qr, ck, cv, ks, vs, kw, vw, gates, mapT, expand)


def _attn_dec_dense_kernel(q_ref, qr_ref, ck_ref, cv_ref, kw_ref, vw_ref, map_ref, ocmp_ref, owin_ref, ids_ref,
                           *, nc, ns, pos):
    G, R = q_ref.shape[1], q_ref.shape[2]
    ncp = ck_ref.shape[2]
    nsl = map_ref.shape[1]
    cidx = lax.broadcasted_iota(jnp.int32, (1, ncp), 1)
    ok_c = (cidx * CMP_STRIDE + CMP_BLOCK - 1 <= pos) & (cidx < nc)
    blk_l = lax.broadcasted_iota(jnp.int32, (1, nsl), 1)
    cur = pos // SEL_BLOCK
    vis = blk_l * SEL_BLOCK <= pos
    forced = vis & ((blk_l == 0) | (blk_l == cur) | (blk_l == cur - 1))
    ii = lax.broadcasted_iota(jnp.int32, (nsl, nsl), 0)
    jj = lax.broadcasted_iota(jnp.int32, (nsl, nsl), 1)
    slot = lax.broadcasted_iota(jnp.int32, (nsl, LANE), 1).astype(F32)
    blk_s = lax.broadcasted_iota(jnp.int32, (nsl, LANE), 0).astype(F32)
    for g in range(G):
        s = jnp.where(ok_c, _dot_nt(q_ref[0, g], ck_ref[0, g]) * SCALE, NEG)
        e = jnp.exp(s - jnp.max(s, axis=1, keepdims=True))
        p = jnp.where(ok_c, e * (1.0 / jnp.sum(e, axis=1, keepdims=True)), 0.0)
        ocmp_ref[0, g] = _dot(p.astype(BF16), cv_ref[0, g])
        imp = jnp.broadcast_to(jnp.sum(p, axis=0, keepdims=True), (R, ncp))
        p_slc = jnp.dot(imp, map_ref[...], precision=lax.Precision.HIGHEST, preferred_element_type=F32)[0:1, :]
        score_l = jnp.where(forced, FORCE, jnp.where(vis, p_slc, NEG))
        score_l = jnp.where(blk_l < ns, score_l, PAD_SCORE)
        score_s = jnp.sum(jnp.where(ii == jj, score_l, 0.0), axis=1, keepdims=True)
        beats = (score_l > score_s) | ((score_l == score_s) & (jj < ii))
        rank = jnp.sum(beats.astype(F32), axis=1, keepdims=True)
        ids = jnp.sum(jnp.where(rank == slot, blk_s, 0.0), axis=0, keepdims=True)
        ids_ref[0, g] = ids[:, :N_SEL].astype(jnp.int32)
        kw = kw_ref[0, :, g, :].astype(BF16)
        vw = vw_ref[0, :, g, :].astype(BF16)
        s = _dot_nt(qr_ref[0, g], kw) * SCALE
        e = jnp.exp(s - jnp.max(s, axis=1, keepdims=True))
        p = e * (1.0 / jnp.sum(e, axis=1, keepdims=True))
        owin_ref[0, g] = _dot(p.astype(BF16), vw)


def attn_dec_dense(q, qr, ck, cv, kw, vw, *, nc, ns, pos):
    B, G, R, _ = q.shape
    ncp = ck.shape[2]
    wb = kw.shape[1]
    nsl = -(-ns // LANE) * LANE
    omap = jnp.asarray(_overlap_map(ncp, nsl, ns))
    q_spec = pl.BlockSpec((1, G, R, HEAD_DIM), lambda b: (b, 0, 0, 0))
    c_spec = pl.BlockSpec((1, G, ncp, HEAD_DIM), lambda b: (b, 0, 0, 0))
    w_spec = pl.BlockSpec((1, wb, G, HEAD_DIM), lambda b: (b, 0, 0, 0))
    return pl.pallas_call(
        functools.partial(_attn_dec_dense_kernel, nc=nc, ns=ns, pos=pos),
        grid=(B,),
        in_specs=[q_spec, q_spec, c_spec, c_spec, w_spec, w_spec, pl.BlockSpec((ncp, nsl), lambda b: (0, 0))],
        out_specs=[q_spec, q_spec, pl.BlockSpec((1, G, 1, N_SEL), lambda b: (b, 0, 0, 0))],
        out_shape=[jax.ShapeDtypeStruct((B, G, R, HEAD_DIM), F32), jax.ShapeDtypeStruct((B, G, R, HEAD_DIM), F32),
                   jax.ShapeDtypeStruct((B, G, 1, N_SEL), jnp.int32)],
        compiler_params=_params(("parallel",)),
        name="attn_dec_dense",
    )(q, qr, ck, cv, kw, vw, omap)


def _attn_dec_sel_kernel(pt_ref, ids_ref, qr_ref, *refs, n_past, pos):
    del pt_ref
    G = qr_ref.shape[1]
    kc_refs, vc_refs = refs[:G], refs[G:2 * G]
    kn_ref, vn_ref, ocmp_ref, owin_ref, gate_ref, o_ref, m_ref, l_ref, acc_ref = refs[2 * G:]
    b, n = pl.program_id(0), pl.program_id(1)

    @pl.when(n == 0)
    def _():
        m_ref[...] = jnp.full(m_ref.shape, NEG, F32)
        l_ref[...] = jnp.zeros(l_ref.shape, F32)
        acc_ref[...] = jnp.zeros(acc_ref.shape, F32)

    row = lax.broadcasted_iota(jnp.int32, (SEL_BLOCK, 1), 0)
    for g in range(G):
        bid = ids_ref[(b * G + g) * N_SEL + n]
        is_new = bid >= n_past
        first = (row == 0) & (bid == n_past)
        sl = slice(g * HEAD_DIM, (g + 1) * HEAD_DIM)
        k = jnp.where(is_new, jnp.where(first, kn_ref[0, :, sl], 0.0), kc_refs[g][0, :, g, :]).astype(BF16)
        v = jnp.where(is_new, jnp.where(first, vn_ref[0, :, sl], 0.0), vc_refs[g][0, :, g, :]).astype(BF16)
        kpos = bid * SEL_BLOCK + lax.broadcasted_iota(jnp.int32, (1, SEL_BLOCK), 1)
        ok = kpos <= pos
        s = jnp.where(ok, _dot_nt(qr_ref[0, g], k) * SCALE, NEG)
        m_prev = m_ref[g]
        m_new = jnp.maximum(m_prev, jnp.max(s, axis=1, keepdims=True))
        alpha = jnp.exp(m_prev - m_new)
        p = jnp.where(ok, jnp.exp(s - m_new), 0.0)
        l_ref[g] = alpha * l_ref[g] + jnp.sum(p, axis=1, keepdims=True)
        acc_ref[g] = alpha * acc_ref[g] + _dot(p.astype(BF16), v)
        m_ref[g] = m_new

    @pl.when(n == N_SEL - 1)
    def _():
        for g in range(G):
            gt = gate_ref[0, g]
            o_sel = acc_ref[g] * (1.0 / l_ref[g])
            o_ref[0, g] = gt[:, 0:1] * ocmp_ref[0, g] + gt[:, 1:2] * o_sel + gt[:, 2:3] * owin_ref[0, g]


def attn_dec_sel(table, ids, qr, k_cache, v_cache, k_new, v_new, ocmp, owin, gates, *, pos):
    B, G, R, _ = qr.shape
    page = k_cache.shape[1]
    bpp = page // SEL_BLOCK
    n_past = table.shape[1] * bpp
    kc = k_cache.reshape(k_cache.shape[0] * bpp, SEL_BLOCK, G, HEAD_DIM)
    vc = v_cache.reshape(v_cache.shape[0] * bpp, SEL_BLOCK, G, HEAD_DIM)

    def cache_spec(g):
        def index(b, n, pt, ids_):
            bid = jnp.minimum(ids_[(b * G + g) * N_SEL + n], n_past - 1)
            return (pt[b, bid // bpp] * bpp + bid % bpp, 0, 0, 0)
        return pl.BlockSpec((1, SEL_BLOCK, G, HEAD_DIM), index)

    q_spec = pl.BlockSpec((1, G, R, HEAD_DIM), lambda b, n, pt, ids_: (b, 0, 0, 0))
    n_spec = pl.BlockSpec((1, 1, G * HEAD_DIM), lambda b, n, pt, ids_: (b, 0, 0))
    g_spec = pl.BlockSpec((1, G, R, LANE), lambda b, n, pt, ids_: (b, 0, 0, 0))
    c_specs = [cache_spec(g) for g in range(G)]
    return pl.pallas_call(
        functools.partial(_attn_dec_sel_kernel, n_past=n_past, pos=pos),
        grid_spec=pltpu.PrefetchScalarGridSpec(
            num_scalar_prefetch=2,
            grid=(B, N_SEL),
            in_specs=[q_spec] + c_specs + c_specs + [n_spec, n_spec, q_spec, q_spec, g_spec],
            out_specs=q_spec,
            scratch_shapes=[pltpu.VMEM((G, R, 1), F32), pltpu.VMEM((G, R, 1), F32),
                            pltpu.VMEM((G, R, HEAD_DIM), F32)],
        ),
        out_shape=jax.ShapeDtypeStruct((B, G, R, HEAD_DIM), F32),
        compiler_params=_params(("parallel", "arbitrary")),
        name="attn_dec_sel",
    )(table, ids.reshape(-1), qr, *([kc] * G), *([vc] * G), k_new, v_new, ocmp, owin, gates)


def _rope_tables(pos):
    half = HEAD_DIM // 2
    inv = ROPE_THETA ** (-jnp.arange(half, dtype=F32) / half)
    ang = pos.astype(F32)[:, None] * inv[None, :]
    cos, sin = jnp.cos(ang), jnp.sin(ang)
    return jnp.concatenate([cos, cos], axis=1), jnp.concatenate([-sin, sin], axis=1)


def _prep_weights(W):
    D = W['w_kv'].shape[0]
    H = D // HEAD_DIM
    R = H // N_KV_HEADS
    pg = D // len(POOL_WINDOWS)
    half = CMP_STRIDE * HEAD_DIM
    P = {}
    P['w_pool'] = W['w_pool'].astype(BF16).reshape(-1, len(POOL_WINDOWS) * pg, pg)
    kv_gain = jnp.ones((W['w_kv'].shape[1], KV_W), F32)
    kv_gain = kv_gain.at[2].set(jnp.tile(W['g_k_sel'], N_KV_HEADS)).at[4].set(jnp.tile(W['g_k_win'], N_KV_HEADS))
    P['kv_gain'] = kv_gain
    for t in ('k', 'v'):
        w1 = W['w_cmp_%s1' % t].astype(BF16)
        P['w_cmp_%s1' % t] = w1
        P['w_cmp_%scat' % t] = jnp.concatenate([w1[:half], w1[half:]], axis=1)
        P['w_cmp_%s2' % t] = W['w_cmp_%s2' % t].astype(BF16)
        P['pe_%s' % t] = W['pe_cmp_%s' % t].astype(BF16).reshape(1, -1)
    n_b = W['w_qg'].shape[0]
    wg = W['w_qg'][:, :, H * HEAD_DIM:].astype(BF16).reshape(n_b, D, N_KV_HEADS, R * N_BRANCH)
    wg = jnp.pad(wg, ((0, 0), (0, 0), (0, 0), (0, LANE - R * N_BRANCH)))
    P['w_gate'] = wg.reshape(n_b, D, N_KV_HEADS * LANE)
    P['w_ple'] = W['w_ple'].astype(BF16)
    return P


def _dense(x, wname, layer, W, P, *, N, epilogue, specs, extras=(), out_dtypes, emit=False, name):
    M, K = x.shape
    shapes = [d if isinstance(d, jax.ShapeDtypeStruct) else jax.ShapeDtypeStruct((M, N), d) for d in out_dtypes]
    key = (wname, layer)
    tmx = 1024 if M >= 1024 else M
    if key in P:
        grid, tm, tn, tk = _tiles(M, N, K, tmx, 1024, 4096)
        extra_specs, out_specs = specs(tm, tn)
        return mm(x, P[key], grid=grid, tm=tm, tn=tn, tk=tk, epilogue=epilogue, extras=extras,
                  extra_specs=extra_specs, out_shapes=shapes, out_specs=out_specs, name=name)
    grid, tm, tn, tk = _tiles(M, N, K, tmx, 512, K if K <= 4096 else 2048)
    extra_specs, out_specs = specs(tm, tn)
    w = W[wname]
    outs = list(mm_ws(x, w, layer=layer if w.ndim == 3 else None, emit=emit, grid=grid, tm=tm, tn=tn, tk=tk,
                      epilogue=epilogue, extras=extras, extra_specs=extra_specs, out_shapes=shapes,
                      out_specs=out_specs, name=name))
    if emit:
        P[key] = outs.pop()
    return outs


def _ffn_ple(h, p_l, layer, W, P):
    M, D = h.shape
    F = W['w_up'].shape[2]
    emit = M >= 1024
    mn = lambda tm, tn: ([], [_mn_spec(tm, tn)])
    res = lambda tm, tn: ([_mn_spec(tm, tn)], [_mn_spec(tm, tn)])
    (m_,) = rmsnorm_rows(h, [W['g_ffn'][layer]], [BF16])
    (u,) = _dense(m_, 'w_up', layer, W, P, N=F, epilogue=lambda acc: (jnp.square(jnp.maximum(acc, 0.0)),),
                  specs=mn, out_dtypes=[BF16], emit=emit, name="ffn_up")
    (h,) = _dense(u, 'w_down', layer, W, P, N=D, epilogue=lambda acc, r: (r[...] + acc,), specs=res, extras=[h],
                  out_dtypes=[F32], emit=emit, name="ffn_down")
    (e_,) = rmsnorm_rows(h, [W['g_ple'][layer]], [BF16])
    ple_dim = p_l.shape[1]
    ple_specs = lambda tm, tn: ([_mn_spec(tm, tn), pl.BlockSpec((tm, ple_dim), lambda i, j, k: (i, 0)),
                                 pl.BlockSpec((ple_dim, tn), lambda i, j, k: (0, j))], [_mn_spec(tm, tn)])
    (h,) = _dense(e_, 'w_ple_gate', layer, W, P, N=D,
                  epilogue=lambda acc, r, pp, wp: (r[...] + _dot(pp[...], wp[...]) * _sigmoid(acc),),
                  specs=ple_specs, extras=[h, p_l.astype(BF16), P['w_ple'][layer]], out_dtypes=[F32], name="ple")
    return h


def _kv_epilogue(acc, gain_ref, cos_ref, sin_ref):
    j = pl.program_id(0)
    cosf, sinf = cos_ref[...], sin_ref[...]
    heads = []
    for hh in range(N_KV_HEADS):
        sl = slice(hh * HEAD_DIM, (hh + 1) * HEAD_DIM)
        heads.append(_rope(_head_norm(acc[:, sl], gain_ref[0, :, sl]), cosf, sinf))
    out = jnp.where((j == 2) | (j == 4), jnp.concatenate(heads, axis=1), acc)
    return out, out


def _q_epilogue(acc, gq_ref, cos_ref, sin_ref):
    cosf, sinf = cos_ref[...], sin_ref[...]
    qs, qrs = [], []
    for hh in range(acc.shape[1] // HEAD_DIM):
        qn = _head_norm(acc[:, hh * HEAD_DIM:(hh + 1) * HEAD_DIM], gq_ref[...])
        qs.append(qn)
        qrs.append(_rope(qn, cosf, sinf))
    return jnp.concatenate(qs, axis=1), jnp.concatenate(qrs, axis=1)


def _trunk(x, p, pool_prefix, pos0, W, P, attend):
    B, T, D = x.shape
    M = B * T
    tmx = 1024 if M >= 1024 else M
    h = x.reshape(M, D)
    pg = D // len(POOL_WINDOWS)

    (a0,) = rmsnorm_rows(h, [W['g_mix'][0]], [F32])
    a0 = a0.reshape(B, T, D)
    d = pool_diff(a0, pool_prefix[0], pos0).reshape(M, D)
    pool_new = jnp.concatenate([pool_prefix[0], a0], axis=1)[:, -POOL_STATE:][None]
    grid, tm, tn, tk = _tiles(M, D, pg, tmx, pg, pg)
    (h,) = mm(d, P['w_pool'][0], grid=grid, tm=tm, tn=tn, tk=tk,
              x_map=lambda i, j, k: (i, j), w_map=lambda i, j, k: (j, 0),
              epilogue=lambda acc, sc, r: (r[...] + acc * sc[...],),
              extras=[W['pool_scale'][0].reshape(1, D), h],
              extra_specs=[pl.BlockSpec((1, tn), lambda i, j, k: (0, j)), _mn_spec(tm, tn)],
              out_shapes=[jax.ShapeDtypeStruct((M, D), F32)], out_specs=[_mn_spec(tm, tn)], name="pool_mix")
    h = _ffn_ple(h, p[0].reshape(M, -1), 0, W, P)

    hkv, a1 = rmsnorm_rows(h, [W['g_kv'], W['g_mix'][1]], [BF16, BF16])
    pos = pos0 + jnp.tile(jnp.arange(T, dtype=jnp.int32), B)
    cosf, sinf = _rope_tables(pos)
    n_kv = W['w_kv'].shape[1]
    rope_spec = lambda tm: pl.BlockSpec((tm, HEAD_DIM), lambda i, j, k: (i, 0))
    kv_spec = lambda tm: pl.BlockSpec((1, tm, KV_W), lambda i, j, k: (j, i, 0))
    kv_specs = lambda tm, tn: ([pl.BlockSpec((1, 1, KV_W), lambda i, j, k: (j, 0, 0)), rope_spec(tm), rope_spec(tm)],
                               [kv_spec(tm), kv_spec(tm)])
    kv, kv_b = _dense(hkv, 'w_kv2d', None, W, P, N=n_kv * KV_W, epilogue=_kv_epilogue, specs=kv_specs,
                      extras=[P['kv_gain'].reshape(n_kv, 1, KV_W), cosf, sinf],
                      out_dtypes=[jax.ShapeDtypeStruct((n_kv, M, KV_W), F32),
                                  jax.ShapeDtypeStruct((n_kv, M, KV_W), BF16)], name="kv_proj")
    q_specs = lambda tm, tn: ([pl.BlockSpec((1, HEAD_DIM), lambda i, j, k: (0, 0)), rope_spec(tm), rope_spec(tm)],
                              [_mn_spec(tm, tn)] * 2)
    q, qr = _dense(a1, 'w_qg', 0, W, P, N=D, epilogue=_q_epilogue, specs=q_specs,
                   extras=[W['g_q'][0].reshape(1, HEAD_DIM), cosf, sinf], out_dtypes=[BF16, BF16], name="q_proj")
    ng = N_KV_HEADS * LANE
    grid, tm, tn, tk = _tiles(M, ng, D, tmx, ng, D)
    (gates,) = mm(a1, P['w_gate'][0], grid=grid, tm=tm, tn=tn, tk=tk, epilogue=lambda acc: (_sigmoid(acc),),
                  out_shapes=[jax.ShapeDtypeStruct((M, ng), F32)], out_specs=[_mn_spec(tm, tn)], name="gate_proj")

    o, win_state = attend(kv, kv_b, q, qr, gates)

    (h,) = _dense(o, 'w_o', 0, W, P, N=D, epilogue=lambda acc, r: (r[...] + acc,), extras=[h],
                  specs=lambda tm, tn: ([_mn_spec(tm, tn)], [_mn_spec(tm, tn)]), out_dtypes=[F32], name="attn_out")
    h = _ffn_ple(h, p[1].reshape(M, -1), 1, W, P)
    rows = tuple(kv[n].reshape(B, T, N_KV_HEADS, HEAD_DIM) for n in range(4))
    return h.reshape(B, T, D), pool_new, rows, win_state


def _compress(k_pages, v_pages, table, cp, W, P):
    abk, abv = cmp_partials(k_pages, v_pages, table, P['w_cmp_kcat'], P['w_cmp_vcat'], cp)
    return cmp_finish(abk, abv, P['pe_k'], P['pe_v'], P['w_cmp_k1'], P['w_cmp_v1'], P['w_cmp_k2'], P['w_cmp_v2'],
                      W['g_k_cmp'].reshape(1, HEAD_DIM))


def kernel(x_prompt, x_sample, state_pool, cache_k_cmp, cache_v_cmp, cache_k_sel, cache_v_sel, state_k_win, state_v_win, page_table, p_prompt, p_sample, g_mix, w_pool, pool_scale, g_kv, w_kv, g_k_cmp, g_k_sel, g_k_win, w_cmp_k1, w_cmp_k2, pe_cmp_k, w_cmp_v1, w_cmp_v2, pe_cmp_v, w_qg, g_q, w_o, g_ffn, w_up, w_down, g_ple, w_ple, w_ple_gate):
    W = dict(g_mix=g_mix, w_pool=w_pool, pool_scale=pool_scale, g_kv=g_kv, w_kv=w_kv, g_k_cmp=g_k_cmp,
             g_k_sel=g_k_sel, g_k_win=g_k_win, w_cmp_k1=w_cmp_k1, w_cmp_k2=w_cmp_k2, pe_cmp_k=pe_cmp_k,
             w_cmp_v1=w_cmp_v1, w_cmp_v2=w_cmp_v2, pe_cmp_v=pe_cmp_v, w_qg=w_qg, g_q=g_q, w_o=w_o,
             g_ffn=g_ffn, w_up=w_up, w_down=w_down, g_ple=g_ple, w_ple=w_ple, w_ple_gate=w_ple_gate)
    P = _prep_weights(W)
    W['w_kv2d'] = w_kv.reshape(w_kv.shape[0], -1)
    Bp, Tp, D = x_prompt.shape
    Bs, Ts, _ = x_sample.shape
    assert Ts == 1, "the decode path handles one new token per sequence"
    page = cache_k_cmp.shape[1]
    past_len = page_table.shape[1] * page
    R = D // HEAD_DIM // N_KV_HEADS
    assert Tp % page == 0 and past_len % SEL_BLOCK == 0

    def attend_prompt(kv, kv_b, q, qr, gates):
        ppb = Tp // page
        table = jnp.arange(Bp * ppb, dtype=jnp.int32).reshape(Bp, ppb)
        pages = lambda a: a.reshape(-1, page, N_KV_HEADS, HEAD_DIM)
        ck, cv = _compress(pages(kv[0]), pages(kv[1]), table, ppb, W, P)
        nc = Tp // CMP_STRIDE - CMP_BLOCK // CMP_STRIDE + 1
        seq = lambda a: a.reshape(Bp, Tp, -1)
        o = attn_prompt(seq(q), seq(qr), ck, cv, seq(kv_b[2]), seq(kv_b[3]), seq(kv_b[4]), seq(kv_b[5]),
                        seq(gates), nc=nc)
        nw = min(WINDOW, Tp)
        win = tuple(kv[n].reshape(Bp, Tp, N_KV_HEADS, HEAD_DIM)[:, -nw:] for n in (4, 5))
        return o.reshape(Bp * Tp, D), win

    def attend_sample(kv, kv_b, q, qr, gates):
        del kv_b
        ck, cv = _compress(cache_k_cmp, cache_v_cmp, page_table, min(32, page_table.shape[1]), W, P)
        nc = (past_len - (CMP_BLOCK - 1)) // CMP_STRIDE + 1
        ns = past_len // SEL_BLOCK + 1
        wb = state_k_win.shape[1]
        new_row = lambda a: a.reshape(Bs, 1, N_KV_HEADS, HEAD_DIM)
        kw = jnp.concatenate([state_k_win, new_row(kv[4])], axis=1)[:, -wb:]
        vw = jnp.concatenate([state_v_win, new_row(kv[5])], axis=1)[:, -wb:]
        heads = lambda a: a.reshape(Bs, N_KV_HEADS, R, HEAD_DIM)
        ocmp, owin, ids = attn_dec_dense(heads(q), heads(qr), ck, cv, kw, vw, nc=nc, ns=ns, pos=past_len)
        gt = gates.reshape(Bs, N_KV_HEADS, LANE)[:, :, :R * N_BRANCH].reshape(Bs, N_KV_HEADS, R, N_BRANCH)
        gt = jnp.pad(gt, ((0, 0), (0, 0), (0, 0), (0, LANE - N_BRANCH)))
        o = attn_dec_sel(page_table, ids, heads(qr), cache_k_sel, cache_v_sel,
                         kv[2][:, None], kv[3][:, None], ocmp, owin, gt, pos=past_len)
        return o.reshape(Bs, D).astype(BF16), (kw, vw)

    pool_zero = jnp.zeros((state_pool.shape[0], Bp, POOL_STATE, D), x_prompt.dtype)
    y_p, pool_p, rows_p, win_p = _trunk(x_prompt, p_prompt, pool_zero, 0, W, P, attend_prompt)
    y_s, pool_s, rows_s, win_s = _trunk(x_sample, p_sample, state_pool, past_len, W, P, attend_sample)
    return (y_p, y_s, pool_p, pool_s, rows_p[0], rows_p[1], rows_p[2], rows_p[3], win_p[0], win_p[1],
            rows_s[0], rows_s[1], rows_s[2], rows_s[3], win_s[0], win_s[1])
```

```python
import functools

import jax
import jax.numpy as jnp
import numpy as np
from jax import lax
from jax.experimental import pallas as pl
from jax.experimental.pallas import tpu as pltpu

F32 = jnp.float32
BF16 = jnp.bfloat16

POOL_WINDOWS = (2, 4, 8, 16)
POOL_STATE = max(POOL_WINDOWS) - 1
POOL_PAD = POOL_STATE + 1
HEAD_DIM = 128
N_KV_HEADS = 4
N_BRANCH = 3
CMP_BLOCK = 32
CMP_STRIDE = 16
CMP_HIDDEN = 2 * HEAD_DIM
SEL_BLOCK = 64
N_SEL = 16
WINDOW = 512
ROPE_THETA = 10000.0
EPS = 1e-6
SCALE = HEAD_DIM ** -0.5
NEG = -1e30
FORCE = 1e9
PAD_SCORE = -3e38
KV_W = N_KV_HEADS * HEAD_DIM
LANE = 128
VMEM_LIMIT = 56 * 1024 * 1024


def _params(sem):
    return pltpu.CompilerParams(dimension_semantics=sem, vmem_limit_bytes=VMEM_LIMIT)


def _sigmoid(x):
    return 1.0 / (1.0 + jnp.exp(-x))


def _dot(a, b):
    return jnp.dot(a, b, preferred_element_type=F32)


def _dot_nt(a, b):
    return lax.dot_general(a, b, (((1,), (1,)), ((), ())), preferred_element_type=F32)


def _dot_tn(a, b):
    return lax.dot_general(a, b, (((0,), (0,)), ((), ())), preferred_element_type=F32)


def _head_norm(x, g):
    return x * lax.rsqrt(jnp.mean(x * x, axis=-1, keepdims=True) + EPS) * g


def _rope(x, cosf, sinf):
    return x * cosf + pltpu.roll(x, HEAD_DIM // 2, 1) * sinf


def _rms_kernel(x_ref, g_ref, *o_refs):
    x = x_ref[...]
    y = x * lax.rsqrt(jnp.mean(x * x, axis=-1, keepdims=True) + EPS)
    for i, o_ref in enumerate(o_refs):
        o_ref[...] = (y * g_ref[i:i + 1, :]).astype(o_ref.dtype)


def rmsnorm_rows(x, gains, dtypes):
    M, D = x.shape
    tm = min(M, 256)
    g = jnp.stack(gains).astype(F32)
    n = len(gains)
    return pl.pallas_call(
        _rms_kernel,
        grid=(M // tm,),
        in_specs=[pl.BlockSpec((tm, D), lambda i: (i, 0)), pl.BlockSpec((n, D), lambda i: (0, 0))],
        out_specs=[pl.BlockSpec((tm, D), lambda i: (i, 0)) for _ in range(n)],
        out_shape=[jax.ShapeDtypeStruct((M, D), dt) for dt in dtypes],
        compiler_params=_params(("parallel",)),
        name="rmsnorm_rows",
    )(x, g)


EPILOGUE_ROWS = 256


def _finish_rows(acc_of, tm, outs, extras, epilogue):
    ch = min(tm, EPILOGUE_ROWS)
    for c in range(tm // ch):
        rows = slice(c * ch, (c + 1) * ch)
        for o_ref, r in zip(outs, epilogue(acc_of(rows), rows, *extras)):
            if len(o_ref.shape) == 3:
                o_ref[0, rows, :] = r.astype(o_ref.dtype)
            else:
                o_ref[rows, :] = r.astype(o_ref.dtype)


def _mm_kernel(*refs, nk, n_extra, n_out, epilogue):
    x_ref, w_ref = refs[0], refs[1]
    extras = refs[2:2 + n_extra]
    outs = refs[2 + n_extra:2 + n_extra + n_out]
    tm = x_ref.shape[0]

    if nk == 1:
        _finish_rows(lambda rows: _dot(x_ref[rows, :], w_ref[...]), tm, outs, extras, epilogue)
    else:
        acc_ref = refs[-1]
        k = pl.program_id(2)

        @pl.when(k == 0)
        def _():
            acc_ref[...] = jnp.zeros_like(acc_ref)

        acc_ref[...] += _dot(x_ref[...], w_ref[...])

        @pl.when(k == nk - 1)
        def _():
            _finish_rows(lambda rows: acc_ref[rows, :], tm, outs, extras, epilogue)


def mm(x, w, *, grid, tm, tn, tk, epilogue, extras=(), extra_specs=(), out_shapes, out_specs,
       x_map=None, w_map=None, name):
    nk = grid[2]
    x_map = x_map or (lambda i, j, k: (i, k))
    w_map = w_map or (lambda i, j, k: (k, j))
    kern = functools.partial(_mm_kernel, nk=nk, n_extra=len(extras), n_out=len(out_shapes), epilogue=epilogue)
    return pl.pallas_call(
        kern,
        grid=grid,
        in_specs=[pl.BlockSpec((tm, tk), x_map), pl.BlockSpec((tk, tn), w_map)] + list(extra_specs),
        out_specs=list(out_specs),
        out_shape=list(out_shapes),
        scratch_shapes=[pltpu.VMEM((tm, tn), F32)] if nk > 1 else [],
        compiler_params=_params(("parallel", "parallel", "arbitrary")),
        name=name,
    )(x, w, *extras)


def _mm_ws_kernel(*refs, nk, n_extra, n_out, epilogue, emit):
    x_ref, w_ref = refs[0], refs[1]
    extras = refs[2:2 + n_extra]
    outs = refs[2 + n_extra:2 + n_extra + n_out]
    rest = refs[2 + n_extra + n_out:]
    wb_ref = rest[1 if emit else 0]
    i, k = pl.program_id(1), pl.program_id(2)

    @pl.when(i == 0)
    def _():
        wb = w_ref[...].astype(BF16)
        wb_ref[k] = wb
        if emit:
            rest[0][...] = wb

    tm = x_ref.shape[0]
    if nk == 1:
        _finish_rows(lambda rows: _dot(x_ref[rows, :], wb_ref[0]), tm, outs, extras, epilogue)
    else:
        acc_ref = rest[-1]

        @pl.when(k == 0)
        def _():
            acc_ref[...] = jnp.zeros_like(acc_ref)

        acc_ref[...] += _dot(x_ref[...], wb_ref[k])

        @pl.when(k == nk - 1)
        def _():
            _finish_rows(lambda rows: acc_ref[rows, :], tm, outs, extras, epilogue)


def mm_ws(x, w, *, layer=None, emit=False, grid, tm, tn, tk, epilogue, extras=(), extra_specs=(), out_shapes,
          out_specs, name):
    gm, gn, nk = grid
    swap = lambda f: (lambda j, i, k: f(i, j, k))
    respec = lambda s: pl.BlockSpec(s.block_shape, swap(s.index_map))
    k_once = lambda i, k: jnp.where(i == 0, k, nk - 1)
    if layer is None:
        w_spec = pl.BlockSpec((tk, tn), lambda j, i, k: (k_once(i, k), j))
    else:
        w_spec = pl.BlockSpec((None, tk, tn), lambda j, i, k: (layer, k_once(i, k), j))
    out_shapes, out_specs = list(out_shapes), [respec(s) for s in out_specs]
    if emit:
        out_shapes.append(jax.ShapeDtypeStruct((nk * tk, gn * tn), BF16))
        out_specs.append(pl.BlockSpec((tk, tn), lambda j, i, k: (k_once(i, k), j)))
    kern = functools.partial(_mm_ws_kernel, nk=nk, n_extra=len(extras), n_out=len(out_shapes) - emit,
                             epilogue=epilogue, emit=emit)
    return pl.pallas_call(
        kern,
        grid=(gn, gm, nk),
        in_specs=[pl.BlockSpec((tm, tk), lambda j, i, k: (i, k)), w_spec] + [respec(s) for s in extra_specs],
        out_specs=out_specs,
        out_shape=out_shapes,
        scratch_shapes=[pltpu.VMEM((nk, tk, tn), BF16)] + ([pltpu.VMEM((tm, tn), F32)] if nk > 1 else []),
        compiler_params=_params(("parallel", "arbitrary", "arbitrary")),
        name=name,
    )(x, w, *extras)


def _tiles(M, N, K, tm, tn, tk):
    tm, tn, tk = min(tm, M), min(tn, N), min(tk, K)
    return (M // tm, N // tn, K // tk), tm, tn, tk


def _mn_spec(tm, tn):
    return pl.BlockSpec((tm, tn), lambda i, j, k: (i, j))


def _pool_diff_kernel(a_ref, pre_ref, d_ref, seq_ref, *, T, pos0):
    seq_ref[0:POOL_PAD, :] = pre_ref[0]
    seq_ref[POOL_PAD:POOL_PAD + T, :] = a_ref[0]
    x_t = a_ref[0]
    pos = pos0 + lax.broadcasted_iota(jnp.int32, (T, 1), 0)
    for g, w in enumerate(POOL_WINDOWS):
        @pl.when(pl.program_id(1) == g)
        def _(w=w):
            s = x_t
            for j in range(1, w):
                s = s + seq_ref[POOL_PAD - j:POOL_PAD - j + T, :]
            cnt = jnp.minimum(pos + 1, w).astype(F32)
            d_ref[0] = (s / cnt - x_t).astype(d_ref.dtype)


def pool_diff(a, prefix, pos0):
    B, T, D = a.shape
    pg = D // len(POOL_WINDOWS)
    tc = min(pg, 512)
    cpg = pg // tc
    pre = jnp.concatenate([jnp.zeros((B, 1, D), F32), prefix], axis=1)
    return pl.pallas_call(
        functools.partial(_pool_diff_kernel, T=T, pos0=pos0),
        grid=(B, len(POOL_WINDOWS), cpg),
        in_specs=[pl.BlockSpec((1, T, tc), lambda b, g, c: (b, 0, g * cpg + c)),
                  pl.BlockSpec((1, POOL_PAD, tc), lambda b, g, c: (b, 0, g * cpg + c))],
        out_specs=pl.BlockSpec((1, T, tc), lambda b, g, c: (b, 0, g * cpg + c)),
        out_shape=jax.ShapeDtypeStruct((B, T, D), BF16),
        scratch_shapes=[pltpu.VMEM((POOL_PAD + T, tc), F32)],
        compiler_params=_params(("parallel", "parallel", "parallel")),
        name="pool_diff",
    )(a, pre)


PAGES_PER_STEP = 2
CMP_FINISH_ROWS = 512


def _cmp_partial_kernel(pt_ref, *refs, steps, pps):
    del pt_ref
    k_pages, v_pages = refs[:pps], refs[pps:2 * pps]
    wk_ref, wv_ref, abk_ref, abv_ref, xk_ref, xv_ref = refs[2 * pps:]
    p = pl.program_id(2)
    rows_per_page = (k_pages[0].shape[1] // CMP_STRIDE) * N_KV_HEADS
    for pages, x_ref in ((k_pages, xk_ref), (v_pages, xv_ref)):
        for q, page_ref in enumerate(pages):
            for sb in range(page_ref.shape[1] // CMP_STRIDE):
                row = q * rows_per_page + sb * N_KV_HEADS
                for r in range(CMP_STRIDE):
                    x_ref[p, row:row + N_KV_HEADS, r * HEAD_DIM:(r + 1) * HEAD_DIM] = page_ref[0, sb * CMP_STRIDE + r]

    @pl.when(p == steps - 1)
    def _():
        for x_ref, w_ref, ab_ref in ((xk_ref, wk_ref, abk_ref), (xv_ref, wv_ref, abv_ref)):
            x = x_ref[...].reshape(ab_ref.shape[1], x_ref.shape[2])
            ab_ref[0] = _dot(x.astype(BF16), w_ref[...])


def cmp_partials(k_pages, v_pages, table, wk_cat, wv_cat, cp):
    B, ppb = table.shape
    page = k_pages.shape[1]
    pps = PAGES_PER_STEP
    rows_per_page = page // CMP_STRIDE * N_KV_HEADS
    nch, steps = ppb // cp, cp // pps
    m = cp * rows_per_page
    kdim = CMP_STRIDE * HEAD_DIM

    def page_spec(q):
        return pl.BlockSpec((1, page, N_KV_HEADS, HEAD_DIM),
                            lambda b, c, p, pt: (pt[b, c * cp + p * pps + q], 0, 0, 0))

    page_specs = [page_spec(q) for q in range(pps)]
    w_spec = pl.BlockSpec((kdim, 2 * CMP_HIDDEN), lambda b, c, p, pt: (0, 0))
    out_spec = pl.BlockSpec((1, m, 2 * CMP_HIDDEN), lambda b, c, p, pt: (b, c, 0))
    out_shape = jax.ShapeDtypeStruct((B, ppb * rows_per_page, 2 * CMP_HIDDEN), F32)
    x_scratch = pltpu.VMEM((steps, pps * rows_per_page, kdim), F32)
    return pl.pallas_call(
        functools.partial(_cmp_partial_kernel, steps=steps, pps=pps),
        grid_spec=pltpu.PrefetchScalarGridSpec(
            num_scalar_prefetch=1,
            grid=(B, nch, steps),
            in_specs=page_specs + page_specs + [w_spec, w_spec],
            out_specs=[out_spec, out_spec],
            scratch_shapes=[x_scratch, x_scratch],
        ),
        out_shape=[out_shape, out_shape],
        compiler_params=_params(("parallel", "parallel", "arbitrary")),
        name="cmp_partials",
    )(table, *([k_pages] * pps), *([v_pages] * pps), wk_cat, wv_cat)


def _cmp_finish_kernel(ab_ref, pe_ref, w1_ref, w2_ref, g_ref, o_ref, out_ref, *, norm):
    n4 = ab_ref.shape[1]
    G = o_ref.shape[1]
    bias = _dot(jnp.broadcast_to(pe_ref[...], (8, pe_ref.shape[1])), w1_ref[...])[0:1, :]
    ch = min(n4, CMP_FINISH_ROWS)
    for c in range(n4 // ch):
        lo, hi = c * ch, (c + 1) * ch
        first = ab_ref[0, lo:hi, :CMP_HIDDEN]
        if hi + G <= n4:
            second = ab_ref[0, lo + G:hi + G, CMP_HIDDEN:]
        else:
            second = jnp.concatenate([ab_ref[0, lo + G:hi, CMP_HIDDEN:], ab_ref[0, hi - G:hi, CMP_HIDDEN:]], axis=0)
        pre = first + second + bias
        out = _dot((pre * _sigmoid(pre)).astype(BF16), w2_ref[...])
        if norm:
            out = _head_norm(out, g_ref[...])
        out_ref[lo:hi, :] = out
    for g in range(G):
        o_ref[0, g] = out_ref[pl.ds(g, n4 // G, stride=G), :].astype(o_ref.dtype)


def cmp_finish(ab, pe, w1, w2, gain, *, norm):
    B, n4, _ = ab.shape
    G = N_KV_HEADS
    full = lambda a: pl.BlockSpec(a.shape, lambda b: (0,) * a.ndim)
    args = (pe, w1, w2, gain)
    return pl.pallas_call(
        functools.partial(_cmp_finish_kernel, norm=norm),
        grid=(B,),
        in_specs=[pl.BlockSpec((1, n4, 2 * CMP_HIDDEN), lambda b: (b, 0, 0))] + [full(a) for a in args],
        out_specs=pl.BlockSpec((1, G, n4 // G, HEAD_DIM), lambda b: (b, 0, 0, 0)),
        out_shape=jax.ShapeDtypeStruct((B, G, n4 // G, HEAD_DIM), BF16),
        scratch_shapes=[pltpu.VMEM((n4, HEAD_DIM), F32)],
        compiler_params=_params(("parallel",)),
        name="cmp_finish",
    )(ab, *args)


def _select_blocks(score, blk, ns):
    rank = jnp.zeros(score.shape, jnp.int32)
    for j in range(ns):
        sj = score[j:j + 1, :]
        beats = (sj > score) | ((sj == score) & (j < blk))
        rank = rank + beats.astype(jnp.int32)
    return (rank < min(N_SEL, ns)) & (score > 0.5 * NEG)


def _attn_prompt_kernel(q_ref, qr_ref, ck_ref, cv_ref, ks_ref, vs_ref, kw_ref, vw_ref, gate_ref, mapT_ref,
                        expand_ref, o_ref, part_ref, sbias_ref, wbias_ref, m_ref, l_ref, acc_ref, *, tq, kc, nc, ns, R):
    qi = pl.program_id(2)
    q0 = qi * tq
    pos = q0 + lax.broadcasted_iota(jnp.int32, (tq, 1), 0)
    ncp = ck_ref.shape[2]

    ck = ck_ref[0, 0]
    cv = cv_ref[0, 0]
    cidx = lax.broadcasted_iota(jnp.int32, (1, ncp), 1)
    ok_c = (cidx * CMP_STRIDE + CMP_BLOCK - 1 <= pos) & (cidx < nc)
    imp = jnp.zeros((tq, ncp), F32)
    for r in range(R):
        qh = q_ref[0, :, r * HEAD_DIM:(r + 1) * HEAD_DIM]
        s = jnp.where(ok_c, _dot_nt(qh, ck) * SCALE, NEG)
        e = jnp.exp(s - jnp.max(s, axis=1, keepdims=True))
        p = jnp.where(ok_c, e * (1.0 / jnp.sum(e, axis=1, keepdims=True)), 0.0)
        imp = imp + p
        part_ref[r] = gate_ref[0, :, r * N_BRANCH:r * N_BRANCH + 1] * _dot(p.astype(BF16), cv)

    nsp = mapT_ref.shape[0]
    p_slc = lax.dot_general(mapT_ref[...], imp, (((1,), (1,)), ((), ())), precision=lax.Precision.HIGHEST,
                            preferred_element_type=F32)
    blk = lax.broadcasted_iota(jnp.int32, (nsp, tq), 0)
    pos_l = q0 + lax.broadcasted_iota(jnp.int32, (nsp, tq), 1)
    cur = pos_l // SEL_BLOCK
    vis = blk * SEL_BLOCK <= pos_l
    forced = vis & ((blk == 0) | (blk == cur) | (blk == cur - 1))
    score = jnp.where(forced, FORCE, jnp.where(vis, p_slc, NEG))
    score = jnp.where(blk < ns, score, PAD_SCORE)
    sel = _select_blocks(score, blk, ns).astype(BF16)

    c_hi = (q0 + tq) // kc
    col = lax.broadcasted_iota(jnp.int32, (1, kc), 1)
    for c in range(sbias_ref.shape[0]):
        @pl.when(c < c_hi)
        def _(c=c):
            ok = (_dot_tn(sel, expand_ref[:, c * kc:(c + 1) * kc]) > 0.5) & (c * kc + col <= pos)
            sbias_ref[c] = jnp.where(ok, 0.0, NEG)
    nwc = wbias_ref.shape[0]
    c_w0 = c_hi - nwc
    for d in range(nwc):
        kpos = (c_w0 + d) * kc + col
        wbias_ref[d] = jnp.where((kpos <= pos) & (kpos > pos - WINDOW), 0.0, NEG)

    def fold(t):
        return [t[:, i * LANE:(i + 1) * LANE] for i in range(kc // LANE)]

    def branch(k_ref, v_ref, c_lo, bias_of, gate_col):
        def logits(r, c, bias):
            k = k_ref[0, pl.ds(pl.multiple_of(c * kc, kc), kc), :]
            return _dot_nt(qr_ref[0, :, r * HEAD_DIM:(r + 1) * HEAD_DIM], k) * SCALE + bias

        m_ref[...] = jnp.full(m_ref.shape, NEG, F32)
        l_ref[...] = jnp.zeros(l_ref.shape, F32)
        acc_ref[...] = jnp.zeros(acc_ref.shape, F32)

        def max_body(c, carry):
            bias = bias_of(c)
            for r in range(R):
                mx = m_ref[r]
                for part in fold(logits(r, c, bias)):
                    mx = jnp.maximum(mx, part)
                m_ref[r] = mx
            return carry

        lax.fori_loop(c_lo, c_hi, max_body, 0)
        for r in range(R):
            m_ref[r] = jnp.broadcast_to(jnp.max(m_ref[r], axis=1, keepdims=True), (tq, LANE))

        def sum_body(c, carry):
            bias = bias_of(c)
            v = v_ref[0, pl.ds(pl.multiple_of(c * kc, kc), kc), :]
            for r in range(R):
                t = logits(r, c, bias)
                m = m_ref[r]
                ps = [jnp.exp(part - m) for part in fold(t)]
                acc_ref[r] += _dot(jnp.concatenate(ps, axis=1).astype(BF16), v)
                lsum = l_ref[r]
                for part in ps:
                    lsum = lsum + part
                l_ref[r] = lsum
            return carry

        lax.fori_loop(c_lo, c_hi, sum_body, 0)
        for r in range(R):
            gate = gate_ref[0, :, r * N_BRANCH + gate_col:r * N_BRANCH + gate_col + 1]
            inv_l = 1.0 / jnp.sum(l_ref[r], axis=1, keepdims=True)
            part_ref[r] += gate * (acc_ref[r] * inv_l)

    branch(ks_ref, vs_ref, 0, lambda c: sbias_ref[c], 1)
    branch(kw_ref, vw_ref, jnp.maximum(c_w0, 0), lambda c: wbias_ref[c - c_w0], 2)
    for r in range(R):
        o_ref[0, :, r * HEAD_DIM:(r + 1) * HEAD_DIM] = part_ref[r].astype(o_ref.dtype)


def _overlap_map(ncp, nsp, ns):
    ratio = CMP_BLOCK // CMP_STRIDE
    per_sel = SEL_BLOCK // CMP_STRIDE
    m = np.zeros((ncp, nsp), np.float32)
    for b in range(ns):
        for mm_ in range(per_sel):
            for n in range(ratio):
                j = per_sel * b + mm_ - n
                if 0 <= j < ncp:
                    m[j, b] += 1.0
    return m


def attn_prompt(q, qr, ck, cv, ks, vs, kw, vw, gates, *, nc):
    B, T, HD = q.shape
    G = N_KV_HEADS
    R = HD // HEAD_DIM // G
    tq = min(T, 256)
    kc = tq
    ns = T // SEL_BLOCK
    nsp = -(-ns // 8) * 8
    ncp = ck.shape[2]
    mapT = jnp.asarray(_overlap_map(ncp, nsp, ns).T)
    expand = jnp.asarray((np.arange(T)[None, :] // SEL_BLOCK == np.arange(nsp)[:, None]).astype(np.float32), BF16)
    q_spec = pl.BlockSpec((1, tq, R * HEAD_DIM), lambda b, g, i: (b, i, g))
    c_spec = pl.BlockSpec((1, 1, ncp, HEAD_DIM), lambda b, g, i: (b, g, 0, 0))
    kv_spec = pl.BlockSpec((1, T, HEAD_DIM), lambda b, g, i: (b, 0, g))
    return pl.pallas_call(
        functools.partial(_attn_prompt_kernel, tq=tq, kc=kc, nc=nc, ns=ns, R=R),
        grid=(B, G, T // tq),
        in_specs=[q_spec, q_spec, c_spec, c_spec, kv_spec, kv_spec, kv_spec, kv_spec,
                  pl.BlockSpec((1, tq, LANE), lambda b, g, i: (b, i, g)),
                  pl.BlockSpec((nsp, ncp), lambda b, g, i: (0, 0)),
                  pl.BlockSpec((nsp, T), lambda b, g, i: (0, 0))],
        out_specs=q_spec,
        out_shape=jax.ShapeDtypeStruct((B, T, HD), BF16),
        scratch_shapes=[pltpu.VMEM((R, tq, HEAD_DIM), F32), pltpu.VMEM((T // kc, tq, kc), F32),
                        pltpu.VMEM((min(WINDOW, T) // kc + tq // kc, tq, kc), F32)]
                       + [pltpu.VMEM((R, tq, HEAD_DIM), F32)] * 3,
        compiler_params=_params(("parallel", "parallel", "arbitrary")),
        name="attn_prompt",
    )(q, qr, ck, cv, ks, vs, kw, vw, gates, mapT, expand)


def _attn_dec_dense_kernel(q_ref, qr_ref, ck_ref, cv_ref, kw_ref, vw_ref, map_ref, ocmp_ref, owin_ref, ids_ref,
                           *, nc, ns, pos):
    G, R = q_ref.shape[1], q_ref.shape[2]
    ncp = ck_ref.shape[2]
    nsl = map_ref.shape[1]
    cidx = lax.broadcasted_iota(jnp.int32, (1, ncp), 1)
    ok_c = (cidx * CMP_STRIDE + CMP_BLOCK - 1 <= pos) & (cidx < nc)
    blk_l = lax.broadcasted_iota(jnp.int32, (1, nsl), 1)
    cur = pos // SEL_BLOCK
    vis = blk_l * SEL_BLOCK <= pos
    forced = vis & ((blk_l == 0) | (blk_l == cur) | (blk_l == cur - 1))
    ii = lax.broadcasted_iota(jnp.int32, (nsl, nsl), 0)
    jj = lax.broadcasted_iota(jnp.int32, (nsl, nsl), 1)
    slot = lax.broadcasted_iota(jnp.int32, (nsl, LANE), 1).astype(F32)
    blk_s = lax.broadcasted_iota(jnp.int32, (nsl, LANE), 0).astype(F32)
    for g in range(G):
        s = jnp.where(ok_c, _dot_nt(q_ref[0, g], ck_ref[0, g]) * SCALE, NEG)
        e = jnp.exp(s - jnp.max(s, axis=1, keepdims=True))
        p = jnp.where(ok_c, e * (1.0 / jnp.sum(e, axis=1, keepdims=True)), 0.0)
        ocmp_ref[0, g] = _dot(p.astype(BF16), cv_ref[0, g])
        imp = jnp.broadcast_to(jnp.sum(p, axis=0, keepdims=True), (R, ncp))
        p_slc = jnp.dot(imp, map_ref[...], precision=lax.Precision.HIGHEST, preferred_element_type=F32)[0:1, :]
        score_l = jnp.where(forced, FORCE, jnp.where(vis, p_slc, NEG))
        score_l = jnp.where(blk_l < ns, score_l, PAD_SCORE)
        score_s = jnp.sum(jnp.where(ii == jj, score_l, 0.0), axis=1, keepdims=True)
        beats = (score_l > score_s) | ((score_l == score_s) & (jj < ii))
        rank = jnp.sum(beats.astype(F32), axis=1, keepdims=True)
        ids = jnp.sum(jnp.where(rank == slot, blk_s, 0.0), axis=0, keepdims=True)
        ids_ref[0, g] = ids[:, :N_SEL].astype(jnp.int32)
        kw = kw_ref[0, :, g, :].astype(BF16)
        vw = vw_ref[0, :, g, :].astype(BF16)
        s = _dot_nt(qr_ref[0, g], kw) * SCALE
        e = jnp.exp(s - jnp.max(s, axis=1, keepdims=True))
        p = e * (1.0 / jnp.sum(e, axis=1, keepdims=True))
        owin_ref[0, g] = _dot(p.astype(BF16), vw)


def attn_dec_dense(q, qr, ck, cv, kw, vw, *, nc, ns, pos):
    B, G, R, _ = q.shape
    ncp = ck.shape[2]
    wb = kw.shape[1]
    nsl = -(-ns // LANE) * LANE
    omap = jnp.asarray(_overlap_map(ncp, nsl, ns))
    q_spec = pl.BlockSpec((1, G, R, HEAD_DIM), lambda b: (b, 0, 0, 0))
    c_spec = pl.BlockSpec((1, G, ncp, HEAD_DIM), lambda b: (b, 0, 0, 0))
    w_spec = pl.BlockSpec((1, wb, G, HEAD_DIM), lambda b: (b, 0, 0, 0))
    return pl.pallas_call(
        functools.partial(_attn_dec_dense_kernel, nc=nc, ns=ns, pos=pos),
        grid=(B,),
        in_specs=[q_spec, q_spec, c_spec, c_spec, w_spec, w_spec, pl.BlockSpec((ncp, nsl), lambda b: (0, 0))],
        out_specs=[q_spec, q_spec, pl.BlockSpec((1, G, 1, N_SEL), lambda b: (b, 0, 0, 0))],
        out_shape=[jax.ShapeDtypeStruct((B, G, R, HEAD_DIM), F32), jax.ShapeDtypeStruct((B, G, R, HEAD_DIM), F32),
                   jax.ShapeDtypeStruct((B, G, 1, N_SEL), jnp.int32)],
        compiler_params=_params(("parallel",)),
        name="attn_dec_dense",
    )(q, qr, ck, cv, kw, vw, omap)


def _attn_dec_sel_kernel(pt_ref, ids_ref, qr_ref, *refs, n_past, pos):
    del pt_ref
    G = qr_ref.shape[1]
    kc_refs, vc_refs = refs[:G], refs[G:2 * G]
    kn_ref, vn_ref, ocmp_ref, owin_ref, gate_ref, o_ref, m_ref, l_ref, acc_ref = refs[2 * G:]
    b, n = pl.program_id(0), pl.program_id(1)

    @pl.when(n == 0)
    def _():
        m_ref[...] = jnp.full(m_ref.shape, NEG, F32)
        l_ref[...] = jnp.zeros(l_ref.shape, F32)
        acc_ref[...] = jnp.zeros(acc_ref.shape, F32)

    row = lax.broadcasted_iota(jnp.int32, (SEL_BLOCK, 1), 0)
    for g in range(G):
        bid = ids_ref[(b * G + g) * N_SEL + n]
        is_new = bid >= n_past
        first = (row == 0) & (bid == n_past)
        sl = slice(g * HEAD_DIM, (g + 1) * HEAD_DIM)
        k = jnp.where(is_new, jnp.where(first, kn_ref[0, :, sl], 0.0), kc_refs[g][0, :, g, :]).astype(BF16)
        v = jnp.where(is_new, jnp.where(first, vn_ref[0, :, sl], 0.0), vc_refs[g][0, :, g, :]).astype(BF16)
        kpos = bid * SEL_BLOCK + lax.broadcasted_iota(jnp.int32, (1, SEL_BLOCK), 1)
        ok = kpos <= pos
        s = jnp.where(ok, _dot_nt(qr_ref[0, g], k) * SCALE, NEG)
        m_prev = m_ref[g]
        m_new = jnp.maximum(m_prev, jnp.max(s, axis=1, keepdims=True))
        alpha = jnp.exp(m_prev - m_new)
        p = jnp.where(ok, jnp.exp(s - m_new), 0.0)
        l_ref[g] = alpha * l_ref[g] + jnp.sum(p, axis=1, keepdims=True)
        acc_ref[g] = alpha * acc_ref[g] + _dot(p.astype(BF16), v)
        m_ref[g] = m_new

    @pl.when(n == N_SEL - 1)
    def _():
        for g in range(G):
            gt = gate_ref[0, g]
            o_sel = acc_ref[g] * (1.0 / l_ref[g])
            o_ref[0, g] = gt[:, 0:1] * ocmp_ref[0, g] + gt[:, 1:2] * o_sel + gt[:, 2:3] * owin_ref[0, g]


def attn_dec_sel(table, ids, qr, k_cache, v_cache, k_new, v_new, ocmp, owin, gates, *, pos):
    B, G, R, _ = qr.shape
    page = k_cache.shape[1]
    bpp = page // SEL_BLOCK
    n_past = table.shape[1] * bpp
    kc = k_cache.reshape(k_cache.shape[0] * bpp, SEL_BLOCK, G, HEAD_DIM)
    vc = v_cache.reshape(v_cache.shape[0] * bpp, SEL_BLOCK, G, HEAD_DIM)

    def cache_spec(g):
        def index(b, n, pt, ids_):
            bid = jnp.minimum(ids_[(b * G + g) * N_SEL + n], n_past - 1)
            return (pt[b, bid // bpp] * bpp + bid % bpp, 0, 0, 0)
        return pl.BlockSpec((1, SEL_BLOCK, G, HEAD_DIM), index)

    q_spec = pl.BlockSpec((1, G, R, HEAD_DIM), lambda b, n, pt, ids_: (b, 0, 0, 0))
    n_spec = pl.BlockSpec((1, 1, G * HEAD_DIM), lambda b, n, pt, ids_: (b, 0, 0))
    g_spec = pl.BlockSpec((1, G, R, LANE), lambda b, n, pt, ids_: (b, 0, 0, 0))
    c_specs = [cache_spec(g) for g in range(G)]
    return pl.pallas_call(
        functools.partial(_attn_dec_sel_kernel, n_past=n_past, pos=pos),
        grid_spec=pltpu.PrefetchScalarGridSpec(
            num_scalar_prefetch=2,
            grid=(B, N_SEL),
            in_specs=[q_spec] + c_specs + c_specs + [n_spec, n_spec, q_spec, q_spec, g_spec],
            out_specs=q_spec,
            scratch_shapes=[pltpu.VMEM((G, R, 1), F32), pltpu.VMEM((G, R, 1), F32),
                            pltpu.VMEM((G, R, HEAD_DIM), F32)],
        ),
        out_shape=jax.ShapeDtypeStruct((B, G, R, HEAD_DIM), F32),
        compiler_params=_params(("parallel", "arbitrary")),
        name="attn_dec_sel",
    )(table, ids.reshape(-1), qr, *([kc] * G), *([vc] * G), k_new, v_new, ocmp, owin, gates)


def _rope_tables(pos):
    half = HEAD_DIM // 2
    inv = ROPE_THETA ** (-jnp.arange(half, dtype=F32) / half)
    ang = pos.astype(F32)[:, None] * inv[None, :]
    cos, sin = jnp.cos(ang), jnp.sin(ang)
    return jnp.concatenate([cos, cos], axis=1), jnp.concatenate([-sin, sin], axis=1)


def _prep_weights(W):
    D = W['w_kv'].shape[0]
    H = D // HEAD_DIM
    R = H // N_KV_HEADS
    pg = D // len(POOL_WINDOWS)
    half = CMP_STRIDE * HEAD_DIM
    P = {}
    P['w_pool'] = W['w_pool'].astype(BF16).reshape(-1, len(POOL_WINDOWS) * pg, pg)
    kv_gain = jnp.ones((W['w_kv'].shape[1], KV_W), F32)
    kv_gain = kv_gain.at[2].set(jnp.tile(W['g_k_sel'], N_KV_HEADS)).at[4].set(jnp.tile(W['g_k_win'], N_KV_HEADS))
    P['kv_gain'] = kv_gain
    for t in ('k', 'v'):
        w1 = W['w_cmp_%s1' % t].astype(BF16)
        P['w_cmp_%s1' % t] = w1
        P['w_cmp_%scat' % t] = jnp.concatenate([w1[:half], w1[half:]], axis=1)
        P['w_cmp_%s2' % t] = W['w_cmp_%s2' % t].astype(BF16)
        P['pe_%s' % t] = W['pe_cmp_%s' % t].astype(BF16).reshape(1, -1)
    n_b = W['w_qg'].shape[0]
    wg = W['w_qg'][:, :, H * HEAD_DIM:].astype(BF16).reshape(n_b, D, N_KV_HEADS, R * N_BRANCH)
    wg = jnp.pad(wg, ((0, 0), (0, 0), (0, 0), (0, LANE - R * N_BRANCH)))
    P['w_gate'] = wg.reshape(n_b, D, N_KV_HEADS * LANE)
    P['w_ple'] = W['w_ple'].astype(BF16)
    return P


def _dense(x, wname, layer, W, P, *, N, epilogue, specs, extras=(), out_dtypes, emit=False, name):
    M, K = x.shape
    shapes = [d if isinstance(d, jax.ShapeDtypeStruct) else jax.ShapeDtypeStruct((M, N), d) for d in out_dtypes]
    key = (wname, layer)
    tmx = 1024 if M >= 1024 else M
    if key in P:
        grid, tm, tn, tk = _tiles(M, N, K, tmx, 1024, 4096)
        extra_specs, out_specs = specs(tm, tn)
        return mm(x, P[key], grid=grid, tm=tm, tn=tn, tk=tk, epilogue=epilogue, extras=extras,
                  extra_specs=extra_specs, out_shapes=shapes, out_specs=out_specs, name=name)
    grid, tm, tn, tk = _tiles(M, N, K, tmx, 512, K if K <= 4096 else 2048)
    extra_specs, out_specs = specs(tm, tn)
    w = W[wname]
    outs = list(mm_ws(x, w, layer=layer if w.ndim == 3 else None, emit=emit, grid=grid, tm=tm, tn=tn, tk=tk,
                      epilogue=epilogue, extras=extras, extra_specs=extra_specs, out_shapes=shapes,
                      out_specs=out_specs, name=name))
    if emit:
        P[key] = outs.pop()
    return outs


def _ffn_ple(h, p_l, layer, W, P):
    M, D = h.shape
    F = W['w_up'].shape[2]
    emit = M >= 1024
    mn = lambda tm, tn: ([], [_mn_spec(tm, tn)])
    res = lambda tm, tn: ([_mn_spec(tm, tn)], [_mn_spec(tm, tn)])
    (m_,) = rmsnorm_rows(h, [W['g_ffn'][layer]], [BF16])
    (u,) = _dense(m_, 'w_up', layer, W, P, N=F, epilogue=lambda acc, rows: (jnp.square(jnp.maximum(acc, 0.0)),),
                  specs=mn, out_dtypes=[BF16], emit=emit, name="ffn_up")
    (h,) = _dense(u, 'w_down', layer, W, P, N=D, epilogue=lambda acc, rows, r: (r[rows, :] + acc,), specs=res,
                  extras=[h], out_dtypes=[F32], emit=emit, name="ffn_down")
    (e_,) = rmsnorm_rows(h, [W['g_ple'][layer]], [BF16])
    ple_dim = p_l.shape[1]
    ple_specs = lambda tm, tn: ([_mn_spec(tm, tn), pl.BlockSpec((tm, ple_dim), lambda i, j, k: (i, 0)),
                                 pl.BlockSpec((ple_dim, tn), lambda i, j, k: (0, j))], [_mn_spec(tm, tn)])
    (h,) = _dense(e_, 'w_ple_gate', layer, W, P, N=D,
                  epilogue=lambda acc, rows, r, pp, wp: (r[rows, :] + _dot(pp[rows, :], wp[...]) * _sigmoid(acc),),
                  specs=ple_specs, extras=[h, p_l.astype(BF16), P['w_ple'][layer]], out_dtypes=[F32], name="ple")
    return h


def _kv_epilogue(acc, rows, gain_ref, cos_ref, sin_ref):
    j = pl.program_id(0)
    cosf, sinf = cos_ref[rows, :], sin_ref[rows, :]
    heads = []
    for hh in range(N_KV_HEADS):
        sl = slice(hh * HEAD_DIM, (hh + 1) * HEAD_DIM)
        heads.append(_rope(_head_norm(acc[:, sl], gain_ref[0, :, sl]), cosf, sinf))
    out = jnp.where((j == 2) | (j == 4), jnp.concatenate(heads, axis=1), acc)
    return out, out


def _q_epilogue(acc, rows, gq_ref, cos_ref, sin_ref):
    cosf, sinf = cos_ref[rows, :], sin_ref[rows, :]
    qs, qrs = [], []
    for hh in range(acc.shape[1] // HEAD_DIM):
        qn = _head_norm(acc[:, hh * HEAD_DIM:(hh + 1) * HEAD_DIM], gq_ref[...])
        qs.append(qn)
        qrs.append(_rope(qn, cosf, sinf))
    return jnp.concatenate(qs, axis=1), jnp.concatenate(qrs, axis=1)


def _trunk(x, p, pool_prefix, pos0, W, P, attend):
    B, T, D = x.shape
    M = B * T
    tmx = 1024 if M >= 1024 else M
    h = x.reshape(M, D)
    pg = D // len(POOL_WINDOWS)

    (a0,) = rmsnorm_rows(h, [W['g_mix'][0]], [F32])
    a0 = a0.reshape(B, T, D)
    d = pool_diff(a0, pool_prefix[0], pos0).reshape(M, D)
    pool_new = jnp.concatenate([pool_prefix[0], a0], axis=1)[:, -POOL_STATE:][None]
    grid, tm, tn, tk = _tiles(M, D, pg, tmx, pg, pg)
    (h,) = mm(d, P['w_pool'][0], grid=grid, tm=tm, tn=tn, tk=tk,
              x_map=lambda i, j, k: (i, j), w_map=lambda i, j, k: (j, 0),
              epilogue=lambda acc, rows, sc, r: (r[rows, :] + acc * sc[...],),
              extras=[W['pool_scale'][0].reshape(1, D), h],
              extra_specs=[pl.BlockSpec((1, tn), lambda i, j, k: (0, j)), _mn_spec(tm, tn)],
              out_shapes=[jax.ShapeDtypeStruct((M, D), F32)], out_specs=[_mn_spec(tm, tn)], name="pool_mix")
    h = _ffn_ple(h, p[0].reshape(M, -1), 0, W, P)

    hkv, a1 = rmsnorm_rows(h, [W['g_kv'], W['g_mix'][1]], [BF16, BF16])
    pos = pos0 + jnp.tile(jnp.arange(T, dtype=jnp.int32), B)
    cosf, sinf = _rope_tables(pos)
    n_kv = W['w_kv'].shape[1]
    rope_spec = lambda tm: pl.BlockSpec((tm, HEAD_DIM), lambda i, j, k: (i, 0))
    kv_spec = lambda tm: pl.BlockSpec((1, tm, KV_W), lambda i, j, k: (j, i, 0))
    kv_specs = lambda tm, tn: ([pl.BlockSpec((1, 1, KV_W), lambda i, j, k: (j, 0, 0)), rope_spec(tm), rope_spec(tm)],
                               [kv_spec(tm), kv_spec(tm)])
    kv, kv_b = _dense(hkv, 'w_kv2d', None, W, P, N=n_kv * KV_W, epilogue=_kv_epilogue, specs=kv_specs,
                      extras=[P['kv_gain'].reshape(n_kv, 1, KV_W), cosf, sinf],
                      out_dtypes=[jax.ShapeDtypeStruct((n_kv, M, KV_W), F32),
                                  jax.ShapeDtypeStruct((n_kv, M, KV_W), BF16)], name="kv_proj")
    q_specs = lambda tm, tn: ([pl.BlockSpec((1, HEAD_DIM), lambda i, j, k: (0, 0)), rope_spec(tm), rope_spec(tm)],
                              [_mn_spec(tm, tn)] * 2)
    q, qr = _dense(a1, 'w_qg', 0, W, P, N=D, epilogue=_q_epilogue, specs=q_specs,
                   extras=[W['g_q'][0].reshape(1, HEAD_DIM), cosf, sinf], out_dtypes=[BF16, BF16], name="q_proj")
    ng = N_KV_HEADS * LANE
    grid, tm, tn, tk = _tiles(M, ng, D, tmx, ng, D)
    (gates,) = mm(a1, P['w_gate'][0], grid=grid, tm=tm, tn=tn, tk=tk, epilogue=lambda acc, rows: (_sigmoid(acc),),
                  out_shapes=[jax.ShapeDtypeStruct((M, ng), F32)], out_specs=[_mn_spec(tm, tn)], name="gate_proj")

    o, win_state = attend(kv, kv_b, q, qr, gates)

    (h,) = _dense(o, 'w_o', 0, W, P, N=D, epilogue=lambda acc, rows, r: (r[rows, :] + acc,), extras=[h],
                  specs=lambda tm, tn: ([_mn_spec(tm, tn)], [_mn_spec(tm, tn)]), out_dtypes=[F32], name="attn_out")
    h = _ffn_ple(h, p[1].reshape(M, -1), 1, W, P)
    rows = tuple(kv[n].reshape(B, T, N_KV_HEADS, HEAD_DIM) for n in range(4))
    return h.reshape(B, T, D), pool_new, rows, win_state


def _compress(k_pages, v_pages, table, cp, W, P):
    abk, abv = cmp_partials(k_pages, v_pages, table, P['w_cmp_kcat'], P['w_cmp_vcat'], cp)
    gain = W['g_k_cmp'].reshape(1, HEAD_DIM)
    ck = cmp_finish(abk, P['pe_k'], P['w_cmp_k1'], P['w_cmp_k2'], gain, norm=True)
    cv = cmp_finish(abv, P['pe_v'], P['w_cmp_v1'], P['w_cmp_v2'], gain, norm=False)
    return ck, cv


def kernel(x_prompt, x_sample, state_pool, cache_k_cmp, cache_v_cmp, cache_k_sel, cache_v_sel, state_k_win, state_v_win, page_table, p_prompt, p_sample, g_mix, w_pool, pool_scale, g_kv, w_kv, g_k_cmp, g_k_sel, g_k_win, w_cmp_k1, w_cmp_k2, pe_cmp_k, w_cmp_v1, w_cmp_v2, pe_cmp_v, w_qg, g_q, w_o, g_ffn, w_up, w_down, g_ple, w_ple, w_ple_gate):
    W = dict(g_mix=g_mix, w_pool=w_pool, pool_scale=pool_scale, g_kv=g_kv, w_kv=w_kv, g_k_cmp=g_k_cmp,
             g_k_sel=g_k_sel, g_k_win=g_k_win, w_cmp_k1=w_cmp_k1, w_cmp_k2=w_cmp_k2, pe_cmp_k=pe_cmp_k,
             w_cmp_v1=w_cmp_v1, w_cmp_v2=w_cmp_v2, pe_cmp_v=pe_cmp_v, w_qg=w_qg, g_q=g_q, w_o=w_o,
             g_ffn=g_ffn, w_up=w_up, w_down=w_down, g_ple=g_ple, w_ple=w_ple, w_ple_gate=w_ple_gate)
    P = _prep_weights(W)
    W['w_kv2d'] = w_kv.reshape(w_kv.shape[0], -1)
    Bp, Tp, D = x_prompt.shape
    Bs, Ts, _ = x_sample.shape
    assert Ts == 1, "the decode path handles one new token per sequence"
    page = cache_k_cmp.shape[1]
    past_len = page_table.shape[1] * page
    R = D // HEAD_DIM // N_KV_HEADS
    assert Tp % page == 0 and past_len % SEL_BLOCK == 0

    def attend_prompt(kv, kv_b, q, qr, gates):
        ppb = Tp // page
        table = jnp.arange(Bp * ppb, dtype=jnp.int32).reshape(Bp, ppb)
        pages = lambda a: a.reshape(-1, page, N_KV_HEADS, HEAD_DIM)
        ck, cv = _compress(pages(kv[0]), pages(kv[1]), table, ppb, W, P)
        nc = Tp // CMP_STRIDE - CMP_BLOCK // CMP_STRIDE + 1
        seq = lambda a: a.reshape(Bp, Tp, -1)
        o = attn_prompt(seq(q), seq(qr), ck, cv, seq(kv_b[2]), seq(kv_b[3]), seq(kv_b[4]), seq(kv_b[5]),
                        seq(gates), nc=nc)
        nw = min(WINDOW, Tp)
        win = tuple(kv[n].reshape(Bp, Tp, N_KV_HEADS, HEAD_DIM)[:, -nw:] for n in (4, 5))
        return o.reshape(Bp * Tp, D), win

    def attend_sample(kv, kv_b, q, qr, gates):
        del kv_b
        ck, cv = _compress(cache_k_cmp, cache_v_cmp, page_table, min(32, page_table.shape[1]), W, P)
        nc = (past_len - (CMP_BLOCK - 1)) // CMP_STRIDE + 1
        ns = past_len // SEL_BLOCK + 1
        wb = state_k_win.shape[1]
        new_row = lambda a: a.reshape(Bs, 1, N_KV_HEADS, HEAD_DIM)
        kw = jnp.concatenate([state_k_win, new_row(kv[4])], axis=1)[:, -wb:]
        vw = jnp.concatenate([state_v_win, new_row(kv[5])], axis=1)[:, -wb:]
        heads = lambda a: a.reshape(Bs, N_KV_HEADS, R, HEAD_DIM)
        ocmp, owin, ids = attn_dec_dense(heads(q), heads(qr), ck, cv, kw, vw, nc=nc, ns=ns, pos=past_len)
        gt = gates.reshape(Bs, N_KV_HEADS, LANE)[:, :, :R * N_BRANCH].reshape(Bs, N_KV_HEADS, R, N_BRANCH)
        gt = jnp.pad(gt, ((0, 0), (0, 0), (0, 0), (0, LANE - N_BRANCH)))
        o = attn_dec_sel(page_table, ids, heads(qr), cache_k_sel, cache_v_sel,
                         kv[2][:, None], kv[3][:, None], ocmp, owin, gt, pos=past_len)
        return o.reshape(Bs, D).astype(BF16), (kw, vw)

    pool_zero = jnp.zeros((state_pool.shape[0], Bp, POOL_STATE, D), x_prompt.dtype)
    y_p, pool_p, rows_p, win_p = _trunk(x_prompt, p_prompt, pool_zero, 0, W, P, attend_prompt)
    y_s, pool_s, rows_s, win_s = _trunk(x_sample, p_sample, state_pool, past_len, W, P, attend_sample)
    return (y_p, y_s, pool_p, pool_s, rows_p[0], rows_p[1], rows_p[2], rows_p[3], win_p[0], win_p[1],
            rows_s[0], rows_s[1], rows_s[2], rows_s[3], win_s[0], win_s[1])
```

```python
import functools

import jax
import jax.numpy as jnp
import numpy as np
from jax import lax
from jax.experimental import pallas as pl
from jax.experimental.pallas import tpu as pltpu

F32 = jnp.float32
BF16 = jnp.bfloat16

POOL_WINDOWS = (2, 4, 8, 16)
POOL_STATE = max(POOL_WINDOWS) - 1
POOL_PAD = POOL_STATE + 1
HEAD_DIM = 128
N_KV_HEADS = 4
N_BRANCH = 3
CMP_BLOCK = 32
CMP_STRIDE = 16
CMP_HIDDEN = 2 * HEAD_DIM
SEL_BLOCK = 64
N_SEL = 16
WINDOW = 512
ROPE_THETA = 10000.0
EPS = 1e-6
SCALE = HEAD_DIM ** -0.5
NEG = -1e30
FORCE = 1e9
PAD_SCORE = -3e38
KV_W = N_KV_HEADS * HEAD_DIM
LANE = 128
VMEM_LIMIT = 56 * 1024 * 1024


def _params(sem):
    return pltpu.CompilerParams(dimension_semantics=sem, vmem_limit_bytes=VMEM_LIMIT)


def _sigmoid(x):
    return 1.0 / (1.0 + jnp.exp(-x))


def _dot(a, b):
    return jnp.dot(a, b, preferred_element_type=F32)


def _dot_nt(a, b):
    return lax.dot_general(a, b, (((1,), (1,)), ((), ())), preferred_element_type=F32)


def _dot_tn(a, b):
    return lax.dot_general(a, b, (((0,), (0,)), ((), ())), preferred_element_type=F32)


def _head_norm(x, g):
    return x * lax.rsqrt(jnp.mean(x * x, axis=-1, keepdims=True) + EPS) * g


def _rope(x, cosf, sinf):
    return x * cosf + pltpu.roll(x, HEAD_DIM // 2, 1) * sinf


def _rms_kernel(x_ref, g_ref, *o_refs):
    x = x_ref[...]
    y = x * lax.rsqrt(jnp.mean(x * x, axis=-1, keepdims=True) + EPS)
    for i, o_ref in enumerate(o_refs):
        o_ref[...] = (y * g_ref[i:i + 1, :]).astype(o_ref.dtype)


def rmsnorm_rows(x, gains, dtypes):
    M, D = x.shape
    tm = min(M, 256)
    g = jnp.stack(gains).astype(F32)
    n = len(gains)
    return pl.pallas_call(
        _rms_kernel,
        grid=(M // tm,),
        in_specs=[pl.BlockSpec((tm, D), lambda i: (i, 0)), pl.BlockSpec((n, D), lambda i: (0, 0))],
        out_specs=[pl.BlockSpec((tm, D), lambda i: (i, 0)) for _ in range(n)],
        out_shape=[jax.ShapeDtypeStruct((M, D), dt) for dt in dtypes],
        compiler_params=_params(("parallel",)),
        name="rmsnorm_rows",
    )(x, g)


EPILOGUE_ROWS = 256


def _finish_rows(acc_of, tm, outs, extras, epilogue):
    ch = min(tm, EPILOGUE_ROWS)
    for c in range(tm // ch):
        rows = slice(c * ch, (c + 1) * ch)
        for o_ref, r in zip(outs, epilogue(acc_of(rows), rows, *extras)):
            if len(o_ref.shape) == 4:
                for hh in range(o_ref.shape[2]):
                    o_ref[0, rows, hh, :] = r[:, hh * HEAD_DIM:(hh + 1) * HEAD_DIM].astype(o_ref.dtype)
            elif len(o_ref.shape) == 3:
                o_ref[0, rows, :] = r.astype(o_ref.dtype)
            else:
                o_ref[rows, :] = r.astype(o_ref.dtype)


def _mm_kernel(*refs, nk, n_extra, n_out, epilogue):
    x_ref, w_ref = refs[0], refs[1]
    extras = refs[2:2 + n_extra]
    outs = refs[2 + n_extra:2 + n_extra + n_out]
    tm = x_ref.shape[0]

    if nk == 1:
        _finish_rows(lambda rows: _dot(x_ref[rows, :], w_ref[...]), tm, outs, extras, epilogue)
    else:
        acc_ref = refs[-1]
        k = pl.program_id(2)

        @pl.when(k == 0)
        def _():
            acc_ref[...] = jnp.zeros_like(acc_ref)

        acc_ref[...] += _dot(x_ref[...], w_ref[...])

        @pl.when(k == nk - 1)
        def _():
            _finish_rows(lambda rows: acc_ref[rows, :], tm, outs, extras, epilogue)


def mm(x, w, *, grid, tm, tn, tk, epilogue, extras=(), extra_specs=(), out_shapes, out_specs,
       x_map=None, w_map=None, name):
    nk = grid[2]
    x_map = x_map or (lambda i, j, k: (i, k))
    w_map = w_map or (lambda i, j, k: (k, j))
    kern = functools.partial(_mm_kernel, nk=nk, n_extra=len(extras), n_out=len(out_shapes), epilogue=epilogue)
    return pl.pallas_call(
        kern,
        grid=grid,
        in_specs=[pl.BlockSpec((tm, tk), x_map), pl.BlockSpec((tk, tn), w_map)] + list(extra_specs),
        out_specs=list(out_specs),
        out_shape=list(out_shapes),
        scratch_shapes=[pltpu.VMEM((tm, tn), F32)] if nk > 1 else [],
        compiler_params=_params(("parallel", "parallel", "arbitrary")),
        name=name,
    )(x, w, *extras)


def _mm_ws_kernel(*refs, nk, n_extra, n_out, epilogue, emit):
    x_ref, w_ref = refs[0], refs[1]
    extras = refs[2:2 + n_extra]
    outs = refs[2 + n_extra:2 + n_extra + n_out]
    rest = refs[2 + n_extra + n_out:]
    wb_ref = rest[1 if emit else 0]
    i, k = pl.program_id(1), pl.program_id(2)

    @pl.when(i == 0)
    def _():
        wb = w_ref[...].astype(BF16)
        wb_ref[k] = wb
        if emit:
            rest[0][...] = wb

    tm = x_ref.shape[0]
    if nk == 1:
        _finish_rows(lambda rows: _dot(x_ref[rows, :], wb_ref[0]), tm, outs, extras, epilogue)
    else:
        acc_ref = rest[-1]

        @pl.when(k == 0)
        def _():
            acc_ref[...] = jnp.zeros_like(acc_ref)

        acc_ref[...] += _dot(x_ref[...], wb_ref[k])

        @pl.when(k == nk - 1)
        def _():
            _finish_rows(lambda rows: acc_ref[rows, :], tm, outs, extras, epilogue)


def mm_ws(x, w, *, layer=None, emit=False, grid, tm, tn, tk, epilogue, extras=(), extra_specs=(), out_shapes,
          out_specs, name):
    gm, gn, nk = grid
    swap = lambda f: (lambda j, i, k: f(i, j, k))
    respec = lambda s: pl.BlockSpec(s.block_shape, swap(s.index_map))
    k_once = lambda i, k: jnp.where(i == 0, k, nk - 1)
    if layer is None:
        w_spec = pl.BlockSpec((tk, tn), lambda j, i, k: (k_once(i, k), j))
    else:
        w_spec = pl.BlockSpec((None, tk, tn), lambda j, i, k: (layer, k_once(i, k), j))
    out_shapes, out_specs = list(out_shapes), [respec(s) for s in out_specs]
    if emit:
        out_shapes.append(jax.ShapeDtypeStruct((nk * tk, gn * tn), BF16))
        out_specs.append(pl.BlockSpec((tk, tn), lambda j, i, k: (k_once(i, k), j)))
    kern = functools.partial(_mm_ws_kernel, nk=nk, n_extra=len(extras), n_out=len(out_shapes) - emit,
                             epilogue=epilogue, emit=emit)
    return pl.pallas_call(
        kern,
        grid=(gn, gm, nk),
        in_specs=[pl.BlockSpec((tm, tk), lambda j, i, k: (i, k)), w_spec] + [respec(s) for s in extra_specs],
        out_specs=out_specs,
        out_shape=out_shapes,
        scratch_shapes=[pltpu.VMEM((nk, tk, tn), BF16)] + ([pltpu.VMEM((tm, tn), F32)] if nk > 1 else []),
        compiler_params=_params(("parallel", "arbitrary", "arbitrary")),
        name=name,
    )(x, w, *extras)


def _tiles(M, N, K, tm, tn, tk):
    tm, tn, tk = min(tm, M), min(tn, N), min(tk, K)
    return (M // tm, N // tn, K // tk), tm, tn, tk


def _mn_spec(tm, tn):
    return pl.BlockSpec((tm, tn), lambda i, j, k: (i, j))


def _pool_diff_kernel(a_ref, pre_ref, d_ref, seq_ref, *, T, pos0):
    seq_ref[0:POOL_PAD, :] = pre_ref[0]
    seq_ref[POOL_PAD:POOL_PAD + T, :] = a_ref[0]
    x_t = a_ref[0]
    pos = pos0 + lax.broadcasted_iota(jnp.int32, (T, 1), 0)
    for g, w in enumerate(POOL_WINDOWS):
        @pl.when(pl.program_id(1) == g)
        def _(w=w):
            s = x_t
            for j in range(1, w):
                s = s + seq_ref[POOL_PAD - j:POOL_PAD - j + T, :]
            cnt = jnp.minimum(pos + 1, w).astype(F32)
            d_ref[0] = (s / cnt - x_t).astype(d_ref.dtype)


def pool_diff(a, prefix, pos0):
    B, T, D = a.shape
    pg = D // len(POOL_WINDOWS)
    tc = min(pg, 512)
    cpg = pg // tc
    pre = jnp.concatenate([jnp.zeros((B, 1, D), F32), prefix], axis=1)
    return pl.pallas_call(
        functools.partial(_pool_diff_kernel, T=T, pos0=pos0),
        grid=(B, len(POOL_WINDOWS), cpg),
        in_specs=[pl.BlockSpec((1, T, tc), lambda b, g, c: (b, 0, g * cpg + c)),
                  pl.BlockSpec((1, POOL_PAD, tc), lambda b, g, c: (b, 0, g * cpg + c))],
        out_specs=pl.BlockSpec((1, T, tc), lambda b, g, c: (b, 0, g * cpg + c)),
        out_shape=jax.ShapeDtypeStruct((B, T, D), BF16),
        scratch_shapes=[pltpu.VMEM((POOL_PAD + T, tc), F32)],
        compiler_params=_params(("parallel", "parallel", "parallel")),
        name="pool_diff",
    )(a, pre)


PAGES_PER_STEP = 8
CMP_FINISH_ROWS = 512


def _cmp_partial_kernel(pt_ref, *refs, steps, pps):
    del pt_ref
    k_pages, v_pages = refs[:pps], refs[pps:2 * pps]
    wk_ref, wv_ref, abk_ref, abv_ref, xk_ref, xv_ref = refs[2 * pps:]
    p = pl.program_id(2)
    rows_per_page = (k_pages[0].shape[1] // CMP_STRIDE) * N_KV_HEADS
    for pages, x_ref in ((k_pages, xk_ref), (v_pages, xv_ref)):
        for q, page_ref in enumerate(pages):
            for sb in range(page_ref.shape[1] // CMP_STRIDE):
                row = q * rows_per_page + sb * N_KV_HEADS
                for r in range(CMP_STRIDE):
                    x_ref[p, row:row + N_KV_HEADS, r * HEAD_DIM:(r + 1) * HEAD_DIM] = page_ref[0, sb * CMP_STRIDE + r]

    @pl.when(p == steps - 1)
    def _():
        for x_ref, w_ref, ab_ref in ((xk_ref, wk_ref, abk_ref), (xv_ref, wv_ref, abv_ref)):
            x = x_ref[...].reshape(ab_ref.shape[1], x_ref.shape[2])
            ab_ref[0] = _dot(x.astype(BF16), w_ref[...])


def cmp_partials(k_pages, v_pages, table, wk_cat, wv_cat, cp):
    B, ppb = table.shape
    page = k_pages.shape[1]
    pps = PAGES_PER_STEP
    rows_per_page = page // CMP_STRIDE * N_KV_HEADS
    nch, steps = ppb // cp, cp // pps
    m = cp * rows_per_page
    kdim = CMP_STRIDE * HEAD_DIM

    def page_spec(q):
        return pl.BlockSpec((1, page, N_KV_HEADS, HEAD_DIM),
                            lambda b, c, p, pt: (pt[b, c * cp + p * pps + q], 0, 0, 0))

    page_specs = [page_spec(q) for q in range(pps)]
    w_spec = pl.BlockSpec((kdim, 2 * CMP_HIDDEN), lambda b, c, p, pt: (0, 0))
    out_spec = pl.BlockSpec((1, m, 2 * CMP_HIDDEN), lambda b, c, p, pt: (b, c, 0))
    out_shape = jax.ShapeDtypeStruct((B, ppb * rows_per_page, 2 * CMP_HIDDEN), F32)
    x_scratch = pltpu.VMEM((steps, pps * rows_per_page, kdim), F32)
    return pl.pallas_call(
        functools.partial(_cmp_partial_kernel, steps=steps, pps=pps),
        grid_spec=pltpu.PrefetchScalarGridSpec(
            num_scalar_prefetch=1,
            grid=(B, nch, steps),
            in_specs=page_specs + page_specs + [w_spec, w_spec],
            out_specs=[out_spec, out_spec],
            scratch_shapes=[x_scratch, x_scratch],
        ),
        out_shape=[out_shape, out_shape],
        compiler_params=_params(("parallel", "parallel", "arbitrary")),
        name="cmp_partials",
    )(table, *([k_pages] * pps), *([v_pages] * pps), wk_cat, wv_cat)


def _cmp_finish_kernel(ab_ref, pe_ref, w1_ref, w2_ref, g_ref, o_ref, out_ref, *, norm):
    n4 = ab_ref.shape[1]
    G = o_ref.shape[1]
    bias = _dot(jnp.broadcast_to(pe_ref[...], (8, pe_ref.shape[1])), w1_ref[...])[0:1, :]
    ch = min(n4, CMP_FINISH_ROWS)
    for c in range(n4 // ch):
        lo, hi = c * ch, (c + 1) * ch
        first = ab_ref[0, lo:hi, :CMP_HIDDEN]
        if hi + G <= n4:
            second = ab_ref[0, lo + G:hi + G, CMP_HIDDEN:]
        else:
            second = jnp.concatenate([ab_ref[0, lo + G:hi, CMP_HIDDEN:], ab_ref[0, hi - G:hi, CMP_HIDDEN:]], axis=0)
        pre = first + second + bias
        out = _dot((pre * _sigmoid(pre)).astype(BF16), w2_ref[...])
        if norm:
            out = _head_norm(out, g_ref[...])
        out_ref[lo:hi, :] = out
    for g in range(G):
        o_ref[0, g] = out_ref[pl.ds(g, n4 // G, stride=G), :].astype(o_ref.dtype)


def cmp_finish(ab, pe, w1, w2, gain, *, norm):
    B, n4, _ = ab.shape
    G = N_KV_HEADS
    full = lambda a: pl.BlockSpec(a.shape, lambda b: (0,) * a.ndim)
    args = (pe, w1, w2, gain)
    return pl.pallas_call(
        functools.partial(_cmp_finish_kernel, norm=norm),
        grid=(B,),
        in_specs=[pl.BlockSpec((1, n4, 2 * CMP_HIDDEN), lambda b: (b, 0, 0))] + [full(a) for a in args],
        out_specs=pl.BlockSpec((1, G, n4 // G, HEAD_DIM), lambda b: (b, 0, 0, 0)),
        out_shape=jax.ShapeDtypeStruct((B, G, n4 // G, HEAD_DIM), BF16),
        scratch_shapes=[pltpu.VMEM((n4, HEAD_DIM), F32)],
        compiler_params=_params(("parallel",)),
        name="cmp_finish",
    )(ab, *args)


def _select_blocks(score, blk, ns):
    rank = jnp.zeros(score.shape, jnp.int32)
    for j in range(ns):
        sj = score[j:j + 1, :]
        beats = (sj > score) | ((sj == score) & (j < blk))
        rank = rank + beats.astype(jnp.int32)
    return (rank < min(N_SEL, ns)) & (score > 0.5 * NEG)


def _attn_prompt_kernel(q_ref, qr_ref, ck_ref, cv_ref, ks_ref, vs_ref, kw_ref, vw_ref, gate_ref, mapT_ref,
                        expand_ref, o_ref, part_ref, sbias_ref, wbias_ref, m_ref, l_ref, acc_ref, *, tq, kc, nc, ns, R):
    qi = pl.program_id(2)
    q0 = qi * tq
    pos = q0 + lax.broadcasted_iota(jnp.int32, (tq, 1), 0)
    ncp = ck_ref.shape[2]

    ck = ck_ref[0, 0]
    cv = cv_ref[0, 0]
    cidx = lax.broadcasted_iota(jnp.int32, (1, ncp), 1)
    ok_c = (cidx * CMP_STRIDE + CMP_BLOCK - 1 <= pos) & (cidx < nc)
    imp = jnp.zeros((tq, ncp), F32)
    for r in range(R):
        qh = q_ref[0, :, r * HEAD_DIM:(r + 1) * HEAD_DIM]
        s = jnp.where(ok_c, _dot_nt(qh, ck) * SCALE, NEG)
        e = jnp.exp(s - jnp.max(s, axis=1, keepdims=True))
        p = jnp.where(ok_c, e * (1.0 / jnp.sum(e, axis=1, keepdims=True)), 0.0)
        imp = imp + p
        part_ref[r] = gate_ref[0, :, r * N_BRANCH:r * N_BRANCH + 1] * _dot(p.astype(BF16), cv)

    nsp = mapT_ref.shape[0]
    p_slc = lax.dot_general(mapT_ref[...], imp, (((1,), (1,)), ((), ())), precision=lax.Precision.HIGHEST,
                            preferred_element_type=F32)
    blk = lax.broadcasted_iota(jnp.int32, (nsp, tq), 0)
    pos_l = q0 + lax.broadcasted_iota(jnp.int32, (nsp, tq), 1)
    cur = pos_l // SEL_BLOCK
    vis = blk * SEL_BLOCK <= pos_l
    forced = vis & ((blk == 0) | (blk == cur) | (blk == cur - 1))
    score = jnp.where(forced, FORCE, jnp.where(vis, p_slc, NEG))
    score = jnp.where(blk < ns, score, PAD_SCORE)
    sel = _select_blocks(score, blk, ns).astype(BF16)

    c_hi = (q0 + tq) // kc
    col = lax.broadcasted_iota(jnp.int32, (1, kc), 1)
    for c in range(sbias_ref.shape[0]):
        @pl.when(c < c_hi)
        def _(c=c):
            ok = (_dot_tn(sel, expand_ref[:, c * kc:(c + 1) * kc]) > 0.5) & (c * kc + col <= pos)
            sbias_ref[c] = jnp.where(ok, 0.0, NEG)
    nwc = wbias_ref.shape[0]
    c_w0 = c_hi - nwc
    for d in range(nwc):
        kpos = (c_w0 + d) * kc + col
        wbias_ref[d] = jnp.where((kpos <= pos) & (kpos > pos - WINDOW), 0.0, NEG)

    def fold(t):
        return [t[:, i * LANE:(i + 1) * LANE] for i in range(kc // LANE)]

    def branch(k_ref, v_ref, c_lo, bias_of, gate_col):
        def logits(r, c, bias):
            k = k_ref[0, pl.ds(pl.multiple_of(c * kc, kc), kc), :]
            return _dot_nt(qr_ref[0, :, r * HEAD_DIM:(r + 1) * HEAD_DIM], k) * SCALE + bias

        m_ref[...] = jnp.full(m_ref.shape, NEG, F32)
        l_ref[...] = jnp.zeros(l_ref.shape, F32)
        acc_ref[...] = jnp.zeros(acc_ref.shape, F32)

        def max_body(c, carry):
            bias = bias_of(c)
            for r in range(R):
                mx = m_ref[r]
                for part in fold(logits(r, c, bias)):
                    mx = jnp.maximum(mx, part)
                m_ref[r] = mx
            return carry

        lax.fori_loop(c_lo, c_hi, max_body, 0)
        for r in range(R):
            m_ref[r] = jnp.broadcast_to(jnp.max(m_ref[r], axis=1, keepdims=True), (tq, LANE))

        def sum_body(c, carry):
            bias = bias_of(c)
            v = v_ref[0, pl.ds(pl.multiple_of(c * kc, kc), kc), :]
            for r in range(R):
                t = logits(r, c, bias)
                m = m_ref[r]
                ps = [jnp.exp(part - m) for part in fold(t)]
                acc_ref[r] += _dot(jnp.concatenate(ps, axis=1).astype(BF16), v)
                lsum = l_ref[r]
                for part in ps:
                    lsum = lsum + part
                l_ref[r] = lsum
            return carry

        lax.fori_loop(c_lo, c_hi, sum_body, 0)
        for r in range(R):
            gate = gate_ref[0, :, r * N_BRANCH + gate_col:r * N_BRANCH + gate_col + 1]
            inv_l = 1.0 / jnp.sum(l_ref[r], axis=1, keepdims=True)
            part_ref[r] += gate * (acc_ref[r] * inv_l)

    branch(ks_ref, vs_ref, 0, lambda c: sbias_ref[c], 1)
    branch(kw_ref, vw_ref, jnp.maximum(c_w0, 0), lambda c: wbias_ref[c - c_w0], 2)
    for r in range(R):
        o_ref[0, :, r * HEAD_DIM:(r + 1) * HEAD_DIM] = part_ref[r].astype(o_ref.dtype)


def _overlap_map(ncp, nsp, ns):
    ratio = CMP_BLOCK // CMP_STRIDE
    per_sel = SEL_BLOCK // CMP_STRIDE
    m = np.zeros((ncp, nsp), np.float32)
    for b in range(ns):
        for mm_ in range(per_sel):
            for n in range(ratio):
                j = per_sel * b + mm_ - n
                if 0 <= j < ncp:
                    m[j, b] += 1.0
    return m


def attn_prompt(q, qr, ck, cv, ks, vs, kw, vw, gates, *, nc):
    B, T, HD = q.shape
    G = N_KV_HEADS
    R = HD // HEAD_DIM // G
    tq = min(T, 256)
    kc = tq
    ns = T // SEL_BLOCK
    nsp = -(-ns // 8) * 8
    ncp = ck.shape[2]
    mapT = jnp.asarray(_overlap_map(ncp, nsp, ns).T)
    expand = jnp.asarray((np.arange(T)[None, :] // SEL_BLOCK == np.arange(nsp)[:, None]).astype(np.float32), BF16)
    q_spec = pl.BlockSpec((1, tq, R * HEAD_DIM), lambda b, g, i: (b, i, g))
    c_spec = pl.BlockSpec((1, 1, ncp, HEAD_DIM), lambda b, g, i: (b, g, 0, 0))
    kv_spec = pl.BlockSpec((1, T, HEAD_DIM), lambda b, g, i: (b, 0, g))
    return pl.pallas_call(
        functools.partial(_attn_prompt_kernel, tq=tq, kc=kc, nc=nc, ns=ns, R=R),
        grid=(B, G, T // tq),
        in_specs=[q_spec, q_spec, c_spec, c_spec, kv_spec, kv_spec, kv_spec, kv_spec,
                  pl.BlockSpec((1, tq, LANE), lambda b, g, i: (b, i, g)),
                  pl.BlockSpec((nsp, ncp), lambda b, g, i: (0, 0)),
                  pl.BlockSpec((nsp, T), lambda b, g, i: (0, 0))],
        out_specs=q_spec,
        out_shape=jax.ShapeDtypeStruct((B, T, HD), BF16),
        scratch_shapes=[pltpu.VMEM((R, tq, HEAD_DIM), F32), pltpu.VMEM((T // kc, tq, kc), F32),
                        pltpu.VMEM((min(WINDOW, T) // kc + tq // kc, tq, kc), F32)]
                       + [pltpu.VMEM((R, tq, HEAD_DIM), F32)] * 3,
        compiler_params=_params(("parallel", "parallel", "arbitrary")),
        name="attn_prompt",
    )(q, qr, ck, cv, ks, vs, kw, vw, gates, mapT, expand)


def _attn_dec_dense_kernel(q_ref, qr_ref, ck_ref, cv_ref, kw_ref, vw_ref, map_ref, ocmp_ref, owin_ref, ids_ref,
                           *, nc, ns, pos):
    G, R = q_ref.shape[1], q_ref.shape[2]
    ncp = ck_ref.shape[2]
    nsl = map_ref.shape[1]
    cidx = lax.broadcasted_iota(jnp.int32, (1, ncp), 1)
    ok_c = (cidx * CMP_STRIDE + CMP_BLOCK - 1 <= pos) & (cidx < nc)
    blk_l = lax.broadcasted_iota(jnp.int32, (1, nsl), 1)
    cur = pos // SEL_BLOCK
    vis = blk_l * SEL_BLOCK <= pos
    forced = vis & ((blk_l == 0) | (blk_l == cur) | (blk_l == cur - 1))
    ii = lax.broadcasted_iota(jnp.int32, (nsl, nsl), 0)
    jj = lax.broadcasted_iota(jnp.int32, (nsl, nsl), 1)
    slot = lax.broadcasted_iota(jnp.int32, (nsl, LANE), 1).astype(F32)
    blk_s = lax.broadcasted_iota(jnp.int32, (nsl, LANE), 0).astype(F32)
    for g in range(G):
        s = jnp.where(ok_c, _dot_nt(q_ref[0, g], ck_ref[0, g]) * SCALE, NEG)
        e = jnp.exp(s - jnp.max(s, axis=1, keepdims=True))
        p = jnp.where(ok_c, e * (1.0 / jnp.sum(e, axis=1, keepdims=True)), 0.0)
        ocmp_ref[0, g] = _dot(p.astype(BF16), cv_ref[0, g])
        imp = jnp.broadcast_to(jnp.sum(p, axis=0, keepdims=True), (R, ncp))
        p_slc = jnp.dot(imp, map_ref[...], precision=lax.Precision.HIGHEST, preferred_element_type=F32)[0:1, :]
        score_l = jnp.where(forced, FORCE, jnp.where(vis, p_slc, NEG))
        score_l = jnp.where(blk_l < ns, score_l, PAD_SCORE)
        score_s = jnp.sum(jnp.where(ii == jj, score_l, 0.0), axis=1, keepdims=True)
        beats = (score_l > score_s) | ((score_l == score_s) & (jj < ii))
        rank = jnp.sum(beats.astype(F32), axis=1, keepdims=True)
        ids = jnp.sum(jnp.where(rank == slot, blk_s, 0.0), axis=0, keepdims=True)
        ids_ref[0, g] = ids[:, :N_SEL].astype(jnp.int32)
        kw = kw_ref[0, :, g, :].astype(BF16)
        vw = vw_ref[0, :, g, :].astype(BF16)
        s = _dot_nt(qr_ref[0, g], kw) * SCALE
        e = jnp.exp(s - jnp.max(s, axis=1, keepdims=True))
        p = e * (1.0 / jnp.sum(e, axis=1, keepdims=True))
        owin_ref[0, g] = _dot(p.astype(BF16), vw)


def attn_dec_dense(q, qr, ck, cv, kw, vw, *, nc, ns, pos):
    B, G, R, _ = q.shape
    ncp = ck.shape[2]
    wb = kw.shape[1]
    nsl = -(-ns // LANE) * LANE
    omap = jnp.asarray(_overlap_map(ncp, nsl, ns))
    q_spec = pl.BlockSpec((1, G, R, HEAD_DIM), lambda b: (b, 0, 0, 0))
    c_spec = pl.BlockSpec((1, G, ncp, HEAD_DIM), lambda b: (b, 0, 0, 0))
    w_spec = pl.BlockSpec((1, wb, G, HEAD_DIM), lambda b: (b, 0, 0, 0))
    return pl.pallas_call(
        functools.partial(_attn_dec_dense_kernel, nc=nc, ns=ns, pos=pos),
        grid=(B,),
        in_specs=[q_spec, q_spec, c_spec, c_spec, w_spec, w_spec, pl.BlockSpec((ncp, nsl), lambda b: (0, 0))],
        out_specs=[q_spec, q_spec, pl.BlockSpec((1, G, 1, N_SEL), lambda b: (b, 0, 0, 0))],
        out_shape=[jax.ShapeDtypeStruct((B, G, R, HEAD_DIM), F32), jax.ShapeDtypeStruct((B, G, R, HEAD_DIM), F32),
                   jax.ShapeDtypeStruct((B, G, 1, N_SEL), jnp.int32)],
        compiler_params=_params(("parallel",)),
        name="attn_dec_dense",
    )(q, qr, ck, cv, kw, vw, omap)


def _attn_dec_sel_kernel(pt_ref, ids_ref, qr_ref, *refs, n_past, pos):
    del pt_ref
    G = qr_ref.shape[1]
    kc_refs, vc_refs = refs[:G], refs[G:2 * G]
    kn_ref, vn_ref, ocmp_ref, owin_ref, gate_ref, o_ref, m_ref, l_ref, acc_ref = refs[2 * G:]
    b, n = pl.program_id(0), pl.program_id(1)

    @pl.when(n == 0)
    def _():
        m_ref[...] = jnp.full(m_ref.shape, NEG, F32)
        l_ref[...] = jnp.zeros(l_ref.shape, F32)
        acc_ref[...] = jnp.zeros(acc_ref.shape, F32)

    row = lax.broadcasted_iota(jnp.int32, (SEL_BLOCK, 1), 0)
    for g in range(G):
        bid = ids_ref[(b * G + g) * N_SEL + n]
        is_new = bid >= n_past
        first = (row == 0) & (bid == n_past)
        sl = slice(g * HEAD_DIM, (g + 1) * HEAD_DIM)
        k = jnp.where(is_new, jnp.where(first, kn_ref[0, :, sl], 0.0), kc_refs[g][0, :, g, :]).astype(BF16)
        v = jnp.where(is_new, jnp.where(first, vn_ref[0, :, sl], 0.0), vc_refs[g][0, :, g, :]).astype(BF16)
        kpos = bid * SEL_BLOCK + lax.broadcasted_iota(jnp.int32, (1, SEL_BLOCK), 1)
        ok = kpos <= pos
        s = jnp.where(ok, _dot_nt(qr_ref[0, g], k) * SCALE, NEG)
        m_prev = m_ref[g]
        m_new = jnp.maximum(m_prev, jnp.max(s, axis=1, keepdims=True))
        alpha = jnp.exp(m_prev - m_new)
        p = jnp.where(ok, jnp.exp(s - m_new), 0.0)
        l_ref[g] = alpha * l_ref[g] + jnp.sum(p, axis=1, keepdims=True)
        acc_ref[g] = alpha * acc_ref[g] + _dot(p.astype(BF16), v)
        m_ref[g] = m_new

    @pl.when(n == N_SEL - 1)
    def _():
        for g in range(G):
            gt = gate_ref[0, g]
            o_sel = acc_ref[g] * (1.0 / l_ref[g])
            o_ref[0, g] = gt[:, 0:1] * ocmp_ref[0, g] + gt[:, 1:2] * o_sel + gt[:, 2:3] * owin_ref[0, g]


def attn_dec_sel(table, ids, qr, k_cache, v_cache, k_new, v_new, ocmp, owin, gates, *, pos):
    B, G, R, _ = qr.shape
    page = k_cache.shape[1]
    bpp = page // SEL_BLOCK
    n_past = table.shape[1] * bpp
    kc = k_cache.reshape(k_cache.shape[0] * bpp, SEL_BLOCK, G, HEAD_DIM)
    vc = v_cache.reshape(v_cache.shape[0] * bpp, SEL_BLOCK, G, HEAD_DIM)

    def cache_spec(g):
        def index(b, n, pt, ids_):
            bid = jnp.minimum(ids_[(b * G + g) * N_SEL + n], n_past - 1)
            return (pt[b, bid // bpp] * bpp + bid % bpp, 0, 0, 0)
        return pl.BlockSpec((1, SEL_BLOCK, G, HEAD_DIM), index)

    q_spec = pl.BlockSpec((1, G, R, HEAD_DIM), lambda b, n, pt, ids_: (b, 0, 0, 0))
    n_spec = pl.BlockSpec((1, 1, G * HEAD_DIM), lambda b, n, pt, ids_: (b, 0, 0))
    g_spec = pl.BlockSpec((1, G, R, LANE), lambda b, n, pt, ids_: (b, 0, 0, 0))
    c_specs = [cache_spec(g) for g in range(G)]
    return pl.pallas_call(
        functools.partial(_attn_dec_sel_kernel, n_past=n_past, pos=pos),
        grid_spec=pltpu.PrefetchScalarGridSpec(
            num_scalar_prefetch=2,
            grid=(B, N_SEL),
            in_specs=[q_spec] + c_specs + c_specs + [n_spec, n_spec, q_spec, q_spec, g_spec],
            out_specs=q_spec,
            scratch_shapes=[pltpu.VMEM((G, R, 1), F32), pltpu.VMEM((G, R, 1), F32),
                            pltpu.VMEM((G, R, HEAD_DIM), F32)],
        ),
        out_shape=jax.ShapeDtypeStruct((B, G, R, HEAD_DIM), F32),
        compiler_params=_params(("parallel", "arbitrary")),
        name="attn_dec_sel",
    )(table, ids.reshape(-1), qr, *([kc] * G), *([vc] * G), k_new, v_new, ocmp, owin, gates)


def _rope_tables(pos):
    half = HEAD_DIM // 2
    inv = ROPE_THETA ** (-jnp.arange(half, dtype=F32) / half)
    ang = pos.astype(F32)[:, None] * inv[None, :]
    cos, sin = jnp.cos(ang), jnp.sin(ang)
    return jnp.concatenate([cos, cos], axis=1), jnp.concatenate([-sin, sin], axis=1)


def _prep_weights(W):
    D = W['w_kv'].shape[0]
    H = D // HEAD_DIM
    R = H // N_KV_HEADS
    pg = D // len(POOL_WINDOWS)
    half = CMP_STRIDE * HEAD_DIM
    P = {}
    P['w_pool'] = W['w_pool'].astype(BF16).reshape(-1, len(POOL_WINDOWS) * pg, pg)
    kv_gain = jnp.ones((W['w_kv'].shape[1], KV_W), F32)
    kv_gain = kv_gain.at[2].set(jnp.tile(W['g_k_sel'], N_KV_HEADS)).at[4].set(jnp.tile(W['g_k_win'], N_KV_HEADS))
    P['kv_gain'] = kv_gain
    for t in ('k', 'v'):
        w1 = W['w_cmp_%s1' % t].astype(BF16)
        P['w_cmp_%s1' % t] = w1
        P['w_cmp_%scat' % t] = jnp.concatenate([w1[:half], w1[half:]], axis=1)
        P['w_cmp_%s2' % t] = W['w_cmp_%s2' % t].astype(BF16)
        P['pe_%s' % t] = W['pe_cmp_%s' % t].astype(BF16).reshape(1, -1)
    n_b = W['w_qg'].shape[0]
    wg = W['w_qg'][:, :, H * HEAD_DIM:].astype(BF16).reshape(n_b, D, N_KV_HEADS, R * N_BRANCH)
    wg = jnp.pad(wg, ((0, 0), (0, 0), (0, 0), (0, LANE - R * N_BRANCH)))
    P['w_gate'] = wg.reshape(n_b, D, N_KV_HEADS * LANE)
    P['w_ple'] = W['w_ple'].astype(BF16)
    for layer in range(W['w_down'].shape[0]):
        P[('w_down', layer)] = W['w_down'][layer].astype(BF16)
    return P


def _dense(x, wname, layer, W, P, *, N, epilogue, specs, extras=(), out_dtypes, emit=False, name):
    M, K = x.shape
    shapes = [d if isinstance(d, jax.ShapeDtypeStruct) else jax.ShapeDtypeStruct((M, N), d) for d in out_dtypes]
    key = (wname, layer)
    tmx = 1024 if M >= 1024 else M
    if key in P:
        grid, tm, tn, tk = _tiles(M, N, K, tmx, 1024, 2048 if M >= 1024 else 4096)
        extra_specs, out_specs = specs(tm, tn)
        return mm(x, P[key], grid=grid, tm=tm, tn=tn, tk=tk, epilogue=epilogue, extras=extras,
                  extra_specs=extra_specs, out_shapes=shapes, out_specs=out_specs, name=name)
    grid, tm, tn, tk = _tiles(M, N, K, tmx, 512, K if K <= 4096 else 2048)
    extra_specs, out_specs = specs(tm, tn)
    w = W[wname]
    outs = list(mm_ws(x, w, layer=layer if w.ndim == 3 else None, emit=emit, grid=grid, tm=tm, tn=tn, tk=tk,
                      epilogue=epilogue, extras=extras, extra_specs=extra_specs, out_shapes=shapes,
                      out_specs=out_specs, name=name))
    if emit:
        P[key] = outs.pop()
    return outs


def _ffn_ple(h, p_l, layer, W, P):
    M, D = h.shape
    F = W['w_up'].shape[2]
    emit = M >= 1024
    mn = lambda tm, tn: ([], [_mn_spec(tm, tn)])
    res = lambda tm, tn: ([_mn_spec(tm, tn)], [_mn_spec(tm, tn)])
    (m_,) = rmsnorm_rows(h, [W['g_ffn'][layer]], [BF16])
    (u,) = _dense(m_, 'w_up', layer, W, P, N=F, epilogue=lambda acc, rows: (jnp.square(jnp.maximum(acc, 0.0)),),
                  specs=mn, out_dtypes=[BF16], emit=emit, name="ffn_up")
    (h,) = _dense(u, 'w_down', layer, W, P, N=D, epilogue=lambda acc, rows, r: (r[rows, :] + acc,), specs=res,
                  extras=[h], out_dtypes=[F32], emit=emit, name="ffn_down")
    (e_,) = rmsnorm_rows(h, [W['g_ple'][layer]], [BF16])
    ple_dim = p_l.shape[1]
    ple_specs = lambda tm, tn: ([_mn_spec(tm, tn), pl.BlockSpec((tm, ple_dim), lambda i, j, k: (i, 0)),
                                 pl.BlockSpec((ple_dim, tn), lambda i, j, k: (0, j))], [_mn_spec(tm, tn)])
    (h,) = _dense(e_, 'w_ple_gate', layer, W, P, N=D,
                  epilogue=lambda acc, rows, r, pp, wp: (r[rows, :] + _dot(pp[rows, :], wp[...]) * _sigmoid(acc),),
                  specs=ple_specs, extras=[h, p_l.astype(BF16), P['w_ple'][layer]], out_dtypes=[F32], name="ple")
    return h


def _kv_epilogue(acc, rows, gain_ref, cos_ref, sin_ref):
    j = pl.program_id(0)
    cosf, sinf = cos_ref[rows, :], sin_ref[rows, :]
    heads = []
    for hh in range(N_KV_HEADS):
        sl = slice(hh * HEAD_DIM, (hh + 1) * HEAD_DIM)
        heads.append(_rope(_head_norm(acc[:, sl], gain_ref[0, :, sl]), cosf, sinf))
    out = jnp.where((j == 2) | (j == 4), jnp.concatenate(heads, axis=1), acc)
    return out, out


def _q_epilogue(acc, rows, gq_ref, cos_ref, sin_ref):
    cosf, sinf = cos_ref[rows, :], sin_ref[rows, :]
    qs, qrs = [], []
    for hh in range(acc.shape[1] // HEAD_DIM):
        qn = _head_norm(acc[:, hh * HEAD_DIM:(hh + 1) * HEAD_DIM], gq_ref[...])
        qs.append(qn)
        qrs.append(_rope(qn, cosf, sinf))
    return jnp.concatenate(qs, axis=1), jnp.concatenate(qrs, axis=1)


def _trunk(x, p, pool_prefix, pos0, W, P, attend):
    B, T, D = x.shape
    M = B * T
    tmx = 1024 if M >= 1024 else M
    h = x.reshape(M, D)
    pg = D // len(POOL_WINDOWS)

    (a0,) = rmsnorm_rows(h, [W['g_mix'][0]], [F32])
    a0 = a0.reshape(B, T, D)
    d = pool_diff(a0, pool_prefix[0], pos0).reshape(M, D)
    pool_new = jnp.concatenate([pool_prefix[0], a0], axis=1)[:, -POOL_STATE:][None]
    grid, tm, tn, tk = _tiles(M, D, pg, tmx, pg, pg)
    (h,) = mm(d, P['w_pool'][0], grid=grid, tm=tm, tn=tn, tk=tk,
              x_map=lambda i, j, k: (i, j), w_map=lambda i, j, k: (j, 0),
              epilogue=lambda acc, rows, sc, r: (r[rows, :] + acc * sc[...],),
              extras=[W['pool_scale'][0].reshape(1, D), h],
              extra_specs=[pl.BlockSpec((1, tn), lambda i, j, k: (0, j)), _mn_spec(tm, tn)],
              out_shapes=[jax.ShapeDtypeStruct((M, D), F32)], out_specs=[_mn_spec(tm, tn)], name="pool_mix")
    h = _ffn_ple(h, p[0].reshape(M, -1), 0, W, P)

    hkv, a1 = rmsnorm_rows(h, [W['g_kv'], W['g_mix'][1]], [BF16, BF16])
    pos = pos0 + jnp.tile(jnp.arange(T, dtype=jnp.int32), B)
    cosf, sinf = _rope_tables(pos)
    n_kv = W['w_kv'].shape[1]
    rope_spec = lambda tm: pl.BlockSpec((tm, HEAD_DIM), lambda i, j, k: (i, 0))
    kv_spec = lambda tm: pl.BlockSpec((1, tm, KV_W), lambda i, j, k: (j, i, 0))
    kv4_spec = lambda tm: pl.BlockSpec((1, tm, N_KV_HEADS, HEAD_DIM), lambda i, j, k: (j, i, 0, 0))
    kv_specs = lambda tm, tn: ([pl.BlockSpec((1, 1, KV_W), lambda i, j, k: (j, 0, 0)), rope_spec(tm), rope_spec(tm)],
                               [kv4_spec(tm), kv_spec(tm)])
    kv, kv_b = _dense(hkv, 'w_kv2d', None, W, P, N=n_kv * KV_W, epilogue=_kv_epilogue, specs=kv_specs,
                      extras=[P['kv_gain'].reshape(n_kv, 1, KV_W), cosf, sinf],
                      out_dtypes=[jax.ShapeDtypeStruct((n_kv, M, N_KV_HEADS, HEAD_DIM), F32),
                                  jax.ShapeDtypeStruct((n_kv, M, KV_W), BF16)], name="kv_proj")
    q_specs = lambda tm, tn: ([pl.BlockSpec((1, HEAD_DIM), lambda i, j, k: (0, 0)), rope_spec(tm), rope_spec(tm)],
                              [_mn_spec(tm, tn)] * 2)
    q, qr = _dense(a1, 'w_qg', 0, W, P, N=D, epilogue=_q_epilogue, specs=q_specs,
                   extras=[W['g_q'][0].reshape(1, HEAD_DIM), cosf, sinf], out_dtypes=[BF16, BF16], name="q_proj")
    ng = N_KV_HEADS * LANE
    grid, tm, tn, tk = _tiles(M, ng, D, tmx, ng, D)
    (gates,) = mm(a1, P['w_gate'][0], grid=grid, tm=tm, tn=tn, tk=tk, epilogue=lambda acc, rows: (_sigmoid(acc),),
                  out_shapes=[jax.ShapeDtypeStruct((M, ng), F32)], out_specs=[_mn_spec(tm, tn)], name="gate_proj")

    o, win_state = attend(kv, kv_b, q, qr, gates)

    (h,) = _dense(o, 'w_o', 0, W, P, N=D, epilogue=lambda acc, rows, r: (r[rows, :] + acc,), extras=[h],
                  specs=lambda tm, tn: ([_mn_spec(tm, tn)], [_mn_spec(tm, tn)]), out_dtypes=[F32], name="attn_out")
    h = _ffn_ple(h, p[1].reshape(M, -1), 1, W, P)
    rows = tuple(kv[n].reshape(B, T, N_KV_HEADS, HEAD_DIM) for n in range(4))
    return h.reshape(B, T, D), pool_new, rows, win_state


def _compress(k_pages, v_pages, table, cp, W, P):
    abk, abv = cmp_partials(k_pages, v_pages, table, P['w_cmp_kcat'], P['w_cmp_vcat'], cp)
    gain = W['g_k_cmp'].reshape(1, HEAD_DIM)
    ck = cmp_finish(abk, P['pe_k'], P['w_cmp_k1'], P['w_cmp_k2'], gain, norm=True)
    cv = cmp_finish(abv, P['pe_v'], P['w_cmp_v1'], P['w_cmp_v2'], gain, norm=False)
    return ck, cv


def kernel(x_prompt, x_sample, state_pool, cache_k_cmp, cache_v_cmp, cache_k_sel, cache_v_sel, state_k_win, state_v_win, page_table, p_prompt, p_sample, g_mix, w_pool, pool_scale, g_kv, w_kv, g_k_cmp, g_k_sel, g_k_win, w_cmp_k1, w_cmp_k2, pe_cmp_k, w_cmp_v1, w_cmp_v2, pe_cmp_v, w_qg, g_q, w_o, g_ffn, w_up, w_down, g_ple, w_ple, w_ple_gate):
    W = dict(g_mix=g_mix, w_pool=w_pool, pool_scale=pool_scale, g_kv=g_kv, w_kv=w_kv, g_k_cmp=g_k_cmp,
             g_k_sel=g_k_sel, g_k_win=g_k_win, w_cmp_k1=w_cmp_k1, w_cmp_k2=w_cmp_k2, pe_cmp_k=pe_cmp_k,
             w_cmp_v1=w_cmp_v1, w_cmp_v2=w_cmp_v2, pe_cmp_v=pe_cmp_v, w_qg=w_qg, g_q=g_q, w_o=w_o,
             g_ffn=g_ffn, w_up=w_up, w_down=w_down, g_ple=g_ple, w_ple=w_ple, w_ple_gate=w_ple_gate)
    P = _prep_weights(W)
    W['w_kv2d'] = w_kv.reshape(w_kv.shape[0], -1)
    Bp, Tp, D = x_prompt.shape
    Bs, Ts, _ = x_sample.shape
    assert Ts == 1, "the decode path handles one new token per sequence"
    page = cache_k_cmp.shape[1]
    past_len = page_table.shape[1] * page
    R = D // HEAD_DIM // N_KV_HEADS
    assert Tp % page == 0 and past_len % SEL_BLOCK == 0

    def attend_prompt(kv, kv_b, q, qr, gates):
        ppb = Tp // page
        table = jnp.arange(Bp * ppb, dtype=jnp.int32).reshape(Bp, ppb)
        pages = lambda a: a.reshape(-1, page, N_KV_HEADS, HEAD_DIM)
        ck, cv = _compress(pages(kv[0]), pages(kv[1]), table, ppb, W, P)
        nc = Tp // CMP_STRIDE - CMP_BLOCK // CMP_STRIDE + 1
        seq = lambda a: a.reshape(Bp, Tp, -1)
        o = attn_prompt(seq(q), seq(qr), ck, cv, seq(kv_b[2]), seq(kv_b[3]), seq(kv_b[4]), seq(kv_b[5]),
                        seq(gates), nc=nc)
        nw = min(WINDOW, Tp)
        win = tuple(kv[n].reshape(Bp, Tp, N_KV_HEADS, HEAD_DIM)[:, -nw:] for n in (4, 5))
        return o.reshape(Bp * Tp, D), win

    def attend_sample(kv, kv_b, q, qr, gates):
        del kv_b
        ck, cv = _compress(cache_k_cmp, cache_v_cmp, page_table, min(32, page_table.shape[1]), W, P)
        nc = (past_len - (CMP_BLOCK - 1)) // CMP_STRIDE + 1
        ns = past_len // SEL_BLOCK + 1
        wb = state_k_win.shape[1]
        new_row = lambda a: a.reshape(Bs, 1, N_KV_HEADS, HEAD_DIM)
        kw = jnp.concatenate([state_k_win, new_row(kv[4])], axis=1)[:, -wb:]
        vw = jnp.concatenate([state_v_win, new_row(kv[5])], axis=1)[:, -wb:]
        heads = lambda a: a.reshape(Bs, N_KV_HEADS, R, HEAD_DIM)
        ocmp, owin, ids = attn_dec_dense(heads(q), heads(qr), ck, cv, kw, vw, nc=nc, ns=ns, pos=past_len)
        gt = gates.reshape(Bs, N_KV_HEADS, LANE)[:, :, :R * N_BRANCH].reshape(Bs, N_KV_HEADS, R, N_BRANCH)
        gt = jnp.pad(gt, ((0, 0), (0, 0), (0, 0), (0, LANE - N_BRANCH)))
        o = attn_dec_sel(page_table, ids, heads(qr), cache_k_sel, cache_v_sel,
                         kv[2].reshape(Bs, 1, KV_W), kv[3].reshape(Bs, 1, KV_W), ocmp, owin, gt, pos=past_len)
        return o.reshape(Bs, D).astype(BF16), (kw, vw)

    pool_zero = jnp.zeros((state_pool.shape[0], Bp, POOL_STATE, D), x_prompt.dtype)
    y_p, pool_p, rows_p, win_p = _trunk(x_prompt, p_prompt, pool_zero, 0, W, P, attend_prompt)
    y_s, pool_s, rows_s, win_s = _trunk(x_sample, p_sample, state_pool, past_len, W, P, attend_sample)
    return (y_p, y_s, pool_p, pool_s, rows_p[0], rows_p[1], rows_p[2], rows_p[3], win_p[0], win_p[1],
            rows_s[0], rows_s[1], rows_s[2], rows_s[3], win_s[0], win_s[1])
```

```python
import functools

import jax
import jax.numpy as jnp
import numpy as np
from jax import lax
from jax.experimental import pallas as pl
from jax.experimental.pallas import tpu as pltpu

F32 = jnp.float32
BF16 = jnp.bfloat16

POOL_WINDOWS = (2, 4, 8, 16)
POOL_STATE = max(POOL_WINDOWS) - 1
POOL_PAD = POOL_STATE + 1
HEAD_DIM = 128
N_KV_HEADS = 4
N_BRANCH = 3
CMP_BLOCK = 32
CMP_STRIDE = 16
CMP_HIDDEN = 2 * HEAD_DIM
SEL_BLOCK = 64
N_SEL = 16
WINDOW = 512
ROPE_THETA = 10000.0
EPS = 1e-6
SCALE = HEAD_DIM ** -0.5
NEG = -1e30
FORCE = 1e9
PAD_SCORE = -3e38
KV_W = N_KV_HEADS * HEAD_DIM
LANE = 128
VMEM_LIMIT = 56 * 1024 * 1024


def _params(sem):
    return pltpu.CompilerParams(dimension_semantics=sem, vmem_limit_bytes=VMEM_LIMIT)


def _sigmoid(x):
    return 1.0 / (1.0 + jnp.exp(-x))


def _dot(a, b):
    return jnp.dot(a, b, preferred_element_type=F32)


def _dot_nt(a, b):
    return lax.dot_general(a, b, (((1,), (1,)), ((), ())), preferred_element_type=F32)


def _dot_tn(a, b):
    return lax.dot_general(a, b, (((0,), (0,)), ((), ())), preferred_element_type=F32)


def _head_norm(x, g):
    return x * lax.rsqrt(jnp.mean(x * x, axis=-1, keepdims=True) + EPS) * g


def _rope(x, cosf, sinf):
    return x * cosf + pltpu.roll(x, HEAD_DIM // 2, 1) * sinf


def _rms_kernel(x_ref, g_ref, *o_refs):
    x = x_ref[...]
    y = x * lax.rsqrt(jnp.mean(x * x, axis=-1, keepdims=True) + EPS)
    for i, o_ref in enumerate(o_refs):
        o_ref[...] = (y * g_ref[i:i + 1, :]).astype(o_ref.dtype)


def rmsnorm_rows(x, gains, dtypes):
    M, D = x.shape
    tm = min(M, 256)
    g = jnp.stack(gains).astype(F32)
    n = len(gains)
    return pl.pallas_call(
        _rms_kernel,
        grid=(M // tm,),
        in_specs=[pl.BlockSpec((tm, D), lambda i: (i, 0)), pl.BlockSpec((n, D), lambda i: (0, 0))],
        out_specs=[pl.BlockSpec((tm, D), lambda i: (i, 0)) for _ in range(n)],
        out_shape=[jax.ShapeDtypeStruct((M, D), dt) for dt in dtypes],
        compiler_params=_params(("parallel",)),
        name="rmsnorm_rows",
    )(x, g)


EPILOGUE_ROWS = 256


def _finish_rows(acc_of, tm, outs, extras, epilogue):
    ch = min(tm, EPILOGUE_ROWS)
    for c in range(tm // ch):
        rows = slice(c * ch, (c + 1) * ch)
        for o_ref, r in zip(outs, epilogue(acc_of(rows), rows, *extras)):
            if len(o_ref.shape) == 4:
                for hh in range(o_ref.shape[2]):
                    o_ref[0, rows, hh, :] = r[:, hh * HEAD_DIM:(hh + 1) * HEAD_DIM].astype(o_ref.dtype)
            elif len(o_ref.shape) == 3:
                o_ref[0, rows, :] = r.astype(o_ref.dtype)
            else:
                o_ref[rows, :] = r.astype(o_ref.dtype)


def _mm_kernel(*refs, nk, n_extra, n_out, epilogue):
    x_ref, w_ref = refs[0], refs[1]
    extras = refs[2:2 + n_extra]
    outs = refs[2 + n_extra:2 + n_extra + n_out]
    tm = x_ref.shape[0]

    if nk == 1:
        _finish_rows(lambda rows: _dot(x_ref[rows, :], w_ref[...]), tm, outs, extras, epilogue)
    else:
        acc_ref = refs[-1]
        k = pl.program_id(2)

        @pl.when(k == 0)
        def _():
            acc_ref[...] = jnp.zeros_like(acc_ref)

        acc_ref[...] += _dot(x_ref[...], w_ref[...])

        @pl.when(k == nk - 1)
        def _():
            _finish_rows(lambda rows: acc_ref[rows, :], tm, outs, extras, epilogue)


def mm(x, w, *, grid, tm, tn, tk, epilogue, extras=(), extra_specs=(), out_shapes, out_specs,
       x_map=None, w_map=None, name):
    nk = grid[2]
    x_map = x_map or (lambda i, j, k: (i, k))
    w_map = w_map or (lambda i, j, k: (k, j))
    kern = functools.partial(_mm_kernel, nk=nk, n_extra=len(extras), n_out=len(out_shapes), epilogue=epilogue)
    return pl.pallas_call(
        kern,
        grid=grid,
        in_specs=[pl.BlockSpec((tm, tk), x_map), pl.BlockSpec((tk, tn), w_map)] + list(extra_specs),
        out_specs=list(out_specs),
        out_shape=list(out_shapes),
        scratch_shapes=[pltpu.VMEM((tm, tn), F32)] if nk > 1 else [],
        compiler_params=_params(("parallel", "parallel", "arbitrary")),
        name=name,
    )(x, w, *extras)


def _mm_ws_kernel(*refs, nk, n_extra, n_out, epilogue, emit, side):
    x_ref, w_ref = refs[0], refs[1]
    extras = refs[2:2 + n_extra]
    n_in = 2 + n_extra + side
    outs = refs[n_in:n_in + n_out]
    rest = refs[n_in + n_out:]
    wb_ref = rest[emit + side]
    i, k = pl.program_id(1), pl.program_id(2)

    @pl.when(i == 0)
    def _():
        if len(w_ref.shape) == 3:
            for hh in range(w_ref.shape[1]):
                wb_ref[k, :, hh * HEAD_DIM:(hh + 1) * HEAD_DIM] = w_ref[:, hh, :].astype(BF16)
        else:
            wb_ref[k] = w_ref[...].astype(BF16)
        if emit:
            rest[0][...] = wb_ref[k]

    if side:
        rest[emit][...] = refs[n_in - 1][...].astype(BF16)

    tm = x_ref.shape[0]
    if nk == 1:
        _finish_rows(lambda rows: _dot(x_ref[rows, :], wb_ref[0]), tm, outs, extras, epilogue)
    else:
        acc_ref = rest[-1]

        @pl.when(k == 0)
        def _():
            acc_ref[...] = jnp.zeros_like(acc_ref)

        acc_ref[...] += _dot(x_ref[...], wb_ref[k])

        @pl.when(k == nk - 1)
        def _():
            _finish_rows(lambda rows: acc_ref[rows, :], tm, outs, extras, epilogue)


def mm_ws(x, w, *, layer=None, emit=False, side=None, grid, tm, tn, tk, epilogue, extras=(), extra_specs=(),
          out_shapes, out_specs, name):
    gm, gn, nk = grid
    swap = lambda f: (lambda j, i, k: f(i, j, k))
    respec = lambda s: pl.BlockSpec(s.block_shape, swap(s.index_map))
    k_once = lambda i, k: jnp.where(i == 0, k, nk - 1)
    if w.ndim == 4:
        assert tn == w.shape[2] * w.shape[3]
        w_spec = pl.BlockSpec((tk, None) + w.shape[2:], lambda j, i, k: (k_once(i, k), j, 0, 0))
    elif layer is None:
        w_spec = pl.BlockSpec((tk, tn), lambda j, i, k: (k_once(i, k), j))
    else:
        w_spec = pl.BlockSpec((None, tk, tn), lambda j, i, k: (layer, k_once(i, k), j))
    n_out = len(out_shapes)
    ins, in_specs = [x, w, *extras], [pl.BlockSpec((tm, tk), lambda j, i, k: (i, k)), w_spec]
    in_specs += [respec(s) for s in extra_specs]
    out_shapes, out_specs = list(out_shapes), [respec(s) for s in out_specs]
    if emit:
        out_shapes.append(jax.ShapeDtypeStruct((nk * tk, gn * tn), BF16))
        out_specs.append(pl.BlockSpec((tk, tn), lambda j, i, k: (k_once(i, k), j)))
    if side is not None:
        s_arr, s_layer = side
        _, rows, cols = s_arr.shape
        rs = rows // (gn * gm * nk)
        assert rs * gn * gm * nk == rows and rs % 16 == 0
        step = lambda j, i, k: (j * gm + i) * nk + k
        ins.append(s_arr)
        in_specs.append(pl.BlockSpec((None, rs, cols), lambda j, i, k: (s_layer, step(j, i, k), 0)))
        out_shapes.append(jax.ShapeDtypeStruct((rows, cols), BF16))
        out_specs.append(pl.BlockSpec((rs, cols), lambda j, i, k: (step(j, i, k), 0)))
    kern = functools.partial(_mm_ws_kernel, nk=nk, n_extra=len(extras), n_out=n_out, epilogue=epilogue, emit=emit,
                             side=side is not None)
    return pl.pallas_call(
        kern,
        grid=(gn, gm, nk),
        in_specs=in_specs,
        out_specs=out_specs,
        out_shape=out_shapes,
        scratch_shapes=[pltpu.VMEM((nk, tk, tn), BF16)] + ([pltpu.VMEM((tm, tn), F32)] if nk > 1 else []),
        compiler_params=_params(("parallel", "arbitrary", "arbitrary")),
        name=name,
    )(*ins)


def _tiles(M, N, K, tm, tn, tk):
    tm, tn, tk = min(tm, M), min(tn, N), min(tk, K)
    return (M // tm, N // tn, K // tk), tm, tn, tk


def _mn_spec(tm, tn):
    return pl.BlockSpec((tm, tn), lambda i, j, k: (i, j))


def _pool_diff_kernel(a_ref, pre_ref, d_ref, seq_ref, *, T, pos0):
    seq_ref[0:POOL_PAD, :] = pre_ref[0]
    seq_ref[POOL_PAD:POOL_PAD + T, :] = a_ref[0]
    x_t = a_ref[0]
    pos = pos0 + lax.broadcasted_iota(jnp.int32, (T, 1), 0)
    for g, w in enumerate(POOL_WINDOWS):
        @pl.when(pl.program_id(1) == g)
        def _(w=w):
            s = x_t
            for j in range(1, w):
                s = s + seq_ref[POOL_PAD - j:POOL_PAD - j + T, :]
            cnt = jnp.minimum(pos + 1, w).astype(F32)
            d_ref[0] = (s / cnt - x_t).astype(d_ref.dtype)


def pool_diff(a, prefix, pos0):
    B, T, D = a.shape
    pg = D // len(POOL_WINDOWS)
    tc = min(pg, 512)
    cpg = pg // tc
    pre = jnp.concatenate([jnp.zeros((B, 1, D), F32), prefix], axis=1)
    return pl.pallas_call(
        functools.partial(_pool_diff_kernel, T=T, pos0=pos0),
        grid=(B, len(POOL_WINDOWS), cpg),
        in_specs=[pl.BlockSpec((1, T, tc), lambda b, g, c: (b, 0, g * cpg + c)),
                  pl.BlockSpec((1, POOL_PAD, tc), lambda b, g, c: (b, 0, g * cpg + c))],
        out_specs=pl.BlockSpec((1, T, tc), lambda b, g, c: (b, 0, g * cpg + c)),
        out_shape=jax.ShapeDtypeStruct((B, T, D), BF16),
        scratch_shapes=[pltpu.VMEM((POOL_PAD + T, tc), F32)],
        compiler_params=_params(("parallel", "parallel", "parallel")),
        name="pool_diff",
    )(a, pre)


PAGES_PER_STEP = 8
CMP_FINISH_ROWS = 512


def _cmp_partial_kernel(pt_ref, *refs, steps, pps):
    del pt_ref
    k_pages, v_pages = refs[:pps], refs[pps:2 * pps]
    wk_ref, wv_ref, abk_ref, abv_ref, xk_ref, xv_ref = refs[2 * pps:]
    p = pl.program_id(2)
    rows_per_page = (k_pages[0].shape[1] // CMP_STRIDE) * N_KV_HEADS
    for pages, x_ref in ((k_pages, xk_ref), (v_pages, xv_ref)):
        for q, page_ref in enumerate(pages):
            for sb in range(page_ref.shape[1] // CMP_STRIDE):
                row = q * rows_per_page + sb * N_KV_HEADS
                for r in range(CMP_STRIDE):
                    x_ref[p, row:row + N_KV_HEADS, r * HEAD_DIM:(r + 1) * HEAD_DIM] = page_ref[0, sb * CMP_STRIDE + r]

    @pl.when(p == steps - 1)
    def _():
        for x_ref, w_ref, ab_ref in ((xk_ref, wk_ref, abk_ref), (xv_ref, wv_ref, abv_ref)):
            x = x_ref[...].reshape(ab_ref.shape[1], x_ref.shape[2])
            ab_ref[0] = _dot(x.astype(BF16), w_ref[...])


def cmp_partials(k_pages, v_pages, table, wk_cat, wv_cat, cp):
    B, ppb = table.shape
    page = k_pages.shape[1]
    pps = PAGES_PER_STEP
    rows_per_page = page // CMP_STRIDE * N_KV_HEADS
    nch, steps = ppb // cp, cp // pps
    m = cp * rows_per_page
    kdim = CMP_STRIDE * HEAD_DIM

    def page_spec(q):
        return pl.BlockSpec((1, page, N_KV_HEADS, HEAD_DIM),
                            lambda b, c, p, pt: (pt[b, c * cp + p * pps + q], 0, 0, 0))

    page_specs = [page_spec(q) for q in range(pps)]
    w_spec = pl.BlockSpec((kdim, 2 * CMP_HIDDEN), lambda b, c, p, pt: (0, 0))
    out_spec = pl.BlockSpec((1, m, 2 * CMP_HIDDEN), lambda b, c, p, pt: (b, c, 0))
    out_shape = jax.ShapeDtypeStruct((B, ppb * rows_per_page, 2 * CMP_HIDDEN), F32)
    x_scratch = pltpu.VMEM((steps, pps * rows_per_page, kdim), F32)
    return pl.pallas_call(
        functools.partial(_cmp_partial_kernel, steps=steps, pps=pps),
        grid_spec=pltpu.PrefetchScalarGridSpec(
            num_scalar_prefetch=1,
            grid=(B, nch, steps),
            in_specs=page_specs + page_specs + [w_spec, w_spec],
            out_specs=[out_spec, out_spec],
            scratch_shapes=[x_scratch, x_scratch],
        ),
        out_shape=[out_shape, out_shape],
        compiler_params=_params(("parallel", "parallel", "arbitrary")),
        name="cmp_partials",
    )(table, *([k_pages] * pps), *([v_pages] * pps), wk_cat, wv_cat)


def _cmp_finish_kernel(ab_ref, pe_ref, w1_ref, w2_ref, g_ref, o_ref, out_ref, *, norm):
    n4 = ab_ref.shape[1]
    G = o_ref.shape[1]
    bias = _dot(jnp.broadcast_to(pe_ref[...], (8, pe_ref.shape[1])), w1_ref[...])[0:1, :]
    ch = min(n4, CMP_FINISH_ROWS)
    for c in range(n4 // ch):
        lo, hi = c * ch, (c + 1) * ch
        first = ab_ref[0, lo:hi, :CMP_HIDDEN]
        if hi + G <= n4:
            second = ab_ref[0, lo + G:hi + G, CMP_HIDDEN:]
        else:
            second = jnp.concatenate([ab_ref[0, lo + G:hi, CMP_HIDDEN:], ab_ref[0, hi - G:hi, CMP_HIDDEN:]], axis=0)
        pre = first + second + bias
        out = _dot((pre * _sigmoid(pre)).astype(BF16), w2_ref[...])
        if norm:
            out = _head_norm(out, g_ref[...])
        out_ref[lo:hi, :] = out
    for g in range(G):
        o_ref[0, g] = out_ref[pl.ds(g, n4 // G, stride=G), :].astype(o_ref.dtype)


def cmp_finish(ab, pe, w1, w2, gain, *, norm):
    B, n4, _ = ab.shape
    G = N_KV_HEADS
    full = lambda a: pl.BlockSpec(a.shape, lambda b: (0,) * a.ndim)
    args = (pe, w1, w2, gain)
    return pl.pallas_call(
        functools.partial(_cmp_finish_kernel, norm=norm),
        grid=(B,),
        in_specs=[pl.BlockSpec((1, n4, 2 * CMP_HIDDEN), lambda b: (b, 0, 0))] + [full(a) for a in args],
        out_specs=pl.BlockSpec((1, G, n4 // G, HEAD_DIM), lambda b: (b, 0, 0, 0)),
        out_shape=jax.ShapeDtypeStruct((B, G, n4 // G, HEAD_DIM), BF16),
        scratch_shapes=[pltpu.VMEM((n4, HEAD_DIM), F32)],
        compiler_params=_params(("parallel",)),
        name="cmp_finish",
    )(ab, *args)


def _select_blocks(score, blk, ns):
    rank = jnp.zeros(score.shape, jnp.int32)
    for j in range(ns):
        sj = score[j:j + 1, :]
        beats = (sj > score) | ((sj == score) & (j < blk))
        rank = rank + beats.astype(jnp.int32)
    return (rank < min(N_SEL, ns)) & (score > 0.5 * NEG)


def _attn_prompt_kernel(q_ref, qr_ref, ck_ref, cv_ref, ks_ref, vs_ref, kw_ref, vw_ref, gate_ref, mapT_ref,
                        expand_ref, o_ref, part_ref, sbias_ref, wbias_ref, m_ref, l_ref, acc_ref, *, tq, kc, nc, ns, R):
    qi = pl.program_id(2)
    q0 = qi * tq
    pos = q0 + lax.broadcasted_iota(jnp.int32, (tq, 1), 0)
    ncp = ck_ref.shape[2]

    ck = ck_ref[0, 0]
    cv = cv_ref[0, 0]
    cidx = lax.broadcasted_iota(jnp.int32, (1, ncp), 1)
    ok_c = (cidx * CMP_STRIDE + CMP_BLOCK - 1 <= pos) & (cidx < nc)
    imp = jnp.zeros((tq, ncp), F32)
    for r in range(R):
        qh = q_ref[0, :, r * HEAD_DIM:(r + 1) * HEAD_DIM]
        s = jnp.where(ok_c, _dot_nt(qh, ck) * SCALE, NEG)
        e = jnp.exp(s - jnp.max(s, axis=1, keepdims=True))
        p = jnp.where(ok_c, e * (1.0 / jnp.sum(e, axis=1, keepdims=True)), 0.0)
        imp = imp + p
        part_ref[r] = gate_ref[0, :, r * N_BRANCH:r * N_BRANCH + 1] * _dot(p.astype(BF16), cv)

    nsp = mapT_ref.shape[0]
    p_slc = lax.dot_general(mapT_ref[...], imp, (((1,), (1,)), ((), ())), precision=lax.Precision.HIGHEST,
                            preferred_element_type=F32)
    blk = lax.broadcasted_iota(jnp.int32, (nsp, tq), 0)
    pos_l = q0 + lax.broadcasted_iota(jnp.int32, (nsp, tq), 1)
    cur = pos_l // SEL_BLOCK
    vis = blk * SEL_BLOCK <= pos_l
    forced = vis & ((blk == 0) | (blk == cur) | (blk == cur - 1))
    score = jnp.where(forced, FORCE, jnp.where(vis, p_slc, NEG))
    score = jnp.where(blk < ns, score, PAD_SCORE)
    sel = _select_blocks(score, blk, ns).astype(BF16)

    c_hi = (q0 + tq) // kc
    col = lax.broadcasted_iota(jnp.int32, (1, kc), 1)
    for c in range(sbias_ref.shape[0]):
        @pl.when(c < c_hi)
        def _(c=c):
            ok = (_dot_tn(sel, expand_ref[:, c * kc:(c + 1) * kc]) > 0.5) & (c * kc + col <= pos)
            sbias_ref[c] = jnp.where(ok, 0.0, NEG)
    nwc = wbias_ref.shape[0]
    c_w0 = c_hi - nwc
    for d in range(nwc):
        kpos = (c_w0 + d) * kc + col
        wbias_ref[d] = jnp.where((kpos <= pos) & (kpos > pos - WINDOW), 0.0, NEG)

    def fold(t):
        return [t[:, i * LANE:(i + 1) * LANE] for i in range(kc // LANE)]

    def branch(k_ref, v_ref, c_lo, bias_of, gate_col):
        def logits(r, c, bias):
            k = k_ref[0, pl.ds(pl.multiple_of(c * kc, kc), kc), :]
            return _dot_nt(qr_ref[0, :, r * HEAD_DIM:(r + 1) * HEAD_DIM], k) * SCALE + bias

        m_ref[...] = jnp.full(m_ref.shape, NEG, F32)
        l_ref[...] = jnp.zeros(l_ref.shape, F32)
        acc_ref[...] = jnp.zeros(acc_ref.shape, F32)

        def max_body(c, carry):
            bias = bias_of(c)
            for r in range(R):
                mx = m_ref[r]
                for part in fold(logits(r, c, bias)):
                    mx = jnp.maximum(mx, part)
                m_ref[r] = mx
            return carry

        lax.fori_loop(c_lo, c_hi, max_body, 0)
        for r in range(R):
            m_ref[r] = jnp.broadcast_to(jnp.max(m_ref[r], axis=1, keepdims=True), (tq, LANE))

        def sum_body(c, carry):
            bias = bias_of(c)
            v = v_ref[0, pl.ds(pl.multiple_of(c * kc, kc), kc), :]
            for r in range(R):
                t = logits(r, c, bias)
                m = m_ref[r]
                ps = [jnp.exp(part - m) for part in fold(t)]
                acc_ref[r] += _dot(jnp.concatenate(ps, axis=1).astype(BF16), v)
                lsum = l_ref[r]
                for part in ps:
                    lsum = lsum + part
                l_ref[r] = lsum
            return carry

        lax.fori_loop(c_lo, c_hi, sum_body, 0)
        for r in range(R):
            gate = gate_ref[0, :, r * N_BRANCH + gate_col:r * N_BRANCH + gate_col + 1]
            inv_l = 1.0 / jnp.sum(l_ref[r], axis=1, keepdims=True)
            part_ref[r] += gate * (acc_ref[r] * inv_l)

    branch(ks_ref, vs_ref, 0, lambda c: sbias_ref[c], 1)
    branch(kw_ref, vw_ref, jnp.maximum(c_w0, 0), lambda c: wbias_ref[c - c_w0], 2)
    for r in range(R):
        o_ref[0, :, r * HEAD_DIM:(r + 1) * HEAD_DIM] = part_ref[r].astype(o_ref.dtype)


def _overlap_map(ncp, nsp, ns):
    ratio = CMP_BLOCK // CMP_STRIDE
    per_sel = SEL_BLOCK // CMP_STRIDE
    m = np.zeros((ncp, nsp), np.float32)
    for b in range(ns):
        for mm_ in range(per_sel):
            for n in range(ratio):
                j = per_sel * b + mm_ - n
                if 0 <= j < ncp:
                    m[j, b] += 1.0
    return m


def attn_prompt(q, qr, ck, cv, kvb, gates, *, nc):
    B, T, HD = q.shape
    G = N_KV_HEADS
    R = HD // HEAD_DIM // G
    tq = min(T, 256)
    kc = tq
    ns = T // SEL_BLOCK
    nsp = -(-ns // 8) * 8
    ncp = ck.shape[2]
    mapT = jnp.asarray(_overlap_map(ncp, nsp, ns).T)
    expand = jnp.asarray((np.arange(T)[None, :] // SEL_BLOCK == np.arange(nsp)[:, None]).astype(np.float32), BF16)
    q_spec = pl.BlockSpec((1, tq, R * HEAD_DIM), lambda b, g, i: (b, i, g))
    c_spec = pl.BlockSpec((1, 1, ncp, HEAD_DIM), lambda b, g, i: (b, g, 0, 0))
    kv_spec = lambda n: pl.BlockSpec((None, 1, T, HEAD_DIM), lambda b, g, i: (n, b, 0, g))
    return pl.pallas_call(
        functools.partial(_attn_prompt_kernel, tq=tq, kc=kc, nc=nc, ns=ns, R=R),
        grid=(B, G, T // tq),
        in_specs=[q_spec, q_spec, c_spec, c_spec, kv_spec(2), kv_spec(3), kv_spec(4), kv_spec(5),
                  pl.BlockSpec((1, tq, LANE), lambda b, g, i: (b, i, g)),
                  pl.BlockSpec((nsp, ncp), lambda b, g, i: (0, 0)),
                  pl.BlockSpec((nsp, T), lambda b, g, i: (0, 0))],
        out_specs=q_spec,
        out_shape=jax.ShapeDtypeStruct((B, T, HD), BF16),
        scratch_shapes=[pltpu.VMEM((R, tq, HEAD_DIM), F32), pltpu.VMEM((T // kc, tq, kc), F32),
                        pltpu.VMEM((min(WINDOW, T) // kc + tq // kc, tq, kc), F32)]
                       + [pltpu.VMEM((R, tq, HEAD_DIM), F32)] * 3,
        compiler_params=_params(("parallel", "parallel", "arbitrary")),
        name="attn_prompt",
    )(q, qr, ck, cv, kvb, kvb, kvb, kvb, gates, mapT, expand)


def _attn_dec_dense_kernel(q_ref, qr_ref, ck_ref, cv_ref, kw_ref, vw_ref, map_ref, ocmp_ref, owin_ref, ids_ref,
                           *, nc, ns, pos):
    G, R = q_ref.shape[1], q_ref.shape[2]
    ncp = ck_ref.shape[2]
    nsl = map_ref.shape[1]
    cidx = lax.broadcasted_iota(jnp.int32, (1, ncp), 1)
    ok_c = (cidx * CMP_STRIDE + CMP_BLOCK - 1 <= pos) & (cidx < nc)
    blk_l = lax.broadcasted_iota(jnp.int32, (1, nsl), 1)
    cur = pos // SEL_BLOCK
    vis = blk_l * SEL_BLOCK <= pos
    forced = vis & ((blk_l == 0) | (blk_l == cur) | (blk_l == cur - 1))
    ii = lax.broadcasted_iota(jnp.int32, (nsl, nsl), 0)
    jj = lax.broadcasted_iota(jnp.int32, (nsl, nsl), 1)
    slot = lax.broadcasted_iota(jnp.int32, (nsl, LANE), 1).astype(F32)
    blk_s = lax.broadcasted_iota(jnp.int32, (nsl, LANE), 0).astype(F32)
    for g in range(G):
        s = jnp.where(ok_c, _dot_nt(q_ref[0, g], ck_ref[0, g]) * SCALE, NEG)
        e = jnp.exp(s - jnp.max(s, axis=1, keepdims=True))
        p = jnp.where(ok_c, e * (1.0 / jnp.sum(e, axis=1, keepdims=True)), 0.0)
        ocmp_ref[0, g] = _dot(p.astype(BF16), cv_ref[0, g])
        imp = jnp.broadcast_to(jnp.sum(p, axis=0, keepdims=True), (R, ncp))
        p_slc = jnp.dot(imp, map_ref[...], precision=lax.Precision.HIGHEST, preferred_element_type=F32)[0:1, :]
        score_l = jnp.where(forced, FORCE, jnp.where(vis, p_slc, NEG))
        score_l = jnp.where(blk_l < ns, score_l, PAD_SCORE)
        score_s = jnp.sum(jnp.where(ii == jj, score_l, 0.0), axis=1, keepdims=True)
        beats = (score_l > score_s) | ((score_l == score_s) & (jj < ii))
        rank = jnp.sum(beats.astype(F32), axis=1, keepdims=True)
        ids = jnp.sum(jnp.where(rank == slot, blk_s, 0.0), axis=0, keepdims=True)
        ids_ref[0, g] = ids[:, :N_SEL].astype(jnp.int32)
        kw = kw_ref[0, :, g, :].astype(BF16)
        vw = vw_ref[0, :, g, :].astype(BF16)
        s = _dot_nt(qr_ref[0, g], kw) * SCALE
        e = jnp.exp(s - jnp.max(s, axis=1, keepdims=True))
        p = e * (1.0 / jnp.sum(e, axis=1, keepdims=True))
        owin_ref[0, g] = _dot(p.astype(BF16), vw)


def attn_dec_dense(q, qr, ck, cv, kw, vw, *, nc, ns, pos):
    B, G, R, _ = q.shape
    ncp = ck.shape[2]
    wb = kw.shape[1]
    nsl = -(-ns // LANE) * LANE
    omap = jnp.asarray(_overlap_map(ncp, nsl, ns))
    q_spec = pl.BlockSpec((1, G, R, HEAD_DIM), lambda b: (b, 0, 0, 0))
    c_spec = pl.BlockSpec((1, G, ncp, HEAD_DIM), lambda b: (b, 0, 0, 0))
    w_spec = pl.BlockSpec((1, wb, G, HEAD_DIM), lambda b: (b, 0, 0, 0))
    return pl.pallas_call(
        functools.partial(_attn_dec_dense_kernel, nc=nc, ns=ns, pos=pos),
        grid=(B,),
        in_specs=[q_spec, q_spec, c_spec, c_spec, w_spec, w_spec, pl.BlockSpec((ncp, nsl), lambda b: (0, 0))],
        out_specs=[q_spec, q_spec, pl.BlockSpec((1, G, 1, N_SEL), lambda b: (b, 0, 0, 0))],
        out_shape=[jax.ShapeDtypeStruct((B, G, R, HEAD_DIM), F32), jax.ShapeDtypeStruct((B, G, R, HEAD_DIM), F32),
                   jax.ShapeDtypeStruct((B, G, 1, N_SEL), jnp.int32)],
        compiler_params=_params(("parallel",)),
        name="attn_dec_dense",
    )(q, qr, ck, cv, kw, vw, omap)


def _attn_dec_sel_kernel(pt_ref, ids_ref, qr_ref, *refs, n_past, pos):
    del pt_ref
    G = qr_ref.shape[1]
    kc_refs, vc_refs = refs[:G], refs[G:2 * G]
    kn_ref, vn_ref, ocmp_ref, owin_ref, gate_ref, o_ref, m_ref, l_ref, acc_ref = refs[2 * G:]
    b, n = pl.program_id(0), pl.program_id(1)

    @pl.when(n == 0)
    def _():
        m_ref[...] = jnp.full(m_ref.shape, NEG, F32)
        l_ref[...] = jnp.zeros(l_ref.shape, F32)
        acc_ref[...] = jnp.zeros(acc_ref.shape, F32)

    row = lax.broadcasted_iota(jnp.int32, (SEL_BLOCK, 1), 0)
    for g in range(G):
        bid = ids_ref[(b * G + g) * N_SEL + n]
        is_new = bid >= n_past
        first = (row == 0) & (bid == n_past)
        sl = slice(g * HEAD_DIM, (g + 1) * HEAD_DIM)
        k = jnp.where(is_new, jnp.where(first, kn_ref[0, :, sl], 0.0), kc_refs[g][0, :, g, :]).astype(BF16)
        v = jnp.where(is_new, jnp.where(first, vn_ref[0, :, sl], 0.0), vc_refs[g][0, :, g, :]).astype(BF16)
        kpos = bid * SEL_BLOCK + lax.broadcasted_iota(jnp.int32, (1, SEL_BLOCK), 1)
        ok = kpos <= pos
        s = jnp.where(ok, _dot_nt(qr_ref[0, g], k) * SCALE, NEG)
        m_prev = m_ref[g]
        m_new = jnp.maximum(m_prev, jnp.max(s, axis=1, keepdims=True))
        alpha = jnp.exp(m_prev - m_new)
        p = jnp.where(ok, jnp.exp(s - m_new), 0.0)
        l_ref[g] = alpha * l_ref[g] + jnp.sum(p, axis=1, keepdims=True)
        acc_ref[g] = alpha * acc_ref[g] + _dot(p.astype(BF16), v)
        m_ref[g] = m_new

    @pl.when(n == N_SEL - 1)
    def _():
        for g in range(G):
            gt = gate_ref[0, g]
            o_sel = acc_ref[g] * (1.0 / l_ref[g])
            o_ref[0, g] = gt[:, 0:1] * ocmp_ref[0, g] + gt[:, 1:2] * o_sel + gt[:, 2:3] * owin_ref[0, g]


def attn_dec_sel(table, ids, qr, k_cache, v_cache, k_new, v_new, ocmp, owin, gates, *, pos):
    B, G, R, _ = qr.shape
    page = k_cache.shape[1]
    bpp = page // SEL_BLOCK
    n_past = table.shape[1] * bpp
    kc = k_cache.reshape(k_cache.shape[0] * bpp, SEL_BLOCK, G, HEAD_DIM)
    vc = v_cache.reshape(v_cache.shape[0] * bpp, SEL_BLOCK, G, HEAD_DIM)

    def cache_spec(g):
        def index(b, n, pt, ids_):
            bid = jnp.minimum(ids_[(b * G + g) * N_SEL + n], n_past - 1)
            return (pt[b, bid // bpp] * bpp + bid % bpp, 0, 0, 0)
        return pl.BlockSpec((1, SEL_BLOCK, G, HEAD_DIM), index)

    q_spec = pl.BlockSpec((1, G, R, HEAD_DIM), lambda b, n, pt, ids_: (b, 0, 0, 0))
    n_spec = pl.BlockSpec((1, 1, G * HEAD_DIM), lambda b, n, pt, ids_: (b, 0, 0))
    g_spec = pl.BlockSpec((1, G, R, LANE), lambda b, n, pt, ids_: (b, 0, 0, 0))
    c_specs = [cache_spec(g) for g in range(G)]
    return pl.pallas_call(
        functools.partial(_attn_dec_sel_kernel, n_past=n_past, pos=pos),
        grid_spec=pltpu.PrefetchScalarGridSpec(
            num_scalar_prefetch=2,
            grid=(B, N_SEL),
            in_specs=[q_spec] + c_specs + c_specs + [n_spec, n_spec, q_spec, q_spec, g_spec],
            out_specs=q_spec,
            scratch_shapes=[pltpu.VMEM((G, R, 1), F32), pltpu.VMEM((G, R, 1), F32),
                            pltpu.VMEM((G, R, HEAD_DIM), F32)],
        ),
        out_shape=jax.ShapeDtypeStruct((B, G, R, HEAD_DIM), F32),
        compiler_params=_params(("parallel", "arbitrary")),
        name="attn_dec_sel",
    )(table, ids.reshape(-1), qr, *([kc] * G), *([vc] * G), k_new, v_new, ocmp, owin, gates)


def _rope_tables(pos):
    half = HEAD_DIM // 2
    inv = ROPE_THETA ** (-jnp.arange(half, dtype=F32) / half)
    ang = pos.astype(F32)[:, None] * inv[None, :]
    cos, sin = jnp.cos(ang), jnp.sin(ang)
    return jnp.concatenate([cos, cos], axis=1), jnp.concatenate([-sin, sin], axis=1)


def _prep_weights(W):
    D = W['w_kv'].shape[0]
    H = D // HEAD_DIM
    R = H // N_KV_HEADS
    pg = D // len(POOL_WINDOWS)
    half = CMP_STRIDE * HEAD_DIM
    P = {}
    P['w_pool'] = W['w_pool'].astype(BF16).reshape(-1, len(POOL_WINDOWS) * pg, pg)
    kv_gain = jnp.ones((W['w_kv'].shape[1], KV_W), F32)
    kv_gain = kv_gain.at[2].set(jnp.tile(W['g_k_sel'], N_KV_HEADS)).at[4].set(jnp.tile(W['g_k_win'], N_KV_HEADS))
    P['kv_gain'] = kv_gain
    for t in ('k', 'v'):
        w1 = W['w_cmp_%s1' % t].astype(BF16)
        P['w_cmp_%s1' % t] = w1
        P['w_cmp_%scat' % t] = jnp.concatenate([w1[:half], w1[half:]], axis=1)
        P['w_cmp_%s2' % t] = W['w_cmp_%s2' % t].astype(BF16)
        P['pe_%s' % t] = W['pe_cmp_%s' % t].astype(BF16).reshape(1, -1)
    n_b = W['w_qg'].shape[0]
    wg = W['w_qg'][:, :, H * HEAD_DIM:].astype(BF16).reshape(n_b, D, N_KV_HEADS, R * N_BRANCH)
    wg = jnp.pad(wg, ((0, 0), (0, 0), (0, 0), (0, LANE - R * N_BRANCH)))
    P['w_gate'] = wg.reshape(n_b, D, N_KV_HEADS * LANE)
    P['w_ple'] = W['w_ple'].astype(BF16)
    return P


def _dense(x, wname, layer, W, P, *, N, epilogue, specs, extras=(), out_dtypes, emit=False, side_cast=None, name):
    M, K = x.shape
    shapes = [d if isinstance(d, jax.ShapeDtypeStruct) else jax.ShapeDtypeStruct((M, N), d) for d in out_dtypes]
    key = (wname, layer)
    tmx = 1024 if M >= 1024 else M
    if key in P:
        grid, tm, tn, tk = _tiles(M, N, K, tmx, 1024, 2048 if M >= 1024 else 4096)
        extra_specs, out_specs = specs(tm, tn)
        return mm(x, P[key], grid=grid, tm=tm, tn=tn, tk=tk, epilogue=epilogue, extras=extras,
                  extra_specs=extra_specs, out_shapes=shapes, out_specs=out_specs, name=name)
    grid, tm, tn, tk = _tiles(M, N, K, tmx, 512, K if K <= 4096 else 2048)
    extra_specs, out_specs = specs(tm, tn)
    w = W[wname]
    side = None if side_cast is None else (W[side_cast], layer)
    outs = list(mm_ws(x, w, layer=layer if w.ndim == 3 else None, emit=emit, side=side, grid=grid, tm=tm, tn=tn,
                      tk=tk, epilogue=epilogue, extras=extras, extra_specs=extra_specs, out_shapes=shapes,
                      out_specs=out_specs, name=name))
    if side is not None:
        P[(side_cast, layer)] = outs.pop()
    if emit:
        P[key] = outs.pop()
    return outs


def _ffn_ple(h, p_l, layer, W, P):
    M, D = h.shape
    F = W['w_up'].shape[2]
    emit = M >= 1024
    mn = lambda tm, tn: ([], [_mn_spec(tm, tn)])
    res = lambda tm, tn: ([_mn_spec(tm, tn)], [_mn_spec(tm, tn)])
    (m_,) = rmsnorm_rows(h, [W['g_ffn'][layer]], [BF16])
    (u,) = _dense(m_, 'w_up', layer, W, P, N=F, epilogue=lambda acc, rows: (jnp.square(jnp.maximum(acc, 0.0)),),
                  specs=mn, out_dtypes=[BF16], emit=emit, side_cast='w_down' if emit else None, name="ffn_up")
    (h,) = _dense(u, 'w_down', layer, W, P, N=D, epilogue=lambda acc, rows, r: (r[rows, :] + acc,), specs=res,
                  extras=[h], out_dtypes=[F32], emit=emit, name="ffn_down")
    (e_,) = rmsnorm_rows(h, [W['g_ple'][layer]], [BF16])
    ple_dim = p_l.shape[1]
    ple_specs = lambda tm, tn: ([_mn_spec(tm, tn), pl.BlockSpec((tm, ple_dim), lambda i, j, k: (i, 0)),
                                 pl.BlockSpec((ple_dim, tn), lambda i, j, k: (0, j))], [_mn_spec(tm, tn)])
    (h,) = _dense(e_, 'w_ple_gate', layer, W, P, N=D,
                  epilogue=lambda acc, rows, r, pp, wp: (r[rows, :] + _dot(pp[rows, :], wp[...]) * _sigmoid(acc),),
                  specs=ple_specs, extras=[h, p_l.astype(BF16), P['w_ple'][layer]], out_dtypes=[F32], name="ple")
    return h


def _kv_epilogue(acc, rows, gain_ref, cos_ref, sin_ref):
    j = pl.program_id(0)
    cosf, sinf = cos_ref[rows, :], sin_ref[rows, :]
    heads = []
    for hh in range(N_KV_HEADS):
        sl = slice(hh * HEAD_DIM, (hh + 1) * HEAD_DIM)
        heads.append(_rope(_head_norm(acc[:, sl], gain_ref[0, :, sl]), cosf, sinf))
    out = jnp.where((j == 2) | (j == 4), jnp.concatenate(heads, axis=1), acc)
    return out, out


def _q_epilogue(acc, rows, gq_ref, cos_ref, sin_ref):
    cosf, sinf = cos_ref[rows, :], sin_ref[rows, :]
    qs, qrs = [], []
    for hh in range(acc.shape[1] // HEAD_DIM):
        qn = _head_norm(acc[:, hh * HEAD_DIM:(hh + 1) * HEAD_DIM], gq_ref[...])
        qs.append(qn)
        qrs.append(_rope(qn, cosf, sinf))
    return jnp.concatenate(qs, axis=1), jnp.concatenate(qrs, axis=1)


def _trunk(x, p, pool_prefix, pos0, W, P, attend):
    B, T, D = x.shape
    M = B * T
    tmx = 1024 if M >= 1024 else M
    h = x.reshape(M, D)
    pg = D // len(POOL_WINDOWS)

    (a0,) = rmsnorm_rows(h, [W['g_mix'][0]], [F32])
    a0 = a0.reshape(B, T, D)
    d = pool_diff(a0, pool_prefix[0], pos0).reshape(M, D)
    pool_new = jnp.concatenate([pool_prefix[0], a0], axis=1)[:, -POOL_STATE:][None]
    grid, tm, tn, tk = _tiles(M, D, pg, tmx, pg, pg)
    (h,) = mm(d, P['w_pool'][0], grid=grid, tm=tm, tn=tn, tk=tk,
              x_map=lambda i, j, k: (i, j), w_map=lambda i, j, k: (j, 0),
              epilogue=lambda acc, rows, sc, r: (r[rows, :] + acc * sc[...],),
              extras=[W['pool_scale'][0].reshape(1, D), h],
              extra_specs=[pl.BlockSpec((1, tn), lambda i, j, k: (0, j)), _mn_spec(tm, tn)],
              out_shapes=[jax.ShapeDtypeStruct((M, D), F32)], out_specs=[_mn_spec(tm, tn)], name="pool_mix")
    h = _ffn_ple(h, p[0].reshape(M, -1), 0, W, P)

    hkv, a1 = rmsnorm_rows(h, [W['g_kv'], W['g_mix'][1]], [BF16, BF16])
    pos = pos0 + jnp.tile(jnp.arange(T, dtype=jnp.int32), B)
    cosf, sinf = _rope_tables(pos)
    n_kv = W['w_kv'].shape[1]
    rope_spec = lambda tm: pl.BlockSpec((tm, HEAD_DIM), lambda i, j, k: (i, 0))
    kv_spec = lambda tm: pl.BlockSpec((1, tm, KV_W), lambda i, j, k: (j, i, 0))
    kv4_spec = lambda tm: pl.BlockSpec((1, tm, N_KV_HEADS, HEAD_DIM), lambda i, j, k: (j, i, 0, 0))
    kv_specs = lambda tm, tn: ([pl.BlockSpec((1, 1, KV_W), lambda i, j, k: (j, 0, 0)), rope_spec(tm), rope_spec(tm)],
                               [kv4_spec(tm), kv_spec(tm)])
    kv, kv_b = _dense(hkv, 'w_kv', None, W, P, N=n_kv * KV_W, epilogue=_kv_epilogue, specs=kv_specs,
                      extras=[P['kv_gain'].reshape(n_kv, 1, KV_W), cosf, sinf],
                      out_dtypes=[jax.ShapeDtypeStruct((n_kv, M, N_KV_HEADS, HEAD_DIM), F32),
                                  jax.ShapeDtypeStruct((n_kv, M, KV_W), BF16)], name="kv_proj")
    q_specs = lambda tm, tn: ([pl.BlockSpec((1, HEAD_DIM), lambda i, j, k: (0, 0)), rope_spec(tm), rope_spec(tm)],
                              [_mn_spec(tm, tn)] * 2)
    q, qr = _dense(a1, 'w_qg', 0, W, P, N=D, epilogue=_q_epilogue, specs=q_specs,
                   extras=[W['g_q'][0].reshape(1, HEAD_DIM), cosf, sinf], out_dtypes=[BF16, BF16], name="q_proj")
    ng = N_KV_HEADS * LANE
    grid, tm, tn, tk = _tiles(M, ng, D, tmx, ng, D)
    (gates,) = mm(a1, P['w_gate'][0], grid=grid, tm=tm, tn=tn, tk=tk, epilogue=lambda acc, rows: (_sigmoid(acc),),
                  out_shapes=[jax.ShapeDtypeStruct((M, ng), F32)], out_specs=[_mn_spec(tm, tn)], name="gate_proj")

    o, win_state = attend(kv, kv_b, q, qr, gates)

    (h,) = _dense(o, 'w_o', 0, W, P, N=D, epilogue=lambda acc, rows, r: (r[rows, :] + acc,), extras=[h],
                  specs=lambda tm, tn: ([_mn_spec(tm, tn)], [_mn_spec(tm, tn)]), out_dtypes=[F32], name="attn_out")
    h = _ffn_ple(h, p[1].reshape(M, -1), 1, W, P)
    rows = tuple(kv[n].reshape(B, T, N_KV_HEADS, HEAD_DIM) for n in range(4))
    return h.reshape(B, T, D), pool_new, rows, win_state


def _compress(k_pages, v_pages, table, cp, W, P):
    abk, abv = cmp_partials(k_pages, v_pages, table, P['w_cmp_kcat'], P['w_cmp_vcat'], cp)
    gain = W['g_k_cmp'].reshape(1, HEAD_DIM)
    ck = cmp_finish(abk, P['pe_k'], P['w_cmp_k1'], P['w_cmp_k2'], gain, norm=True)
    cv = cmp_finish(abv, P['pe_v'], P['w_cmp_v1'], P['w_cmp_v2'], gain, norm=False)
    return ck, cv


def kernel(x_prompt, x_sample, state_pool, cache_k_cmp, cache_v_cmp, cache_k_sel, cache_v_sel, state_k_win, state_v_win, page_table, p_prompt, p_sample, g_mix, w_pool, pool_scale, g_kv, w_kv, g_k_cmp, g_k_sel, g_k_win, w_cmp_k1, w_cmp_k2, pe_cmp_k, w_cmp_v1, w_cmp_v2, pe_cmp_v, w_qg, g_q, w_o, g_ffn, w_up, w_down, g_ple, w_ple, w_ple_gate):
    W = dict(g_mix=g_mix, w_pool=w_pool, pool_scale=pool_scale, g_kv=g_kv, w_kv=w_kv, g_k_cmp=g_k_cmp,
             g_k_sel=g_k_sel, g_k_win=g_k_win, w_cmp_k1=w_cmp_k1, w_cmp_k2=w_cmp_k2, pe_cmp_k=pe_cmp_k,
             w_cmp_v1=w_cmp_v1, w_cmp_v2=w_cmp_v2, pe_cmp_v=pe_cmp_v, w_qg=w_qg, g_q=g_q, w_o=w_o,
             g_ffn=g_ffn, w_up=w_up, w_down=w_down, g_ple=g_ple, w_ple=w_ple, w_ple_gate=w_ple_gate)
    P = _prep_weights(W)
    Bp, Tp, D = x_prompt.shape
    Bs, Ts, _ = x_sample.shape
    assert Ts == 1, "the decode path handles one new token per sequence"
    page = cache_k_cmp.shape[1]
    past_len = page_table.shape[1] * page
    R = D // HEAD_DIM // N_KV_HEADS
    assert Tp % page == 0 and past_len % SEL_BLOCK == 0

    def attend_prompt(kv, kv_b, q, qr, gates):
        ppb = Tp // page
        table = jnp.arange(Bp * ppb, dtype=jnp.int32).reshape(Bp, ppb)
        pages = lambda a: a.reshape(-1, page, N_KV_HEADS, HEAD_DIM)
        ck, cv = _compress(pages(kv[0]), pages(kv[1]), table, ppb, W, P)
        nc = Tp // CMP_STRIDE - CMP_BLOCK // CMP_STRIDE + 1
        seq = lambda a: a.reshape(Bp, Tp, -1)
        o = attn_prompt(seq(q), seq(qr), ck, cv, kv_b.reshape(-1, Bp, Tp, KV_W), seq(gates), nc=nc)
        nw = min(WINDOW, Tp)
        win = tuple(kv[n].reshape(Bp, Tp, N_KV_HEADS, HEAD_DIM)[:, -nw:] for n in (4, 5))
        return o.reshape(Bp * Tp, D), win

    def attend_sample(kv, kv_b, q, qr, gates):
        del kv_b
        ck, cv = _compress(cache_k_cmp, cache_v_cmp, page_table, min(32, page_table.shape[1]), W, P)
        nc = (past_len - (CMP_BLOCK - 1)) // CMP_STRIDE + 1
        ns = past_len // SEL_BLOCK + 1
        wb = state_k_win.shape[1]
        new_row = lambda a: a.reshape(Bs, 1, N_KV_HEADS, HEAD_DIM)
        kw = jnp.concatenate([state_k_win, new_row(kv[4])], axis=1)[:, -wb:]
        vw = jnp.concatenate([state_v_win, new_row(kv[5])], axis=1)[:, -wb:]
        heads = lambda a: a.reshape(Bs, N_KV_HEADS, R, HEAD_DIM)
        ocmp, owin, ids = attn_dec_dense(heads(q), heads(qr), ck, cv, kw, vw, nc=nc, ns=ns, pos=past_len)
        gt = gates.reshape(Bs, N_KV_HEADS, LANE)[:, :, :R * N_BRANCH].reshape(Bs, N_KV_HEADS, R, N_BRANCH)
        gt = jnp.pad(gt, ((0, 0), (0, 0), (0, 0), (0, LANE - N_BRANCH)))
        o = attn_dec_sel(page_table, ids, heads(qr), cache_k_sel, cache_v_sel,
                         kv[2].reshape(Bs, 1, KV_W), kv[3].reshape(Bs, 1, KV_W), ocmp, owin, gt, pos=past_len)
        return o.reshape(Bs, D).astype(BF16), (kw, vw)

    pool_zero = jnp.zeros((state_pool.shape[0], Bp, POOL_STATE, D), x_prompt.dtype)
    y_p, pool_p, rows_p, win_p = _trunk(x_prompt, p_prompt, pool_zero, 0, W, P, attend_prompt)
    y_s, pool_s, rows_s, win_s = _trunk(x_sample, p_sample, state_pool, past_len, W, P, attend_sample)
    return (y_p, y_s, pool_p, pool_s, rows_p[0], rows_p[1], rows_p[2], rows_p[3], win_p[0], win_p[1],
            rows_s[0], rows_s[1], rows_s[2], rows_s[3], win_s[0], win_s[1])
```

```python
import functools

import jax
import jax.numpy as jnp
import numpy as np
from jax import lax
from jax.experimental import pallas as pl
from jax.experimental.pallas import tpu as pltpu

F32 = jnp.float32
BF16 = jnp.bfloat16

POOL_WINDOWS = (2, 4, 8, 16)
POOL_STATE = max(POOL_WINDOWS) - 1
POOL_PAD = POOL_STATE + 1
HEAD_DIM = 128
N_KV_HEADS = 4
N_BRANCH = 3
CMP_BLOCK = 32
CMP_STRIDE = 16
CMP_HIDDEN = 2 * HEAD_DIM
SEL_BLOCK = 64
N_SEL = 16
WINDOW = 512
ROPE_THETA = 10000.0
EPS = 1e-6
SCALE = HEAD_DIM ** -0.5
NEG = -1e30
FORCE = 1e9
PAD_SCORE = -3e38
KV_W = N_KV_HEADS * HEAD_DIM
LANE = 128
VMEM_LIMIT = 56 * 1024 * 1024


def _params(sem):
    return pltpu.CompilerParams(dimension_semantics=sem, vmem_limit_bytes=VMEM_LIMIT)


def _sigmoid(x):
    return 1.0 / (1.0 + jnp.exp(-x))


def _dot(a, b):
    return jnp.dot(a, b, preferred_element_type=F32)


def _dot_nt(a, b):
    return lax.dot_general(a, b, (((1,), (1,)), ((), ())), preferred_element_type=F32)


def _dot_tn(a, b):
    return lax.dot_general(a, b, (((0,), (0,)), ((), ())), preferred_element_type=F32)


def _head_norm(x, g):
    return x * lax.rsqrt(jnp.mean(x * x, axis=-1, keepdims=True) + EPS) * g


def _rope(x, cosf, sinf):
    return x * cosf + pltpu.roll(x, HEAD_DIM // 2, 1) * sinf


def _rms_kernel(x_ref, g_ref, *o_refs):
    x = x_ref[...]
    y = x * lax.rsqrt(jnp.mean(x * x, axis=-1, keepdims=True) + EPS)
    for i, o_ref in enumerate(o_refs):
        o_ref[...] = (y * g_ref[i:i + 1, :]).astype(o_ref.dtype)


def rmsnorm_rows(x, gains, dtypes):
    M, D = x.shape
    tm = min(M, 256)
    g = jnp.stack(gains).astype(F32)
    n = len(gains)
    return pl.pallas_call(
        _rms_kernel,
        grid=(M // tm,),
        in_specs=[pl.BlockSpec((tm, D), lambda i: (i, 0)), pl.BlockSpec((n, D), lambda i: (0, 0))],
        out_specs=[pl.BlockSpec((tm, D), lambda i: (i, 0)) for _ in range(n)],
        out_shape=[jax.ShapeDtypeStruct((M, D), dt) for dt in dtypes],
        compiler_params=_params(("parallel",)),
        name="rmsnorm_rows",
    )(x, g)


EPILOGUE_ROWS = 256


def _finish_rows(acc_of, tm, outs, extras, epilogue):
    ch = min(tm, EPILOGUE_ROWS)
    for c in range(tm // ch):
        rows = slice(c * ch, (c + 1) * ch)
        for o_ref, r in zip(outs, epilogue(acc_of(rows), rows, *extras)):
            if len(o_ref.shape) == 4:
                for hh in range(o_ref.shape[2]):
                    o_ref[0, rows, hh, :] = r[:, hh * HEAD_DIM:(hh + 1) * HEAD_DIM].astype(o_ref.dtype)
            elif len(o_ref.shape) == 3:
                o_ref[0, rows, :] = r.astype(o_ref.dtype)
            else:
                o_ref[rows, :] = r.astype(o_ref.dtype)


def _mm_kernel(*refs, nk, n_extra, n_out, epilogue):
    x_ref, w_ref = refs[0], refs[1]
    extras = refs[2:2 + n_extra]
    outs = refs[2 + n_extra:2 + n_extra + n_out]
    tm = x_ref.shape[0]

    if nk == 1:
        _finish_rows(lambda rows: _dot(x_ref[rows, :], w_ref[...]), tm, outs, extras, epilogue)
    else:
        acc_ref = refs[-1]
        k = pl.program_id(2)

        @pl.when(k == 0)
        def _():
            acc_ref[...] = jnp.zeros_like(acc_ref)

        acc_ref[...] += _dot(x_ref[...], w_ref[...])

        @pl.when(k == nk - 1)
        def _():
            _finish_rows(lambda rows: acc_ref[rows, :], tm, outs, extras, epilogue)


def mm(x, w, *, grid, tm, tn, tk, epilogue, extras=(), extra_specs=(), out_shapes, out_specs,
       x_map=None, w_map=None, name):
    nk = grid[2]
    x_map = x_map or (lambda i, j, k: (i, k))
    w_map = w_map or (lambda i, j, k: (k, j))
    kern = functools.partial(_mm_kernel, nk=nk, n_extra=len(extras), n_out=len(out_shapes), epilogue=epilogue)
    return pl.pallas_call(
        kern,
        grid=grid,
        in_specs=[pl.BlockSpec((tm, tk), x_map), pl.BlockSpec((tk, tn), w_map)] + list(extra_specs),
        out_specs=list(out_specs),
        out_shape=list(out_shapes),
        scratch_shapes=[pltpu.VMEM((tm, tn), F32)] if nk > 1 else [],
        compiler_params=_params(("parallel", "parallel", "arbitrary")),
        name=name,
    )(x, w, *extras)


def _mm_ws_kernel(*refs, nk, n_extra, n_out, epilogue, emit, side):
    x_ref, w_ref = refs[0], refs[1]
    extras = refs[2:2 + n_extra]
    n_in = 2 + n_extra + side
    outs = refs[n_in:n_in + n_out]
    rest = refs[n_in + n_out:]
    wb_ref = rest[emit + side]
    i, k = pl.program_id(1), pl.program_id(2)

    @pl.when(i == 0)
    def _():
        wb_ref[k] = w_ref[...].astype(BF16)
        if emit:
            rest[0][...] = wb_ref[k]

    if side:
        rest[emit][...] = refs[n_in - 1][...].astype(BF16)

    tm = x_ref.shape[0]
    if nk == 1:
        _finish_rows(lambda rows: _dot(x_ref[rows, :], wb_ref[0]), tm, outs, extras, epilogue)
    else:
        acc_ref = rest[-1]

        @pl.when(k == 0)
        def _():
            acc_ref[...] = jnp.zeros_like(acc_ref)

        acc_ref[...] += _dot(x_ref[...], wb_ref[k])

        @pl.when(k == nk - 1)
        def _():
            _finish_rows(lambda rows: acc_ref[rows, :], tm, outs, extras, epilogue)


def mm_ws(x, w, *, layer=None, emit=False, side=None, grid, tm, tn, tk, epilogue, extras=(), extra_specs=(),
          out_shapes, out_specs, name):
    gm, gn, nk = grid
    swap = lambda f: (lambda j, i, k: f(i, j, k))
    respec = lambda s: pl.BlockSpec(s.block_shape, swap(s.index_map))
    k_once = lambda i, k: jnp.where(i == 0, k, nk - 1)
    if layer is None:
        w_spec = pl.BlockSpec((tk, tn), lambda j, i, k: (k_once(i, k), j))
    else:
        w_spec = pl.BlockSpec((None, tk, tn), lambda j, i, k: (layer, k_once(i, k), j))
    n_out = len(out_shapes)
    ins, in_specs = [x, w, *extras], [pl.BlockSpec((tm, tk), lambda j, i, k: (i, k)), w_spec]
    in_specs += [respec(s) for s in extra_specs]
    out_shapes, out_specs = list(out_shapes), [respec(s) for s in out_specs]
    if emit:
        out_shapes.append(jax.ShapeDtypeStruct((nk * tk, gn * tn), BF16))
        out_specs.append(pl.BlockSpec((tk, tn), lambda j, i, k: (k_once(i, k), j)))
    if side is not None:
        s_arr, s_layer = side
        _, rows, cols = s_arr.shape
        rs = rows // (gn * gm * nk)
        assert rs * gn * gm * nk == rows and rs % 16 == 0
        step = lambda j, i, k: (j * gm + i) * nk + k
        ins.append(s_arr)
        in_specs.append(pl.BlockSpec((None, rs, cols), lambda j, i, k: (s_layer, step(j, i, k), 0)))
        out_shapes.append(jax.ShapeDtypeStruct((rows, cols), BF16))
        out_specs.append(pl.BlockSpec((rs, cols), lambda j, i, k: (step(j, i, k), 0)))
    kern = functools.partial(_mm_ws_kernel, nk=nk, n_extra=len(extras), n_out=n_out, epilogue=epilogue, emit=emit,
                             side=side is not None)
    return pl.pallas_call(
        kern,
        grid=(gn, gm, nk),
        in_specs=in_specs,
        out_specs=out_specs,
        out_shape=out_shapes,
        scratch_shapes=[pltpu.VMEM((nk, tk, tn), BF16)] + ([pltpu.VMEM((tm, tn), F32)] if nk > 1 else []),
        compiler_params=_params(("parallel", "arbitrary", "arbitrary")),
        name=name,
    )(*ins)


def _tiles(M, N, K, tm, tn, tk):
    tm, tn, tk = min(tm, M), min(tn, N), min(tk, K)
    return (M // tm, N // tn, K // tk), tm, tn, tk


def _mn_spec(tm, tn):
    return pl.BlockSpec((tm, tn), lambda i, j, k: (i, j))


def _pool_diff_kernel(*refs, tt, pos0, halo):
    x_ref, pre_ref, g_ref = refs[0], refs[1 + halo], refs[2 + halo]
    d_ref, st_ref, seq_ref = refs[3 + halo:]
    t = pl.program_id(1)
    norm = lambda x: x * lax.rsqrt(jnp.mean(x * x, axis=-1, keepdims=True) + EPS) * g_ref[...]
    a = norm(x_ref[0])
    seq_ref[POOL_PAD:POOL_PAD + tt, :] = a

    @pl.when(t == 0)
    def _():
        seq_ref[0:POOL_PAD, :] = pre_ref[0]

    if halo:
        @pl.when(t > 0)
        def _():
            seq_ref[0:POOL_PAD, :] = norm(refs[1][0])

    pos = pos0 + t * tt + lax.broadcasted_iota(jnp.int32, (tt, 1), 0)
    pg = a.shape[1] // len(POOL_WINDOWS)
    for g, w in enumerate(POOL_WINDOWS):
        cols = slice(g * pg, (g + 1) * pg)
        s = a[:, cols]
        for j in range(1, w):
            s = s + seq_ref[POOL_PAD - j:POOL_PAD - j + tt, cols]
        cnt = jnp.minimum(pos + 1, w).astype(F32)
        d_ref[0, :, cols] = (s / cnt - a[:, cols]).astype(d_ref.dtype)
    st_ref[0] = seq_ref[tt:tt + POOL_PAD, :]


def pool_diff(x, gain, prefix, pos0):
    B, T, D = x.shape
    tt = min(T, 256)
    halo = T > tt
    pre = jnp.concatenate([jnp.zeros((B, 1, D), F32), prefix], axis=1)
    hpt = tt // POOL_PAD
    in_specs = [pl.BlockSpec((1, tt, D), lambda b, t: (b, t, 0))]
    if halo:
        in_specs.append(pl.BlockSpec((1, POOL_PAD, D), lambda b, t: (b, jnp.maximum(t * hpt - 1, 0), 0)))
    in_specs += [pl.BlockSpec((1, POOL_PAD, D), lambda b, t: (b, 0, 0)), pl.BlockSpec((1, D), lambda b, t: (0, 0))]
    d, st = pl.pallas_call(
        functools.partial(_pool_diff_kernel, tt=tt, pos0=pos0, halo=halo),
        grid=(B, T // tt),
        in_specs=in_specs,
        out_specs=[pl.BlockSpec((1, tt, D), lambda b, t: (b, t, 0)),
                   pl.BlockSpec((1, POOL_PAD, D), lambda b, t: (b, 0, 0))],
        out_shape=[jax.ShapeDtypeStruct((B, T, D), BF16), jax.ShapeDtypeStruct((B, POOL_PAD, D), F32)],
        scratch_shapes=[pltpu.VMEM((POOL_PAD + tt, D), F32)],
        compiler_params=_params(("parallel", "arbitrary")),
        name="pool_diff",
    )(*([x, x] if halo else [x]), pre, gain.reshape(1, D).astype(F32))
    return d, st[:, 1:]


PAGES_PER_STEP = 8
CMP_FINISH_ROWS = 512


def _cmp_partial_kernel(pt_ref, *refs, steps, pps):
    del pt_ref
    k_pages, v_pages = refs[:pps], refs[pps:2 * pps]
    wk_ref, wv_ref, abk_ref, abv_ref, xk_ref, xv_ref = refs[2 * pps:]
    p = pl.program_id(2)
    rows_per_page = (k_pages[0].shape[1] // CMP_STRIDE) * N_KV_HEADS
    for pages, x_ref in ((k_pages, xk_ref), (v_pages, xv_ref)):
        for q, page_ref in enumerate(pages):
            for sb in range(page_ref.shape[1] // CMP_STRIDE):
                row = q * rows_per_page + sb * N_KV_HEADS
                for r in range(CMP_STRIDE):
                    x_ref[p, row:row + N_KV_HEADS, r * HEAD_DIM:(r + 1) * HEAD_DIM] = page_ref[0, sb * CMP_STRIDE + r]

    @pl.when(p == steps - 1)
    def _():
        for x_ref, w_ref, ab_ref in ((xk_ref, wk_ref, abk_ref), (xv_ref, wv_ref, abv_ref)):
            x = x_ref[...].reshape(ab_ref.shape[1], x_ref.shape[2])
            ab_ref[0] = _dot(x.astype(BF16), w_ref[...])


def cmp_partials(k_pages, v_pages, table, wk_cat, wv_cat, cp):
    B, ppb = table.shape
    page = k_pages.shape[1]
    pps = PAGES_PER_STEP
    rows_per_page = page // CMP_STRIDE * N_KV_HEADS
    nch, steps = ppb // cp, cp // pps
    m = cp * rows_per_page
    kdim = CMP_STRIDE * HEAD_DIM

    def page_spec(q):
        return pl.BlockSpec((1, page, N_KV_HEADS, HEAD_DIM),
                            lambda b, c, p, pt: (pt[b, c * cp + p * pps + q], 0, 0, 0))

    page_specs = [page_spec(q) for q in range(pps)]
    w_spec = pl.BlockSpec((kdim, 2 * CMP_HIDDEN), lambda b, c, p, pt: (0, 0))
    out_spec = pl.BlockSpec((1, m, 2 * CMP_HIDDEN), lambda b, c, p, pt: (b, c, 0))
    out_shape = jax.ShapeDtypeStruct((B, ppb * rows_per_page, 2 * CMP_HIDDEN), F32)
    x_scratch = pltpu.VMEM((steps, pps * rows_per_page, kdim), F32)
    return pl.pallas_call(
        functools.partial(_cmp_partial_kernel, steps=steps, pps=pps),
        grid_spec=pltpu.PrefetchScalarGridSpec(
            num_scalar_prefetch=1,
            grid=(B, nch, steps),
            in_specs=page_specs + page_specs + [w_spec, w_spec],
            out_specs=[out_spec, out_spec],
            scratch_shapes=[x_scratch, x_scratch],
        ),
        out_shape=[out_shape, out_shape],
        compiler_params=_params(("parallel", "parallel", "arbitrary")),
        name="cmp_partials",
    )(table, *([k_pages] * pps), *([v_pages] * pps), wk_cat, wv_cat)


def _cmp_finish_kernel(ab_ref, pe_ref, w1_ref, w2_ref, g_ref, o_ref, out_ref, *, norm):
    n4 = ab_ref.shape[1]
    G = o_ref.shape[1]
    bias = _dot(jnp.broadcast_to(pe_ref[...], (8, pe_ref.shape[1])), w1_ref[...])[0:1, :]
    ch = min(n4, CMP_FINISH_ROWS)
    for c in range(n4 // ch):
        lo, hi = c * ch, (c + 1) * ch
        first = ab_ref[0, lo:hi, :CMP_HIDDEN]
        if hi + G <= n4:
            second = ab_ref[0, lo + G:hi + G, CMP_HIDDEN:]
        else:
            second = jnp.concatenate([ab_ref[0, lo + G:hi, CMP_HIDDEN:], ab_ref[0, hi - G:hi, CMP_HIDDEN:]], axis=0)
        pre = first + second + bias
        out = _dot((pre * _sigmoid(pre)).astype(BF16), w2_ref[...])
        if norm:
            out = _head_norm(out, g_ref[...])
        out_ref[lo:hi, :] = out
    for g in range(G):
        o_ref[0, g] = out_ref[pl.ds(g, n4 // G, stride=G), :].astype(o_ref.dtype)


def cmp_finish(ab, pe, w1, w2, gain, *, norm):
    B, n4, _ = ab.shape
    G = N_KV_HEADS
    full = lambda a: pl.BlockSpec(a.shape, lambda b: (0,) * a.ndim)
    args = (pe, w1, w2, gain)
    return pl.pallas_call(
        functools.partial(_cmp_finish_kernel, norm=norm),
        grid=(B,),
        in_specs=[pl.BlockSpec((1, n4, 2 * CMP_HIDDEN), lambda b: (b, 0, 0))] + [full(a) for a in args],
        out_specs=pl.BlockSpec((1, G, n4 // G, HEAD_DIM), lambda b: (b, 0, 0, 0)),
        out_shape=jax.ShapeDtypeStruct((B, G, n4 // G, HEAD_DIM), BF16),
        scratch_shapes=[pltpu.VMEM((n4, HEAD_DIM), F32)],
        compiler_params=_params(("parallel",)),
        name="cmp_finish",
    )(ab, *args)


def _select_blocks(score, blk, ns):
    rank = jnp.zeros(score.shape, jnp.int32)
    for j in range(ns):
        sj = score[j:j + 1, :]
        beats = (sj > score) | ((sj == score) & (j < blk))
        rank = rank + beats.astype(jnp.int32)
    return (rank < min(N_SEL, ns)) & (score > 0.5 * NEG)


def _attn_prompt_kernel(q_ref, qr_ref, ck_ref, cv_ref, ks_ref, vs_ref, kw_ref, vw_ref, gate_ref, mapT_ref,
                        expand_ref, o_ref, part_ref, sbias_ref, wbias_ref, m_ref, acc_ref, *, tq, kc, nc, ns, R):
    qi = pl.program_id(2)
    q0 = qi * tq
    pos = q0 + lax.broadcasted_iota(jnp.int32, (tq, 1), 0)
    ncp = ck_ref.shape[2]

    ck = ck_ref[0, 0]
    cv = cv_ref[0, 0]
    cidx = lax.broadcasted_iota(jnp.int32, (1, ncp), 1)
    ok_c = (cidx * CMP_STRIDE + CMP_BLOCK - 1 <= pos) & (cidx < nc)
    imp = jnp.zeros((tq, ncp), F32)
    for r in range(R):
        qh = q_ref[0, :, r * HEAD_DIM:(r + 1) * HEAD_DIM]
        s = jnp.where(ok_c, _dot_nt(qh, ck) * SCALE, NEG)
        e = jnp.exp(s - jnp.max(s, axis=1, keepdims=True))
        p = jnp.where(ok_c, e * (1.0 / jnp.sum(e, axis=1, keepdims=True)), 0.0)
        imp = imp + p
        part_ref[r] = gate_ref[0, :, r * N_BRANCH:r * N_BRANCH + 1] * _dot(p.astype(BF16), cv)

    nsp = mapT_ref.shape[0]
    p_slc = lax.dot_general(mapT_ref[...], imp, (((1,), (1,)), ((), ())), precision=lax.Precision.HIGHEST,
                            preferred_element_type=F32)
    blk = lax.broadcasted_iota(jnp.int32, (nsp, tq), 0)
    pos_l = q0 + lax.broadcasted_iota(jnp.int32, (nsp, tq), 1)
    cur = pos_l // SEL_BLOCK
    vis = blk * SEL_BLOCK <= pos_l
    forced = vis & ((blk == 0) | (blk == cur) | (blk == cur - 1))
    score = jnp.where(forced, FORCE, jnp.where(vis, p_slc, NEG))
    score = jnp.where(blk < ns, score, PAD_SCORE)
    sel = _select_blocks(score, blk, ns).astype(BF16)

    c_hi = (q0 + tq) // kc
    col = lax.broadcasted_iota(jnp.int32, (1, kc), 1)
    for c in range(sbias_ref.shape[0]):
        @pl.when(c < c_hi)
        def _(c=c):
            ok = (_dot_tn(sel, expand_ref[:, c * kc:(c + 1) * kc]) > 0.5) & (c * kc + col <= pos)
            sbias_ref[c] = jnp.where(ok, 0.0, NEG)
    nwc = wbias_ref.shape[0]
    c_w0 = c_hi - nwc
    for d in range(nwc):
        kpos = (c_w0 + d) * kc + col
        wbias_ref[d] = jnp.where((kpos <= pos) & (kpos > pos - WINDOW), 0.0, NEG)

    def fold(t):
        return [t[:, i * LANE:(i + 1) * LANE] for i in range(kc // LANE)]

    ones_col = (lax.broadcasted_iota(jnp.int32, (kc, LANE), 1) == 0).astype(BF16)

    def branch(k_ref, v_ref, c_lo, bias_of, gate_col):
        def logits(r, c, bias):
            k = k_ref[0, pl.ds(pl.multiple_of(c * kc, kc), kc), :]
            return _dot_nt(qr_ref[0, :, r * HEAD_DIM:(r + 1) * HEAD_DIM], k) * SCALE + bias

        m_ref[...] = jnp.full(m_ref.shape, NEG, F32)
        acc_ref[...] = jnp.zeros(acc_ref.shape, F32)

        def max_body(c, carry):
            bias = bias_of(c)
            for r in range(R):
                mx = m_ref[r]
                for part in fold(logits(r, c, bias)):
                    mx = jnp.maximum(mx, part)
                m_ref[r] = mx
            return carry

        lax.fori_loop(c_lo, c_hi, max_body, 0)
        for r in range(R):
            m_ref[r] = jnp.broadcast_to(jnp.max(m_ref[r], axis=1, keepdims=True), (tq, LANE))

        def sum_body(c, carry):
            bias = bias_of(c)
            v = jnp.concatenate([v_ref[0, pl.ds(pl.multiple_of(c * kc, kc), kc), :], ones_col], axis=1)
            for r in range(R):
                t = logits(r, c, bias)
                m = m_ref[r]
                ps = [jnp.exp(part - m) for part in fold(t)]
                acc_ref[r] += _dot(jnp.concatenate(ps, axis=1).astype(BF16), v)
            return carry

        lax.fori_loop(c_lo, c_hi, sum_body, 0)
        for r in range(R):
            gate = gate_ref[0, :, r * N_BRANCH + gate_col:r * N_BRANCH + gate_col + 1]
            part_ref[r] += (gate * (1.0 / acc_ref[r, :, HEAD_DIM:HEAD_DIM + 1])) * acc_ref[r, :, :HEAD_DIM]

    branch(ks_ref, vs_ref, 0, lambda c: sbias_ref[c], 1)
    branch(kw_ref, vw_ref, jnp.maximum(c_w0, 0), lambda c: wbias_ref[c - c_w0], 2)
    for r in range(R):
        o_ref[0, :, r * HEAD_DIM:(r + 1) * HEAD_DIM] = part_ref[r].astype(o_ref.dtype)


def _overlap_map(ncp, nsp, ns):
    ratio = CMP_BLOCK // CMP_STRIDE
    per_sel = SEL_BLOCK // CMP_STRIDE
    m = np.zeros((ncp, nsp), np.float32)
    for b in range(ns):
        for mm_ in range(per_sel):
            for n in range(ratio):
                j = per_sel * b + mm_ - n
                if 0 <= j < ncp:
                    m[j, b] += 1.0
    return m


def attn_prompt(q, qr, ck, cv, kvb, gates, *, nc):
    B, T, HD = q.shape
    G = N_KV_HEADS
    R = HD // HEAD_DIM // G
    tq = min(T, 256)
    kc = tq
    ns = T // SEL_BLOCK
    nsp = -(-ns // 8) * 8
    ncp = ck.shape[2]
    mapT = jnp.asarray(_overlap_map(ncp, nsp, ns).T)
    expand = jnp.asarray((np.arange(T)[None, :] // SEL_BLOCK == np.arange(nsp)[:, None]).astype(np.float32), BF16)
    q_spec = pl.BlockSpec((1, tq, R * HEAD_DIM), lambda b, g, i: (b, i, g))
    c_spec = pl.BlockSpec((1, 1, ncp, HEAD_DIM), lambda b, g, i: (b, g, 0, 0))
    kv_spec = lambda n: pl.BlockSpec((None, 1, T, HEAD_DIM), lambda b, g, i: (n, b, 0, g))
    return pl.pallas_call(
        functools.partial(_attn_prompt_kernel, tq=tq, kc=kc, nc=nc, ns=ns, R=R),
        grid=(B, G, T // tq),
        in_specs=[q_spec, q_spec, c_spec, c_spec, kv_spec(2), kv_spec(3), kv_spec(4), kv_spec(5),
                  pl.BlockSpec((1, tq, LANE), lambda b, g, i: (b, i, g)),
                  pl.BlockSpec((nsp, ncp), lambda b, g, i: (0, 0)),
                  pl.BlockSpec((nsp, T), lambda b, g, i: (0, 0))],
        out_specs=q_spec,
        out_shape=jax.ShapeDtypeStruct((B, T, HD), BF16),
        scratch_shapes=[pltpu.VMEM((R, tq, HEAD_DIM), F32), pltpu.VMEM((T // kc, tq, kc), F32),
                        pltpu.VMEM((min(WINDOW, T) // kc + tq // kc, tq, kc), F32)]
                       + [pltpu.VMEM((R, tq, HEAD_DIM), F32), pltpu.VMEM((R, tq, HEAD_DIM + LANE), F32)],
        compiler_params=_params(("parallel", "parallel", "arbitrary")),
        name="attn_prompt",
    )(q, qr, ck, cv, kvb, kvb, kvb, kvb, gates, mapT, expand)


def _attn_dec_dense_kernel(q_ref, qr_ref, ck_ref, cv_ref, kw_ref, vw_ref, map_ref, ocmp_ref, owin_ref, ids_ref,
                           *, nc, ns, pos):
    G, R = q_ref.shape[1], q_ref.shape[2]
    ncp = ck_ref.shape[2]
    nsl = map_ref.shape[1]
    cidx = lax.broadcasted_iota(jnp.int32, (1, ncp), 1)
    ok_c = (cidx * CMP_STRIDE + CMP_BLOCK - 1 <= pos) & (cidx < nc)
    blk_l = lax.broadcasted_iota(jnp.int32, (1, nsl), 1)
    cur = pos // SEL_BLOCK
    vis = blk_l * SEL_BLOCK <= pos
    forced = vis & ((blk_l == 0) | (blk_l == cur) | (blk_l == cur - 1))
    ii = lax.broadcasted_iota(jnp.int32, (nsl, nsl), 0)
    jj = lax.broadcasted_iota(jnp.int32, (nsl, nsl), 1)
    slot = lax.broadcasted_iota(jnp.int32, (nsl, LANE), 1).astype(F32)
    blk_s = lax.broadcasted_iota(jnp.int32, (nsl, LANE), 0).astype(F32)
    for g in range(G):
        s = jnp.where(ok_c, _dot_nt(q_ref[0, g], ck_ref[0, g]) * SCALE, NEG)
        e = jnp.exp(s - jnp.max(s, axis=1, keepdims=True))
        p = jnp.where(ok_c, e * (1.0 / jnp.sum(e, axis=1, keepdims=True)), 0.0)
        ocmp_ref[0, g] = _dot(p.astype(BF16), cv_ref[0, g])
        imp = jnp.broadcast_to(jnp.sum(p, axis=0, keepdims=True), (R, ncp))
        p_slc = jnp.dot(imp, map_ref[...], precision=lax.Precision.HIGHEST, preferred_element_type=F32)[0:1, :]
        score_l = jnp.where(forced, FORCE, jnp.where(vis, p_slc, NEG))
        score_l = jnp.where(blk_l < ns, score_l, PAD_SCORE)
        score_s = jnp.sum(jnp.where(ii == jj, score_l, 0.0), axis=1, keepdims=True)
        beats = (score_l > score_s) | ((score_l == score_s) & (jj < ii))
        rank = jnp.sum(beats.astype(F32), axis=1, keepdims=True)
        ids = jnp.sum(jnp.where(rank == slot, blk_s, 0.0), axis=0, keepdims=True)
        ids_ref[0, g] = ids[:, :N_SEL].astype(jnp.int32)
        kw = kw_ref[0, :, g, :].astype(BF16)
        vw = vw_ref[0, :, g, :].astype(BF16)
        s = _dot_nt(qr_ref[0, g], kw) * SCALE
        e = jnp.exp(s - jnp.max(s, axis=1, keepdims=True))
        p = e * (1.0 / jnp.sum(e, axis=1, keepdims=True))
        owin_ref[0, g] = _dot(p.astype(BF16), vw)


def attn_dec_dense(q, qr, ck, cv, kw, vw, *, nc, ns, pos):
    B, G, R, _ = q.shape
    ncp = ck.shape[2]
    wb = kw.shape[1]
    nsl = -(-ns // LANE) * LANE
    omap = jnp.asarray(_overlap_map(ncp, nsl, ns))
    q_spec = pl.BlockSpec((1, G, R, HEAD_DIM), lambda b: (b, 0, 0, 0))
    c_spec = pl.BlockSpec((1, G, ncp, HEAD_DIM), lambda b: (b, 0, 0, 0))
    w_spec = pl.BlockSpec((1, wb, G, HEAD_DIM), lambda b: (b, 0, 0, 0))
    return pl.pallas_call(
        functools.partial(_attn_dec_dense_kernel, nc=nc, ns=ns, pos=pos),
        grid=(B,),
        in_specs=[q_spec, q_spec, c_spec, c_spec, w_spec, w_spec, pl.BlockSpec((ncp, nsl), lambda b: (0, 0))],
        out_specs=[q_spec, q_spec, pl.BlockSpec((1, G, 1, N_SEL), lambda b: (b, 0, 0, 0))],
        out_shape=[jax.ShapeDtypeStruct((B, G, R, HEAD_DIM), F32), jax.ShapeDtypeStruct((B, G, R, HEAD_DIM), F32),
                   jax.ShapeDtypeStruct((B, G, 1, N_SEL), jnp.int32)],
        compiler_params=_params(("parallel",)),
        name="attn_dec_dense",
    )(q, qr, ck, cv, kw, vw, omap)


def _attn_dec_sel_kernel(pt_ref, ids_ref, qr_ref, *refs, n_past, pos):
    del pt_ref
    G = qr_ref.shape[1]
    kc_refs, vc_refs = refs[:G], refs[G:2 * G]
    kn_ref, vn_ref, ocmp_ref, owin_ref, gate_ref, o_ref, m_ref, l_ref, acc_ref = refs[2 * G:]
    b, n = pl.program_id(0), pl.program_id(1)

    @pl.when(n == 0)
    def _():
        m_ref[...] = jnp.full(m_ref.shape, NEG, F32)
        l_ref[...] = jnp.zeros(l_ref.shape, F32)
        acc_ref[...] = jnp.zeros(acc_ref.shape, F32)

    row = lax.broadcasted_iota(jnp.int32, (SEL_BLOCK, 1), 0)
    for g in range(G):
        bid = ids_ref[(b * G + g) * N_SEL + n]
        is_new = bid >= n_past
        first = (row == 0) & (bid == n_past)
        sl = slice(g * HEAD_DIM, (g + 1) * HEAD_DIM)
        k = jnp.where(is_new, jnp.where(first, kn_ref[0, :, sl], 0.0), kc_refs[g][0, :, g, :]).astype(BF16)
        v = jnp.where(is_new, jnp.where(first, vn_ref[0, :, sl], 0.0), vc_refs[g][0, :, g, :]).astype(BF16)
        kpos = bid * SEL_BLOCK + lax.broadcasted_iota(jnp.int32, (1, SEL_BLOCK), 1)
        ok = kpos <= pos
        s = jnp.where(ok, _dot_nt(qr_ref[0, g], k) * SCALE, NEG)
        m_prev = m_ref[g]
        m_new = jnp.maximum(m_prev, jnp.max(s, axis=1, keepdims=True))
        alpha = jnp.exp(m_prev - m_new)
        p = jnp.where(ok, jnp.exp(s - m_new), 0.0)
        l_ref[g] = alpha * l_ref[g] + jnp.sum(p, axis=1, keepdims=True)
        acc_ref[g] = alpha * acc_ref[g] + _dot(p.astype(BF16), v)
        m_ref[g] = m_new

    @pl.when(n == N_SEL - 1)
    def _():
        for g in range(G):
            gt = gate_ref[0, g]
            o_sel = acc_ref[g] * (1.0 / l_ref[g])
            o_ref[0, g] = gt[:, 0:1] * ocmp_ref[0, g] + gt[:, 1:2] * o_sel + gt[:, 2:3] * owin_ref[0, g]


def attn_dec_sel(table, ids, qr, k_cache, v_cache, k_new, v_new, ocmp, owin, gates, *, pos):
    B, G, R, _ = qr.shape
    page = k_cache.shape[1]
    bpp = page // SEL_BLOCK
    n_past = table.shape[1] * bpp
    kc = k_cache.reshape(k_cache.shape[0] * bpp, SEL_BLOCK, G, HEAD_DIM)
    vc = v_cache.reshape(v_cache.shape[0] * bpp, SEL_BLOCK, G, HEAD_DIM)

    def cache_spec(g):
        def index(b, n, pt, ids_):
            bid = jnp.minimum(ids_[(b * G + g) * N_SEL + n], n_past - 1)
            return (pt[b, bid // bpp] * bpp + bid % bpp, 0, 0, 0)
        return pl.BlockSpec((1, SEL_BLOCK, G, HEAD_DIM), index)

    q_spec = pl.BlockSpec((1, G, R, HEAD_DIM), lambda b, n, pt, ids_: (b, 0, 0, 0))
    n_spec = pl.BlockSpec((1, 1, G * HEAD_DIM), lambda b, n, pt, ids_: (b, 0, 0))
    g_spec = pl.BlockSpec((1, G, R, LANE), lambda b, n, pt, ids_: (b, 0, 0, 0))
    c_specs = [cache_spec(g) for g in range(G)]
    return pl.pallas_call(
        functools.partial(_attn_dec_sel_kernel, n_past=n_past, pos=pos),
        grid_spec=pltpu.PrefetchScalarGridSpec(
            num_scalar_prefetch=2,
            grid=(B, N_SEL),
            in_specs=[q_spec] + c_specs + c_specs + [n_spec, n_spec, q_spec, q_spec, g_spec],
            out_specs=q_spec,
            scratch_shapes=[pltpu.VMEM((G, R, 1), F32), pltpu.VMEM((G, R, 1), F32),
                            pltpu.VMEM((G, R, HEAD_DIM), F32)],
        ),
        out_shape=jax.ShapeDtypeStruct((B, G, R, HEAD_DIM), F32),
        compiler_params=_params(("parallel", "arbitrary")),
        name="attn_dec_sel",
    )(table, ids.reshape(-1), qr, *([kc] * G), *([vc] * G), k_new, v_new, ocmp, owin, gates)


def _rope_tables(pos):
    half = HEAD_DIM // 2
    inv = ROPE_THETA ** (-jnp.arange(half, dtype=F32) / half)
    ang = pos.astype(F32)[:, None] * inv[None, :]
    cos, sin = jnp.cos(ang), jnp.sin(ang)
    return jnp.concatenate([cos, cos], axis=1), jnp.concatenate([-sin, sin], axis=1)


def _prep_weights(W):
    D = W['w_kv'].shape[0]
    H = D // HEAD_DIM
    R = H // N_KV_HEADS
    pg = D // len(POOL_WINDOWS)
    half = CMP_STRIDE * HEAD_DIM
    P = {}
    P['w_pool'] = W['w_pool'].astype(BF16).reshape(-1, len(POOL_WINDOWS) * pg, pg)
    kv_gain = jnp.ones((W['w_kv'].shape[1], KV_W), F32)
    kv_gain = kv_gain.at[2].set(jnp.tile(W['g_k_sel'], N_KV_HEADS)).at[4].set(jnp.tile(W['g_k_win'], N_KV_HEADS))
    P['kv_gain'] = kv_gain
    for t in ('k', 'v'):
        w1 = W['w_cmp_%s1' % t].astype(BF16)
        P['w_cmp_%s1' % t] = w1
        P['w_cmp_%scat' % t] = jnp.concatenate([w1[:half], w1[half:]], axis=1)
        P['w_cmp_%s2' % t] = W['w_cmp_%s2' % t].astype(BF16)
        P['pe_%s' % t] = W['pe_cmp_%s' % t].astype(BF16).reshape(1, -1)
    n_b = W['w_qg'].shape[0]
    wg = W['w_qg'][:, :, H * HEAD_DIM:].astype(BF16).reshape(n_b, D, N_KV_HEADS, R * N_BRANCH)
    wg = jnp.pad(wg, ((0, 0), (0, 0), (0, 0), (0, LANE - R * N_BRANCH)))
    P['w_gate'] = wg.reshape(n_b, D, N_KV_HEADS * LANE)
    P['w_ple'] = W['w_ple'].astype(BF16)
    return P


def _dense(x, wname, layer, W, P, *, N, epilogue, specs, extras=(), out_dtypes, emit=False, side_cast=None, name):
    M, K = x.shape
    shapes = [d if isinstance(d, jax.ShapeDtypeStruct) else jax.ShapeDtypeStruct((M, N), d) for d in out_dtypes]
    key = (wname, layer)
    tmx = 1024 if M >= 1024 else M
    if key in P:
        grid, tm, tn, tk = _tiles(M, N, K, tmx, 1024, 2048 if M >= 1024 else 4096)
        extra_specs, out_specs = specs(tm, tn)
        return mm(x, P[key], grid=grid, tm=tm, tn=tn, tk=tk, epilogue=epilogue, extras=extras,
                  extra_specs=extra_specs, out_shapes=shapes, out_specs=out_specs, name=name)
    grid, tm, tn, tk = _tiles(M, N, K, tmx, 512, K if K <= 4096 else 2048)
    extra_specs, out_specs = specs(tm, tn)
    w = W[wname]
    side = None if side_cast is None else (W[side_cast], layer)
    outs = list(mm_ws(x, w, layer=layer if w.ndim == 3 else None, emit=emit, side=side, grid=grid, tm=tm, tn=tn,
                      tk=tk, epilogue=epilogue, extras=extras, extra_specs=extra_specs, out_shapes=shapes,
                      out_specs=out_specs, name=name))
    if side is not None:
        P[(side_cast, layer)] = outs.pop()
    if emit:
        P[key] = outs.pop()
    return outs


def _ffn_ple(h, p_l, layer, W, P):
    M, D = h.shape
    F = W['w_up'].shape[2]
    emit = M >= 1024
    mn = lambda tm, tn: ([], [_mn_spec(tm, tn)])
    res = lambda tm, tn: ([_mn_spec(tm, tn)], [_mn_spec(tm, tn)])
    (m_,) = rmsnorm_rows(h, [W['g_ffn'][layer]], [BF16])
    (u,) = _dense(m_, 'w_up', layer, W, P, N=F, epilogue=lambda acc, rows: (jnp.square(jnp.maximum(acc, 0.0)),),
                  specs=mn, out_dtypes=[BF16], emit=emit, side_cast='w_down' if emit else None, name="ffn_up")
    (h,) = _dense(u, 'w_down', layer, W, P, N=D, epilogue=lambda acc, rows, r: (r[rows, :] + acc,), specs=res,
                  extras=[h], out_dtypes=[F32], emit=emit, name="ffn_down")
    (e_,) = rmsnorm_rows(h, [W['g_ple'][layer]], [BF16])
    ple_dim = p_l.shape[1]
    ple_specs = lambda tm, tn: ([_mn_spec(tm, tn), pl.BlockSpec((tm, ple_dim), lambda i, j, k: (i, 0)),
                                 pl.BlockSpec((ple_dim, tn), lambda i, j, k: (0, j))], [_mn_spec(tm, tn)])
    (h,) = _dense(e_, 'w_ple_gate', layer, W, P, N=D,
                  epilogue=lambda acc, rows, r, pp, wp: (r[rows, :] + _dot(pp[rows, :], wp[...]) * _sigmoid(acc),),
                  specs=ple_specs, extras=[h, p_l.astype(BF16), P['w_ple'][layer]], out_dtypes=[F32], name="ple")
    return h


def _kv_epilogue(acc, rows, gain_ref, cos_ref, sin_ref):
    j = pl.program_id(0)
    cosf, sinf = cos_ref[rows, :], sin_ref[rows, :]
    heads = []
    for hh in range(N_KV_HEADS):
        sl = slice(hh * HEAD_DIM, (hh + 1) * HEAD_DIM)
        heads.append(_rope(_head_norm(acc[:, sl], gain_ref[0, :, sl]), cosf, sinf))
    out = jnp.where((j == 2) | (j == 4), jnp.concatenate(heads, axis=1), acc)
    return out, out


def _q_epilogue(acc, rows, gq_ref, cos_ref, sin_ref):
    cosf, sinf = cos_ref[rows, :], sin_ref[rows, :]
    qs, qrs = [], []
    for hh in range(acc.shape[1] // HEAD_DIM):
        qn = _head_norm(acc[:, hh * HEAD_DIM:(hh + 1) * HEAD_DIM], gq_ref[...])
        qs.append(qn)
        qrs.append(_rope(qn, cosf, sinf))
    return jnp.concatenate(qs, axis=1), jnp.concatenate(qrs, axis=1)


def _trunk(x, p, pool_prefix, pos0, W, P, attend):
    B, T, D = x.shape
    M = B * T
    tmx = 1024 if M >= 1024 else M
    h = x.reshape(M, D)
    pg = D // len(POOL_WINDOWS)

    d, pool_new = pool_diff(x, W['g_mix'][0], pool_prefix[0], pos0)
    d, pool_new = d.reshape(M, D), pool_new[None]
    grid, tm, tn, tk = _tiles(M, D, pg, tmx, pg, pg)
    (h,) = mm(d, P['w_pool'][0], grid=grid, tm=tm, tn=tn, tk=tk,
              x_map=lambda i, j, k: (i, j), w_map=lambda i, j, k: (j, 0),
              epilogue=lambda acc, rows, sc, r: (r[rows, :] + acc * sc[...],),
              extras=[W['pool_scale'][0].reshape(1, D), h],
              extra_specs=[pl.BlockSpec((1, tn), lambda i, j, k: (0, j)), _mn_spec(tm, tn)],
              out_shapes=[jax.ShapeDtypeStruct((M, D), F32)], out_specs=[_mn_spec(tm, tn)], name="pool_mix")
    h = _ffn_ple(h, p[0].reshape(M, -1), 0, W, P)

    hkv, a1 = rmsnorm_rows(h, [W['g_kv'], W['g_mix'][1]], [BF16, BF16])
    pos = pos0 + jnp.tile(jnp.arange(T, dtype=jnp.int32), B)
    cosf, sinf = _rope_tables(pos)
    n_kv = W['w_kv'].shape[1]
    rope_spec = lambda tm: pl.BlockSpec((tm, HEAD_DIM), lambda i, j, k: (i, 0))
    kv_spec = lambda tm: pl.BlockSpec((1, tm, KV_W), lambda i, j, k: (j, i, 0))
    kv4_spec = lambda tm: pl.BlockSpec((1, tm, N_KV_HEADS, HEAD_DIM), lambda i, j, k: (j, i, 0, 0))
    kv_specs = lambda tm, tn: ([pl.BlockSpec((1, 1, KV_W), lambda i, j, k: (j, 0, 0)), rope_spec(tm), rope_spec(tm)],
                               [kv4_spec(tm), kv_spec(tm)])
    kv, kv_b = _dense(hkv, 'w_kv2d', None, W, P, N=n_kv * KV_W, epilogue=_kv_epilogue, specs=kv_specs,
                      extras=[P['kv_gain'].reshape(n_kv, 1, KV_W), cosf, sinf],
                      out_dtypes=[jax.ShapeDtypeStruct((n_kv, M, N_KV_HEADS, HEAD_DIM), F32),
                                  jax.ShapeDtypeStruct((n_kv, M, KV_W), BF16)], name="kv_proj")
    q_specs = lambda tm, tn: ([pl.BlockSpec((1, HEAD_DIM), lambda i, j, k: (0, 0)), rope_spec(tm), rope_spec(tm)],
                              [_mn_spec(tm, tn)] * 2)
    q, qr = _dense(a1, 'w_qg', 0, W, P, N=D, epilogue=_q_epilogue, specs=q_specs,
                   extras=[W['g_q'][0].reshape(1, HEAD_DIM), cosf, sinf], out_dtypes=[BF16, BF16], name="q_proj")
    ng = N_KV_HEADS * LANE
    grid, tm, tn, tk = _tiles(M, ng, D, tmx, ng, D)
    (gates,) = mm(a1, P['w_gate'][0], grid=grid, tm=tm, tn=tn, tk=tk, epilogue=lambda acc, rows: (_sigmoid(acc),),
                  out_shapes=[jax.ShapeDtypeStruct((M, ng), F32)], out_specs=[_mn_spec(tm, tn)], name="gate_proj")

    o, win_state = attend(kv, kv_b, q, qr, gates)

    (h,) = _dense(o, 'w_o', 0, W, P, N=D, epilogue=lambda acc, rows, r: (r[rows, :] + acc,), extras=[h],
                  specs=lambda tm, tn: ([_mn_spec(tm, tn)], [_mn_spec(tm, tn)]), out_dtypes=[F32], name="attn_out")
    h = _ffn_ple(h, p[1].reshape(M, -1), 1, W, P)
    rows = tuple(kv[n].reshape(B, T, N_KV_HEADS, HEAD_DIM) for n in range(4))
    return h.reshape(B, T, D), pool_new, rows, win_state


def _compress(k_pages, v_pages, table, cp, W, P):
    abk, abv = cmp_partials(k_pages, v_pages, table, P['w_cmp_kcat'], P['w_cmp_vcat'], cp)
    gain = W['g_k_cmp'].reshape(1, HEAD_DIM)
    ck = cmp_finish(abk, P['pe_k'], P['w_cmp_k1'], P['w_cmp_k2'], gain, norm=True)
    cv = cmp_finish(abv, P['pe_v'], P['w_cmp_v1'], P['w_cmp_v2'], gain, norm=False)
    return ck, cv


def kernel(x_prompt, x_sample, state_pool, cache_k_cmp, cache_v_cmp, cache_k_sel, cache_v_sel, state_k_win, state_v_win, page_table, p_prompt, p_sample, g_mix, w_pool, pool_scale, g_kv, w_kv, g_k_cmp, g_k_sel, g_k_win, w_cmp_k1, w_cmp_k2, pe_cmp_k, w_cmp_v1, w_cmp_v2, pe_cmp_v, w_qg, g_q, w_o, g_ffn, w_up, w_down, g_ple, w_ple, w_ple_gate):
    W = dict(g_mix=g_mix, w_pool=w_pool, pool_scale=pool_scale, g_kv=g_kv, w_kv=w_kv, g_k_cmp=g_k_cmp,
             g_k_sel=g_k_sel, g_k_win=g_k_win, w_cmp_k1=w_cmp_k1, w_cmp_k2=w_cmp_k2, pe_cmp_k=pe_cmp_k,
             w_cmp_v1=w_cmp_v1, w_cmp_v2=w_cmp_v2, pe_cmp_v=pe_cmp_v, w_qg=w_qg, g_q=g_q, w_o=w_o,
             g_ffn=g_ffn, w_up=w_up, w_down=w_down, g_ple=g_ple, w_ple=w_ple, w_ple_gate=w_ple_gate)
    P = _prep_weights(W)
    W['w_kv2d'] = w_kv.reshape(w_kv.shape[0], -1)
    Bp, Tp, D = x_prompt.shape
    Bs, Ts, _ = x_sample.shape
    assert Ts == 1, "the decode path handles one new token per sequence"
    page = cache_k_cmp.shape[1]
    past_len = page_table.shape[1] * page
    R = D // HEAD_DIM // N_KV_HEADS
    assert Tp % page == 0 and past_len % SEL_BLOCK == 0

    def attend_prompt(kv, kv_b, q, qr, gates):
        ppb = Tp // page
        table = jnp.arange(Bp * ppb, dtype=jnp.int32).reshape(Bp, ppb)
        pages = lambda a: a.reshape(-1, page, N_KV_HEADS, HEAD_DIM)
        ck, cv = _compress(pages(kv[0]), pages(kv[1]), table, ppb, W, P)
        nc = Tp // CMP_STRIDE - CMP_BLOCK // CMP_STRIDE + 1
        seq = lambda a: a.reshape(Bp, Tp, -1)
        o = attn_prompt(seq(q), seq(qr), ck, cv, kv_b.reshape(-1, Bp, Tp, KV_W), seq(gates), nc=nc)
        nw = min(WINDOW, Tp)
        win = tuple(kv[n].reshape(Bp, Tp, N_KV_HEADS, HEAD_DIM)[:, -nw:] for n in (4, 5))
        return o.reshape(Bp * Tp, D), win

    def attend_sample(kv, kv_b, q, qr, gates):
        del kv_b
        ck, cv = _compress(cache_k_cmp, cache_v_cmp, page_table, min(32, page_table.shape[1]), W, P)
        nc = (past_len - (CMP_BLOCK - 1)) // CMP_STRIDE + 1
        ns = past_len // SEL_BLOCK + 1
        wb = state_k_win.shape[1]
        new_row = lambda a: a.reshape(Bs, 1, N_KV_HEADS, HEAD_DIM)
        kw = jnp.concatenate([state_k_win, new_row(kv[4])], axis=1)[:, -wb:]
        vw = jnp.concatenate([state_v_win, new_row(kv[5])], axis=1)[:, -wb:]
        heads = lambda a: a.reshape(Bs, N_KV_HEADS, R, HEAD_DIM)
        ocmp, owin, ids = attn_dec_dense(heads(q), heads(qr), ck, cv, kw, vw, nc=nc, ns=ns, pos=past_len)
        gt = gates.reshape(Bs, N_KV_HEADS, LANE)[:, :, :R * N_BRANCH].reshape(Bs, N_KV_HEADS, R, N_BRANCH)
        gt = jnp.pad(gt, ((0, 0), (0, 0), (0, 0), (0, LANE - N_BRANCH)))
        o = attn_dec_sel(page_table, ids, heads(qr), cache_k_sel, cache_v_sel,
                         kv[2].reshape(Bs, 1, KV_W), kv[3].reshape(Bs, 1, KV_W), ocmp, owin, gt, pos=past_len)
        return o.reshape(Bs, D).astype(BF16), (kw, vw)

    pool_zero = jnp.zeros((state_pool.shape[0], Bp, POOL_STATE, D), x_prompt.dtype)
    y_p, pool_p, rows_p, win_p = _trunk(x_prompt, p_prompt, pool_zero, 0, W, P, attend_prompt)
    y_s, pool_s, rows_s, win_s = _trunk(x_sample, p_sample, state_pool, past_len, W, P, attend_sample)
    return (y_p, y_s, pool_p, pool_s, rows_p[0], rows_p[1], rows_p[2], rows_p[3], win_p[0], win_p[1],
            rows_s[0], rows_s[1], rows_s[2], rows_s[3], win_s[0], win_s[1])
```

```python
import functools

import jax
import jax.numpy as jnp
import numpy as np
from jax import lax
from jax.experimental import pallas as pl
from jax.experimental.pallas import tpu as pltpu

F32 = jnp.float32
BF16 = jnp.bfloat16

POOL_WINDOWS = (2, 4, 8, 16)
POOL_STATE = max(POOL_WINDOWS) - 1
POOL_PAD = POOL_STATE + 1
HEAD_DIM = 128
N_KV_HEADS = 4
N_BRANCH = 3
CMP_BLOCK = 32
CMP_STRIDE = 16
CMP_HIDDEN = 2 * HEAD_DIM
SEL_BLOCK = 64
N_SEL = 16
WINDOW = 512
ROPE_THETA = 10000.0
EPS = 1e-6
SCALE = HEAD_DIM ** -0.5
NEG = -1e30
FORCE = 1e9
PAD_SCORE = -3e38
KV_W = N_KV_HEADS * HEAD_DIM
LANE = 128
VMEM_LIMIT = 56 * 1024 * 1024


def _params(sem):
    return pltpu.CompilerParams(dimension_semantics=sem, vmem_limit_bytes=VMEM_LIMIT)


def _sigmoid(x):
    return 1.0 / (1.0 + jnp.exp(-x))


def _dot(a, b):
    return jnp.dot(a, b, preferred_element_type=F32)


def _dot_nt(a, b):
    return lax.dot_general(a, b, (((1,), (1,)), ((), ())), preferred_element_type=F32)


def _dot_tn(a, b):
    return lax.dot_general(a, b, (((0,), (0,)), ((), ())), preferred_element_type=F32)


def _head_norm(x, g):
    return x * lax.rsqrt(jnp.mean(x * x, axis=-1, keepdims=True) + EPS) * g


def _rope(x, cosf, sinf):
    return x * cosf + pltpu.roll(x, HEAD_DIM // 2, 1) * sinf


def _fold_lanes(x):
    parts = [x[:, c * LANE:(c + 1) * LANE] for c in range(x.shape[1] // LANE)]
    return functools.reduce(lambda u, v: u + v, parts)


def _norm_producer(epilogue, n_base):
    def wrapped(acc, rows, *extras):
        (h,) = epilogue(acc, rows, *extras[:n_base])
        return (h, *[(h * g[...]).astype(BF16) for g in extras[n_base:]], _fold_lanes(h * h))
    return wrapped


def _row_scaled(epilogue):
    def wrapped(acc, rows, rs_ref, *extras):
        rs = rs_ref[rows, :]
        acc = jnp.concatenate([acc[:, c * LANE:(c + 1) * LANE] * rs for c in range(acc.shape[1] // LANE)], axis=1)
        return epilogue(acc, rows, *extras)
    return wrapped


def _row_scale_kernel(ssq_ref, o_ref, *, d):
    tot = jnp.sum(functools.reduce(lambda u, v: u + v, [ssq_ref[j] for j in range(ssq_ref.shape[0])]),
                  axis=1, keepdims=True)
    o_ref[...] = jnp.broadcast_to(lax.rsqrt(tot / d + EPS), o_ref.shape)


def row_scale(ssq, d):
    gn, M, _ = ssq.shape
    tm = min(M, 1024)
    return pl.pallas_call(
        functools.partial(_row_scale_kernel, d=d),
        grid=(M // tm,),
        in_specs=[pl.BlockSpec((gn, tm, LANE), lambda i: (0, i, 0))],
        out_specs=pl.BlockSpec((tm, LANE), lambda i: (i, 0)),
        out_shape=jax.ShapeDtypeStruct((M, LANE), F32),
        compiler_params=_params(("parallel",)),
        name="row_scale",
    )(ssq)


def _norm_io(gains, M, N, gn, tm, tn):
    extras = [g.reshape(1, N).astype(F32) for g in gains]
    extra_specs = [pl.BlockSpec((1, tn), lambda i, j, k: (0, j)) for _ in gains]
    shapes = [jax.ShapeDtypeStruct((M, N), BF16) for _ in gains] + [jax.ShapeDtypeStruct((gn, M, LANE), F32)]
    specs = [_mn_spec(tm, tn) for _ in gains] + [pl.BlockSpec((1, tm, LANE), lambda i, j, k: (j, i, 0))]
    return extras, extra_specs, shapes, specs


EPILOGUE_ROWS = 256


def _finish_rows(acc_of, tm, outs, extras, epilogue):
    ch = min(tm, EPILOGUE_ROWS)
    for c in range(tm // ch):
        rows = slice(c * ch, (c + 1) * ch)
        for o_ref, r in zip(outs, epilogue(acc_of(rows), rows, *extras)):
            if len(o_ref.shape) == 4:
                for hh in range(o_ref.shape[2]):
                    o_ref[0, rows, hh, :] = r[:, hh * HEAD_DIM:(hh + 1) * HEAD_DIM].astype(o_ref.dtype)
            elif len(o_ref.shape) == 3:
                o_ref[0, rows, :] = r.astype(o_ref.dtype)
            else:
                o_ref[rows, :] = r.astype(o_ref.dtype)


def _mm_kernel(*refs, nk, n_extra, n_out, epilogue):
    x_ref, w_ref = refs[0], refs[1]
    extras = refs[2:2 + n_extra]
    outs = refs[2 + n_extra:2 + n_extra + n_out]
    tm = x_ref.shape[0]

    if nk == 1:
        _finish_rows(lambda rows: _dot(x_ref[rows, :], w_ref[...]), tm, outs, extras, epilogue)
    else:
        acc_ref = refs[-1]
        k = pl.program_id(2)

        @pl.when(k == 0)
        def _():
            acc_ref[...] = jnp.zeros_like(acc_ref)

        acc_ref[...] += _dot(x_ref[...], w_ref[...])

        @pl.when(k == nk - 1)
        def _():
            _finish_rows(lambda rows: acc_ref[rows, :], tm, outs, extras, epilogue)


def mm(x, w, *, grid, tm, tn, tk, epilogue, extras=(), extra_specs=(), out_shapes, out_specs,
       x_map=None, w_map=None, name):
    nk = grid[2]
    x_map = x_map or (lambda i, j, k: (i, k))
    w_map = w_map or (lambda i, j, k: (k, j))
    kern = functools.partial(_mm_kernel, nk=nk, n_extra=len(extras), n_out=len(out_shapes), epilogue=epilogue)
    return pl.pallas_call(
        kern,
        grid=grid,
        in_specs=[pl.BlockSpec((tm, tk), x_map), pl.BlockSpec((tk, tn), w_map)] + list(extra_specs),
        out_specs=list(out_specs),
        out_shape=list(out_shapes),
        scratch_shapes=[pltpu.VMEM((tm, tn), F32)] if nk > 1 else [],
        compiler_params=_params(("parallel", "parallel", "arbitrary")),
        name=name,
    )(x, w, *extras)


def _mm_ws_kernel(*refs, nk, n_extra, n_out, epilogue, emit, side):
    x_ref, w_ref = refs[0], refs[1]
    extras = refs[2:2 + n_extra]
    n_in = 2 + n_extra + side
    outs = refs[n_in:n_in + n_out]
    rest = refs[n_in + n_out:]
    wb_ref = rest[emit + side]
    i, k = pl.program_id(1), pl.program_id(2)

    @pl.when(i == 0)
    def _():
        wb_ref[k] = w_ref[...].astype(BF16)
        if emit:
            rest[0][...] = wb_ref[k]

    if side:
        rest[emit][...] = refs[n_in - 1][...].astype(BF16)

    tm = x_ref.shape[0]
    if nk == 1:
        _finish_rows(lambda rows: _dot(x_ref[rows, :], wb_ref[0]), tm, outs, extras, epilogue)
    else:
        acc_ref = rest[-1]

        @pl.when(k == 0)
        def _():
            acc_ref[...] = jnp.zeros_like(acc_ref)

        acc_ref[...] += _dot(x_ref[...], wb_ref[k])

        @pl.when(k == nk - 1)
        def _():
            _finish_rows(lambda rows: acc_ref[rows, :], tm, outs, extras, epilogue)


def mm_ws(x, w, *, layer=None, emit=False, side=None, grid, tm, tn, tk, epilogue, extras=(), extra_specs=(),
          out_shapes, out_specs, name):
    gm, gn, nk = grid
    swap = lambda f: (lambda j, i, k: f(i, j, k))
    respec = lambda s: pl.BlockSpec(s.block_shape, swap(s.index_map))
    k_once = lambda i, k: jnp.where(i == 0, k, nk - 1)
    if layer is None:
        w_spec = pl.BlockSpec((tk, tn), lambda j, i, k: (k_once(i, k), j))
    else:
        w_spec = pl.BlockSpec((None, tk, tn), lambda j, i, k: (layer, k_once(i, k), j))
    n_out = len(out_shapes)
    ins, in_specs = [x, w, *extras], [pl.BlockSpec((tm, tk), lambda j, i, k: (i, k)), w_spec]
    in_specs += [respec(s) for s in extra_specs]
    out_shapes, out_specs = list(out_shapes), [respec(s) for s in out_specs]
    if emit:
        out_shapes.append(jax.ShapeDtypeStruct((nk * tk, gn * tn), BF16))
        out_specs.append(pl.BlockSpec((tk, tn), lambda j, i, k: (k_once(i, k), j)))
    if side is not None:
        s_arr, s_layer = side
        _, rows, cols = s_arr.shape
        rs = rows // (gn * gm * nk)
        assert rs * gn * gm * nk == rows and rs % 16 == 0
        step = lambda j, i, k: (j * gm + i) * nk + k
        ins.append(s_arr)
        in_specs.append(pl.BlockSpec((None, rs, cols), lambda j, i, k: (s_layer, step(j, i, k), 0)))
        out_shapes.append(jax.ShapeDtypeStruct((rows, cols), BF16))
        out_specs.append(pl.BlockSpec((rs, cols), lambda j, i, k: (step(j, i, k), 0)))
    kern = functools.partial(_mm_ws_kernel, nk=nk, n_extra=len(extras), n_out=n_out, epilogue=epilogue, emit=emit,
                             side=side is not None)
    return pl.pallas_call(
        kern,
        grid=(gn, gm, nk),
        in_specs=in_specs,
        out_specs=out_specs,
        out_shape=out_shapes,
        scratch_shapes=[pltpu.VMEM((nk, tk, tn), BF16)] + ([pltpu.VMEM((tm, tn), F32)] if nk > 1 else []),
        compiler_params=_params(("parallel", "arbitrary", "arbitrary")),
        name=name,
    )(*ins)


def _tiles(M, N, K, tm, tn, tk):
    tm, tn, tk = min(tm, M), min(tn, N), min(tk, K)
    return (M // tm, N // tn, K // tk), tm, tn, tk


def _mn_spec(tm, tn):
    return pl.BlockSpec((tm, tn), lambda i, j, k: (i, j))


def _pool_diff_kernel(*refs, tt, pos0, halo):
    x_ref, pre_ref, g_ref = refs[0], refs[1 + halo], refs[2 + halo]
    d_ref, st_ref, seq_ref = refs[3 + halo:]
    t = pl.program_id(1)
    norm = lambda x: x * lax.rsqrt(jnp.mean(x * x, axis=-1, keepdims=True) + EPS) * g_ref[...]
    a = norm(x_ref[0])
    seq_ref[POOL_PAD:POOL_PAD + tt, :] = a

    @pl.when(t == 0)
    def _():
        seq_ref[0:POOL_PAD, :] = pre_ref[0]

    if halo:
        @pl.when(t > 0)
        def _():
            seq_ref[0:POOL_PAD, :] = norm(refs[1][0])

    pos = pos0 + t * tt + lax.broadcasted_iota(jnp.int32, (tt, 1), 0)
    pg = a.shape[1] // len(POOL_WINDOWS)
    for g, w in enumerate(POOL_WINDOWS):
        cols = slice(g * pg, (g + 1) * pg)
        s = a[:, cols]
        for j in range(1, w):
            s = s + seq_ref[POOL_PAD - j:POOL_PAD - j + tt, cols]
        cnt = jnp.minimum(pos + 1, w).astype(F32)
        d_ref[0, :, cols] = (s / cnt - a[:, cols]).astype(d_ref.dtype)
    st_ref[0] = seq_ref[tt:tt + POOL_PAD, :]


def pool_diff(x, gain, prefix, pos0):
    B, T, D = x.shape
    tt = min(T, 256)
    halo = T > tt
    pre = jnp.concatenate([jnp.zeros((B, 1, D), F32), prefix], axis=1)
    hpt = tt // POOL_PAD
    in_specs = [pl.BlockSpec((1, tt, D), lambda b, t: (b, t, 0))]
    if halo:
        in_specs.append(pl.BlockSpec((1, POOL_PAD, D), lambda b, t: (b, jnp.maximum(t * hpt - 1, 0), 0)))
    in_specs += [pl.BlockSpec((1, POOL_PAD, D), lambda b, t: (b, 0, 0)), pl.BlockSpec((1, D), lambda b, t: (0, 0))]
    d, st = pl.pallas_call(
        functools.partial(_pool_diff_kernel, tt=tt, pos0=pos0, halo=halo),
        grid=(B, T // tt),
        in_specs=in_specs,
        out_specs=[pl.BlockSpec((1, tt, D), lambda b, t: (b, t, 0)),
                   pl.BlockSpec((1, POOL_PAD, D), lambda b, t: (b, 0, 0))],
        out_shape=[jax.ShapeDtypeStruct((B, T, D), BF16), jax.ShapeDtypeStruct((B, POOL_PAD, D), F32)],
        scratch_shapes=[pltpu.VMEM((POOL_PAD + tt, D), F32)],
        compiler_params=_params(("parallel", "arbitrary")),
        name="pool_diff",
    )(*([x, x] if halo else [x]), pre, gain.reshape(1, D).astype(F32))
    return d, st[:, 1:]


PAGES_PER_STEP = 8
CMP_FINISH_ROWS = 512


def _cmp_partial_kernel(pt_ref, *refs, steps, pps):
    del pt_ref
    k_pages, v_pages = refs[:pps], refs[pps:2 * pps]
    wk_ref, wv_ref, abk_ref, abv_ref, xk_ref, xv_ref = refs[2 * pps:]
    p = pl.program_id(2)
    rows_per_page = (k_pages[0].shape[1] // CMP_STRIDE) * N_KV_HEADS
    for pages, x_ref in ((k_pages, xk_ref), (v_pages, xv_ref)):
        for q, page_ref in enumerate(pages):
            for sb in range(page_ref.shape[1] // CMP_STRIDE):
                row = q * rows_per_page + sb * N_KV_HEADS
                for r in range(CMP_STRIDE):
                    x_ref[p, row:row + N_KV_HEADS, r * HEAD_DIM:(r + 1) * HEAD_DIM] = page_ref[0, sb * CMP_STRIDE + r]

    @pl.when(p == steps - 1)
    def _():
        for x_ref, w_ref, ab_ref in ((xk_ref, wk_ref, abk_ref), (xv_ref, wv_ref, abv_ref)):
            x = x_ref[...].reshape(ab_ref.shape[1], x_ref.shape[2])
            ab_ref[0] = _dot(x.astype(BF16), w_ref[...])


def cmp_partials(k_pages, v_pages, table, wk_cat, wv_cat, cp):
    B, ppb = table.shape
    page = k_pages.shape[1]
    pps = PAGES_PER_STEP
    rows_per_page = page // CMP_STRIDE * N_KV_HEADS
    nch, steps = ppb // cp, cp // pps
    m = cp * rows_per_page
    kdim = CMP_STRIDE * HEAD_DIM

    def page_spec(q):
        return pl.BlockSpec((1, page, N_KV_HEADS, HEAD_DIM),
                            lambda b, c, p, pt: (pt[b, c * cp + p * pps + q], 0, 0, 0))

    page_specs = [page_spec(q) for q in range(pps)]
    w_spec = pl.BlockSpec((kdim, 2 * CMP_HIDDEN), lambda b, c, p, pt: (0, 0))
    out_spec = pl.BlockSpec((1, m, 2 * CMP_HIDDEN), lambda b, c, p, pt: (b, c, 0))
    out_shape = jax.ShapeDtypeStruct((B, ppb * rows_per_page, 2 * CMP_HIDDEN), F32)
    x_scratch = pltpu.VMEM((steps, pps * rows_per_page, kdim), F32)
    return pl.pallas_call(
        functools.partial(_cmp_partial_kernel, steps=steps, pps=pps),
        grid_spec=pltpu.PrefetchScalarGridSpec(
            num_scalar_prefetch=1,
            grid=(B, nch, steps),
            in_specs=page_specs + page_specs + [w_spec, w_spec],
            out_specs=[out_spec, out_spec],
            scratch_shapes=[x_scratch, x_scratch],
        ),
        out_shape=[out_shape, out_shape],
        compiler_params=_params(("parallel", "parallel", "arbitrary")),
        name="cmp_partials",
    )(table, *([k_pages] * pps), *([v_pages] * pps), wk_cat, wv_cat)


def _cmp_finish_kernel(ab_ref, pe_ref, w1_ref, w2_ref, g_ref, o_ref, out_ref, *, norm):
    n4 = ab_ref.shape[1]
    G = o_ref.shape[1]
    bias = _dot(jnp.broadcast_to(pe_ref[...], (8, pe_ref.shape[1])), w1_ref[...])[0:1, :]
    ch = min(n4, CMP_FINISH_ROWS)
    for c in range(n4 // ch):
        lo, hi = c * ch, (c + 1) * ch
        first = ab_ref[0, lo:hi, :CMP_HIDDEN]
        if hi + G <= n4:
            second = ab_ref[0, lo + G:hi + G, CMP_HIDDEN:]
        else:
            second = jnp.concatenate([ab_ref[0, lo + G:hi, CMP_HIDDEN:], ab_ref[0, hi - G:hi, CMP_HIDDEN:]], axis=0)
        pre = first + second + bias
        out = _dot((pre * _sigmoid(pre)).astype(BF16), w2_ref[...])
        if norm:
            out = _head_norm(out, g_ref[...])
        out_ref[lo:hi, :] = out
    for g in range(G):
        o_ref[0, g] = out_ref[pl.ds(g, n4 // G, stride=G), :].astype(o_ref.dtype)


def cmp_finish(ab, pe, w1, w2, gain, *, norm):
    B, n4, _ = ab.shape
    G = N_KV_HEADS
    full = lambda a: pl.BlockSpec(a.shape, lambda b: (0,) * a.ndim)
    args = (pe, w1, w2, gain)
    return pl.pallas_call(
        functools.partial(_cmp_finish_kernel, norm=norm),
        grid=(B,),
        in_specs=[pl.BlockSpec((1, n4, 2 * CMP_HIDDEN), lambda b: (b, 0, 0))] + [full(a) for a in args],
        out_specs=pl.BlockSpec((1, G, n4 // G, HEAD_DIM), lambda b: (b, 0, 0, 0)),
        out_shape=jax.ShapeDtypeStruct((B, G, n4 // G, HEAD_DIM), BF16),
        scratch_shapes=[pltpu.VMEM((n4, HEAD_DIM), F32)],
        compiler_params=_params(("parallel",)),
        name="cmp_finish",
    )(ab, *args)


def _select_blocks(score, blk, ns):
    rank = jnp.zeros(score.shape, jnp.int32)
    for j in range(ns):
        sj = score[j:j + 1, :]
        beats = (sj > score) | ((sj == score) & (j < blk))
        rank = rank + beats.astype(jnp.int32)
    return (rank < min(N_SEL, ns)) & (score > 0.5 * NEG)


def _attn_prompt_kernel(q_ref, qr_ref, ck_ref, cv_ref, ks_ref, vs_ref, kw_ref, vw_ref, gate_ref, mapT_ref,
                        expand_ref, o_ref, part_ref, sbias_ref, wbias_ref, m_ref, acc_ref, *, tq, kc, nc, ns, R):
    qi = pl.program_id(2)
    q0 = qi * tq
    pos = q0 + lax.broadcasted_iota(jnp.int32, (tq, 1), 0)
    ncp = ck_ref.shape[2]

    ck = ck_ref[0, 0]
    cv = cv_ref[0, 0]
    cidx = lax.broadcasted_iota(jnp.int32, (1, ncp), 1)
    ok_c = (cidx * CMP_STRIDE + CMP_BLOCK - 1 <= pos) & (cidx < nc)
    imp = jnp.zeros((tq, ncp), F32)
    for r in range(R):
        qh = q_ref[0, :, r * HEAD_DIM:(r + 1) * HEAD_DIM]
        s = jnp.where(ok_c, _dot_nt(qh, ck) * SCALE, NEG)
        e = jnp.exp(s - jnp.max(s, axis=1, keepdims=True))
        p = jnp.where(ok_c, e * (1.0 / jnp.sum(e, axis=1, keepdims=True)), 0.0)
        imp = imp + p
        part_ref[r] = gate_ref[0, :, r * N_BRANCH:r * N_BRANCH + 1] * _dot(p.astype(BF16), cv)

    nsp = mapT_ref.shape[0]
    p_slc = lax.dot_general(mapT_ref[...], imp, (((1,), (1,)), ((), ())), precision=lax.Precision.HIGHEST,
                            preferred_element_type=F32)
    blk = lax.broadcasted_iota(jnp.int32, (nsp, tq), 0)
    pos_l = q0 + lax.broadcasted_iota(jnp.int32, (nsp, tq), 1)
    cur = pos_l // SEL_BLOCK
    vis = blk * SEL_BLOCK <= pos_l
    forced = vis & ((blk == 0) | (blk == cur) | (blk == cur - 1))
    score = jnp.where(forced, FORCE, jnp.where(vis, p_slc, NEG))
    score = jnp.where(blk < ns, score, PAD_SCORE)
    sel = _select_blocks(score, blk, ns).astype(BF16)

    c_hi = (q0 + tq) // kc
    col = lax.broadcasted_iota(jnp.int32, (1, kc), 1)
    sel_keys = _dot_tn(sel, expand_ref[...])
    for c in range(sbias_ref.shape[0]):
        @pl.when(c < c_hi)
        def _(c=c):
            ok = (sel_keys[:, c * kc:(c + 1) * kc] > 0.5) & (c * kc + col <= pos)
            sbias_ref[c] = jnp.where(ok, 0.0, NEG)
    nwc = wbias_ref.shape[0]
    c_w0 = c_hi - nwc
    for d in range(nwc):
        kpos = (c_w0 + d) * kc + col
        wbias_ref[d] = jnp.where((kpos <= pos) & (kpos > pos - WINDOW), 0.0, NEG)

    def fold(t):
        return [t[:, i * LANE:(i + 1) * LANE] for i in range(kc // LANE)]

    ones_col = (lax.broadcasted_iota(jnp.int32, (kc, LANE), 1) == 0).astype(BF16)

    def branch(k_ref, v_ref, c_lo, bias_of, gate_col):
        def logits(r, c, bias):
            k = k_ref[0, pl.ds(pl.multiple_of(c * kc, kc), kc), :]
            return _dot_nt(qr_ref[0, :, r * HEAD_DIM:(r + 1) * HEAD_DIM], k) * SCALE + bias

        m_ref[...] = jnp.full(m_ref.shape, NEG, F32)
        acc_ref[...] = jnp.zeros(acc_ref.shape, F32)

        def max_body(c, carry):
            bias = bias_of(c)
            for r in range(R):
                mx = m_ref[r]
                for part in fold(logits(r, c, bias)):
                    mx = jnp.maximum(mx, part)
                m_ref[r] = mx
            return carry

        lax.fori_loop(c_lo, c_hi, max_body, 0)
        for r in range(R):
            m_ref[r] = jnp.broadcast_to(jnp.max(m_ref[r], axis=1, keepdims=True), (tq, LANE))

        def sum_body(c, carry):
            bias = bias_of(c)
            v = jnp.concatenate([v_ref[0, pl.ds(pl.multiple_of(c * kc, kc), kc), :], ones_col], axis=1)
            for r in range(R):
                t = logits(r, c, bias)
                m = m_ref[r]
                ps = [jnp.exp(part - m) for part in fold(t)]
                acc_ref[r] += _dot(jnp.concatenate(ps, axis=1).astype(BF16), v)
            return carry

        lax.fori_loop(c_lo, c_hi, sum_body, 0)
        for r in range(R):
            gate = gate_ref[0, :, r * N_BRANCH + gate_col:r * N_BRANCH + gate_col + 1]
            part_ref[r] += (gate * (1.0 / acc_ref[r, :, HEAD_DIM:HEAD_DIM + 1])) * acc_ref[r, :, :HEAD_DIM]

    branch(ks_ref, vs_ref, 0, lambda c: sbias_ref[c], 1)
    branch(kw_ref, vw_ref, jnp.maximum(c_w0, 0), lambda c: wbias_ref[c - c_w0], 2)
    for r in range(R):
        o_ref[0, :, r * HEAD_DIM:(r + 1) * HEAD_DIM] = part_ref[r].astype(o_ref.dtype)


def _overlap_map(ncp, nsp, ns):
    ratio = CMP_BLOCK // CMP_STRIDE
    per_sel = SEL_BLOCK // CMP_STRIDE
    m = np.zeros((ncp, nsp), np.float32)
    for b in range(ns):
        for mm_ in range(per_sel):
            for n in range(ratio):
                j = per_sel * b + mm_ - n
                if 0 <= j < ncp:
                    m[j, b] += 1.0
    return m


def attn_prompt(q, qr, ck, cv, kvb, gates, *, nc):
    B, T, HD = q.shape
    G = N_KV_HEADS
    R = HD // HEAD_DIM // G
    tq = min(T, 256)
    kc = tq
    ns = T // SEL_BLOCK
    nsp = -(-ns // 8) * 8
    ncp = ck.shape[2]
    mapT = jnp.asarray(_overlap_map(ncp, nsp, ns).T)
    expand = jnp.asarray((np.arange(T)[None, :] // SEL_BLOCK == np.arange(nsp)[:, None]).astype(np.float32), BF16)
    q_spec = pl.BlockSpec((1, tq, R * HEAD_DIM), lambda b, g, i: (b, i, g))
    c_spec = pl.BlockSpec((1, 1, ncp, HEAD_DIM), lambda b, g, i: (b, g, 0, 0))
    kv_spec = lambda n: pl.BlockSpec((None, 1, T, HEAD_DIM), lambda b, g, i: (n, b, 0, g))
    return pl.pallas_call(
        functools.partial(_attn_prompt_kernel, tq=tq, kc=kc, nc=nc, ns=ns, R=R),
        grid=(B, G, T // tq),
        in_specs=[q_spec, q_spec, c_spec, c_spec, kv_spec(2), kv_spec(3), kv_spec(4), kv_spec(5),
                  pl.BlockSpec((1, tq, LANE), lambda b, g, i: (b, i, g)),
                  pl.BlockSpec((nsp, ncp), lambda b, g, i: (0, 0)),
                  pl.BlockSpec((nsp, T), lambda b, g, i: (0, 0))],
        out_specs=q_spec,
        out_shape=jax.ShapeDtypeStruct((B, T, HD), BF16),
        scratch_shapes=[pltpu.VMEM((R, tq, HEAD_DIM), F32), pltpu.VMEM((T // kc, tq, kc), F32),
                        pltpu.VMEM((min(WINDOW, T) // kc + tq // kc, tq, kc), F32)]
                       + [pltpu.VMEM((R, tq, HEAD_DIM), F32), pltpu.VMEM((R, tq, HEAD_DIM + LANE), F32)],
        compiler_params=_params(("parallel", "parallel", "arbitrary")),
        name="attn_prompt",
    )(q, qr, ck, cv, kvb, kvb, kvb, kvb, gates, mapT, expand)


def _attn_dec_dense_kernel(q_ref, qr_ref, ck_ref, cv_ref, kw_ref, vw_ref, map_ref, ocmp_ref, owin_ref, ids_ref,
                           *, nc, ns, pos):
    G, R = q_ref.shape[1], q_ref.shape[2]
    ncp = ck_ref.shape[2]
    nsl = map_ref.shape[1]
    cidx = lax.broadcasted_iota(jnp.int32, (1, ncp), 1)
    ok_c = (cidx * CMP_STRIDE + CMP_BLOCK - 1 <= pos) & (cidx < nc)
    blk_l = lax.broadcasted_iota(jnp.int32, (1, nsl), 1)
    cur = pos // SEL_BLOCK
    vis = blk_l * SEL_BLOCK <= pos
    forced = vis & ((blk_l == 0) | (blk_l == cur) | (blk_l == cur - 1))
    ii = lax.broadcasted_iota(jnp.int32, (nsl, nsl), 0)
    jj = lax.broadcasted_iota(jnp.int32, (nsl, nsl), 1)
    slot = lax.broadcasted_iota(jnp.int32, (nsl, LANE), 1).astype(F32)
    blk_s = lax.broadcasted_iota(jnp.int32, (nsl, LANE), 0).astype(F32)
    for g in range(G):
        s = jnp.where(ok_c, _dot_nt(q_ref[0, g], ck_ref[0, g]) * SCALE, NEG)
        e = jnp.exp(s - jnp.max(s, axis=1, keepdims=True))
        p = jnp.where(ok_c, e * (1.0 / jnp.sum(e, axis=1, keepdims=True)), 0.0)
        ocmp_ref[0, g] = _dot(p.astype(BF16), cv_ref[0, g])
        imp = jnp.broadcast_to(jnp.sum(p, axis=0, keepdims=True), (R, ncp))
        p_slc = jnp.dot(imp, map_ref[...], precision=lax.Precision.HIGHEST, preferred_element_type=F32)[0:1, :]
        score_l = jnp.where(forced, FORCE, jnp.where(vis, p_slc, NEG))
        score_l = jnp.where(blk_l < ns, score_l, PAD_SCORE)
        score_s = jnp.sum(jnp.where(ii == jj, score_l, 0.0), axis=1, keepdims=True)
        beats = (score_l > score_s) | ((score_l == score_s) & (jj < ii))
        rank = jnp.sum(beats.astype(F32), axis=1, keepdims=True)
        ids = jnp.sum(jnp.where(rank == slot, blk_s, 0.0), axis=0, keepdims=True)
        ids_ref[0, g] = ids[:, :N_SEL].astype(jnp.int32)
        kw = kw_ref[0, :, g, :].astype(BF16)
        vw = vw_ref[0, :, g, :].astype(BF16)
        s = _dot_nt(qr_ref[0, g], kw) * SCALE
        e = jnp.exp(s - jnp.max(s, axis=1, keepdims=True))
        p = e * (1.0 / jnp.sum(e, axis=1, keepdims=True))
        owin_ref[0, g] = _dot(p.astype(BF16), vw)


def attn_dec_dense(q, qr, ck, cv, kw, vw, *, nc, ns, pos):
    B, G, R, _ = q.shape
    ncp = ck.shape[2]
    wb = kw.shape[1]
    nsl = -(-ns // LANE) * LANE
    omap = jnp.asarray(_overlap_map(ncp, nsl, ns))
    q_spec = pl.BlockSpec((1, G, R, HEAD_DIM), lambda b: (b, 0, 0, 0))
    c_spec = pl.BlockSpec((1, G, ncp, HEAD_DIM), lambda b: (b, 0, 0, 0))
    w_spec = pl.BlockSpec((1, wb, G, HEAD_DIM), lambda b: (b, 0, 0, 0))
    return pl.pallas_call(
        functools.partial(_attn_dec_dense_kernel, nc=nc, ns=ns, pos=pos),
        grid=(B,),
        in_specs=[q_spec, q_spec, c_spec, c_spec, w_spec, w_spec, pl.BlockSpec((ncp, nsl), lambda b: (0, 0))],
        out_specs=[q_spec, q_spec, pl.BlockSpec((1, G, 1, N_SEL), lambda b: (b, 0, 0, 0))],
        out_shape=[jax.ShapeDtypeStruct((B, G, R, HEAD_DIM), F32), jax.ShapeDtypeStruct((B, G, R, HEAD_DIM), F32),
                   jax.ShapeDtypeStruct((B, G, 1, N_SEL), jnp.int32)],
        compiler_params=_params(("parallel",)),
        name="attn_dec_dense",
    )(q, qr, ck, cv, kw, vw, omap)


def _attn_dec_sel_kernel(pt_ref, ids_ref, qr_ref, *refs, n_past, pos):
    del pt_ref
    G = qr_ref.shape[1]
    kc_refs, vc_refs = refs[:G], refs[G:2 * G]
    kn_ref, vn_ref, ocmp_ref, owin_ref, gate_ref, o_ref, m_ref, l_ref, acc_ref = refs[2 * G:]
    b, n = pl.program_id(0), pl.program_id(1)

    @pl.when(n == 0)
    def _():
        m_ref[...] = jnp.full(m_ref.shape, NEG, F32)
        l_ref[...] = jnp.zeros(l_ref.shape, F32)
        acc_ref[...] = jnp.zeros(acc_ref.shape, F32)

    row = lax.broadcasted_iota(jnp.int32, (SEL_BLOCK, 1), 0)
    for g in range(G):
        bid = ids_ref[(b * G + g) * N_SEL + n]
        is_new = bid >= n_past
        first = (row == 0) & (bid == n_past)
        sl = slice(g * HEAD_DIM, (g + 1) * HEAD_DIM)
        k = jnp.where(is_new, jnp.where(first, kn_ref[0, :, sl], 0.0), kc_refs[g][0, :, g, :]).astype(BF16)
        v = jnp.where(is_new, jnp.where(first, vn_ref[0, :, sl], 0.0), vc_refs[g][0, :, g, :]).astype(BF16)
        kpos = bid * SEL_BLOCK + lax.broadcasted_iota(jnp.int32, (1, SEL_BLOCK), 1)
        ok = kpos <= pos
        s = jnp.where(ok, _dot_nt(qr_ref[0, g], k) * SCALE, NEG)
        m_prev = m_ref[g]
        m_new = jnp.maximum(m_prev, jnp.max(s, axis=1, keepdims=True))
        alpha = jnp.exp(m_prev - m_new)
        p = jnp.where(ok, jnp.exp(s - m_new), 0.0)
        l_ref[g] = alpha * l_ref[g] + jnp.sum(p, axis=1, keepdims=True)
        acc_ref[g] = alpha * acc_ref[g] + _dot(p.astype(BF16), v)
        m_ref[g] = m_new

    @pl.when(n == N_SEL - 1)
    def _():
        for g in range(G):
            gt = gate_ref[0, g]
            o_sel = acc_ref[g] * (1.0 / l_ref[g])
            o_ref[0, g] = gt[:, 0:1] * ocmp_ref[0, g] + gt[:, 1:2] * o_sel + gt[:, 2:3] * owin_ref[0, g]


def attn_dec_sel(table, ids, qr, k_cache, v_cache, k_new, v_new, ocmp, owin, gates, *, pos):
    B, G, R, _ = qr.shape
    page = k_cache.shape[1]
    bpp = page // SEL_BLOCK
    n_past = table.shape[1] * bpp
    kc = k_cache.reshape(k_cache.shape[0] * bpp, SEL_BLOCK, G, HEAD_DIM)
    vc = v_cache.reshape(v_cache.shape[0] * bpp, SEL_BLOCK, G, HEAD_DIM)

    def cache_spec(g):
        def index(b, n, pt, ids_):
            bid = jnp.minimum(ids_[(b * G + g) * N_SEL + n], n_past - 1)
            return (pt[b, bid // bpp] * bpp + bid % bpp, 0, 0, 0)
        return pl.BlockSpec((1, SEL_BLOCK, G, HEAD_DIM), index)

    q_spec = pl.BlockSpec((1, G, R, HEAD_DIM), lambda b, n, pt, ids_: (b, 0, 0, 0))
    n_spec = pl.BlockSpec((1, 1, G * HEAD_DIM), lambda b, n, pt, ids_: (b, 0, 0))
    g_spec = pl.BlockSpec((1, G, R, LANE), lambda b, n, pt, ids_: (b, 0, 0, 0))
    c_specs = [cache_spec(g) for g in range(G)]
    return pl.pallas_call(
        functools.partial(_attn_dec_sel_kernel, n_past=n_past, pos=pos),
        grid_spec=pltpu.PrefetchScalarGridSpec(
            num_scalar_prefetch=2,
            grid=(B, N_SEL),
            in_specs=[q_spec] + c_specs + c_specs + [n_spec, n_spec, q_spec, q_spec, g_spec],
            out_specs=q_spec,
            scratch_shapes=[pltpu.VMEM((G, R, 1), F32), pltpu.VMEM((G, R, 1), F32),
                            pltpu.VMEM((G, R, HEAD_DIM), F32)],
        ),
        out_shape=jax.ShapeDtypeStruct((B, G, R, HEAD_DIM), F32),
        compiler_params=_params(("parallel", "arbitrary")),
        name="attn_dec_sel",
    )(table, ids.reshape(-1), qr, *([kc] * G), *([vc] * G), k_new, v_new, ocmp, owin, gates)


def _rope_tables(pos):
    half = HEAD_DIM // 2
    inv = ROPE_THETA ** (-jnp.arange(half, dtype=F32) / half)
    ang = pos.astype(F32)[:, None] * inv[None, :]
    cos, sin = jnp.cos(ang), jnp.sin(ang)
    return jnp.concatenate([cos, cos], axis=1), jnp.concatenate([-sin, sin], axis=1)


def _prep_weights(W):
    D = W['w_kv'].shape[0]
    H = D // HEAD_DIM
    R = H // N_KV_HEADS
    pg = D // len(POOL_WINDOWS)
    half = CMP_STRIDE * HEAD_DIM
    P = {}
    P['w_pool'] = W['w_pool'].astype(BF16).reshape(-1, len(POOL_WINDOWS) * pg, pg)
    kv_gain = jnp.ones((W['w_kv'].shape[1], KV_W), F32)
    kv_gain = kv_gain.at[2].set(jnp.tile(W['g_k_sel'], N_KV_HEADS)).at[4].set(jnp.tile(W['g_k_win'], N_KV_HEADS))
    P['kv_gain'] = kv_gain
    for t in ('k', 'v'):
        w1 = W['w_cmp_%s1' % t].astype(BF16)
        P['w_cmp_%s1' % t] = w1
        P['w_cmp_%scat' % t] = jnp.concatenate([w1[:half], w1[half:]], axis=1)
        P['w_cmp_%s2' % t] = W['w_cmp_%s2' % t].astype(BF16)
        P['pe_%s' % t] = W['pe_cmp_%s' % t].astype(BF16).reshape(1, -1)
    n_b = W['w_qg'].shape[0]
    wg = W['w_qg'][:, :, H * HEAD_DIM:].astype(BF16).reshape(n_b, D, N_KV_HEADS, R * N_BRANCH)
    wg = jnp.pad(wg, ((0, 0), (0, 0), (0, 0), (0, LANE - R * N_BRANCH)))
    P['w_gate'] = wg.reshape(n_b, D, N_KV_HEADS * LANE)
    P['w_ple'] = W['w_ple'].astype(BF16)
    return P


def _dense(x, wname, layer, W, P, *, N, epilogue, specs, extras=(), out_dtypes, emit=False, side_cast=None, row_scale_in=None,
           norm_gains=None, name):
    M, K = x.shape
    shapes = [d if isinstance(d, jax.ShapeDtypeStruct) else jax.ShapeDtypeStruct((M, N), d) for d in out_dtypes]
    key = (wname, layer)
    tmx = 1024 if M >= 1024 else M
    use_mm = key in P
    if use_mm:
        grid, tm, tn, tk = _tiles(M, N, K, tmx, 1024, 2048 if M >= 1024 else 4096)
    else:
        grid, tm, tn, tk = _tiles(M, N, K, tmx, 512, K if K <= 4096 else 2048)
    extra_specs, out_specs = specs(tm, tn)
    extras = list(extras)
    if row_scale_in is not None:
        epilogue = _row_scaled(epilogue)
        extras.insert(0, row_scale_in)
        extra_specs = [pl.BlockSpec((tm, LANE), lambda i, j, k: (i, 0))] + list(extra_specs)
    if norm_gains:
        epilogue = _norm_producer(epilogue, len(extras))
        n_ex, n_ex_specs, n_shapes, n_specs = _norm_io(norm_gains, M, N, grid[1], tm, tn)
        extras, extra_specs = extras + n_ex, list(extra_specs) + n_ex_specs
        shapes, out_specs = shapes + n_shapes, list(out_specs) + n_specs
    if use_mm:
        outs = list(mm(x, P[key], grid=grid, tm=tm, tn=tn, tk=tk, epilogue=epilogue, extras=extras,
                       extra_specs=extra_specs, out_shapes=shapes, out_specs=out_specs, name=name))
    else:
        w = W[wname]
        side = None if side_cast is None else (W[side_cast], layer)
        outs = list(mm_ws(x, w, layer=layer if w.ndim == 3 else None, emit=emit, side=side, grid=grid, tm=tm,
                          tn=tn, tk=tk, epilogue=epilogue, extras=extras, extra_specs=extra_specs,
                          out_shapes=shapes, out_specs=out_specs, name=name))
        if side is not None:
            P[(side_cast, layer)] = outs.pop()
        if emit:
            P[key] = outs.pop()
    if norm_gains:
        outs.append(row_scale(outs.pop(), N))
    return outs


def _ffn_ple(h, normed, p_l, layer, W, P, next_gains):
    M, D = h.shape
    F = W['w_up'].shape[2]
    emit = M >= 1024
    mn = lambda tm, tn: ([], [_mn_spec(tm, tn)])
    res = lambda tm, tn: ([_mn_spec(tm, tn)], [_mn_spec(tm, tn)])
    y, rs = normed
    (u,) = _dense(y, 'w_up', layer, W, P, N=F, epilogue=lambda acc, rows: (jnp.square(jnp.maximum(acc, 0.0)),),
                  specs=mn, out_dtypes=[BF16], emit=emit, side_cast='w_down' if emit else None, row_scale_in=rs,
                  name="ffn_up")
    h, y, rs = _dense(u, 'w_down', layer, W, P, N=D, epilogue=lambda acc, rows, r: (r[rows, :] + acc,), specs=res,
                      extras=[h], out_dtypes=[F32], norm_gains=[W['g_ple'][layer]], name="ffn_down")
    ple_dim = p_l.shape[1]
    ple_specs = lambda tm, tn: ([_mn_spec(tm, tn), pl.BlockSpec((tm, ple_dim), lambda i, j, k: (i, 0)),
                                 pl.BlockSpec((ple_dim, tn), lambda i, j, k: (0, j))], [_mn_spec(tm, tn)])
    return _dense(y, 'w_ple_gate', layer, W, P, N=D,
                  epilogue=lambda acc, rows, r, pp, wp: (r[rows, :] + _dot(pp[rows, :], wp[...]) * _sigmoid(acc),),
                  specs=ple_specs, extras=[h, p_l.astype(BF16), P['w_ple'][layer]], out_dtypes=[F32],
                  row_scale_in=rs, norm_gains=next_gains, name="ple")


def _kv_epilogue(acc, rows, gain_ref, cos_ref, sin_ref):
    j = pl.program_id(0)
    cosf, sinf = cos_ref[rows, :], sin_ref[rows, :]
    heads = []
    for hh in range(N_KV_HEADS):
        sl = slice(hh * HEAD_DIM, (hh + 1) * HEAD_DIM)
        heads.append(_rope(_head_norm(acc[:, sl], gain_ref[0, :, sl]), cosf, sinf))
    out = jnp.where((j == 2) | (j == 4), jnp.concatenate(heads, axis=1), acc)
    return out, out


def _q_epilogue(acc, rows, gq_ref, cos_ref, sin_ref):
    cosf, sinf = cos_ref[rows, :], sin_ref[rows, :]
    qs, qrs = [], []
    for hh in range(acc.shape[1] // HEAD_DIM):
        qn = _head_norm(acc[:, hh * HEAD_DIM:(hh + 1) * HEAD_DIM], gq_ref[...])
        qs.append(qn)
        qrs.append(_rope(qn, cosf, sinf))
    return jnp.concatenate(qs, axis=1), jnp.concatenate(qrs, axis=1)


def _trunk(x, p, pool_prefix, pos0, W, P, attend):
    B, T, D = x.shape
    M = B * T
    tmx = 1024 if M >= 1024 else M
    h = x.reshape(M, D)
    pg = D // len(POOL_WINDOWS)

    d, pool_new = pool_diff(x, W['g_mix'][0], pool_prefix[0], pos0)
    d, pool_new = d.reshape(M, D), pool_new[None]
    grid, tm, tn, tk = _tiles(M, D, pg, tmx, pg, pg)
    n_ex, n_ex_specs, n_shapes, n_specs = _norm_io([W['g_ffn'][0]], M, D, grid[1], tm, tn)
    h, y, ssq = mm(d, P['w_pool'][0], grid=grid, tm=tm, tn=tn, tk=tk,
                   x_map=lambda i, j, k: (i, j), w_map=lambda i, j, k: (j, 0),
                   epilogue=_norm_producer(lambda acc, rows, sc, r: (r[rows, :] + acc * sc[...],), 2),
                   extras=[W['pool_scale'][0].reshape(1, D), h] + n_ex,
                   extra_specs=[pl.BlockSpec((1, tn), lambda i, j, k: (0, j)), _mn_spec(tm, tn)] + n_ex_specs,
                   out_shapes=[jax.ShapeDtypeStruct((M, D), F32)] + n_shapes,
                   out_specs=[_mn_spec(tm, tn)] + n_specs, name="pool_mix")
    h, hkv, a1, rs = _ffn_ple(h, (y, row_scale(ssq, D)), p[0].reshape(M, -1), 0, W, P,
                              [W['g_kv'], W['g_mix'][1]])

    pos = pos0 + jnp.tile(jnp.arange(T, dtype=jnp.int32), B)
    cosf, sinf = _rope_tables(pos)
    n_kv = W['w_kv'].shape[1]
    rope_spec = lambda tm: pl.BlockSpec((tm, HEAD_DIM), lambda i, j, k: (i, 0))
    kv_spec = lambda tm: pl.BlockSpec((1, tm, KV_W), lambda i, j, k: (j, i, 0))
    kv4_spec = lambda tm: pl.BlockSpec((1, tm, N_KV_HEADS, HEAD_DIM), lambda i, j, k: (j, i, 0, 0))
    kv_specs = lambda tm, tn: ([pl.BlockSpec((1, 1, KV_W), lambda i, j, k: (j, 0, 0)), rope_spec(tm), rope_spec(tm)],
                               [kv4_spec(tm), kv_spec(tm)])
    kv, kv_b = _dense(hkv, 'w_kv2d', None, W, P, N=n_kv * KV_W, epilogue=_kv_epilogue, specs=kv_specs,
                      extras=[P['kv_gain'].reshape(n_kv, 1, KV_W), cosf, sinf],
                      out_dtypes=[jax.ShapeDtypeStruct((n_kv, M, N_KV_HEADS, HEAD_DIM), F32),
                                  jax.ShapeDtypeStruct((n_kv, M, KV_W), BF16)], row_scale_in=rs, name="kv_proj")
    q_specs = lambda tm, tn: ([pl.BlockSpec((1, HEAD_DIM), lambda i, j, k: (0, 0)), rope_spec(tm), rope_spec(tm)],
                              [_mn_spec(tm, tn)] * 2)
    q, qr = _dense(a1, 'w_qg', 0, W, P, N=D, epilogue=_q_epilogue, specs=q_specs,
                   extras=[W['g_q'][0].reshape(1, HEAD_DIM), cosf, sinf], out_dtypes=[BF16, BF16], row_scale_in=rs,
                   name="q_proj")
    ng = N_KV_HEADS * LANE
    grid, tm, tn, tk = _tiles(M, ng, D, tmx, ng, D)
    (gates,) = mm(a1, P['w_gate'][0], grid=grid, tm=tm, tn=tn, tk=tk,
                  epilogue=_row_scaled(lambda acc, rows: (_sigmoid(acc),)), extras=[rs],
                  extra_specs=[pl.BlockSpec((tm, LANE), lambda i, j, k: (i, 0))],
                  out_shapes=[jax.ShapeDtypeStruct((M, ng), F32)], out_specs=[_mn_spec(tm, tn)], name="gate_proj")

    o, win_state = attend(kv, kv_b, q, qr, gates)

    h, y, rs = _dense(o, 'w_o', 0, W, P, N=D, epilogue=lambda acc, rows, r: (r[rows, :] + acc,), extras=[h],
                      specs=lambda tm, tn: ([_mn_spec(tm, tn)], [_mn_spec(tm, tn)]), out_dtypes=[F32],
                      norm_gains=[W['g_ffn'][1]], name="attn_out")
    (h,) = _ffn_ple(h, (y, rs), p[1].reshape(M, -1), 1, W, P, [])
    rows = tuple(kv[n].reshape(B, T, N_KV_HEADS, HEAD_DIM) for n in range(4))
    return h.reshape(B, T, D), pool_new, rows, win_state


def _compress(k_pages, v_pages, table, cp, W, P):
    abk, abv = cmp_partials(k_pages, v_pages, table, P['w_cmp_kcat'], P['w_cmp_vcat'], cp)
    gain = W['g_k_cmp'].reshape(1, HEAD_DIM)
    ck = cmp_finish(abk, P['pe_k'], P['w_cmp_k1'], P['w_cmp_k2'], gain, norm=True)
    cv = cmp_finish(abv, P['pe_v'], P['w_cmp_v1'], P['w_cmp_v2'], gain, norm=False)
    return ck, cv


def kernel(x_prompt, x_sample, state_pool, cache_k_cmp, cache_v_cmp, cache_k_sel, cache_v_sel, state_k_win, state_v_win, page_table, p_prompt, p_sample, g_mix, w_pool, pool_scale, g_kv, w_kv, g_k_cmp, g_k_sel, g_k_win, w_cmp_k1, w_cmp_k2, pe_cmp_k, w_cmp_v1, w_cmp_v2, pe_cmp_v, w_qg, g_q, w_o, g_ffn, w_up, w_down, g_ple, w_ple, w_ple_gate):
    W = dict(g_mix=g_mix, w_pool=w_pool, pool_scale=pool_scale, g_kv=g_kv, w_kv=w_kv, g_k_cmp=g_k_cmp,
             g_k_sel=g_k_sel, g_k_win=g_k_win, w_cmp_k1=w_cmp_k1, w_cmp_k2=w_cmp_k2, pe_cmp_k=pe_cmp_k,
             w_cmp_v1=w_cmp_v1, w_cmp_v2=w_cmp_v2, pe_cmp_v=pe_cmp_v, w_qg=w_qg, g_q=g_q, w_o=w_o,
             g_ffn=g_ffn, w_up=w_up, w_down=w_down, g_ple=g_ple, w_ple=w_ple, w_ple_gate=w_ple_gate)
    P = _prep_weights(W)
    W['w_kv2d'] = w_kv.reshape(w_kv.shape[0], -1)
    Bp, Tp, D = x_prompt.shape
    Bs, Ts, _ = x_sample.shape
    assert Ts == 1, "the decode path handles one new token per sequence"
    page = cache_k_cmp.shape[1]
    past_len = page_table.shape[1] * page
    R = D // HEAD_DIM // N_KV_HEADS
    assert Tp % page == 0 and past_len % SEL_BLOCK == 0

    def attend_prompt(kv, kv_b, q, qr, gates):
        ppb = Tp // page
        table = jnp.arange(Bp * ppb, dtype=jnp.int32).reshape(Bp, ppb)
        pages = lambda a: a.reshape(-1, page, N_KV_HEADS, HEAD_DIM)
        ck, cv = _compress(pages(kv[0]), pages(kv[1]), table, ppb, W, P)
        nc = Tp // CMP_STRIDE - CMP_BLOCK // CMP_STRIDE + 1
        seq = lambda a: a.reshape(Bp, Tp, -1)
        o = attn_prompt(seq(q), seq(qr), ck, cv, kv_b.reshape(-1, Bp, Tp, KV_W), seq(gates), nc=nc)
        nw = min(WINDOW, Tp)
        win = tuple(kv[n].reshape(Bp, Tp, N_KV_HEADS, HEAD_DIM)[:, -nw:] for n in (4, 5))
        return o.reshape(Bp * Tp, D), win

    def attend_sample(kv, kv_b, q, qr, gates):
        del kv_b
        ck, cv = _compress(cache_k_cmp, cache_v_cmp, page_table, min(32, page_table.shape[1]), W, P)
        nc = (past_len - (CMP_BLOCK - 1)) // CMP_STRIDE + 1
        ns = past_len // SEL_BLOCK + 1
        wb = state_k_win.shape[1]
        new_row = lambda a: a.reshape(Bs, 1, N_KV_HEADS, HEAD_DIM)
        kw = jnp.concatenate([state_k_win, new_row(kv[4])], axis=1)[:, -wb:]
        vw = jnp.concatenate([state_v_win, new_row(kv[5])], axis=1)[:, -wb:]
        heads = lambda a: a.reshape(Bs, N_KV_HEADS, R, HEAD_DIM)
        ocmp, owin, ids = attn_dec_dense(heads(q), heads(qr), ck, cv, kw, vw, nc=nc, ns=ns, pos=past_len)
        gt = gates.reshape(Bs, N_KV_HEADS, LANE)[:, :, :R * N_BRANCH].reshape(Bs, N_KV_HEADS, R, N_BRANCH)
        gt = jnp.pad(gt, ((0, 0), (0, 0), (0, 0), (0, LANE - N_BRANCH)))
        o = attn_dec_sel(page_table, ids, heads(qr), cache_k_sel, cache_v_sel,
                         kv[2].reshape(Bs, 1, KV_W), kv[3].reshape(Bs, 1, KV_W), ocmp, owin, gt, pos=past_len)
        return o.reshape(Bs, D).astype(BF16), (kw, vw)

    pool_zero = jnp.zeros((state_pool.shape[0], Bp, POOL_STATE, D), x_prompt.dtype)
    y_p, pool_p, rows_p, win_p = _trunk(x_prompt, p_prompt, pool_zero, 0, W, P, attend_prompt)
    y_s, pool_s, rows_s, win_s = _trunk(x_sample, p_sample, state_pool, past_len, W, P, attend_sample)
    return (y_p, y_s, pool_p, pool_s, rows_p[0], rows_p[1], rows_p[2], rows_p[3], win_p[0], win_p[1],
            rows_s[0], rows_s[1], rows_s[2], rows_s[3], win_s[0], win_s[1])
```

```python
import functools

import jax
import jax.numpy as jnp
import numpy as np
from jax import lax
from jax.experimental import pallas as pl
from jax.experimental.pallas import tpu as pltpu

F32 = jnp.float32
BF16 = jnp.bfloat16

POOL_WINDOWS = (2, 4, 8, 16)
POOL_STATE = max(POOL_WINDOWS) - 1
POOL_PAD = POOL_STATE + 1
HEAD_DIM = 128
N_KV_HEADS = 4
N_BRANCH = 3
CMP_BLOCK = 32
CMP_STRIDE = 16
CMP_HIDDEN = 2 * HEAD_DIM
SEL_BLOCK = 64
N_SEL = 16
WINDOW = 512
ROPE_THETA = 10000.0
EPS = 1e-6
SCALE = HEAD_DIM ** -0.5
NEG = -1e30
FORCE = 1e9
PAD_SCORE = -3e38
KV_W = N_KV_HEADS * HEAD_DIM
LANE = 128
VMEM_LIMIT = 56 * 1024 * 1024


def _params(sem):
    return pltpu.CompilerParams(dimension_semantics=sem, vmem_limit_bytes=VMEM_LIMIT)


def _sigmoid(x):
    return 1.0 / (1.0 + jnp.exp(-x))


def _dot(a, b):
    return jnp.dot(a, b, preferred_element_type=F32)


def _dot_nt(a, b):
    return lax.dot_general(a, b, (((1,), (1,)), ((), ())), preferred_element_type=F32)


def _dot_tn(a, b):
    return lax.dot_general(a, b, (((0,), (0,)), ((), ())), preferred_element_type=F32)


def _head_norm(x, g):
    return x * lax.rsqrt(jnp.mean(x * x, axis=-1, keepdims=True) + EPS) * g


def _rope(x, cosf, sinf):
    return x * cosf + pltpu.roll(x, HEAD_DIM // 2, 1) * sinf


def _fold_lanes(x):
    parts = [x[:, c * LANE:(c + 1) * LANE] for c in range(x.shape[1] // LANE)]
    return functools.reduce(lambda u, v: u + v, parts)


def _norm_producer(epilogue, n_base):
    def wrapped(acc, rows, *extras):
        (h,) = epilogue(acc, rows, *extras[:n_base])
        return (h, *[(h * g[...]).astype(BF16) for g in extras[n_base:]], _fold_lanes(h * h))
    return wrapped


def _row_scaled(epilogue):
    def wrapped(acc, rows, rs_ref, *extras):
        rs = rs_ref[rows, :]
        acc = jnp.concatenate([acc[:, c * LANE:(c + 1) * LANE] * rs for c in range(acc.shape[1] // LANE)], axis=1)
        return epilogue(acc, rows, *extras)
    return wrapped


def _row_scale_kernel(ssq_ref, o_ref, *, d):
    tot = jnp.sum(functools.reduce(lambda u, v: u + v, [ssq_ref[j] for j in range(ssq_ref.shape[0])]),
                  axis=1, keepdims=True)
    o_ref[...] = jnp.broadcast_to(lax.rsqrt(tot / d + EPS), o_ref.shape)


def row_scale(ssq, d):
    gn, M, _ = ssq.shape
    tm = min(M, 1024)
    return pl.pallas_call(
        functools.partial(_row_scale_kernel, d=d),
        grid=(M // tm,),
        in_specs=[pl.BlockSpec((gn, tm, LANE), lambda i: (0, i, 0))],
        out_specs=pl.BlockSpec((tm, LANE), lambda i: (i, 0)),
        out_shape=jax.ShapeDtypeStruct((M, LANE), F32),
        compiler_params=_params(("parallel",)),
        name="row_scale",
    )(ssq)


def _norm_io(gains, M, N, gn, tm, tn):
    extras = [g.reshape(1, N).astype(F32) for g in gains]
    extra_specs = [pl.BlockSpec((1, tn), lambda i, j, k: (0, j)) for _ in gains]
    shapes = [jax.ShapeDtypeStruct((M, N), BF16) for _ in gains] + [jax.ShapeDtypeStruct((gn, M, LANE), F32)]
    specs = [_mn_spec(tm, tn) for _ in gains] + [pl.BlockSpec((1, tm, LANE), lambda i, j, k: (j, i, 0))]
    return extras, extra_specs, shapes, specs


EPILOGUE_ROWS = 256


def _finish_rows(acc_of, tm, outs, extras, epilogue):
    ch = min(tm, EPILOGUE_ROWS)
    for c in range(tm // ch):
        rows = slice(c * ch, (c + 1) * ch)
        for o_ref, r in zip(outs, epilogue(acc_of(rows), rows, *extras)):
            if len(o_ref.shape) == 4:
                for hh in range(o_ref.shape[2]):
                    o_ref[0, rows, hh, :] = r[:, hh * HEAD_DIM:(hh + 1) * HEAD_DIM].astype(o_ref.dtype)
            elif len(o_ref.shape) == 3:
                o_ref[0, rows, :] = r.astype(o_ref.dtype)
            else:
                o_ref[rows, :] = r.astype(o_ref.dtype)


def _side_io(side, n_steps, step):
    s_arr, s_layer = side
    _, rows, cols = s_arr.shape
    rs = rows // n_steps
    assert rs * n_steps == rows and rs % 16 == 0
    in_spec = pl.BlockSpec((None, rs, cols), lambda *g: (s_layer, step(*g), 0))
    out_spec = pl.BlockSpec((rs, cols), lambda *g: (step(*g), 0))
    return s_arr, in_spec, jax.ShapeDtypeStruct((rows, cols), BF16), out_spec


def _mm_kernel(*refs, nk, n_extra, n_out, epilogue, side):
    x_ref, w_ref = refs[0], refs[1]
    extras = refs[2:2 + n_extra]
    n_in = 2 + n_extra + side
    outs = refs[n_in:n_in + n_out]
    tm = x_ref.shape[0]
    if side:
        refs[n_in + n_out][...] = refs[n_in - 1][...].astype(BF16)

    if nk == 1:
        _finish_rows(lambda rows: _dot(x_ref[rows, :], w_ref[...]), tm, outs, extras, epilogue)
    else:
        acc_ref = refs[-1]
        k = pl.program_id(2)

        @pl.when(k == 0)
        def _():
            acc_ref[...] = jnp.zeros_like(acc_ref)

        acc_ref[...] += _dot(x_ref[...], w_ref[...])

        @pl.when(k == nk - 1)
        def _():
            _finish_rows(lambda rows: acc_ref[rows, :], tm, outs, extras, epilogue)


def mm(x, w, *, grid, tm, tn, tk, epilogue, extras=(), extra_specs=(), out_shapes, out_specs,
       x_map=None, w_map=None, side=None, name):
    gm, gn, nk = grid
    x_map = x_map or (lambda i, j, k: (i, k))
    w_map = w_map or (lambda i, j, k: (k, j))
    ins, in_specs = [x, w, *extras], [pl.BlockSpec((tm, tk), x_map), pl.BlockSpec((tk, tn), w_map)]
    in_specs += list(extra_specs)
    n_out, out_shapes, out_specs = len(out_shapes), list(out_shapes), list(out_specs)
    if side is not None:
        s_arr, s_in, s_shape, s_out = _side_io(side, gm * gn * nk, lambda i, j, k: (i * gn + j) * nk + k)
        ins.append(s_arr), in_specs.append(s_in), out_shapes.append(s_shape), out_specs.append(s_out)
    kern = functools.partial(_mm_kernel, nk=nk, n_extra=len(extras), n_out=n_out, epilogue=epilogue,
                             side=side is not None)
    return pl.pallas_call(
        kern,
        grid=grid,
        in_specs=in_specs,
        out_specs=out_specs,
        out_shape=out_shapes,
        scratch_shapes=[pltpu.VMEM((tm, tn), F32)] if nk > 1 else [],
        compiler_params=_params(("parallel", "parallel", "arbitrary")),
        name=name,
    )(*ins)


def _mm_ws_kernel(*refs, nk, n_extra, n_out, epilogue, emit, side):
    x_ref, w_ref = refs[0], refs[1]
    extras = refs[2:2 + n_extra]
    n_in = 2 + n_extra + side
    outs = refs[n_in:n_in + n_out]
    rest = refs[n_in + n_out:]
    wb_ref = rest[emit + side]
    i, k = pl.program_id(1), pl.program_id(2)

    @pl.when(i == 0)
    def _():
        wb_ref[k] = w_ref[...].astype(BF16)
        if emit:
            rest[0][...] = wb_ref[k]

    if side:
        rest[emit][...] = refs[n_in - 1][...].astype(BF16)

    tm = x_ref.shape[0]
    if nk == 1:
        _finish_rows(lambda rows: _dot(x_ref[rows, :], wb_ref[0]), tm, outs, extras, epilogue)
    else:
        acc_ref = rest[-1]

        @pl.when(k == 0)
        def _():
            acc_ref[...] = jnp.zeros_like(acc_ref)

        acc_ref[...] += _dot(x_ref[...], wb_ref[k])

        @pl.when(k == nk - 1)
        def _():
            _finish_rows(lambda rows: acc_ref[rows, :], tm, outs, extras, epilogue)


def mm_ws(x, w, *, layer=None, emit=False, side=None, grid, tm, tn, tk, epilogue, extras=(), extra_specs=(),
          out_shapes, out_specs, name):
    gm, gn, nk = grid
    swap = lambda f: (lambda j, i, k: f(i, j, k))
    respec = lambda s: pl.BlockSpec(s.block_shape, swap(s.index_map))
    k_once = lambda i, k: jnp.where(i == 0, k, nk - 1)
    if layer is None:
        w_spec = pl.BlockSpec((tk, tn), lambda j, i, k: (k_once(i, k), j))
    else:
        w_spec = pl.BlockSpec((None, tk, tn), lambda j, i, k: (layer, k_once(i, k), j))
    n_out = len(out_shapes)
    ins, in_specs = [x, w, *extras], [pl.BlockSpec((tm, tk), lambda j, i, k: (i, k)), w_spec]
    in_specs += [respec(s) for s in extra_specs]
    out_shapes, out_specs = list(out_shapes), [respec(s) for s in out_specs]
    if emit:
        out_shapes.append(jax.ShapeDtypeStruct((nk * tk, gn * tn), BF16))
        out_specs.append(pl.BlockSpec((tk, tn), lambda j, i, k: (k_once(i, k), j)))
    if side is not None:
        s_arr, s_in, s_shape, s_out = _side_io(side, gm * gn * nk, lambda j, i, k: (j * gm + i) * nk + k)
        ins.append(s_arr), in_specs.append(s_in), out_shapes.append(s_shape), out_specs.append(s_out)
    kern = functools.partial(_mm_ws_kernel, nk=nk, n_extra=len(extras), n_out=n_out, epilogue=epilogue, emit=emit,
                             side=side is not None)
    return pl.pallas_call(
        kern,
        grid=(gn, gm, nk),
        in_specs=in_specs,
        out_specs=out_specs,
        out_shape=out_shapes,
        scratch_shapes=[pltpu.VMEM((nk, tk, tn), BF16)] + ([pltpu.VMEM((tm, tn), F32)] if nk > 1 else []),
        compiler_params=_params(("parallel", "arbitrary", "arbitrary")),
        name=name,
    )(*ins)


def _tiles(M, N, K, tm, tn, tk):
    tm, tn, tk = min(tm, M), min(tn, N), min(tk, K)
    return (M // tm, N // tn, K // tk), tm, tn, tk


def _mn_spec(tm, tn):
    return pl.BlockSpec((tm, tn), lambda i, j, k: (i, j))


def _pool_diff_kernel(*refs, tt, pos0, halo):
    x_ref, pre_ref, g_ref = refs[0], refs[1 + halo], refs[2 + halo]
    d_ref, st_ref, seq_ref = refs[3 + halo:]
    t = pl.program_id(1)
    norm = lambda x: x * lax.rsqrt(jnp.mean(x * x, axis=-1, keepdims=True) + EPS) * g_ref[...]
    a = norm(x_ref[0])
    seq_ref[POOL_PAD:POOL_PAD + tt, :] = a

    @pl.when(t == 0)
    def _():
        seq_ref[0:POOL_PAD, :] = pre_ref[0]

    if halo:
        @pl.when(t > 0)
        def _():
            seq_ref[0:POOL_PAD, :] = norm(refs[1][0])

    pos = pos0 + t * tt + lax.broadcasted_iota(jnp.int32, (tt, 1), 0)
    pg = a.shape[1] // len(POOL_WINDOWS)
    for g, w in enumerate(POOL_WINDOWS):
        cols = slice(g * pg, (g + 1) * pg)
        s = a[:, cols]
        for j in range(1, w):
            s = s + seq_ref[POOL_PAD - j:POOL_PAD - j + tt, cols]
        cnt = jnp.minimum(pos + 1, w).astype(F32)
        d_ref[0, :, cols] = (s / cnt - a[:, cols]).astype(d_ref.dtype)
    st_ref[0] = seq_ref[tt:tt + POOL_PAD, :]


def pool_diff(x, gain, prefix, pos0):
    B, T, D = x.shape
    tt = min(T, 256)
    halo = T > tt
    pre = jnp.concatenate([jnp.zeros((B, 1, D), F32), prefix], axis=1)
    hpt = tt // POOL_PAD
    in_specs = [pl.BlockSpec((1, tt, D), lambda b, t: (b, t, 0))]
    if halo:
        in_specs.append(pl.BlockSpec((1, POOL_PAD, D), lambda b, t: (b, jnp.maximum(t * hpt - 1, 0), 0)))
    in_specs += [pl.BlockSpec((1, POOL_PAD, D), lambda b, t: (b, 0, 0)), pl.BlockSpec((1, D), lambda b, t: (0, 0))]
    d, st = pl.pallas_call(
        functools.partial(_pool_diff_kernel, tt=tt, pos0=pos0, halo=halo),
        grid=(B, T // tt),
        in_specs=in_specs,
        out_specs=[pl.BlockSpec((1, tt, D), lambda b, t: (b, t, 0)),
                   pl.BlockSpec((1, POOL_PAD, D), lambda b, t: (b, 0, 0))],
        out_shape=[jax.ShapeDtypeStruct((B, T, D), BF16), jax.ShapeDtypeStruct((B, POOL_PAD, D), F32)],
        scratch_shapes=[pltpu.VMEM((POOL_PAD + tt, D), F32)],
        compiler_params=_params(("parallel", "arbitrary")),
        name="pool_diff",
    )(*([x, x] if halo else [x]), pre, gain.reshape(1, D).astype(F32))
    return d, st[:, 1:]


PAGES_PER_STEP = 8
CMP_FINISH_ROWS = 512


def _cmp_partial_kernel(pt_ref, *refs, steps, pps):
    del pt_ref
    k_pages, v_pages = refs[:pps], refs[pps:2 * pps]
    wk_ref, wv_ref, abk_ref, abv_ref, xk_ref, xv_ref = refs[2 * pps:]
    p = pl.program_id(2)
    rows_per_page = (k_pages[0].shape[1] // CMP_STRIDE) * N_KV_HEADS
    for pages, x_ref in ((k_pages, xk_ref), (v_pages, xv_ref)):
        for q, page_ref in enumerate(pages):
            for sb in range(page_ref.shape[1] // CMP_STRIDE):
                row = q * rows_per_page + sb * N_KV_HEADS
                for r in range(CMP_STRIDE):
                    x_ref[p, row:row + N_KV_HEADS, r * HEAD_DIM:(r + 1) * HEAD_DIM] = page_ref[0, sb * CMP_STRIDE + r]

    @pl.when(p == steps - 1)
    def _():
        for x_ref, w_ref, ab_ref in ((xk_ref, wk_ref, abk_ref), (xv_ref, wv_ref, abv_ref)):
            x = x_ref[...].reshape(ab_ref.shape[1], x_ref.shape[2])
            ab_ref[0] = _dot(x.astype(BF16), w_ref[...])


def cmp_partials(k_pages, v_pages, table, wk_cat, wv_cat, cp):
    B, ppb = table.shape
    page = k_pages.shape[1]
    pps = PAGES_PER_STEP
    rows_per_page = page // CMP_STRIDE * N_KV_HEADS
    nch, steps = ppb // cp, cp // pps
    m = cp * rows_per_page
    kdim = CMP_STRIDE * HEAD_DIM

    def page_spec(q):
        return pl.BlockSpec((1, page, N_KV_HEADS, HEAD_DIM),
                            lambda b, c, p, pt: (pt[b, c * cp + p * pps + q], 0, 0, 0))

    page_specs = [page_spec(q) for q in range(pps)]
    w_spec = pl.BlockSpec((kdim, 2 * CMP_HIDDEN), lambda b, c, p, pt: (0, 0))
    out_spec = pl.BlockSpec((1, m, 2 * CMP_HIDDEN), lambda b, c, p, pt: (b, c, 0))
    out_shape = jax.ShapeDtypeStruct((B, ppb * rows_per_page, 2 * CMP_HIDDEN), F32)
    x_scratch = pltpu.VMEM((steps, pps * rows_per_page, kdim), F32)
    return pl.pallas_call(
        functools.partial(_cmp_partial_kernel, steps=steps, pps=pps),
        grid_spec=pltpu.PrefetchScalarGridSpec(
            num_scalar_prefetch=1,
            grid=(B, nch, steps),
            in_specs=page_specs + page_specs + [w_spec, w_spec],
            out_specs=[out_spec, out_spec],
            scratch_shapes=[x_scratch, x_scratch],
        ),
        out_shape=[out_shape, out_shape],
        compiler_params=_params(("parallel", "parallel", "arbitrary")),
        name="cmp_partials",
    )(table, *([k_pages] * pps), *([v_pages] * pps), wk_cat, wv_cat)


def _cmp_finish_kernel(ab_ref, pe_ref, w1_ref, w2_ref, g_ref, o_ref, out_ref, *, norm):
    n4 = ab_ref.shape[1]
    G = o_ref.shape[1]
    bias = _dot(jnp.broadcast_to(pe_ref[...], (8, pe_ref.shape[1])), w1_ref[...])[0:1, :]
    ch = min(n4, CMP_FINISH_ROWS)
    for c in range(n4 // ch):
        lo, hi = c * ch, (c + 1) * ch
        first = ab_ref[0, lo:hi, :CMP_HIDDEN]
        if hi + G <= n4:
            second = ab_ref[0, lo + G:hi + G, CMP_HIDDEN:]
        else:
            second = jnp.concatenate([ab_ref[0, lo + G:hi, CMP_HIDDEN:], ab_ref[0, hi - G:hi, CMP_HIDDEN:]], axis=0)
        pre = first + second + bias
        out = _dot((pre * _sigmoid(pre)).astype(BF16), w2_ref[...])
        if norm:
            out = _head_norm(out, g_ref[...])
        out_ref[lo:hi, :] = out
    for g in range(G):
        o_ref[0, g] = out_ref[pl.ds(g, n4 // G, stride=G), :].astype(o_ref.dtype)


def cmp_finish(ab, pe, w1, w2, gain, *, norm):
    B, n4, _ = ab.shape
    G = N_KV_HEADS
    full = lambda a: pl.BlockSpec(a.shape, lambda b: (0,) * a.ndim)
    args = (pe, w1, w2, gain)
    return pl.pallas_call(
        functools.partial(_cmp_finish_kernel, norm=norm),
        grid=(B,),
        in_specs=[pl.BlockSpec((1, n4, 2 * CMP_HIDDEN), lambda b: (b, 0, 0))] + [full(a) for a in args],
        out_specs=pl.BlockSpec((1, G, n4 // G, HEAD_DIM), lambda b: (b, 0, 0, 0)),
        out_shape=jax.ShapeDtypeStruct((B, G, n4 // G, HEAD_DIM), BF16),
        scratch_shapes=[pltpu.VMEM((n4, HEAD_DIM), F32)],
        compiler_params=_params(("parallel",)),
        name="cmp_finish",
    )(ab, *args)


def _select_blocks(score, blk, ns):
    rank = jnp.zeros(score.shape, jnp.int32)
    for j in range(ns):
        sj = score[j:j + 1, :]
        beats = (sj > score) | ((sj == score) & (j < blk))
        rank = rank + beats.astype(jnp.int32)
    return (rank < min(N_SEL, ns)) & (score > 0.5 * NEG)


def _attn_prompt_kernel(q_ref, qr_ref, ck_ref, cv_ref, ks_ref, vs_ref, kw_ref, vw_ref, gate_ref, mapT_ref,
                        expand_ref, o_ref, part_ref, sbias_ref, wbias_ref, m_ref, acc_ref, *, tq, kc, nc, ns, R):
    qi = pl.program_id(2)
    q0 = qi * tq
    pos = q0 + lax.broadcasted_iota(jnp.int32, (tq, 1), 0)
    ncp = ck_ref.shape[2]

    ck = ck_ref[0, 0]
    cv = cv_ref[0, 0]
    cidx = lax.broadcasted_iota(jnp.int32, (1, ncp), 1)
    ok_c = (cidx * CMP_STRIDE + CMP_BLOCK - 1 <= pos) & (cidx < nc)
    imp = jnp.zeros((tq, ncp), F32)
    for r in range(R):
        qh = q_ref[0, :, r * HEAD_DIM:(r + 1) * HEAD_DIM]
        s = jnp.where(ok_c, _dot_nt(qh, ck) * SCALE, NEG)
        e = jnp.exp(s - jnp.max(s, axis=1, keepdims=True))
        p = jnp.where(ok_c, e * (1.0 / jnp.sum(e, axis=1, keepdims=True)), 0.0)
        imp = imp + p
        part_ref[r] = gate_ref[0, :, r * N_BRANCH:r * N_BRANCH + 1] * _dot(p.astype(BF16), cv)

    nsp = mapT_ref.shape[0]
    p_slc = lax.dot_general(mapT_ref[...], imp, (((1,), (1,)), ((), ())), precision=lax.Precision.HIGHEST,
                            preferred_element_type=F32)
    blk = lax.broadcasted_iota(jnp.int32, (nsp, tq), 0)
    pos_l = q0 + lax.broadcasted_iota(jnp.int32, (nsp, tq), 1)
    cur = pos_l // SEL_BLOCK
    vis = blk * SEL_BLOCK <= pos_l
    forced = vis & ((blk == 0) | (blk == cur) | (blk == cur - 1))
    score = jnp.where(forced, FORCE, jnp.where(vis, p_slc, NEG))
    score = jnp.where(blk < ns, score, PAD_SCORE)
    sel = _select_blocks(score, blk, ns).astype(BF16)

    c_hi = (q0 + tq) // kc
    col = lax.broadcasted_iota(jnp.int32, (1, kc), 1)
    sel_keys = _dot_tn(sel, expand_ref[...])
    for c in range(sbias_ref.shape[0]):
        @pl.when(c < c_hi)
        def _(c=c):
            ok = (sel_keys[:, c * kc:(c + 1) * kc] > 0.5) & (c * kc + col <= pos)
            sbias_ref[c] = jnp.where(ok, 0.0, NEG)
    nwc = wbias_ref.shape[0]
    c_w0 = c_hi - nwc
    for d in range(nwc):
        kpos = (c_w0 + d) * kc + col
        wbias_ref[d] = jnp.where((kpos <= pos) & (kpos > pos - WINDOW), 0.0, NEG)

    def fold(t):
        return [t[:, i * LANE:(i + 1) * LANE] for i in range(kc // LANE)]

    ones_col = (lax.broadcasted_iota(jnp.int32, (kc, LANE), 1) == 0).astype(BF16)

    def branch(k_ref, v_ref, c_lo, bias_of, gate_col):
        def logits(r, c, bias):
            k = k_ref[0, pl.ds(pl.multiple_of(c * kc, kc), kc), :]
            return _dot_nt(qr_ref[0, :, r * HEAD_DIM:(r + 1) * HEAD_DIM], k) * SCALE + bias

        m_ref[...] = jnp.full(m_ref.shape, NEG, F32)
        acc_ref[...] = jnp.zeros(acc_ref.shape, F32)

        def max_body(c, carry):
            bias = bias_of(c)
            for r in range(R):
                mx = m_ref[r]
                for part in fold(logits(r, c, bias)):
                    mx = jnp.maximum(mx, part)
                m_ref[r] = mx
            return carry

        lax.fori_loop(c_lo, c_hi, max_body, 0)
        for r in range(R):
            m_ref[r] = jnp.broadcast_to(jnp.max(m_ref[r], axis=1, keepdims=True), (tq, LANE))

        def sum_body(c, carry):
            bias = bias_of(c)
            v = jnp.concatenate([v_ref[0, pl.ds(pl.multiple_of(c * kc, kc), kc), :], ones_col], axis=1)
            for r in range(R):
                t = logits(r, c, bias)
                m = m_ref[r]
                ps = [jnp.exp(part - m) for part in fold(t)]
                acc_ref[r] += _dot(jnp.concatenate(ps, axis=1).astype(BF16), v)
            return carry

        lax.fori_loop(c_lo, c_hi, sum_body, 0)
        for r in range(R):
            gate = gate_ref[0, :, r * N_BRANCH + gate_col:r * N_BRANCH + gate_col + 1]
            part_ref[r] += (gate * (1.0 / acc_ref[r, :, HEAD_DIM:HEAD_DIM + 1])) * acc_ref[r, :, :HEAD_DIM]

    branch(ks_ref, vs_ref, 0, lambda c: sbias_ref[c], 1)
    branch(kw_ref, vw_ref, jnp.maximum(c_w0, 0), lambda c: wbias_ref[c - c_w0], 2)
    for r in range(R):
        o_ref[0, :, r * HEAD_DIM:(r + 1) * HEAD_DIM] = part_ref[r].astype(o_ref.dtype)


def _overlap_map(ncp, nsp, ns):
    ratio = CMP_BLOCK // CMP_STRIDE
    per_sel = SEL_BLOCK // CMP_STRIDE
    m = np.zeros((ncp, nsp), np.float32)
    for b in range(ns):
        for mm_ in range(per_sel):
            for n in range(ratio):
                j = per_sel * b + mm_ - n
                if 0 <= j < ncp:
                    m[j, b] += 1.0
    return m


def attn_prompt(q, qr, ck, cv, kvb, gates, *, nc):
    B, T, HD = q.shape
    G = N_KV_HEADS
    R = HD // HEAD_DIM // G
    tq = min(T, 256)
    kc = tq
    ns = T // SEL_BLOCK
    nsp = -(-ns // 8) * 8
    ncp = ck.shape[2]
    mapT = jnp.asarray(_overlap_map(ncp, nsp, ns).T)
    expand = jnp.asarray((np.arange(T)[None, :] // SEL_BLOCK == np.arange(nsp)[:, None]).astype(np.float32), BF16)
    q_spec = pl.BlockSpec((1, tq, R * HEAD_DIM), lambda b, g, i: (b, i, g))
    c_spec = pl.BlockSpec((1, 1, ncp, HEAD_DIM), lambda b, g, i: (b, g, 0, 0))
    kv_spec = lambda n: pl.BlockSpec((None, 1, T, HEAD_DIM), lambda b, g, i: (n, b, 0, g))
    return pl.pallas_call(
        functools.partial(_attn_prompt_kernel, tq=tq, kc=kc, nc=nc, ns=ns, R=R),
        grid=(B, G, T // tq),
        in_specs=[q_spec, q_spec, c_spec, c_spec, kv_spec(2), kv_spec(3), kv_spec(4), kv_spec(5),
                  pl.BlockSpec((1, tq, LANE), lambda b, g, i: (b, i, g)),
                  pl.BlockSpec((nsp, ncp), lambda b, g, i: (0, 0)),
                  pl.BlockSpec((nsp, T), lambda b, g, i: (0, 0))],
        out_specs=q_spec,
        out_shape=jax.ShapeDtypeStruct((B, T, HD), BF16),
        scratch_shapes=[pltpu.VMEM((R, tq, HEAD_DIM), F32), pltpu.VMEM((T // kc, tq, kc), F32),
                        pltpu.VMEM((min(WINDOW, T) // kc + tq // kc, tq, kc), F32)]
                       + [pltpu.VMEM((R, tq, HEAD_DIM), F32), pltpu.VMEM((R, tq, HEAD_DIM + LANE), F32)],
        compiler_params=_params(("parallel", "parallel", "arbitrary")),
        name="attn_prompt",
    )(q, qr, ck, cv, kvb, kvb, kvb, kvb, gates, mapT, expand)


def _attn_dec_dense_kernel(q_ref, qr_ref, ck_ref, cv_ref, kw_ref, vw_ref, map_ref, ocmp_ref, owin_ref, ids_ref,
                           *, nc, ns, pos):
    G, R = q_ref.shape[1], q_ref.shape[2]
    ncp = ck_ref.shape[2]
    nsl = map_ref.shape[1]
    cidx = lax.broadcasted_iota(jnp.int32, (1, ncp), 1)
    ok_c = (cidx * CMP_STRIDE + CMP_BLOCK - 1 <= pos) & (cidx < nc)
    blk_l = lax.broadcasted_iota(jnp.int32, (1, nsl), 1)
    cur = pos // SEL_BLOCK
    vis = blk_l * SEL_BLOCK <= pos
    forced = vis & ((blk_l == 0) | (blk_l == cur) | (blk_l == cur - 1))
    ii = lax.broadcasted_iota(jnp.int32, (nsl, nsl), 0)
    jj = lax.broadcasted_iota(jnp.int32, (nsl, nsl), 1)
    slot = lax.broadcasted_iota(jnp.int32, (nsl, LANE), 1).astype(F32)
    blk_s = lax.broadcasted_iota(jnp.int32, (nsl, LANE), 0).astype(F32)
    for g in range(G):
        s = jnp.where(ok_c, _dot_nt(q_ref[0, g], ck_ref[0, g]) * SCALE, NEG)
        e = jnp.exp(s - jnp.max(s, axis=1, keepdims=True))
        p = jnp.where(ok_c, e * (1.0 / jnp.sum(e, axis=1, keepdims=True)), 0.0)
        ocmp_ref[0, g] = _dot(p.astype(BF16), cv_ref[0, g])
        imp = jnp.broadcast_to(jnp.sum(p, axis=0, keepdims=True), (R, ncp))
        p_slc = jnp.dot(imp, map_ref[...], precision=lax.Precision.HIGHEST, preferred_element_type=F32)[0:1, :]
        score_l = jnp.where(forced, FORCE, jnp.where(vis, p_slc, NEG))
        score_l = jnp.where(blk_l < ns, score_l, PAD_SCORE)
        score_s = jnp.sum(jnp.where(ii == jj, score_l, 0.0), axis=1, keepdims=True)
        beats = (score_l > score_s) | ((score_l == score_s) & (jj < ii))
        rank = jnp.sum(beats.astype(F32), axis=1, keepdims=True)
        ids = jnp.sum(jnp.where(rank == slot, blk_s, 0.0), axis=0, keepdims=True)
        ids_ref[0, g] = ids[:, :N_SEL].astype(jnp.int32)
        kw = kw_ref[0, :, g, :].astype(BF16)
        vw = vw_ref[0, :, g, :].astype(BF16)
        s = _dot_nt(qr_ref[0, g], kw) * SCALE
        e = jnp.exp(s - jnp.max(s, axis=1, keepdims=True))
        p = e * (1.0 / jnp.sum(e, axis=1, keepdims=True))
        owin_ref[0, g] = _dot(p.astype(BF16), vw)


def attn_dec_dense(q, qr, ck, cv, kw, vw, *, nc, ns, pos):
    B, G, R, _ = q.shape
    ncp = ck.shape[2]
    wb = kw.shape[1]
    nsl = -(-ns // LANE) * LANE
    omap = jnp.asarray(_overlap_map(ncp, nsl, ns))
    q_spec = pl.BlockSpec((1, G, R, HEAD_DIM), lambda b: (b, 0, 0, 0))
    c_spec = pl.BlockSpec((1, G, ncp, HEAD_DIM), lambda b: (b, 0, 0, 0))
    w_spec = pl.BlockSpec((1, wb, G, HEAD_DIM), lambda b: (b, 0, 0, 0))
    return pl.pallas_call(
        functools.partial(_attn_dec_dense_kernel, nc=nc, ns=ns, pos=pos),
        grid=(B,),
        in_specs=[q_spec, q_spec, c_spec, c_spec, w_spec, w_spec, pl.BlockSpec((ncp, nsl), lambda b: (0, 0))],
        out_specs=[q_spec, q_spec, pl.BlockSpec((1, G, 1, N_SEL), lambda b: (b, 0, 0, 0))],
        out_shape=[jax.ShapeDtypeStruct((B, G, R, HEAD_DIM), F32), jax.ShapeDtypeStruct((B, G, R, HEAD_DIM), F32),
                   jax.ShapeDtypeStruct((B, G, 1, N_SEL), jnp.int32)],
        compiler_params=_params(("parallel",)),
        name="attn_dec_dense",
    )(q, qr, ck, cv, kw, vw, omap)


def _attn_dec_sel_kernel(pt_ref, ids_ref, qr_ref, *refs, n_past, pos):
    del pt_ref
    G = qr_ref.shape[1]
    kc_refs, vc_refs = refs[:G], refs[G:2 * G]
    kn_ref, vn_ref, ocmp_ref, owin_ref, gate_ref, o_ref, m_ref, l_ref, acc_ref = refs[2 * G:]
    b, n = pl.program_id(0), pl.program_id(1)

    @pl.when(n == 0)
    def _():
        m_ref[...] = jnp.full(m_ref.shape, NEG, F32)
        l_ref[...] = jnp.zeros(l_ref.shape, F32)
        acc_ref[...] = jnp.zeros(acc_ref.shape, F32)

    row = lax.broadcasted_iota(jnp.int32, (SEL_BLOCK, 1), 0)
    for g in range(G):
        bid = ids_ref[(b * G + g) * N_SEL + n]
        is_new = bid >= n_past
        first = (row == 0) & (bid == n_past)
        sl = slice(g * HEAD_DIM, (g + 1) * HEAD_DIM)
        k = jnp.where(is_new, jnp.where(first, kn_ref[0, :, sl], 0.0), kc_refs[g][0, :, g, :]).astype(BF16)
        v = jnp.where(is_new, jnp.where(first, vn_ref[0, :, sl], 0.0), vc_refs[g][0, :, g, :]).astype(BF16)
        kpos = bid * SEL_BLOCK + lax.broadcasted_iota(jnp.int32, (1, SEL_BLOCK), 1)
        ok = kpos <= pos
        s = jnp.where(ok, _dot_nt(qr_ref[0, g], k) * SCALE, NEG)
        m_prev = m_ref[g]
        m_new = jnp.maximum(m_prev, jnp.max(s, axis=1, keepdims=True))
        alpha = jnp.exp(m_prev - m_new)
        p = jnp.where(ok, jnp.exp(s - m_new), 0.0)
        l_ref[g] = alpha * l_ref[g] + jnp.sum(p, axis=1, keepdims=True)
        acc_ref[g] = alpha * acc_ref[g] + _dot(p.astype(BF16), v)
        m_ref[g] = m_new

    @pl.when(n == N_SEL - 1)
    def _():
        for g in range(G):
            gt = gate_ref[0, g]
            o_sel = acc_ref[g] * (1.0 / l_ref[g])
            o_ref[0, g] = gt[:, 0:1] * ocmp_ref[0, g] + gt[:, 1:2] * o_sel + gt[:, 2:3] * owin_ref[0, g]


def attn_dec_sel(table, ids, qr, k_cache, v_cache, k_new, v_new, ocmp, owin, gates, *, pos):
    B, G, R, _ = qr.shape
    page = k_cache.shape[1]
    bpp = page // SEL_BLOCK
    n_past = table.shape[1] * bpp
    kc = k_cache.reshape(k_cache.shape[0] * bpp, SEL_BLOCK, G, HEAD_DIM)
    vc = v_cache.reshape(v_cache.shape[0] * bpp, SEL_BLOCK, G, HEAD_DIM)

    def cache_spec(g):
        def index(b, n, pt, ids_):
            bid = jnp.minimum(ids_[(b * G + g) * N_SEL + n], n_past - 1)
            return (pt[b, bid // bpp] * bpp + bid % bpp, 0, 0, 0)
        return pl.BlockSpec((1, SEL_BLOCK, G, HEAD_DIM), index)

    q_spec = pl.BlockSpec((1, G, R, HEAD_DIM), lambda b, n, pt, ids_: (b, 0, 0, 0))
    n_spec = pl.BlockSpec((1, 1, G * HEAD_DIM), lambda b, n, pt, ids_: (b, 0, 0))
    g_spec = pl.BlockSpec((1, G, R, LANE), lambda b, n, pt, ids_: (b, 0, 0, 0))
    c_specs = [cache_spec(g) for g in range(G)]
    return pl.pallas_call(
        functools.partial(_attn_dec_sel_kernel, n_past=n_past, pos=pos),
        grid_spec=pltpu.PrefetchScalarGridSpec(
            num_scalar_prefetch=2,
            grid=(B, N_SEL),
            in_specs=[q_spec] + c_specs + c_specs + [n_spec, n_spec, q_spec, q_spec, g_spec],
            out_specs=q_spec,
            scratch_shapes=[pltpu.VMEM((G, R, 1), F32), pltpu.VMEM((G, R, 1), F32),
                            pltpu.VMEM((G, R, HEAD_DIM), F32)],
        ),
        out_shape=jax.ShapeDtypeStruct((B, G, R, HEAD_DIM), F32),
        compiler_params=_params(("parallel", "arbitrary")),
        name="attn_dec_sel",
    )(table, ids.reshape(-1), qr, *([kc] * G), *([vc] * G), k_new, v_new, ocmp, owin, gates)


def _rope_tables(pos):
    half = HEAD_DIM // 2
    inv = ROPE_THETA ** (-jnp.arange(half, dtype=F32) / half)
    ang = pos.astype(F32)[:, None] * inv[None, :]
    cos, sin = jnp.cos(ang), jnp.sin(ang)
    return jnp.concatenate([cos, cos], axis=1), jnp.concatenate([-sin, sin], axis=1)


def _prep_weights(W):
    D = W['w_kv'].shape[0]
    H = D // HEAD_DIM
    R = H // N_KV_HEADS
    pg = D // len(POOL_WINDOWS)
    half = CMP_STRIDE * HEAD_DIM
    P = {}
    P['w_pool'] = W['w_pool'].astype(BF16).reshape(-1, len(POOL_WINDOWS) * pg, pg)
    kv_gain = jnp.ones((W['w_kv'].shape[1], KV_W), F32)
    kv_gain = kv_gain.at[2].set(jnp.tile(W['g_k_sel'], N_KV_HEADS)).at[4].set(jnp.tile(W['g_k_win'], N_KV_HEADS))
    P['kv_gain'] = kv_gain
    for t in ('k', 'v'):
        w1 = W['w_cmp_%s1' % t].astype(BF16)
        P['w_cmp_%s1' % t] = w1
        P['w_cmp_%scat' % t] = jnp.concatenate([w1[:half], w1[half:]], axis=1)
        P['w_cmp_%s2' % t] = W['w_cmp_%s2' % t].astype(BF16)
        P['pe_%s' % t] = W['pe_cmp_%s' % t].astype(BF16).reshape(1, -1)
    n_b = W['w_qg'].shape[0]
    wg = W['w_qg'][:, :, H * HEAD_DIM:].astype(BF16).reshape(n_b, D, N_KV_HEADS, R * N_BRANCH)
    wg = jnp.pad(wg, ((0, 0), (0, 0), (0, 0), (0, LANE - R * N_BRANCH)))
    P['w_gate'] = wg.reshape(n_b, D, N_KV_HEADS * LANE)
    P['w_ple'] = W['w_ple'].astype(BF16)
    P[('w_up', 0)] = W['w_up'][0].astype(BF16)
    return P


def _dense(x, wname, layer, W, P, *, N, epilogue, specs, extras=(), out_dtypes, emit=False, side_cast=None, row_scale_in=None,
           norm_gains=None, name):
    M, K = x.shape
    shapes = [d if isinstance(d, jax.ShapeDtypeStruct) else jax.ShapeDtypeStruct((M, N), d) for d in out_dtypes]
    key = (wname, layer)
    tmx = 1024 if M >= 1024 else M
    use_mm = key in P
    if use_mm:
        grid, tm, tn, tk = _tiles(M, N, K, tmx, 1024, 2048 if (M >= 1024 and K > 4096) else 4096)
    else:
        grid, tm, tn, tk = _tiles(M, N, K, tmx, 512, K if K <= 4096 else 2048)
    extra_specs, out_specs = specs(tm, tn)
    extras = list(extras)
    if row_scale_in is not None:
        epilogue = _row_scaled(epilogue)
        extras.insert(0, row_scale_in)
        extra_specs = [pl.BlockSpec((tm, LANE), lambda i, j, k: (i, 0))] + list(extra_specs)
    if norm_gains:
        epilogue = _norm_producer(epilogue, len(extras))
        n_ex, n_ex_specs, n_shapes, n_specs = _norm_io(norm_gains, M, N, grid[1], tm, tn)
        extras, extra_specs = extras + n_ex, list(extra_specs) + n_ex_specs
        shapes, out_specs = shapes + n_shapes, list(out_specs) + n_specs
    side = None if side_cast is None else (W[side_cast[0]], side_cast[1])
    if use_mm:
        outs = list(mm(x, P[key], grid=grid, tm=tm, tn=tn, tk=tk, epilogue=epilogue, extras=extras,
                       extra_specs=extra_specs, out_shapes=shapes, out_specs=out_specs, side=side, name=name))
    else:
        w = W[wname]
        outs = list(mm_ws(x, w, layer=layer if w.ndim == 3 else None, emit=emit, side=side, grid=grid, tm=tm,
                          tn=tn, tk=tk, epilogue=epilogue, extras=extras, extra_specs=extra_specs,
                          out_shapes=shapes, out_specs=out_specs, name=name))
    if side is not None:
        P[side_cast] = outs.pop()
    if emit and not use_mm:
        P[key] = outs.pop()
    if norm_gains:
        outs.append(row_scale(outs.pop(), N))
    return outs


def _ffn_ple(h, normed, p_l, layer, W, P, next_gains):
    M, D = h.shape
    F = W['w_up'].shape[2]
    host = M >= 1024
    next_up = ('w_up', layer + 1) if host and layer + 1 < W['w_up'].shape[0] else None
    mn = lambda tm, tn: ([], [_mn_spec(tm, tn)])
    res = lambda tm, tn: ([_mn_spec(tm, tn)], [_mn_spec(tm, tn)])
    y, rs = normed
    (u,) = _dense(y, 'w_up', layer, W, P, N=F, epilogue=lambda acc, rows: (jnp.square(jnp.maximum(acc, 0.0)),),
                  specs=mn, out_dtypes=[BF16], side_cast=('w_down', layer) if host else None, row_scale_in=rs,
                  name="ffn_up")
    h, y, rs = _dense(u, 'w_down', layer, W, P, N=D, epilogue=lambda acc, rows, r: (r[rows, :] + acc,), specs=res,
                      extras=[h], out_dtypes=[F32], norm_gains=[W['g_ple'][layer]], side_cast=next_up,
                      name="ffn_down")
    ple_dim = p_l.shape[1]
    ple_specs = lambda tm, tn: ([_mn_spec(tm, tn), pl.BlockSpec((tm, ple_dim), lambda i, j, k: (i, 0)),
                                 pl.BlockSpec((ple_dim, tn), lambda i, j, k: (0, j))], [_mn_spec(tm, tn)])
    return _dense(y, 'w_ple_gate', layer, W, P, N=D,
                  epilogue=lambda acc, rows, r, pp, wp: (r[rows, :] + _dot(pp[rows, :], wp[...]) * _sigmoid(acc),),
                  specs=ple_specs, extras=[h, p_l.astype(BF16), P['w_ple'][layer]], out_dtypes=[F32],
                  row_scale_in=rs, norm_gains=next_gains, name="ple")


def _kv_epilogue(acc, rows, gain_ref, cos_ref, sin_ref):
    j = pl.program_id(0)
    cosf, sinf = cos_ref[rows, :], sin_ref[rows, :]
    heads = []
    for hh in range(N_KV_HEADS):
        sl = slice(hh * HEAD_DIM, (hh + 1) * HEAD_DIM)
        heads.append(_rope(_head_norm(acc[:, sl], gain_ref[0, :, sl]), cosf, sinf))
    out = jnp.where((j == 2) | (j == 4), jnp.concatenate(heads, axis=1), acc)
    return out, out


def _q_epilogue(acc, rows, gq_ref, cos_ref, sin_ref):
    cosf, sinf = cos_ref[rows, :], sin_ref[rows, :]
    qs, qrs = [], []
    for hh in range(acc.shape[1] // HEAD_DIM):
        qn = _head_norm(acc[:, hh * HEAD_DIM:(hh + 1) * HEAD_DIM], gq_ref[...])
        qs.append(qn)
        qrs.append(_rope(qn, cosf, sinf))
    return jnp.concatenate(qs, axis=1), jnp.concatenate(qrs, axis=1)


def _trunk(x, p, pool_prefix, pos0, W, P, attend):
    B, T, D = x.shape
    M = B * T
    tmx = 1024 if M >= 1024 else M
    h = x.reshape(M, D)
    pg = D // len(POOL_WINDOWS)

    d, pool_new = pool_diff(x, W['g_mix'][0], pool_prefix[0], pos0)
    d, pool_new = d.reshape(M, D), pool_new[None]
    grid, tm, tn, tk = _tiles(M, D, pg, tmx, pg, pg)
    n_ex, n_ex_specs, n_shapes, n_specs = _norm_io([W['g_ffn'][0]], M, D, grid[1], tm, tn)
    h, y, ssq = mm(d, P['w_pool'][0], grid=grid, tm=tm, tn=tn, tk=tk,
                   x_map=lambda i, j, k: (i, j), w_map=lambda i, j, k: (j, 0),
                   epilogue=_norm_producer(lambda acc, rows, sc, r: (r[rows, :] + acc * sc[...],), 2),
                   extras=[W['pool_scale'][0].reshape(1, D), h] + n_ex,
                   extra_specs=[pl.BlockSpec((1, tn), lambda i, j, k: (0, j)), _mn_spec(tm, tn)] + n_ex_specs,
                   out_shapes=[jax.ShapeDtypeStruct((M, D), F32)] + n_shapes,
                   out_specs=[_mn_spec(tm, tn)] + n_specs, name="pool_mix")
    h, hkv, a1, rs = _ffn_ple(h, (y, row_scale(ssq, D)), p[0].reshape(M, -1), 0, W, P,
                              [W['g_kv'], W['g_mix'][1]])

    pos = pos0 + jnp.tile(jnp.arange(T, dtype=jnp.int32), B)
    cosf, sinf = _rope_tables(pos)
    n_kv = W['w_kv'].shape[1]
    rope_spec = lambda tm: pl.BlockSpec((tm, HEAD_DIM), lambda i, j, k: (i, 0))
    kv_spec = lambda tm: pl.BlockSpec((1, tm, KV_W), lambda i, j, k: (j, i, 0))
    kv4_spec = lambda tm: pl.BlockSpec((1, tm, N_KV_HEADS, HEAD_DIM), lambda i, j, k: (j, i, 0, 0))
    kv_specs = lambda tm, tn: ([pl.BlockSpec((1, 1, KV_W), lambda i, j, k: (j, 0, 0)), rope_spec(tm), rope_spec(tm)],
                               [kv4_spec(tm), kv_spec(tm)])
    kv, kv_b = _dense(hkv, 'w_kv2d', None, W, P, N=n_kv * KV_W, epilogue=_kv_epilogue, specs=kv_specs,
                      extras=[P['kv_gain'].reshape(n_kv, 1, KV_W), cosf, sinf],
                      out_dtypes=[jax.ShapeDtypeStruct((n_kv, M, N_KV_HEADS, HEAD_DIM), F32),
                                  jax.ShapeDtypeStruct((n_kv, M, KV_W), BF16)], row_scale_in=rs, name="kv_proj")
    q_specs = lambda tm, tn: ([pl.BlockSpec((1, HEAD_DIM), lambda i, j, k: (0, 0)), rope_spec(tm), rope_spec(tm)],
                              [_mn_spec(tm, tn)] * 2)
    q, qr = _dense(a1, 'w_qg', 0, W, P, N=D, epilogue=_q_epilogue, specs=q_specs,
                   extras=[W['g_q'][0].reshape(1, HEAD_DIM), cosf, sinf], out_dtypes=[BF16, BF16], row_scale_in=rs,
                   name="q_proj")
    ng = N_KV_HEADS * LANE
    grid, tm, tn, tk = _tiles(M, ng, D, tmx, ng, D)
    (gates,) = mm(a1, P['w_gate'][0], grid=grid, tm=tm, tn=tn, tk=tk,
                  epilogue=_row_scaled(lambda acc, rows: (_sigmoid(acc),)), extras=[rs],
                  extra_specs=[pl.BlockSpec((tm, LANE), lambda i, j, k: (i, 0))],
                  out_shapes=[jax.ShapeDtypeStruct((M, ng), F32)], out_specs=[_mn_spec(tm, tn)], name="gate_proj")

    o, win_state = attend(kv, kv_b, q, qr, gates)

    h, y, rs = _dense(o, 'w_o', 0, W, P, N=D, epilogue=lambda acc, rows, r: (r[rows, :] + acc,), extras=[h],
                      specs=lambda tm, tn: ([_mn_spec(tm, tn)], [_mn_spec(tm, tn)]), out_dtypes=[F32],
                      norm_gains=[W['g_ffn'][1]], name="attn_out")
    (h,) = _ffn_ple(h, (y, rs), p[1].reshape(M, -1), 1, W, P, [])
    rows = tuple(kv[n].reshape(B, T, N_KV_HEADS, HEAD_DIM) for n in range(4))
    return h.reshape(B, T, D), pool_new, rows, win_state


def _compress(k_pages, v_pages, table, cp, W, P):
    abk, abv = cmp_partials(k_pages, v_pages, table, P['w_cmp_kcat'], P['w_cmp_vcat'], cp)
    gain = W['g_k_cmp'].reshape(1, HEAD_DIM)
    ck = cmp_finish(abk, P['pe_k'], P['w_cmp_k1'], P['w_cmp_k2'], gain, norm=True)
    cv = cmp_finish(abv, P['pe_v'], P['w_cmp_v1'], P['w_cmp_v2'], gain, norm=False)
    return ck, cv


def kernel(x_prompt, x_sample, state_pool, cache_k_cmp, cache_v_cmp, cache_k_sel, cache_v_sel, state_k_win, state_v_win, page_table, p_prompt, p_sample, g_mix, w_pool, pool_scale, g_kv, w_kv, g_k_cmp, g_k_sel, g_k_win, w_cmp_k1, w_cmp_k2, pe_cmp_k, w_cmp_v1, w_cmp_v2, pe_cmp_v, w_qg, g_q, w_o, g_ffn, w_up, w_down, g_ple, w_ple, w_ple_gate):
    W = dict(g_mix=g_mix, w_pool=w_pool, pool_scale=pool_scale, g_kv=g_kv, w_kv=w_kv, g_k_cmp=g_k_cmp,
             g_k_sel=g_k_sel, g_k_win=g_k_win, w_cmp_k1=w_cmp_k1, w_cmp_k2=w_cmp_k2, pe_cmp_k=pe_cmp_k,
             w_cmp_v1=w_cmp_v1, w_cmp_v2=w_cmp_v2, pe_cmp_v=pe_cmp_v, w_qg=w_qg, g_q=g_q, w_o=w_o,
             g_ffn=g_ffn, w_up=w_up, w_down=w_down, g_ple=g_ple, w_ple=w_ple, w_ple_gate=w_ple_gate)
    P = _prep_weights(W)
    W['w_kv2d'] = w_kv.reshape(w_kv.shape[0], -1)
    Bp, Tp, D = x_prompt.shape
    Bs, Ts, _ = x_sample.shape
    assert Ts == 1, "the decode path handles one new token per sequence"
    page = cache_k_cmp.shape[1]
    past_len = page_table.shape[1] * page
    R = D // HEAD_DIM // N_KV_HEADS
    assert Tp % page == 0 and past_len % SEL_BLOCK == 0

    def attend_prompt(kv, kv_b, q, qr, gates):
        ppb = Tp // page
        table = jnp.arange(Bp * ppb, dtype=jnp.int32).reshape(Bp, ppb)
        pages = lambda a: a.reshape(-1, page, N_KV_HEADS, HEAD_DIM)
        ck, cv = _compress(pages(kv[0]), pages(kv[1]), table, ppb, W, P)
        nc = Tp // CMP_STRIDE - CMP_BLOCK // CMP_STRIDE + 1
        seq = lambda a: a.reshape(Bp, Tp, -1)
        o = attn_prompt(seq(q), seq(qr), ck, cv, kv_b.reshape(-1, Bp, Tp, KV_W), seq(gates), nc=nc)
        nw = min(WINDOW, Tp)
        win = tuple(kv[n].reshape(Bp, Tp, N_KV_HEADS, HEAD_DIM)[:, -nw:] for n in (4, 5))
        return o.reshape(Bp * Tp, D), win

    def attend_sample(kv, kv_b, q, qr, gates):
        del kv_b
        ck, cv = _compress(cache_k_cmp, cache_v_cmp, page_table, min(32, page_table.shape[1]), W, P)
        nc = (past_len - (CMP_BLOCK - 1)) // CMP_STRIDE + 1
        ns = past_len // SEL_BLOCK + 1
        wb = state_k_win.shape[1]
        new_row = lambda a: a.reshape(Bs, 1, N_KV_HEADS, HEAD_DIM)
        kw = jnp.concatenate([state_k_win, new_row(kv[4])], axis=1)[:, -wb:]
        vw = jnp.concatenate([state_v_win, new_row(kv[5])], axis=1)[:, -wb:]
        heads = lambda a: a.reshape(Bs, N_KV_HEADS, R, HEAD_DIM)
        ocmp, owin, ids = attn_dec_dense(heads(q), heads(qr), ck, cv, kw, vw, nc=nc, ns=ns, pos=past_len)
        gt = gates.reshape(Bs, N_KV_HEADS, LANE)[:, :, :R * N_BRANCH].reshape(Bs, N_KV_HEADS, R, N_BRANCH)
        gt = jnp.pad(gt, ((0, 0), (0, 0), (0, 0), (0, LANE - N_BRANCH)))
        o = attn_dec_sel(page_table, ids, heads(qr), cache_k_sel, cache_v_sel,
                         kv[2].reshape(Bs, 1, KV_W), kv[3].reshape(Bs, 1, KV_W), ocmp, owin, gt, pos=past_len)
        return o.reshape(Bs, D).astype(BF16), (kw, vw)

    pool_zero = jnp.zeros((state_pool.shape[0], Bp, POOL_STATE, D), x_prompt.dtype)
    y_p, pool_p, rows_p, win_p = _trunk(x_prompt, p_prompt, pool_zero, 0, W, P, attend_prompt)
    y_s, pool_s, rows_s, win_s = _trunk(x_sample, p_sample, state_pool, past_len, W, P, attend_sample)
    return (y_p, y_s, pool_p, pool_s, rows_p[0], rows_p[1], rows_p[2], rows_p[3], win_p[0], win_p[1],
            rows_s[0], rows_s[1], rows_s[2], rows_s[3], win_s[0], win_s[1])
```

```python
import functools

import jax
import jax.numpy as jnp
import numpy as np
from jax import lax
from jax.experimental import pallas as pl
from jax.experimental.pallas import tpu as pltpu

F32 = jnp.float32
BF16 = jnp.bfloat16

POOL_WINDOWS = (2, 4, 8, 16)
POOL_STATE = max(POOL_WINDOWS) - 1
POOL_PAD = POOL_STATE + 1
HEAD_DIM = 128
N_KV_HEADS = 4
N_BRANCH = 3
CMP_BLOCK = 32
CMP_STRIDE = 16
CMP_HIDDEN = 2 * HEAD_DIM
SEL_BLOCK = 64
N_SEL = 16
WINDOW = 512
ROPE_THETA = 10000.0
EPS = 1e-6
SCALE = HEAD_DIM ** -0.5
NEG = -1e30
FORCE = 1e9
PAD_SCORE = -3e38
KV_W = N_KV_HEADS * HEAD_DIM
LANE = 128
VMEM_LIMIT = 56 * 1024 * 1024


def _params(sem):
    return pltpu.CompilerParams(dimension_semantics=sem, vmem_limit_bytes=VMEM_LIMIT)


def _sigmoid(x):
    return 1.0 / (1.0 + jnp.exp(-x))


def _dot(a, b):
    return jnp.dot(a, b, preferred_element_type=F32)


def _dot_nt(a, b):
    return lax.dot_general(a, b, (((1,), (1,)), ((), ())), preferred_element_type=F32)


def _dot_tn(a, b):
    return lax.dot_general(a, b, (((0,), (0,)), ((), ())), preferred_element_type=F32)


def _head_norm(x, g):
    return x * lax.rsqrt(jnp.mean(x * x, axis=-1, keepdims=True) + EPS) * g


def _rope(x, cosf, sinf):
    return x * cosf + pltpu.roll(x, HEAD_DIM // 2, 1) * sinf


def _fold_lanes(x):
    parts = [x[:, c * LANE:(c + 1) * LANE] for c in range(x.shape[1] // LANE)]
    return functools.reduce(lambda u, v: u + v, parts)


def _norm_producer(epilogue, n_base):
    def wrapped(acc, rows, *extras):
        (h,) = epilogue(acc, rows, *extras[:n_base])
        return (h, *[(h * g[...]).astype(BF16) for g in extras[n_base:]], _fold_lanes(h * h))
    return wrapped


def _row_scaled(epilogue):
    def wrapped(acc, rows, rs_ref, *extras):
        rs = rs_ref[rows, :]
        acc = jnp.concatenate([acc[:, c * LANE:(c + 1) * LANE] * rs for c in range(acc.shape[1] // LANE)], axis=1)
        return epilogue(acc, rows, *extras)
    return wrapped


def _row_scale_kernel(ssq_ref, o_ref, *, d):
    tot = jnp.sum(functools.reduce(lambda u, v: u + v, [ssq_ref[j] for j in range(ssq_ref.shape[0])]),
                  axis=1, keepdims=True)
    o_ref[...] = jnp.broadcast_to(lax.rsqrt(tot / d + EPS), o_ref.shape)


def row_scale(ssq, d):
    gn, M, _ = ssq.shape
    tm = min(M, 1024)
    return pl.pallas_call(
        functools.partial(_row_scale_kernel, d=d),
        grid=(M // tm,),
        in_specs=[pl.BlockSpec((gn, tm, LANE), lambda i: (0, i, 0))],
        out_specs=pl.BlockSpec((tm, LANE), lambda i: (i, 0)),
        out_shape=jax.ShapeDtypeStruct((M, LANE), F32),
        compiler_params=_params(("parallel",)),
        name="row_scale",
    )(ssq)


def _norm_io(gains, M, N, gn, tm, tn):
    extras = [g.reshape(1, N).astype(F32) for g in gains]
    extra_specs = [pl.BlockSpec((1, tn), lambda i, j, k: (0, j)) for _ in gains]
    shapes = [jax.ShapeDtypeStruct((M, N), BF16) for _ in gains] + [jax.ShapeDtypeStruct((gn, M, LANE), F32)]
    specs = [_mn_spec(tm, tn) for _ in gains] + [pl.BlockSpec((1, tm, LANE), lambda i, j, k: (j, i, 0))]
    return extras, extra_specs, shapes, specs


EPILOGUE_ROWS = 256


def _finish_rows(acc_of, tm, outs, extras, epilogue):
    ch = min(tm, EPILOGUE_ROWS)
    for c in range(tm // ch):
        rows = slice(c * ch, (c + 1) * ch)
        for o_ref, r in zip(outs, epilogue(acc_of(rows), rows, *extras)):
            if len(o_ref.shape) == 4:
                for hh in range(o_ref.shape[2]):
                    o_ref[0, rows, hh, :] = r[:, hh * HEAD_DIM:(hh + 1) * HEAD_DIM].astype(o_ref.dtype)
            elif len(o_ref.shape) == 3:
                o_ref[0, rows, :] = r.astype(o_ref.dtype)
            else:
                o_ref[rows, :] = r.astype(o_ref.dtype)


def _side_io(side, n_steps, step):
    s_arr, s_layer = side
    _, rows, cols = s_arr.shape
    rs = rows // n_steps
    assert rs * n_steps == rows and rs % 16 == 0
    in_spec = pl.BlockSpec((None, rs, cols), lambda *g: (s_layer, step(*g), 0))
    out_spec = pl.BlockSpec((rs, cols), lambda *g: (step(*g), 0))
    return s_arr, in_spec, jax.ShapeDtypeStruct((rows, cols), BF16), out_spec


def _mm_kernel(*refs, nk, n_extra, n_out, epilogue, side):
    x_ref, w_ref = refs[0], refs[1]
    extras = refs[2:2 + n_extra]
    n_in = 2 + n_extra + side
    outs = refs[n_in:n_in + n_out]
    tm = x_ref.shape[0]
    if side:
        refs[n_in + n_out][...] = refs[n_in - 1][...].astype(BF16)

    if nk == 1:
        _finish_rows(lambda rows: _dot(x_ref[rows, :], w_ref[...]), tm, outs, extras, epilogue)
    else:
        acc_ref = refs[-1]
        k = pl.program_id(2)

        @pl.when(k == 0)
        def _():
            acc_ref[...] = jnp.zeros_like(acc_ref)

        acc_ref[...] += _dot(x_ref[...], w_ref[...])

        @pl.when(k == nk - 1)
        def _():
            _finish_rows(lambda rows: acc_ref[rows, :], tm, outs, extras, epilogue)


def mm(x, w, *, grid, tm, tn, tk, epilogue, extras=(), extra_specs=(), out_shapes, out_specs,
       x_map=None, w_map=None, side=None, name):
    gm, gn, nk = grid
    x_map = x_map or (lambda i, j, k: (i, k))
    w_map = w_map or (lambda i, j, k: (k, j))
    ins, in_specs = [x, w, *extras], [pl.BlockSpec((tm, tk), x_map), pl.BlockSpec((tk, tn), w_map)]
    in_specs += list(extra_specs)
    n_out, out_shapes, out_specs = len(out_shapes), list(out_shapes), list(out_specs)
    if side is not None:
        s_arr, s_in, s_shape, s_out = _side_io(side, gm * gn * nk, lambda i, j, k: (i * gn + j) * nk + k)
        ins.append(s_arr), in_specs.append(s_in), out_shapes.append(s_shape), out_specs.append(s_out)
    kern = functools.partial(_mm_kernel, nk=nk, n_extra=len(extras), n_out=n_out, epilogue=epilogue,
                             side=side is not None)
    return pl.pallas_call(
        kern,
        grid=grid,
        in_specs=in_specs,
        out_specs=out_specs,
        out_shape=out_shapes,
        scratch_shapes=[pltpu.VMEM((tm, tn), F32)] if nk > 1 else [],
        compiler_params=_params(("parallel", "parallel", "arbitrary")),
        name=name,
    )(*ins)


def _mm_ws_kernel(*refs, nk, n_extra, n_out, epilogue, emit, side):
    x_ref, w_ref = refs[0], refs[1]
    extras = refs[2:2 + n_extra]
    n_in = 2 + n_extra + side
    outs = refs[n_in:n_in + n_out]
    rest = refs[n_in + n_out:]
    wb_ref = rest[emit + side]
    i, k = pl.program_id(1), pl.program_id(2)

    @pl.when(i == 0)
    def _():
        wb_ref[k] = w_ref[...].astype(BF16)
        if emit:
            rest[0][...] = wb_ref[k]

    if side:
        rest[emit][...] = refs[n_in - 1][...].astype(BF16)

    tm = x_ref.shape[0]
    if nk == 1:
        _finish_rows(lambda rows: _dot(x_ref[rows, :], wb_ref[0]), tm, outs, extras, epilogue)
    else:
        acc_ref = rest[-1]

        @pl.when(k == 0)
        def _():
            acc_ref[...] = jnp.zeros_like(acc_ref)

        acc_ref[...] += _dot(x_ref[...], wb_ref[k])

        @pl.when(k == nk - 1)
        def _():
            _finish_rows(lambda rows: acc_ref[rows, :], tm, outs, extras, epilogue)


def mm_ws(x, w, *, layer=None, emit=False, side=None, grid, tm, tn, tk, epilogue, extras=(), extra_specs=(),
          out_shapes, out_specs, name):
    gm, gn, nk = grid
    swap = lambda f: (lambda j, i, k: f(i, j, k))
    respec = lambda s: pl.BlockSpec(s.block_shape, swap(s.index_map))
    k_once = lambda i, k: jnp.where(i == 0, k, nk - 1)
    if layer is None:
        w_spec = pl.BlockSpec((tk, tn), lambda j, i, k: (k_once(i, k), j))
    else:
        w_spec = pl.BlockSpec((None, tk, tn), lambda j, i, k: (layer, k_once(i, k), j))
    n_out = len(out_shapes)
    ins, in_specs = [x, w, *extras], [pl.BlockSpec((tm, tk), lambda j, i, k: (i, k)), w_spec]
    in_specs += [respec(s) for s in extra_specs]
    out_shapes, out_specs = list(out_shapes), [respec(s) for s in out_specs]
    if emit:
        out_shapes.append(jax.ShapeDtypeStruct((nk * tk, gn * tn), BF16))
        out_specs.append(pl.BlockSpec((tk, tn), lambda j, i, k: (k_once(i, k), j)))
    if side is not None:
        s_arr, s_in, s_shape, s_out = _side_io(side, gm * gn * nk, lambda j, i, k: (j * gm + i) * nk + k)
        ins.append(s_arr), in_specs.append(s_in), out_shapes.append(s_shape), out_specs.append(s_out)
    kern = functools.partial(_mm_ws_kernel, nk=nk, n_extra=len(extras), n_out=n_out, epilogue=epilogue, emit=emit,
                             side=side is not None)
    return pl.pallas_call(
        kern,
        grid=(gn, gm, nk),
        in_specs=in_specs,
        out_specs=out_specs,
        out_shape=out_shapes,
        scratch_shapes=[pltpu.VMEM((nk, tk, tn), BF16)] + ([pltpu.VMEM((tm, tn), F32)] if nk > 1 else []),
        compiler_params=_params(("parallel", "arbitrary", "arbitrary")),
        name=name,
    )(*ins)


def _tiles(M, N, K, tm, tn, tk):
    tm, tn, tk = min(tm, M), min(tn, N), min(tk, K)
    return (M // tm, N // tn, K // tk), tm, tn, tk


def _mn_spec(tm, tn):
    return pl.BlockSpec((tm, tn), lambda i, j, k: (i, j))


def _pool_diff_kernel(*refs, tt, pos0, halo):
    x_ref, pre_ref, g_ref = refs[0], refs[1 + halo], refs[2 + halo]
    d_ref, st_ref, seq_ref = refs[3 + halo:]
    t = pl.program_id(1)
    norm = lambda x: x * lax.rsqrt(jnp.mean(x * x, axis=-1, keepdims=True) + EPS) * g_ref[...]
    a = norm(x_ref[0])
    seq_ref[POOL_PAD:POOL_PAD + tt, :] = a

    @pl.when(t == 0)
    def _():
        seq_ref[0:POOL_PAD, :] = pre_ref[0]

    if halo:
        @pl.when(t > 0)
        def _():
            seq_ref[0:POOL_PAD, :] = norm(refs[1][0])

    pos = pos0 + t * tt + lax.broadcasted_iota(jnp.int32, (tt, 1), 0)
    pg = a.shape[1] // len(POOL_WINDOWS)
    for g, w in enumerate(POOL_WINDOWS):
        cols = slice(g * pg, (g + 1) * pg)
        s = a[:, cols]
        for j in range(1, w):
            s = s + seq_ref[POOL_PAD - j:POOL_PAD - j + tt, cols]
        cnt = jnp.minimum(pos + 1, w).astype(F32)
        d_ref[0, :, cols] = (s / cnt - a[:, cols]).astype(d_ref.dtype)
    st_ref[0] = seq_ref[tt:tt + POOL_PAD, :]


def pool_diff(x, gain, prefix, pos0):
    B, T, D = x.shape
    tt = min(T, 256)
    halo = T > tt
    pre = jnp.concatenate([jnp.zeros((B, 1, D), F32), prefix], axis=1)
    hpt = tt // POOL_PAD
    in_specs = [pl.BlockSpec((1, tt, D), lambda b, t: (b, t, 0))]
    if halo:
        in_specs.append(pl.BlockSpec((1, POOL_PAD, D), lambda b, t: (b, jnp.maximum(t * hpt - 1, 0), 0)))
    in_specs += [pl.BlockSpec((1, POOL_PAD, D), lambda b, t: (b, 0, 0)), pl.BlockSpec((1, D), lambda b, t: (0, 0))]
    d, st = pl.pallas_call(
        functools.partial(_pool_diff_kernel, tt=tt, pos0=pos0, halo=halo),
        grid=(B, T // tt),
        in_specs=in_specs,
        out_specs=[pl.BlockSpec((1, tt, D), lambda b, t: (b, t, 0)),
                   pl.BlockSpec((1, POOL_PAD, D), lambda b, t: (b, 0, 0))],
        out_shape=[jax.ShapeDtypeStruct((B, T, D), BF16), jax.ShapeDtypeStruct((B, POOL_PAD, D), F32)],
        scratch_shapes=[pltpu.VMEM((POOL_PAD + tt, D), F32)],
        compiler_params=_params(("parallel", "arbitrary")),
        name="pool_diff",
    )(*([x, x] if halo else [x]), pre, gain.reshape(1, D).astype(F32))
    return d, st[:, 1:]


PAGES_PER_STEP = 8
CMP_FINISH_ROWS = 512


def _cmp_partial_kernel(pt_ref, *refs, steps, pps):
    del pt_ref
    k_pages, v_pages = refs[:pps], refs[pps:2 * pps]
    wk_ref, wv_ref, abk_ref, abv_ref, xk_ref, xv_ref = refs[2 * pps:]
    p = pl.program_id(2)
    G = N_KV_HEADS
    half = CMP_STRIDE // 2
    rows_per_page = (k_pages[0].shape[1] // half) * G
    for pages, x_ref in ((k_pages, xk_ref), (v_pages, xv_ref)):
        for q, page_ref in enumerate(pages):
            for sb in range(page_ref.shape[1] // half):
                row = q * rows_per_page + sb * G
                for r in range(CMP_STRIDE):
                    x_ref[p, row:row + G, r * HEAD_DIM:(r + 1) * HEAD_DIM] = (
                        page_ref[0, sb * half + r // 2, (r % 2) * G:(r % 2 + 1) * G, :])

    @pl.when(p == steps - 1)
    def _():
        for x_ref, w_ref, ab_ref in ((xk_ref, wk_ref, abk_ref), (xv_ref, wv_ref, abv_ref)):
            x = x_ref[...].reshape(ab_ref.shape[1], x_ref.shape[2])
            ab_ref[0] = _dot(x.astype(BF16), w_ref[...])


def cmp_partials(k_pages, v_pages, table, wk_cat, wv_cat, cp):
    B, ppb = table.shape
    page = k_pages.shape[1]
    pps = PAGES_PER_STEP
    rows_per_page = page // CMP_STRIDE * N_KV_HEADS
    nch, steps = ppb // cp, cp // pps
    m = cp * rows_per_page
    kdim = CMP_STRIDE * HEAD_DIM
    pair_rows = lambda a: a.reshape(a.shape[0], page // 2, 2 * N_KV_HEADS, HEAD_DIM)

    def page_spec(q):
        return pl.BlockSpec((1, page // 2, 2 * N_KV_HEADS, HEAD_DIM),
                            lambda b, c, p, pt: (pt[b, c * cp + p * pps + q], 0, 0, 0))

    page_specs = [page_spec(q) for q in range(pps)]
    w_spec = pl.BlockSpec((kdim, 2 * CMP_HIDDEN), lambda b, c, p, pt: (0, 0))
    out_spec = pl.BlockSpec((1, m, 2 * CMP_HIDDEN), lambda b, c, p, pt: (b, c, 0))
    out_shape = jax.ShapeDtypeStruct((B, ppb * rows_per_page, 2 * CMP_HIDDEN), F32)
    x_scratch = pltpu.VMEM((steps, pps * rows_per_page, kdim), F32)
    return pl.pallas_call(
        functools.partial(_cmp_partial_kernel, steps=steps, pps=pps),
        grid_spec=pltpu.PrefetchScalarGridSpec(
            num_scalar_prefetch=1,
            grid=(B, nch, steps),
            in_specs=page_specs + page_specs + [w_spec, w_spec],
            out_specs=[out_spec, out_spec],
            scratch_shapes=[x_scratch, x_scratch],
        ),
        out_shape=[out_shape, out_shape],
        compiler_params=_params(("parallel", "parallel", "arbitrary")),
        name="cmp_partials",
    )(table, *([pair_rows(k_pages)] * pps), *([pair_rows(v_pages)] * pps), wk_cat, wv_cat)


def _cmp_finish_kernel(ab_ref, pe_ref, w1_ref, w2_ref, g_ref, o_ref, out_ref, *, norm):
    n4 = ab_ref.shape[1]
    G = o_ref.shape[1]
    bias = _dot(jnp.broadcast_to(pe_ref[...], (8, pe_ref.shape[1])), w1_ref[...])[0:1, :]
    ch = min(n4, CMP_FINISH_ROWS)
    for c in range(n4 // ch):
        lo, hi = c * ch, (c + 1) * ch
        first = ab_ref[0, lo:hi, :CMP_HIDDEN]
        if hi + G <= n4:
            second = ab_ref[0, lo + G:hi + G, CMP_HIDDEN:]
        else:
            second = jnp.concatenate([ab_ref[0, lo + G:hi, CMP_HIDDEN:], ab_ref[0, hi - G:hi, CMP_HIDDEN:]], axis=0)
        pre = first + second + bias
        out = _dot((pre * _sigmoid(pre)).astype(BF16), w2_ref[...])
        if norm:
            out = _head_norm(out, g_ref[...])
        out_ref[lo:hi, :] = out
    for g in range(G):
        o_ref[0, g] = out_ref[pl.ds(g, n4 // G, stride=G), :].astype(o_ref.dtype)


def cmp_finish(ab, pe, w1, w2, gain, *, norm):
    B, n4, _ = ab.shape
    G = N_KV_HEADS
    full = lambda a: pl.BlockSpec(a.shape, lambda b: (0,) * a.ndim)
    args = (pe, w1, w2, gain)
    return pl.pallas_call(
        functools.partial(_cmp_finish_kernel, norm=norm),
        grid=(B,),
        in_specs=[pl.BlockSpec((1, n4, 2 * CMP_HIDDEN), lambda b: (b, 0, 0))] + [full(a) for a in args],
        out_specs=pl.BlockSpec((1, G, n4 // G, HEAD_DIM), lambda b: (b, 0, 0, 0)),
        out_shape=jax.ShapeDtypeStruct((B, G, n4 // G, HEAD_DIM), BF16),
        scratch_shapes=[pltpu.VMEM((n4, HEAD_DIM), F32)],
        compiler_params=_params(("parallel",)),
        name="cmp_finish",
    )(ab, *args)


def _select_blocks(score, blk, ns):
    rank = jnp.zeros(score.shape, jnp.int32)
    for j in range(ns):
        sj = score[j:j + 1, :]
        beats = (sj > score) | ((sj == score) & (j < blk))
        rank = rank + beats.astype(jnp.int32)
    return (rank < min(N_SEL, ns)) & (score > 0.5 * NEG)


def _attn_prompt_kernel(q_ref, qr_ref, ck_ref, cv_ref, ks_ref, vs_ref, kw_ref, vw_ref, gate_ref, mapT_ref,
                        expand_ref, o_ref, part_ref, sbias_ref, wbias_ref, m_ref, acc_ref, *, tq, kc, nc, ns, R):
    qi = pl.program_id(2)
    q0 = qi * tq
    pos = q0 + lax.broadcasted_iota(jnp.int32, (tq, 1), 0)
    ncp = ck_ref.shape[2]

    ck = ck_ref[0, 0]
    cv = cv_ref[0, 0]
    cidx = lax.broadcasted_iota(jnp.int32, (1, ncp), 1)
    ok_c = (cidx * CMP_STRIDE + CMP_BLOCK - 1 <= pos) & (cidx < nc)
    imp = jnp.zeros((tq, ncp), F32)
    for r in range(R):
        qh = q_ref[0, :, r * HEAD_DIM:(r + 1) * HEAD_DIM]
        s = jnp.where(ok_c, _dot_nt(qh, ck) * SCALE, NEG)
        e = jnp.exp(s - jnp.max(s, axis=1, keepdims=True))
        p = jnp.where(ok_c, e * (1.0 / jnp.sum(e, axis=1, keepdims=True)), 0.0)
        imp = imp + p
        part_ref[r] = gate_ref[0, :, r * N_BRANCH:r * N_BRANCH + 1] * _dot(p.astype(BF16), cv)

    nsp = mapT_ref.shape[0]
    p_slc = lax.dot_general(mapT_ref[...], imp, (((1,), (1,)), ((), ())), precision=lax.Precision.HIGHEST,
                            preferred_element_type=F32)
    blk = lax.broadcasted_iota(jnp.int32, (nsp, tq), 0)
    pos_l = q0 + lax.broadcasted_iota(jnp.int32, (nsp, tq), 1)
    cur = pos_l // SEL_BLOCK
    vis = blk * SEL_BLOCK <= pos_l
    forced = vis & ((blk == 0) | (blk == cur) | (blk == cur - 1))
    score = jnp.where(forced, FORCE, jnp.where(vis, p_slc, NEG))
    score = jnp.where(blk < ns, score, PAD_SCORE)
    sel = _select_blocks(score, blk, ns).astype(BF16)

    c_hi = (q0 + tq) // kc
    col = lax.broadcasted_iota(jnp.int32, (1, kc), 1)
    sel_keys = _dot_tn(sel, expand_ref[...])
    for c in range(sbias_ref.shape[0]):
        @pl.when(c < c_hi)
        def _(c=c):
            ok = (sel_keys[:, c * kc:(c + 1) * kc] > 0.5) & (c * kc + col <= pos)
            sbias_ref[c] = jnp.where(ok, 0.0, NEG)
    nwc = wbias_ref.shape[0]
    c_w0 = c_hi - nwc
    for d in range(nwc):
        kpos = (c_w0 + d) * kc + col
        wbias_ref[d] = jnp.where((kpos <= pos) & (kpos > pos - WINDOW), 0.0, NEG)

    def fold(t):
        return [t[:, i * LANE:(i + 1) * LANE] for i in range(kc // LANE)]

    ones_col = (lax.broadcasted_iota(jnp.int32, (kc, LANE), 1) == 0).astype(BF16)

    def branch(k_ref, v_ref, c_lo, bias_of, gate_col):
        def logits(r, c, bias):
            k = k_ref[0, pl.ds(pl.multiple_of(c * kc, kc), kc), :]
            return _dot_nt(qr_ref[0, :, r * HEAD_DIM:(r + 1) * HEAD_DIM], k) * SCALE + bias

        m_ref[...] = jnp.full(m_ref.shape, NEG, F32)
        acc_ref[...] = jnp.zeros(acc_ref.shape, F32)

        def max_body(c, carry):
            bias = bias_of(c)
            for r in range(R):
                mx = m_ref[r]
                for part in fold(logits(r, c, bias)):
                    mx = jnp.maximum(mx, part)
                m_ref[r] = mx
            return carry

        lax.fori_loop(c_lo, c_hi, max_body, 0)
        for r in range(R):
            m_ref[r] = jnp.broadcast_to(jnp.max(m_ref[r], axis=1, keepdims=True), (tq, LANE))

        def sum_body(c, carry):
            bias = bias_of(c)
            v = jnp.concatenate([v_ref[0, pl.ds(pl.multiple_of(c * kc, kc), kc), :], ones_col], axis=1)
            for r in range(R):
                t = logits(r, c, bias)
                m = m_ref[r]
                ps = [jnp.exp(part - m) for part in fold(t)]
                acc_ref[r] += _dot(jnp.concatenate(ps, axis=1).astype(BF16), v)
            return carry

        lax.fori_loop(c_lo, c_hi, sum_body, 0)
        for r in range(R):
            gate = gate_ref[0, :, r * N_BRANCH + gate_col:r * N_BRANCH + gate_col + 1]
            part_ref[r] += (gate * (1.0 / acc_ref[r, :, HEAD_DIM:HEAD_DIM + 1])) * acc_ref[r, :, :HEAD_DIM]

    branch(ks_ref, vs_ref, 0, lambda c: sbias_ref[c], 1)
    branch(kw_ref, vw_ref, jnp.maximum(c_w0, 0), lambda c: wbias_ref[c - c_w0], 2)
    for r in range(R):
        o_ref[0, :, r * HEAD_DIM:(r + 1) * HEAD_DIM] = part_ref[r].astype(o_ref.dtype)


def _overlap_map(ncp, nsp, ns):
    ratio = CMP_BLOCK // CMP_STRIDE
    per_sel = SEL_BLOCK // CMP_STRIDE
    m = np.zeros((ncp, nsp), np.float32)
    for b in range(ns):
        for mm_ in range(per_sel):
            for n in range(ratio):
                j = per_sel * b + mm_ - n
                if 0 <= j < ncp:
                    m[j, b] += 1.0
    return m


def attn_prompt(q, qr, ck, cv, kvb, gates, *, nc):
    B, T, HD = q.shape
    G = N_KV_HEADS
    R = HD // HEAD_DIM // G
    tq = min(T, 256)
    kc = tq
    ns = T // SEL_BLOCK
    nsp = -(-ns // 8) * 8
    ncp = ck.shape[2]
    mapT = jnp.asarray(_overlap_map(ncp, nsp, ns).T)
    expand = jnp.asarray((np.arange(T)[None, :] // SEL_BLOCK == np.arange(nsp)[:, None]).astype(np.float32), BF16)
    q_spec = pl.BlockSpec((1, tq, R * HEAD_DIM), lambda b, g, i: (b, i, g))
    c_spec = pl.BlockSpec((1, 1, ncp, HEAD_DIM), lambda b, g, i: (b, g, 0, 0))
    kv_spec = lambda n: pl.BlockSpec((None, 1, T, HEAD_DIM), lambda b, g, i: (n, b, 0, g))
    return pl.pallas_call(
        functools.partial(_attn_prompt_kernel, tq=tq, kc=kc, nc=nc, ns=ns, R=R),
        grid=(B, G, T // tq),
        in_specs=[q_spec, q_spec, c_spec, c_spec, kv_spec(2), kv_spec(3), kv_spec(4), kv_spec(5),
                  pl.BlockSpec((1, tq, LANE), lambda b, g, i: (b, i, g)),
                  pl.BlockSpec((nsp, ncp), lambda b, g, i: (0, 0)),
                  pl.BlockSpec((nsp, T), lambda b, g, i: (0, 0))],
        out_specs=q_spec,
        out_shape=jax.ShapeDtypeStruct((B, T, HD), BF16),
        scratch_shapes=[pltpu.VMEM((R, tq, HEAD_DIM), F32), pltpu.VMEM((T // kc, tq, kc), F32),
                        pltpu.VMEM((min(WINDOW, T) // kc + tq // kc, tq, kc), F32)]
                       + [pltpu.VMEM((R, tq, HEAD_DIM), F32), pltpu.VMEM((R, tq, HEAD_DIM + LANE), F32)],
        compiler_params=_params(("parallel", "parallel", "arbitrary")),
        name="attn_prompt",
    )(q, qr, ck, cv, kvb, kvb, kvb, kvb, gates, mapT, expand)


def _attn_dec_dense_kernel(q_ref, qr_ref, ck_ref, cv_ref, kw_ref, vw_ref, map_ref, ocmp_ref, owin_ref, ids_ref,
                           *, nc, ns, pos):
    G, R = q_ref.shape[1], q_ref.shape[2]
    ncp = ck_ref.shape[2]
    nsl = map_ref.shape[1]
    cidx = lax.broadcasted_iota(jnp.int32, (1, ncp), 1)
    ok_c = (cidx * CMP_STRIDE + CMP_BLOCK - 1 <= pos) & (cidx < nc)
    blk_l = lax.broadcasted_iota(jnp.int32, (1, nsl), 1)
    cur = pos // SEL_BLOCK
    vis = blk_l * SEL_BLOCK <= pos
    forced = vis & ((blk_l == 0) | (blk_l == cur) | (blk_l == cur - 1))
    ii = lax.broadcasted_iota(jnp.int32, (nsl, nsl), 0)
    jj = lax.broadcasted_iota(jnp.int32, (nsl, nsl), 1)
    slot = lax.broadcasted_iota(jnp.int32, (nsl, LANE), 1).astype(F32)
    blk_s = lax.broadcasted_iota(jnp.int32, (nsl, LANE), 0).astype(F32)
    for g in range(G):
        s = jnp.where(ok_c, _dot_nt(q_ref[0, g], ck_ref[0, g]) * SCALE, NEG)
        e = jnp.exp(s - jnp.max(s, axis=1, keepdims=True))
        p = jnp.where(ok_c, e * (1.0 / jnp.sum(e, axis=1, keepdims=True)), 0.0)
        ocmp_ref[0, g] = _dot(p.astype(BF16), cv_ref[0, g])
        imp = jnp.broadcast_to(jnp.sum(p, axis=0, keepdims=True), (R, ncp))
        p_slc = jnp.dot(imp, map_ref[...], precision=lax.Precision.HIGHEST, preferred_element_type=F32)[0:1, :]
        score_l = jnp.where(forced, FORCE, jnp.where(vis, p_slc, NEG))
        score_l = jnp.where(blk_l < ns, score_l, PAD_SCORE)
        score_s = jnp.sum(jnp.where(ii == jj, score_l, 0.0), axis=1, keepdims=True)
        beats = (score_l > score_s) | ((score_l == score_s) & (jj < ii))
        rank = jnp.sum(beats.astype(F32), axis=1, keepdims=True)
        ids = jnp.sum(jnp.where(rank == slot, blk_s, 0.0), axis=0, keepdims=True)
        ids_ref[0, g] = ids[:, :N_SEL].astype(jnp.int32)
        kw = kw_ref[0, :, g, :].astype(BF16)
        vw = vw_ref[0, :, g, :].astype(BF16)
        s = _dot_nt(qr_ref[0, g], kw) * SCALE
        e = jnp.exp(s - jnp.max(s, axis=1, keepdims=True))
        p = e * (1.0 / jnp.sum(e, axis=1, keepdims=True))
        owin_ref[0, g] = _dot(p.astype(BF16), vw)


def attn_dec_dense(q, qr, ck, cv, kw, vw, *, nc, ns, pos):
    B, G, R, _ = q.shape
    ncp = ck.shape[2]
    wb = kw.shape[1]
    nsl = -(-ns // LANE) * LANE
    omap = jnp.asarray(_overlap_map(ncp, nsl, ns))
    q_spec = pl.BlockSpec((1, G, R, HEAD_DIM), lambda b: (b, 0, 0, 0))
    c_spec = pl.BlockSpec((1, G, ncp, HEAD_DIM), lambda b: (b, 0, 0, 0))
    w_spec = pl.BlockSpec((1, wb, G, HEAD_DIM), lambda b: (b, 0, 0, 0))
    return pl.pallas_call(
        functools.partial(_attn_dec_dense_kernel, nc=nc, ns=ns, pos=pos),
        grid=(B,),
        in_specs=[q_spec, q_spec, c_spec, c_spec, w_spec, w_spec, pl.BlockSpec((ncp, nsl), lambda b: (0, 0))],
        out_specs=[q_spec, q_spec, pl.BlockSpec((1, G, 1, N_SEL), lambda b: (b, 0, 0, 0))],
        out_shape=[jax.ShapeDtypeStruct((B, G, R, HEAD_DIM), F32), jax.ShapeDtypeStruct((B, G, R, HEAD_DIM), F32),
                   jax.ShapeDtypeStruct((B, G, 1, N_SEL), jnp.int32)],
        compiler_params=_params(("parallel",)),
        name="attn_dec_dense",
    )(q, qr, ck, cv, kw, vw, omap)


def _attn_dec_sel_kernel(pt_ref, ids_ref, qr_ref, *refs, n_past, pos):
    del pt_ref
    G = qr_ref.shape[1]
    kc_refs, vc_refs = refs[:G], refs[G:2 * G]
    kn_ref, vn_ref, ocmp_ref, owin_ref, gate_ref, o_ref, m_ref, l_ref, acc_ref = refs[2 * G:]
    b, n = pl.program_id(0), pl.program_id(1)

    @pl.when(n == 0)
    def _():
        m_ref[...] = jnp.full(m_ref.shape, NEG, F32)
        l_ref[...] = jnp.zeros(l_ref.shape, F32)
        acc_ref[...] = jnp.zeros(acc_ref.shape, F32)

    row = lax.broadcasted_iota(jnp.int32, (SEL_BLOCK, 1), 0)
    lane = lax.broadcasted_iota(jnp.int32, (1, SEL_BLOCK), 1)
    key_row = jnp.where(lane < SEL_BLOCK // 2, 2 * lane, 2 * lane - (SEL_BLOCK - 1))
    head_rows = lambda ref, g: jnp.concatenate([ref[0, :, g, :], ref[0, :, G + g, :]], axis=0)
    for g in range(G):
        bid = ids_ref[(b * G + g) * N_SEL + n]
        is_new = bid >= n_past
        first = (row == 0) & (bid == n_past)
        sl = slice(g * HEAD_DIM, (g + 1) * HEAD_DIM)
        k = jnp.where(is_new, jnp.where(first, kn_ref[0, :, sl], 0.0), head_rows(kc_refs[g], g)).astype(BF16)
        v = jnp.where(is_new, jnp.where(first, vn_ref[0, :, sl], 0.0), head_rows(vc_refs[g], g)).astype(BF16)
        kpos = bid * SEL_BLOCK + key_row
        ok = kpos <= pos
        s = jnp.where(ok, _dot_nt(qr_ref[0, g], k) * SCALE, NEG)
        m_prev = m_ref[g]
        m_new = jnp.maximum(m_prev, jnp.max(s, axis=1, keepdims=True))
        alpha = jnp.exp(m_prev - m_new)
        p = jnp.where(ok, jnp.exp(s - m_new), 0.0)
        l_ref[g] = alpha * l_ref[g] + jnp.sum(p, axis=1, keepdims=True)
        acc_ref[g] = alpha * acc_ref[g] + _dot(p.astype(BF16), v)
        m_ref[g] = m_new

    @pl.when(n == N_SEL - 1)
    def _():
        for g in range(G):
            gt = gate_ref[0, g]
            o_sel = acc_ref[g] * (1.0 / l_ref[g])
            o_ref[0, g] = gt[:, 0:1] * ocmp_ref[0, g] + gt[:, 1:2] * o_sel + gt[:, 2:3] * owin_ref[0, g]


def attn_dec_sel(table, ids, qr, k_cache, v_cache, k_new, v_new, ocmp, owin, gates, *, pos):
    B, G, R, _ = qr.shape
    page = k_cache.shape[1]
    bpp = page // SEL_BLOCK
    n_past = table.shape[1] * bpp
    kc = k_cache.reshape(k_cache.shape[0] * bpp, SEL_BLOCK // 2, 2 * G, HEAD_DIM)
    vc = v_cache.reshape(v_cache.shape[0] * bpp, SEL_BLOCK // 2, 2 * G, HEAD_DIM)

    def cache_spec(g):
        def index(b, n, pt, ids_):
            bid = jnp.minimum(ids_[(b * G + g) * N_SEL + n], n_past - 1)
            return (pt[b, bid // bpp] * bpp + bid % bpp, 0, 0, 0)
        return pl.BlockSpec((1, SEL_BLOCK // 2, 2 * G, HEAD_DIM), index)

    q_spec = pl.BlockSpec((1, G, R, HEAD_DIM), lambda b, n, pt, ids_: (b, 0, 0, 0))
    n_spec = pl.BlockSpec((1, 1, G * HEAD_DIM), lambda b, n, pt, ids_: (b, 0, 0))
    g_spec = pl.BlockSpec((1, G, R, LANE), lambda b, n, pt, ids_: (b, 0, 0, 0))
    c_specs = [cache_spec(g) for g in range(G)]
    return pl.pallas_call(
        functools.partial(_attn_dec_sel_kernel, n_past=n_past, pos=pos),
        grid_spec=pltpu.PrefetchScalarGridSpec(
            num_scalar_prefetch=2,
            grid=(B, N_SEL),
            in_specs=[q_spec] + c_specs + c_specs + [n_spec, n_spec, q_spec, q_spec, g_spec],
            out_specs=q_spec,
            scratch_shapes=[pltpu.VMEM((G, R, 1), F32), pltpu.VMEM((G, R, 1), F32),
                            pltpu.VMEM((G, R, HEAD_DIM), F32)],
        ),
        out_shape=jax.ShapeDtypeStruct((B, G, R, HEAD_DIM), F32),
        compiler_params=_params(("parallel", "arbitrary")),
        name="attn_dec_sel",
    )(table, ids.reshape(-1), qr, *([kc] * G), *([vc] * G), k_new, v_new, ocmp, owin, gates)


def _rope_tables(pos):
    half = HEAD_DIM // 2
    inv = ROPE_THETA ** (-jnp.arange(half, dtype=F32) / half)
    ang = pos.astype(F32)[:, None] * inv[None, :]
    cos, sin = jnp.cos(ang), jnp.sin(ang)
    return jnp.concatenate([cos, cos], axis=1), jnp.concatenate([-sin, sin], axis=1)


def _prep_weights(W):
    D = W['w_kv'].shape[0]
    H = D // HEAD_DIM
    R = H // N_KV_HEADS
    pg = D // len(POOL_WINDOWS)
    half = CMP_STRIDE * HEAD_DIM
    P = {}
    P['w_pool'] = W['w_pool'].astype(BF16).reshape(-1, len(POOL_WINDOWS) * pg, pg)
    kv_gain = jnp.ones((W['w_kv'].shape[1], KV_W), F32)
    kv_gain = kv_gain.at[2].set(jnp.tile(W['g_k_sel'], N_KV_HEADS)).at[4].set(jnp.tile(W['g_k_win'], N_KV_HEADS))
    P['kv_gain'] = kv_gain
    for t in ('k', 'v'):
        w1 = W['w_cmp_%s1' % t].astype(BF16)
        P['w_cmp_%s1' % t] = w1
        P['w_cmp_%scat' % t] = jnp.concatenate([w1[:half], w1[half:]], axis=1)
        P['w_cmp_%s2' % t] = W['w_cmp_%s2' % t].astype(BF16)
        P['pe_%s' % t] = W['pe_cmp_%s' % t].astype(BF16).reshape(1, -1)
    n_b = W['w_qg'].shape[0]
    wg = W['w_qg'][:, :, H * HEAD_DIM:].astype(BF16).reshape(n_b, D, N_KV_HEADS, R * N_BRANCH)
    wg = jnp.pad(wg, ((0, 0), (0, 0), (0, 0), (0, LANE - R * N_BRANCH)))
    P['w_gate'] = wg.reshape(n_b, D, N_KV_HEADS * LANE)
    P['w_ple'] = W['w_ple'].astype(BF16)
    P[('w_up', 0)] = W['w_up'][0].astype(BF16)
    return P


def _dense(x, wname, layer, W, P, *, N, epilogue, specs, extras=(), out_dtypes, emit=False, side_cast=None, row_scale_in=None,
           norm_gains=None, name):
    M, K = x.shape
    shapes = [d if isinstance(d, jax.ShapeDtypeStruct) else jax.ShapeDtypeStruct((M, N), d) for d in out_dtypes]
    key = (wname, layer)
    tmx = 1024 if M >= 1024 else M
    use_mm = key in P
    if use_mm:
        grid, tm, tn, tk = _tiles(M, N, K, tmx, 1024, 2048 if (M >= 1024 and K > 4096) else 4096)
    else:
        grid, tm, tn, tk = _tiles(M, N, K, tmx, 512, K if K <= 4096 else 2048)
    extra_specs, out_specs = specs(tm, tn)
    extras = list(extras)
    if row_scale_in is not None:
        epilogue = _row_scaled(epilogue)
        extras.insert(0, row_scale_in)
        extra_specs = [pl.BlockSpec((tm, LANE), lambda i, j, k: (i, 0))] + list(extra_specs)
    if norm_gains:
        epilogue = _norm_producer(epilogue, len(extras))
        n_ex, n_ex_specs, n_shapes, n_specs = _norm_io(norm_gains, M, N, grid[1], tm, tn)
        extras, extra_specs = extras + n_ex, list(extra_specs) + n_ex_specs
        shapes, out_specs = shapes + n_shapes, list(out_specs) + n_specs
    side = None if side_cast is None else (W[side_cast[0]], side_cast[1])
    if use_mm:
        outs = list(mm(x, P[key], grid=grid, tm=tm, tn=tn, tk=tk, epilogue=epilogue, extras=extras,
                       extra_specs=extra_specs, out_shapes=shapes, out_specs=out_specs, side=side, name=name))
    else:
        w = W[wname]
        outs = list(mm_ws(x, w, layer=layer if w.ndim == 3 else None, emit=emit, side=side, grid=grid, tm=tm,
                          tn=tn, tk=tk, epilogue=epilogue, extras=extras, extra_specs=extra_specs,
                          out_shapes=shapes, out_specs=out_specs, name=name))
    if side is not None:
        P[side_cast] = outs.pop()
    if emit and not use_mm:
        P[key] = outs.pop()
    if norm_gains:
        outs.append(row_scale(outs.pop(), N))
    return outs


def _ffn_ple(h, normed, p_l, layer, W, P, next_gains):
    M, D = h.shape
    F = W['w_up'].shape[2]
    host = M >= 1024
    next_up = ('w_up', layer + 1) if host and layer + 1 < W['w_up'].shape[0] else None
    mn = lambda tm, tn: ([], [_mn_spec(tm, tn)])
    res = lambda tm, tn: ([_mn_spec(tm, tn)], [_mn_spec(tm, tn)])
    y, rs = normed
    (u,) = _dense(y, 'w_up', layer, W, P, N=F, epilogue=lambda acc, rows: (jnp.square(jnp.maximum(acc, 0.0)),),
                  specs=mn, out_dtypes=[BF16], side_cast=('w_down', layer) if host else None, row_scale_in=rs,
                  name="ffn_up")
    h, y, rs = _dense(u, 'w_down', layer, W, P, N=D, epilogue=lambda acc, rows, r: (r[rows, :] + acc,), specs=res,
                      extras=[h], out_dtypes=[F32], norm_gains=[W['g_ple'][layer]], side_cast=next_up,
                      name="ffn_down")
    ple_dim = p_l.shape[1]
    ple_specs = lambda tm, tn: ([_mn_spec(tm, tn), pl.BlockSpec((tm, ple_dim), lambda i, j, k: (i, 0)),
                                 pl.BlockSpec((ple_dim, tn), lambda i, j, k: (0, j))], [_mn_spec(tm, tn)])
    return _dense(y, 'w_ple_gate', layer, W, P, N=D,
                  epilogue=lambda acc, rows, r, pp, wp: (r[rows, :] + _dot(pp[rows, :], wp[...]) * _sigmoid(acc),),
                  specs=ple_specs, extras=[h, p_l.astype(BF16), P['w_ple'][layer]], out_dtypes=[F32],
                  row_scale_in=rs, norm_gains=next_gains, name="ple")


def _kv_epilogue(acc, rows, gain_ref, cos_ref, sin_ref):
    j = pl.program_id(0)
    cosf, sinf = cos_ref[rows, :], sin_ref[rows, :]
    heads = []
    for hh in range(N_KV_HEADS):
        sl = slice(hh * HEAD_DIM, (hh + 1) * HEAD_DIM)
        heads.append(_rope(_head_norm(acc[:, sl], gain_ref[0, :, sl]), cosf, sinf))
    out = jnp.where((j == 2) | (j == 4), jnp.concatenate(heads, axis=1), acc)
    return out, out


def _q_epilogue(acc, rows, gq_ref, cos_ref, sin_ref):
    cosf, sinf = cos_ref[rows, :], sin_ref[rows, :]
    qs, qrs = [], []
    for hh in range(acc.shape[1] // HEAD_DIM):
        qn = _head_norm(acc[:, hh * HEAD_DIM:(hh + 1) * HEAD_DIM], gq_ref[...])
        qs.append(qn)
        qrs.append(_rope(qn, cosf, sinf))
    return jnp.concatenate(qs, axis=1), jnp.concatenate(qrs, axis=1)


def _trunk(x, p, pool_prefix, pos0, W, P, attend):
    B, T, D = x.shape
    M = B * T
    tmx = 1024 if M >= 1024 else M
    h = x.reshape(M, D)
    pg = D // len(POOL_WINDOWS)

    d, pool_new = pool_diff(x, W['g_mix'][0], pool_prefix[0], pos0)
    d, pool_new = d.reshape(M, D), pool_new[None]
    grid, tm, tn, tk = _tiles(M, D, pg, tmx, pg, pg)
    n_ex, n_ex_specs, n_shapes, n_specs = _norm_io([W['g_ffn'][0]], M, D, grid[1], tm, tn)
    h, y, ssq = mm(d, P['w_pool'][0], grid=grid, tm=tm, tn=tn, tk=tk,
                   x_map=lambda i, j, k: (i, j), w_map=lambda i, j, k: (j, 0),
                   epilogue=_norm_producer(lambda acc, rows, sc, r: (r[rows, :] + acc * sc[...],), 2),
                   extras=[W['pool_scale'][0].reshape(1, D), h] + n_ex,
                   extra_specs=[pl.BlockSpec((1, tn), lambda i, j, k: (0, j)), _mn_spec(tm, tn)] + n_ex_specs,
                   out_shapes=[jax.ShapeDtypeStruct((M, D), F32)] + n_shapes,
                   out_specs=[_mn_spec(tm, tn)] + n_specs, name="pool_mix")
    h, hkv, a1, rs = _ffn_ple(h, (y, row_scale(ssq, D)), p[0].reshape(M, -1), 0, W, P,
                              [W['g_kv'], W['g_mix'][1]])

    pos = pos0 + jnp.tile(jnp.arange(T, dtype=jnp.int32), B)
    cosf, sinf = _rope_tables(pos)
    n_kv = W['w_kv'].shape[1]
    rope_spec = lambda tm: pl.BlockSpec((tm, HEAD_DIM), lambda i, j, k: (i, 0))
    kv_spec = lambda tm: pl.BlockSpec((1, tm, KV_W), lambda i, j, k: (j, i, 0))
    kv4_spec = lambda tm: pl.BlockSpec((1, tm, N_KV_HEADS, HEAD_DIM), lambda i, j, k: (j, i, 0, 0))
    kv_specs = lambda tm, tn: ([pl.BlockSpec((1, 1, KV_W), lambda i, j, k: (j, 0, 0)), rope_spec(tm), rope_spec(tm)],
                               [kv4_spec(tm), kv_spec(tm)])
    kv, kv_b = _dense(hkv, 'w_kv2d', None, W, P, N=n_kv * KV_W, epilogue=_kv_epilogue, specs=kv_specs,
                      extras=[P['kv_gain'].reshape(n_kv, 1, KV_W), cosf, sinf],
                      out_dtypes=[jax.ShapeDtypeStruct((n_kv, M, N_KV_HEADS, HEAD_DIM), F32),
                                  jax.ShapeDtypeStruct((n_kv, M, KV_W), BF16)], row_scale_in=rs, name="kv_proj")
    q_specs = lambda tm, tn: ([pl.BlockSpec((1, HEAD_DIM), lambda i, j, k: (0, 0)), rope_spec(tm), rope_spec(tm)],
                              [_mn_spec(tm, tn)] * 2)
    q, qr = _dense(a1, 'w_qg', 0, W, P, N=D, epilogue=_q_epilogue, specs=q_specs,
                   extras=[W['g_q'][0].reshape(1, HEAD_DIM), cosf, sinf], out_dtypes=[BF16, BF16], row_scale_in=rs,
                   name="q_proj")
    ng = N_KV_HEADS * LANE
    grid, tm, tn, tk = _tiles(M, ng, D, tmx, ng, D)
    (gates,) = mm(a1, P['w_gate'][0], grid=grid, tm=tm, tn=tn, tk=tk,
                  epilogue=_row_scaled(lambda acc, rows: (_sigmoid(acc),)), extras=[rs],
                  extra_specs=[pl.BlockSpec((tm, LANE), lambda i, j, k: (i, 0))],
                  out_shapes=[jax.ShapeDtypeStruct((M, ng), F32)], out_specs=[_mn_spec(tm, tn)], name="gate_proj")

    o, win_state = attend(kv, kv_b, q, qr, gates)

    h, y, rs = _dense(o, 'w_o', 0, W, P, N=D, epilogue=lambda acc, rows, r: (r[rows, :] + acc,), extras=[h],
                      specs=lambda tm, tn: ([_mn_spec(tm, tn)], [_mn_spec(tm, tn)]), out_dtypes=[F32],
                      norm_gains=[W['g_ffn'][1]], name="attn_out")
    (h,) = _ffn_ple(h, (y, rs), p[1].reshape(M, -1), 1, W, P, [])
    rows = tuple(kv[n].reshape(B, T, N_KV_HEADS, HEAD_DIM) for n in range(4))
    return h.reshape(B, T, D), pool_new, rows, win_state


def _compress(k_pages, v_pages, table, cp, W, P):
    abk, abv = cmp_partials(k_pages, v_pages, table, P['w_cmp_kcat'], P['w_cmp_vcat'], cp)
    gain = W['g_k_cmp'].reshape(1, HEAD_DIM)
    ck = cmp_finish(abk, P['pe_k'], P['w_cmp_k1'], P['w_cmp_k2'], gain, norm=True)
    cv = cmp_finish(abv, P['pe_v'], P['w_cmp_v1'], P['w_cmp_v2'], gain, norm=False)
    return ck, cv


def kernel(x_prompt, x_sample, state_pool, cache_k_cmp, cache_v_cmp, cache_k_sel, cache_v_sel, state_k_win, state_v_win, page_table, p_prompt, p_sample, g_mix, w_pool, pool_scale, g_kv, w_kv, g_k_cmp, g_k_sel, g_k_win, w_cmp_k1, w_cmp_k2, pe_cmp_k, w_cmp_v1, w_cmp_v2, pe_cmp_v, w_qg, g_q, w_o, g_ffn, w_up, w_down, g_ple, w_ple, w_ple_gate):
    W = dict(g_mix=g_mix, w_pool=w_pool, pool_scale=pool_scale, g_kv=g_kv, w_kv=w_kv, g_k_cmp=g_k_cmp,
             g_k_sel=g_k_sel, g_k_win=g_k_win, w_cmp_k1=w_cmp_k1, w_cmp_k2=w_cmp_k2, pe_cmp_k=pe_cmp_k,
             w_cmp_v1=w_cmp_v1, w_cmp_v2=w_cmp_v2, pe_cmp_v=pe_cmp_v, w_qg=w_qg, g_q=g_q, w_o=w_o,
             g_ffn=g_ffn, w_up=w_up, w_down=w_down, g_ple=g_ple, w_ple=w_ple, w_ple_gate=w_ple_gate)
    P = _prep_weights(W)
    W['w_kv2d'] = w_kv.reshape(w_kv.shape[0], -1)
    Bp, Tp, D = x_prompt.shape
    Bs, Ts, _ = x_sample.shape
    assert Ts == 1, "the decode path handles one new token per sequence"
    page = cache_k_cmp.shape[1]
    past_len = page_table.shape[1] * page
    R = D // HEAD_DIM // N_KV_HEADS
    assert Tp % page == 0 and past_len % SEL_BLOCK == 0

    def attend_prompt(kv, kv_b, q, qr, gates):
        ppb = Tp // page
        table = jnp.arange(Bp * ppb, dtype=jnp.int32).reshape(Bp, ppb)
        pages = lambda a: a.reshape(-1, page, N_KV_HEADS, HEAD_DIM)
        ck, cv = _compress(pages(kv[0]), pages(kv[1]), table, ppb, W, P)
        nc = Tp // CMP_STRIDE - CMP_BLOCK // CMP_STRIDE + 1
        seq = lambda a: a.reshape(Bp, Tp, -1)
        o = attn_prompt(seq(q), seq(qr), ck, cv, kv_b.reshape(-1, Bp, Tp, KV_W), seq(gates), nc=nc)
        nw = min(WINDOW, Tp)
        win = tuple(kv[n].reshape(Bp, Tp, N_KV_HEADS, HEAD_DIM)[:, -nw:] for n in (4, 5))
        return o.reshape(Bp * Tp, D), win

    def attend_sample(kv, kv_b, q, qr, gates):
        del kv_b
        ck, cv = _compress(cache_k_cmp, cache_v_cmp, page_table, min(32, page_table.shape[1]), W, P)
        nc = (past_len - (CMP_BLOCK - 1)) // CMP_STRIDE + 1
        ns = past_len // SEL_BLOCK + 1
        wb = state_k_win.shape[1]
        new_row = lambda a: a.reshape(Bs, 1, N_KV_HEADS, HEAD_DIM)
        kw = jnp.concatenate([state_k_win, new_row(kv[4])], axis=1)[:, -wb:]
        vw = jnp.concatenate([state_v_win, new_row(kv[5])], axis=1)[:, -wb:]
        heads = lambda a: a.reshape(Bs, N_KV_HEADS, R, HEAD_DIM)
        ocmp, owin, ids = attn_dec_dense(heads(q), heads(qr), ck, cv, kw, vw, nc=nc, ns=ns, pos=past_len)
        gt = gates.reshape(Bs, N_KV_HEADS, LANE)[:, :, :R * N_BRANCH].reshape(Bs, N_KV_HEADS, R, N_BRANCH)
        gt = jnp.pad(gt, ((0, 0), (0, 0), (0, 0), (0, LANE - N_BRANCH)))
        o = attn_dec_sel(page_table, ids, heads(qr), cache_k_sel, cache_v_sel,
                         kv[2].reshape(Bs, 1, KV_W), kv[3].reshape(Bs, 1, KV_W), ocmp, owin, gt, pos=past_len)
        return o.reshape(Bs, D).astype(BF16), (kw, vw)

    pool_zero = jnp.zeros((state_pool.shape[0], Bp, POOL_STATE, D), x_prompt.dtype)
    y_p, pool_p, rows_p, win_p = _trunk(x_prompt, p_prompt, pool_zero, 0, W, P, attend_prompt)
    y_s, pool_s, rows_s, win_s = _trunk(x_sample, p_sample, state_pool, past_len, W, P, attend_sample)
    return (y_p, y_s, pool_p, pool_s, rows_p[0], rows_p[1], rows_p[2], rows_p[3], win_p[0], win_p[1],
            rows_s[0], rows_s[1], rows_s[2], rows_s[3], win_s[0], win_s[1])
```

```python
import functools

import jax
import jax.numpy as jnp
import numpy as np
from jax import lax
from jax.experimental import pallas as pl
from jax.experimental.pallas import tpu as pltpu

F32 = jnp.float32
BF16 = jnp.bfloat16

POOL_WINDOWS = (2, 4, 8, 16)
POOL_STATE = max(POOL_WINDOWS) - 1
POOL_PAD = POOL_STATE + 1
HEAD_DIM = 128
N_KV_HEADS = 4
N_BRANCH = 3
CMP_BLOCK = 32
CMP_STRIDE = 16
CMP_HIDDEN = 2 * HEAD_DIM
SEL_BLOCK = 64
N_SEL = 16
WINDOW = 512
ROPE_THETA = 10000.0
EPS = 1e-6
SCALE = HEAD_DIM ** -0.5
NEG = -1e30
FORCE = 1e9
PAD_SCORE = -3e38
KV_W = N_KV_HEADS * HEAD_DIM
LANE = 128
VMEM_LIMIT = 56 * 1024 * 1024


def _params(sem):
    return pltpu.CompilerParams(dimension_semantics=sem, vmem_limit_bytes=VMEM_LIMIT)


def _sigmoid(x):
    return 1.0 / (1.0 + jnp.exp(-x))


def _dot(a, b):
    return jnp.dot(a, b, preferred_element_type=F32)


def _dot_nt(a, b):
    return lax.dot_general(a, b, (((1,), (1,)), ((), ())), preferred_element_type=F32)


def _dot_tn(a, b):
    return lax.dot_general(a, b, (((0,), (0,)), ((), ())), preferred_element_type=F32)


def _head_norm(x, g):
    return x * lax.rsqrt(jnp.mean(x * x, axis=-1, keepdims=True) + EPS) * g


def _rope(x, cosf, sinf):
    return x * cosf + pltpu.roll(x, HEAD_DIM // 2, 1) * sinf


def _fold_lanes(x):
    parts = [x[:, c * LANE:(c + 1) * LANE] for c in range(x.shape[1] // LANE)]
    return functools.reduce(lambda u, v: u + v, parts)


def _norm_producer(epilogue, n_base):
    def wrapped(acc, rows, *extras):
        (h,) = epilogue(acc, rows, *extras[:n_base])
        return (h, *[(h * g[...]).astype(BF16) for g in extras[n_base:]], _fold_lanes(h * h))
    return wrapped


def _row_scaled(epilogue):
    def wrapped(acc, rows, rs_ref, *extras):
        rs = rs_ref[rows, :]
        acc = jnp.concatenate([acc[:, c * LANE:(c + 1) * LANE] * rs for c in range(acc.shape[1] // LANE)], axis=1)
        return epilogue(acc, rows, *extras)
    return wrapped


def _row_scale_kernel(ssq_ref, o_ref, *, d):
    tot = jnp.sum(functools.reduce(lambda u, v: u + v, [ssq_ref[j] for j in range(ssq_ref.shape[0])]),
                  axis=1, keepdims=True)
    o_ref[...] = jnp.broadcast_to(lax.rsqrt(tot / d + EPS), o_ref.shape)


def row_scale(ssq, d):
    gn, M, _ = ssq.shape
    tm = min(M, 1024)
    return pl.pallas_call(
        functools.partial(_row_scale_kernel, d=d),
        grid=(M // tm,),
        in_specs=[pl.BlockSpec((gn, tm, LANE), lambda i: (0, i, 0))],
        out_specs=pl.BlockSpec((tm, LANE), lambda i: (i, 0)),
        out_shape=jax.ShapeDtypeStruct((M, LANE), F32),
        compiler_params=_params(("parallel",)),
        name="row_scale",
    )(ssq)


def _norm_io(gains, M, N, gn, tm, tn):
    extras = [g.reshape(1, N).astype(F32) for g in gains]
    extra_specs = [pl.BlockSpec((1, tn), lambda i, j, k: (0, j)) for _ in gains]
    shapes = [jax.ShapeDtypeStruct((M, N), BF16) for _ in gains] + [jax.ShapeDtypeStruct((gn, M, LANE), F32)]
    specs = [_mn_spec(tm, tn) for _ in gains] + [pl.BlockSpec((1, tm, LANE), lambda i, j, k: (j, i, 0))]
    return extras, extra_specs, shapes, specs


EPILOGUE_ROWS = 256


def _finish_rows(acc_of, tm, outs, extras, epilogue):
    ch = min(tm, EPILOGUE_ROWS)
    for c in range(tm // ch):
        rows = slice(c * ch, (c + 1) * ch)
        for o_ref, r in zip(outs, epilogue(acc_of(rows), rows, *extras)):
            if len(o_ref.shape) == 4:
                for hh in range(o_ref.shape[2]):
                    o_ref[0, rows, hh, :] = r[:, hh * HEAD_DIM:(hh + 1) * HEAD_DIM].astype(o_ref.dtype)
            elif len(o_ref.shape) == 3:
                o_ref[0, rows, :] = r.astype(o_ref.dtype)
            else:
                o_ref[rows, :] = r.astype(o_ref.dtype)


def _k_steps(x_ref, w, acc_ref, nk, outs, extras, epilogue):
    k = pl.program_id(2)

    @pl.when(k == 0)
    def _():
        acc_ref[...] = _dot(x_ref[...], w())

    @pl.when((k > 0) & (k < nk - 1))
    def _():
        acc_ref[...] += _dot(x_ref[...], w())

    @pl.when(k == nk - 1)
    def _():
        _finish_rows(lambda rows: acc_ref[rows, :] + _dot(x_ref[rows, :], w()), x_ref.shape[0], outs, extras,
                     epilogue)


def _side_io(side, n_steps, step):
    s_arr, s_layer = side
    _, rows, cols = s_arr.shape
    rs = rows // n_steps
    assert rs * n_steps == rows and rs % 16 == 0
    in_spec = pl.BlockSpec((None, rs, cols), lambda *g: (s_layer, step(*g), 0))
    out_spec = pl.BlockSpec((rs, cols), lambda *g: (step(*g), 0))
    return s_arr, in_spec, jax.ShapeDtypeStruct((rows, cols), BF16), out_spec


def _mm_kernel(*refs, nk, n_extra, n_out, epilogue, side):
    x_ref, w_ref = refs[0], refs[1]
    extras = refs[2:2 + n_extra]
    n_in = 2 + n_extra + side
    outs = refs[n_in:n_in + n_out]
    tm = x_ref.shape[0]
    if side:
        refs[n_in + n_out][...] = refs[n_in - 1][...].astype(BF16)

    if nk == 1:
        _finish_rows(lambda rows: _dot(x_ref[rows, :], w_ref[...]), tm, outs, extras, epilogue)
    else:
        _k_steps(x_ref, lambda: w_ref[...], refs[-1], nk, outs, extras, epilogue)


def mm(x, w, *, grid, tm, tn, tk, epilogue, extras=(), extra_specs=(), out_shapes, out_specs,
       x_map=None, w_map=None, side=None, name):
    gm, gn, nk = grid
    x_map = x_map or (lambda i, j, k: (i, k))
    w_map = w_map or (lambda i, j, k: (k, j))
    ins, in_specs = [x, w, *extras], [pl.BlockSpec((tm, tk), x_map), pl.BlockSpec((tk, tn), w_map)]
    in_specs += list(extra_specs)
    n_out, out_shapes, out_specs = len(out_shapes), list(out_shapes), list(out_specs)
    if side is not None:
        s_arr, s_in, s_shape, s_out = _side_io(side, gm * gn * nk, lambda i, j, k: (i * gn + j) * nk + k)
        ins.append(s_arr), in_specs.append(s_in), out_shapes.append(s_shape), out_specs.append(s_out)
    kern = functools.partial(_mm_kernel, nk=nk, n_extra=len(extras), n_out=n_out, epilogue=epilogue,
                             side=side is not None)
    return pl.pallas_call(
        kern,
        grid=grid,
        in_specs=in_specs,
        out_specs=out_specs,
        out_shape=out_shapes,
        scratch_shapes=[pltpu.VMEM((tm, tn), F32)] if nk > 1 else [],
        compiler_params=_params(("parallel", "parallel", "arbitrary")),
        name=name,
    )(*ins)


def _mm_ws_kernel(*refs, nk, n_extra, n_out, epilogue, emit, side):
    x_ref, w_ref = refs[0], refs[1]
    extras = refs[2:2 + n_extra]
    n_in = 2 + n_extra + side
    outs = refs[n_in:n_in + n_out]
    rest = refs[n_in + n_out:]
    wb_ref = rest[emit + side]
    i, k = pl.program_id(1), pl.program_id(2)

    @pl.when(i == 0)
    def _():
        wb_ref[k] = w_ref[...].astype(BF16)
        if emit:
            rest[0][...] = wb_ref[k]

    if side:
        rest[emit][...] = refs[n_in - 1][...].astype(BF16)

    tm = x_ref.shape[0]
    if nk == 1:
        _finish_rows(lambda rows: _dot(x_ref[rows, :], wb_ref[0]), tm, outs, extras, epilogue)
    else:
        _k_steps(x_ref, lambda: wb_ref[k], rest[-1], nk, outs, extras, epilogue)


def mm_ws(x, w, *, layer=None, emit=False, side=None, grid, tm, tn, tk, epilogue, extras=(), extra_specs=(),
          out_shapes, out_specs, name):
    gm, gn, nk = grid
    swap = lambda f: (lambda j, i, k: f(i, j, k))
    respec = lambda s: pl.BlockSpec(s.block_shape, swap(s.index_map))
    k_once = lambda i, k: jnp.where(i == 0, k, nk - 1)
    if layer is None:
        w_spec = pl.BlockSpec((tk, tn), lambda j, i, k: (k_once(i, k), j))
    else:
        w_spec = pl.BlockSpec((None, tk, tn), lambda j, i, k: (layer, k_once(i, k), j))
    n_out = len(out_shapes)
    ins, in_specs = [x, w, *extras], [pl.BlockSpec((tm, tk), lambda j, i, k: (i, k)), w_spec]
    in_specs += [respec(s) for s in extra_specs]
    out_shapes, out_specs = list(out_shapes), [respec(s) for s in out_specs]
    if emit:
        out_shapes.append(jax.ShapeDtypeStruct((nk * tk, gn * tn), BF16))
        out_specs.append(pl.BlockSpec((tk, tn), lambda j, i, k: (k_once(i, k), j)))
    if side is not None:
        s_arr, s_in, s_shape, s_out = _side_io(side, gm * gn * nk, lambda j, i, k: (j * gm + i) * nk + k)
        ins.append(s_arr), in_specs.append(s_in), out_shapes.append(s_shape), out_specs.append(s_out)
    kern = functools.partial(_mm_ws_kernel, nk=nk, n_extra=len(extras), n_out=n_out, epilogue=epilogue, emit=emit,
                             side=side is not None)
    return pl.pallas_call(
        kern,
        grid=(gn, gm, nk),
        in_specs=in_specs,
        out_specs=out_specs,
        out_shape=out_shapes,
        scratch_shapes=[pltpu.VMEM((nk, tk, tn), BF16)] + ([pltpu.VMEM((tm, tn), F32)] if nk > 1 else []),
        compiler_params=_params(("parallel", "arbitrary", "arbitrary")),
        name=name,
    )(*ins)


def _tiles(M, N, K, tm, tn, tk):
    tm, tn, tk = min(tm, M), min(tn, N), min(tk, K)
    return (M // tm, N // tn, K // tk), tm, tn, tk


def _mn_spec(tm, tn):
    return pl.BlockSpec((tm, tn), lambda i, j, k: (i, j))


def _pool_diff_kernel(*refs, tt, pos0, halo):
    x_ref, pre_ref, g_ref = refs[0], refs[1 + halo], refs[2 + halo]
    d_ref, st_ref, seq_ref = refs[3 + halo:]
    t = pl.program_id(1)
    norm = lambda x: x * lax.rsqrt(jnp.mean(x * x, axis=-1, keepdims=True) + EPS) * g_ref[...]
    a = norm(x_ref[0])
    seq_ref[POOL_PAD:POOL_PAD + tt, :] = a

    @pl.when(t == 0)
    def _():
        seq_ref[0:POOL_PAD, :] = pre_ref[0]

    if halo:
        @pl.when(t > 0)
        def _():
            seq_ref[0:POOL_PAD, :] = norm(refs[1][0])

    pos = pos0 + t * tt + lax.broadcasted_iota(jnp.int32, (tt, 1), 0)
    pg = a.shape[1] // len(POOL_WINDOWS)
    for g, w in enumerate(POOL_WINDOWS):
        cols = slice(g * pg, (g + 1) * pg)
        s = a[:, cols]
        for j in range(1, w):
            s = s + seq_ref[POOL_PAD - j:POOL_PAD - j + tt, cols]
        cnt = jnp.minimum(pos + 1, w).astype(F32)
        d_ref[0, :, cols] = (s / cnt - a[:, cols]).astype(d_ref.dtype)
    st_ref[0] = seq_ref[tt:tt + POOL_PAD, :]


def pool_diff(x, gain, prefix, pos0):
    B, T, D = x.shape
    tt = min(T, 256)
    halo = T > tt
    pre = jnp.concatenate([jnp.zeros((B, 1, D), F32), prefix], axis=1)
    hpt = tt // POOL_PAD
    in_specs = [pl.BlockSpec((1, tt, D), lambda b, t: (b, t, 0))]
    if halo:
        in_specs.append(pl.BlockSpec((1, POOL_PAD, D), lambda b, t: (b, jnp.maximum(t * hpt - 1, 0), 0)))
    in_specs += [pl.BlockSpec((1, POOL_PAD, D), lambda b, t: (b, 0, 0)), pl.BlockSpec((1, D), lambda b, t: (0, 0))]
    d, st = pl.pallas_call(
        functools.partial(_pool_diff_kernel, tt=tt, pos0=pos0, halo=halo),
        grid=(B, T // tt),
        in_specs=in_specs,
        out_specs=[pl.BlockSpec((1, tt, D), lambda b, t: (b, t, 0)),
                   pl.BlockSpec((1, POOL_PAD, D), lambda b, t: (b, 0, 0))],
        out_shape=[jax.ShapeDtypeStruct((B, T, D), BF16), jax.ShapeDtypeStruct((B, POOL_PAD, D), F32)],
        scratch_shapes=[pltpu.VMEM((POOL_PAD + tt, D), F32)],
        compiler_params=_params(("parallel", "arbitrary")),
        name="pool_diff",
    )(*([x, x] if halo else [x]), pre, gain.reshape(1, D).astype(F32))
    return d, st[:, 1:]


PAGES_PER_STEP = 8
CMP_FINISH_ROWS = 512


def _cmp_partial_kernel(pt_ref, *refs, steps, pps):
    del pt_ref
    k_pages, v_pages = refs[:pps], refs[pps:2 * pps]
    wk_ref, wv_ref, abk_ref, abv_ref, xk_ref, xv_ref = refs[2 * pps:]
    p = pl.program_id(2)
    G = N_KV_HEADS
    half = CMP_STRIDE // 2
    rows_per_page = (k_pages[0].shape[1] // half) * G
    for pages, x_ref in ((k_pages, xk_ref), (v_pages, xv_ref)):
        for q, page_ref in enumerate(pages):
            for sb in range(page_ref.shape[1] // half):
                row = q * rows_per_page + sb * G
                for r in range(CMP_STRIDE):
                    x_ref[p, row:row + G, r * HEAD_DIM:(r + 1) * HEAD_DIM] = (
                        page_ref[0, sb * half + r // 2, (r % 2) * G:(r % 2 + 1) * G, :])

    @pl.when(p == steps - 1)
    def _():
        for x_ref, w_ref, ab_ref in ((xk_ref, wk_ref, abk_ref), (xv_ref, wv_ref, abv_ref)):
            x = x_ref[...].reshape(ab_ref.shape[1], x_ref.shape[2])
            ab_ref[0] = _dot(x.astype(BF16), w_ref[...])


def cmp_partials(k_pages, v_pages, table, wk_cat, wv_cat, cp):
    B, ppb = table.shape
    page = k_pages.shape[1]
    pps = PAGES_PER_STEP
    rows_per_page = page // CMP_STRIDE * N_KV_HEADS
    nch, steps = ppb // cp, cp // pps
    m = cp * rows_per_page
    kdim = CMP_STRIDE * HEAD_DIM
    pair_rows = lambda a: a.reshape(a.shape[0], page // 2, 2 * N_KV_HEADS, HEAD_DIM)

    def page_spec(q):
        return pl.BlockSpec((1, page // 2, 2 * N_KV_HEADS, HEAD_DIM),
                            lambda b, c, p, pt: (pt[b, c * cp + p * pps + q], 0, 0, 0))

    page_specs = [page_spec(q) for q in range(pps)]
    w_spec = pl.BlockSpec((kdim, 2 * CMP_HIDDEN), lambda b, c, p, pt: (0, 0))
    out_spec = pl.BlockSpec((1, m, 2 * CMP_HIDDEN), lambda b, c, p, pt: (b, c, 0))
    out_shape = jax.ShapeDtypeStruct((B, ppb * rows_per_page, 2 * CMP_HIDDEN), F32)
    x_scratch = pltpu.VMEM((steps, pps * rows_per_page, kdim), F32)
    return pl.pallas_call(
        functools.partial(_cmp_partial_kernel, steps=steps, pps=pps),
        grid_spec=pltpu.PrefetchScalarGridSpec(
            num_scalar_prefetch=1,
            grid=(B, nch, steps),
            in_specs=page_specs + page_specs + [w_spec, w_spec],
            out_specs=[out_spec, out_spec],
            scratch_shapes=[x_scratch, x_scratch],
        ),
        out_shape=[out_shape, out_shape],
        compiler_params=_params(("parallel", "parallel", "arbitrary")),
        name="cmp_partials",
    )(table, *([pair_rows(k_pages)] * pps), *([pair_rows(v_pages)] * pps), wk_cat, wv_cat)


def _cmp_finish_kernel(ab_ref, pe_ref, w1_ref, w2_ref, g_ref, o_ref, out_ref, *, norm):
    n4 = ab_ref.shape[1]
    G = o_ref.shape[1]
    bias = _dot(jnp.broadcast_to(pe_ref[...], (8, pe_ref.shape[1])), w1_ref[...])[0:1, :]
    ch = min(n4, CMP_FINISH_ROWS)
    for c in range(n4 // ch):
        lo, hi = c * ch, (c + 1) * ch
        first = ab_ref[0, lo:hi, :CMP_HIDDEN]
        if hi + G <= n4:
            second = ab_ref[0, lo + G:hi + G, CMP_HIDDEN:]
        else:
            second = jnp.concatenate([ab_ref[0, lo + G:hi, CMP_HIDDEN:], ab_ref[0, hi - G:hi, CMP_HIDDEN:]], axis=0)
        pre = first + second + bias
        out = _dot((pre * _sigmoid(pre)).astype(BF16), w2_ref[...])
        if norm:
            out = _head_norm(out, g_ref[...])
        out_ref[lo:hi, :] = out
    for g in range(G):
        o_ref[0, g] = out_ref[pl.ds(g, n4 // G, stride=G), :].astype(o_ref.dtype)


def cmp_finish(ab, pe, w1, w2, gain, *, norm):
    B, n4, _ = ab.shape
    G = N_KV_HEADS
    full = lambda a: pl.BlockSpec(a.shape, lambda b: (0,) * a.ndim)
    args = (pe, w1, w2, gain)
    return pl.pallas_call(
        functools.partial(_cmp_finish_kernel, norm=norm),
        grid=(B,),
        in_specs=[pl.BlockSpec((1, n4, 2 * CMP_HIDDEN), lambda b: (b, 0, 0))] + [full(a) for a in args],
        out_specs=pl.BlockSpec((1, G, n4 // G, HEAD_DIM), lambda b: (b, 0, 0, 0)),
        out_shape=jax.ShapeDtypeStruct((B, G, n4 // G, HEAD_DIM), BF16),
        scratch_shapes=[pltpu.VMEM((n4, HEAD_DIM), F32)],
        compiler_params=_params(("parallel",)),
        name="cmp_finish",
    )(ab, *args)


def _select_blocks(score, blk, ns):
    rank = jnp.zeros(score.shape, jnp.int32)
    for j in range(ns):
        sj = score[j:j + 1, :]
        beats = (sj > score) | ((sj == score) & (j < blk))
        rank = rank + beats.astype(jnp.int32)
    return (rank < min(N_SEL, ns)) & (score > 0.5 * NEG)


def _attn_prompt_kernel(q_ref, qr_ref, ck_ref, cv_ref, ks_ref, vs_ref, kw_ref, vw_ref, gate_ref, mapT_ref,
                        expand_ref, o_ref, part_ref, sbias_ref, wbias_ref, m_ref, acc_ref, *, tq, kc, nc, ns, R):
    qi = pl.program_id(2)
    q0 = qi * tq
    pos = q0 + lax.broadcasted_iota(jnp.int32, (tq, 1), 0)
    ncp = ck_ref.shape[2]

    ck = ck_ref[0, 0]
    cv = cv_ref[0, 0]
    cidx = lax.broadcasted_iota(jnp.int32, (1, ncp), 1)
    ok_c = (cidx * CMP_STRIDE + CMP_BLOCK - 1 <= pos) & (cidx < nc)
    imp = jnp.zeros((tq, ncp), F32)
    for r in range(R):
        qh = q_ref[0, :, r * HEAD_DIM:(r + 1) * HEAD_DIM]
        s = jnp.where(ok_c, _dot_nt(qh, ck) * SCALE, NEG)
        e = jnp.exp(s - jnp.max(s, axis=1, keepdims=True))
        p = jnp.where(ok_c, e * (1.0 / jnp.sum(e, axis=1, keepdims=True)), 0.0)
        imp = imp + p
        part_ref[r] = gate_ref[0, :, r * N_BRANCH:r * N_BRANCH + 1] * _dot(p.astype(BF16), cv)

    nsp = mapT_ref.shape[0]
    p_slc = lax.dot_general(mapT_ref[...], imp, (((1,), (1,)), ((), ())), precision=lax.Precision.HIGHEST,
                            preferred_element_type=F32)
    blk = lax.broadcasted_iota(jnp.int32, (nsp, tq), 0)
    pos_l = q0 + lax.broadcasted_iota(jnp.int32, (nsp, tq), 1)
    cur = pos_l // SEL_BLOCK
    vis = blk * SEL_BLOCK <= pos_l
    forced = vis & ((blk == 0) | (blk == cur) | (blk == cur - 1))
    score = jnp.where(forced, FORCE, jnp.where(vis, p_slc, NEG))
    score = jnp.where(blk < ns, score, PAD_SCORE)
    sel = _select_blocks(score, blk, ns).astype(BF16)

    c_hi = (q0 + tq) // kc
    col = lax.broadcasted_iota(jnp.int32, (1, kc), 1)
    sel_keys = _dot_tn(sel, expand_ref[...])
    for c in range(sbias_ref.shape[0]):
        @pl.when(c < c_hi)
        def _(c=c):
            ok = (sel_keys[:, c * kc:(c + 1) * kc] > 0.5) & (c * kc + col <= pos)
            sbias_ref[c] = jnp.where(ok, 0.0, NEG)
    nwc = wbias_ref.shape[0]
    c_w0 = c_hi - nwc
    for d in range(nwc):
        kpos = (c_w0 + d) * kc + col
        wbias_ref[d] = jnp.where((kpos <= pos) & (kpos > pos - WINDOW), 0.0, NEG)

    def fold(t):
        return [t[:, i * LANE:(i + 1) * LANE] for i in range(kc // LANE)]

    ones_col = (lax.broadcasted_iota(jnp.int32, (kc, LANE), 1) == 0).astype(BF16)

    def branch(k_ref, v_ref, c_lo, bias_of, gate_col):
        def logits(r, c, bias):
            k = k_ref[0, pl.ds(pl.multiple_of(c * kc, kc), kc), :]
            return _dot_nt(qr_ref[0, :, r * HEAD_DIM:(r + 1) * HEAD_DIM], k) * SCALE + bias

        m_ref[...] = jnp.full(m_ref.shape, NEG, F32)
        acc_ref[...] = jnp.zeros(acc_ref.shape, F32)

        def max_body(c, carry):
            bias = bias_of(c)
            for r in range(R):
                mx = m_ref[r]
                for part in fold(logits(r, c, bias)):
                    mx = jnp.maximum(mx, part)
                m_ref[r] = mx
            return carry

        lax.fori_loop(c_lo, c_hi, max_body, 0)
        for r in range(R):
            m_ref[r] = jnp.broadcast_to(jnp.max(m_ref[r], axis=1, keepdims=True), (tq, LANE))

        def sum_body(c, carry):
            bias = bias_of(c)
            v = jnp.concatenate([v_ref[0, pl.ds(pl.multiple_of(c * kc, kc), kc), :], ones_col], axis=1)
            for r in range(R):
                t = logits(r, c, bias)
                m = m_ref[r]
                ps = [jnp.exp(part - m) for part in fold(t)]
                acc_ref[r] += _dot(jnp.concatenate(ps, axis=1).astype(BF16), v)
            return carry

        lax.fori_loop(c_lo, c_hi, sum_body, 0)
        for r in range(R):
            gate = gate_ref[0, :, r * N_BRANCH + gate_col:r * N_BRANCH + gate_col + 1]
            part_ref[r] += (gate * (1.0 / acc_ref[r, :, HEAD_DIM:HEAD_DIM + 1])) * acc_ref[r, :, :HEAD_DIM]

    branch(ks_ref, vs_ref, 0, lambda c: sbias_ref[c], 1)
    branch(kw_ref, vw_ref, jnp.maximum(c_w0, 0), lambda c: wbias_ref[c - c_w0], 2)
    for r in range(R):
        o_ref[0, :, r * HEAD_DIM:(r + 1) * HEAD_DIM] = part_ref[r].astype(o_ref.dtype)


def _overlap_map(ncp, nsp, ns):
    ratio = CMP_BLOCK // CMP_STRIDE
    per_sel = SEL_BLOCK // CMP_STRIDE
    m = np.zeros((ncp, nsp), np.float32)
    for b in range(ns):
        for mm_ in range(per_sel):
            for n in range(ratio):
                j = per_sel * b + mm_ - n
                if 0 <= j < ncp:
                    m[j, b] += 1.0
    return m


def attn_prompt(q, qr, ck, cv, kvb, gates, *, nc):
    B, T, HD = q.shape
    G = N_KV_HEADS
    R = HD // HEAD_DIM // G
    tq = min(T, 256)
    kc = tq
    ns = T // SEL_BLOCK
    nsp = -(-ns // 8) * 8
    ncp = ck.shape[2]
    mapT = jnp.asarray(_overlap_map(ncp, nsp, ns).T)
    expand = jnp.asarray((np.arange(T)[None, :] // SEL_BLOCK == np.arange(nsp)[:, None]).astype(np.float32), BF16)
    q_spec = pl.BlockSpec((1, tq, R * HEAD_DIM), lambda b, g, i: (b, i, g))
    c_spec = pl.BlockSpec((1, 1, ncp, HEAD_DIM), lambda b, g, i: (b, g, 0, 0))
    kv_spec = lambda n: pl.BlockSpec((None, 1, T, HEAD_DIM), lambda b, g, i: (n, b, 0, g))
    return pl.pallas_call(
        functools.partial(_attn_prompt_kernel, tq=tq, kc=kc, nc=nc, ns=ns, R=R),
        grid=(B, G, T // tq),
        in_specs=[q_spec, q_spec, c_spec, c_spec, kv_spec(2), kv_spec(3), kv_spec(4), kv_spec(5),
                  pl.BlockSpec((1, tq, LANE), lambda b, g, i: (b, i, g)),
                  pl.BlockSpec((nsp, ncp), lambda b, g, i: (0, 0)),
                  pl.BlockSpec((nsp, T), lambda b, g, i: (0, 0))],
        out_specs=q_spec,
        out_shape=jax.ShapeDtypeStruct((B, T, HD), BF16),
        scratch_shapes=[pltpu.VMEM((R, tq, HEAD_DIM), F32), pltpu.VMEM((T // kc, tq, kc), F32),
                        pltpu.VMEM((min(WINDOW, T) // kc + tq // kc, tq, kc), F32)]
                       + [pltpu.VMEM((R, tq, HEAD_DIM), F32), pltpu.VMEM((R, tq, HEAD_DIM + LANE), F32)],
        compiler_params=_params(("parallel", "parallel", "arbitrary")),
        name="attn_prompt",
    )(q, qr, ck, cv, kvb, kvb, kvb, kvb, gates, mapT, expand)


def _attn_dec_dense_kernel(q_ref, qr_ref, ck_ref, cv_ref, kw_ref, vw_ref, map_ref, ocmp_ref, owin_ref, ids_ref,
                           *, nc, ns, pos):
    G, R = q_ref.shape[1], q_ref.shape[2]
    ncp = ck_ref.shape[2]
    nsl = map_ref.shape[1]
    cidx = lax.broadcasted_iota(jnp.int32, (1, ncp), 1)
    ok_c = (cidx * CMP_STRIDE + CMP_BLOCK - 1 <= pos) & (cidx < nc)
    blk_l = lax.broadcasted_iota(jnp.int32, (1, nsl), 1)
    cur = pos // SEL_BLOCK
    vis = blk_l * SEL_BLOCK <= pos
    forced = vis & ((blk_l == 0) | (blk_l == cur) | (blk_l == cur - 1))
    ii = lax.broadcasted_iota(jnp.int32, (nsl, nsl), 0)
    jj = lax.broadcasted_iota(jnp.int32, (nsl, nsl), 1)
    slot = lax.broadcasted_iota(jnp.int32, (nsl, LANE), 1).astype(F32)
    blk_s = lax.broadcasted_iota(jnp.int32, (nsl, LANE), 0).astype(F32)
    for g in range(G):
        s = jnp.where(ok_c, _dot_nt(q_ref[0, g], ck_ref[0, g]) * SCALE, NEG)
        e = jnp.exp(s - jnp.max(s, axis=1, keepdims=True))
        p = jnp.where(ok_c, e * (1.0 / jnp.sum(e, axis=1, keepdims=True)), 0.0)
        ocmp_ref[0, g] = _dot(p.astype(BF16), cv_ref[0, g])
        imp = jnp.broadcast_to(jnp.sum(p, axis=0, keepdims=True), (R, ncp))
        p_slc = jnp.dot(imp, map_ref[...], precision=lax.Precision.HIGHEST, preferred_element_type=F32)[0:1, :]
        score_l = jnp.where(forced, FORCE, jnp.where(vis, p_slc, NEG))
        score_l = jnp.where(blk_l < ns, score_l, PAD_SCORE)
        score_s = jnp.sum(jnp.where(ii == jj, score_l, 0.0), axis=1, keepdims=True)
        beats = (score_l > score_s) | ((score_l == score_s) & (jj < ii))
        rank = jnp.sum(beats.astype(F32), axis=1, keepdims=True)
        ids = jnp.sum(jnp.where(rank == slot, blk_s, 0.0), axis=0, keepdims=True)
        ids_ref[0, g] = ids[:, :N_SEL].astype(jnp.int32)
        kw = kw_ref[0, :, g, :].astype(BF16)
        vw = vw_ref[0, :, g, :].astype(BF16)
        s = _dot_nt(qr_ref[0, g], kw) * SCALE
        e = jnp.exp(s - jnp.max(s, axis=1, keepdims=True))
        p = e * (1.0 / jnp.sum(e, axis=1, keepdims=True))
        owin_ref[0, g] = _dot(p.astype(BF16), vw)


def attn_dec_dense(q, qr, ck, cv, kw, vw, *, nc, ns, pos):
    B, G, R, _ = q.shape
    ncp = ck.shape[2]
    wb = kw.shape[1]
    nsl = -(-ns // LANE) * LANE
    omap = jnp.asarray(_overlap_map(ncp, nsl, ns))
    q_spec = pl.BlockSpec((1, G, R, HEAD_DIM), lambda b: (b, 0, 0, 0))
    c_spec = pl.BlockSpec((1, G, ncp, HEAD_DIM), lambda b: (b, 0, 0, 0))
    w_spec = pl.BlockSpec((1, wb, G, HEAD_DIM), lambda b: (b, 0, 0, 0))
    return pl.pallas_call(
        functools.partial(_attn_dec_dense_kernel, nc=nc, ns=ns, pos=pos),
        grid=(B,),
        in_specs=[q_spec, q_spec, c_spec, c_spec, w_spec, w_spec, pl.BlockSpec((ncp, nsl), lambda b: (0, 0))],
        out_specs=[q_spec, q_spec, pl.BlockSpec((1, G, 1, N_SEL), lambda b: (b, 0, 0, 0))],
        out_shape=[jax.ShapeDtypeStruct((B, G, R, HEAD_DIM), F32), jax.ShapeDtypeStruct((B, G, R, HEAD_DIM), F32),
                   jax.ShapeDtypeStruct((B, G, 1, N_SEL), jnp.int32)],
        compiler_params=_params(("parallel",)),
        name="attn_dec_dense",
    )(q, qr, ck, cv, kw, vw, omap)


def _attn_dec_sel_kernel(pt_ref, ids_ref, qr_ref, *refs, n_past, pos):
    del pt_ref
    G = qr_ref.shape[1]
    kc_refs, vc_refs = refs[:G], refs[G:2 * G]
    kn_ref, vn_ref, ocmp_ref, owin_ref, gate_ref, o_ref, m_ref, l_ref, acc_ref = refs[2 * G:]
    b, n = pl.program_id(0), pl.program_id(1)

    @pl.when(n == 0)
    def _():
        m_ref[...] = jnp.full(m_ref.shape, NEG, F32)
        l_ref[...] = jnp.zeros(l_ref.shape, F32)
        acc_ref[...] = jnp.zeros(acc_ref.shape, F32)

    row = lax.broadcasted_iota(jnp.int32, (SEL_BLOCK, 1), 0)
    lane = lax.broadcasted_iota(jnp.int32, (1, SEL_BLOCK), 1)
    key_row = jnp.where(lane < SEL_BLOCK // 2, 2 * lane, 2 * lane - (SEL_BLOCK - 1))
    head_rows = lambda ref, g: jnp.concatenate([ref[0, :, g, :], ref[0, :, G + g, :]], axis=0)
    for g in range(G):
        bid = ids_ref[(b * G + g) * N_SEL + n]
        is_new = bid >= n_past
        first = (row == 0) & (bid == n_past)
        sl = slice(g * HEAD_DIM, (g + 1) * HEAD_DIM)
        k = jnp.where(is_new, jnp.where(first, kn_ref[0, :, sl], 0.0), head_rows(kc_refs[g], g)).astype(BF16)
        v = jnp.where(is_new, jnp.where(first, vn_ref[0, :, sl], 0.0), head_rows(vc_refs[g], g)).astype(BF16)
        kpos = bid * SEL_BLOCK + key_row
        ok = kpos <= pos
        s = jnp.where(ok, _dot_nt(qr_ref[0, g], k) * SCALE, NEG)
        m_prev = m_ref[g]
        m_new = jnp.maximum(m_prev, jnp.max(s, axis=1, keepdims=True))
        alpha = jnp.exp(m_prev - m_new)
        p = jnp.where(ok, jnp.exp(s - m_new), 0.0)
        l_ref[g] = alpha * l_ref[g] + jnp.sum(p, axis=1, keepdims=True)
        acc_ref[g] = alpha * acc_ref[g] + _dot(p.astype(BF16), v)
        m_ref[g] = m_new

    @pl.when(n == N_SEL - 1)
    def _():
        for g in range(G):
            gt = gate_ref[0, g]
            o_sel = acc_ref[g] * (1.0 / l_ref[g])
            o_ref[0, g] = gt[:, 0:1] * ocmp_ref[0, g] + gt[:, 1:2] * o_sel + gt[:, 2:3] * owin_ref[0, g]


def attn_dec_sel(table, ids, qr, k_cache, v_cache, k_new, v_new, ocmp, owin, gates, *, pos):
    B, G, R, _ = qr.shape
    page = k_cache.shape[1]
    bpp = page // SEL_BLOCK
    n_past = table.shape[1] * bpp
    kc = k_cache.reshape(k_cache.shape[0] * bpp, SEL_BLOCK // 2, 2 * G, HEAD_DIM)
    vc = v_cache.reshape(v_cache.shape[0] * bpp, SEL_BLOCK // 2, 2 * G, HEAD_DIM)

    def cache_spec(g):
        def index(b, n, pt, ids_):
            bid = jnp.minimum(ids_[(b * G + g) * N_SEL + n], n_past - 1)
            return (pt[b, bid // bpp] * bpp + bid % bpp, 0, 0, 0)
        return pl.BlockSpec((1, SEL_BLOCK // 2, 2 * G, HEAD_DIM), index)

    q_spec = pl.BlockSpec((1, G, R, HEAD_DIM), lambda b, n, pt, ids_: (b, 0, 0, 0))
    n_spec = pl.BlockSpec((1, 1, G * HEAD_DIM), lambda b, n, pt, ids_: (b, 0, 0))
    g_spec = pl.BlockSpec((1, G, R, LANE), lambda b, n, pt, ids_: (b, 0, 0, 0))
    c_specs = [cache_spec(g) for g in range(G)]
    return pl.pallas_call(
        functools.partial(_attn_dec_sel_kernel, n_past=n_past, pos=pos),
        grid_spec=pltpu.PrefetchScalarGridSpec(
            num_scalar_prefetch=2,
            grid=(B, N_SEL),
            in_specs=[q_spec] + c_specs + c_specs + [n_spec, n_spec, q_spec, q_spec, g_spec],
            out_specs=q_spec,
            scratch_shapes=[pltpu.VMEM((G, R, 1), F32), pltpu.VMEM((G, R, 1), F32),
                            pltpu.VMEM((G, R, HEAD_DIM), F32)],
        ),
        out_shape=jax.ShapeDtypeStruct((B, G, R, HEAD_DIM), F32),
        compiler_params=_params(("parallel", "arbitrary")),
        name="attn_dec_sel",
    )(table, ids.reshape(-1), qr, *([kc] * G), *([vc] * G), k_new, v_new, ocmp, owin, gates)


def _rope_tables(pos):
    half = HEAD_DIM // 2
    inv = ROPE_THETA ** (-jnp.arange(half, dtype=F32) / half)
    ang = pos.astype(F32)[:, None] * inv[None, :]
    cos, sin = jnp.cos(ang), jnp.sin(ang)
    return jnp.concatenate([cos, cos], axis=1), jnp.concatenate([-sin, sin], axis=1)


def _prep_weights(W):
    D = W['w_kv'].shape[0]
    H = D // HEAD_DIM
    R = H // N_KV_HEADS
    pg = D // len(POOL_WINDOWS)
    half = CMP_STRIDE * HEAD_DIM
    P = {}
    P['w_pool'] = W['w_pool'].astype(BF16).reshape(-1, len(POOL_WINDOWS) * pg, pg)
    kv_gain = jnp.ones((W['w_kv'].shape[1], KV_W), F32)
    kv_gain = kv_gain.at[2].set(jnp.tile(W['g_k_sel'], N_KV_HEADS)).at[4].set(jnp.tile(W['g_k_win'], N_KV_HEADS))
    P['kv_gain'] = kv_gain
    for t in ('k', 'v'):
        w1 = W['w_cmp_%s1' % t].astype(BF16)
        P['w_cmp_%s1' % t] = w1
        P['w_cmp_%scat' % t] = jnp.concatenate([w1[:half], w1[half:]], axis=1)
        P['w_cmp_%s2' % t] = W['w_cmp_%s2' % t].astype(BF16)
        P['pe_%s' % t] = W['pe_cmp_%s' % t].astype(BF16).reshape(1, -1)
    n_b = W['w_qg'].shape[0]
    wg = W['w_qg'][:, :, H * HEAD_DIM:].astype(BF16).reshape(n_b, D, N_KV_HEADS, R * N_BRANCH)
    wg = jnp.pad(wg, ((0, 0), (0, 0), (0, 0), (0, LANE - R * N_BRANCH)))
    P['w_gate'] = wg.reshape(n_b, D, N_KV_HEADS * LANE)
    P['w_ple'] = W['w_ple'].astype(BF16)
    P[('w_up', 0)] = W['w_up'][0].astype(BF16)
    return P


def _dense(x, wname, layer, W, P, *, N, epilogue, specs, extras=(), out_dtypes, emit=False, side_cast=None, row_scale_in=None,
           norm_gains=None, name):
    M, K = x.shape
    shapes = [d if isinstance(d, jax.ShapeDtypeStruct) else jax.ShapeDtypeStruct((M, N), d) for d in out_dtypes]
    key = (wname, layer)
    tmx = 1024 if M >= 1024 else M
    use_mm = key in P
    if use_mm:
        grid, tm, tn, tk = _tiles(M, N, K, tmx, 1024, 2048 if (M >= 1024 and K > 4096) else 4096)
    else:
        grid, tm, tn, tk = _tiles(M, N, K, tmx, 512, K if K <= 4096 else 2048)
    extra_specs, out_specs = specs(tm, tn)
    extras = list(extras)
    if row_scale_in is not None:
        epilogue = _row_scaled(epilogue)
        extras.insert(0, row_scale_in)
        extra_specs = [pl.BlockSpec((tm, LANE), lambda i, j, k: (i, 0))] + list(extra_specs)
    if norm_gains:
        epilogue = _norm_producer(epilogue, len(extras))
        n_ex, n_ex_specs, n_shapes, n_specs = _norm_io(norm_gains, M, N, grid[1], tm, tn)
        extras, extra_specs = extras + n_ex, list(extra_specs) + n_ex_specs
        shapes, out_specs = shapes + n_shapes, list(out_specs) + n_specs
    side = None if side_cast is None else (W[side_cast[0]], side_cast[1])
    if use_mm:
        outs = list(mm(x, P[key], grid=grid, tm=tm, tn=tn, tk=tk, epilogue=epilogue, extras=extras,
                       extra_specs=extra_specs, out_shapes=shapes, out_specs=out_specs, side=side, name=name))
    else:
        w = W[wname]
        outs = list(mm_ws(x, w, layer=layer if w.ndim == 3 else None, emit=emit, side=side, grid=grid, tm=tm,
                          tn=tn, tk=tk, epilogue=epilogue, extras=extras, extra_specs=extra_specs,
                          out_shapes=shapes, out_specs=out_specs, name=name))
    if side is not None:
        P[side_cast] = outs.pop()
    if emit and not use_mm:
        P[key] = outs.pop()
    if norm_gains:
        outs.append(row_scale(outs.pop(), N))
    return outs


def _ffn_ple(h, normed, p_l, layer, W, P, next_gains):
    M, D = h.shape
    F = W['w_up'].shape[2]
    host = M >= 1024
    next_up = ('w_up', layer + 1) if host and layer + 1 < W['w_up'].shape[0] else None
    mn = lambda tm, tn: ([], [_mn_spec(tm, tn)])
    res = lambda tm, tn: ([_mn_spec(tm, tn)], [_mn_spec(tm, tn)])
    y, rs = normed
    (u,) = _dense(y, 'w_up', layer, W, P, N=F, epilogue=lambda acc, rows: (jnp.square(jnp.maximum(acc, 0.0)),),
                  specs=mn, out_dtypes=[BF16], side_cast=('w_down', layer) if host else None, row_scale_in=rs,
                  name="ffn_up")
    h, y, rs = _dense(u, 'w_down', layer, W, P, N=D, epilogue=lambda acc, rows, r: (r[rows, :] + acc,), specs=res,
                      extras=[h], out_dtypes=[F32], norm_gains=[W['g_ple'][layer]], side_cast=next_up,
                      name="ffn_down")
    ple_dim = p_l.shape[1]
    ple_specs = lambda tm, tn: ([_mn_spec(tm, tn), pl.BlockSpec((tm, ple_dim), lambda i, j, k: (i, 0)),
                                 pl.BlockSpec((ple_dim, tn), lambda i, j, k: (0, j))], [_mn_spec(tm, tn)])
    return _dense(y, 'w_ple_gate', layer, W, P, N=D,
                  epilogue=lambda acc, rows, r, pp, wp: (r[rows, :] + _dot(pp[rows, :], wp[...]) * _sigmoid(acc),),
                  specs=ple_specs, extras=[h, p_l.astype(BF16), P['w_ple'][layer]], out_dtypes=[F32],
                  row_scale_in=rs, norm_gains=next_gains, name="ple")


def _kv_epilogue(acc, rows, gain_ref, cos_ref, sin_ref):
    j = pl.program_id(0)
    cosf, sinf = cos_ref[rows, :], sin_ref[rows, :]
    heads = []
    for hh in range(N_KV_HEADS):
        sl = slice(hh * HEAD_DIM, (hh + 1) * HEAD_DIM)
        heads.append(_rope(_head_norm(acc[:, sl], gain_ref[0, :, sl]), cosf, sinf))
    out = jnp.where((j == 2) | (j == 4), jnp.concatenate(heads, axis=1), acc)
    return out, out


def _q_epilogue(acc, rows, gq_ref, cos_ref, sin_ref):
    cosf, sinf = cos_ref[rows, :], sin_ref[rows, :]
    qs, qrs = [], []
    for hh in range(acc.shape[1] // HEAD_DIM):
        qn = _head_norm(acc[:, hh * HEAD_DIM:(hh + 1) * HEAD_DIM], gq_ref[...])
        qs.append(qn)
        qrs.append(_rope(qn, cosf, sinf))
    return jnp.concatenate(qs, axis=1), jnp.concatenate(qrs, axis=1)


def _trunk(x, p, pool_prefix, pos0, W, P, attend):
    B, T, D = x.shape
    M = B * T
    tmx = 1024 if M >= 1024 else M
    h = x.reshape(M, D)
    pg = D // len(POOL_WINDOWS)

    d, pool_new = pool_diff(x, W['g_mix'][0], pool_prefix[0], pos0)
    d, pool_new = d.reshape(M, D), pool_new[None]
    grid, tm, tn, tk = _tiles(M, D, pg, tmx, pg, pg)
    n_ex, n_ex_specs, n_shapes, n_specs = _norm_io([W['g_ffn'][0]], M, D, grid[1], tm, tn)
    h, y, ssq = mm(d, P['w_pool'][0], grid=grid, tm=tm, tn=tn, tk=tk,
                   x_map=lambda i, j, k: (i, j), w_map=lambda i, j, k: (j, 0),
                   epilogue=_norm_producer(lambda acc, rows, sc, r: (r[rows, :] + acc * sc[...],), 2),
                   extras=[W['pool_scale'][0].reshape(1, D), h] + n_ex,
                   extra_specs=[pl.BlockSpec((1, tn), lambda i, j, k: (0, j)), _mn_spec(tm, tn)] + n_ex_specs,
                   out_shapes=[jax.ShapeDtypeStruct((M, D), F32)] + n_shapes,
                   out_specs=[_mn_spec(tm, tn)] + n_specs, name="pool_mix")
    h, hkv, a1, rs = _ffn_ple(h, (y, row_scale(ssq, D)), p[0].reshape(M, -1), 0, W, P,
                              [W['g_kv'], W['g_mix'][1]])

    pos = pos0 + jnp.tile(jnp.arange(T, dtype=jnp.int32), B)
    cosf, sinf = _rope_tables(pos)
    n_kv = W['w_kv'].shape[1]
    rope_spec = lambda tm: pl.BlockSpec((tm, HEAD_DIM), lambda i, j, k: (i, 0))
    kv_spec = lambda tm: pl.BlockSpec((1, tm, KV_W), lambda i, j, k: (j, i, 0))
    kv4_spec = lambda tm: pl.BlockSpec((1, tm, N_KV_HEADS, HEAD_DIM), lambda i, j, k: (j, i, 0, 0))
    kv_specs = lambda tm, tn: ([pl.BlockSpec((1, 1, KV_W), lambda i, j, k: (j, 0, 0)), rope_spec(tm), rope_spec(tm)],
                               [kv4_spec(tm), kv_spec(tm)])
    kv, kv_b = _dense(hkv, 'w_kv2d', None, W, P, N=n_kv * KV_W, epilogue=_kv_epilogue, specs=kv_specs,
                      extras=[P['kv_gain'].reshape(n_kv, 1, KV_W), cosf, sinf],
                      out_dtypes=[jax.ShapeDtypeStruct((n_kv, M, N_KV_HEADS, HEAD_DIM), F32),
                                  jax.ShapeDtypeStruct((n_kv, M, KV_W), BF16)], row_scale_in=rs, name="kv_proj")
    q_specs = lambda tm, tn: ([pl.BlockSpec((1, HEAD_DIM), lambda i, j, k: (0, 0)), rope_spec(tm), rope_spec(tm)],
                              [_mn_spec(tm, tn)] * 2)
    q, qr = _dense(a1, 'w_qg', 0, W, P, N=D, epilogue=_q_epilogue, specs=q_specs,
                   extras=[W['g_q'][0].reshape(1, HEAD_DIM), cosf, sinf], out_dtypes=[BF16, BF16], row_scale_in=rs,
                   name="q_proj")
    ng = N_KV_HEADS * LANE
    grid, tm, tn, tk = _tiles(M, ng, D, tmx, ng, D)
    (gates,) = mm(a1, P['w_gate'][0], grid=grid, tm=tm, tn=tn, tk=tk,
                  epilogue=_row_scaled(lambda acc, rows: (_sigmoid(acc),)), extras=[rs],
                  extra_specs=[pl.BlockSpec((tm, LANE), lambda i, j, k: (i, 0))],
                  out_shapes=[jax.ShapeDtypeStruct((M, ng), F32)], out_specs=[_mn_spec(tm, tn)], name="gate_proj")

    o, win_state = attend(kv, kv_b, q, qr, gates)

    h, y, rs = _dense(o, 'w_o', 0, W, P, N=D, epilogue=lambda acc, rows, r: (r[rows, :] + acc,), extras=[h],
                      specs=lambda tm, tn: ([_mn_spec(tm, tn)], [_mn_spec(tm, tn)]), out_dtypes=[F32],
                      norm_gains=[W['g_ffn'][1]], name="attn_out")
    (h,) = _ffn_ple(h, (y, rs), p[1].reshape(M, -1), 1, W, P, [])
    rows = tuple(kv[n].reshape(B, T, N_KV_HEADS, HEAD_DIM) for n in range(4))
    return h.reshape(B, T, D), pool_new, rows, win_state


def _compress(k_pages, v_pages, table, cp, W, P):
    abk, abv = cmp_partials(k_pages, v_pages, table, P['w_cmp_kcat'], P['w_cmp_vcat'], cp)
    gain = W['g_k_cmp'].reshape(1, HEAD_DIM)
    ck = cmp_finish(abk, P['pe_k'], P['w_cmp_k1'], P['w_cmp_k2'], gain, norm=True)
    cv = cmp_finish(abv, P['pe_v'], P['w_cmp_v1'], P['w_cmp_v2'], gain, norm=False)
    return ck, cv


def kernel(x_prompt, x_sample, state_pool, cache_k_cmp, cache_v_cmp, cache_k_sel, cache_v_sel, state_k_win, state_v_win, page_table, p_prompt, p_sample, g_mix, w_pool, pool_scale, g_kv, w_kv, g_k_cmp, g_k_sel, g_k_win, w_cmp_k1, w_cmp_k2, pe_cmp_k, w_cmp_v1, w_cmp_v2, pe_cmp_v, w_qg, g_q, w_o, g_ffn, w_up, w_down, g_ple, w_ple, w_ple_gate):
    W = dict(g_mix=g_mix, w_pool=w_pool, pool_scale=pool_scale, g_kv=g_kv, w_kv=w_kv, g_k_cmp=g_k_cmp,
             g_k_sel=g_k_sel, g_k_win=g_k_win, w_cmp_k1=w_cmp_k1, w_cmp_k2=w_cmp_k2, pe_cmp_k=pe_cmp_k,
             w_cmp_v1=w_cmp_v1, w_cmp_v2=w_cmp_v2, pe_cmp_v=pe_cmp_v, w_qg=w_qg, g_q=g_q, w_o=w_o,
             g_ffn=g_ffn, w_up=w_up, w_down=w_down, g_ple=g_ple, w_ple=w_ple, w_ple_gate=w_ple_gate)
    P = _prep_weights(W)
    W['w_kv2d'] = w_kv.reshape(w_kv.shape[0], -1)
    Bp, Tp, D = x_prompt.shape
    Bs, Ts, _ = x_sample.shape
    assert Ts == 1, "the decode path handles one new token per sequence"
    page = cache_k_cmp.shape[1]
    past_len = page_table.shape[1] * page
    R = D // HEAD_DIM // N_KV_HEADS
    assert Tp % page == 0 and past_len % SEL_BLOCK == 0

    def attend_prompt(kv, kv_b, q, qr, gates):
        ppb = Tp // page
        table = jnp.arange(Bp * ppb, dtype=jnp.int32).reshape(Bp, ppb)
        pages = lambda a: a.reshape(-1, page, N_KV_HEADS, HEAD_DIM)
        ck, cv = _compress(pages(kv[0]), pages(kv[1]), table, ppb, W, P)
        nc = Tp // CMP_STRIDE - CMP_BLOCK // CMP_STRIDE + 1
        seq = lambda a: a.reshape(Bp, Tp, -1)
        o = attn_prompt(seq(q), seq(qr), ck, cv, kv_b.reshape(-1, Bp, Tp, KV_W), seq(gates), nc=nc)
        nw = min(WINDOW, Tp)
        win = tuple(kv[n].reshape(Bp, Tp, N_KV_HEADS, HEAD_DIM)[:, -nw:] for n in (4, 5))
        return o.reshape(Bp * Tp, D), win

    def attend_sample(kv, kv_b, q, qr, gates):
        del kv_b
        ck, cv = _compress(cache_k_cmp, cache_v_cmp, page_table, min(32, page_table.shape[1]), W, P)
        nc = (past_len - (CMP_BLOCK - 1)) // CMP_STRIDE + 1
        ns = past_len // SEL_BLOCK + 1
        wb = state_k_win.shape[1]
        new_row = lambda a: a.reshape(Bs, 1, N_KV_HEADS, HEAD_DIM)
        kw = jnp.concatenate([state_k_win, new_row(kv[4])], axis=1)[:, -wb:]
        vw = jnp.concatenate([state_v_win, new_row(kv[5])], axis=1)[:, -wb:]
        heads = lambda a: a.reshape(Bs, N_KV_HEADS, R, HEAD_DIM)
        ocmp, owin, ids = attn_dec_dense(heads(q), heads(qr), ck, cv, kw, vw, nc=nc, ns=ns, pos=past_len)
        gt = gates.reshape(Bs, N_KV_HEADS, LANE)[:, :, :R * N_BRANCH].reshape(Bs, N_KV_HEADS, R, N_BRANCH)
        gt = jnp.pad(gt, ((0, 0), (0, 0), (0, 0), (0, LANE - N_BRANCH)))
        o = attn_dec_sel(page_table, ids, heads(qr), cache_k_sel, cache_v_sel,
                         kv[2].reshape(Bs, 1, KV_W), kv[3].reshape(Bs, 1, KV_W), ocmp, owin, gt, pos=past_len)
        return o.reshape(Bs, D).astype(BF16), (kw, vw)

    pool_zero = jnp.zeros((state_pool.shape[0], Bp, POOL_STATE, D), x_prompt.dtype)
    y_p, pool_p, rows_p, win_p = _trunk(x_prompt, p_prompt, pool_zero, 0, W, P, attend_prompt)
    y_s, pool_s, rows_s, win_s = _trunk(x_sample, p_sample, state_pool, past_len, W, P, attend_sample)
    return (y_p, y_s, pool_p, pool_s, rows_p[0], rows_p[1], rows_p[2], rows_p[3], win_p[0], win_p[1],
            rows_s[0], rows_s[1], rows_s[2], rows_s[3], win_s[0], win_s[1])
```

```python
import functools

import jax
import jax.numpy as jnp
import numpy as np
from jax import lax
from jax.experimental import pallas as pl
from jax.experimental.pallas import tpu as pltpu

F32 = jnp.float32
BF16 = jnp.bfloat16

POOL_WINDOWS = (2, 4, 8, 16)
POOL_STATE = max(POOL_WINDOWS) - 1
POOL_PAD = POOL_STATE + 1
HEAD_DIM = 128
N_KV_HEADS = 4
N_BRANCH = 3
CMP_BLOCK = 32
CMP_STRIDE = 16
CMP_HIDDEN = 2 * HEAD_DIM
SEL_BLOCK = 64
N_SEL = 16
WINDOW = 512
ROPE_THETA = 10000.0
EPS = 1e-6
SCALE = HEAD_DIM ** -0.5
NEG = -1e30
FORCE = 1e9
PAD_SCORE = -3e38
KV_W = N_KV_HEADS * HEAD_DIM
LANE = 128
VMEM_LIMIT = 56 * 1024 * 1024


def _params(sem):
    return pltpu.CompilerParams(dimension_semantics=sem, vmem_limit_bytes=VMEM_LIMIT)


def _sigmoid(x):
    return 1.0 / (1.0 + jnp.exp(-x))


def _dot(a, b):
    return jnp.dot(a, b, preferred_element_type=F32)


def _dot_nt(a, b):
    return lax.dot_general(a, b, (((1,), (1,)), ((), ())), preferred_element_type=F32)


def _dot_tn(a, b):
    return lax.dot_general(a, b, (((0,), (0,)), ((), ())), preferred_element_type=F32)


def _head_norm(x, g):
    return x * lax.rsqrt(jnp.mean(x * x, axis=-1, keepdims=True) + EPS) * g


def _rope(x, cosf, sinf):
    return x * cosf + pltpu.roll(x, HEAD_DIM // 2, 1) * sinf


def _fold_lanes(x):
    parts = [x[:, c * LANE:(c + 1) * LANE] for c in range(x.shape[1] // LANE)]
    return functools.reduce(lambda u, v: u + v, parts)


def _norm_producer(epilogue, n_base):
    def wrapped(acc, rows, *extras):
        (h,) = epilogue(acc, rows, *extras[:n_base])
        return (h, *[(h * g[...]).astype(BF16) for g in extras[n_base:]], _fold_lanes(h * h))
    return wrapped


def _row_scaled(epilogue):
    def wrapped(acc, rows, rs_ref, *extras):
        rs = rs_ref[rows, :]
        acc = jnp.concatenate([acc[:, c * LANE:(c + 1) * LANE] * rs for c in range(acc.shape[1] // LANE)], axis=1)
        return epilogue(acc, rows, *extras)
    return wrapped


def _row_scale_kernel(ssq_ref, o_ref, *, d):
    tot = jnp.sum(functools.reduce(lambda u, v: u + v, [ssq_ref[j] for j in range(ssq_ref.shape[0])]),
                  axis=1, keepdims=True)
    o_ref[...] = jnp.broadcast_to(lax.rsqrt(tot / d + EPS), o_ref.shape)


def row_scale(ssq, d):
    gn, M, _ = ssq.shape
    tm = min(M, 1024)
    return pl.pallas_call(
        functools.partial(_row_scale_kernel, d=d),
        grid=(M // tm,),
        in_specs=[pl.BlockSpec((gn, tm, LANE), lambda i: (0, i, 0))],
        out_specs=pl.BlockSpec((tm, LANE), lambda i: (i, 0)),
        out_shape=jax.ShapeDtypeStruct((M, LANE), F32),
        compiler_params=_params(("parallel",)),
        name="row_scale",
    )(ssq)


def _norm_io(gains, M, N, gn, tm, tn):
    extras = [g.reshape(1, N).astype(F32) for g in gains]
    extra_specs = [pl.BlockSpec((1, tn), lambda i, j, k: (0, j)) for _ in gains]
    shapes = [jax.ShapeDtypeStruct((M, N), BF16) for _ in gains] + [jax.ShapeDtypeStruct((gn, M, LANE), F32)]
    specs = [_mn_spec(tm, tn) for _ in gains] + [pl.BlockSpec((1, tm, LANE), lambda i, j, k: (j, i, 0))]
    return extras, extra_specs, shapes, specs


EPILOGUE_ROWS = 256


def _finish_rows(acc_of, tm, outs, extras, epilogue):
    ch = min(tm, EPILOGUE_ROWS)
    for c in range(tm // ch):
        rows = slice(c * ch, (c + 1) * ch)
        for o_ref, r in zip(outs, epilogue(acc_of(rows), rows, *extras)):
            if len(o_ref.shape) == 4:
                for hh in range(o_ref.shape[2]):
                    o_ref[0, rows, hh, :] = r[:, hh * HEAD_DIM:(hh + 1) * HEAD_DIM].astype(o_ref.dtype)
            elif len(o_ref.shape) == 3:
                o_ref[0, rows, :] = r.astype(o_ref.dtype)
            else:
                o_ref[rows, :] = r.astype(o_ref.dtype)


def _k_steps(x_ref, w, acc_ref, nk, outs, extras, epilogue):
    k = pl.program_id(2)

    @pl.when(k == 0)
    def _():
        acc_ref[...] = _dot(x_ref[...], w())

    @pl.when((k > 0) & (k < nk - 1))
    def _():
        acc_ref[...] += _dot(x_ref[...], w())

    @pl.when(k == nk - 1)
    def _():
        _finish_rows(lambda rows: acc_ref[rows, :] + _dot(x_ref[rows, :], w()), x_ref.shape[0], outs, extras,
                     epilogue)


def _side_io(side, n_steps, step):
    s_arr, s_layer = side
    _, rows, cols = s_arr.shape
    rs = rows // n_steps
    assert rs * n_steps == rows and rs % 16 == 0
    in_spec = pl.BlockSpec((None, rs, cols), lambda *g: (s_layer, step(*g), 0))
    out_spec = pl.BlockSpec((rs, cols), lambda *g: (step(*g), 0))
    return s_arr, in_spec, jax.ShapeDtypeStruct((rows, cols), BF16), out_spec


def _mm_kernel(*refs, nk, n_extra, n_out, epilogue, side):
    x_ref, w_ref = refs[0], refs[1]
    extras = refs[2:2 + n_extra]
    n_in = 2 + n_extra + side
    outs = refs[n_in:n_in + n_out]
    tm = x_ref.shape[0]
    if side:
        refs[n_in + n_out][...] = refs[n_in - 1][...].astype(BF16)

    if nk == 1:
        _finish_rows(lambda rows: _dot(x_ref[rows, :], w_ref[...]), tm, outs, extras, epilogue)
    else:
        _k_steps(x_ref, lambda: w_ref[...], refs[-1], nk, outs, extras, epilogue)


def mm(x, w, *, grid, tm, tn, tk, epilogue, extras=(), extra_specs=(), out_shapes, out_specs,
       x_map=None, w_map=None, side=None, name):
    gm, gn, nk = grid
    x_map = x_map or (lambda i, j, k: (i, k))
    w_map = w_map or (lambda i, j, k: (k, j))
    ins, in_specs = [x, w, *extras], [pl.BlockSpec((tm, tk), x_map), pl.BlockSpec((tk, tn), w_map)]
    in_specs += list(extra_specs)
    n_out, out_shapes, out_specs = len(out_shapes), list(out_shapes), list(out_specs)
    if side is not None:
        s_arr, s_in, s_shape, s_out = _side_io(side, gm * gn * nk, lambda i, j, k: (i * gn + j) * nk + k)
        ins.append(s_arr), in_specs.append(s_in), out_shapes.append(s_shape), out_specs.append(s_out)
    kern = functools.partial(_mm_kernel, nk=nk, n_extra=len(extras), n_out=n_out, epilogue=epilogue,
                             side=side is not None)
    return pl.pallas_call(
        kern,
        grid=grid,
        in_specs=in_specs,
        out_specs=out_specs,
        out_shape=out_shapes,
        scratch_shapes=[pltpu.VMEM((tm, tn), F32)] if nk > 1 else [],
        compiler_params=_params(("parallel", "parallel", "arbitrary")),
        name=name,
    )(*ins)


def _mm_ws_kernel(*refs, nk, n_extra, n_out, epilogue, emit, side):
    x_ref, w_ref = refs[0], refs[1]
    extras = refs[2:2 + n_extra]
    n_in = 2 + n_extra + side
    outs = refs[n_in:n_in + n_out]
    rest = refs[n_in + n_out:]
    wb_ref = rest[emit + side]
    i, k = pl.program_id(1), pl.program_id(2)

    @pl.when(i == 0)
    def _():
        wb_ref[k] = w_ref[...].astype(BF16)
        if emit:
            rest[0][...] = wb_ref[k]

    if side:
        rest[emit][...] = refs[n_in - 1][...].astype(BF16)

    tm = x_ref.shape[0]
    if nk == 1:
        _finish_rows(lambda rows: _dot(x_ref[rows, :], wb_ref[0]), tm, outs, extras, epilogue)
    else:
        _k_steps(x_ref, lambda: wb_ref[k], rest[-1], nk, outs, extras, epilogue)


def mm_ws(x, w, *, layer=None, emit=False, side=None, grid, tm, tn, tk, epilogue, extras=(), extra_specs=(),
          out_shapes, out_specs, name):
    gm, gn, nk = grid
    swap = lambda f: (lambda j, i, k: f(i, j, k))
    respec = lambda s: pl.BlockSpec(s.block_shape, swap(s.index_map))
    k_once = lambda i, k: jnp.where(i == 0, k, nk - 1)
    if layer is None:
        w_spec = pl.BlockSpec((tk, tn), lambda j, i, k: (k_once(i, k), j))
    else:
        w_spec = pl.BlockSpec((None, tk, tn), lambda j, i, k: (layer, k_once(i, k), j))
    n_out = len(out_shapes)
    ins, in_specs = [x, w, *extras], [pl.BlockSpec((tm, tk), lambda j, i, k: (i, k)), w_spec]
    in_specs += [respec(s) for s in extra_specs]
    out_shapes, out_specs = list(out_shapes), [respec(s) for s in out_specs]
    if emit:
        out_shapes.append(jax.ShapeDtypeStruct((nk * tk, gn * tn), BF16))
        out_specs.append(pl.BlockSpec((tk, tn), lambda j, i, k: (k_once(i, k), j)))
    if side is not None:
        s_arr, s_in, s_shape, s_out = _side_io(side, gm * gn * nk, lambda j, i, k: (j * gm + i) * nk + k)
        ins.append(s_arr), in_specs.append(s_in), out_shapes.append(s_shape), out_specs.append(s_out)
    kern = functools.partial(_mm_ws_kernel, nk=nk, n_extra=len(extras), n_out=n_out, epilogue=epilogue, emit=emit,
                             side=side is not None)
    return pl.pallas_call(
        kern,
        grid=(gn, gm, nk),
        in_specs=in_specs,
        out_specs=out_specs,
        out_shape=out_shapes,
        scratch_shapes=[pltpu.VMEM((nk, tk, tn), BF16)] + ([pltpu.VMEM((tm, tn), F32)] if nk > 1 else []),
        compiler_params=_params(("parallel", "arbitrary", "arbitrary")),
        name=name,
    )(*ins)


def _tiles(M, N, K, tm, tn, tk):
    tm, tn, tk = min(tm, M), min(tn, N), min(tk, K)
    return (M // tm, N // tn, K // tk), tm, tn, tk


def _mn_spec(tm, tn):
    return pl.BlockSpec((tm, tn), lambda i, j, k: (i, j))


def _pool_diff_kernel(*refs, tt, pos0, halo):
    x_ref, pre_ref, g_ref = refs[0], refs[1 + halo], refs[2 + halo]
    d_ref, st_ref, seq_ref = refs[3 + halo:]
    t = pl.program_id(1)
    norm = lambda x: x * lax.rsqrt(jnp.mean(x * x, axis=-1, keepdims=True) + EPS) * g_ref[...]
    a = norm(x_ref[0])
    seq_ref[POOL_PAD:POOL_PAD + tt, :] = a

    @pl.when(t == 0)
    def _():
        seq_ref[0:POOL_PAD, :] = pre_ref[0]

    if halo:
        @pl.when(t > 0)
        def _():
            seq_ref[0:POOL_PAD, :] = norm(refs[1][0])

    pos = pos0 + t * tt + lax.broadcasted_iota(jnp.int32, (tt, 1), 0)
    pg = a.shape[1] // len(POOL_WINDOWS)
    for g, w in enumerate(POOL_WINDOWS):
        cols = slice(g * pg, (g + 1) * pg)
        s = a[:, cols]
        for j in range(1, w):
            s = s + seq_ref[POOL_PAD - j:POOL_PAD - j + tt, cols]
        cnt = jnp.minimum(pos + 1, w).astype(F32)
        d_ref[0, :, cols] = (s / cnt - a[:, cols]).astype(d_ref.dtype)
    st_ref[0] = seq_ref[tt:tt + POOL_PAD, :]


def pool_diff(x, gain, prefix, pos0):
    B, T, D = x.shape
    tt = min(T, 256)
    halo = T > tt
    pre = jnp.concatenate([jnp.zeros((B, 1, D), F32), prefix], axis=1)
    hpt = tt // POOL_PAD
    in_specs = [pl.BlockSpec((1, tt, D), lambda b, t: (b, t, 0))]
    if halo:
        in_specs.append(pl.BlockSpec((1, POOL_PAD, D), lambda b, t: (b, jnp.maximum(t * hpt - 1, 0), 0)))
    in_specs += [pl.BlockSpec((1, POOL_PAD, D), lambda b, t: (b, 0, 0)), pl.BlockSpec((1, D), lambda b, t: (0, 0))]
    d, st = pl.pallas_call(
        functools.partial(_pool_diff_kernel, tt=tt, pos0=pos0, halo=halo),
        grid=(B, T // tt),
        in_specs=in_specs,
        out_specs=[pl.BlockSpec((1, tt, D), lambda b, t: (b, t, 0)),
                   pl.BlockSpec((1, POOL_PAD, D), lambda b, t: (b, 0, 0))],
        out_shape=[jax.ShapeDtypeStruct((B, T, D), BF16), jax.ShapeDtypeStruct((B, POOL_PAD, D), F32)],
        scratch_shapes=[pltpu.VMEM((POOL_PAD + tt, D), F32)],
        compiler_params=_params(("parallel", "arbitrary")),
        name="pool_diff",
    )(*([x, x] if halo else [x]), pre, gain.reshape(1, D).astype(F32))
    return d, st[:, 1:]


PAGES_PER_STEP = 8
CMP_FINISH_ROWS = 512


def _cmp_partial_kernel(pt_ref, *refs, steps, pps):
    del pt_ref
    k_pages, v_pages = refs[:pps], refs[pps:2 * pps]
    wk_ref, wv_ref, abk_ref, abv_ref, xk_ref, xv_ref = refs[2 * pps:]
    p = pl.program_id(2)
    G = N_KV_HEADS
    half = CMP_STRIDE // 2
    rows_per_page = (k_pages[0].shape[1] // half) * G
    for pages, x_ref in ((k_pages, xk_ref), (v_pages, xv_ref)):
        for q, page_ref in enumerate(pages):
            for sb in range(page_ref.shape[1] // half):
                row = q * rows_per_page + sb * G
                for r in range(CMP_STRIDE):
                    x_ref[p, row:row + G, r * HEAD_DIM:(r + 1) * HEAD_DIM] = (
                        page_ref[0, sb * half + r // 2, (r % 2) * G:(r % 2 + 1) * G, :])

    @pl.when(p == steps - 1)
    def _():
        for x_ref, w_ref, ab_ref in ((xk_ref, wk_ref, abk_ref), (xv_ref, wv_ref, abv_ref)):
            x = x_ref[...].reshape(ab_ref.shape[1], x_ref.shape[2])
            ab_ref[0] = _dot(x.astype(BF16), w_ref[...])


def cmp_partials(k_pages, v_pages, table, wk_cat, wv_cat, cp):
    B, ppb = table.shape
    page = k_pages.shape[1]
    pps = PAGES_PER_STEP
    rows_per_page = page // CMP_STRIDE * N_KV_HEADS
    nch, steps = ppb // cp, cp // pps
    m = cp * rows_per_page
    kdim = CMP_STRIDE * HEAD_DIM
    pair_rows = lambda a: a.reshape(a.shape[0], page // 2, 2 * N_KV_HEADS, HEAD_DIM)

    def page_spec(q):
        return pl.BlockSpec((1, page // 2, 2 * N_KV_HEADS, HEAD_DIM),
                            lambda b, c, p, pt: (pt[b, c * cp + p * pps + q], 0, 0, 0))

    page_specs = [page_spec(q) for q in range(pps)]
    w_spec = pl.BlockSpec((kdim, 2 * CMP_HIDDEN), lambda b, c, p, pt: (0, 0))
    out_spec = pl.BlockSpec((1, m, 2 * CMP_HIDDEN), lambda b, c, p, pt: (b, c, 0))
    out_shape = jax.ShapeDtypeStruct((B, ppb * rows_per_page, 2 * CMP_HIDDEN), F32)
    x_scratch = pltpu.VMEM((steps, pps * rows_per_page, kdim), F32)
    return pl.pallas_call(
        functools.partial(_cmp_partial_kernel, steps=steps, pps=pps),
        grid_spec=pltpu.PrefetchScalarGridSpec(
            num_scalar_prefetch=1,
            grid=(B, nch, steps),
            in_specs=page_specs + page_specs + [w_spec, w_spec],
            out_specs=[out_spec, out_spec],
            scratch_shapes=[x_scratch, x_scratch],
        ),
        out_shape=[out_shape, out_shape],
        compiler_params=_params(("parallel", "parallel", "arbitrary")),
        name="cmp_partials",
    )(table, *([pair_rows(k_pages)] * pps), *([pair_rows(v_pages)] * pps), wk_cat, wv_cat)


def _cmp_finish_kernel(ab_ref, pe_ref, w1_ref, w2_ref, g_ref, o_ref, out_ref, *, norm):
    n4 = ab_ref.shape[1]
    G = o_ref.shape[1]
    bias = _dot(jnp.broadcast_to(pe_ref[...], (8, pe_ref.shape[1])), w1_ref[...])[0:1, :]
    ch = min(n4, CMP_FINISH_ROWS)
    for c in range(n4 // ch):
        lo, hi = c * ch, (c + 1) * ch
        first = ab_ref[0, lo:hi, :CMP_HIDDEN]
        if hi + G <= n4:
            second = ab_ref[0, lo + G:hi + G, CMP_HIDDEN:]
        else:
            second = jnp.concatenate([ab_ref[0, lo + G:hi, CMP_HIDDEN:], ab_ref[0, hi - G:hi, CMP_HIDDEN:]], axis=0)
        pre = first + second + bias
        out = _dot((pre * _sigmoid(pre)).astype(BF16), w2_ref[...])
        if norm:
            out = _head_norm(out, g_ref[...])
        out_ref[lo:hi, :] = out
    for g in range(G):
        o_ref[0, g] = out_ref[pl.ds(g, n4 // G, stride=G), :].astype(o_ref.dtype)


def cmp_finish(ab, pe, w1, w2, gain, *, norm):
    B, n4, _ = ab.shape
    G = N_KV_HEADS
    full = lambda a: pl.BlockSpec(a.shape, lambda b: (0,) * a.ndim)
    args = (pe, w1, w2, gain)
    return pl.pallas_call(
        functools.partial(_cmp_finish_kernel, norm=norm),
        grid=(B,),
        in_specs=[pl.BlockSpec((1, n4, 2 * CMP_HIDDEN), lambda b: (b, 0, 0))] + [full(a) for a in args],
        out_specs=pl.BlockSpec((1, G, n4 // G, HEAD_DIM), lambda b: (b, 0, 0, 0)),
        out_shape=jax.ShapeDtypeStruct((B, G, n4 // G, HEAD_DIM), BF16),
        scratch_shapes=[pltpu.VMEM((n4, HEAD_DIM), F32)],
        compiler_params=_params(("parallel",)),
        name="cmp_finish",
    )(ab, *args)


def _select_blocks(score, blk, ns):
    rank = jnp.zeros(score.shape, jnp.int32)
    for j in range(ns):
        sj = score[j:j + 1, :]
        beats = (sj > score) | ((sj == score) & (j < blk))
        rank = rank + beats.astype(jnp.int32)
    return (rank < min(N_SEL, ns)) & (score > 0.5 * NEG)


def _attn_prompt_kernel(q_ref, qr_ref, ck_ref, cv_ref, ks_ref, vs_ref, kw_ref, vw_ref, gate_ref, mapT_ref,
                        expand_ref, o_ref, part_ref, sbias_ref, wbias_ref, m_ref, acc_ref, *, tq, kc, nc, ns, R):
    qi = pl.program_id(2)
    q0 = qi * tq
    pos = q0 + lax.broadcasted_iota(jnp.int32, (tq, 1), 0)
    ncp = ck_ref.shape[2]

    ck = ck_ref[0, 0]
    cv = cv_ref[0, 0]
    cidx = lax.broadcasted_iota(jnp.int32, (1, ncp), 1)
    ok_c = (cidx * CMP_STRIDE + CMP_BLOCK - 1 <= pos) & (cidx < nc)
    imp = jnp.zeros((tq, ncp), F32)
    for r in range(R):
        qh = q_ref[0, :, r * HEAD_DIM:(r + 1) * HEAD_DIM]
        s = jnp.where(ok_c, _dot_nt(qh, ck) * SCALE, NEG)
        e = jnp.exp(s - jnp.max(s, axis=1, keepdims=True))
        p = jnp.where(ok_c, e * (1.0 / jnp.sum(e, axis=1, keepdims=True)), 0.0)
        imp = imp + p
        part_ref[r] = gate_ref[0, :, r * N_BRANCH:r * N_BRANCH + 1] * _dot(p.astype(BF16), cv)

    nsp = mapT_ref.shape[0]
    p_slc = lax.dot_general(mapT_ref[...], imp, (((1,), (1,)), ((), ())), precision=lax.Precision.HIGHEST,
                            preferred_element_type=F32)
    blk = lax.broadcasted_iota(jnp.int32, (nsp, tq), 0)
    pos_l = q0 + lax.broadcasted_iota(jnp.int32, (nsp, tq), 1)
    cur = pos_l // SEL_BLOCK
    vis = blk * SEL_BLOCK <= pos_l
    forced = vis & ((blk == 0) | (blk == cur) | (blk == cur - 1))
    score = jnp.where(forced, FORCE, jnp.where(vis, p_slc, NEG))
    score = jnp.where(blk < ns, score, PAD_SCORE)
    sel = _select_blocks(score, blk, ns).astype(BF16)

    c_hi = (q0 + tq) // kc
    col = lax.broadcasted_iota(jnp.int32, (1, kc), 1)
    sel_keys = _dot_tn(sel, expand_ref[...])
    for c in range(sbias_ref.shape[0]):
        @pl.when(c < c_hi)
        def _(c=c):
            ok = (sel_keys[:, c * kc:(c + 1) * kc] > 0.5) & (c * kc + col <= pos)
            sbias_ref[c] = jnp.where(ok, 0.0, NEG)
    nwc = wbias_ref.shape[0]
    c_w0 = c_hi - nwc
    for d in range(nwc):
        kpos = (c_w0 + d) * kc + col
        wbias_ref[d] = jnp.where((kpos <= pos) & (kpos > pos - WINDOW), 0.0, NEG)

    def fold(t):
        return [t[:, i * LANE:(i + 1) * LANE] for i in range(kc // LANE)]

    ones_blk = jnp.ones((kc, LANE), BF16)

    def branch(k_ref, v_ref, c_lo, bias_of, gate_col):
        def logits(r, c, bias):
            k = k_ref[0, pl.ds(pl.multiple_of(c * kc, kc), kc), :]
            return _dot_nt(qr_ref[0, :, r * HEAD_DIM:(r + 1) * HEAD_DIM], k) * SCALE + bias

        m_ref[...] = jnp.full(m_ref.shape, NEG, F32)
        acc_ref[...] = jnp.zeros(acc_ref.shape, F32)

        def max_body(c, carry):
            bias = bias_of(c)
            for r in range(R):
                mx = m_ref[r]
                for part in fold(logits(r, c, bias)):
                    mx = jnp.maximum(mx, part)
                m_ref[r] = mx
            return carry

        lax.fori_loop(c_lo, c_hi, max_body, 0)
        for r in range(R):
            m_ref[r] = jnp.broadcast_to(jnp.max(m_ref[r], axis=1, keepdims=True), (tq, LANE))

        def sum_body(c, carry):
            bias = bias_of(c)
            v = jnp.concatenate([v_ref[0, pl.ds(pl.multiple_of(c * kc, kc), kc), :], ones_blk], axis=1)
            for r in range(R):
                t = logits(r, c, bias)
                m = m_ref[r]
                ps = [jnp.exp(part - m) for part in fold(t)]
                acc_ref[r] += _dot(jnp.concatenate(ps, axis=1).astype(BF16), v)
            return carry

        lax.fori_loop(c_lo, c_hi, sum_body, 0)
        for r in range(R):
            gate = gate_ref[0, :, r * N_BRANCH + gate_col:r * N_BRANCH + gate_col + 1]
            part_ref[r] += gate * (acc_ref[r, :, :HEAD_DIM] * (1.0 / acc_ref[r, :, HEAD_DIM:]))

    branch(ks_ref, vs_ref, 0, lambda c: sbias_ref[c], 1)
    branch(kw_ref, vw_ref, jnp.maximum(c_w0, 0), lambda c: wbias_ref[c - c_w0], 2)
    for r in range(R):
        o_ref[0, :, r * HEAD_DIM:(r + 1) * HEAD_DIM] = part_ref[r].astype(o_ref.dtype)


def _overlap_map(ncp, nsp, ns):
    ratio = CMP_BLOCK // CMP_STRIDE
    per_sel = SEL_BLOCK // CMP_STRIDE
    m = np.zeros((ncp, nsp), np.float32)
    for b in range(ns):
        for mm_ in range(per_sel):
            for n in range(ratio):
                j = per_sel * b + mm_ - n
                if 0 <= j < ncp:
                    m[j, b] += 1.0
    return m


def attn_prompt(q, qr, ck, cv, kvb, gates, *, nc):
    B, T, HD = q.shape
    G = N_KV_HEADS
    R = HD // HEAD_DIM // G
    tq = min(T, 256)
    kc = tq
    ns = T // SEL_BLOCK
    nsp = -(-ns // 8) * 8
    ncp = ck.shape[2]
    mapT = jnp.asarray(_overlap_map(ncp, nsp, ns).T)
    expand = jnp.asarray((np.arange(T)[None, :] // SEL_BLOCK == np.arange(nsp)[:, None]).astype(np.float32), BF16)
    q_spec = pl.BlockSpec((1, tq, R * HEAD_DIM), lambda b, g, i: (b, i, g))
    c_spec = pl.BlockSpec((1, 1, ncp, HEAD_DIM), lambda b, g, i: (b, g, 0, 0))
    kv_spec = lambda n: pl.BlockSpec((None, 1, T, HEAD_DIM), lambda b, g, i: (n, b, 0, g))
    return pl.pallas_call(
        functools.partial(_attn_prompt_kernel, tq=tq, kc=kc, nc=nc, ns=ns, R=R),
        grid=(B, G, T // tq),
        in_specs=[q_spec, q_spec, c_spec, c_spec, kv_spec(2), kv_spec(3), kv_spec(4), kv_spec(5),
                  pl.BlockSpec((1, tq, LANE), lambda b, g, i: (b, i, g)),
                  pl.BlockSpec((nsp, ncp), lambda b, g, i: (0, 0)),
                  pl.BlockSpec((nsp, T), lambda b, g, i: (0, 0))],
        out_specs=q_spec,
        out_shape=jax.ShapeDtypeStruct((B, T, HD), BF16),
        scratch_shapes=[pltpu.VMEM((R, tq, HEAD_DIM), F32), pltpu.VMEM((T // kc, tq, kc), F32),
                        pltpu.VMEM((min(WINDOW, T) // kc + tq // kc, tq, kc), F32)]
                       + [pltpu.VMEM((R, tq, HEAD_DIM), F32), pltpu.VMEM((R, tq, HEAD_DIM + LANE), F32)],
        compiler_params=_params(("parallel", "parallel", "arbitrary")),
        name="attn_prompt",
    )(q, qr, ck, cv, kvb, kvb, kvb, kvb, gates, mapT, expand)


def _attn_dec_dense_kernel(q_ref, qr_ref, ck_ref, cv_ref, kw_ref, vw_ref, map_ref, ocmp_ref, owin_ref, ids_ref,
                           *, nc, ns, pos):
    G, R = q_ref.shape[1], q_ref.shape[2]
    ncp = ck_ref.shape[2]
    nsl = map_ref.shape[1]
    cidx = lax.broadcasted_iota(jnp.int32, (1, ncp), 1)
    ok_c = (cidx * CMP_STRIDE + CMP_BLOCK - 1 <= pos) & (cidx < nc)
    blk_l = lax.broadcasted_iota(jnp.int32, (1, nsl), 1)
    cur = pos // SEL_BLOCK
    vis = blk_l * SEL_BLOCK <= pos
    forced = vis & ((blk_l == 0) | (blk_l == cur) | (blk_l == cur - 1))
    ii = lax.broadcasted_iota(jnp.int32, (nsl, nsl), 0)
    jj = lax.broadcasted_iota(jnp.int32, (nsl, nsl), 1)
    slot = lax.broadcasted_iota(jnp.int32, (nsl, LANE), 1).astype(F32)
    blk_s = lax.broadcasted_iota(jnp.int32, (nsl, LANE), 0).astype(F32)
    for g in range(G):
        s = jnp.where(ok_c, _dot_nt(q_ref[0, g], ck_ref[0, g]) * SCALE, NEG)
        e = jnp.exp(s - jnp.max(s, axis=1, keepdims=True))
        p = jnp.where(ok_c, e * (1.0 / jnp.sum(e, axis=1, keepdims=True)), 0.0)
        ocmp_ref[0, g] = _dot(p.astype(BF16), cv_ref[0, g])
        imp = jnp.broadcast_to(jnp.sum(p, axis=0, keepdims=True), (R, ncp))
        p_slc = jnp.dot(imp, map_ref[...], precision=lax.Precision.HIGHEST, preferred_element_type=F32)[0:1, :]
        score_l = jnp.where(forced, FORCE, jnp.where(vis, p_slc, NEG))
        score_l = jnp.where(blk_l < ns, score_l, PAD_SCORE)
        score_s = jnp.sum(jnp.where(ii == jj, score_l, 0.0), axis=1, keepdims=True)
        beats = (score_l > score_s) | ((score_l == score_s) & (jj < ii))
        rank = jnp.sum(beats.astype(F32), axis=1, keepdims=True)
        ids = jnp.sum(jnp.where(rank == slot, blk_s, 0.0), axis=0, keepdims=True)
        ids_ref[0, g] = ids[:, :N_SEL].astype(jnp.int32)
        kw = kw_ref[0, :, g, :].astype(BF16)
        vw = vw_ref[0, :, g, :].astype(BF16)
        s = _dot_nt(qr_ref[0, g], kw) * SCALE
        e = jnp.exp(s - jnp.max(s, axis=1, keepdims=True))
        p = e * (1.0 / jnp.sum(e, axis=1, keepdims=True))
        owin_ref[0, g] = _dot(p.astype(BF16), vw)


def attn_dec_dense(q, qr, ck, cv, kw, vw, *, nc, ns, pos):
    B, G, R, _ = q.shape
    ncp = ck.shape[2]
    wb = kw.shape[1]
    nsl = -(-ns // LANE) * LANE
    omap = jnp.asarray(_overlap_map(ncp, nsl, ns))
    q_spec = pl.BlockSpec((1, G, R, HEAD_DIM), lambda b: (b, 0, 0, 0))
    c_spec = pl.BlockSpec((1, G, ncp, HEAD_DIM), lambda b: (b, 0, 0, 0))
    w_spec = pl.BlockSpec((1, wb, G, HEAD_DIM), lambda b: (b, 0, 0, 0))
    return pl.pallas_call(
        functools.partial(_attn_dec_dense_kernel, nc=nc, ns=ns, pos=pos),
        grid=(B,),
        in_specs=[q_spec, q_spec, c_spec, c_spec, w_spec, w_spec, pl.BlockSpec((ncp, nsl), lambda b: (0, 0))],
        out_specs=[q_spec, q_spec, pl.BlockSpec((1, G, 1, N_SEL), lambda b: (b, 0, 0, 0))],
        out_shape=[jax.ShapeDtypeStruct((B, G, R, HEAD_DIM), F32), jax.ShapeDtypeStruct((B, G, R, HEAD_DIM), F32),
                   jax.ShapeDtypeStruct((B, G, 1, N_SEL), jnp.int32)],
        compiler_params=_params(("parallel",)),
        name="attn_dec_dense",
    )(q, qr, ck, cv, kw, vw, omap)


DEC_BLOCKS_PER_STEP = 4


def _attn_dec_sel_kernel(pt_ref, ids_ref, qr_ref, *refs, n_past, pos):
    del pt_ref
    G = qr_ref.shape[1]
    nb = DEC_BLOCKS_PER_STEP
    kc_refs, vc_refs = refs[:G * nb], refs[G * nb:2 * G * nb]
    kn_ref, vn_ref, ocmp_ref, owin_ref, gate_ref, o_ref, m_ref, l_ref, acc_ref = refs[2 * G * nb:]
    b, n = pl.program_id(0), pl.program_id(1)

    @pl.when(n == 0)
    def _():
        m_ref[...] = jnp.full(m_ref.shape, NEG, F32)
        l_ref[...] = jnp.zeros(l_ref.shape, F32)
        acc_ref[...] = jnp.zeros(acc_ref.shape, F32)

    row = lax.broadcasted_iota(jnp.int32, (SEL_BLOCK, 1), 0)
    lane = lax.broadcasted_iota(jnp.int32, (1, SEL_BLOCK), 1)
    key_row = jnp.where(lane < SEL_BLOCK // 2, 2 * lane, 2 * lane - (SEL_BLOCK - 1))
    head_rows = lambda ref, g: jnp.concatenate([ref[0, :, g, :], ref[0, :, G + g, :]], axis=0)
    for g, u in [(g, u) for g in range(G) for u in range(nb)]:
        bid = ids_ref[(b * G + g) * N_SEL + n * nb + u]
        is_new = bid >= n_past
        first = (row == 0) & (bid == n_past)
        sl = slice(g * HEAD_DIM, (g + 1) * HEAD_DIM)
        k = jnp.where(is_new, jnp.where(first, kn_ref[0, :, sl], 0.0),
                      head_rows(kc_refs[g * nb + u], g)).astype(BF16)
        v = jnp.where(is_new, jnp.where(first, vn_ref[0, :, sl], 0.0),
                      head_rows(vc_refs[g * nb + u], g)).astype(BF16)
        kpos = bid * SEL_BLOCK + key_row
        ok = kpos <= pos
        s = jnp.where(ok, _dot_nt(qr_ref[0, g], k) * SCALE, NEG)
        m_prev = m_ref[g]
        m_new = jnp.maximum(m_prev, jnp.max(s, axis=1, keepdims=True))
        alpha = jnp.exp(m_prev - m_new)
        p = jnp.where(ok, jnp.exp(s - m_new), 0.0)
        l_ref[g] = alpha * l_ref[g] + jnp.sum(p, axis=1, keepdims=True)
        acc_ref[g] = alpha * acc_ref[g] + _dot(p.astype(BF16), v)
        m_ref[g] = m_new

    @pl.when(n == N_SEL // nb - 1)
    def _():
        for g in range(G):
            gt = gate_ref[0, g]
            o_sel = acc_ref[g] * (1.0 / l_ref[g])
            o_ref[0, g] = gt[:, 0:1] * ocmp_ref[0, g] + gt[:, 1:2] * o_sel + gt[:, 2:3] * owin_ref[0, g]


def attn_dec_sel(table, ids, qr, k_cache, v_cache, k_new, v_new, ocmp, owin, gates, *, pos):
    B, G, R, _ = qr.shape
    page = k_cache.shape[1]
    bpp = page // SEL_BLOCK
    n_past = table.shape[1] * bpp
    kc = k_cache.reshape(k_cache.shape[0] * bpp, SEL_BLOCK // 2, 2 * G, HEAD_DIM)
    vc = v_cache.reshape(v_cache.shape[0] * bpp, SEL_BLOCK // 2, 2 * G, HEAD_DIM)

    nb = DEC_BLOCKS_PER_STEP

    def cache_spec(g, u):
        def index(b, n, pt, ids_):
            bid = jnp.minimum(ids_[(b * G + g) * N_SEL + n * nb + u], n_past - 1)
            return (pt[b, bid // bpp] * bpp + bid % bpp, 0, 0, 0)
        return pl.BlockSpec((1, SEL_BLOCK // 2, 2 * G, HEAD_DIM), index)

    q_spec = pl.BlockSpec((1, G, R, HEAD_DIM), lambda b, n, pt, ids_: (b, 0, 0, 0))
    n_spec = pl.BlockSpec((1, 1, G * HEAD_DIM), lambda b, n, pt, ids_: (b, 0, 0))
    g_spec = pl.BlockSpec((1, G, R, LANE), lambda b, n, pt, ids_: (b, 0, 0, 0))
    c_specs = [cache_spec(g, u) for g in range(G) for u in range(nb)]
    return pl.pallas_call(
        functools.partial(_attn_dec_sel_kernel, n_past=n_past, pos=pos),
        grid_spec=pltpu.PrefetchScalarGridSpec(
            num_scalar_prefetch=2,
            grid=(B, N_SEL // nb),
            in_specs=[q_spec] + c_specs + c_specs + [n_spec, n_spec, q_spec, q_spec, g_spec],
            out_specs=q_spec,
            scratch_shapes=[pltpu.VMEM((G, R, 1), F32), pltpu.VMEM((G, R, 1), F32),
                            pltpu.VMEM((G, R, HEAD_DIM), F32)],
        ),
        out_shape=jax.ShapeDtypeStruct((B, G, R, HEAD_DIM), F32),
        compiler_params=_params(("parallel", "arbitrary")),
        name="attn_dec_sel",
    )(table, ids.reshape(-1), qr, *([kc] * (G * nb)), *([vc] * (G * nb)), k_new, v_new, ocmp, owin, gates)


def _rope_tables(pos):
    half = HEAD_DIM // 2
    inv = ROPE_THETA ** (-jnp.arange(half, dtype=F32) / half)
    ang = pos.astype(F32)[:, None] * inv[None, :]
    cos, sin = jnp.cos(ang), jnp.sin(ang)
    return jnp.concatenate([cos, cos], axis=1), jnp.concatenate([-sin, sin], axis=1)


def _prep_weights(W):
    D = W['w_kv'].shape[0]
    H = D // HEAD_DIM
    R = H // N_KV_HEADS
    pg = D // len(POOL_WINDOWS)
    half = CMP_STRIDE * HEAD_DIM
    P = {}
    P['w_pool'] = W['w_pool'].astype(BF16).reshape(-1, len(POOL_WINDOWS) * pg, pg)
    kv_gain = jnp.ones((W['w_kv'].shape[1], KV_W), F32)
    kv_gain = kv_gain.at[2].set(jnp.tile(W['g_k_sel'], N_KV_HEADS)).at[4].set(jnp.tile(W['g_k_win'], N_KV_HEADS))
    P['kv_gain'] = kv_gain
    for t in ('k', 'v'):
        w1 = W['w_cmp_%s1' % t].astype(BF16)
        P['w_cmp_%s1' % t] = w1
        P['w_cmp_%scat' % t] = jnp.concatenate([w1[:half], w1[half:]], axis=1)
        P['w_cmp_%s2' % t] = W['w_cmp_%s2' % t].astype(BF16)
        P['pe_%s' % t] = W['pe_cmp_%s' % t].astype(BF16).reshape(1, -1)
    n_b = W['w_qg'].shape[0]
    wg = W['w_qg'][:, :, H * HEAD_DIM:].astype(BF16).reshape(n_b, D, N_KV_HEADS, R * N_BRANCH)
    wg = jnp.pad(wg, ((0, 0), (0, 0), (0, 0), (0, LANE - R * N_BRANCH)))
    P['w_gate'] = wg.reshape(n_b, D, N_KV_HEADS * LANE)
    P['w_ple'] = W['w_ple'].astype(BF16)
    P[('w_up', 0)] = W['w_up'][0].astype(BF16)
    return P


def _dense(x, wname, layer, W, P, *, N, epilogue, specs, extras=(), out_dtypes, emit=False, side_cast=None, row_scale_in=None,
           norm_gains=None, name):
    M, K = x.shape
    shapes = [d if isinstance(d, jax.ShapeDtypeStruct) else jax.ShapeDtypeStruct((M, N), d) for d in out_dtypes]
    key = (wname, layer)
    tmx = 1024 if M >= 1024 else M
    use_mm = key in P
    if use_mm:
        grid, tm, tn, tk = _tiles(M, N, K, tmx, 1024, 2048 if (M >= 1024 and K > 4096) else 4096)
    else:
        grid, tm, tn, tk = _tiles(M, N, K, tmx, 512, K if K <= 4096 else 2048)
    extra_specs, out_specs = specs(tm, tn)
    extras = list(extras)
    if row_scale_in is not None:
        epilogue = _row_scaled(epilogue)
        extras.insert(0, row_scale_in)
        extra_specs = [pl.BlockSpec((tm, LANE), lambda i, j, k: (i, 0))] + list(extra_specs)
    if norm_gains:
        epilogue = _norm_producer(epilogue, len(extras))
        n_ex, n_ex_specs, n_shapes, n_specs = _norm_io(norm_gains, M, N, grid[1], tm, tn)
        extras, extra_specs = extras + n_ex, list(extra_specs) + n_ex_specs
        shapes, out_specs = shapes + n_shapes, list(out_specs) + n_specs
    side = None if side_cast is None else (W[side_cast[0]], side_cast[1])
    if use_mm:
        outs = list(mm(x, P[key], grid=grid, tm=tm, tn=tn, tk=tk, epilogue=epilogue, extras=extras,
                       extra_specs=extra_specs, out_shapes=shapes, out_specs=out_specs, side=side, name=name))
    else:
        w = W[wname]
        outs = list(mm_ws(x, w, layer=layer if w.ndim == 3 else None, emit=emit, side=side, grid=grid, tm=tm,
                          tn=tn, tk=tk, epilogue=epilogue, extras=extras, extra_specs=extra_specs,
                          out_shapes=shapes, out_specs=out_specs, name=name))
    if side is not None:
        P[side_cast] = outs.pop()
    if emit and not use_mm:
        P[key] = outs.pop()
    if norm_gains:
        outs.append(row_scale(outs.pop(), N))
    return outs


def _ffn_ple(h, normed, p_l, layer, W, P, next_gains):
    M, D = h.shape
    F = W['w_up'].shape[2]
    host = M >= 1024
    next_up = ('w_up', layer + 1) if host and layer + 1 < W['w_up'].shape[0] else None
    mn = lambda tm, tn: ([], [_mn_spec(tm, tn)])
    res = lambda tm, tn: ([_mn_spec(tm, tn)], [_mn_spec(tm, tn)])
    y, rs = normed
    (u,) = _dense(y, 'w_up', layer, W, P, N=F, epilogue=lambda acc, rows: (jnp.square(jnp.maximum(acc, 0.0)),),
                  specs=mn, out_dtypes=[BF16], side_cast=('w_down', layer) if host else None, row_scale_in=rs,
                  name="ffn_up")
    h, y, rs = _dense(u, 'w_down', layer, W, P, N=D, epilogue=lambda acc, rows, r: (r[rows, :] + acc,), specs=res,
                      extras=[h], out_dtypes=[F32], norm_gains=[W['g_ple'][layer]], side_cast=next_up,
                      name="ffn_down")
    ple_dim = p_l.shape[1]
    ple_specs = lambda tm, tn: ([_mn_spec(tm, tn), pl.BlockSpec((tm, ple_dim), lambda i, j, k: (i, 0)),
                                 pl.BlockSpec((ple_dim, tn), lambda i, j, k: (0, j))], [_mn_spec(tm, tn)])
    return _dense(y, 'w_ple_gate', layer, W, P, N=D,
                  epilogue=lambda acc, rows, r, pp, wp: (r[rows, :] + _dot(pp[rows, :], wp[...]) * _sigmoid(acc),),
                  specs=ple_specs, extras=[h, p_l.astype(BF16), P['w_ple'][layer]], out_dtypes=[F32],
                  row_scale_in=rs, norm_gains=next_gains, name="ple")


def _kv_epilogue(acc, rows, gain_ref, cos_ref, sin_ref):
    j = pl.program_id(0)
    cosf, sinf = cos_ref[rows, :], sin_ref[rows, :]
    heads = []
    for hh in range(N_KV_HEADS):
        sl = slice(hh * HEAD_DIM, (hh + 1) * HEAD_DIM)
        heads.append(_rope(_head_norm(acc[:, sl], gain_ref[0, :, sl]), cosf, sinf))
    out = jnp.where((j == 2) | (j == 4), jnp.concatenate(heads, axis=1), acc)
    return out, out


def _q_epilogue(acc, rows, gq_ref, cos_ref, sin_ref):
    cosf, sinf = cos_ref[rows, :], sin_ref[rows, :]
    qs, qrs = [], []
    for hh in range(acc.shape[1] // HEAD_DIM):
        qn = _head_norm(acc[:, hh * HEAD_DIM:(hh + 1) * HEAD_DIM], gq_ref[...])
        qs.append(qn)
        qrs.append(_rope(qn, cosf, sinf))
    return jnp.concatenate(qs, axis=1), jnp.concatenate(qrs, axis=1)


def _trunk(x, p, pool_prefix, pos0, W, P, attend):
    B, T, D = x.shape
    M = B * T
    tmx = 1024 if M >= 1024 else M
    h = x.reshape(M, D)
    pg = D // len(POOL_WINDOWS)

    d, pool_new = pool_diff(x, W['g_mix'][0], pool_prefix[0], pos0)
    d, pool_new = d.reshape(M, D), pool_new[None]
    grid, tm, tn, tk = _tiles(M, D, pg, tmx, pg, pg)
    n_ex, n_ex_specs, n_shapes, n_specs = _norm_io([W['g_ffn'][0]], M, D, grid[1], tm, tn)
    h, y, ssq = mm(d, P['w_pool'][0], grid=grid, tm=tm, tn=tn, tk=tk,
                   x_map=lambda i, j, k: (i, j), w_map=lambda i, j, k: (j, 0),
                   epilogue=_norm_producer(lambda acc, rows, sc, r: (r[rows, :] + acc * sc[...],), 2),
                   extras=[W['pool_scale'][0].reshape(1, D), h] + n_ex,
                   extra_specs=[pl.BlockSpec((1, tn), lambda i, j, k: (0, j)), _mn_spec(tm, tn)] + n_ex_specs,
                   out_shapes=[jax.ShapeDtypeStruct((M, D), F32)] + n_shapes,
                   out_specs=[_mn_spec(tm, tn)] + n_specs, name="pool_mix")
    h, hkv, a1, rs = _ffn_ple(h, (y, row_scale(ssq, D)), p[0].reshape(M, -1), 0, W, P,
                              [W['g_kv'], W['g_mix'][1]])

    pos = pos0 + jnp.tile(jnp.arange(T, dtype=jnp.int32), B)
    cosf, sinf = _rope_tables(pos)
    n_kv = W['w_kv'].shape[1]
    rope_spec = lambda tm: pl.BlockSpec((tm, HEAD_DIM), lambda i, j, k: (i, 0))
    kv_spec = lambda tm: pl.BlockSpec((1, tm, KV_W), lambda i, j, k: (j, i, 0))
    kv4_spec = lambda tm: pl.BlockSpec((1, tm, N_KV_HEADS, HEAD_DIM), lambda i, j, k: (j, i, 0, 0))
    kv_specs = lambda tm, tn: ([pl.BlockSpec((1, 1, KV_W), lambda i, j, k: (j, 0, 0)), rope_spec(tm), rope_spec(tm)],
                               [kv4_spec(tm), kv_spec(tm)])
    kv, kv_b = _dense(hkv, 'w_kv2d', None, W, P, N=n_kv * KV_W, epilogue=_kv_epilogue, specs=kv_specs,
                      extras=[P['kv_gain'].reshape(n_kv, 1, KV_W), cosf, sinf],
                      out_dtypes=[jax.ShapeDtypeStruct((n_kv, M, N_KV_HEADS, HEAD_DIM), F32),
                                  jax.ShapeDtypeStruct((n_kv, M, KV_W), BF16)], row_scale_in=rs, name="kv_proj")
    q_specs = lambda tm, tn: ([pl.BlockSpec((1, HEAD_DIM), lambda i, j, k: (0, 0)), rope_spec(tm), rope_spec(tm)],
                              [_mn_spec(tm, tn)] * 2)
    q, qr = _dense(a1, 'w_qg', 0, W, P, N=D, epilogue=_q_epilogue, specs=q_specs,
                   extras=[W['g_q'][0].reshape(1, HEAD_DIM), cosf, sinf], out_dtypes=[BF16, BF16], row_scale_in=rs,
                   name="q_proj")
    ng = N_KV_HEADS * LANE
    grid, tm, tn, tk = _tiles(M, ng, D, tmx, ng, D)
    (gates,) = mm(a1, P['w_gate'][0], grid=grid, tm=tm, tn=tn, tk=tk,
                  epilogue=_row_scaled(lambda acc, rows: (_sigmoid(acc),)), extras=[rs],
                  extra_specs=[pl.BlockSpec((tm, LANE), lambda i, j, k: (i, 0))],
                  out_shapes=[jax.ShapeDtypeStruct((M, ng), F32)], out_specs=[_mn_spec(tm, tn)], name="gate_proj")

    o, win_state = attend(kv, kv_b, q, qr, gates)

    h, y, rs = _dense(o, 'w_o', 0, W, P, N=D, epilogue=lambda acc, rows, r: (r[rows, :] + acc,), extras=[h],
                      specs=lambda tm, tn: ([_mn_spec(tm, tn)], [_mn_spec(tm, tn)]), out_dtypes=[F32],
                      norm_gains=[W['g_ffn'][1]], name="attn_out")
    (h,) = _ffn_ple(h, (y, rs), p[1].reshape(M, -1), 1, W, P, [])
    rows = tuple(kv[n].reshape(B, T, N_KV_HEADS, HEAD_DIM) for n in range(4))
    return h.reshape(B, T, D), pool_new, rows, win_state


def _compress(k_pages, v_pages, table, cp, W, P):
    abk, abv = cmp_partials(k_pages, v_pages, table, P['w_cmp_kcat'], P['w_cmp_vcat'], cp)
    gain = W['g_k_cmp'].reshape(1, HEAD_DIM)
    ck = cmp_finish(abk, P['pe_k'], P['w_cmp_k1'], P['w_cmp_k2'], gain, norm=True)
    cv = cmp_finish(abv, P['pe_v'], P['w_cmp_v1'], P['w_cmp_v2'], gain, norm=False)
    return ck, cv


def kernel(x_prompt, x_sample, state_pool, cache_k_cmp, cache_v_cmp, cache_k_sel, cache_v_sel, state_k_win, state_v_win, page_table, p_prompt, p_sample, g_mix, w_pool, pool_scale, g_kv, w_kv, g_k_cmp, g_k_sel, g_k_win, w_cmp_k1, w_cmp_k2, pe_cmp_k, w_cmp_v1, w_cmp_v2, pe_cmp_v, w_qg, g_q, w_o, g_ffn, w_up, w_down, g_ple, w_ple, w_ple_gate):
    W = dict(g_mix=g_mix, w_pool=w_pool, pool_scale=pool_scale, g_kv=g_kv, w_kv=w_kv, g_k_cmp=g_k_cmp,
             g_k_sel=g_k_sel, g_k_win=g_k_win, w_cmp_k1=w_cmp_k1, w_cmp_k2=w_cmp_k2, pe_cmp_k=pe_cmp_k,
             w_cmp_v1=w_cmp_v1, w_cmp_v2=w_cmp_v2, pe_cmp_v=pe_cmp_v, w_qg=w_qg, g_q=g_q, w_o=w_o,
             g_ffn=g_ffn, w_up=w_up, w_down=w_down, g_ple=g_ple, w_ple=w_ple, w_ple_gate=w_ple_gate)
    P = _prep_weights(W)
    W['w_kv2d'] = w_kv.reshape(w_kv.shape[0], -1)
    Bp, Tp, D = x_prompt.shape
    Bs, Ts, _ = x_sample.shape
    assert Ts == 1, "the decode path handles one new token per sequence"
    page = cache_k_cmp.shape[1]
    past_len = page_table.shape[1] * page
    R = D // HEAD_DIM // N_KV_HEADS
    assert Tp % page == 0 and past_len % SEL_BLOCK == 0

    def attend_prompt(kv, kv_b, q, qr, gates):
        ppb = Tp // page
        table = jnp.arange(Bp * ppb, dtype=jnp.int32).reshape(Bp, ppb)
        pages = lambda a: a.reshape(-1, page, N_KV_HEADS, HEAD_DIM)
        ck, cv = _compress(pages(kv[0]), pages(kv[1]), table, ppb, W, P)
        nc = Tp // CMP_STRIDE - CMP_BLOCK // CMP_STRIDE + 1
        seq = lambda a: a.reshape(Bp, Tp, -1)
        o = attn_prompt(seq(q), seq(qr), ck, cv, kv_b.reshape(-1, Bp, Tp, KV_W), seq(gates), nc=nc)
        nw = min(WINDOW, Tp)
        win = tuple(kv[n].reshape(Bp, Tp, N_KV_HEADS, HEAD_DIM)[:, -nw:] for n in (4, 5))
        return o.reshape(Bp * Tp, D), win

    def attend_sample(kv, kv_b, q, qr, gates):
        del kv_b
        ck, cv = _compress(cache_k_cmp, cache_v_cmp, page_table, min(32, page_table.shape[1]), W, P)
        nc = (past_len - (CMP_BLOCK - 1)) // CMP_STRIDE + 1
        ns = past_len // SEL_BLOCK + 1
        wb = state_k_win.shape[1]
        new_row = lambda a: a.reshape(Bs, 1, N_KV_HEADS, HEAD_DIM)
        kw = jnp.concatenate([state_k_win, new_row(kv[4])], axis=1)[:, -wb:]
        vw = jnp.concatenate([state_v_win, new_row(kv[5])], axis=1)[:, -wb:]
        heads = lambda a: a.reshape(Bs, N_KV_HEADS, R, HEAD_DIM)
        ocmp, owin, ids = attn_dec_dense(heads(q), heads(qr), ck, cv, kw, vw, nc=nc, ns=ns, pos=past_len)
        gt = gates.reshape(Bs, N_KV_HEADS, LANE)[:, :, :R * N_BRANCH].reshape(Bs, N_KV_HEADS, R, N_BRANCH)
        gt = jnp.pad(gt, ((0, 0), (0, 0), (0, 0), (0, LANE - N_BRANCH)))
        o = attn_dec_sel(page_table, ids, heads(qr), cache_k_sel, cache_v_sel,
                         kv[2].reshape(Bs, 1, KV_W), kv[3].reshape(Bs, 1, KV_W), ocmp, owin, gt, pos=past_len)
        return o.reshape(Bs, D).astype(BF16), (kw, vw)

    pool_zero = jnp.zeros((state_pool.shape[0], Bp, POOL_STATE, D), x_prompt.dtype)
    y_p, pool_p, rows_p, win_p = _trunk(x_prompt, p_prompt, pool_zero, 0, W, P, attend_prompt)
    y_s, pool_s, rows_s, win_s = _trunk(x_sample, p_sample, state_pool, past_len, W, P, attend_sample)
    return (y_p, y_s, pool_p, pool_s, rows_p[0], rows_p[1], rows_p[2], rows_p[3], win_p[0], win_p[1],
            rows_s[0], rows_s[1], rows_s[2], rows_s[3], win_s[0], win_s[1])
```

```python
import functools

import jax
import jax.numpy as jnp
import numpy as np
from jax import lax
from jax.experimental import pallas as pl
from jax.experimental.pallas import tpu as pltpu

F32 = jnp.float32
BF16 = jnp.bfloat16

POOL_WINDOWS = (2, 4, 8, 16)
POOL_STATE = max(POOL_WINDOWS) - 1
POOL_PAD = POOL_STATE + 1
HEAD_DIM = 128
N_KV_HEADS = 4
N_BRANCH = 3
CMP_BLOCK = 32
CMP_STRIDE = 16
CMP_HIDDEN = 2 * HEAD_DIM
SEL_BLOCK = 64
N_SEL = 16
WINDOW = 512
ROPE_THETA = 10000.0
EPS = 1e-6
SCALE = HEAD_DIM ** -0.5
NEG = -1e30
FORCE = 1e9
PAD_SCORE = -3e38
KV_W = N_KV_HEADS * HEAD_DIM
LANE = 128
VMEM_LIMIT = 56 * 1024 * 1024


def _params(sem):
    return pltpu.CompilerParams(dimension_semantics=sem, vmem_limit_bytes=VMEM_LIMIT)


def _sigmoid(x):
    return 1.0 / (1.0 + jnp.exp(-x))


def _dot(a, b):
    return jnp.dot(a, b, preferred_element_type=F32)


def _dot_nt(a, b):
    return lax.dot_general(a, b, (((1,), (1,)), ((), ())), preferred_element_type=F32)


def _dot_tn(a, b):
    return lax.dot_general(a, b, (((0,), (0,)), ((), ())), preferred_element_type=F32)


def _head_norm(x, g):
    return x * lax.rsqrt(jnp.mean(x * x, axis=-1, keepdims=True) + EPS) * g


def _rope(x, cosf, sinf):
    return x * cosf + pltpu.roll(x, HEAD_DIM // 2, 1) * sinf


def _fold_lanes(x):
    parts = [x[:, c * LANE:(c + 1) * LANE] for c in range(x.shape[1] // LANE)]
    return functools.reduce(lambda u, v: u + v, parts)


def _norm_producer(epilogue, n_base):
    def wrapped(acc, rows, *extras):
        (h,) = epilogue(acc, rows, *extras[:n_base])
        return (h, *[(h * g[...]).astype(BF16) for g in extras[n_base:]], _fold_lanes(h * h))
    return wrapped


def _row_scaled(epilogue):
    def wrapped(acc, rows, rs_ref, *extras):
        rs = rs_ref[rows, :]
        acc = jnp.concatenate([acc[:, c * LANE:(c + 1) * LANE] * rs for c in range(acc.shape[1] // LANE)], axis=1)
        return epilogue(acc, rows, *extras)
    return wrapped


def _row_scale_kernel(ssq_ref, o_ref, *, d):
    tot = jnp.sum(functools.reduce(lambda u, v: u + v, [ssq_ref[j] for j in range(ssq_ref.shape[0])]),
                  axis=1, keepdims=True)
    o_ref[...] = jnp.broadcast_to(lax.rsqrt(tot / d + EPS), o_ref.shape)


def row_scale(ssq, d):
    gn, M, _ = ssq.shape
    tm = min(M, 1024)
    return pl.pallas_call(
        functools.partial(_row_scale_kernel, d=d),
        grid=(M // tm,),
        in_specs=[pl.BlockSpec((gn, tm, LANE), lambda i: (0, i, 0))],
        out_specs=pl.BlockSpec((tm, LANE), lambda i: (i, 0)),
        out_shape=jax.ShapeDtypeStruct((M, LANE), F32),
        compiler_params=_params(("parallel",)),
        name="row_scale",
    )(ssq)


def _norm_io(gains, M, N, gn, tm, tn):
    extras = [g.reshape(1, N).astype(F32) for g in gains]
    extra_specs = [pl.BlockSpec((1, tn), lambda i, j, k: (0, j)) for _ in gains]
    shapes = [jax.ShapeDtypeStruct((M, N), BF16) for _ in gains] + [jax.ShapeDtypeStruct((gn, M, LANE), F32)]
    specs = [_mn_spec(tm, tn) for _ in gains] + [pl.BlockSpec((1, tm, LANE), lambda i, j, k: (j, i, 0))]
    return extras, extra_specs, shapes, specs


EPILOGUE_ROWS = 256


def _finish_rows(acc_of, tm, outs, extras, epilogue):
    ch = min(tm, EPILOGUE_ROWS)
    for c in range(tm // ch):
        rows = slice(c * ch, (c + 1) * ch)
        for o_ref, r in zip(outs, epilogue(acc_of(rows), rows, *extras)):
            if len(o_ref.shape) == 4:
                for hh in range(o_ref.shape[2]):
                    o_ref[0, rows, hh, :] = r[:, hh * HEAD_DIM:(hh + 1) * HEAD_DIM].astype(o_ref.dtype)
            elif len(o_ref.shape) == 3:
                o_ref[0, rows, :] = r.astype(o_ref.dtype)
            else:
                o_ref[rows, :] = r.astype(o_ref.dtype)


def _k_steps(x_ref, w, acc_ref, nk, outs, extras, epilogue):
    k = pl.program_id(2)

    @pl.when(k == 0)
    def _():
        acc_ref[...] = _dot(x_ref[...], w())

    @pl.when((k > 0) & (k < nk - 1))
    def _():
        acc_ref[...] += _dot(x_ref[...], w())

    @pl.when(k == nk - 1)
    def _():
        _finish_rows(lambda rows: acc_ref[rows, :] + _dot(x_ref[rows, :], w()), x_ref.shape[0], outs, extras,
                     epilogue)


def _side_io(side, n_steps, step):
    s_arr, s_layer = side
    _, rows, cols = s_arr.shape
    rs = rows // n_steps
    assert rs * n_steps == rows and rs % 16 == 0
    in_spec = pl.BlockSpec((None, rs, cols), lambda *g: (s_layer, step(*g), 0))
    out_spec = pl.BlockSpec((rs, cols), lambda *g: (step(*g), 0))
    return s_arr, in_spec, jax.ShapeDtypeStruct((rows, cols), BF16), out_spec


def _mm_kernel(*refs, nk, n_extra, n_out, epilogue, side):
    x_ref, w_ref = refs[0], refs[1]
    extras = refs[2:2 + n_extra]
    n_in = 2 + n_extra + side
    outs = refs[n_in:n_in + n_out]
    tm = x_ref.shape[0]
    if side:
        refs[n_in + n_out][...] = refs[n_in - 1][...].astype(BF16)

    if nk == 1:
        _finish_rows(lambda rows: _dot(x_ref[rows, :], w_ref[...]), tm, outs, extras, epilogue)
    else:
        _k_steps(x_ref, lambda: w_ref[...], refs[-1], nk, outs, extras, epilogue)


def mm(x, w, *, grid, tm, tn, tk, epilogue, extras=(), extra_specs=(), out_shapes, out_specs,
       x_map=None, w_map=None, side=None, name):
    gm, gn, nk = grid
    x_map = x_map or (lambda i, j, k: (i, k))
    w_map = w_map or (lambda i, j, k: (k, j))
    ins, in_specs = [x, w, *extras], [pl.BlockSpec((tm, tk), x_map), pl.BlockSpec((tk, tn), w_map)]
    in_specs += list(extra_specs)
    n_out, out_shapes, out_specs = len(out_shapes), list(out_shapes), list(out_specs)
    if side is not None:
        s_arr, s_in, s_shape, s_out = _side_io(side, gm * gn * nk, lambda i, j, k: (i * gn + j) * nk + k)
        ins.append(s_arr), in_specs.append(s_in), out_shapes.append(s_shape), out_specs.append(s_out)
    kern = functools.partial(_mm_kernel, nk=nk, n_extra=len(extras), n_out=n_out, epilogue=epilogue,
                             side=side is not None)
    return pl.pallas_call(
        kern,
        grid=grid,
        in_specs=in_specs,
        out_specs=out_specs,
        out_shape=out_shapes,
        scratch_shapes=[pltpu.VMEM((tm, tn), F32)] if nk > 1 else [],
        compiler_params=_params(("parallel", "parallel", "arbitrary")),
        name=name,
    )(*ins)


def _mm_ws_kernel(*refs, nk, n_extra, n_out, epilogue, emit, side):
    x_ref, w_ref = refs[0], refs[1]
    extras = refs[2:2 + n_extra]
    n_in = 2 + n_extra + side
    outs = refs[n_in:n_in + n_out]
    rest = refs[n_in + n_out:]
    wb_ref = rest[emit + side]
    i, k = pl.program_id(1), pl.program_id(2)

    @pl.when(i == 0)
    def _():
        wb_ref[k] = w_ref[...].astype(BF16)
        if emit:
            rest[0][...] = wb_ref[k]

    if side:
        rest[emit][...] = refs[n_in - 1][...].astype(BF16)

    tm = x_ref.shape[0]
    if nk == 1:
        _finish_rows(lambda rows: _dot(x_ref[rows, :], wb_ref[0]), tm, outs, extras, epilogue)
    else:
        _k_steps(x_ref, lambda: wb_ref[k], rest[-1], nk, outs, extras, epilogue)


def mm_ws(x, w, *, layer=None, emit=False, side=None, grid, tm, tn, tk, epilogue, extras=(), extra_specs=(),
          out_shapes, out_specs, name):
    gm, gn, nk = grid
    swap = lambda f: (lambda j, i, k: f(i, j, k))
    respec = lambda s: pl.BlockSpec(s.block_shape, swap(s.index_map))
    k_once = lambda i, k: jnp.where(i == 0, k, nk - 1)
    if layer is None:
        w_spec = pl.BlockSpec((tk, tn), lambda j, i, k: (k_once(i, k), j))
    else:
        w_spec = pl.BlockSpec((None, tk, tn), lambda j, i, k: (layer, k_once(i, k), j))
    n_out = len(out_shapes)
    ins, in_specs = [x, w, *extras], [pl.BlockSpec((tm, tk), lambda j, i, k: (i, k)), w_spec]
    in_specs += [respec(s) for s in extra_specs]
    out_shapes, out_specs = list(out_shapes), [respec(s) for s in out_specs]
    if emit:
        out_shapes.append(jax.ShapeDtypeStruct((nk * tk, gn * tn), BF16))
        out_specs.append(pl.BlockSpec((tk, tn), lambda j, i, k: (k_once(i, k), j)))
    if side is not None:
        s_arr, s_in, s_shape, s_out = _side_io(side, gm * gn * nk, lambda j, i, k: (j * gm + i) * nk + k)
        ins.append(s_arr), in_specs.append(s_in), out_shapes.append(s_shape), out_specs.append(s_out)
    kern = functools.partial(_mm_ws_kernel, nk=nk, n_extra=len(extras), n_out=n_out, epilogue=epilogue, emit=emit,
                             side=side is not None)
    return pl.pallas_call(
        kern,
        grid=(gn, gm, nk),
        in_specs=in_specs,
        out_specs=out_specs,
        out_shape=out_shapes,
        scratch_shapes=[pltpu.VMEM((nk, tk, tn), BF16)] + ([pltpu.VMEM((tm, tn), F32)] if nk > 1 else []),
        compiler_params=_params(("parallel", "arbitrary", "arbitrary")),
        name=name,
    )(*ins)


def _tiles(M, N, K, tm, tn, tk):
    tm, tn, tk = min(tm, M), min(tn, N), min(tk, K)
    return (M // tm, N // tn, K // tk), tm, tn, tk


def _mn_spec(tm, tn):
    return pl.BlockSpec((tm, tn), lambda i, j, k: (i, j))


def _pool_diff_kernel(*refs, tt, pos0, halo):
    x_ref, pre_ref, g_ref = refs[0], refs[1 + halo], refs[2 + halo]
    d_ref, st_ref, seq_ref = refs[3 + halo:]
    t = pl.program_id(1)
    norm = lambda x: x * lax.rsqrt(jnp.mean(x * x, axis=-1, keepdims=True) + EPS) * g_ref[...]
    a = norm(x_ref[0])
    seq_ref[POOL_PAD:POOL_PAD + tt, :] = a

    @pl.when(t == 0)
    def _():
        seq_ref[0:POOL_PAD, :] = pre_ref[0]

    if halo:
        @pl.when(t > 0)
        def _():
            seq_ref[0:POOL_PAD, :] = norm(refs[1][0])

    pos = pos0 + t * tt + lax.broadcasted_iota(jnp.int32, (tt, 1), 0)
    pg = a.shape[1] // len(POOL_WINDOWS)
    for g, w in enumerate(POOL_WINDOWS):
        cols = slice(g * pg, (g + 1) * pg)
        s = a[:, cols]
        for j in range(1, w):
            s = s + seq_ref[POOL_PAD - j:POOL_PAD - j + tt, cols]
        cnt = jnp.minimum(pos + 1, w).astype(F32)
        d_ref[0, :, cols] = (s / cnt - a[:, cols]).astype(d_ref.dtype)
    st_ref[0] = seq_ref[tt:tt + POOL_PAD, :]


def pool_diff(x, gain, prefix, pos0):
    B, T, D = x.shape
    tt = min(T, 256)
    halo = T > tt
    pre = jnp.concatenate([jnp.zeros((B, 1, D), F32), prefix], axis=1)
    hpt = tt // POOL_PAD
    in_specs = [pl.BlockSpec((1, tt, D), lambda b, t: (b, t, 0))]
    if halo:
        in_specs.append(pl.BlockSpec((1, POOL_PAD, D), lambda b, t: (b, jnp.maximum(t * hpt - 1, 0), 0)))
    in_specs += [pl.BlockSpec((1, POOL_PAD, D), lambda b, t: (b, 0, 0)), pl.BlockSpec((1, D), lambda b, t: (0, 0))]
    d, st = pl.pallas_call(
        functools.partial(_pool_diff_kernel, tt=tt, pos0=pos0, halo=halo),
        grid=(B, T // tt),
        in_specs=in_specs,
        out_specs=[pl.BlockSpec((1, tt, D), lambda b, t: (b, t, 0)),
                   pl.BlockSpec((1, POOL_PAD, D), lambda b, t: (b, 0, 0))],
        out_shape=[jax.ShapeDtypeStruct((B, T, D), BF16), jax.ShapeDtypeStruct((B, POOL_PAD, D), F32)],
        scratch_shapes=[pltpu.VMEM((POOL_PAD + tt, D), F32)],
        compiler_params=_params(("parallel", "arbitrary")),
        name="pool_diff",
    )(*([x, x] if halo else [x]), pre, gain.reshape(1, D).astype(F32))
    return d, st[:, 1:]


PAGES_PER_STEP = 8
CMP_FINISH_ROWS = 512


def _cmp_partial_kernel(pt_ref, *refs, steps, pps):
    del pt_ref
    k_pages, v_pages = refs[:pps], refs[pps:2 * pps]
    wk_ref, wv_ref, abk_ref, abv_ref, xk_ref, xv_ref = refs[2 * pps:]
    p = pl.program_id(2)
    G = N_KV_HEADS
    half = CMP_STRIDE // 2
    rows_per_page = (k_pages[0].shape[1] // half) * G
    for pages, x_ref in ((k_pages, xk_ref), (v_pages, xv_ref)):
        for q, page_ref in enumerate(pages):
            for sb in range(page_ref.shape[1] // half):
                row = q * rows_per_page + sb * G
                for r in range(CMP_STRIDE):
                    x_ref[p, row:row + G, r * HEAD_DIM:(r + 1) * HEAD_DIM] = (
                        page_ref[0, sb * half + r // 2, (r % 2) * G:(r % 2 + 1) * G, :])

    @pl.when(p == steps - 1)
    def _():
        for x_ref, w_ref, ab_ref in ((xk_ref, wk_ref, abk_ref), (xv_ref, wv_ref, abv_ref)):
            x = x_ref[...].reshape(ab_ref.shape[1], x_ref.shape[2])
            ab_ref[0] = _dot(x.astype(BF16), w_ref[...])


def cmp_partials(k_pages, v_pages, table, wk_cat, wv_cat, cp):
    B, ppb = table.shape
    page = k_pages.shape[1]
    pps = PAGES_PER_STEP
    rows_per_page = page // CMP_STRIDE * N_KV_HEADS
    nch, steps = ppb // cp, cp // pps
    m = cp * rows_per_page
    kdim = CMP_STRIDE * HEAD_DIM
    pair_rows = lambda a: a.reshape(a.shape[0], page // 2, 2 * N_KV_HEADS, HEAD_DIM)

    def page_spec(q):
        return pl.BlockSpec((1, page // 2, 2 * N_KV_HEADS, HEAD_DIM),
                            lambda b, c, p, pt: (pt[b, c * cp + p * pps + q], 0, 0, 0))

    page_specs = [page_spec(q) for q in range(pps)]
    w_spec = pl.BlockSpec((kdim, 2 * CMP_HIDDEN), lambda b, c, p, pt: (0, 0))
    out_spec = pl.BlockSpec((1, m, 2 * CMP_HIDDEN), lambda b, c, p, pt: (b, c, 0))
    out_shape = jax.ShapeDtypeStruct((B, ppb * rows_per_page, 2 * CMP_HIDDEN), F32)
    x_scratch = pltpu.VMEM((steps, pps * rows_per_page, kdim), F32)
    return pl.pallas_call(
        functools.partial(_cmp_partial_kernel, steps=steps, pps=pps),
        grid_spec=pltpu.PrefetchScalarGridSpec(
            num_scalar_prefetch=1,
            grid=(B, nch, steps),
            in_specs=page_specs + page_specs + [w_spec, w_spec],
            out_specs=[out_spec, out_spec],
            scratch_shapes=[x_scratch, x_scratch],
        ),
        out_shape=[out_shape, out_shape],
        compiler_params=_params(("parallel", "parallel", "arbitrary")),
        name="cmp_partials",
    )(table, *([pair_rows(k_pages)] * pps), *([pair_rows(v_pages)] * pps), wk_cat, wv_cat)


def _cmp_finish_kernel(ab_ref, pe_ref, w1_ref, w2_ref, g_ref, o_ref, out_ref, *, norm):
    n4 = ab_ref.shape[1]
    G = o_ref.shape[1]
    bias = _dot(jnp.broadcast_to(pe_ref[...], (8, pe_ref.shape[1])), w1_ref[...])[0:1, :]
    ch = min(n4, CMP_FINISH_ROWS)
    for c in range(n4 // ch):
        lo, hi = c * ch, (c + 1) * ch
        first = ab_ref[0, lo:hi, :CMP_HIDDEN]
        if hi + G <= n4:
            second = ab_ref[0, lo + G:hi + G, CMP_HIDDEN:]
        else:
            second = jnp.concatenate([ab_ref[0, lo + G:hi, CMP_HIDDEN:], ab_ref[0, hi - G:hi, CMP_HIDDEN:]], axis=0)
        pre = first + second + bias
        out = _dot((pre * _sigmoid(pre)).astype(BF16), w2_ref[...])
        if norm:
            out = _head_norm(out, g_ref[...])
        out_ref[lo:hi, :] = out
    for g in range(G):
        o_ref[0, g] = out_ref[pl.ds(g, n4 // G, stride=G), :].astype(o_ref.dtype)


def cmp_finish(ab, pe, w1, w2, gain, *, norm):
    B, n4, _ = ab.shape
    G = N_KV_HEADS
    full = lambda a: pl.BlockSpec(a.shape, lambda b: (0,) * a.ndim)
    args = (pe, w1, w2, gain)
    return pl.pallas_call(
        functools.partial(_cmp_finish_kernel, norm=norm),
        grid=(B,),
        in_specs=[pl.BlockSpec((1, n4, 2 * CMP_HIDDEN), lambda b: (b, 0, 0))] + [full(a) for a in args],
        out_specs=pl.BlockSpec((1, G, n4 // G, HEAD_DIM), lambda b: (b, 0, 0, 0)),
        out_shape=jax.ShapeDtypeStruct((B, G, n4 // G, HEAD_DIM), BF16),
        scratch_shapes=[pltpu.VMEM((n4, HEAD_DIM), F32)],
        compiler_params=_params(("parallel",)),
        name="cmp_finish",
    )(ab, *args)


def _select_blocks(score, blk, ns):
    rank = jnp.zeros(score.shape, jnp.int32)
    for j in range(ns):
        sj = score[j:j + 1, :]
        beats = (sj > score) | ((sj == score) & (j < blk))
        rank = rank + beats.astype(jnp.int32)
    return (rank < min(N_SEL, ns)) & (score > 0.5 * NEG)


def _attn_prompt_kernel(q_ref, qr_ref, ck_ref, cv_ref, ks_ref, vs_ref, kw_ref, vw_ref, gate_ref, mapT_ref,
                        expand_ref, o_ref, part_ref, sbias_ref, wbias_ref, m_ref, acc_ref, *, tq, kc, nc, ns, R):
    qi = pl.program_id(2)
    q0 = qi * tq
    pos = q0 + lax.broadcasted_iota(jnp.int32, (tq, 1), 0)
    ncp = ck_ref.shape[2]

    ck = ck_ref[0, 0]
    cv = cv_ref[0, 0]
    cidx = lax.broadcasted_iota(jnp.int32, (1, ncp), 1)
    ok_c = (cidx * CMP_STRIDE + CMP_BLOCK - 1 <= pos) & (cidx < nc)
    imp = jnp.zeros((tq, ncp), F32)
    for r in range(R):
        qh = q_ref[0, :, r * HEAD_DIM:(r + 1) * HEAD_DIM]
        s = jnp.where(ok_c, _dot_nt(qh, ck) * SCALE, NEG)
        e = jnp.exp(s - jnp.max(s, axis=1, keepdims=True))
        p = jnp.where(ok_c, e * (1.0 / jnp.sum(e, axis=1, keepdims=True)), 0.0)
        imp = imp + p
        part_ref[r] = gate_ref[0, :, r * N_BRANCH:r * N_BRANCH + 1] * _dot(p.astype(BF16), cv)

    nsp = mapT_ref.shape[0]
    p_slc = lax.dot_general(mapT_ref[...], imp, (((1,), (1,)), ((), ())), precision=lax.Precision.HIGHEST,
                            preferred_element_type=F32)
    blk = lax.broadcasted_iota(jnp.int32, (nsp, tq), 0)
    pos_l = q0 + lax.broadcasted_iota(jnp.int32, (nsp, tq), 1)
    cur = pos_l // SEL_BLOCK
    vis = blk * SEL_BLOCK <= pos_l
    forced = vis & ((blk == 0) | (blk == cur) | (blk == cur - 1))
    score = jnp.where(forced, FORCE, jnp.where(vis, p_slc, NEG))
    score = jnp.where(blk < ns, score, PAD_SCORE)
    sel = _select_blocks(score, blk, ns).astype(BF16)

    c_hi = (q0 + tq) // kc
    col = lax.broadcasted_iota(jnp.int32, (1, kc), 1)
    sel_keys = _dot_tn(sel, expand_ref[...])
    for c in range(sbias_ref.shape[0]):
        @pl.when(c < c_hi)
        def _(c=c):
            ok = (sel_keys[:, c * kc:(c + 1) * kc] > 0.5) & (c * kc + col <= pos)
            sbias_ref[c] = jnp.where(ok, 0.0, NEG)
    nwc = wbias_ref.shape[0]
    c_w0 = c_hi - nwc
    for d in range(nwc):
        kpos = (c_w0 + d) * kc + col
        wbias_ref[d] = jnp.where((kpos <= pos) & (kpos > pos - WINDOW), 0.0, NEG)

    def fold(t):
        return [t[:, i * LANE:(i + 1) * LANE] for i in range(kc // LANE)]

    ones_blk = jnp.ones((kc, LANE), BF16)

    def branch(k_ref, v_ref, c_lo, bias_of, gate_col):
        def logits(r, c, bias):
            k = k_ref[0, pl.ds(pl.multiple_of(c * kc, kc), kc), :]
            return _dot_nt(qr_ref[0, :, r * HEAD_DIM:(r + 1) * HEAD_DIM], k) * SCALE + bias

        m_ref[...] = jnp.full(m_ref.shape, NEG, F32)
        acc_ref[...] = jnp.zeros(acc_ref.shape, F32)

        def over_chunks(body):
            n = c_hi - c_lo

            def pair(i, carry):
                body(c_lo + 2 * i, carry)
                return body(c_lo + 2 * i + 1, carry)

            lax.fori_loop(0, n // 2, pair, 0)

            @pl.when(n % 2 == 1)
            def _():
                body(c_hi - 1, 0)

        def max_body(c, carry):
            bias = bias_of(c)
            for r in range(R):
                mx = m_ref[r]
                for part in fold(logits(r, c, bias)):
                    mx = jnp.maximum(mx, part)
                m_ref[r] = mx
            return carry

        over_chunks(max_body)
        for r in range(R):
            m_ref[r] = jnp.broadcast_to(jnp.max(m_ref[r], axis=1, keepdims=True), (tq, LANE))

        def sum_body(c, carry):
            bias = bias_of(c)
            v = jnp.concatenate([v_ref[0, pl.ds(pl.multiple_of(c * kc, kc), kc), :], ones_blk], axis=1)
            for r in range(R):
                t = logits(r, c, bias)
                m = m_ref[r]
                ps = [jnp.exp(part - m) for part in fold(t)]
                acc_ref[r] += _dot(jnp.concatenate(ps, axis=1).astype(BF16), v)
            return carry

        over_chunks(sum_body)
        for r in range(R):
            gate = gate_ref[0, :, r * N_BRANCH + gate_col:r * N_BRANCH + gate_col + 1]
            part_ref[r] += gate * (acc_ref[r, :, :HEAD_DIM] * (1.0 / acc_ref[r, :, HEAD_DIM:]))

    branch(ks_ref, vs_ref, 0, lambda c: sbias_ref[c], 1)
    branch(kw_ref, vw_ref, jnp.maximum(c_w0, 0), lambda c: wbias_ref[c - c_w0], 2)
    for r in range(R):
        o_ref[0, :, r * HEAD_DIM:(r + 1) * HEAD_DIM] = part_ref[r].astype(o_ref.dtype)


def _overlap_map(ncp, nsp, ns):
    ratio = CMP_BLOCK // CMP_STRIDE
    per_sel = SEL_BLOCK // CMP_STRIDE
    m = np.zeros((ncp, nsp), np.float32)
    for b in range(ns):
        for mm_ in range(per_sel):
            for n in range(ratio):
                j = per_sel * b + mm_ - n
                if 0 <= j < ncp:
                    m[j, b] += 1.0
    return m


def attn_prompt(q, qr, ck, cv, kvb, gates, *, nc):
    B, T, HD = q.shape
    G = N_KV_HEADS
    R = HD // HEAD_DIM // G
    tq = min(T, 256)
    kc = tq
    ns = T // SEL_BLOCK
    nsp = -(-ns // 8) * 8
    ncp = ck.shape[2]
    mapT = jnp.asarray(_overlap_map(ncp, nsp, ns).T)
    expand = jnp.asarray((np.arange(T)[None, :] // SEL_BLOCK == np.arange(nsp)[:, None]).astype(np.float32), BF16)
    q_spec = pl.BlockSpec((1, tq, R * HEAD_DIM), lambda b, g, i: (b, i, g))
    c_spec = pl.BlockSpec((1, 1, ncp, HEAD_DIM), lambda b, g, i: (b, g, 0, 0))
    kv_spec = lambda n: pl.BlockSpec((None, 1, T, HEAD_DIM), lambda b, g, i: (n, b, 0, g))
    return pl.pallas_call(
        functools.partial(_attn_prompt_kernel, tq=tq, kc=kc, nc=nc, ns=ns, R=R),
        grid=(B, G, T // tq),
        in_specs=[q_spec, q_spec, c_spec, c_spec, kv_spec(2), kv_spec(3), kv_spec(4), kv_spec(5),
                  pl.BlockSpec((1, tq, LANE), lambda b, g, i: (b, i, g)),
                  pl.BlockSpec((nsp, ncp), lambda b, g, i: (0, 0)),
                  pl.BlockSpec((nsp, T), lambda b, g, i: (0, 0))],
        out_specs=q_spec,
        out_shape=jax.ShapeDtypeStruct((B, T, HD), BF16),
        scratch_shapes=[pltpu.VMEM((R, tq, HEAD_DIM), F32), pltpu.VMEM((T // kc, tq, kc), F32),
                        pltpu.VMEM((min(WINDOW, T) // kc + tq // kc, tq, kc), F32)]
                       + [pltpu.VMEM((R, tq, HEAD_DIM), F32), pltpu.VMEM((R, tq, HEAD_DIM + LANE), F32)],
        compiler_params=_params(("parallel", "parallel", "arbitrary")),
        name="attn_prompt",
    )(q, qr, ck, cv, kvb, kvb, kvb, kvb, gates, mapT, expand)


def _attn_dec_dense_kernel(q_ref, qr_ref, ck_ref, cv_ref, kw_ref, vw_ref, map_ref, ocmp_ref, owin_ref, ids_ref,
                           *, nc, ns, pos):
    G, R = q_ref.shape[1], q_ref.shape[2]
    ncp = ck_ref.shape[2]
    nsl = map_ref.shape[1]
    cidx = lax.broadcasted_iota(jnp.int32, (1, ncp), 1)
    ok_c = (cidx * CMP_STRIDE + CMP_BLOCK - 1 <= pos) & (cidx < nc)
    blk_l = lax.broadcasted_iota(jnp.int32, (1, nsl), 1)
    cur = pos // SEL_BLOCK
    vis = blk_l * SEL_BLOCK <= pos
    forced = vis & ((blk_l == 0) | (blk_l == cur) | (blk_l == cur - 1))
    ii = lax.broadcasted_iota(jnp.int32, (nsl, nsl), 0)
    jj = lax.broadcasted_iota(jnp.int32, (nsl, nsl), 1)
    slot = lax.broadcasted_iota(jnp.int32, (nsl, LANE), 1).astype(F32)
    blk_s = lax.broadcasted_iota(jnp.int32, (nsl, LANE), 0).astype(F32)
    for g in range(G):
        s = jnp.where(ok_c, _dot_nt(q_ref[0, g], ck_ref[0, g]) * SCALE, NEG)
        e = jnp.exp(s - jnp.max(s, axis=1, keepdims=True))
        p = jnp.where(ok_c, e * (1.0 / jnp.sum(e, axis=1, keepdims=True)), 0.0)
        ocmp_ref[0, g] = _dot(p.astype(BF16), cv_ref[0, g])
        imp = jnp.broadcast_to(jnp.sum(p, axis=0, keepdims=True), (R, ncp))
        p_slc = jnp.dot(imp, map_ref[...], precision=lax.Precision.HIGHEST, preferred_element_type=F32)[0:1, :]
        score_l = jnp.where(forced, FORCE, jnp.where(vis, p_slc, NEG))
        score_l = jnp.where(blk_l < ns, score_l, PAD_SCORE)
        score_s = jnp.sum(jnp.where(ii == jj, score_l, 0.0), axis=1, keepdims=True)
        beats = (score_l > score_s) | ((score_l == score_s) & (jj < ii))
        rank = jnp.sum(beats.astype(F32), axis=1, keepdims=True)
        ids = jnp.sum(jnp.where(rank == slot, blk_s, 0.0), axis=0, keepdims=True)
        ids_ref[0, g] = ids[:, :N_SEL].astype(jnp.int32)
        kw = kw_ref[0, :, g, :].astype(BF16)
        vw = vw_ref[0, :, g, :].astype(BF16)
        s = _dot_nt(qr_ref[0, g], kw) * SCALE
        e = jnp.exp(s - jnp.max(s, axis=1, keepdims=True))
        p = e * (1.0 / jnp.sum(e, axis=1, keepdims=True))
        owin_ref[0, g] = _dot(p.astype(BF16), vw)


def attn_dec_dense(q, qr, ck, cv, kw, vw, *, nc, ns, pos):
    B, G, R, _ = q.shape
    ncp = ck.shape[2]
    wb = kw.shape[1]
    nsl = -(-ns // LANE) * LANE
    omap = jnp.asarray(_overlap_map(ncp, nsl, ns))
    q_spec = pl.BlockSpec((1, G, R, HEAD_DIM), lambda b: (b, 0, 0, 0))
    c_spec = pl.BlockSpec((1, G, ncp, HEAD_DIM), lambda b: (b, 0, 0, 0))
    w_spec = pl.BlockSpec((1, wb, G, HEAD_DIM), lambda b: (b, 0, 0, 0))
    return pl.pallas_call(
        functools.partial(_attn_dec_dense_kernel, nc=nc, ns=ns, pos=pos),
        grid=(B,),
        in_specs=[q_spec, q_spec, c_spec, c_spec, w_spec, w_spec, pl.BlockSpec((ncp, nsl), lambda b: (0, 0))],
        out_specs=[q_spec, q_spec, pl.BlockSpec((1, G, 1, N_SEL), lambda b: (b, 0, 0, 0))],
        out_shape=[jax.ShapeDtypeStruct((B, G, R, HEAD_DIM), F32), jax.ShapeDtypeStruct((B, G, R, HEAD_DIM), F32),
                   jax.ShapeDtypeStruct((B, G, 1, N_SEL), jnp.int32)],
        compiler_params=_params(("parallel",)),
        name="attn_dec_dense",
    )(q, qr, ck, cv, kw, vw, omap)


DEC_BLOCKS_PER_STEP = 4


def _attn_dec_sel_kernel(pt_ref, ids_ref, qr_ref, *refs, n_past, pos):
    del pt_ref
    G = qr_ref.shape[1]
    nb = DEC_BLOCKS_PER_STEP
    kc_refs, vc_refs = refs[:G * nb], refs[G * nb:2 * G * nb]
    kn_ref, vn_ref, ocmp_ref, owin_ref, gate_ref, o_ref, m_ref, l_ref, acc_ref = refs[2 * G * nb:]
    b, n = pl.program_id(0), pl.program_id(1)

    @pl.when(n == 0)
    def _():
        m_ref[...] = jnp.full(m_ref.shape, NEG, F32)
        l_ref[...] = jnp.zeros(l_ref.shape, F32)
        acc_ref[...] = jnp.zeros(acc_ref.shape, F32)

    row = lax.broadcasted_iota(jnp.int32, (SEL_BLOCK, 1), 0)
    lane = lax.broadcasted_iota(jnp.int32, (1, SEL_BLOCK), 1)
    key_row = jnp.where(lane < SEL_BLOCK // 2, 2 * lane, 2 * lane - (SEL_BLOCK - 1))
    head_rows = lambda ref, g: jnp.concatenate([ref[0, :, g, :], ref[0, :, G + g, :]], axis=0)
    for g, u in [(g, u) for g in range(G) for u in range(nb)]:
        bid = ids_ref[(b * G + g) * N_SEL + n * nb + u]
        is_new = bid >= n_past
        first = (row == 0) & (bid == n_past)
        sl = slice(g * HEAD_DIM, (g + 1) * HEAD_DIM)
        k = jnp.where(is_new, jnp.where(first, kn_ref[0, :, sl], 0.0),
                      head_rows(kc_refs[g * nb + u], g)).astype(BF16)
        v = jnp.where(is_new, jnp.where(first, vn_ref[0, :, sl], 0.0),
                      head_rows(vc_refs[g * nb + u], g)).astype(BF16)
        kpos = bid * SEL_BLOCK + key_row
        ok = kpos <= pos
        s = jnp.where(ok, _dot_nt(qr_ref[0, g], k) * SCALE, NEG)
        m_prev = m_ref[g]
        m_new = jnp.maximum(m_prev, jnp.max(s, axis=1, keepdims=True))
        alpha = jnp.exp(m_prev - m_new)
        p = jnp.where(ok, jnp.exp(s - m_new), 0.0)
        l_ref[g] = alpha * l_ref[g] + jnp.sum(p, axis=1, keepdims=True)
        acc_ref[g] = alpha * acc_ref[g] + _dot(p.astype(BF16), v)
        m_ref[g] = m_new

    @pl.when(n == N_SEL // nb - 1)
    def _():
        for g in range(G):
            gt = gate_ref[0, g]
            o_sel = acc_ref[g] * (1.0 / l_ref[g])
            o_ref[0, g] = gt[:, 0:1] * ocmp_ref[0, g] + gt[:, 1:2] * o_sel + gt[:, 2:3] * owin_ref[0, g]


def attn_dec_sel(table, ids, qr, k_cache, v_cache, k_new, v_new, ocmp, owin, gates, *, pos):
    B, G, R, _ = qr.shape
    page = k_cache.shape[1]
    bpp = page // SEL_BLOCK
    n_past = table.shape[1] * bpp
    kc = k_cache.reshape(k_cache.shape[0] * bpp, SEL_BLOCK // 2, 2 * G, HEAD_DIM)
    vc = v_cache.reshape(v_cache.shape[0] * bpp, SEL_BLOCK // 2, 2 * G, HEAD_DIM)

    nb = DEC_BLOCKS_PER_STEP

    def cache_spec(g, u):
        def index(b, n, pt, ids_):
            bid = jnp.minimum(ids_[(b * G + g) * N_SEL + n * nb + u], n_past - 1)
            return (pt[b, bid // bpp] * bpp + bid % bpp, 0, 0, 0)
        return pl.BlockSpec((1, SEL_BLOCK // 2, 2 * G, HEAD_DIM), index)

    q_spec = pl.BlockSpec((1, G, R, HEAD_DIM), lambda b, n, pt, ids_: (b, 0, 0, 0))
    n_spec = pl.BlockSpec((1, 1, G * HEAD_DIM), lambda b, n, pt, ids_: (b, 0, 0))
    g_spec = pl.BlockSpec((1, G, R, LANE), lambda b, n, pt, ids_: (b, 0, 0, 0))
    c_specs = [cache_spec(g, u) for g in range(G) for u in range(nb)]
    return pl.pallas_call(
        functools.partial(_attn_dec_sel_kernel, n_past=n_past, pos=pos),
        grid_spec=pltpu.PrefetchScalarGridSpec(
            num_scalar_prefetch=2,
            grid=(B, N_SEL // nb),
            in_specs=[q_spec] + c_specs + c_specs + [n_spec, n_spec, q_spec, q_spec, g_spec],
            out_specs=q_spec,
            scratch_shapes=[pltpu.VMEM((G, R, 1), F32), pltpu.VMEM((G, R, 1), F32),
                            pltpu.VMEM((G, R, HEAD_DIM), F32)],
        ),
        out_shape=jax.ShapeDtypeStruct((B, G, R, HEAD_DIM), F32),
        compiler_params=_params(("parallel", "arbitrary")),
        name="attn_dec_sel",
    )(table, ids.reshape(-1), qr, *([kc] * (G * nb)), *([vc] * (G * nb)), k_new, v_new, ocmp, owin, gates)


def _rope_tables(pos):
    half = HEAD_DIM // 2
    inv = ROPE_THETA ** (-jnp.arange(half, dtype=F32) / half)
    ang = pos.astype(F32)[:, None] * inv[None, :]
    cos, sin = jnp.cos(ang), jnp.sin(ang)
    return jnp.concatenate([cos, cos], axis=1), jnp.concatenate([-sin, sin], axis=1)


def _prep_weights(W):
    D = W['w_kv'].shape[0]
    H = D // HEAD_DIM
    R = H // N_KV_HEADS
    pg = D // len(POOL_WINDOWS)
    half = CMP_STRIDE * HEAD_DIM
    P = {}
    P['w_pool'] = W['w_pool'].astype(BF16).reshape(-1, len(POOL_WINDOWS) * pg, pg)
    kv_gain = jnp.ones((W['w_kv'].shape[1], KV_W), F32)
    kv_gain = kv_gain.at[2].set(jnp.tile(W['g_k_sel'], N_KV_HEADS)).at[4].set(jnp.tile(W['g_k_win'], N_KV_HEADS))
    P['kv_gain'] = kv_gain
    for t in ('k', 'v'):
        w1 = W['w_cmp_%s1' % t].astype(BF16)
        P['w_cmp_%s1' % t] = w1
        P['w_cmp_%scat' % t] = jnp.concatenate([w1[:half], w1[half:]], axis=1)
        P['w_cmp_%s2' % t] = W['w_cmp_%s2' % t].astype(BF16)
        P['pe_%s' % t] = W['pe_cmp_%s' % t].astype(BF16).reshape(1, -1)
    n_b = W['w_qg'].shape[0]
    wg = W['w_qg'][:, :, H * HEAD_DIM:].astype(BF16).reshape(n_b, D, N_KV_HEADS, R * N_BRANCH)
    wg = jnp.pad(wg, ((0, 0), (0, 0), (0, 0), (0, LANE - R * N_BRANCH)))
    P['w_gate'] = wg.reshape(n_b, D, N_KV_HEADS * LANE)
    P['w_ple'] = W['w_ple'].astype(BF16)
    P[('w_up', 0)] = W['w_up'][0].astype(BF16)
    return P


def _dense(x, wname, layer, W, P, *, N, epilogue, specs, extras=(), out_dtypes, emit=False, side_cast=None, row_scale_in=None,
           norm_gains=None, name):
    M, K = x.shape
    shapes = [d if isinstance(d, jax.ShapeDtypeStruct) else jax.ShapeDtypeStruct((M, N), d) for d in out_dtypes]
    key = (wname, layer)
    tmx = 1024 if M >= 1024 else M
    use_mm = key in P
    if use_mm:
        grid, tm, tn, tk = _tiles(M, N, K, tmx, 1024, 2048 if (M >= 1024 and K > 4096) else 4096)
    else:
        grid, tm, tn, tk = _tiles(M, N, K, tmx, 512, K if K <= 4096 else 2048)
    extra_specs, out_specs = specs(tm, tn)
    extras = list(extras)
    if row_scale_in is not None:
        epilogue = _row_scaled(epilogue)
        extras.insert(0, row_scale_in)
        extra_specs = [pl.BlockSpec((tm, LANE), lambda i, j, k: (i, 0))] + list(extra_specs)
    if norm_gains:
        epilogue = _norm_producer(epilogue, len(extras))
        n_ex, n_ex_specs, n_shapes, n_specs = _norm_io(norm_gains, M, N, grid[1], tm, tn)
        extras, extra_specs = extras + n_ex, list(extra_specs) + n_ex_specs
        shapes, out_specs = shapes + n_shapes, list(out_specs) + n_specs
    side = None if side_cast is None else (W[side_cast[0]], side_cast[1])
    if use_mm:
        outs = list(mm(x, P[key], grid=grid, tm=tm, tn=tn, tk=tk, epilogue=epilogue, extras=extras,
                       extra_specs=extra_specs, out_shapes=shapes, out_specs=out_specs, side=side, name=name))
    else:
        w = W[wname]
        outs = list(mm_ws(x, w, layer=layer if w.ndim == 3 else None, emit=emit, side=side, grid=grid, tm=tm,
                          tn=tn, tk=tk, epilogue=epilogue, extras=extras, extra_specs=extra_specs,
                          out_shapes=shapes, out_specs=out_specs, name=name))
    if side is not None:
        P[side_cast] = outs.pop()
    if emit and not use_mm:
        P[key] = outs.pop()
    if norm_gains:
        outs.append(row_scale(outs.pop(), N))
    return outs


def _ffn_ple(h, normed, p_l, layer, W, P, next_gains):
    M, D = h.shape
    F = W['w_up'].shape[2]
    host = M >= 1024
    next_up = ('w_up', layer + 1) if host and layer + 1 < W['w_up'].shape[0] else None
    mn = lambda tm, tn: ([], [_mn_spec(tm, tn)])
    res = lambda tm, tn: ([_mn_spec(tm, tn)], [_mn_spec(tm, tn)])
    y, rs = normed
    (u,) = _dense(y, 'w_up', layer, W, P, N=F, epilogue=lambda acc, rows: (jnp.square(jnp.maximum(acc, 0.0)),),
                  specs=mn, out_dtypes=[BF16], side_cast=('w_down', layer) if host else None, row_scale_in=rs,
                  name="ffn_up")
    h, y, rs = _dense(u, 'w_down', layer, W, P, N=D, epilogue=lambda acc, rows, r: (r[rows, :] + acc,), specs=res,
                      extras=[h], out_dtypes=[F32], norm_gains=[W['g_ple'][layer]], side_cast=next_up,
                      name="ffn_down")
    ple_dim = p_l.shape[1]
    ple_specs = lambda tm, tn: ([_mn_spec(tm, tn), pl.BlockSpec((tm, ple_dim), lambda i, j, k: (i, 0)),
                                 pl.BlockSpec((ple_dim, tn), lambda i, j, k: (0, j))], [_mn_spec(tm, tn)])
    return _dense(y, 'w_ple_gate', layer, W, P, N=D,
                  epilogue=lambda acc, rows, r, pp, wp: (r[rows, :] + _dot(pp[rows, :], wp[...]) * _sigmoid(acc),),
                  specs=ple_specs, extras=[h, p_l.astype(BF16), P['w_ple'][layer]], out_dtypes=[F32],
                  row_scale_in=rs, norm_gains=next_gains, name="ple")


def _kv_epilogue(acc, rows, gain_ref, cos_ref, sin_ref):
    j = pl.program_id(0)
    cosf, sinf = cos_ref[rows, :], sin_ref[rows, :]
    heads = []
    for hh in range(N_KV_HEADS):
        sl = slice(hh * HEAD_DIM, (hh + 1) * HEAD_DIM)
        heads.append(_rope(_head_norm(acc[:, sl], gain_ref[0, :, sl]), cosf, sinf))
    out = jnp.where((j == 2) | (j == 4), jnp.concatenate(heads, axis=1), acc)
    return out, out


def _q_epilogue(acc, rows, gq_ref, cos_ref, sin_ref):
    cosf, sinf = cos_ref[rows, :], sin_ref[rows, :]
    qs, qrs = [], []
    for hh in range(acc.shape[1] // HEAD_DIM):
        qn = _head_norm(acc[:, hh * HEAD_DIM:(hh + 1) * HEAD_DIM], gq_ref[...])
        qs.append(qn)
        qrs.append(_rope(qn, cosf, sinf))
    return jnp.concatenate(qs, axis=1), jnp.concatenate(qrs, axis=1)


def _trunk(x, p, pool_prefix, pos0, W, P, attend):
    B, T, D = x.shape
    M = B * T
    tmx = 1024 if M >= 1024 else M
    h = x.reshape(M, D)
    pg = D // len(POOL_WINDOWS)

    d, pool_new = pool_diff(x, W['g_mix'][0], pool_prefix[0], pos0)
    d, pool_new = d.reshape(M, D), pool_new[None]
    grid, tm, tn, tk = _tiles(M, D, pg, tmx, pg, pg)
    n_ex, n_ex_specs, n_shapes, n_specs = _norm_io([W['g_ffn'][0]], M, D, grid[1], tm, tn)
    h, y, ssq = mm(d, P['w_pool'][0], grid=grid, tm=tm, tn=tn, tk=tk,
                   x_map=lambda i, j, k: (i, j), w_map=lambda i, j, k: (j, 0),
                   epilogue=_norm_producer(lambda acc, rows, sc, r: (r[rows, :] + acc * sc[...],), 2),
                   extras=[W['pool_scale'][0].reshape(1, D), h] + n_ex,
                   extra_specs=[pl.BlockSpec((1, tn), lambda i, j, k: (0, j)), _mn_spec(tm, tn)] + n_ex_specs,
                   out_shapes=[jax.ShapeDtypeStruct((M, D), F32)] + n_shapes,
                   out_specs=[_mn_spec(tm, tn)] + n_specs, name="pool_mix")
    h, hkv, a1, rs = _ffn_ple(h, (y, row_scale(ssq, D)), p[0].reshape(M, -1), 0, W, P,
                              [W['g_kv'], W['g_mix'][1]])

    pos = pos0 + jnp.tile(jnp.arange(T, dtype=jnp.int32), B)
    cosf, sinf = _rope_tables(pos)
    n_kv = W['w_kv'].shape[1]
    rope_spec = lambda tm: pl.BlockSpec((tm, HEAD_DIM), lambda i, j, k: (i, 0))
    kv_spec = lambda tm: pl.BlockSpec((1, tm, KV_W), lambda i, j, k: (j, i, 0))
    kv4_spec = lambda tm: pl.BlockSpec((1, tm, N_KV_HEADS, HEAD_DIM), lambda i, j, k: (j, i, 0, 0))
    kv_specs = lambda tm, tn: ([pl.BlockSpec((1, 1, KV_W), lambda i, j, k: (j, 0, 0)), rope_spec(tm), rope_spec(tm)],
                               [kv4_spec(tm), kv_spec(tm)])
    kv, kv_b = _dense(hkv, 'w_kv2d', None, W, P, N=n_kv * KV_W, epilogue=_kv_epilogue, specs=kv_specs,
                      extras=[P['kv_gain'].reshape(n_kv, 1, KV_W), cosf, sinf],
                      out_dtypes=[jax.ShapeDtypeStruct((n_kv, M, N_KV_HEADS, HEAD_DIM), F32),
                                  jax.ShapeDtypeStruct((n_kv, M, KV_W), BF16)], row_scale_in=rs, name="kv_proj")
    q_specs = lambda tm, tn: ([pl.BlockSpec((1, HEAD_DIM), lambda i, j, k: (0, 0)), rope_spec(tm), rope_spec(tm)],
                              [_mn_spec(tm, tn)] * 2)
    q, qr = _dense(a1, 'w_qg', 0, W, P, N=D, epilogue=_q_epilogue, specs=q_specs,
                   extras=[W['g_q'][0].reshape(1, HEAD_DIM), cosf, sinf], out_dtypes=[BF16, BF16], row_scale_in=rs,
                   name="q_proj")
    ng = N_KV_HEADS * LANE
    grid, tm, tn, tk = _tiles(M, ng, D, tmx, ng, D)
    (gates,) = mm(a1, P['w_gate'][0], grid=grid, tm=tm, tn=tn, tk=tk,
                  epilogue=_row_scaled(lambda acc, rows: (_sigmoid(acc),)), extras=[rs],
                  extra_specs=[pl.BlockSpec((tm, LANE), lambda i, j, k: (i, 0))],
                  out_shapes=[jax.ShapeDtypeStruct((M, ng), F32)], out_specs=[_mn_spec(tm, tn)], name="gate_proj")

    o, win_state = attend(kv, kv_b, q, qr, gates)

    h, y, rs = _dense(o, 'w_o', 0, W, P, N=D, epilogue=lambda acc, rows, r: (r[rows, :] + acc,), extras=[h],
                      specs=lambda tm, tn: ([_mn_spec(tm, tn)], [_mn_spec(tm, tn)]), out_dtypes=[F32],
                      norm_gains=[W['g_ffn'][1]], name="attn_out")
    (h,) = _ffn_ple(h, (y, rs), p[1].reshape(M, -1), 1, W, P, [])
    rows = tuple(kv[n].reshape(B, T, N_KV_HEADS, HEAD_DIM) for n in range(4))
    return h.reshape(B, T, D), pool_new, rows, win_state


def _compress(k_pages, v_pages, table, cp, W, P):
    abk, abv = cmp_partials(k_pages, v_pages, table, P['w_cmp_kcat'], P['w_cmp_vcat'], cp)
    gain = W['g_k_cmp'].reshape(1, HEAD_DIM)
    ck = cmp_finish(abk, P['pe_k'], P['w_cmp_k1'], P['w_cmp_k2'], gain, norm=True)
    cv = cmp_finish(abv, P['pe_v'], P['w_cmp_v1'], P['w_cmp_v2'], gain, norm=False)
    return ck, cv


def kernel(x_prompt, x_sample, state_pool, cache_k_cmp, cache_v_cmp, cache_k_sel, cache_v_sel, state_k_win, state_v_win, page_table, p_prompt, p_sample, g_mix, w_pool, pool_scale, g_kv, w_kv, g_k_cmp, g_k_sel, g_k_win, w_cmp_k1, w_cmp_k2, pe_cmp_k, w_cmp_v1, w_cmp_v2, pe_cmp_v, w_qg, g_q, w_o, g_ffn, w_up, w_down, g_ple, w_ple, w_ple_gate):
    W = dict(g_mix=g_mix, w_pool=w_pool, pool_scale=pool_scale, g_kv=g_kv, w_kv=w_kv, g_k_cmp=g_k_cmp,
             g_k_sel=g_k_sel, g_k_win=g_k_win, w_cmp_k1=w_cmp_k1, w_cmp_k2=w_cmp_k2, pe_cmp_k=pe_cmp_k,
             w_cmp_v1=w_cmp_v1, w_cmp_v2=w_cmp_v2, pe_cmp_v=pe_cmp_v, w_qg=w_qg, g_q=g_q, w_o=w_o,
             g_ffn=g_ffn, w_up=w_up, w_down=w_down, g_ple=g_ple, w_ple=w_ple, w_ple_gate=w_ple_gate)
    P = _prep_weights(W)
    W['w_kv2d'] = w_kv.reshape(w_kv.shape[0], -1)
    Bp, Tp, D = x_prompt.shape
    Bs, Ts, _ = x_sample.shape
    assert Ts == 1, "the decode path handles one new token per sequence"
    page = cache_k_cmp.shape[1]
    past_len = page_table.shape[1] * page
    R = D // HEAD_DIM // N_KV_HEADS
    assert Tp % page == 0 and past_len % SEL_BLOCK == 0

    def attend_prompt(kv, kv_b, q, qr, gates):
        ppb = Tp // page
        table = jnp.arange(Bp * ppb, dtype=jnp.int32).reshape(Bp, ppb)
        pages = lambda a: a.reshape(-1, page, N_KV_HEADS, HEAD_DIM)
        ck, cv = _compress(pages(kv[0]), pages(kv[1]), table, ppb, W, P)
        nc = Tp // CMP_STRIDE - CMP_BLOCK // CMP_STRIDE + 1
        seq = lambda a: a.reshape(Bp, Tp, -1)
        o = attn_prompt(seq(q), seq(qr), ck, cv, kv_b.reshape(-1, Bp, Tp, KV_W), seq(gates), nc=nc)
        nw = min(WINDOW, Tp)
        win = tuple(kv[n].reshape(Bp, Tp, N_KV_HEADS, HEAD_DIM)[:, -nw:] for n in (4, 5))
        return o.reshape(Bp * Tp, D), win

    def attend_sample(kv, kv_b, q, qr, gates):
        del kv_b
        ck, cv = _compress(cache_k_cmp, cache_v_cmp, page_table, min(32, page_table.shape[1]), W, P)
        nc = (past_len - (CMP_BLOCK - 1)) // CMP_STRIDE + 1
        ns = past_len // SEL_BLOCK + 1
        wb = state_k_win.shape[1]
        new_row = lambda a: a.reshape(Bs, 1, N_KV_HEADS, HEAD_DIM)
        kw = jnp.concatenate([state_k_win, new_row(kv[4])], axis=1)[:, -wb:]
        vw = jnp.concatenate([state_v_win, new_row(kv[5])], axis=1)[:, -wb:]
        heads = lambda a: a.reshape(Bs, N_KV_HEADS, R, HEAD_DIM)
        ocmp, owin, ids = attn_dec_dense(heads(q), heads(qr), ck, cv, kw, vw, nc=nc, ns=ns, pos=past_len)
        gt = gates.reshape(Bs, N_KV_HEADS, LANE)[:, :, :R * N_BRANCH].reshape(Bs, N_KV_HEADS, R, N_BRANCH)
        gt = jnp.pad(gt, ((0, 0), (0, 0), (0, 0), (0, LANE - N_BRANCH)))
        o = attn_dec_sel(page_table, ids, heads(qr), cache_k_sel, cache_v_sel,
                         kv[2].reshape(Bs, 1, KV_W), kv[3].reshape(Bs, 1, KV_W), ocmp, owin, gt, pos=past_len)
        return o.reshape(Bs, D).astype(BF16), (kw, vw)

    pool_zero = jnp.zeros((state_pool.shape[0], Bp, POOL_STATE, D), x_prompt.dtype)
    y_p, pool_p, rows_p, win_p = _trunk(x_prompt, p_prompt, pool_zero, 0, W, P, attend_prompt)
    y_s, pool_s, rows_s, win_s = _trunk(x_sample, p_sample, state_pool, past_len, W, P, attend_sample)
    return (y_p, y_s, pool_p, pool_s, rows_p[0], rows_p[1], rows_p[2], rows_p[3], win_p[0], win_p[1],
            rows_s[0], rows_s[1], rows_s[2], rows_s[3], win_s[0], win_s[1])
```

```python
import functools

import jax
import jax.numpy as jnp
import numpy as np
from jax import lax
from jax.experimental import pallas as pl
from jax.experimental.pallas import tpu as pltpu

F32 = jnp.float32
BF16 = jnp.bfloat16

POOL_WINDOWS = (2, 4, 8, 16)
POOL_STATE = max(POOL_WINDOWS) - 1
POOL_PAD = POOL_STATE + 1
HEAD_DIM = 128
N_KV_HEADS = 4
N_BRANCH = 3
CMP_BLOCK = 32
CMP_STRIDE = 16
CMP_HIDDEN = 2 * HEAD_DIM
SEL_BLOCK = 64
N_SEL = 16
WINDOW = 512
ROPE_THETA = 10000.0
EPS = 1e-6
SCALE = HEAD_DIM ** -0.5
NEG = -1e30
FORCE = 1e9
PAD_SCORE = -3e38
KV_W = N_KV_HEADS * HEAD_DIM
LANE = 128
VMEM_LIMIT = 56 * 1024 * 1024


def _params(sem):
    return pltpu.CompilerParams(dimension_semantics=sem, vmem_limit_bytes=VMEM_LIMIT)


def _sigmoid(x):
    return 1.0 / (1.0 + jnp.exp(-x))


def _dot(a, b):
    return jnp.dot(a, b, preferred_element_type=F32)


def _dot_nt(a, b):
    return lax.dot_general(a, b, (((1,), (1,)), ((), ())), preferred_element_type=F32)


def _dot_tn(a, b):
    return lax.dot_general(a, b, (((0,), (0,)), ((), ())), preferred_element_type=F32)


def _head_norm(x, g):
    return x * lax.rsqrt(jnp.mean(x * x, axis=-1, keepdims=True) + EPS) * g


def _rope(x, cosf, sinf):
    return x * cosf + pltpu.roll(x, HEAD_DIM // 2, 1) * sinf


def _fold_lanes(x):
    parts = [x[:, c * LANE:(c + 1) * LANE] for c in range(x.shape[1] // LANE)]
    return functools.reduce(lambda u, v: u + v, parts)


def _norm_producer(epilogue, n_base):
    def wrapped(acc, rows, *extras):
        (h,) = epilogue(acc, rows, *extras[:n_base])
        return (h, *[(h * g[...]).astype(BF16) for g in extras[n_base:]], _fold_lanes(h * h))
    return wrapped


def _row_scaled(epilogue):
    def wrapped(acc, rows, rs_ref, *extras):
        rs = rs_ref[rows, :]
        acc = jnp.concatenate([acc[:, c * LANE:(c + 1) * LANE] * rs for c in range(acc.shape[1] // LANE)], axis=1)
        return epilogue(acc, rows, *extras)
    return wrapped


def _row_scale_kernel(ssq_ref, o_ref, *, d):
    tot = jnp.sum(functools.reduce(lambda u, v: u + v, [ssq_ref[j] for j in range(ssq_ref.shape[0])]),
                  axis=1, keepdims=True)
    o_ref[...] = jnp.broadcast_to(lax.rsqrt(tot / d + EPS), o_ref.shape)


def row_scale(ssq, d):
    gn, M, _ = ssq.shape
    tm = min(M, 1024)
    return pl.pallas_call(
        functools.partial(_row_scale_kernel, d=d),
        grid=(M // tm,),
        in_specs=[pl.BlockSpec((gn, tm, LANE), lambda i: (0, i, 0))],
        out_specs=pl.BlockSpec((tm, LANE), lambda i: (i, 0)),
        out_shape=jax.ShapeDtypeStruct((M, LANE), F32),
        compiler_params=_params(("parallel",)),
        name="row_scale",
    )(ssq)


def _norm_io(gains, M, N, gn, tm, tn):
    extras = [g.reshape(1, N).astype(F32) for g in gains]
    extra_specs = [pl.BlockSpec((1, tn), lambda i, j, k: (0, j)) for _ in gains]
    shapes = [jax.ShapeDtypeStruct((M, N), BF16) for _ in gains] + [jax.ShapeDtypeStruct((gn, M, LANE), F32)]
    specs = [_mn_spec(tm, tn) for _ in gains] + [pl.BlockSpec((1, tm, LANE), lambda i, j, k: (j, i, 0))]
    return extras, extra_specs, shapes, specs


EPILOGUE_ROWS = 256


def _store_rows(o_ref, rows, r):
    if len(o_ref.shape) == 4:
        for hh in range(o_ref.shape[2]):
            o_ref[0, rows, hh, :] = r[:, hh * HEAD_DIM:(hh + 1) * HEAD_DIM].astype(o_ref.dtype)
    elif len(o_ref.shape) == 3:
        o_ref[0, rows, :] = r.astype(o_ref.dtype)
    else:
        o_ref[rows, :] = r.astype(o_ref.dtype)


def _finish_rows(acc_of, tm, outs, extras, epilogue):
    ch = min(tm, EPILOGUE_ROWS)
    for c in range(tm // ch):
        rows = slice(c * ch, (c + 1) * ch)
        for o_ref, r in zip(outs, epilogue(acc_of(rows), rows, *extras)):
            if isinstance(r, tuple):
                pl.when(r[0])(functools.partial(_store_rows, o_ref, rows, r[1]))
            else:
                _store_rows(o_ref, rows, r)


def _k_steps(x_ref, w, acc_ref, nk, outs, extras, epilogue):
    k = pl.program_id(2)

    @pl.when(k == 0)
    def _():
        acc_ref[...] = _dot(x_ref[...], w())

    @pl.when((k > 0) & (k < nk - 1))
    def _():
        acc_ref[...] += _dot(x_ref[...], w())

    @pl.when(k == nk - 1)
    def _():
        _finish_rows(lambda rows: acc_ref[rows, :] + _dot(x_ref[rows, :], w()), x_ref.shape[0], outs, extras,
                     epilogue)


def _side_io(side, n_steps, step):
    s_arr, s_layer = side
    _, rows, cols = s_arr.shape
    rs = rows // n_steps
    assert rs * n_steps == rows and rs % 16 == 0
    in_spec = pl.BlockSpec((None, rs, cols), lambda *g: (s_layer, step(*g), 0))
    out_spec = pl.BlockSpec((rs, cols), lambda *g: (step(*g), 0))
    return s_arr, in_spec, jax.ShapeDtypeStruct((rows, cols), BF16), out_spec


def _mm_kernel(*refs, nk, n_extra, n_out, epilogue, side):
    x_ref, w_ref = refs[0], refs[1]
    extras = refs[2:2 + n_extra]
    n_in = 2 + n_extra + side
    outs = refs[n_in:n_in + n_out]
    tm = x_ref.shape[0]
    if side:
        refs[n_in + n_out][...] = refs[n_in - 1][...].astype(BF16)

    if nk == 1:
        _finish_rows(lambda rows: _dot(x_ref[rows, :], w_ref[...]), tm, outs, extras, epilogue)
    else:
        _k_steps(x_ref, lambda: w_ref[...], refs[-1], nk, outs, extras, epilogue)


def mm(x, w, *, grid, tm, tn, tk, epilogue, extras=(), extra_specs=(), out_shapes, out_specs,
       x_map=None, w_map=None, side=None, name):
    gm, gn, nk = grid
    x_map = x_map or (lambda i, j, k: (i, k))
    w_map = w_map or (lambda i, j, k: (k, j))
    ins, in_specs = [x, w, *extras], [pl.BlockSpec((tm, tk), x_map), pl.BlockSpec((tk, tn), w_map)]
    in_specs += list(extra_specs)
    n_out, out_shapes, out_specs = len(out_shapes), list(out_shapes), list(out_specs)
    if side is not None:
        s_arr, s_in, s_shape, s_out = _side_io(side, gm * gn * nk, lambda i, j, k: (i * gn + j) * nk + k)
        ins.append(s_arr), in_specs.append(s_in), out_shapes.append(s_shape), out_specs.append(s_out)
    kern = functools.partial(_mm_kernel, nk=nk, n_extra=len(extras), n_out=n_out, epilogue=epilogue,
                             side=side is not None)
    return pl.pallas_call(
        kern,
        grid=grid,
        in_specs=in_specs,
        out_specs=out_specs,
        out_shape=out_shapes,
        scratch_shapes=[pltpu.VMEM((tm, tn), F32)] if nk > 1 else [],
        compiler_params=_params(("parallel", "arbitrary", "arbitrary")),
        name=name,
    )(*ins)


def _mm_ws_kernel(*refs, nk, n_extra, n_out, epilogue, emit, side):
    x_ref, w_ref = refs[0], refs[1]
    extras = refs[2:2 + n_extra]
    n_in = 2 + n_extra + side
    outs = refs[n_in:n_in + n_out]
    rest = refs[n_in + n_out:]
    wb_ref = rest[emit + side]
    i, k = pl.program_id(1), pl.program_id(2)

    @pl.when(i == 0)
    def _():
        wb_ref[k] = w_ref[...].astype(BF16)
        if emit:
            rest[0][...] = wb_ref[k]

    if side:
        rest[emit][...] = refs[n_in - 1][...].astype(BF16)

    tm = x_ref.shape[0]
    if nk == 1:
        _finish_rows(lambda rows: _dot(x_ref[rows, :], wb_ref[0]), tm, outs, extras, epilogue)
    else:
        _k_steps(x_ref, lambda: wb_ref[k], rest[-1], nk, outs, extras, epilogue)


def mm_ws(x, w, *, layer=None, emit=False, side=None, grid, tm, tn, tk, epilogue, extras=(), extra_specs=(),
          out_shapes, out_specs, name):
    gm, gn, nk = grid
    swap = lambda f: (lambda j, i, k: f(i, j, k))
    respec = lambda s: pl.BlockSpec(s.block_shape, swap(s.index_map))
    k_once = lambda i, k: jnp.where(i == 0, k, nk - 1)
    if layer is None:
        w_spec = pl.BlockSpec((tk, tn), lambda j, i, k: (k_once(i, k), j))
    else:
        w_spec = pl.BlockSpec((None, tk, tn), lambda j, i, k: (layer, k_once(i, k), j))
    n_out = len(out_shapes)
    ins, in_specs = [x, w, *extras], [pl.BlockSpec((tm, tk), lambda j, i, k: (i, k)), w_spec]
    in_specs += [respec(s) for s in extra_specs]
    out_shapes, out_specs = list(out_shapes), [respec(s) for s in out_specs]
    if emit:
        out_shapes.append(jax.ShapeDtypeStruct((nk * tk, gn * tn), BF16))
        out_specs.append(pl.BlockSpec((tk, tn), lambda j, i, k: (k_once(i, k), j)))
    if side is not None:
        s_arr, s_in, s_shape, s_out = _side_io(side, gm * gn * nk, lambda j, i, k: (j * gm + i) * nk + k)
        ins.append(s_arr), in_specs.append(s_in), out_shapes.append(s_shape), out_specs.append(s_out)
    kern = functools.partial(_mm_ws_kernel, nk=nk, n_extra=len(extras), n_out=n_out, epilogue=epilogue, emit=emit,
                             side=side is not None)
    return pl.pallas_call(
        kern,
        grid=(gn, gm, nk),
        in_specs=in_specs,
        out_specs=out_specs,
        out_shape=out_shapes,
        scratch_shapes=[pltpu.VMEM((nk, tk, tn), BF16)] + ([pltpu.VMEM((tm, tn), F32)] if nk > 1 else []),
        compiler_params=_params(("parallel", "arbitrary", "arbitrary")),
        name=name,
    )(*ins)


def _tiles(M, N, K, tm, tn, tk):
    tm, tn, tk = min(tm, M), min(tn, N), min(tk, K)
    return (M // tm, N // tn, K // tk), tm, tn, tk


def _mn_spec(tm, tn):
    return pl.BlockSpec((tm, tn), lambda i, j, k: (i, j))


def _pool_diff_kernel(*refs, tt, pos0, halo):
    x_ref, pre_ref, g_ref = refs[0], refs[1 + halo], refs[2 + halo]
    d_ref, st_ref, seq_ref = refs[3 + halo:]
    t = pl.program_id(1)
    norm = lambda x: x * lax.rsqrt(jnp.mean(x * x, axis=-1, keepdims=True) + EPS) * g_ref[...]
    a = norm(x_ref[0])
    seq_ref[POOL_PAD:POOL_PAD + tt, :] = a

    @pl.when(t == 0)
    def _():
        seq_ref[0:POOL_PAD, :] = pre_ref[0]

    if halo:
        @pl.when(t > 0)
        def _():
            seq_ref[0:POOL_PAD, :] = norm(refs[1][0])

    pos = pos0 + t * tt + lax.broadcasted_iota(jnp.int32, (tt, 1), 0)
    pg = a.shape[1] // len(POOL_WINDOWS)
    for g, w in enumerate(POOL_WINDOWS):
        cols = slice(g * pg, (g + 1) * pg)
        s = a[:, cols]
        for j in range(1, w):
            s = s + seq_ref[POOL_PAD - j:POOL_PAD - j + tt, cols]
        cnt = jnp.minimum(pos + 1, w).astype(F32)
        d_ref[0, :, cols] = (s / cnt - a[:, cols]).astype(d_ref.dtype)
    st_ref[0] = seq_ref[tt:tt + POOL_PAD, :]


def pool_diff(x, gain, prefix, pos0):
    B, T, D = x.shape
    tt = min(T, 256)
    halo = T > tt
    pre = jnp.concatenate([jnp.zeros((B, 1, D), F32), prefix], axis=1)
    hpt = tt // POOL_PAD
    in_specs = [pl.BlockSpec((1, tt, D), lambda b, t: (b, t, 0))]
    if halo:
        in_specs.append(pl.BlockSpec((1, POOL_PAD, D), lambda b, t: (b, jnp.maximum(t * hpt - 1, 0), 0)))
    in_specs += [pl.BlockSpec((1, POOL_PAD, D), lambda b, t: (b, 0, 0)), pl.BlockSpec((1, D), lambda b, t: (0, 0))]
    d, st = pl.pallas_call(
        functools.partial(_pool_diff_kernel, tt=tt, pos0=pos0, halo=halo),
        grid=(B, T // tt),
        in_specs=in_specs,
        out_specs=[pl.BlockSpec((1, tt, D), lambda b, t: (b, t, 0)),
                   pl.BlockSpec((1, POOL_PAD, D), lambda b, t: (b, 0, 0))],
        out_shape=[jax.ShapeDtypeStruct((B, T, D), BF16), jax.ShapeDtypeStruct((B, POOL_PAD, D), F32)],
        scratch_shapes=[pltpu.VMEM((POOL_PAD + tt, D), F32)],
        compiler_params=_params(("parallel", "arbitrary")),
        name="pool_diff",
    )(*([x, x] if halo else [x]), pre, gain.reshape(1, D).astype(F32))
    return d, st[:, 1:]


PAGES_PER_STEP = 8
CMP_FINISH_ROWS = 512


def _cmp_partial_kernel(pt_ref, *refs, steps, pps):
    del pt_ref
    k_pages, v_pages = refs[:pps], refs[pps:2 * pps]
    wk_ref, wv_ref, abk_ref, abv_ref, xk_ref, xv_ref = refs[2 * pps:]
    p = pl.program_id(2)
    G = N_KV_HEADS
    half = CMP_STRIDE // 2
    rows_per_page = (k_pages[0].shape[1] // half) * G
    for pages, x_ref in ((k_pages, xk_ref), (v_pages, xv_ref)):
        for q, page_ref in enumerate(pages):
            for sb in range(page_ref.shape[1] // half):
                row = q * rows_per_page + sb * G
                for r in range(CMP_STRIDE):
                    x_ref[p, row:row + G, r * HEAD_DIM:(r + 1) * HEAD_DIM] = (
                        page_ref[0, sb * half + r // 2, (r % 2) * G:(r % 2 + 1) * G, :])

    @pl.when(p == steps - 1)
    def _():
        for x_ref, w_ref, ab_ref in ((xk_ref, wk_ref, abk_ref), (xv_ref, wv_ref, abv_ref)):
            x = x_ref[...].reshape(ab_ref.shape[1], x_ref.shape[2])
            ab_ref[0] = _dot(x.astype(BF16), w_ref[...])


def cmp_partials(k_pages, v_pages, table, wk_cat, wv_cat, cp):
    B, ppb = table.shape
    page = k_pages.shape[1]
    pps = PAGES_PER_STEP
    rows_per_page = page // CMP_STRIDE * N_KV_HEADS
    nch, steps = ppb // cp, cp // pps
    m = cp * rows_per_page
    kdim = CMP_STRIDE * HEAD_DIM
    pair_rows = lambda a: a.reshape(a.shape[0], page // 2, 2 * N_KV_HEADS, HEAD_DIM)

    def page_spec(q):
        return pl.BlockSpec((1, page // 2, 2 * N_KV_HEADS, HEAD_DIM),
                            lambda b, c, p, pt: (pt[b, c * cp + p * pps + q], 0, 0, 0))

    page_specs = [page_spec(q) for q in range(pps)]
    w_spec = pl.BlockSpec((kdim, 2 * CMP_HIDDEN), lambda b, c, p, pt: (0, 0))
    out_spec = pl.BlockSpec((1, m, 2 * CMP_HIDDEN), lambda b, c, p, pt: (b, c, 0))
    out_shape = jax.ShapeDtypeStruct((B, ppb * rows_per_page, 2 * CMP_HIDDEN), F32)
    x_scratch = pltpu.VMEM((steps, pps * rows_per_page, kdim), F32)
    return pl.pallas_call(
        functools.partial(_cmp_partial_kernel, steps=steps, pps=pps),
        grid_spec=pltpu.PrefetchScalarGridSpec(
            num_scalar_prefetch=1,
            grid=(B, nch, steps),
            in_specs=page_specs + page_specs + [w_spec, w_spec],
            out_specs=[out_spec, out_spec],
            scratch_shapes=[x_scratch, x_scratch],
        ),
        out_shape=[out_shape, out_shape],
        compiler_params=_params(("parallel", "parallel", "arbitrary")),
        name="cmp_partials",
    )(table, *([pair_rows(k_pages)] * pps), *([pair_rows(v_pages)] * pps), wk_cat, wv_cat)


def _cmp_finish_kernel(ab_ref, pe_ref, w1_ref, w2_ref, g_ref, o_ref, out_ref, *, norm):
    n4 = ab_ref.shape[1]
    G = o_ref.shape[1]
    bias = _dot(jnp.broadcast_to(pe_ref[...], (8, pe_ref.shape[1])), w1_ref[...])[0:1, :]
    ch = min(n4, CMP_FINISH_ROWS)
    for c in range(n4 // ch):
        lo, hi = c * ch, (c + 1) * ch
        first = ab_ref[0, lo:hi, :CMP_HIDDEN]
        if hi + G <= n4:
            second = ab_ref[0, lo + G:hi + G, CMP_HIDDEN:]
        else:
            second = jnp.concatenate([ab_ref[0, lo + G:hi, CMP_HIDDEN:], ab_ref[0, hi - G:hi, CMP_HIDDEN:]], axis=0)
        pre = first + second + bias
        out = _dot((pre * _sigmoid(pre)).astype(BF16), w2_ref[...])
        if norm:
            out = _head_norm(out, g_ref[...])
        out_ref[lo:hi, :] = out
    for g in range(G):
        o_ref[0, g] = out_ref[pl.ds(g, n4 // G, stride=G), :].astype(o_ref.dtype)


def cmp_finish(ab, pe, w1, w2, gain, *, norm):
    B, n4, _ = ab.shape
    G = N_KV_HEADS
    full = lambda a: pl.BlockSpec(a.shape, lambda b: (0,) * a.ndim)
    args = (pe, w1, w2, gain)
    return pl.pallas_call(
        functools.partial(_cmp_finish_kernel, norm=norm),
        grid=(B,),
        in_specs=[pl.BlockSpec((1, n4, 2 * CMP_HIDDEN), lambda b: (b, 0, 0))] + [full(a) for a in args],
        out_specs=pl.BlockSpec((1, G, n4 // G, HEAD_DIM), lambda b: (b, 0, 0, 0)),
        out_shape=jax.ShapeDtypeStruct((B, G, n4 // G, HEAD_DIM), BF16),
        scratch_shapes=[pltpu.VMEM((n4, HEAD_DIM), F32)],
        compiler_params=_params(("parallel",)),
        name="cmp_finish",
    )(ab, *args)


def _select_blocks(score, blk, ns):
    rank = jnp.zeros(score.shape, jnp.int32)
    for j in range(ns):
        sj = score[j:j + 1, :]
        beats = (sj > score) | ((sj == score) & (j < blk))
        rank = rank + beats.astype(jnp.int32)
    return (rank < min(N_SEL, ns)) & (score > 0.5 * NEG)


def _attn_prompt_kernel(q_ref, qr_ref, ck_ref, cv_ref, ks_ref, vs_ref, kw_ref, vw_ref, gate_ref, mapT_ref,
                        expand_ref, o_ref, part_ref, sbias_ref, wbias_ref, m_ref, acc_ref, *, tq, kc, nc, ns, R):
    qi = pl.program_id(2)
    q0 = qi * tq
    pos = q0 + lax.broadcasted_iota(jnp.int32, (tq, 1), 0)
    ncp = ck_ref.shape[2]

    ck = ck_ref[0, 0]
    cv = cv_ref[0, 0]
    cidx = lax.broadcasted_iota(jnp.int32, (1, ncp), 1)
    ok_c = (cidx * CMP_STRIDE + CMP_BLOCK - 1 <= pos) & (cidx < nc)
    imp = jnp.zeros((tq, ncp), F32)
    for r in range(R):
        qh = q_ref[0, :, r * HEAD_DIM:(r + 1) * HEAD_DIM]
        s = jnp.where(ok_c, _dot_nt(qh, ck) * SCALE, NEG)
        e = jnp.exp(s - jnp.max(s, axis=1, keepdims=True))
        p = jnp.where(ok_c, e * (1.0 / jnp.sum(e, axis=1, keepdims=True)), 0.0)
        imp = imp + p
        part_ref[r] = gate_ref[0, :, r * N_BRANCH:r * N_BRANCH + 1] * _dot(p.astype(BF16), cv)

    nsp = mapT_ref.shape[0]
    p_slc = lax.dot_general(mapT_ref[...], imp, (((1,), (1,)), ((), ())), precision=lax.Precision.HIGHEST,
                            preferred_element_type=F32)
    blk = lax.broadcasted_iota(jnp.int32, (nsp, tq), 0)
    pos_l = q0 + lax.broadcasted_iota(jnp.int32, (nsp, tq), 1)
    cur = pos_l // SEL_BLOCK
    vis = blk * SEL_BLOCK <= pos_l
    forced = vis & ((blk == 0) | (blk == cur) | (blk == cur - 1))
    score = jnp.where(forced, FORCE, jnp.where(vis, p_slc, NEG))
    score = jnp.where(blk < ns, score, PAD_SCORE)
    sel = _select_blocks(score, blk, ns).astype(BF16)

    c_hi = (q0 + tq) // kc
    col = lax.broadcasted_iota(jnp.int32, (1, kc), 1)
    sel_keys = _dot_tn(sel, expand_ref[...])
    for c in range(sbias_ref.shape[0]):
        @pl.when(c < c_hi)
        def _(c=c):
            ok = (sel_keys[:, c * kc:(c + 1) * kc] > 0.5) & (c * kc + col <= pos)
            sbias_ref[c] = jnp.where(ok, 0.0, NEG)
    nwc = wbias_ref.shape[0]
    c_w0 = c_hi - nwc
    for d in range(nwc):
        kpos = (c_w0 + d) * kc + col
        wbias_ref[d] = jnp.where((kpos <= pos) & (kpos > pos - WINDOW), 0.0, NEG)

    def fold(t):
        return [t[:, i * LANE:(i + 1) * LANE] for i in range(kc // LANE)]

    ones_blk = jnp.ones((kc, LANE), BF16)

    def branch(k_ref, v_ref, c_lo, bias_of, gate_col):
        def logits(r, c, bias):
            k = k_ref[0, pl.ds(pl.multiple_of(c * kc, kc), kc), :]
            return _dot_nt(qr_ref[0, :, r * HEAD_DIM:(r + 1) * HEAD_DIM], k) * SCALE + bias

        m_ref[...] = jnp.full(m_ref.shape, NEG, F32)
        acc_ref[...] = jnp.zeros(acc_ref.shape, F32)

        def over_chunks(body):
            n = c_hi - c_lo

            def pair(i, carry):
                body(c_lo + 2 * i, carry)
                return body(c_lo + 2 * i + 1, carry)

            lax.fori_loop(0, n // 2, pair, 0)

            @pl.when(n % 2 == 1)
            def _():
                body(c_hi - 1, 0)

        def max_body(c, carry):
            bias = bias_of(c)
            for r in range(R):
                mx = m_ref[r]
                for part in fold(logits(r, c, bias)):
                    mx = jnp.maximum(mx, part)
                m_ref[r] = mx
            return carry

        over_chunks(max_body)
        for r in range(R):
            m_ref[r] = jnp.broadcast_to(jnp.max(m_ref[r], axis=1, keepdims=True), (tq, LANE))

        def sum_body(c, carry):
            bias = bias_of(c)
            v = jnp.concatenate([v_ref[0, pl.ds(pl.multiple_of(c * kc, kc), kc), :], ones_blk], axis=1)
            for r in range(R):
                t = logits(r, c, bias)
                m = m_ref[r]
                ps = [jnp.exp(part - m) for part in fold(t)]
                acc_ref[r] += _dot(jnp.concatenate(ps, axis=1).astype(BF16), v)
            return carry

        over_chunks(sum_body)
        for r in range(R):
            gate = gate_ref[0, :, r * N_BRANCH + gate_col:r * N_BRANCH + gate_col + 1]
            part_ref[r] += gate * (acc_ref[r, :, :HEAD_DIM] * (1.0 / acc_ref[r, :, HEAD_DIM:]))

    branch(ks_ref, vs_ref, 0, lambda c: sbias_ref[c], 1)
    branch(kw_ref, vw_ref, jnp.maximum(c_w0, 0), lambda c: wbias_ref[c - c_w0], 2)
    for r in range(R):
        o_ref[0, :, r * HEAD_DIM:(r + 1) * HEAD_DIM] = part_ref[r].astype(o_ref.dtype)


def _overlap_map(ncp, nsp, ns):
    ratio = CMP_BLOCK // CMP_STRIDE
    per_sel = SEL_BLOCK // CMP_STRIDE
    m = np.zeros((ncp, nsp), np.float32)
    for b in range(ns):
        for mm_ in range(per_sel):
            for n in range(ratio):
                j = per_sel * b + mm_ - n
                if 0 <= j < ncp:
                    m[j, b] += 1.0
    return m


def attn_prompt(q, qr, ck, cv, kvb, gates, *, nc):
    B, T, HD = q.shape
    G = N_KV_HEADS
    R = HD // HEAD_DIM // G
    tq = min(T, 256)
    kc = tq
    ns = T // SEL_BLOCK
    nsp = -(-ns // 8) * 8
    ncp = ck.shape[2]
    mapT = jnp.asarray(_overlap_map(ncp, nsp, ns).T)
    expand = jnp.asarray((np.arange(T)[None, :] // SEL_BLOCK == np.arange(nsp)[:, None]).astype(np.float32), BF16)
    q_spec = pl.BlockSpec((1, tq, R * HEAD_DIM), lambda b, g, i: (b, i, g))
    c_spec = pl.BlockSpec((1, 1, ncp, HEAD_DIM), lambda b, g, i: (b, g, 0, 0))
    kv_spec = lambda n: pl.BlockSpec((None, 1, T, HEAD_DIM), lambda b, g, i: (n, b, 0, g))
    return pl.pallas_call(
        functools.partial(_attn_prompt_kernel, tq=tq, kc=kc, nc=nc, ns=ns, R=R),
        grid=(B, G, T // tq),
        in_specs=[q_spec, q_spec, c_spec, c_spec, kv_spec(2), kv_spec(3), kv_spec(4), kv_spec(5),
                  pl.BlockSpec((1, tq, LANE), lambda b, g, i: (b, i, g)),
                  pl.BlockSpec((nsp, ncp), lambda b, g, i: (0, 0)),
                  pl.BlockSpec((nsp, T), lambda b, g, i: (0, 0))],
        out_specs=q_spec,
        out_shape=jax.ShapeDtypeStruct((B, T, HD), BF16),
        scratch_shapes=[pltpu.VMEM((R, tq, HEAD_DIM), F32), pltpu.VMEM((T // kc, tq, kc), F32),
                        pltpu.VMEM((min(WINDOW, T) // kc + tq // kc, tq, kc), F32)]
                       + [pltpu.VMEM((R, tq, HEAD_DIM), F32), pltpu.VMEM((R, tq, HEAD_DIM + LANE), F32)],
        compiler_params=_params(("parallel", "parallel", "arbitrary")),
        name="attn_prompt",
    )(q, qr, ck, cv, kvb, kvb, kvb, kvb, gates, mapT, expand)


def _attn_dec_dense_kernel(q_ref, qr_ref, ck_ref, cv_ref, kw_ref, vw_ref, map_ref, ocmp_ref, owin_ref, ids_ref,
                           *, nc, ns, pos):
    G, R = q_ref.shape[1], q_ref.shape[2]
    ncp = ck_ref.shape[2]
    nsl = map_ref.shape[1]
    cidx = lax.broadcasted_iota(jnp.int32, (1, ncp), 1)
    ok_c = (cidx * CMP_STRIDE + CMP_BLOCK - 1 <= pos) & (cidx < nc)
    blk_l = lax.broadcasted_iota(jnp.int32, (1, nsl), 1)
    cur = pos // SEL_BLOCK
    vis = blk_l * SEL_BLOCK <= pos
    forced = vis & ((blk_l == 0) | (blk_l == cur) | (blk_l == cur - 1))
    ii = lax.broadcasted_iota(jnp.int32, (nsl, nsl), 0)
    jj = lax.broadcasted_iota(jnp.int32, (nsl, nsl), 1)
    slot = lax.broadcasted_iota(jnp.int32, (nsl, LANE), 1).astype(F32)
    blk_s = lax.broadcasted_iota(jnp.int32, (nsl, LANE), 0).astype(F32)
    for g in range(G):
        s = jnp.where(ok_c, _dot_nt(q_ref[0, g], ck_ref[0, g]) * SCALE, NEG)
        e = jnp.exp(s - jnp.max(s, axis=1, keepdims=True))
        p = jnp.where(ok_c, e * (1.0 / jnp.sum(e, axis=1, keepdims=True)), 0.0)
        ocmp_ref[0, g] = _dot(p.astype(BF16), cv_ref[0, g])
        imp = jnp.broadcast_to(jnp.sum(p, axis=0, keepdims=True), (R, ncp))
        p_slc = jnp.dot(imp, map_ref[...], precision=lax.Precision.HIGHEST, preferred_element_type=F32)[0:1, :]
        score_l = jnp.where(forced, FORCE, jnp.where(vis, p_slc, NEG))
        score_l = jnp.where(blk_l < ns, score_l, PAD_SCORE)
        score_s = jnp.sum(jnp.where(ii == jj, score_l, 0.0), axis=1, keepdims=True)
        beats = (score_l > score_s) | ((score_l == score_s) & (jj < ii))
        rank = jnp.sum(beats.astype(F32), axis=1, keepdims=True)
        ids = jnp.sum(jnp.where(rank == slot, blk_s, 0.0), axis=0, keepdims=True)
        ids_ref[0, g] = ids[:, :N_SEL].astype(jnp.int32)
        kw = kw_ref[0, :, g, :].astype(BF16)
        vw = vw_ref[0, :, g, :].astype(BF16)
        s = _dot_nt(qr_ref[0, g], kw) * SCALE
        e = jnp.exp(s - jnp.max(s, axis=1, keepdims=True))
        p = e * (1.0 / jnp.sum(e, axis=1, keepdims=True))
        owin_ref[0, g] = _dot(p.astype(BF16), vw)


def attn_dec_dense(q, qr, ck, cv, kw, vw, *, nc, ns, pos):
    B, G, R, _ = q.shape
    ncp = ck.shape[2]
    wb = kw.shape[1]
    nsl = -(-ns // LANE) * LANE
    omap = jnp.asarray(_overlap_map(ncp, nsl, ns))
    q_spec = pl.BlockSpec((1, G, R, HEAD_DIM), lambda b: (b, 0, 0, 0))
    c_spec = pl.BlockSpec((1, G, ncp, HEAD_DIM), lambda b: (b, 0, 0, 0))
    w_spec = pl.BlockSpec((1, wb, G, HEAD_DIM), lambda b: (b, 0, 0, 0))
    return pl.pallas_call(
        functools.partial(_attn_dec_dense_kernel, nc=nc, ns=ns, pos=pos),
        grid=(B,),
        in_specs=[q_spec, q_spec, c_spec, c_spec, w_spec, w_spec, pl.BlockSpec((ncp, nsl), lambda b: (0, 0))],
        out_specs=[q_spec, q_spec, pl.BlockSpec((1, G, 1, N_SEL), lambda b: (b, 0, 0, 0))],
        out_shape=[jax.ShapeDtypeStruct((B, G, R, HEAD_DIM), F32), jax.ShapeDtypeStruct((B, G, R, HEAD_DIM), F32),
                   jax.ShapeDtypeStruct((B, G, 1, N_SEL), jnp.int32)],
        compiler_params=_params(("parallel",)),
        name="attn_dec_dense",
    )(q, qr, ck, cv, kw, vw, omap)


DEC_BLOCKS_PER_STEP = 4


def _attn_dec_sel_kernel(pt_ref, ids_ref, qr_ref, *refs, n_past, pos):
    del pt_ref
    G = qr_ref.shape[1]
    nb = DEC_BLOCKS_PER_STEP
    kc_refs, vc_refs = refs[:G * nb], refs[G * nb:2 * G * nb]
    kn_ref, vn_ref, ocmp_ref, owin_ref, gate_ref, o_ref, m_ref, l_ref, acc_ref = refs[2 * G * nb:]
    b, n = pl.program_id(0), pl.program_id(1)

    @pl.when(n == 0)
    def _():
        m_ref[...] = jnp.full(m_ref.shape, NEG, F32)
        l_ref[...] = jnp.zeros(l_ref.shape, F32)
        acc_ref[...] = jnp.zeros(acc_ref.shape, F32)

    row = lax.broadcasted_iota(jnp.int32, (SEL_BLOCK, 1), 0)
    lane = lax.broadcasted_iota(jnp.int32, (1, SEL_BLOCK), 1)
    key_row = jnp.where(lane < SEL_BLOCK // 2, 2 * lane, 2 * lane - (SEL_BLOCK - 1))
    head_rows = lambda ref, g: jnp.concatenate([ref[0, :, g, :], ref[0, :, G + g, :]], axis=0)
    for g, u in [(g, u) for g in range(G) for u in range(nb)]:
        bid = ids_ref[(b * G + g) * N_SEL + n * nb + u]
        is_new = bid >= n_past
        first = (row == 0) & (bid == n_past)
        sl = slice(g * HEAD_DIM, (g + 1) * HEAD_DIM)
        k = jnp.where(is_new, jnp.where(first, kn_ref[0, :, sl], 0.0),
                      head_rows(kc_refs[g * nb + u], g)).astype(BF16)
        v = jnp.where(is_new, jnp.where(first, vn_ref[0, :, sl], 0.0),
                      head_rows(vc_refs[g * nb + u], g)).astype(BF16)
        kpos = bid * SEL_BLOCK + key_row
        ok = kpos <= pos
        s = jnp.where(ok, _dot_nt(qr_ref[0, g], k) * SCALE, NEG)
        m_prev = m_ref[g]
        m_new = jnp.maximum(m_prev, jnp.max(s, axis=1, keepdims=True))
        alpha = jnp.exp(m_prev - m_new)
        p = jnp.where(ok, jnp.exp(s - m_new), 0.0)
        l_ref[g] = alpha * l_ref[g] + jnp.sum(p, axis=1, keepdims=True)
        acc_ref[g] = alpha * acc_ref[g] + _dot(p.astype(BF16), v)
        m_ref[g] = m_new

    @pl.when(n == N_SEL // nb - 1)
    def _():
        for g in range(G):
            gt = gate_ref[0, g]
            o_sel = acc_ref[g] * (1.0 / l_ref[g])
            o_ref[0, g] = gt[:, 0:1] * ocmp_ref[0, g] + gt[:, 1:2] * o_sel + gt[:, 2:3] * owin_ref[0, g]


def attn_dec_sel(table, ids, qr, k_cache, v_cache, k_new, v_new, ocmp, owin, gates, *, pos):
    B, G, R, _ = qr.shape
    page = k_cache.shape[1]
    bpp = page // SEL_BLOCK
    n_past = table.shape[1] * bpp
    kc = k_cache.reshape(k_cache.shape[0] * bpp, SEL_BLOCK // 2, 2 * G, HEAD_DIM)
    vc = v_cache.reshape(v_cache.shape[0] * bpp, SEL_BLOCK // 2, 2 * G, HEAD_DIM)

    nb = DEC_BLOCKS_PER_STEP

    def cache_spec(g, u):
        def index(b, n, pt, ids_):
            bid = jnp.minimum(ids_[(b * G + g) * N_SEL + n * nb + u], n_past - 1)
            return (pt[b, bid // bpp] * bpp + bid % bpp, 0, 0, 0)
        return pl.BlockSpec((1, SEL_BLOCK // 2, 2 * G, HEAD_DIM), index)

    q_spec = pl.BlockSpec((1, G, R, HEAD_DIM), lambda b, n, pt, ids_: (b, 0, 0, 0))
    n_spec = pl.BlockSpec((1, 1, G * HEAD_DIM), lambda b, n, pt, ids_: (b, 0, 0))
    g_spec = pl.BlockSpec((1, G, R, LANE), lambda b, n, pt, ids_: (b, 0, 0, 0))
    c_specs = [cache_spec(g, u) for g in range(G) for u in range(nb)]
    return pl.pallas_call(
        functools.partial(_attn_dec_sel_kernel, n_past=n_past, pos=pos),
        grid_spec=pltpu.PrefetchScalarGridSpec(
            num_scalar_prefetch=2,
            grid=(B, N_SEL // nb),
            in_specs=[q_spec] + c_specs + c_specs + [n_spec, n_spec, q_spec, q_spec, g_spec],
            out_specs=q_spec,
            scratch_shapes=[pltpu.VMEM((G, R, 1), F32), pltpu.VMEM((G, R, 1), F32),
                            pltpu.VMEM((G, R, HEAD_DIM), F32)],
        ),
        out_shape=jax.ShapeDtypeStruct((B, G, R, HEAD_DIM), F32),
        compiler_params=_params(("parallel", "arbitrary")),
        name="attn_dec_sel",
    )(table, ids.reshape(-1), qr, *([kc] * (G * nb)), *([vc] * (G * nb)), k_new, v_new, ocmp, owin, gates)


def _rope_tables(pos):
    half = HEAD_DIM // 2
    inv = ROPE_THETA ** (-jnp.arange(half, dtype=F32) / half)
    ang = pos.astype(F32)[:, None] * inv[None, :]
    cos, sin = jnp.cos(ang), jnp.sin(ang)
    return jnp.concatenate([cos, cos], axis=1), jnp.concatenate([-sin, sin], axis=1)


def _prep_weights(W):
    D = W['w_kv'].shape[0]
    H = D // HEAD_DIM
    R = H // N_KV_HEADS
    pg = D // len(POOL_WINDOWS)
    half = CMP_STRIDE * HEAD_DIM
    P = {}
    P['w_pool'] = W['w_pool'].astype(BF16).reshape(-1, len(POOL_WINDOWS) * pg, pg)
    kv_gain = jnp.ones((W['w_kv'].shape[1], KV_W), F32)
    kv_gain = kv_gain.at[2].set(jnp.tile(W['g_k_sel'], N_KV_HEADS)).at[4].set(jnp.tile(W['g_k_win'], N_KV_HEADS))
    P['kv_gain'] = kv_gain
    for t in ('k', 'v'):
        w1 = W['w_cmp_%s1' % t].astype(BF16)
        P['w_cmp_%s1' % t] = w1
        P['w_cmp_%scat' % t] = jnp.concatenate([w1[:half], w1[half:]], axis=1)
        P['w_cmp_%s2' % t] = W['w_cmp_%s2' % t].astype(BF16)
        P['pe_%s' % t] = W['pe_cmp_%s' % t].astype(BF16).reshape(1, -1)
    n_b = W['w_qg'].shape[0]
    wg = W['w_qg'][:, :, H * HEAD_DIM:].astype(BF16).reshape(n_b, D, N_KV_HEADS, R * N_BRANCH)
    wg = jnp.pad(wg, ((0, 0), (0, 0), (0, 0), (0, LANE - R * N_BRANCH)))
    P['w_gate'] = wg.reshape(n_b, D, N_KV_HEADS * LANE)
    P['w_ple'] = W['w_ple'].astype(BF16)
    P[('w_up', 0)] = W['w_up'][0].astype(BF16)
    P[('w_qg', 0)] = W['w_qg'][0, :, :H * HEAD_DIM].astype(BF16)
    P[('w_kv2d', None)] = W['w_kv'].reshape(D, -1).astype(BF16)
    return P


def _dense(x, wname, layer, W, P, *, N, epilogue, specs, extras=(), out_dtypes, emit=False, side_cast=None, row_scale_in=None,
           norm_gains=None, mm_tn=1024, name):
    M, K = x.shape
    shapes = [d if isinstance(d, jax.ShapeDtypeStruct) else jax.ShapeDtypeStruct((M, N), d) for d in out_dtypes]
    key = (wname, layer)
    tmx = 1024 if M >= 1024 else M
    use_mm = key in P
    if use_mm:
        grid, tm, tn, tk = _tiles(M, N, K, tmx, mm_tn, 2048 if (M >= 1024 and K > 4096) else 4096)
    else:
        grid, tm, tn, tk = _tiles(M, N, K, tmx, 512, K if K <= 4096 else 2048)
    extra_specs, out_specs = specs(tm, tn)
    extras = list(extras)
    if row_scale_in is not None:
        epilogue = _row_scaled(epilogue)
        extras.insert(0, row_scale_in)
        extra_specs = [pl.BlockSpec((tm, LANE), lambda i, j, k: (i, 0))] + list(extra_specs)
    if norm_gains:
        epilogue = _norm_producer(epilogue, len(extras))
        n_ex, n_ex_specs, n_shapes, n_specs = _norm_io(norm_gains, M, N, grid[1], tm, tn)
        extras, extra_specs = extras + n_ex, list(extra_specs) + n_ex_specs
        shapes, out_specs = shapes + n_shapes, list(out_specs) + n_specs
    side = None if side_cast is None else (W[side_cast[0]], side_cast[1])
    if use_mm:
        outs = list(mm(x, P[key], grid=grid, tm=tm, tn=tn, tk=tk, epilogue=epilogue, extras=extras,
                       extra_specs=extra_specs, out_shapes=shapes, out_specs=out_specs, side=side, name=name))
    else:
        w = W[wname]
        outs = list(mm_ws(x, w, layer=layer if w.ndim == 3 else None, emit=emit, side=side, grid=grid, tm=tm,
                          tn=tn, tk=tk, epilogue=epilogue, extras=extras, extra_specs=extra_specs,
                          out_shapes=shapes, out_specs=out_specs, name=name))
    if side is not None:
        P[side_cast] = outs.pop()
    if emit and not use_mm:
        P[key] = outs.pop()
    if norm_gains:
        outs.append(row_scale(outs.pop(), N))
    return outs


def _ffn_ple(h, normed, p_l, layer, W, P, next_gains):
    M, D = h.shape
    F = W['w_up'].shape[2]
    host = M >= 1024
    next_up = ('w_up', layer + 1) if host and layer + 1 < W['w_up'].shape[0] else None
    mn = lambda tm, tn: ([], [_mn_spec(tm, tn)])
    res = lambda tm, tn: ([_mn_spec(tm, tn)], [_mn_spec(tm, tn)])
    y, rs = normed
    (u,) = _dense(y, 'w_up', layer, W, P, N=F, epilogue=lambda acc, rows: (jnp.square(jnp.maximum(acc, 0.0)),),
                  specs=mn, out_dtypes=[BF16], side_cast=('w_down', layer) if host else None, row_scale_in=rs,
                  name="ffn_up")
    h, y, rs = _dense(u, 'w_down', layer, W, P, N=D, epilogue=lambda acc, rows, r: (r[rows, :] + acc,), specs=res,
                      extras=[h], out_dtypes=[F32], norm_gains=[W['g_ple'][layer]], side_cast=next_up,
                      name="ffn_down")
    ple_dim = p_l.shape[1]
    ple_specs = lambda tm, tn: ([_mn_spec(tm, tn), pl.BlockSpec((tm, ple_dim), lambda i, j, k: (i, 0)),
                                 pl.BlockSpec((ple_dim, tn), lambda i, j, k: (0, j))], [_mn_spec(tm, tn)])
    return _dense(y, 'w_ple_gate', layer, W, P, N=D,
                  epilogue=lambda acc, rows, r, pp, wp: (r[rows, :] + _dot(pp[rows, :], wp[...]) * _sigmoid(acc),),
                  specs=ple_specs, extras=[h, p_l.astype(BF16), P['w_ple'][layer]], out_dtypes=[F32],
                  row_scale_in=rs, norm_gains=next_gains, name="ple")


def _kv_epilogue(acc, rows, gain_ref, cos_ref, sin_ref, *, n_kv):
    j = pl.program_id(1)
    cosf, sinf = cos_ref[rows, :], sin_ref[rows, :]
    heads = []
    for hh in range(N_KV_HEADS):
        sl = slice(hh * HEAD_DIM, (hh + 1) * HEAD_DIM)
        heads.append(_rope(_head_norm(acc[:, sl], gain_ref[0, :, sl]), cosf, sinf))
    out = jnp.where((j == 2) | (j == 4), jnp.concatenate(heads, axis=1), acc)
    return tuple((j == n, out) for n in range(n_kv)) + (out,)


def _q_epilogue(acc, rows, gq_ref, cos_ref, sin_ref):
    cosf, sinf = cos_ref[rows, :], sin_ref[rows, :]
    qs, qrs = [], []
    for hh in range(acc.shape[1] // HEAD_DIM):
        qn = _head_norm(acc[:, hh * HEAD_DIM:(hh + 1) * HEAD_DIM], gq_ref[...])
        qs.append(qn)
        qrs.append(_rope(qn, cosf, sinf))
    return jnp.concatenate(qs, axis=1), jnp.concatenate(qrs, axis=1)


def _trunk(x, p, pool_prefix, pos0, W, P, attend):
    B, T, D = x.shape
    M = B * T
    tmx = 1024 if M >= 1024 else M
    h = x.reshape(M, D)
    pg = D // len(POOL_WINDOWS)

    d, pool_new = pool_diff(x, W['g_mix'][0], pool_prefix[0], pos0)
    d, pool_new = d.reshape(M, D), pool_new[None]
    grid, tm, tn, tk = _tiles(M, D, pg, tmx, pg, pg)
    n_ex, n_ex_specs, n_shapes, n_specs = _norm_io([W['g_ffn'][0]], M, D, grid[1], tm, tn)
    h, y, ssq = mm(d, P['w_pool'][0], grid=grid, tm=tm, tn=tn, tk=tk,
                   x_map=lambda i, j, k: (i, j), w_map=lambda i, j, k: (j, 0),
                   epilogue=_norm_producer(lambda acc, rows, sc, r: (r[rows, :] + acc * sc[...],), 2),
                   extras=[W['pool_scale'][0].reshape(1, D), h] + n_ex,
                   extra_specs=[pl.BlockSpec((1, tn), lambda i, j, k: (0, j)), _mn_spec(tm, tn)] + n_ex_specs,
                   out_shapes=[jax.ShapeDtypeStruct((M, D), F32)] + n_shapes,
                   out_specs=[_mn_spec(tm, tn)] + n_specs, name="pool_mix")
    h, hkv, a1, rs = _ffn_ple(h, (y, row_scale(ssq, D)), p[0].reshape(M, -1), 0, W, P,
                              [W['g_kv'], W['g_mix'][1]])

    pos = pos0 + jnp.tile(jnp.arange(T, dtype=jnp.int32), B)
    cosf, sinf = _rope_tables(pos)
    n_kv = W['w_kv'].shape[1]
    rope_spec = lambda tm: pl.BlockSpec((tm, HEAD_DIM), lambda i, j, k: (i, 0))
    kv_spec = lambda tm: pl.BlockSpec((1, tm, KV_W), lambda i, j, k: (j, i, 0))
    kv4_spec = lambda tm: pl.BlockSpec((1, tm, N_KV_HEADS, HEAD_DIM), lambda i, j, k: (0, i, 0, 0))
    kv_specs = lambda tm, tn: ([pl.BlockSpec((1, 1, KV_W), lambda i, j, k: (j, 0, 0)), rope_spec(tm), rope_spec(tm)],
                               [kv4_spec(tm)] * n_kv + [kv_spec(tm)])
    *kv, kv_b = _dense(hkv, 'w_kv2d', None, W, P, N=n_kv * KV_W, epilogue=functools.partial(_kv_epilogue, n_kv=n_kv),
                       specs=kv_specs, extras=[P['kv_gain'].reshape(n_kv, 1, KV_W), cosf, sinf],
                       out_dtypes=[jax.ShapeDtypeStruct((1, M, N_KV_HEADS, HEAD_DIM), F32)] * n_kv
                       + [jax.ShapeDtypeStruct((n_kv, M, KV_W), BF16)], row_scale_in=rs, mm_tn=KV_W, name="kv_proj")
    kv = [a[0] for a in kv]
    q_specs = lambda tm, tn: ([pl.BlockSpec((1, HEAD_DIM), lambda i, j, k: (0, 0)), rope_spec(tm), rope_spec(tm)],
                              [_mn_spec(tm, tn)] * 2)
    q, qr = _dense(a1, 'w_qg', 0, W, P, N=D, epilogue=_q_epilogue, specs=q_specs,
                   extras=[W['g_q'][0].reshape(1, HEAD_DIM), cosf, sinf], out_dtypes=[BF16, BF16], row_scale_in=rs,
                   name="q_proj")
    ng = N_KV_HEADS * LANE
    grid, tm, tn, tk = _tiles(M, ng, D, tmx, ng, D)
    (gates,) = mm(a1, P['w_gate'][0], grid=grid, tm=tm, tn=tn, tk=tk,
                  epilogue=_row_scaled(lambda acc, rows: (_sigmoid(acc),)), extras=[rs],
                  extra_specs=[pl.BlockSpec((tm, LANE), lambda i, j, k: (i, 0))],
                  out_shapes=[jax.ShapeDtypeStruct((M, ng), F32)], out_specs=[_mn_spec(tm, tn)], name="gate_proj")

    o, win_state = attend(kv, kv_b, q, qr, gates)

    h, y, rs = _dense(o, 'w_o', 0, W, P, N=D, epilogue=lambda acc, rows, r: (r[rows, :] + acc,), extras=[h],
                      specs=lambda tm, tn: ([_mn_spec(tm, tn)], [_mn_spec(tm, tn)]), out_dtypes=[F32],
                      norm_gains=[W['g_ffn'][1]], name="attn_out")
    (h,) = _ffn_ple(h, (y, rs), p[1].reshape(M, -1), 1, W, P, [])
    rows = tuple(kv[n].reshape(B, T, N_KV_HEADS, HEAD_DIM) for n in range(4))
    return h.reshape(B, T, D), pool_new, rows, win_state


def _compress(k_pages, v_pages, table, cp, W, P):
    abk, abv = cmp_partials(k_pages, v_pages, table, P['w_cmp_kcat'], P['w_cmp_vcat'], cp)
    gain = W['g_k_cmp'].reshape(1, HEAD_DIM)
    ck = cmp_finish(abk, P['pe_k'], P['w_cmp_k1'], P['w_cmp_k2'], gain, norm=True)
    cv = cmp_finish(abv, P['pe_v'], P['w_cmp_v1'], P['w_cmp_v2'], gain, norm=False)
    return ck, cv


def kernel(x_prompt, x_sample, state_pool, cache_k_cmp, cache_v_cmp, cache_k_sel, cache_v_sel, state_k_win, state_v_win, page_table, p_prompt, p_sample, g_mix, w_pool, pool_scale, g_kv, w_kv, g_k_cmp, g_k_sel, g_k_win, w_cmp_k1, w_cmp_k2, pe_cmp_k, w_cmp_v1, w_cmp_v2, pe_cmp_v, w_qg, g_q, w_o, g_ffn, w_up, w_down, g_ple, w_ple, w_ple_gate):
    W = dict(g_mix=g_mix, w_pool=w_pool, pool_scale=pool_scale, g_kv=g_kv, w_kv=w_kv, g_k_cmp=g_k_cmp,
             g_k_sel=g_k_sel, g_k_win=g_k_win, w_cmp_k1=w_cmp_k1, w_cmp_k2=w_cmp_k2, pe_cmp_k=pe_cmp_k,
             w_cmp_v1=w_cmp_v1, w_cmp_v2=w_cmp_v2, pe_cmp_v=pe_cmp_v, w_qg=w_qg, g_q=g_q, w_o=w_o,
             g_ffn=g_ffn, w_up=w_up, w_down=w_down, g_ple=g_ple, w_ple=w_ple, w_ple_gate=w_ple_gate)
    P = _prep_weights(W)
    Bp, Tp, D = x_prompt.shape
    Bs, Ts, _ = x_sample.shape
    assert Ts == 1, "the decode path handles one new token per sequence"
    page = cache_k_cmp.shape[1]
    past_len = page_table.shape[1] * page
    R = D // HEAD_DIM // N_KV_HEADS
    assert Tp % page == 0 and past_len % SEL_BLOCK == 0

    def attend_prompt(kv, kv_b, q, qr, gates):
        ppb = Tp // page
        table = jnp.arange(Bp * ppb, dtype=jnp.int32).reshape(Bp, ppb)
        pages = lambda a: a.reshape(-1, page, N_KV_HEADS, HEAD_DIM)
        ck, cv = _compress(pages(kv[0]), pages(kv[1]), table, ppb, W, P)
        nc = Tp // CMP_STRIDE - CMP_BLOCK // CMP_STRIDE + 1
        seq = lambda a: a.reshape(Bp, Tp, -1)
        o = attn_prompt(seq(q), seq(qr), ck, cv, kv_b.reshape(-1, Bp, Tp, KV_W), seq(gates), nc=nc)
        nw = min(WINDOW, Tp)
        win = tuple(kv[n].reshape(Bp, Tp, N_KV_HEADS, HEAD_DIM)[:, -nw:] for n in (4, 5))
        return o.reshape(Bp * Tp, D), win

    def attend_sample(kv, kv_b, q, qr, gates):
        del kv_b
        ck, cv = _compress(cache_k_cmp, cache_v_cmp, page_table, min(32, page_table.shape[1]), W, P)
        nc = (past_len - (CMP_BLOCK - 1)) // CMP_STRIDE + 1
        ns = past_len // SEL_BLOCK + 1
        wb = state_k_win.shape[1]
        new_row = lambda a: a.reshape(Bs, 1, N_KV_HEADS, HEAD_DIM)
        kw = jnp.concatenate([state_k_win, new_row(kv[4])], axis=1)[:, -wb:]
        vw = jnp.concatenate([state_v_win, new_row(kv[5])], axis=1)[:, -wb:]
        heads = lambda a: a.reshape(Bs, N_KV_HEADS, R, HEAD_DIM)
        ocmp, owin, ids = attn_dec_dense(heads(q), heads(qr), ck, cv, kw, vw, nc=nc, ns=ns, pos=past_len)
        gt = gates.reshape(Bs, N_KV_HEADS, LANE)[:, :, :R * N_BRANCH].reshape(Bs, N_KV_HEADS, R, N_BRANCH)
        gt = jnp.pad(gt, ((0, 0), (0, 0), (0, 0), (0, LANE - N_BRANCH)))
        o = attn_dec_sel(page_table, ids, heads(qr), cache_k_sel, cache_v_sel,
                         kv[2].reshape(Bs, 1, KV_W), kv[3].reshape(Bs, 1, KV_W), ocmp, owin, gt, pos=past_len)
        return o.reshape(Bs, D).astype(BF16), (kw, vw)

    pool_zero = jnp.zeros((state_pool.shape[0], Bp, POOL_STATE, D), x_prompt.dtype)
    y_p, pool_p, rows_p, win_p = _trunk(x_prompt, p_prompt, pool_zero, 0, W, P, attend_prompt)
    y_s, pool_s, rows_s, win_s = _trunk(x_sample, p_sample, state_pool, past_len, W, P, attend_sample)
    return (y_p, y_s, pool_p, pool_s, rows_p[0], rows_p[1], rows_p[2], rows_p[3], win_p[0], win_p[1],
            rows_s[0], rows_s[1], rows_s[2], rows_s[3], win_s[0], win_s[1])
```

```python
import functools

import jax
import jax.numpy as jnp
import numpy as np
from jax import lax
from jax.experimental import pallas as pl
from jax.experimental.pallas import tpu as pltpu

F32 = jnp.float32
BF16 = jnp.bfloat16

POOL_WINDOWS = (2, 4, 8, 16)
POOL_STATE = max(POOL_WINDOWS) - 1
POOL_PAD = POOL_STATE + 1
HEAD_DIM = 128
N_KV_HEADS = 4
N_BRANCH = 3
CMP_BLOCK = 32
CMP_STRIDE = 16
CMP_HIDDEN = 2 * HEAD_DIM
SEL_BLOCK = 64
N_SEL = 16
WINDOW = 512
ROPE_THETA = 10000.0
EPS = 1e-6
SCALE = HEAD_DIM ** -0.5
NEG = -1e30
FORCE = 1e9
PAD_SCORE = -3e38
KV_W = N_KV_HEADS * HEAD_DIM
LANE = 128
VMEM_LIMIT = 56 * 1024 * 1024


def _params(sem):
    return pltpu.CompilerParams(dimension_semantics=sem, vmem_limit_bytes=VMEM_LIMIT)


def _sigmoid(x):
    return 1.0 / (1.0 + jnp.exp(-x))


def _dot(a, b):
    return jnp.dot(a, b, preferred_element_type=F32)


def _dot_nt(a, b):
    return lax.dot_general(a, b, (((1,), (1,)), ((), ())), preferred_element_type=F32)


def _dot_tn(a, b):
    return lax.dot_general(a, b, (((0,), (0,)), ((), ())), preferred_element_type=F32)


def _head_norm(x, g):
    return x * lax.rsqrt(jnp.mean(x * x, axis=-1, keepdims=True) + EPS) * g


def _rope(x, cosf, sinf):
    return x * cosf + pltpu.roll(x, HEAD_DIM // 2, 1) * sinf


def _fold_lanes(x):
    parts = [x[:, c * LANE:(c + 1) * LANE] for c in range(x.shape[1] // LANE)]
    return functools.reduce(lambda u, v: u + v, parts)


def _norm_producer(epilogue, n_base):
    def wrapped(acc, rows, *extras):
        (h,) = epilogue(acc, rows, *extras[:n_base])
        return (h, *[(h * g[...]).astype(BF16) for g in extras[n_base:]], _fold_lanes(h * h))
    return wrapped


def _row_scaled(epilogue):
    def wrapped(acc, rows, rs_ref, *extras):
        rs = rs_ref[rows, :]
        acc = jnp.concatenate([acc[:, c * LANE:(c + 1) * LANE] * rs for c in range(acc.shape[1] // LANE)], axis=1)
        return epilogue(acc, rows, *extras)
    return wrapped


def _row_scale_kernel(ssq_ref, o_ref, *, d):
    tot = jnp.sum(functools.reduce(lambda u, v: u + v, [ssq_ref[j] for j in range(ssq_ref.shape[0])]),
                  axis=1, keepdims=True)
    o_ref[...] = jnp.broadcast_to(lax.rsqrt(tot / d + EPS), o_ref.shape)


def row_scale(ssq, d):
    gn, M, _ = ssq.shape
    tm = min(M, 1024)
    return pl.pallas_call(
        functools.partial(_row_scale_kernel, d=d),
        grid=(M // tm,),
        in_specs=[pl.BlockSpec((gn, tm, LANE), lambda i: (0, i, 0))],
        out_specs=pl.BlockSpec((tm, LANE), lambda i: (i, 0)),
        out_shape=jax.ShapeDtypeStruct((M, LANE), F32),
        compiler_params=_params(("parallel",)),
        name="row_scale",
    )(ssq)


def _norm_io(gains, M, N, gn, tm, tn):
    extras = [g.reshape(1, N).astype(F32) for g in gains]
    extra_specs = [pl.BlockSpec((1, tn), lambda i, j, k: (0, j)) for _ in gains]
    shapes = [jax.ShapeDtypeStruct((M, N), BF16) for _ in gains] + [jax.ShapeDtypeStruct((gn, M, LANE), F32)]
    specs = [_mn_spec(tm, tn) for _ in gains] + [pl.BlockSpec((1, tm, LANE), lambda i, j, k: (j, i, 0))]
    return extras, extra_specs, shapes, specs


EPILOGUE_ROWS = 256


def _store_rows(o_ref, rows, r):
    if len(o_ref.shape) == 4:
        for hh in range(o_ref.shape[2]):
            o_ref[0, rows, hh, :] = r[:, hh * HEAD_DIM:(hh + 1) * HEAD_DIM].astype(o_ref.dtype)
    elif len(o_ref.shape) == 3:
        o_ref[0, rows, :] = r.astype(o_ref.dtype)
    else:
        o_ref[rows, :] = r.astype(o_ref.dtype)


def _finish_rows(acc_of, tm, outs, extras, epilogue):
    ch = min(tm, EPILOGUE_ROWS)
    for c in range(tm // ch):
        rows = slice(c * ch, (c + 1) * ch)
        for o_ref, r in zip(outs, epilogue(acc_of(rows), rows, *extras)):
            _store_rows(o_ref, rows, r)


def _k_steps(x_ref, w, acc_ref, nk, outs, extras, epilogue):
    k = pl.program_id(2)

    @pl.when(k == 0)
    def _():
        acc_ref[...] = _dot(x_ref[...], w())

    @pl.when((k > 0) & (k < nk - 1))
    def _():
        acc_ref[...] += _dot(x_ref[...], w())

    @pl.when(k == nk - 1)
    def _():
        _finish_rows(lambda rows: acc_ref[rows, :] + _dot(x_ref[rows, :], w()), x_ref.shape[0], outs, extras,
                     epilogue)


def _side_io(side, n_steps, step):
    s_arr, s_layer = side
    _, rows, cols = s_arr.shape
    rs = rows // n_steps
    assert rs * n_steps == rows and rs % 16 == 0
    in_spec = pl.BlockSpec((None, rs, cols), lambda *g: (s_layer, step(*g), 0))
    out_spec = pl.BlockSpec((rs, cols), lambda *g: (step(*g), 0))
    return s_arr, in_spec, jax.ShapeDtypeStruct((rows, cols), BF16), out_spec


def _mm_kernel(*refs, nk, n_extra, n_out, epilogue, side):
    x_ref, w_ref = refs[0], refs[1]
    extras = refs[2:2 + n_extra]
    n_in = 2 + n_extra + side
    outs = refs[n_in:n_in + n_out]
    tm = x_ref.shape[0]
    if side:
        refs[n_in + n_out][...] = refs[n_in - 1][...].astype(BF16)

    if nk == 1:
        _finish_rows(lambda rows: _dot(x_ref[rows, :], w_ref[...]), tm, outs, extras, epilogue)
    else:
        _k_steps(x_ref, lambda: w_ref[...], refs[-1], nk, outs, extras, epilogue)


def mm(x, w, *, grid, tm, tn, tk, epilogue, extras=(), extra_specs=(), out_shapes, out_specs,
       x_map=None, w_map=None, side=None, name):
    gm, gn, nk = grid
    x_map = x_map or (lambda i, j, k: (i, k))
    w_map = w_map or (lambda i, j, k: (k, j))
    ins, in_specs = [x, w, *extras], [pl.BlockSpec((tm, tk), x_map), pl.BlockSpec((tk, tn), w_map)]
    in_specs += list(extra_specs)
    n_out, out_shapes, out_specs = len(out_shapes), list(out_shapes), list(out_specs)
    if side is not None:
        s_arr, s_in, s_shape, s_out = _side_io(side, gm * gn * nk, lambda i, j, k: (i * gn + j) * nk + k)
        ins.append(s_arr), in_specs.append(s_in), out_shapes.append(s_shape), out_specs.append(s_out)
    kern = functools.partial(_mm_kernel, nk=nk, n_extra=len(extras), n_out=n_out, epilogue=epilogue,
                             side=side is not None)
    return pl.pallas_call(
        kern,
        grid=grid,
        in_specs=in_specs,
        out_specs=out_specs,
        out_shape=out_shapes,
        scratch_shapes=[pltpu.VMEM((tm, tn), F32)] if nk > 1 else [],
        compiler_params=_params(("parallel", "parallel", "arbitrary")),
        name=name,
    )(*ins)


def _mm_ws_kernel(*refs, nk, n_extra, n_out, epilogue, side):
    x_ref, w_ref = refs[0], refs[1]
    extras = refs[2:2 + n_extra]
    n_in = 2 + n_extra + side
    outs = refs[n_in:n_in + n_out]
    rest = refs[n_in + n_out:]
    wb_ref = rest[side]
    i, k = pl.program_id(1), pl.program_id(2)

    @pl.when(i == 0)
    def _():
        wb_ref[k] = w_ref[...].astype(BF16)

    if side:
        rest[0][...] = refs[n_in - 1][...].astype(BF16)

    tm = x_ref.shape[0]
    if nk == 1:
        _finish_rows(lambda rows: _dot(x_ref[rows, :], wb_ref[0]), tm, outs, extras, epilogue)
    else:
        _k_steps(x_ref, lambda: wb_ref[k], rest[-1], nk, outs, extras, epilogue)


def mm_ws(x, w, *, layer=None, side=None, grid, tm, tn, tk, epilogue, extras=(), extra_specs=(),
          out_shapes, out_specs, name):
    gm, gn, nk = grid
    swap = lambda f: (lambda j, i, k: f(i, j, k))
    respec = lambda s: pl.BlockSpec(s.block_shape, swap(s.index_map))
    k_once = lambda i, k: jnp.where(i == 0, k, nk - 1)
    if layer is None:
        w_spec = pl.BlockSpec((tk, tn), lambda j, i, k: (k_once(i, k), j))
    else:
        w_spec = pl.BlockSpec((None, tk, tn), lambda j, i, k: (layer, k_once(i, k), j))
    n_out = len(out_shapes)
    ins, in_specs = [x, w, *extras], [pl.BlockSpec((tm, tk), lambda j, i, k: (i, k)), w_spec]
    in_specs += [respec(s) for s in extra_specs]
    out_shapes, out_specs = list(out_shapes), [respec(s) for s in out_specs]
    if side is not None:
        s_arr, s_in, s_shape, s_out = _side_io(side, gm * gn * nk, lambda j, i, k: (j * gm + i) * nk + k)
        ins.append(s_arr), in_specs.append(s_in), out_shapes.append(s_shape), out_specs.append(s_out)
    kern = functools.partial(_mm_ws_kernel, nk=nk, n_extra=len(extras), n_out=n_out, epilogue=epilogue,
                             side=side is not None)
    return pl.pallas_call(
        kern,
        grid=(gn, gm, nk),
        in_specs=in_specs,
        out_specs=out_specs,
        out_shape=out_shapes,
        scratch_shapes=[pltpu.VMEM((nk, tk, tn), BF16)] + ([pltpu.VMEM((tm, tn), F32)] if nk > 1 else []),
        compiler_params=_params(("parallel", "arbitrary", "arbitrary")),
        name=name,
    )(*ins)


def _tiles(M, N, K, tm, tn, tk):
    tm, tn, tk = min(tm, M), min(tn, N), min(tk, K)
    return (M // tm, N // tn, K // tk), tm, tn, tk


def _mn_spec(tm, tn):
    return pl.BlockSpec((tm, tn), lambda i, j, k: (i, j))


def _pool_diff_kernel(*refs, tt, pos0, halo):
    x_ref, pre_ref, g_ref = refs[0], refs[1 + halo], refs[2 + halo]
    d_ref, st_ref, seq_ref = refs[3 + halo:]
    t = pl.program_id(1)
    norm = lambda x: x * lax.rsqrt(jnp.mean(x * x, axis=-1, keepdims=True) + EPS) * g_ref[...]
    a = norm(x_ref[0])
    seq_ref[POOL_PAD:POOL_PAD + tt, :] = a

    @pl.when(t == 0)
    def _():
        seq_ref[0:POOL_PAD, :] = pre_ref[0]

    if halo:
        @pl.when(t > 0)
        def _():
            seq_ref[0:POOL_PAD, :] = norm(refs[1][0])

    pos = pos0 + t * tt + lax.broadcasted_iota(jnp.int32, (tt, 1), 0)
    pg = a.shape[1] // len(POOL_WINDOWS)
    for g, w in enumerate(POOL_WINDOWS):
        cols = slice(g * pg, (g + 1) * pg)
        s = a[:, cols]
        for j in range(1, w):
            s = s + seq_ref[POOL_PAD - j:POOL_PAD - j + tt, cols]
        cnt = jnp.minimum(pos + 1, w).astype(F32)
        d_ref[0, :, cols] = (s / cnt - a[:, cols]).astype(d_ref.dtype)
    st_ref[0] = seq_ref[tt:tt + POOL_PAD, :]


def pool_diff(x, gain, prefix, pos0):
    B, T, D = x.shape
    tt = min(T, 256)
    halo = T > tt
    pre = jnp.concatenate([jnp.zeros((B, 1, D), F32), prefix], axis=1)
    hpt = tt // POOL_PAD
    in_specs = [pl.BlockSpec((1, tt, D), lambda b, t: (b, t, 0))]
    if halo:
        in_specs.append(pl.BlockSpec((1, POOL_PAD, D), lambda b, t: (b, jnp.maximum(t * hpt - 1, 0), 0)))
    in_specs += [pl.BlockSpec((1, POOL_PAD, D), lambda b, t: (b, 0, 0)), pl.BlockSpec((1, D), lambda b, t: (0, 0))]
    d, st = pl.pallas_call(
        functools.partial(_pool_diff_kernel, tt=tt, pos0=pos0, halo=halo),
        grid=(B, T // tt),
        in_specs=in_specs,
        out_specs=[pl.BlockSpec((1, tt, D), lambda b, t: (b, t, 0)),
                   pl.BlockSpec((1, POOL_PAD, D), lambda b, t: (b, 0, 0))],
        out_shape=[jax.ShapeDtypeStruct((B, T, D), BF16), jax.ShapeDtypeStruct((B, POOL_PAD, D), F32)],
        scratch_shapes=[pltpu.VMEM((POOL_PAD + tt, D), F32)],
        compiler_params=_params(("parallel", "arbitrary")),
        name="pool_diff",
    )(*([x, x] if halo else [x]), pre, gain.reshape(1, D).astype(F32))
    return d, st[:, 1:]


PAGES_PER_STEP = 8
CMP_FINISH_ROWS = 512


def _cmp_partial_kernel(pt_ref, *refs, steps, pps):
    del pt_ref
    k_pages, v_pages = refs[:pps], refs[pps:2 * pps]
    wk_ref, wv_ref, abk_ref, abv_ref, xk_ref, xv_ref = refs[2 * pps:]
    p = pl.program_id(2)
    G = N_KV_HEADS
    half = CMP_STRIDE // 2
    rows_per_page = (k_pages[0].shape[1] // half) * G
    for pages, x_ref in ((k_pages, xk_ref), (v_pages, xv_ref)):
        for q, page_ref in enumerate(pages):
            for sb in range(page_ref.shape[1] // half):
                row = q * rows_per_page + sb * G
                for r in range(CMP_STRIDE):
                    x_ref[p, row:row + G, r * HEAD_DIM:(r + 1) * HEAD_DIM] = (
                        page_ref[0, sb * half + r // 2, (r % 2) * G:(r % 2 + 1) * G, :])

    @pl.when(p == steps - 1)
    def _():
        for x_ref, w_ref, ab_ref in ((xk_ref, wk_ref, abk_ref), (xv_ref, wv_ref, abv_ref)):
            x = x_ref[...].reshape(ab_ref.shape[1], x_ref.shape[2])
            ab_ref[0] = _dot(x.astype(BF16), w_ref[...])


def cmp_partials(k_pages, v_pages, table, wk_cat, wv_cat, cp):
    B, ppb = table.shape
    page = k_pages.shape[1]
    pps = PAGES_PER_STEP
    rows_per_page = page // CMP_STRIDE * N_KV_HEADS
    nch, steps = ppb // cp, cp // pps
    m = cp * rows_per_page
    kdim = CMP_STRIDE * HEAD_DIM
    pair_rows = lambda a: a.reshape(a.shape[0], page // 2, 2 * N_KV_HEADS, HEAD_DIM)

    def page_spec(q):
        return pl.BlockSpec((1, page // 2, 2 * N_KV_HEADS, HEAD_DIM),
                            lambda b, c, p, pt: (pt[b, c * cp + p * pps + q], 0, 0, 0))

    page_specs = [page_spec(q) for q in range(pps)]
    w_spec = pl.BlockSpec((kdim, 2 * CMP_HIDDEN), lambda b, c, p, pt: (0, 0))
    out_spec = pl.BlockSpec((1, m, 2 * CMP_HIDDEN), lambda b, c, p, pt: (b, c, 0))
    out_shape = jax.ShapeDtypeStruct((B, ppb * rows_per_page, 2 * CMP_HIDDEN), F32)
    x_scratch = pltpu.VMEM((steps, pps * rows_per_page, kdim), F32)
    return pl.pallas_call(
        functools.partial(_cmp_partial_kernel, steps=steps, pps=pps),
        grid_spec=pltpu.PrefetchScalarGridSpec(
            num_scalar_prefetch=1,
            grid=(B, nch, steps),
            in_specs=page_specs + page_specs + [w_spec, w_spec],
            out_specs=[out_spec, out_spec],
            scratch_shapes=[x_scratch, x_scratch],
        ),
        out_shape=[out_shape, out_shape],
        compiler_params=_params(("parallel", "parallel", "arbitrary")),
        name="cmp_partials",
    )(table, *([pair_rows(k_pages)] * pps), *([pair_rows(v_pages)] * pps), wk_cat, wv_cat)


def _cmp_finish_kernel(ab_ref, pe_ref, w1_ref, w2_ref, g_ref, o_ref, out_ref, *, norm):
    n4 = ab_ref.shape[1]
    G = o_ref.shape[1]
    bias = _dot(jnp.broadcast_to(pe_ref[...], (8, pe_ref.shape[1])), w1_ref[...])[0:1, :]
    ch = min(n4, CMP_FINISH_ROWS)
    for c in range(n4 // ch):
        lo, hi = c * ch, (c + 1) * ch
        first = ab_ref[0, lo:hi, :CMP_HIDDEN]
        if hi + G <= n4:
            second = ab_ref[0, lo + G:hi + G, CMP_HIDDEN:]
        else:
            second = jnp.concatenate([ab_ref[0, lo + G:hi, CMP_HIDDEN:], ab_ref[0, hi - G:hi, CMP_HIDDEN:]], axis=0)
        pre = first + second + bias
        out = _dot((pre * _sigmoid(pre)).astype(BF16), w2_ref[...])
        if norm:
            out = _head_norm(out, g_ref[...])
        out_ref[lo:hi, :] = out
    for g in range(G):
        o_ref[0, g] = out_ref[pl.ds(g, n4 // G, stride=G), :].astype(o_ref.dtype)


def cmp_finish(ab, pe, w1, w2, gain, *, norm):
    B, n4, _ = ab.shape
    G = N_KV_HEADS
    full = lambda a: pl.BlockSpec(a.shape, lambda b: (0,) * a.ndim)
    args = (pe, w1, w2, gain)
    return pl.pallas_call(
        functools.partial(_cmp_finish_kernel, norm=norm),
        grid=(B,),
        in_specs=[pl.BlockSpec((1, n4, 2 * CMP_HIDDEN), lambda b: (b, 0, 0))] + [full(a) for a in args],
        out_specs=pl.BlockSpec((1, G, n4 // G, HEAD_DIM), lambda b: (b, 0, 0, 0)),
        out_shape=jax.ShapeDtypeStruct((B, G, n4 // G, HEAD_DIM), BF16),
        scratch_shapes=[pltpu.VMEM((n4, HEAD_DIM), F32)],
        compiler_params=_params(("parallel",)),
        name="cmp_finish",
    )(ab, *args)


def _select_blocks(score, blk, ns):
    rank = jnp.zeros(score.shape, jnp.int32)
    for j in range(ns):
        sj = score[j:j + 1, :]
        beats = (sj > score) | ((sj == score) & (j < blk))
        rank = rank + beats.astype(jnp.int32)
    return (rank < min(N_SEL, ns)) & (score > 0.5 * NEG)


def _attn_prompt_kernel(q_ref, qr_ref, ck_ref, cv_ref, ks_ref, vs_ref, kw_ref, vw_ref, gate_ref, mapT_ref,
                        expand_ref, o_ref, part_ref, sbias_ref, wbias_ref, m_ref, acc_ref, *, tq, kc, nc, ns, R):
    qi = pl.program_id(2)
    q0 = qi * tq
    pos = q0 + lax.broadcasted_iota(jnp.int32, (tq, 1), 0)
    ncp = ck_ref.shape[2]

    ck = ck_ref[0, 0]
    cv = cv_ref[0, 0]
    cidx = lax.broadcasted_iota(jnp.int32, (1, ncp), 1)
    ok_c = (cidx * CMP_STRIDE + CMP_BLOCK - 1 <= pos) & (cidx < nc)
    imp = jnp.zeros((tq, ncp), F32)
    for r in range(R):
        qh = q_ref[0, :, r * HEAD_DIM:(r + 1) * HEAD_DIM]
        s = jnp.where(ok_c, _dot_nt(qh, ck) * SCALE, NEG)
        e = jnp.exp(s - jnp.max(s, axis=1, keepdims=True))
        p = jnp.where(ok_c, e * (1.0 / jnp.sum(e, axis=1, keepdims=True)), 0.0)
        imp = imp + p
        part_ref[r] = gate_ref[0, :, r * N_BRANCH:r * N_BRANCH + 1] * _dot(p.astype(BF16), cv)

    nsp = mapT_ref.shape[0]
    p_slc = lax.dot_general(mapT_ref[...], imp, (((1,), (1,)), ((), ())), precision=lax.Precision.HIGHEST,
                            preferred_element_type=F32)
    blk = lax.broadcasted_iota(jnp.int32, (nsp, tq), 0)
    pos_l = q0 + lax.broadcasted_iota(jnp.int32, (nsp, tq), 1)
    cur = pos_l // SEL_BLOCK
    vis = blk * SEL_BLOCK <= pos_l
    forced = vis & ((blk == 0) | (blk == cur) | (blk == cur - 1))
    score = jnp.where(forced, FORCE, jnp.where(vis, p_slc, NEG))
    score = jnp.where(blk < ns, score, PAD_SCORE)
    sel = _select_blocks(score, blk, ns).astype(BF16)

    c_hi = (q0 + tq) // kc
    col = lax.broadcasted_iota(jnp.int32, (1, kc), 1)
    sel_keys = _dot_tn(sel, expand_ref[...])
    for c in range(sbias_ref.shape[0]):
        @pl.when(c < c_hi)
        def _(c=c):
            ok = (sel_keys[:, c * kc:(c + 1) * kc] > 0.5) & (c * kc + col <= pos)
            sbias_ref[c] = jnp.where(ok, 0.0, NEG)
    nwc = wbias_ref.shape[0]
    c_w0 = c_hi - nwc
    for d in range(nwc):
        kpos = (c_w0 + d) * kc + col
        wbias_ref[d] = jnp.where((kpos <= pos) & (kpos > pos - WINDOW), 0.0, NEG)

    def fold(t):
        return [t[:, i * LANE:(i + 1) * LANE] for i in range(kc // LANE)]

    ones_blk = jnp.ones((kc, LANE), BF16)

    def branch(k_ref, v_ref, c_lo, bias_of, gate_col):
        def logits(r, c, bias):
            k = k_ref[0, pl.ds(pl.multiple_of(c * kc, kc), kc), :]
            return _dot_nt(qr_ref[0, :, r * HEAD_DIM:(r + 1) * HEAD_DIM], k) * SCALE + bias

        m_ref[...] = jnp.full(m_ref.shape, NEG, F32)
        acc_ref[...] = jnp.zeros(acc_ref.shape, F32)

        def over_chunks(body):
            n = c_hi - c_lo

            def pair(i, carry):
                body(c_lo + 2 * i, carry)
                return body(c_lo + 2 * i + 1, carry)

            lax.fori_loop(0, n // 2, pair, 0)

            @pl.when(n % 2 == 1)
            def _():
                body(c_hi - 1, 0)

        def max_body(c, carry):
            bias = bias_of(c)
            for r in range(R):
                mx = m_ref[r]
                for part in fold(logits(r, c, bias)):
                    mx = jnp.maximum(mx, part)
                m_ref[r] = mx
            return carry

        over_chunks(max_body)
        for r in range(R):
            m_ref[r] = jnp.broadcast_to(jnp.max(m_ref[r], axis=1, keepdims=True), (tq, LANE))

        def sum_body(c, carry):
            bias = bias_of(c)
            v = jnp.concatenate([v_ref[0, pl.ds(pl.multiple_of(c * kc, kc), kc), :], ones_blk], axis=1)
            for r in range(R):
                t = logits(r, c, bias)
                m = m_ref[r]
                ps = [jnp.exp(part - m) for part in fold(t)]
                acc_ref[r] += _dot(jnp.concatenate(ps, axis=1).astype(BF16), v)
            return carry

        over_chunks(sum_body)
        for r in range(R):
            gate = gate_ref[0, :, r * N_BRANCH + gate_col:r * N_BRANCH + gate_col + 1]
            part_ref[r] += gate * (acc_ref[r, :, :HEAD_DIM] * (1.0 / acc_ref[r, :, HEAD_DIM:]))

    branch(ks_ref, vs_ref, 0, lambda c: sbias_ref[c], 1)
    branch(kw_ref, vw_ref, jnp.maximum(c_w0, 0), lambda c: wbias_ref[c - c_w0], 2)
    for r in range(R):
        o_ref[0, :, r * HEAD_DIM:(r + 1) * HEAD_DIM] = part_ref[r].astype(o_ref.dtype)


def _overlap_map(ncp, nsp, ns):
    ratio = CMP_BLOCK // CMP_STRIDE
    per_sel = SEL_BLOCK // CMP_STRIDE
    m = np.zeros((ncp, nsp), np.float32)
    for b in range(ns):
        for mm_ in range(per_sel):
            for n in range(ratio):
                j = per_sel * b + mm_ - n
                if 0 <= j < ncp:
                    m[j, b] += 1.0
    return m


def attn_prompt(q, qr, ck, cv, kvb, gates, *, nc):
    B, T, HD = q.shape
    G = N_KV_HEADS
    R = HD // HEAD_DIM // G
    tq = min(T, 256)
    kc = tq
    ns = T // SEL_BLOCK
    nsp = -(-ns // 8) * 8
    ncp = ck.shape[2]
    mapT = jnp.asarray(_overlap_map(ncp, nsp, ns).T)
    expand = jnp.asarray((np.arange(T)[None, :] // SEL_BLOCK == np.arange(nsp)[:, None]).astype(np.float32), BF16)
    q_spec = pl.BlockSpec((1, tq, R * HEAD_DIM), lambda b, g, i: (b, i, g))
    c_spec = pl.BlockSpec((1, 1, ncp, HEAD_DIM), lambda b, g, i: (b, g, 0, 0))
    kv_spec = lambda n: pl.BlockSpec((None, 1, T, HEAD_DIM), lambda b, g, i: (n, b, 0, g))
    return pl.pallas_call(
        functools.partial(_attn_prompt_kernel, tq=tq, kc=kc, nc=nc, ns=ns, R=R),
        grid=(B, G, T // tq),
        in_specs=[q_spec, q_spec, c_spec, c_spec, kv_spec(2), kv_spec(3), kv_spec(4), kv_spec(5),
                  pl.BlockSpec((1, tq, LANE), lambda b, g, i: (b, i, g)),
                  pl.BlockSpec((nsp, ncp), lambda b, g, i: (0, 0)),
                  pl.BlockSpec((nsp, T), lambda b, g, i: (0, 0))],
        out_specs=q_spec,
        out_shape=jax.ShapeDtypeStruct((B, T, HD), BF16),
        scratch_shapes=[pltpu.VMEM((R, tq, HEAD_DIM), F32), pltpu.VMEM((T // kc, tq, kc), F32),
                        pltpu.VMEM((min(WINDOW, T) // kc + tq // kc, tq, kc), F32)]
                       + [pltpu.VMEM((R, tq, HEAD_DIM), F32), pltpu.VMEM((R, tq, HEAD_DIM + LANE), F32)],
        compiler_params=_params(("parallel", "parallel", "arbitrary")),
        name="attn_prompt",
    )(q, qr, ck, cv, kvb, kvb, kvb, kvb, gates, mapT, expand)


def _attn_dec_dense_kernel(q_ref, qr_ref, ck_ref, cv_ref, kw_ref, vw_ref, map_ref, ocmp_ref, owin_ref, ids_ref,
                           *, nc, ns, pos):
    G, R = q_ref.shape[1], q_ref.shape[2]
    ncp = ck_ref.shape[2]
    nsl = map_ref.shape[1]
    cidx = lax.broadcasted_iota(jnp.int32, (1, ncp), 1)
    ok_c = (cidx * CMP_STRIDE + CMP_BLOCK - 1 <= pos) & (cidx < nc)
    blk_l = lax.broadcasted_iota(jnp.int32, (1, nsl), 1)
    cur = pos // SEL_BLOCK
    vis = blk_l * SEL_BLOCK <= pos
    forced = vis & ((blk_l == 0) | (blk_l == cur) | (blk_l == cur - 1))
    ii = lax.broadcasted_iota(jnp.int32, (nsl, nsl), 0)
    jj = lax.broadcasted_iota(jnp.int32, (nsl, nsl), 1)
    slot = lax.broadcasted_iota(jnp.int32, (nsl, LANE), 1).astype(F32)
    blk_s = lax.broadcasted_iota(jnp.int32, (nsl, LANE), 0).astype(F32)
    for g in range(G):
        s = jnp.where(ok_c, _dot_nt(q_ref[0, g], ck_ref[0, g]) * SCALE, NEG)
        e = jnp.exp(s - jnp.max(s, axis=1, keepdims=True))
        p = jnp.where(ok_c, e * (1.0 / jnp.sum(e, axis=1, keepdims=True)), 0.0)
        ocmp_ref[0, g] = _dot(p.astype(BF16), cv_ref[0, g])
        imp = jnp.broadcast_to(jnp.sum(p, axis=0, keepdims=True), (R, ncp))
        p_slc = jnp.dot(imp, map_ref[...], precision=lax.Precision.HIGHEST, preferred_element_type=F32)[0:1, :]
        score_l = jnp.where(forced, FORCE, jnp.where(vis, p_slc, NEG))
        score_l = jnp.where(blk_l < ns, score_l, PAD_SCORE)
        score_s = jnp.sum(jnp.where(ii == jj, score_l, 0.0), axis=1, keepdims=True)
        beats = (score_l > score_s) | ((score_l == score_s) & (jj < ii))
        rank = jnp.sum(beats.astype(F32), axis=1, keepdims=True)
        ids = jnp.sum(jnp.where(rank == slot, blk_s, 0.0), axis=0, keepdims=True)
        ids_ref[0, g] = ids[:, :N_SEL].astype(jnp.int32)
        kw = kw_ref[0, :, g, :].astype(BF16)
        vw = vw_ref[0, :, g, :].astype(BF16)
        s = _dot_nt(qr_ref[0, g], kw) * SCALE
        e = jnp.exp(s - jnp.max(s, axis=1, keepdims=True))
        p = e * (1.0 / jnp.sum(e, axis=1, keepdims=True))
        owin_ref[0, g] = _dot(p.astype(BF16), vw)


def attn_dec_dense(q, qr, ck, cv, kw, vw, *, nc, ns, pos):
    B, G, R, _ = q.shape
    ncp = ck.shape[2]
    wb = kw.shape[1]
    nsl = -(-ns // LANE) * LANE
    omap = jnp.asarray(_overlap_map(ncp, nsl, ns))
    q_spec = pl.BlockSpec((1, G, R, HEAD_DIM), lambda b: (b, 0, 0, 0))
    c_spec = pl.BlockSpec((1, G, ncp, HEAD_DIM), lambda b: (b, 0, 0, 0))
    w_spec = pl.BlockSpec((1, wb, G, HEAD_DIM), lambda b: (b, 0, 0, 0))
    return pl.pallas_call(
        functools.partial(_attn_dec_dense_kernel, nc=nc, ns=ns, pos=pos),
        grid=(B,),
        in_specs=[q_spec, q_spec, c_spec, c_spec, w_spec, w_spec, pl.BlockSpec((ncp, nsl), lambda b: (0, 0))],
        out_specs=[q_spec, q_spec, pl.BlockSpec((1, G, 1, N_SEL), lambda b: (b, 0, 0, 0))],
        out_shape=[jax.ShapeDtypeStruct((B, G, R, HEAD_DIM), F32), jax.ShapeDtypeStruct((B, G, R, HEAD_DIM), F32),
                   jax.ShapeDtypeStruct((B, G, 1, N_SEL), jnp.int32)],
        compiler_params=_params(("parallel",)),
        name="attn_dec_dense",
    )(q, qr, ck, cv, kw, vw, omap)


DEC_BLOCKS_PER_STEP = 4


def _attn_dec_sel_kernel(pt_ref, ids_ref, qr_ref, *refs, n_past, pos):
    del pt_ref
    G = qr_ref.shape[1]
    nb = DEC_BLOCKS_PER_STEP
    kc_refs, vc_refs = refs[:G * nb], refs[G * nb:2 * G * nb]
    kn_ref, vn_ref, ocmp_ref, owin_ref, gate_ref, o_ref, m_ref, l_ref, acc_ref = refs[2 * G * nb:]
    b, n = pl.program_id(0), pl.program_id(1)

    @pl.when(n == 0)
    def _():
        m_ref[...] = jnp.full(m_ref.shape, NEG, F32)
        l_ref[...] = jnp.zeros(l_ref.shape, F32)
        acc_ref[...] = jnp.zeros(acc_ref.shape, F32)

    row = lax.broadcasted_iota(jnp.int32, (SEL_BLOCK, 1), 0)
    lane = lax.broadcasted_iota(jnp.int32, (1, SEL_BLOCK), 1)
    key_row = jnp.where(lane < SEL_BLOCK // 2, 2 * lane, 2 * lane - (SEL_BLOCK - 1))
    head_rows = lambda ref, g: jnp.concatenate([ref[0, :, g, :], ref[0, :, G + g, :]], axis=0)
    for g, u in [(g, u) for g in range(G) for u in range(nb)]:
        bid = ids_ref[(b * G + g) * N_SEL + n * nb + u]
        is_new = bid >= n_past
        first = (row == 0) & (bid == n_past)
        sl = slice(g * HEAD_DIM, (g + 1) * HEAD_DIM)
        k = jnp.where(is_new, jnp.where(first, kn_ref[0, :, sl], 0.0),
                      head_rows(kc_refs[g * nb + u], g)).astype(BF16)
        v = jnp.where(is_new, jnp.where(first, vn_ref[0, :, sl], 0.0),
                      head_rows(vc_refs[g * nb + u], g)).astype(BF16)
        kpos = bid * SEL_BLOCK + key_row
        ok = kpos <= pos
        s = jnp.where(ok, _dot_nt(qr_ref[0, g], k) * SCALE, NEG)
        m_prev = m_ref[g]
        m_new = jnp.maximum(m_prev, jnp.max(s, axis=1, keepdims=True))
        alpha = jnp.exp(m_prev - m_new)
        p = jnp.where(ok, jnp.exp(s - m_new), 0.0)
        l_ref[g] = alpha * l_ref[g] + jnp.sum(p, axis=1, keepdims=True)
        acc_ref[g] = alpha * acc_ref[g] + _dot(p.astype(BF16), v)
        m_ref[g] = m_new

    @pl.when(n == N_SEL // nb - 1)
    def _():
        for g in range(G):
            gt = gate_ref[0, g]
            o_sel = acc_ref[g] * (1.0 / l_ref[g])
            o_ref[0, g] = gt[:, 0:1] * ocmp_ref[0, g] + gt[:, 1:2] * o_sel + gt[:, 2:3] * owin_ref[0, g]


def attn_dec_sel(table, ids, qr, k_cache, v_cache, k_new, v_new, ocmp, owin, gates, *, pos):
    B, G, R, _ = qr.shape
    page = k_cache.shape[1]
    bpp = page // SEL_BLOCK
    n_past = table.shape[1] * bpp
    kc = k_cache.reshape(k_cache.shape[0] * bpp, SEL_BLOCK // 2, 2 * G, HEAD_DIM)
    vc = v_cache.reshape(v_cache.shape[0] * bpp, SEL_BLOCK // 2, 2 * G, HEAD_DIM)

    nb = DEC_BLOCKS_PER_STEP

    def cache_spec(g, u):
        def index(b, n, pt, ids_):
            bid = jnp.minimum(ids_[(b * G + g) * N_SEL + n * nb + u], n_past - 1)
            return (pt[b, bid // bpp] * bpp + bid % bpp, 0, 0, 0)
        return pl.BlockSpec((1, SEL_BLOCK // 2, 2 * G, HEAD_DIM), index)

    q_spec = pl.BlockSpec((1, G, R, HEAD_DIM), lambda b, n, pt, ids_: (b, 0, 0, 0))
    n_spec = pl.BlockSpec((1, 1, G * HEAD_DIM), lambda b, n, pt, ids_: (b, 0, 0))
    g_spec = pl.BlockSpec((1, G, R, LANE), lambda b, n, pt, ids_: (b, 0, 0, 0))
    c_specs = [cache_spec(g, u) for g in range(G) for u in range(nb)]
    return pl.pallas_call(
        functools.partial(_attn_dec_sel_kernel, n_past=n_past, pos=pos),
        grid_spec=pltpu.PrefetchScalarGridSpec(
            num_scalar_prefetch=2,
            grid=(B, N_SEL // nb),
            in_specs=[q_spec] + c_specs + c_specs + [n_spec, n_spec, q_spec, q_spec, g_spec],
            out_specs=q_spec,
            scratch_shapes=[pltpu.VMEM((G, R, 1), F32), pltpu.VMEM((G, R, 1), F32),
                            pltpu.VMEM((G, R, HEAD_DIM), F32)],
        ),
        out_shape=jax.ShapeDtypeStruct((B, G, R, HEAD_DIM), F32),
        compiler_params=_params(("parallel", "arbitrary")),
        name="attn_dec_sel",
    )(table, ids.reshape(-1), qr, *([kc] * (G * nb)), *([vc] * (G * nb)), k_new, v_new, ocmp, owin, gates)


def _rope_tables(pos):
    half = HEAD_DIM // 2
    inv = ROPE_THETA ** (-jnp.arange(half, dtype=F32) / half)
    ang = pos.astype(F32)[:, None] * inv[None, :]
    cos, sin = jnp.cos(ang), jnp.sin(ang)
    return jnp.concatenate([cos, cos], axis=1), jnp.concatenate([-sin, sin], axis=1)


def _prep_weights(W):
    D = W['w_kv'].shape[0]
    H = D // HEAD_DIM
    R = H // N_KV_HEADS
    pg = D // len(POOL_WINDOWS)
    half = CMP_STRIDE * HEAD_DIM
    P = {}
    P['w_pool'] = W['w_pool'].astype(BF16).reshape(-1, len(POOL_WINDOWS) * pg, pg)
    kv_gain = jnp.ones((W['w_kv'].shape[1], KV_W), F32)
    kv_gain = kv_gain.at[2].set(jnp.tile(W['g_k_sel'], N_KV_HEADS)).at[4].set(jnp.tile(W['g_k_win'], N_KV_HEADS))
    P['kv_gain'] = kv_gain
    for t in ('k', 'v'):
        w1 = W['w_cmp_%s1' % t].astype(BF16)
        P['w_cmp_%s1' % t] = w1
        P['w_cmp_%scat' % t] = jnp.concatenate([w1[:half], w1[half:]], axis=1)
        P['w_cmp_%s2' % t] = W['w_cmp_%s2' % t].astype(BF16)
        P['pe_%s' % t] = W['pe_cmp_%s' % t].astype(BF16).reshape(1, -1)
    n_b = W['w_qg'].shape[0]
    wg = W['w_qg'][:, :, H * HEAD_DIM:].astype(BF16).reshape(n_b, D, N_KV_HEADS, R * N_BRANCH)
    wg = jnp.pad(wg, ((0, 0), (0, 0), (0, 0), (0, LANE - R * N_BRANCH)))
    P['w_gate'] = wg.reshape(n_b, D, N_KV_HEADS * LANE)
    P['w_ple'] = W['w_ple'].astype(BF16)
    P[('w_up', 0)] = W['w_up'][0].astype(BF16)
    P[('w_qg', 0)] = W['w_qg'][0, :, :H * HEAD_DIM].astype(BF16)
    return P


def _dense(x, wname, layer, W, P, *, N, epilogue, specs, extras=(), out_dtypes, side_cast=None, row_scale_in=None,
           norm_gains=None, name):
    M, K = x.shape
    shapes = [d if isinstance(d, jax.ShapeDtypeStruct) else jax.ShapeDtypeStruct((M, N), d) for d in out_dtypes]
    key = (wname, layer)
    tmx = 1024 if M >= 1024 else M
    use_mm = key in P
    if use_mm:
        grid, tm, tn, tk = _tiles(M, N, K, tmx, 1024, 2048 if (M >= 1024 and K > 4096) else 4096)
    else:
        grid, tm, tn, tk = _tiles(M, N, K, tmx, 512, K if K <= 4096 else 2048)
    extra_specs, out_specs = specs(tm, tn)
    extras = list(extras)
    if row_scale_in is not None:
        epilogue = _row_scaled(epilogue)
        extras.insert(0, row_scale_in)
        extra_specs = [pl.BlockSpec((tm, LANE), lambda i, j, k: (i, 0))] + list(extra_specs)
    if norm_gains:
        epilogue = _norm_producer(epilogue, len(extras))
        n_ex, n_ex_specs, n_shapes, n_specs = _norm_io(norm_gains, M, N, grid[1], tm, tn)
        extras, extra_specs = extras + n_ex, list(extra_specs) + n_ex_specs
        shapes, out_specs = shapes + n_shapes, list(out_specs) + n_specs
    side = None if side_cast is None else (W[side_cast[0]], side_cast[1])
    if use_mm:
        outs = list(mm(x, P[key], grid=grid, tm=tm, tn=tn, tk=tk, epilogue=epilogue, extras=extras,
                       extra_specs=extra_specs, out_shapes=shapes, out_specs=out_specs, side=side, name=name))
    else:
        w = W[wname]
        outs = list(mm_ws(x, w, layer=layer if w.ndim == 3 else None, side=side, grid=grid, tm=tm, tn=tn, tk=tk,
                          epilogue=epilogue, extras=extras, extra_specs=extra_specs, out_shapes=shapes,
                          out_specs=out_specs, name=name))
    if side is not None:
        P[side_cast] = outs.pop()
    if norm_gains:
        outs.append(row_scale(outs.pop(), N))
    return outs


def _ffn_ple(h, normed, p_l, layer, W, P, next_gains):
    M, D = h.shape
    F = W['w_up'].shape[2]
    host = M >= 1024
    next_up = ('w_up', layer + 1) if host and layer + 1 < W['w_up'].shape[0] else None
    mn = lambda tm, tn: ([], [_mn_spec(tm, tn)])
    res = lambda tm, tn: ([_mn_spec(tm, tn)], [_mn_spec(tm, tn)])
    y, rs = normed
    (u,) = _dense(y, 'w_up', layer, W, P, N=F, epilogue=lambda acc, rows: (jnp.square(jnp.maximum(acc, 0.0)),),
                  specs=mn, out_dtypes=[BF16], side_cast=('w_down', layer) if host else None, row_scale_in=rs,
                  name="ffn_up")
    h, y, rs = _dense(u, 'w_down', layer, W, P, N=D, epilogue=lambda acc, rows, r: (r[rows, :] + acc,), specs=res,
                      extras=[h], out_dtypes=[F32], norm_gains=[W['g_ple'][layer]], side_cast=next_up,
                      name="ffn_down")
    ple_dim = p_l.shape[1]
    ple_specs = lambda tm, tn: ([_mn_spec(tm, tn), pl.BlockSpec((tm, ple_dim), lambda i, j, k: (i, 0)),
                                 pl.BlockSpec((ple_dim, tn), lambda i, j, k: (0, j))], [_mn_spec(tm, tn)])
    return _dense(y, 'w_ple_gate', layer, W, P, N=D,
                  epilogue=lambda acc, rows, r, pp, wp: (r[rows, :] + _dot(pp[rows, :], wp[...]) * _sigmoid(acc),),
                  specs=ple_specs, extras=[h, p_l.astype(BF16), P['w_ple'][layer]], out_dtypes=[F32],
                  row_scale_in=rs, norm_gains=next_gains, name="ple")


def _kv_epilogue(acc, rows, gain_ref, cos_ref, sin_ref):
    j = pl.program_id(0)
    cosf, sinf = cos_ref[rows, :], sin_ref[rows, :]
    heads = []
    for hh in range(N_KV_HEADS):
        sl = slice(hh * HEAD_DIM, (hh + 1) * HEAD_DIM)
        heads.append(_rope(_head_norm(acc[:, sl], gain_ref[0, :, sl]), cosf, sinf))
    out = jnp.where((j == 2) | (j == 4), jnp.concatenate(heads, axis=1), acc)
    return out, out


def _q_epilogue(acc, rows, gq_ref, cos_ref, sin_ref):
    cosf, sinf = cos_ref[rows, :], sin_ref[rows, :]
    qs, qrs = [], []
    for hh in range(acc.shape[1] // HEAD_DIM):
        qn = _head_norm(acc[:, hh * HEAD_DIM:(hh + 1) * HEAD_DIM], gq_ref[...])
        qs.append(qn)
        qrs.append(_rope(qn, cosf, sinf))
    return jnp.concatenate(qs, axis=1), jnp.concatenate(qrs, axis=1)


def _trunk(x, p, pool_prefix, pos0, W, P, attend):
    B, T, D = x.shape
    M = B * T
    tmx = 1024 if M >= 1024 else M
    h = x.reshape(M, D)
    pg = D // len(POOL_WINDOWS)

    d, pool_new = pool_diff(x, W['g_mix'][0], pool_prefix[0], pos0)
    d, pool_new = d.reshape(M, D), pool_new[None]
    grid, tm, tn, tk = _tiles(M, D, pg, tmx, pg, pg)
    n_ex, n_ex_specs, n_shapes, n_specs = _norm_io([W['g_ffn'][0]], M, D, grid[1], tm, tn)
    h, y, ssq = mm(d, P['w_pool'][0], grid=grid, tm=tm, tn=tn, tk=tk,
                   x_map=lambda i, j, k: (i, j), w_map=lambda i, j, k: (j, 0),
                   epilogue=_norm_producer(lambda acc, rows, sc, r: (r[rows, :] + acc * sc[...],), 2),
                   extras=[W['pool_scale'][0].reshape(1, D), h] + n_ex,
                   extra_specs=[pl.BlockSpec((1, tn), lambda i, j, k: (0, j)), _mn_spec(tm, tn)] + n_ex_specs,
                   out_shapes=[jax.ShapeDtypeStruct((M, D), F32)] + n_shapes,
                   out_specs=[_mn_spec(tm, tn)] + n_specs, name="pool_mix")
    h, hkv, a1, rs = _ffn_ple(h, (y, row_scale(ssq, D)), p[0].reshape(M, -1), 0, W, P,
                              [W['g_kv'], W['g_mix'][1]])

    pos = pos0 + jnp.tile(jnp.arange(T, dtype=jnp.int32), B)
    cosf, sinf = _rope_tables(pos)
    n_kv = W['w_kv'].shape[1]
    rope_spec = lambda tm: pl.BlockSpec((tm, HEAD_DIM), lambda i, j, k: (i, 0))
    kv_spec = lambda tm: pl.BlockSpec((1, tm, KV_W), lambda i, j, k: (j, i, 0))
    kv4_spec = lambda tm: pl.BlockSpec((1, tm, N_KV_HEADS, HEAD_DIM), lambda i, j, k: (j, i, 0, 0))
    kv_specs = lambda tm, tn: ([pl.BlockSpec((1, 1, KV_W), lambda i, j, k: (j, 0, 0)), rope_spec(tm), rope_spec(tm)],
                               [kv4_spec(tm), kv_spec(tm)])
    kv, kv_b = _dense(hkv, 'w_kv2d', None, W, P, N=n_kv * KV_W, epilogue=_kv_epilogue, specs=kv_specs,
                      extras=[P['kv_gain'].reshape(n_kv, 1, KV_W), cosf, sinf],
                      out_dtypes=[jax.ShapeDtypeStruct((n_kv, M, N_KV_HEADS, HEAD_DIM), F32),
                                  jax.ShapeDtypeStruct((n_kv, M, KV_W), BF16)], row_scale_in=rs, name="kv_proj")
    q_specs = lambda tm, tn: ([pl.BlockSpec((1, HEAD_DIM), lambda i, j, k: (0, 0)), rope_spec(tm), rope_spec(tm)],
                              [_mn_spec(tm, tn)] * 2)
    q, qr = _dense(a1, 'w_qg', 0, W, P, N=D, epilogue=_q_epilogue, specs=q_specs,
                   extras=[W['g_q'][0].reshape(1, HEAD_DIM), cosf, sinf], out_dtypes=[BF16, BF16], row_scale_in=rs,
                   name="q_proj")
    ng = N_KV_HEADS * LANE
    grid, tm, tn, tk = _tiles(M, ng, D, tmx, ng, D)
    (gates,) = mm(a1, P['w_gate'][0], grid=grid, tm=tm, tn=tn, tk=tk,
                  epilogue=_row_scaled(lambda acc, rows: (_sigmoid(acc),)), extras=[rs],
                  extra_specs=[pl.BlockSpec((tm, LANE), lambda i, j, k: (i, 0))],
                  out_shapes=[jax.ShapeDtypeStruct((M, ng), F32)], out_specs=[_mn_spec(tm, tn)], name="gate_proj")

    o, win_state = attend(kv, kv_b, q, qr, gates)

    h, y, rs = _dense(o, 'w_o', 0, W, P, N=D, epilogue=lambda acc, rows, r: (r[rows, :] + acc,), extras=[h],
                      specs=lambda tm, tn: ([_mn_spec(tm, tn)], [_mn_spec(tm, tn)]), out_dtypes=[F32],
                      norm_gains=[W['g_ffn'][1]], name="attn_out")
    (h,) = _ffn_ple(h, (y, rs), p[1].reshape(M, -1), 1, W, P, [])
    rows = tuple(kv[n].reshape(B, T, N_KV_HEADS, HEAD_DIM) for n in range(4))
    return h.reshape(B, T, D), pool_new, rows, win_state


def _compress(k_pages, v_pages, table, cp, W, P):
    abk, abv = cmp_partials(k_pages, v_pages, table, P['w_cmp_kcat'], P['w_cmp_vcat'], cp)
    gain = W['g_k_cmp'].reshape(1, HEAD_DIM)
    ck = cmp_finish(abk, P['pe_k'], P['w_cmp_k1'], P['w_cmp_k2'], gain, norm=True)
    cv = cmp_finish(abv, P['pe_v'], P['w_cmp_v1'], P['w_cmp_v2'], gain, norm=False)
    return ck, cv


def kernel(x_prompt, x_sample, state_pool, cache_k_cmp, cache_v_cmp, cache_k_sel, cache_v_sel, state_k_win, state_v_win, page_table, p_prompt, p_sample, g_mix, w_pool, pool_scale, g_kv, w_kv, g_k_cmp, g_k_sel, g_k_win, w_cmp_k1, w_cmp_k2, pe_cmp_k, w_cmp_v1, w_cmp_v2, pe_cmp_v, w_qg, g_q, w_o, g_ffn, w_up, w_down, g_ple, w_ple, w_ple_gate):
    W = dict(g_mix=g_mix, w_pool=w_pool, pool_scale=pool_scale, g_kv=g_kv, w_kv=w_kv, g_k_cmp=g_k_cmp,
             g_k_sel=g_k_sel, g_k_win=g_k_win, w_cmp_k1=w_cmp_k1, w_cmp_k2=w_cmp_k2, pe_cmp_k=pe_cmp_k,
             w_cmp_v1=w_cmp_v1, w_cmp_v2=w_cmp_v2, pe_cmp_v=pe_cmp_v, w_qg=w_qg, g_q=g_q, w_o=w_o,
             g_ffn=g_ffn, w_up=w_up, w_down=w_down, g_ple=g_ple, w_ple=w_ple, w_ple_gate=w_ple_gate)
    P = _prep_weights(W)
    W['w_kv2d'] = w_kv.reshape(w_kv.shape[0], -1)
    Bp, Tp, D = x_prompt.shape
    Bs, Ts, _ = x_sample.shape
    assert Ts == 1, "the decode path handles one new token per sequence"
    page = cache_k_cmp.shape[1]
    past_len = page_table.shape[1] * page
    R = D // HEAD_DIM // N_KV_HEADS
    assert Tp % page == 0 and past_len % SEL_BLOCK == 0

    def attend_prompt(kv, kv_b, q, qr, gates):
        ppb = Tp // page
        table = jnp.arange(Bp * ppb, dtype=jnp.int32).reshape(Bp, ppb)
        pages = lambda a: a.reshape(-1, page, N_KV_HEADS, HEAD_DIM)
        ck, cv = _compress(pages(kv[0]), pages(kv[1]), table, ppb, W, P)
        nc = Tp // CMP_STRIDE - CMP_BLOCK // CMP_STRIDE + 1
        seq = lambda a: a.reshape(Bp, Tp, -1)
        o = attn_prompt(seq(q), seq(qr), ck, cv, kv_b.reshape(-1, Bp, Tp, KV_W), seq(gates), nc=nc)
        nw = min(WINDOW, Tp)
        win = tuple(kv[n].reshape(Bp, Tp, N_KV_HEADS, HEAD_DIM)[:, -nw:] for n in (4, 5))
        return o.reshape(Bp * Tp, D), win

    def attend_sample(kv, kv_b, q, qr, gates):
        del kv_b
        ck, cv = _compress(cache_k_cmp, cache_v_cmp, page_table, min(32, page_table.shape[1]), W, P)
        nc = (past_len - (CMP_BLOCK - 1)) // CMP_STRIDE + 1
        ns = past_len // SEL_BLOCK + 1
        wb = state_k_win.shape[1]
        new_row = lambda a: a.reshape(Bs, 1, N_KV_HEADS, HEAD_DIM)
        kw = jnp.concatenate([state_k_win, new_row(kv[4])], axis=1)[:, -wb:]
        vw = jnp.concatenate([state_v_win, new_row(kv[5])], axis=1)[:, -wb:]
        heads = lambda a: a.reshape(Bs, N_KV_HEADS, R, HEAD_DIM)
        ocmp, owin, ids = attn_dec_dense(heads(q), heads(qr), ck, cv, kw, vw, nc=nc, ns=ns, pos=past_len)
        gt = gates.reshape(Bs, N_KV_HEADS, LANE)[:, :, :R * N_BRANCH].reshape(Bs, N_KV_HEADS, R, N_BRANCH)
        gt = jnp.pad(gt, ((0, 0), (0, 0), (0, 0), (0, LANE - N_BRANCH)))
        o = attn_dec_sel(page_table, ids, heads(qr), cache_k_sel, cache_v_sel,
                         kv[2].reshape(Bs, 1, KV_W), kv[3].reshape(Bs, 1, KV_W), ocmp, owin, gt, pos=past_len)
        return o.reshape(Bs, D).astype(BF16), (kw, vw)

    pool_zero = jnp.zeros((state_pool.shape[0], Bp, POOL_STATE, D), x_prompt.dtype)
    y_p, pool_p, rows_p, win_p = _trunk(x_prompt, p_prompt, pool_zero, 0, W, P, attend_prompt)
    y_s, pool_s, rows_s, win_s = _trunk(x_sample, p_sample, state_pool, past_len, W, P, attend_sample)
    return (y_p, y_s, pool_p, pool_s, rows_p[0], rows_p[1], rows_p[2], rows_p[3], win_p[0], win_p[1],
            rows_s[0], rows_s[1], rows_s[2], rows_s[3], win_s[0], win_s[1])
```

```python
import functools

import jax
import jax.numpy as jnp
import numpy as np
from jax import lax
from jax.experimental import pallas as pl
from jax.experimental.pallas import tpu as pltpu

F32 = jnp.float32
BF16 = jnp.bfloat16

POOL_WINDOWS = (2, 4, 8, 16)
POOL_STATE = max(POOL_WINDOWS) - 1
POOL_PAD = POOL_STATE + 1
HEAD_DIM = 128
N_KV_HEADS = 4
N_BRANCH = 3
CMP_BLOCK = 32
CMP_STRIDE = 16
CMP_HIDDEN = 2 * HEAD_DIM
SEL_BLOCK = 64
N_SEL = 16
WINDOW = 512
ROPE_THETA = 10000.0
EPS = 1e-6
SCALE = HEAD_DIM ** -0.5
NEG = -1e30
FORCE = 1e9
PAD_SCORE = -3e38
KV_W = N_KV_HEADS * HEAD_DIM
LANE = 128
VMEM_LIMIT = 56 * 1024 * 1024


def _params(sem):
    return pltpu.CompilerParams(dimension_semantics=sem, vmem_limit_bytes=VMEM_LIMIT)


def _sigmoid(x):
    return 1.0 / (1.0 + jnp.exp(-x))


def _dot(a, b):
    return jnp.dot(a, b, preferred_element_type=F32)


def _dot_nt(a, b):
    return lax.dot_general(a, b, (((1,), (1,)), ((), ())), preferred_element_type=F32)


def _dot_tn(a, b):
    return lax.dot_general(a, b, (((0,), (0,)), ((), ())), preferred_element_type=F32)


def _head_norm(x, g):
    return x * lax.rsqrt(jnp.mean(x * x, axis=-1, keepdims=True) + EPS) * g


def _rope(x, cosf, sinf):
    return x * cosf + pltpu.roll(x, HEAD_DIM // 2, 1) * sinf


def _fold_lanes(x):
    parts = [x[:, c * LANE:(c + 1) * LANE] for c in range(x.shape[1] // LANE)]
    return functools.reduce(lambda u, v: u + v, parts)


def _norm_producer(epilogue, n_base):
    def wrapped(acc, rows, *extras):
        (h,) = epilogue(acc, rows, *extras[:n_base])
        return (h, *[(h * g[...]).astype(BF16) for g in extras[n_base:]], _fold_lanes(h * h))
    return wrapped


def _row_scaled(epilogue):
    def wrapped(acc, rows, rs_ref, *extras):
        rs = rs_ref[rows, :]
        acc = jnp.concatenate([acc[:, c * LANE:(c + 1) * LANE] * rs for c in range(acc.shape[1] // LANE)], axis=1)
        return epilogue(acc, rows, *extras)
    return wrapped


def _row_scale_kernel(ssq_ref, o_ref, *, d):
    tot = jnp.sum(functools.reduce(lambda u, v: u + v, [ssq_ref[j] for j in range(ssq_ref.shape[0])]),
                  axis=1, keepdims=True)
    o_ref[...] = jnp.broadcast_to(lax.rsqrt(tot / d + EPS), o_ref.shape)


def row_scale(ssq, d):
    gn, M, _ = ssq.shape
    tm = min(M, 1024)
    return pl.pallas_call(
        functools.partial(_row_scale_kernel, d=d),
        grid=(M // tm,),
        in_specs=[pl.BlockSpec((gn, tm, LANE), lambda i: (0, i, 0))],
        out_specs=pl.BlockSpec((tm, LANE), lambda i: (i, 0)),
        out_shape=jax.ShapeDtypeStruct((M, LANE), F32),
        compiler_params=_params(("parallel",)),
        name="row_scale",
    )(ssq)


def _norm_io(gains, M, N, gn, tm, tn):
    extras = [g.reshape(1, N).astype(F32) for g in gains]
    extra_specs = [pl.BlockSpec((1, tn), lambda i, j, k: (0, j)) for _ in gains]
    shapes = [jax.ShapeDtypeStruct((M, N), BF16) for _ in gains] + [jax.ShapeDtypeStruct((gn, M, LANE), F32)]
    specs = [_mn_spec(tm, tn) for _ in gains] + [pl.BlockSpec((1, tm, LANE), lambda i, j, k: (j, i, 0))]
    return extras, extra_specs, shapes, specs


EPILOGUE_ROWS = 256


def _store_rows(o_ref, rows, r):
    if len(o_ref.shape) == 4:
        for hh in range(o_ref.shape[2]):
            o_ref[0, rows, hh, :] = r[:, hh * HEAD_DIM:(hh + 1) * HEAD_DIM].astype(o_ref.dtype)
    elif len(o_ref.shape) == 3:
        o_ref[0, rows, :] = r.astype(o_ref.dtype)
    else:
        o_ref[rows, :] = r.astype(o_ref.dtype)


def _finish_rows(acc_of, tm, outs, extras, epilogue):
    ch = min(tm, EPILOGUE_ROWS)
    for c in range(tm // ch):
        rows = slice(c * ch, (c + 1) * ch)
        for o_ref, r in zip(outs, epilogue(acc_of(rows), rows, *extras)):
            _store_rows(o_ref, rows, r)


def _k_steps(x_ref, w, acc_ref, nk, outs, extras, epilogue):
    k = pl.program_id(2)

    @pl.when(k == 0)
    def _():
        acc_ref[...] = _dot(x_ref[...], w())

    @pl.when((k > 0) & (k < nk - 1))
    def _():
        acc_ref[...] += _dot(x_ref[...], w())

    @pl.when(k == nk - 1)
    def _():
        _finish_rows(lambda rows: acc_ref[rows, :] + _dot(x_ref[rows, :], w()), x_ref.shape[0], outs, extras,
                     epilogue)


def _side_io(side, n_steps, step):
    s_arr, s_layer = side
    _, rows, cols = s_arr.shape
    rs = rows // n_steps
    assert rs * n_steps == rows and rs % 16 == 0
    in_spec = pl.BlockSpec((None, rs, cols), lambda *g: (s_layer, step(*g), 0))
    out_spec = pl.BlockSpec((rs, cols), lambda *g: (step(*g), 0))
    return s_arr, in_spec, jax.ShapeDtypeStruct((rows, cols), BF16), out_spec


def _mm_kernel(*refs, nk, n_extra, n_out, epilogue, side):
    x_ref, w_ref = refs[0], refs[1]
    extras = refs[2:2 + n_extra]
    n_in = 2 + n_extra + side
    outs = refs[n_in:n_in + n_out]
    tm = x_ref.shape[0]
    if side:
        refs[n_in + n_out][...] = refs[n_in - 1][...].astype(BF16)

    if nk == 1:
        _finish_rows(lambda rows: _dot(x_ref[rows, :], w_ref[...]), tm, outs, extras, epilogue)
    else:
        _k_steps(x_ref, lambda: w_ref[...], refs[-1], nk, outs, extras, epilogue)


def mm(x, w, *, grid, tm, tn, tk, epilogue, extras=(), extra_specs=(), out_shapes, out_specs,
       x_map=None, w_map=None, side=None, name):
    gm, gn, nk = grid
    x_map = x_map or (lambda i, j, k: (i, k))
    w_map = w_map or (lambda i, j, k: (k, j))
    ins, in_specs = [x, w, *extras], [pl.BlockSpec((tm, tk), x_map), pl.BlockSpec((tk, tn), w_map)]
    in_specs += list(extra_specs)
    n_out, out_shapes, out_specs = len(out_shapes), list(out_shapes), list(out_specs)
    if side is not None:
        s_arr, s_in, s_shape, s_out = _side_io(side, gm * gn * nk, lambda i, j, k: (i * gn + j) * nk + k)
        ins.append(s_arr), in_specs.append(s_in), out_shapes.append(s_shape), out_specs.append(s_out)
    kern = functools.partial(_mm_kernel, nk=nk, n_extra=len(extras), n_out=n_out, epilogue=epilogue,
                             side=side is not None)
    return pl.pallas_call(
        kern,
        grid=grid,
        in_specs=in_specs,
        out_specs=out_specs,
        out_shape=out_shapes,
        scratch_shapes=[pltpu.VMEM((tm, tn), F32)] if nk > 1 else [],
        compiler_params=_params(("parallel", "parallel", "arbitrary")),
        name=name,
    )(*ins)


def _mm_ws_kernel(*refs, nk, n_extra, n_out, epilogue, side):
    x_ref, w_ref = refs[0], refs[1]
    extras = refs[2:2 + n_extra]
    n_in = 2 + n_extra + side
    outs = refs[n_in:n_in + n_out]
    rest = refs[n_in + n_out:]
    wb_ref = rest[side]
    i, k = pl.program_id(1), pl.program_id(2)

    @pl.when(i == 0)
    def _():
        wb_ref[k] = w_ref[...].astype(BF16)

    if side:
        rest[0][...] = refs[n_in - 1][...].astype(BF16)

    tm = x_ref.shape[0]
    if nk == 1:
        _finish_rows(lambda rows: _dot(x_ref[rows, :], wb_ref[0]), tm, outs, extras, epilogue)
    else:
        _k_steps(x_ref, lambda: wb_ref[k], rest[-1], nk, outs, extras, epilogue)


def mm_ws(x, w, *, layer=None, side=None, grid, tm, tn, tk, epilogue, extras=(), extra_specs=(),
          out_shapes, out_specs, name):
    gm, gn, nk = grid
    swap = lambda f: (lambda j, i, k: f(i, j, k))
    respec = lambda s: pl.BlockSpec(s.block_shape, swap(s.index_map))
    k_once = lambda i, k: jnp.where(i == 0, k, nk - 1)
    if layer is None:
        w_spec = pl.BlockSpec((tk, tn), lambda j, i, k: (k_once(i, k), j))
    else:
        w_spec = pl.BlockSpec((None, tk, tn), lambda j, i, k: (layer, k_once(i, k), j))
    n_out = len(out_shapes)
    ins, in_specs = [x, w, *extras], [pl.BlockSpec((tm, tk), lambda j, i, k: (i, k)), w_spec]
    in_specs += [respec(s) for s in extra_specs]
    out_shapes, out_specs = list(out_shapes), [respec(s) for s in out_specs]
    if side is not None:
        s_arr, s_in, s_shape, s_out = _side_io(side, gm * gn * nk, lambda j, i, k: (j * gm + i) * nk + k)
        ins.append(s_arr), in_specs.append(s_in), out_shapes.append(s_shape), out_specs.append(s_out)
    kern = functools.partial(_mm_ws_kernel, nk=nk, n_extra=len(extras), n_out=n_out, epilogue=epilogue,
                             side=side is not None)
    return pl.pallas_call(
        kern,
        grid=(gn, gm, nk),
        in_specs=in_specs,
        out_specs=out_specs,
        out_shape=out_shapes,
        scratch_shapes=[pltpu.VMEM((nk, tk, tn), BF16)] + ([pltpu.VMEM((tm, tn), F32)] if nk > 1 else []),
        compiler_params=_params(("parallel", "arbitrary", "arbitrary")),
        name=name,
    )(*ins)


def _tiles(M, N, K, tm, tn, tk):
    tm, tn, tk = min(tm, M), min(tn, N), min(tk, K)
    return (M // tm, N // tn, K // tk), tm, tn, tk


def _mn_spec(tm, tn):
    return pl.BlockSpec((tm, tn), lambda i, j, k: (i, j))


def _pool_diff_kernel(*refs, tt, pos0, halo, side):
    x_ref, pre_ref, g_ref = refs[0], refs[1 + halo], refs[2 + halo]
    n_in = 3 + halo + side
    d_ref, st_ref, seq_ref = refs[n_in], refs[n_in + 1], refs[-1]
    if side:
        refs[n_in + 2][...] = refs[n_in - 1][...].astype(BF16)
    t = pl.program_id(1)
    norm = lambda x: x * lax.rsqrt(jnp.mean(x * x, axis=-1, keepdims=True) + EPS) * g_ref[...]
    a = norm(x_ref[0])
    seq_ref[POOL_PAD:POOL_PAD + tt, :] = a

    @pl.when(t == 0)
    def _():
        seq_ref[0:POOL_PAD, :] = pre_ref[0]

    if halo:
        @pl.when(t > 0)
        def _():
            seq_ref[0:POOL_PAD, :] = norm(refs[1][0])

    pos = pos0 + t * tt + lax.broadcasted_iota(jnp.int32, (tt, 1), 0)
    pg = a.shape[1] // len(POOL_WINDOWS)
    for g, w in enumerate(POOL_WINDOWS):
        cols = slice(g * pg, (g + 1) * pg)
        s = a[:, cols]
        for j in range(1, w):
            s = s + seq_ref[POOL_PAD - j:POOL_PAD - j + tt, cols]
        cnt = jnp.minimum(pos + 1, w).astype(F32)
        d_ref[0, :, cols] = (s / cnt - a[:, cols]).astype(d_ref.dtype)
    st_ref[0] = seq_ref[tt:tt + POOL_PAD, :]


def pool_diff(x, gain, prefix, pos0, side=None):
    B, T, D = x.shape
    tt = min(T, 256)
    halo = T > tt
    pre = jnp.concatenate([jnp.zeros((B, 1, D), F32), prefix], axis=1)
    hpt = tt // POOL_PAD
    in_specs = [pl.BlockSpec((1, tt, D), lambda b, t: (b, t, 0))]
    if halo:
        in_specs.append(pl.BlockSpec((1, POOL_PAD, D), lambda b, t: (b, jnp.maximum(t * hpt - 1, 0), 0)))
    in_specs += [pl.BlockSpec((1, POOL_PAD, D), lambda b, t: (b, 0, 0)), pl.BlockSpec((1, D), lambda b, t: (0, 0))]
    ins = [x, x] if halo else [x]
    ins += [pre, gain.reshape(1, D).astype(F32)]
    out_specs = [pl.BlockSpec((1, tt, D), lambda b, t: (b, t, 0)), pl.BlockSpec((1, POOL_PAD, D), lambda b, t: (b, 0, 0))]
    out_shapes = [jax.ShapeDtypeStruct((B, T, D), BF16), jax.ShapeDtypeStruct((B, POOL_PAD, D), F32)]
    if side is not None:
        nt = T // tt
        s_arr, s_in, s_shape, s_out = _side_io(side, B * nt, lambda b, t: b * nt + t)
        ins.append(s_arr), in_specs.append(s_in), out_shapes.append(s_shape), out_specs.append(s_out)
    d, st, *cast = pl.pallas_call(
        functools.partial(_pool_diff_kernel, tt=tt, pos0=pos0, halo=halo, side=side is not None),
        grid=(B, T // tt),
        in_specs=in_specs,
        out_specs=out_specs,
        out_shape=out_shapes,
        scratch_shapes=[pltpu.VMEM((POOL_PAD + tt, D), F32)],
        compiler_params=_params(("parallel", "arbitrary")),
        name="pool_diff",
    )(*ins)
    return d, st[:, 1:], (cast[0] if cast else None)


PAGES_PER_STEP = 8
CMP_FINISH_ROWS = 512


def _cmp_partial_kernel(pt_ref, *refs, steps, pps):
    del pt_ref
    k_pages, v_pages = refs[:pps], refs[pps:2 * pps]
    wk_ref, wv_ref, abk_ref, abv_ref, xk_ref, xv_ref = refs[2 * pps:]
    p = pl.program_id(2)
    G = N_KV_HEADS
    half = CMP_STRIDE // 2
    rows_per_page = (k_pages[0].shape[1] // half) * G
    for pages, x_ref in ((k_pages, xk_ref), (v_pages, xv_ref)):
        for q, page_ref in enumerate(pages):
            for sb in range(page_ref.shape[1] // half):
                row = q * rows_per_page + sb * G
                for r in range(CMP_STRIDE):
                    x_ref[p, row:row + G, r * HEAD_DIM:(r + 1) * HEAD_DIM] = (
                        page_ref[0, sb * half + r // 2, (r % 2) * G:(r % 2 + 1) * G, :])

    @pl.when(p == steps - 1)
    def _():
        for x_ref, w_ref, ab_ref in ((xk_ref, wk_ref, abk_ref), (xv_ref, wv_ref, abv_ref)):
            x = x_ref[...].reshape(ab_ref.shape[1], x_ref.shape[2])
            ab_ref[0] = _dot(x.astype(BF16), w_ref[...])


def cmp_partials(k_pages, v_pages, table, wk_cat, wv_cat, cp):
    B, ppb = table.shape
    page = k_pages.shape[1]
    pps = PAGES_PER_STEP
    rows_per_page = page // CMP_STRIDE * N_KV_HEADS
    nch, steps = ppb // cp, cp // pps
    m = cp * rows_per_page
    kdim = CMP_STRIDE * HEAD_DIM
    pair_rows = lambda a: a.reshape(a.shape[0], page // 2, 2 * N_KV_HEADS, HEAD_DIM)

    def page_spec(q):
        return pl.BlockSpec((1, page // 2, 2 * N_KV_HEADS, HEAD_DIM),
                            lambda b, c, p, pt: (pt[b, c * cp + p * pps + q], 0, 0, 0))

    page_specs = [page_spec(q) for q in range(pps)]
    w_spec = pl.BlockSpec((kdim, 2 * CMP_HIDDEN), lambda b, c, p, pt: (0, 0))
    out_spec = pl.BlockSpec((1, m, 2 * CMP_HIDDEN), lambda b, c, p, pt: (b, c, 0))
    out_shape = jax.ShapeDtypeStruct((B, ppb * rows_per_page, 2 * CMP_HIDDEN), F32)
    x_scratch = pltpu.VMEM((steps, pps * rows_per_page, kdim), F32)
    return pl.pallas_call(
        functools.partial(_cmp_partial_kernel, steps=steps, pps=pps),
        grid_spec=pltpu.PrefetchScalarGridSpec(
            num_scalar_prefetch=1,
            grid=(B, nch, steps),
            in_specs=page_specs + page_specs + [w_spec, w_spec],
            out_specs=[out_spec, out_spec],
            scratch_shapes=[x_scratch, x_scratch],
        ),
        out_shape=[out_shape, out_shape],
        compiler_params=_params(("parallel", "parallel", "arbitrary")),
        name="cmp_partials",
    )(table, *([pair_rows(k_pages)] * pps), *([pair_rows(v_pages)] * pps), wk_cat, wv_cat)


def _cmp_finish_kernel(ab_ref, pe_ref, w1_ref, w2_ref, g_ref, o_ref, out_ref, *, norm):
    n4 = ab_ref.shape[1]
    G = o_ref.shape[1]
    bias = _dot(jnp.broadcast_to(pe_ref[...], (8, pe_ref.shape[1])), w1_ref[...])[0:1, :]
    ch = min(n4, CMP_FINISH_ROWS)
    for c in range(n4 // ch):
        lo, hi = c * ch, (c + 1) * ch
        first = ab_ref[0, lo:hi, :CMP_HIDDEN]
        if hi + G <= n4:
            second = ab_ref[0, lo + G:hi + G, CMP_HIDDEN:]
        else:
            second = jnp.concatenate([ab_ref[0, lo + G:hi, CMP_HIDDEN:], ab_ref[0, hi - G:hi, CMP_HIDDEN:]], axis=0)
        pre = first + second + bias
        out = _dot((pre * _sigmoid(pre)).astype(BF16), w2_ref[...])
        if norm:
            out = _head_norm(out, g_ref[...])
        out_ref[lo:hi, :] = out
    for g in range(G):
        o_ref[0, g] = out_ref[pl.ds(g, n4 // G, stride=G), :].astype(o_ref.dtype)


def cmp_finish(ab, pe, w1, w2, gain, *, norm):
    B, n4, _ = ab.shape
    G = N_KV_HEADS
    full = lambda a: pl.BlockSpec(a.shape, lambda b: (0,) * a.ndim)
    args = (pe, w1, w2, gain)
    return pl.pallas_call(
        functools.partial(_cmp_finish_kernel, norm=norm),
        grid=(B,),
        in_specs=[pl.BlockSpec((1, n4, 2 * CMP_HIDDEN), lambda b: (b, 0, 0))] + [full(a) for a in args],
        out_specs=pl.BlockSpec((1, G, n4 // G, HEAD_DIM), lambda b: (b, 0, 0, 0)),
        out_shape=jax.ShapeDtypeStruct((B, G, n4 // G, HEAD_DIM), BF16),
        scratch_shapes=[pltpu.VMEM((n4, HEAD_DIM), F32)],
        compiler_params=_params(("parallel",)),
        name="cmp_finish",
    )(ab, *args)


def _select_blocks(score, blk, ns):
    rank = jnp.zeros(score.shape, jnp.int32)
    for j in range(ns):
        sj = score[j:j + 1, :]
        beats = (sj > score) | ((sj == score) & (j < blk))
        rank = rank + beats.astype(jnp.int32)
    return (rank < min(N_SEL, ns)) & (score > 0.5 * NEG)


def _attn_prompt_kernel(q_ref, qr_ref, ck_ref, cv_ref, ks_ref, vs_ref, kw_ref, vw_ref, gate_ref, mapT_ref,
                        expand_ref, o_ref, part_ref, sbias_ref, wbias_ref, m_ref, acc_ref, *, tq, kc, nc, ns, R):
    qi = pl.program_id(2)
    q0 = qi * tq
    pos = q0 + lax.broadcasted_iota(jnp.int32, (tq, 1), 0)
    ncp = ck_ref.shape[2]

    ck = ck_ref[0, 0]
    cv = cv_ref[0, 0]
    cidx = lax.broadcasted_iota(jnp.int32, (1, ncp), 1)
    ok_c = (cidx * CMP_STRIDE + CMP_BLOCK - 1 <= pos) & (cidx < nc)
    imp = jnp.zeros((tq, ncp), F32)
    for r in range(R):
        qh = q_ref[0, :, r * HEAD_DIM:(r + 1) * HEAD_DIM]
        s = jnp.where(ok_c, _dot_nt(qh, ck) * SCALE, NEG)
        e = jnp.exp(s - jnp.max(s, axis=1, keepdims=True))
        p = jnp.where(ok_c, e * (1.0 / jnp.sum(e, axis=1, keepdims=True)), 0.0)
        imp = imp + p
        part_ref[r] = gate_ref[0, :, r * N_BRANCH:r * N_BRANCH + 1] * _dot(p.astype(BF16), cv)

    nsp = mapT_ref.shape[0]
    p_slc = lax.dot_general(mapT_ref[...], imp, (((1,), (1,)), ((), ())), precision=lax.Precision.HIGHEST,
                            preferred_element_type=F32)
    blk = lax.broadcasted_iota(jnp.int32, (nsp, tq), 0)
    pos_l = q0 + lax.broadcasted_iota(jnp.int32, (nsp, tq), 1)
    cur = pos_l // SEL_BLOCK
    vis = blk * SEL_BLOCK <= pos_l
    forced = vis & ((blk == 0) | (blk == cur) | (blk == cur - 1))
    score = jnp.where(forced, FORCE, jnp.where(vis, p_slc, NEG))
    score = jnp.where(blk < ns, score, PAD_SCORE)
    sel = _select_blocks(score, blk, ns).astype(BF16)

    c_hi = (q0 + tq) // kc
    col = lax.broadcasted_iota(jnp.int32, (1, kc), 1)
    sel_keys = _dot_tn(sel, expand_ref[...])
    for c in range(sbias_ref.shape[0]):
        @pl.when(c < c_hi)
        def _(c=c):
            ok = (sel_keys[:, c * kc:(c + 1) * kc] > 0.5) & (c * kc + col <= pos)
            sbias_ref[c] = jnp.where(ok, 0.0, NEG)
    nwc = wbias_ref.shape[0]
    c_w0 = c_hi - nwc
    for d in range(nwc):
        kpos = (c_w0 + d) * kc + col
        wbias_ref[d] = jnp.where((kpos <= pos) & (kpos > pos - WINDOW), 0.0, NEG)

    def fold(t):
        return [t[:, i * LANE:(i + 1) * LANE] for i in range(kc // LANE)]

    ones_blk = jnp.ones((kc, LANE), BF16)

    def branch(k_ref, v_ref, c_lo, bias_of, gate_col):
        def logits(r, c, bias):
            k = k_ref[0, pl.ds(pl.multiple_of(c * kc, kc), kc), :]
            return _dot_nt(qr_ref[0, :, r * HEAD_DIM:(r + 1) * HEAD_DIM], k) * SCALE + bias

        m_ref[...] = jnp.full(m_ref.shape, NEG, F32)
        acc_ref[...] = jnp.zeros(acc_ref.shape, F32)

        def over_chunks(body):
            n = c_hi - c_lo

            def pair(i, carry):
                body(c_lo + 2 * i, carry)
                return body(c_lo + 2 * i + 1, carry)

            lax.fori_loop(0, n // 2, pair, 0)

            @pl.when(n % 2 == 1)
            def _():
                body(c_hi - 1, 0)

        def max_body(c, carry):
            bias = bias_of(c)
            for r in range(R):
                mx = m_ref[r]
                for part in fold(logits(r, c, bias)):
                    mx = jnp.maximum(mx, part)
                m_ref[r] = mx
            return carry

        over_chunks(max_body)
        for r in range(R):
            m_ref[r] = jnp.broadcast_to(jnp.max(m_ref[r], axis=1, keepdims=True), (tq, LANE))

        def sum_body(c, carry):
            bias = bias_of(c)
            v = jnp.concatenate([v_ref[0, pl.ds(pl.multiple_of(c * kc, kc), kc), :], ones_blk], axis=1)
            for r in range(R):
                t = logits(r, c, bias)
                m = m_ref[r]
                ps = [jnp.exp(part - m) for part in fold(t)]
                acc_ref[r] += _dot(jnp.concatenate(ps, axis=1).astype(BF16), v)
            return carry

        over_chunks(sum_body)
        for r in range(R):
            gate = gate_ref[0, :, r * N_BRANCH + gate_col:r * N_BRANCH + gate_col + 1]
            part_ref[r] += gate * (acc_ref[r, :, :HEAD_DIM] * (1.0 / acc_ref[r, :, HEAD_DIM:]))

    branch(ks_ref, vs_ref, 0, lambda c: sbias_ref[c], 1)
    branch(kw_ref, vw_ref, jnp.maximum(c_w0, 0), lambda c: wbias_ref[c - c_w0], 2)
    for r in range(R):
        o_ref[0, :, r * HEAD_DIM:(r + 1) * HEAD_DIM] = part_ref[r].astype(o_ref.dtype)


def _overlap_map(ncp, nsp, ns):
    ratio = CMP_BLOCK // CMP_STRIDE
    per_sel = SEL_BLOCK // CMP_STRIDE
    m = np.zeros((ncp, nsp), np.float32)
    for b in range(ns):
        for mm_ in range(per_sel):
            for n in range(ratio):
                j = per_sel * b + mm_ - n
                if 0 <= j < ncp:
                    m[j, b] += 1.0
    return m


def attn_prompt(q, qr, ck, cv, kvb, gates, *, nc):
    B, T, HD = q.shape
    G = N_KV_HEADS
    R = HD // HEAD_DIM // G
    tq = min(T, 256)
    kc = tq
    ns = T // SEL_BLOCK
    nsp = -(-ns // 8) * 8
    ncp = ck.shape[2]
    mapT = jnp.asarray(_overlap_map(ncp, nsp, ns).T)
    expand = jnp.asarray((np.arange(T)[None, :] // SEL_BLOCK == np.arange(nsp)[:, None]).astype(np.float32), BF16)
    q_spec = pl.BlockSpec((1, tq, R * HEAD_DIM), lambda b, g, i: (b, i, g))
    c_spec = pl.BlockSpec((1, 1, ncp, HEAD_DIM), lambda b, g, i: (b, g, 0, 0))
    kv_spec = lambda n: pl.BlockSpec((None, 1, T, HEAD_DIM), lambda b, g, i: (n, b, 0, g))
    return pl.pallas_call(
        functools.partial(_attn_prompt_kernel, tq=tq, kc=kc, nc=nc, ns=ns, R=R),
        grid=(B, G, T // tq),
        in_specs=[q_spec, q_spec, c_spec, c_spec, kv_spec(2), kv_spec(3), kv_spec(4), kv_spec(5),
                  pl.BlockSpec((1, tq, LANE), lambda b, g, i: (b, i, g)),
                  pl.BlockSpec((nsp, ncp), lambda b, g, i: (0, 0)),
                  pl.BlockSpec((nsp, T), lambda b, g, i: (0, 0))],
        out_specs=q_spec,
        out_shape=jax.ShapeDtypeStruct((B, T, HD), BF16),
        scratch_shapes=[pltpu.VMEM((R, tq, HEAD_DIM), F32), pltpu.VMEM((T // kc, tq, kc), F32),
                        pltpu.VMEM((min(WINDOW, T) // kc + tq // kc, tq, kc), F32)]
                       + [pltpu.VMEM((R, tq, HEAD_DIM), F32), pltpu.VMEM((R, tq, HEAD_DIM + LANE), F32)],
        compiler_params=_params(("parallel", "parallel", "arbitrary")),
        name="attn_prompt",
    )(q, qr, ck, cv, kvb, kvb, kvb, kvb, gates, mapT, expand)


def _attn_dec_dense_kernel(q_ref, qr_ref, ck_ref, cv_ref, kw_ref, vw_ref, map_ref, ocmp_ref, owin_ref, ids_ref,
                           *, nc, ns, pos):
    G, R = q_ref.shape[1], q_ref.shape[2]
    ncp = ck_ref.shape[2]
    nsl = map_ref.shape[1]
    cidx = lax.broadcasted_iota(jnp.int32, (1, ncp), 1)
    ok_c = (cidx * CMP_STRIDE + CMP_BLOCK - 1 <= pos) & (cidx < nc)
    blk_l = lax.broadcasted_iota(jnp.int32, (1, nsl), 1)
    cur = pos // SEL_BLOCK
    vis = blk_l * SEL_BLOCK <= pos
    forced = vis & ((blk_l == 0) | (blk_l == cur) | (blk_l == cur - 1))
    ii = lax.broadcasted_iota(jnp.int32, (nsl, nsl), 0)
    jj = lax.broadcasted_iota(jnp.int32, (nsl, nsl), 1)
    slot = lax.broadcasted_iota(jnp.int32, (nsl, LANE), 1).astype(F32)
    blk_s = lax.broadcasted_iota(jnp.int32, (nsl, LANE), 0).astype(F32)
    for g in range(G):
        s = jnp.where(ok_c, _dot_nt(q_ref[0, g], ck_ref[0, g]) * SCALE, NEG)
        e = jnp.exp(s - jnp.max(s, axis=1, keepdims=True))
        p = jnp.where(ok_c, e * (1.0 / jnp.sum(e, axis=1, keepdims=True)), 0.0)
        ocmp_ref[0, g] = _dot(p.astype(BF16), cv_ref[0, g])
        imp = jnp.broadcast_to(jnp.sum(p, axis=0, keepdims=True), (R, ncp))
        p_slc = jnp.dot(imp, map_ref[...], precision=lax.Precision.HIGHEST, preferred_element_type=F32)[0:1, :]
        score_l = jnp.where(forced, FORCE, jnp.where(vis, p_slc, NEG))
        score_l = jnp.where(blk_l < ns, score_l, PAD_SCORE)
        score_s = jnp.sum(jnp.where(ii == jj, score_l, 0.0), axis=1, keepdims=True)
        beats = (score_l > score_s) | ((score_l == score_s) & (jj < ii))
        rank = jnp.sum(beats.astype(F32), axis=1, keepdims=True)
        ids = jnp.sum(jnp.where(rank == slot, blk_s, 0.0), axis=0, keepdims=True)
        ids_ref[0, g] = ids[:, :N_SEL].astype(jnp.int32)
        kw = kw_ref[0, :, g, :].astype(BF16)
        vw = vw_ref[0, :, g, :].astype(BF16)
        s = _dot_nt(qr_ref[0, g], kw) * SCALE
        e = jnp.exp(s - jnp.max(s, axis=1, keepdims=True))
        p = e * (1.0 / jnp.sum(e, axis=1, keepdims=True))
        owin_ref[0, g] = _dot(p.astype(BF16), vw)


def attn_dec_dense(q, qr, ck, cv, kw, vw, *, nc, ns, pos):
    B, G, R, _ = q.shape
    ncp = ck.shape[2]
    wb = kw.shape[1]
    nsl = -(-ns // LANE) * LANE
    omap = jnp.asarray(_overlap_map(ncp, nsl, ns))
    q_spec = pl.BlockSpec((1, G, R, HEAD_DIM), lambda b: (b, 0, 0, 0))
    c_spec = pl.BlockSpec((1, G, ncp, HEAD_DIM), lambda b: (b, 0, 0, 0))
    w_spec = pl.BlockSpec((1, wb, G, HEAD_DIM), lambda b: (b, 0, 0, 0))
    return pl.pallas_call(
        functools.partial(_attn_dec_dense_kernel, nc=nc, ns=ns, pos=pos),
        grid=(B,),
        in_specs=[q_spec, q_spec, c_spec, c_spec, w_spec, w_spec, pl.BlockSpec((ncp, nsl), lambda b: (0, 0))],
        out_specs=[q_spec, q_spec, pl.BlockSpec((1, G, 1, N_SEL), lambda b: (b, 0, 0, 0))],
        out_shape=[jax.ShapeDtypeStruct((B, G, R, HEAD_DIM), F32), jax.ShapeDtypeStruct((B, G, R, HEAD_DIM), F32),
                   jax.ShapeDtypeStruct((B, G, 1, N_SEL), jnp.int32)],
        compiler_params=_params(("parallel",)),
        name="attn_dec_dense",
    )(q, qr, ck, cv, kw, vw, omap)


DEC_BLOCKS_PER_STEP = 4


def _attn_dec_sel_kernel(pt_ref, ids_ref, qr_ref, *refs, n_past, pos):
    del pt_ref
    G = qr_ref.shape[1]
    nb = DEC_BLOCKS_PER_STEP
    kc_refs, vc_refs = refs[:G * nb], refs[G * nb:2 * G * nb]
    kn_ref, vn_ref, ocmp_ref, owin_ref, gate_ref, o_ref, m_ref, l_ref, acc_ref = refs[2 * G * nb:]
    b, n = pl.program_id(0), pl.program_id(1)

    @pl.when(n == 0)
    def _():
        m_ref[...] = jnp.full(m_ref.shape, NEG, F32)
        l_ref[...] = jnp.zeros(l_ref.shape, F32)
        acc_ref[...] = jnp.zeros(acc_ref.shape, F32)

    row = lax.broadcasted_iota(jnp.int32, (SEL_BLOCK, 1), 0)
    lane = lax.broadcasted_iota(jnp.int32, (1, SEL_BLOCK), 1)
    key_row = jnp.where(lane < SEL_BLOCK // 2, 2 * lane, 2 * lane - (SEL_BLOCK - 1))
    head_rows = lambda ref, g: jnp.concatenate([ref[0, :, g, :], ref[0, :, G + g, :]], axis=0)
    for g, u in [(g, u) for g in range(G) for u in range(nb)]:
        bid = ids_ref[(b * G + g) * N_SEL + n * nb + u]
        is_new = bid >= n_past
        first = (row == 0) & (bid == n_past)
        sl = slice(g * HEAD_DIM, (g + 1) * HEAD_DIM)
        k = jnp.where(is_new, jnp.where(first, kn_ref[0, :, sl], 0.0),
                      head_rows(kc_refs[g * nb + u], g)).astype(BF16)
        v = jnp.where(is_new, jnp.where(first, vn_ref[0, :, sl], 0.0),
                      head_rows(vc_refs[g * nb + u], g)).astype(BF16)
        kpos = bid * SEL_BLOCK + key_row
        ok = kpos <= pos
        s = jnp.where(ok, _dot_nt(qr_ref[0, g], k) * SCALE, NEG)
        m_prev = m_ref[g]
        m_new = jnp.maximum(m_prev, jnp.max(s, axis=1, keepdims=True))
        alpha = jnp.exp(m_prev - m_new)
        p = jnp.where(ok, jnp.exp(s - m_new), 0.0)
        l_ref[g] = alpha * l_ref[g] + jnp.sum(p, axis=1, keepdims=True)
        acc_ref[g] = alpha * acc_ref[g] + _dot(p.astype(BF16), v)
        m_ref[g] = m_new

    @pl.when(n == N_SEL // nb - 1)
    def _():
        for g in range(G):
            gt = gate_ref[0, g]
            o_sel = acc_ref[g] * (1.0 / l_ref[g])
            o_ref[0, g] = gt[:, 0:1] * ocmp_ref[0, g] + gt[:, 1:2] * o_sel + gt[:, 2:3] * owin_ref[0, g]


def attn_dec_sel(table, ids, qr, k_cache, v_cache, k_new, v_new, ocmp, owin, gates, *, pos):
    B, G, R, _ = qr.shape
    page = k_cache.shape[1]
    bpp = page // SEL_BLOCK
    n_past = table.shape[1] * bpp
    kc = k_cache.reshape(k_cache.shape[0] * bpp, SEL_BLOCK // 2, 2 * G, HEAD_DIM)
    vc = v_cache.reshape(v_cache.shape[0] * bpp, SEL_BLOCK // 2, 2 * G, HEAD_DIM)

    nb = DEC_BLOCKS_PER_STEP

    def cache_spec(g, u):
        def index(b, n, pt, ids_):
            bid = jnp.minimum(ids_[(b * G + g) * N_SEL + n * nb + u], n_past - 1)
            return (pt[b, bid // bpp] * bpp + bid % bpp, 0, 0, 0)
        return pl.BlockSpec((1, SEL_BLOCK // 2, 2 * G, HEAD_DIM), index)

    q_spec = pl.BlockSpec((1, G, R, HEAD_DIM), lambda b, n, pt, ids_: (b, 0, 0, 0))
    n_spec = pl.BlockSpec((1, 1, G * HEAD_DIM), lambda b, n, pt, ids_: (b, 0, 0))
    g_spec = pl.BlockSpec((1, G, R, LANE), lambda b, n, pt, ids_: (b, 0, 0, 0))
    c_specs = [cache_spec(g, u) for g in range(G) for u in range(nb)]
    return pl.pallas_call(
        functools.partial(_attn_dec_sel_kernel, n_past=n_past, pos=pos),
        grid_spec=pltpu.PrefetchScalarGridSpec(
            num_scalar_prefetch=2,
            grid=(B, N_SEL // nb),
            in_specs=[q_spec] + c_specs + c_specs + [n_spec, n_spec, q_spec, q_spec, g_spec],
            out_specs=q_spec,
            scratch_shapes=[pltpu.VMEM((G, R, 1), F32), pltpu.VMEM((G, R, 1), F32),
                            pltpu.VMEM((G, R, HEAD_DIM), F32)],
        ),
        out_shape=jax.ShapeDtypeStruct((B, G, R, HEAD_DIM), F32),
        compiler_params=_params(("parallel", "arbitrary")),
        name="attn_dec_sel",
    )(table, ids.reshape(-1), qr, *([kc] * (G * nb)), *([vc] * (G * nb)), k_new, v_new, ocmp, owin, gates)


def _rope_tables(pos):
    half = HEAD_DIM // 2
    inv = ROPE_THETA ** (-jnp.arange(half, dtype=F32) / half)
    ang = pos.astype(F32)[:, None] * inv[None, :]
    cos, sin = jnp.cos(ang), jnp.sin(ang)
    return jnp.concatenate([cos, cos], axis=1), jnp.concatenate([-sin, sin], axis=1)


def _prep_weights(W):
    D = W['w_kv'].shape[0]
    H = D // HEAD_DIM
    R = H // N_KV_HEADS
    pg = D // len(POOL_WINDOWS)
    half = CMP_STRIDE * HEAD_DIM
    P = {}
    P['w_pool'] = W['w_pool'].astype(BF16).reshape(-1, len(POOL_WINDOWS) * pg, pg)
    kv_gain = jnp.ones((W['w_kv'].shape[1], KV_W), F32)
    kv_gain = kv_gain.at[2].set(jnp.tile(W['g_k_sel'], N_KV_HEADS)).at[4].set(jnp.tile(W['g_k_win'], N_KV_HEADS))
    P['kv_gain'] = kv_gain
    for t in ('k', 'v'):
        w1 = W['w_cmp_%s1' % t].astype(BF16)
        P['w_cmp_%s1' % t] = w1
        P['w_cmp_%scat' % t] = jnp.concatenate([w1[:half], w1[half:]], axis=1)
        P['w_cmp_%s2' % t] = W['w_cmp_%s2' % t].astype(BF16)
        P['pe_%s' % t] = W['pe_cmp_%s' % t].astype(BF16).reshape(1, -1)
    n_b = W['w_qg'].shape[0]
    wg = W['w_qg'][:, :, H * HEAD_DIM:].astype(BF16).reshape(n_b, D, N_KV_HEADS, R * N_BRANCH)
    wg = jnp.pad(wg, ((0, 0), (0, 0), (0, 0), (0, LANE - R * N_BRANCH)))
    P['w_gate'] = wg.reshape(n_b, D, N_KV_HEADS * LANE)
    P['w_ple'] = W['w_ple'].astype(BF16)
    P[('w_qg', 0)] = W['w_qg'][0, :, :H * HEAD_DIM].astype(BF16)
    return P


def _dense(x, wname, layer, W, P, *, N, epilogue, specs, extras=(), out_dtypes, side_cast=None, row_scale_in=None,
           norm_gains=None, name):
    M, K = x.shape
    shapes = [d if isinstance(d, jax.ShapeDtypeStruct) else jax.ShapeDtypeStruct((M, N), d) for d in out_dtypes]
    key = (wname, layer)
    tmx = 1024 if M >= 1024 else M
    use_mm = key in P
    if use_mm:
        grid, tm, tn, tk = _tiles(M, N, K, tmx, 1024, 2048 if (M >= 1024 and K > 4096) else 4096)
    else:
        grid, tm, tn, tk = _tiles(M, N, K, tmx, 512, K if K <= 4096 else 2048)
    extra_specs, out_specs = specs(tm, tn)
    extras = list(extras)
    if row_scale_in is not None:
        epilogue = _row_scaled(epilogue)
        extras.insert(0, row_scale_in)
        extra_specs = [pl.BlockSpec((tm, LANE), lambda i, j, k: (i, 0))] + list(extra_specs)
    if norm_gains:
        epilogue = _norm_producer(epilogue, len(extras))
        n_ex, n_ex_specs, n_shapes, n_specs = _norm_io(norm_gains, M, N, grid[1], tm, tn)
        extras, extra_specs = extras + n_ex, list(extra_specs) + n_ex_specs
        shapes, out_specs = shapes + n_shapes, list(out_specs) + n_specs
    side = None if side_cast is None else (W[side_cast[0]], side_cast[1])
    if use_mm:
        outs = list(mm(x, P[key], grid=grid, tm=tm, tn=tn, tk=tk, epilogue=epilogue, extras=extras,
                       extra_specs=extra_specs, out_shapes=shapes, out_specs=out_specs, side=side, name=name))
    else:
        w = W[wname]
        outs = list(mm_ws(x, w, layer=layer if w.ndim == 3 else None, side=side, grid=grid, tm=tm, tn=tn, tk=tk,
                          epilogue=epilogue, extras=extras, extra_specs=extra_specs, out_shapes=shapes,
                          out_specs=out_specs, name=name))
    if side is not None:
        P[side_cast] = outs.pop()
    if norm_gains:
        outs.append(row_scale(outs.pop(), N))
    return outs


def _ffn_ple(h, normed, p_l, layer, W, P, next_gains):
    M, D = h.shape
    F = W['w_up'].shape[2]
    host = M >= 1024
    next_up = ('w_up', layer + 1) if host and layer + 1 < W['w_up'].shape[0] else None
    mn = lambda tm, tn: ([], [_mn_spec(tm, tn)])
    res = lambda tm, tn: ([_mn_spec(tm, tn)], [_mn_spec(tm, tn)])
    y, rs = normed
    (u,) = _dense(y, 'w_up', layer, W, P, N=F, epilogue=lambda acc, rows: (jnp.square(jnp.maximum(acc, 0.0)),),
                  specs=mn, out_dtypes=[BF16], side_cast=('w_down', layer) if host else None, row_scale_in=rs,
                  name="ffn_up")
    h, y, rs = _dense(u, 'w_down', layer, W, P, N=D, epilogue=lambda acc, rows, r: (r[rows, :] + acc,), specs=res,
                      extras=[h], out_dtypes=[F32], norm_gains=[W['g_ple'][layer]], side_cast=next_up,
                      name="ffn_down")
    ple_dim = p_l.shape[1]
    ple_specs = lambda tm, tn: ([_mn_spec(tm, tn), pl.BlockSpec((tm, ple_dim), lambda i, j, k: (i, 0)),
                                 pl.BlockSpec((ple_dim, tn), lambda i, j, k: (0, j))], [_mn_spec(tm, tn)])
    return _dense(y, 'w_ple_gate', layer, W, P, N=D,
                  epilogue=lambda acc, rows, r, pp, wp: (r[rows, :] + _dot(pp[rows, :], wp[...]) * _sigmoid(acc),),
                  specs=ple_specs, extras=[h, p_l.astype(BF16), P['w_ple'][layer]], out_dtypes=[F32],
                  row_scale_in=rs, norm_gains=next_gains, name="ple")


def _kv_epilogue(acc, rows, gain_ref, cos_ref, sin_ref):
    j = pl.program_id(0)
    cosf, sinf = cos_ref[rows, :], sin_ref[rows, :]
    heads = []
    for hh in range(N_KV_HEADS):
        sl = slice(hh * HEAD_DIM, (hh + 1) * HEAD_DIM)
        heads.append(_rope(_head_norm(acc[:, sl], gain_ref[0, :, sl]), cosf, sinf))
    out = jnp.where((j == 2) | (j == 4), jnp.concatenate(heads, axis=1), acc)
    return out, out


def _q_epilogue(acc, rows, gq_ref, cos_ref, sin_ref):
    cosf, sinf = cos_ref[rows, :], sin_ref[rows, :]
    qs, qrs = [], []
    for hh in range(acc.shape[1] // HEAD_DIM):
        qn = _head_norm(acc[:, hh * HEAD_DIM:(hh + 1) * HEAD_DIM], gq_ref[...])
        qs.append(qn)
        qrs.append(_rope(qn, cosf, sinf))
    return jnp.concatenate(qs, axis=1), jnp.concatenate(qrs, axis=1)


def _trunk(x, p, pool_prefix, pos0, W, P, attend):
    B, T, D = x.shape
    M = B * T
    tmx = 1024 if M >= 1024 else M
    h = x.reshape(M, D)
    pg = D // len(POOL_WINDOWS)

    first_up = ('w_up', 0)
    d, pool_new, cast = pool_diff(x, W['g_mix'][0], pool_prefix[0], pos0,
                                  side=(W['w_up'], 0) if first_up not in P else None)
    if cast is not None:
        P[first_up] = cast
    d, pool_new = d.reshape(M, D), pool_new[None]
    grid, tm, tn, tk = _tiles(M, D, pg, tmx, pg, pg)
    n_ex, n_ex_specs, n_shapes, n_specs = _norm_io([W['g_ffn'][0]], M, D, grid[1], tm, tn)
    h, y, ssq = mm(d, P['w_pool'][0], grid=grid, tm=tm, tn=tn, tk=tk,
                   x_map=lambda i, j, k: (i, j), w_map=lambda i, j, k: (j, 0),
                   epilogue=_norm_producer(lambda acc, rows, sc, r: (r[rows, :] + acc * sc[...],), 2),
                   extras=[W['pool_scale'][0].reshape(1, D), h] + n_ex,
                   extra_specs=[pl.BlockSpec((1, tn), lambda i, j, k: (0, j)), _mn_spec(tm, tn)] + n_ex_specs,
                   out_shapes=[jax.ShapeDtypeStruct((M, D), F32)] + n_shapes,
                   out_specs=[_mn_spec(tm, tn)] + n_specs, name="pool_mix")
    h, hkv, a1, rs = _ffn_ple(h, (y, row_scale(ssq, D)), p[0].reshape(M, -1), 0, W, P,
                              [W['g_kv'], W['g_mix'][1]])

    pos = pos0 + jnp.tile(jnp.arange(T, dtype=jnp.int32), B)
    cosf, sinf = _rope_tables(pos)
    n_kv = W['w_kv'].shape[1]
    rope_spec = lambda tm: pl.BlockSpec((tm, HEAD_DIM), lambda i, j, k: (i, 0))
    kv_spec = lambda tm: pl.BlockSpec((1, tm, KV_W), lambda i, j, k: (j, i, 0))
    kv4_spec = lambda tm: pl.BlockSpec((1, tm, N_KV_HEADS, HEAD_DIM), lambda i, j, k: (j, i, 0, 0))
    kv_specs = lambda tm, tn: ([pl.BlockSpec((1, 1, KV_W), lambda i, j, k: (j, 0, 0)), rope_spec(tm), rope_spec(tm)],
                               [kv4_spec(tm), kv_spec(tm)])
    kv, kv_b = _dense(hkv, 'w_kv2d', None, W, P, N=n_kv * KV_W, epilogue=_kv_epilogue, specs=kv_specs,
                      extras=[P['kv_gain'].reshape(n_kv, 1, KV_W), cosf, sinf],
                      out_dtypes=[jax.ShapeDtypeStruct((n_kv, M, N_KV_HEADS, HEAD_DIM), F32),
                                  jax.ShapeDtypeStruct((n_kv, M, KV_W), BF16)], row_scale_in=rs, name="kv_proj")
    q_specs = lambda tm, tn: ([pl.BlockSpec((1, HEAD_DIM), lambda i, j, k: (0, 0)), rope_spec(tm), rope_spec(tm)],
                              [_mn_spec(tm, tn)] * 2)
    q, qr = _dense(a1, 'w_qg', 0, W, P, N=D, epilogue=_q_epilogue, specs=q_specs,
                   extras=[W['g_q'][0].reshape(1, HEAD_DIM), cosf, sinf], out_dtypes=[BF16, BF16], row_scale_in=rs,
                   name="q_proj")
    ng = N_KV_HEADS * LANE
    grid, tm, tn, tk = _tiles(M, ng, D, tmx, ng, D)
    (gates,) = mm(a1, P['w_gate'][0], grid=grid, tm=tm, tn=tn, tk=tk,
                  epilogue=_row_scaled(lambda acc, rows: (_sigmoid(acc),)), extras=[rs],
                  extra_specs=[pl.BlockSpec((tm, LANE), lambda i, j, k: (i, 0))],
                  out_shapes=[jax.ShapeDtypeStruct((M, ng), F32)], out_specs=[_mn_spec(tm, tn)], name="gate_proj")

    o, win_state = attend(kv, kv_b, q, qr, gates)

    h, y, rs = _dense(o, 'w_o', 0, W, P, N=D, epilogue=lambda acc, rows, r: (r[rows, :] + acc,), extras=[h],
                      specs=lambda tm, tn: ([_mn_spec(tm, tn)], [_mn_spec(tm, tn)]), out_dtypes=[F32],
                      norm_gains=[W['g_ffn'][1]], name="attn_out")
    (h,) = _ffn_ple(h, (y, rs), p[1].reshape(M, -1), 1, W, P, [])
    rows = tuple(kv[n].reshape(B, T, N_KV_HEADS, HEAD_DIM) for n in range(4))
    return h.reshape(B, T, D), pool_new, rows, win_state


def _compress(k_pages, v_pages, table, cp, W, P):
    abk, abv = cmp_partials(k_pages, v_pages, table, P['w_cmp_kcat'], P['w_cmp_vcat'], cp)
    gain = W['g_k_cmp'].reshape(1, HEAD_DIM)
    ck = cmp_finish(abk, P['pe_k'], P['w_cmp_k1'], P['w_cmp_k2'], gain, norm=True)
    cv = cmp_finish(abv, P['pe_v'], P['w_cmp_v1'], P['w_cmp_v2'], gain, norm=False)
    return ck, cv


def kernel(x_prompt, x_sample, state_pool, cache_k_cmp, cache_v_cmp, cache_k_sel, cache_v_sel, state_k_win, state_v_win, page_table, p_prompt, p_sample, g_mix, w_pool, pool_scale, g_kv, w_kv, g_k_cmp, g_k_sel, g_k_win, w_cmp_k1, w_cmp_k2, pe_cmp_k, w_cmp_v1, w_cmp_v2, pe_cmp_v, w_qg, g_q, w_o, g_ffn, w_up, w_down, g_ple, w_ple, w_ple_gate):
    W = dict(g_mix=g_mix, w_pool=w_pool, pool_scale=pool_scale, g_kv=g_kv, w_kv=w_kv, g_k_cmp=g_k_cmp,
             g_k_sel=g_k_sel, g_k_win=g_k_win, w_cmp_k1=w_cmp_k1, w_cmp_k2=w_cmp_k2, pe_cmp_k=pe_cmp_k,
             w_cmp_v1=w_cmp_v1, w_cmp_v2=w_cmp_v2, pe_cmp_v=pe_cmp_v, w_qg=w_qg, g_q=g_q, w_o=w_o,
             g_ffn=g_ffn, w_up=w_up, w_down=w_down, g_ple=g_ple, w_ple=w_ple, w_ple_gate=w_ple_gate)
    P = _prep_weights(W)
    W['w_kv2d'] = w_kv.reshape(w_kv.shape[0], -1)
    Bp, Tp, D = x_prompt.shape
    Bs, Ts, _ = x_sample.shape
    assert Ts == 1, "the decode path handles one new token per sequence"
    page = cache_k_cmp.shape[1]
    past_len = page_table.shape[1] * page
    R = D // HEAD_DIM // N_KV_HEADS
    assert Tp % page == 0 and past_len % SEL_BLOCK == 0

    def attend_prompt(kv, kv_b, q, qr, gates):
        ppb = Tp // page
        table = jnp.arange(Bp * ppb, dtype=jnp.int32).reshape(Bp, ppb)
        pages = lambda a: a.reshape(-1, page, N_KV_HEADS, HEAD_DIM)
        ck, cv = _compress(pages(kv[0]), pages(kv[1]), table, ppb, W, P)
        nc = Tp // CMP_STRIDE - CMP_BLOCK // CMP_STRIDE + 1
        seq = lambda a: a.reshape(Bp, Tp, -1)
        o = attn_prompt(seq(q), seq(qr), ck, cv, kv_b.reshape(-1, Bp, Tp, KV_W), seq(gates), nc=nc)
        nw = min(WINDOW, Tp)
        win = tuple(kv[n].reshape(Bp, Tp, N_KV_HEADS, HEAD_DIM)[:, -nw:] for n in (4, 5))
        return o.reshape(Bp * Tp, D), win

    def attend_sample(kv, kv_b, q, qr, gates):
        del kv_b
        ck, cv = _compress(cache_k_cmp, cache_v_cmp, page_table, min(32, page_table.shape[1]), W, P)
        nc = (past_len - (CMP_BLOCK - 1)) // CMP_STRIDE + 1
        ns = past_len // SEL_BLOCK + 1
        wb = state_k_win.shape[1]
        new_row = lambda a: a.reshape(Bs, 1, N_KV_HEADS, HEAD_DIM)
        kw = jnp.concatenate([state_k_win, new_row(kv[4])], axis=1)[:, -wb:]
        vw = jnp.concatenate([state_v_win, new_row(kv[5])], axis=1)[:, -wb:]
        heads = lambda a: a.reshape(Bs, N_KV_HEADS, R, HEAD_DIM)
        ocmp, owin, ids = attn_dec_dense(heads(q), heads(qr), ck, cv, kw, vw, nc=nc, ns=ns, pos=past_len)
        gt = gates.reshape(Bs, N_KV_HEADS, LANE)[:, :, :R * N_BRANCH].reshape(Bs, N_KV_HEADS, R, N_BRANCH)
        gt = jnp.pad(gt, ((0, 0), (0, 0), (0, 0), (0, LANE - N_BRANCH)))
        o = attn_dec_sel(page_table, ids, heads(qr), cache_k_sel, cache_v_sel,
                         kv[2].reshape(Bs, 1, KV_W), kv[3].reshape(Bs, 1, KV_W), ocmp, owin, gt, pos=past_len)
        return o.reshape(Bs, D).astype(BF16), (kw, vw)

    pool_zero = jnp.zeros((state_pool.shape[0], Bp, POOL_STATE, D), x_prompt.dtype)
    y_p, pool_p, rows_p, win_p = _trunk(x_prompt, p_prompt, pool_zero, 0, W, P, attend_prompt)
    y_s, pool_s, rows_s, win_s = _trunk(x_sample, p_sample, state_pool, past_len, W, P, attend_sample)
    return (y_p, y_s, pool_p, pool_s, rows_p[0], rows_p[1], rows_p[2], rows_p[3], win_p[0], win_p[1],
            rows_s[0], rows_s[1], rows_s[2], rows_s[3], win_s[0], win_s[1])
```

```python
import functools

import jax
import jax.numpy as jnp
import numpy as np
from jax import lax
from jax.experimental import pallas as pl
from jax.experimental.pallas import tpu as pltpu

F32 = jnp.float32
BF16 = jnp.bfloat16

POOL_WINDOWS = (2, 4, 8, 16)
POOL_STATE = max(POOL_WINDOWS) - 1
POOL_PAD = POOL_STATE + 1
HEAD_DIM = 128
N_KV_HEADS = 4
N_BRANCH = 3
CMP_BLOCK = 32
CMP_STRIDE = 16
CMP_HIDDEN = 2 * HEAD_DIM
SEL_BLOCK = 64
N_SEL = 16
WINDOW = 512
ROPE_THETA = 10000.0
EPS = 1e-6
SCALE = HEAD_DIM ** -0.5
NEG = -1e30
FORCE = 1e9
PAD_SCORE = -3e38
KV_W = N_KV_HEADS * HEAD_DIM
LANE = 128
VMEM_LIMIT = 56 * 1024 * 1024


def _params(sem):
    return pltpu.CompilerParams(dimension_semantics=sem, vmem_limit_bytes=VMEM_LIMIT)


def _sigmoid(x):
    return 1.0 / (1.0 + jnp.exp(-x))


def _dot(a, b):
    return jnp.dot(a, b, preferred_element_type=F32)


def _dot_nt(a, b):
    return lax.dot_general(a, b, (((1,), (1,)), ((), ())), preferred_element_type=F32)


def _dot_tn(a, b):
    return lax.dot_general(a, b, (((0,), (0,)), ((), ())), preferred_element_type=F32)


def _head_norm(x, g):
    return x * lax.rsqrt(jnp.mean(x * x, axis=-1, keepdims=True) + EPS) * g


def _rope(x, cosf, sinf):
    return x * cosf + pltpu.roll(x, HEAD_DIM // 2, 1) * sinf


def _fold_lanes(x):
    parts = [x[:, c * LANE:(c + 1) * LANE] for c in range(x.shape[1] // LANE)]
    return functools.reduce(lambda u, v: u + v, parts)


def _norm_producer(epilogue, n_base):
    def wrapped(acc, rows, *extras):
        (h,) = epilogue(acc, rows, *extras[:n_base])
        return (h, *[(h * g[...]).astype(BF16) for g in extras[n_base:]], _fold_lanes(h * h))
    return wrapped


def _row_scaled(epilogue):
    def wrapped(acc, rows, rs_ref, *extras):
        rs = rs_ref[rows, :]
        acc = jnp.concatenate([acc[:, c * LANE:(c + 1) * LANE] * rs for c in range(acc.shape[1] // LANE)], axis=1)
        return epilogue(acc, rows, *extras)
    return wrapped


def _row_scale_kernel(ssq_ref, o_ref, *, d):
    tot = jnp.sum(functools.reduce(lambda u, v: u + v, [ssq_ref[j] for j in range(ssq_ref.shape[0])]),
                  axis=1, keepdims=True)
    o_ref[...] = jnp.broadcast_to(lax.rsqrt(tot / d + EPS), o_ref.shape)


def row_scale(ssq, d):
    gn, M, _ = ssq.shape
    tm = min(M, 1024)
    return pl.pallas_call(
        functools.partial(_row_scale_kernel, d=d),
        grid=(M // tm,),
        in_specs=[pl.BlockSpec((gn, tm, LANE), lambda i: (0, i, 0))],
        out_specs=pl.BlockSpec((tm, LANE), lambda i: (i, 0)),
        out_shape=jax.ShapeDtypeStruct((M, LANE), F32),
        compiler_params=_params(("parallel",)),
        name="row_scale",
    )(ssq)


def _norm_io(gains, M, N, gn, tm, tn):
    extras = [g.reshape(1, N).astype(F32) for g in gains]
    extra_specs = [pl.BlockSpec((1, tn), lambda i, j, k: (0, j)) for _ in gains]
    shapes = [jax.ShapeDtypeStruct((M, N), BF16) for _ in gains] + [jax.ShapeDtypeStruct((gn, M, LANE), F32)]
    specs = [_mn_spec(tm, tn) for _ in gains] + [pl.BlockSpec((1, tm, LANE), lambda i, j, k: (j, i, 0))]
    return extras, extra_specs, shapes, specs


EPILOGUE_ROWS = 256


def _store_rows(o_ref, rows, r):
    if len(o_ref.shape) == 4:
        for hh in range(o_ref.shape[2]):
            o_ref[0, rows, hh, :] = r[:, hh * HEAD_DIM:(hh + 1) * HEAD_DIM].astype(o_ref.dtype)
    elif len(o_ref.shape) == 3:
        o_ref[0, rows, :] = r.astype(o_ref.dtype)
    else:
        o_ref[rows, :] = r.astype(o_ref.dtype)


def _finish_rows(acc_of, tm, outs, extras, epilogue):
    ch = min(tm, EPILOGUE_ROWS)
    for c in range(tm // ch):
        rows = slice(c * ch, (c + 1) * ch)
        for o_ref, r in zip(outs, epilogue(acc_of(rows), rows, *extras)):
            _store_rows(o_ref, rows, r)


def _k_steps(x_ref, w, acc_ref, nk, outs, extras, epilogue):
    k = pl.program_id(2)

    @pl.when(k == 0)
    def _():
        acc_ref[...] = _dot(x_ref[...], w())

    @pl.when((k > 0) & (k < nk - 1))
    def _():
        acc_ref[...] += _dot(x_ref[...], w())

    @pl.when(k == nk - 1)
    def _():
        _finish_rows(lambda rows: acc_ref[rows, :] + _dot(x_ref[rows, :], w()), x_ref.shape[0], outs, extras,
                     epilogue)


def _side_io(side, n_steps, step):
    s_arr, s_layer = side
    _, rows, cols = s_arr.shape
    rs = rows // n_steps
    assert rs * n_steps == rows and rs % 16 == 0
    in_spec = pl.BlockSpec((None, rs, cols), lambda *g: (s_layer, step(*g), 0))
    out_spec = pl.BlockSpec((rs, cols), lambda *g: (step(*g), 0))
    return s_arr, in_spec, jax.ShapeDtypeStruct((rows, cols), BF16), out_spec


def _mm_kernel(*refs, nk, n_extra, n_out, epilogue, side):
    x_ref, w_ref = refs[0], refs[1]
    extras = refs[2:2 + n_extra]
    n_in = 2 + n_extra + side
    outs = refs[n_in:n_in + n_out]
    tm = x_ref.shape[0]
    if side:
        refs[n_in + n_out][...] = refs[n_in - 1][...].astype(BF16)

    if nk == 1:
        _finish_rows(lambda rows: _dot(x_ref[rows, :], w_ref[...]), tm, outs, extras, epilogue)
    else:
        _k_steps(x_ref, lambda: w_ref[...], refs[-1], nk, outs, extras, epilogue)


def mm(x, w, *, grid, tm, tn, tk, epilogue, extras=(), extra_specs=(), out_shapes, out_specs,
       x_map=None, w_map=None, side=None, name):
    gm, gn, nk = grid
    x_map = x_map or (lambda i, j, k: (i, k))
    w_map = w_map or (lambda i, j, k: (k, j))
    ins, in_specs = [x, w, *extras], [pl.BlockSpec((tm, tk), x_map), pl.BlockSpec((tk, tn), w_map)]
    in_specs += list(extra_specs)
    n_out, out_shapes, out_specs = len(out_shapes), list(out_shapes), list(out_specs)
    if side is not None:
        s_arr, s_in, s_shape, s_out = _side_io(side, gm * gn * nk, lambda i, j, k: (i * gn + j) * nk + k)
        ins.append(s_arr), in_specs.append(s_in), out_shapes.append(s_shape), out_specs.append(s_out)
    kern = functools.partial(_mm_kernel, nk=nk, n_extra=len(extras), n_out=n_out, epilogue=epilogue,
                             side=side is not None)
    return pl.pallas_call(
        kern,
        grid=grid,
        in_specs=in_specs,
        out_specs=out_specs,
        out_shape=out_shapes,
        scratch_shapes=[pltpu.VMEM((tm, tn), F32)] if nk > 1 else [],
        compiler_params=_params(("parallel", "parallel", "arbitrary")),
        name=name,
    )(*ins)


def _mm_ws_kernel(*refs, nk, n_extra, n_out, epilogue, side):
    x_ref, w_ref = refs[0], refs[1]
    extras = refs[2:2 + n_extra]
    n_in = 2 + n_extra + side
    outs = refs[n_in:n_in + n_out]
    rest = refs[n_in + n_out:]
    wb_ref = rest[side]
    i, k = pl.program_id(1), pl.program_id(2)

    @pl.when(i == 0)
    def _():
        wb_ref[k] = w_ref[...].astype(BF16)

    if side:
        rest[0][...] = refs[n_in - 1][...].astype(BF16)

    tm = x_ref.shape[0]
    if nk == 1:
        _finish_rows(lambda rows: _dot(x_ref[rows, :], wb_ref[0]), tm, outs, extras, epilogue)
    else:
        _k_steps(x_ref, lambda: wb_ref[k], rest[-1], nk, outs, extras, epilogue)


def mm_ws(x, w, *, layer=None, side=None, grid, tm, tn, tk, epilogue, extras=(), extra_specs=(),
          out_shapes, out_specs, name):
    gm, gn, nk = grid
    swap = lambda f: (lambda j, i, k: f(i, j, k))
    respec = lambda s: pl.BlockSpec(s.block_shape, swap(s.index_map))
    k_once = lambda i, k: jnp.where(i == 0, k, nk - 1)
    if layer is None:
        w_spec = pl.BlockSpec((tk, tn), lambda j, i, k: (k_once(i, k), j))
    else:
        w_spec = pl.BlockSpec((None, tk, tn), lambda j, i, k: (layer, k_once(i, k), j))
    n_out = len(out_shapes)
    ins, in_specs = [x, w, *extras], [pl.BlockSpec((tm, tk), lambda j, i, k: (i, k)), w_spec]
    in_specs += [respec(s) for s in extra_specs]
    out_shapes, out_specs = list(out_shapes), [respec(s) for s in out_specs]
    if side is not None:
        s_arr, s_in, s_shape, s_out = _side_io(side, gm * gn * nk, lambda j, i, k: (j * gm + i) * nk + k)
        ins.append(s_arr), in_specs.append(s_in), out_shapes.append(s_shape), out_specs.append(s_out)
    kern = functools.partial(_mm_ws_kernel, nk=nk, n_extra=len(extras), n_out=n_out, epilogue=epilogue,
                             side=side is not None)
    return pl.pallas_call(
        kern,
        grid=(gn, gm, nk),
        in_specs=in_specs,
        out_specs=out_specs,
        out_shape=out_shapes,
        scratch_shapes=[pltpu.VMEM((nk, tk, tn), BF16)] + ([pltpu.VMEM((tm, tn), F32)] if nk > 1 else []),
        compiler_params=_params(("parallel", "arbitrary", "arbitrary")),
        name=name,
    )(*ins)


def _tiles(M, N, K, tm, tn, tk):
    tm, tn, tk = min(tm, M), min(tn, N), min(tk, K)
    return (M // tm, N // tn, K // tk), tm, tn, tk


def _mn_spec(tm, tn):
    return pl.BlockSpec((tm, tn), lambda i, j, k: (i, j))


def _pool_diff_kernel(*refs, tt, pos0, halo, side):
    x_ref, pre_ref, g_ref = refs[0], refs[1 + halo], refs[2 + halo]
    n_in = 3 + halo + side
    d_ref, st_ref, seq_ref = refs[n_in], refs[n_in + 1], refs[-1]
    if side:
        refs[n_in + 2][...] = refs[n_in - 1][...].astype(BF16)
    t = pl.program_id(1)
    norm = lambda x: x * lax.rsqrt(jnp.mean(x * x, axis=-1, keepdims=True) + EPS) * g_ref[...]
    a = norm(x_ref[0])
    seq_ref[POOL_PAD:POOL_PAD + tt, :] = a

    @pl.when(t == 0)
    def _():
        seq_ref[0:POOL_PAD, :] = pre_ref[0]

    if halo:
        @pl.when(t > 0)
        def _():
            seq_ref[0:POOL_PAD, :] = norm(refs[1][0])

    pos = pos0 + t * tt + lax.broadcasted_iota(jnp.int32, (tt, 1), 0)
    pg = a.shape[1] // len(POOL_WINDOWS)
    for g, w in enumerate(POOL_WINDOWS):
        cols = slice(g * pg, (g + 1) * pg)
        s = a[:, cols]
        for j in range(1, w):
            s = s + seq_ref[POOL_PAD - j:POOL_PAD - j + tt, cols]
        cnt = jnp.minimum(pos + 1, w).astype(F32)
        d_ref[0, :, cols] = (s / cnt - a[:, cols]).astype(d_ref.dtype)
    st_ref[0] = seq_ref[tt:tt + POOL_PAD, :]


def pool_diff(x, gain, prefix, pos0, side=None):
    B, T, D = x.shape
    tt = min(T, 256)
    halo = T > tt
    pre = jnp.concatenate([jnp.zeros((B, 1, D), F32), prefix], axis=1)
    hpt = tt // POOL_PAD
    in_specs = [pl.BlockSpec((1, tt, D), lambda b, t: (b, t, 0))]
    if halo:
        in_specs.append(pl.BlockSpec((1, POOL_PAD, D), lambda b, t: (b, jnp.maximum(t * hpt - 1, 0), 0)))
    in_specs += [pl.BlockSpec((1, POOL_PAD, D), lambda b, t: (b, 0, 0)), pl.BlockSpec((1, D), lambda b, t: (0, 0))]
    ins = [x, x] if halo else [x]
    ins += [pre, gain.reshape(1, D).astype(F32)]
    out_specs = [pl.BlockSpec((1, tt, D), lambda b, t: (b, t, 0)),
                 pl.BlockSpec((1, POOL_PAD, D), lambda b, t: (b, 0, 0))]
    out_shapes = [jax.ShapeDtypeStruct((B, T, D), BF16), jax.ShapeDtypeStruct((B, POOL_PAD, D), F32)]
    if side is not None:
        nt = T // tt
        s_arr, s_in, s_shape, s_out = _side_io(side, B * nt, lambda b, t: b * nt + t)
        ins.append(s_arr), in_specs.append(s_in), out_shapes.append(s_shape), out_specs.append(s_out)
    d, st, *cast = pl.pallas_call(
        functools.partial(_pool_diff_kernel, tt=tt, pos0=pos0, halo=halo, side=side is not None),
        grid=(B, T // tt),
        in_specs=in_specs,
        out_specs=out_specs,
        out_shape=out_shapes,
        scratch_shapes=[pltpu.VMEM((POOL_PAD + tt, D), F32)],
        compiler_params=_params(("parallel", "arbitrary")),
        name="pool_diff",
    )(*ins)
    return d, st[:, 1:], (cast[0] if cast else None)


PAGES_PER_STEP = 16
CMP_FINISH_ROWS = 512


def _cmp_partial_kernel(pt_ref, *refs, steps, pps):
    del pt_ref
    k_pages, v_pages = refs[:pps], refs[pps:2 * pps]
    wk_ref, wv_ref, abk_ref, abv_ref, xk_ref, xv_ref = refs[2 * pps:]
    p = pl.program_id(2)
    G = N_KV_HEADS
    half = CMP_STRIDE // 2
    rows_per_page = (k_pages[0].shape[1] // half) * G
    for pages, x_ref in ((k_pages, xk_ref), (v_pages, xv_ref)):
        for q, page_ref in enumerate(pages):
            for sb in range(page_ref.shape[1] // half):
                row = q * rows_per_page + sb * G
                for r in range(CMP_STRIDE):
                    x_ref[p, row:row + G, r * HEAD_DIM:(r + 1) * HEAD_DIM] = (
                        page_ref[0, sb * half + r // 2, (r % 2) * G:(r % 2 + 1) * G, :])

    @pl.when(p == steps - 1)
    def _():
        for x_ref, w_ref, ab_ref in ((xk_ref, wk_ref, abk_ref), (xv_ref, wv_ref, abv_ref)):
            x = x_ref[...].reshape(ab_ref.shape[1], x_ref.shape[2])
            ab_ref[0] = _dot(x.astype(BF16), w_ref[...])


def cmp_partials(k_pages, v_pages, table, wk_cat, wv_cat, cp):
    B, ppb = table.shape
    page = k_pages.shape[1]
    pps = PAGES_PER_STEP
    rows_per_page = page // CMP_STRIDE * N_KV_HEADS
    nch, steps = ppb // cp, cp // pps
    m = cp * rows_per_page
    kdim = CMP_STRIDE * HEAD_DIM
    pair_rows = lambda a: a.reshape(a.shape[0], page // 2, 2 * N_KV_HEADS, HEAD_DIM)

    def page_spec(q):
        return pl.BlockSpec((1, page // 2, 2 * N_KV_HEADS, HEAD_DIM),
                            lambda b, c, p, pt: (pt[b, c * cp + p * pps + q], 0, 0, 0))

    page_specs = [page_spec(q) for q in range(pps)]
    w_spec = pl.BlockSpec((kdim, 2 * CMP_HIDDEN), lambda b, c, p, pt: (0, 0))
    out_spec = pl.BlockSpec((1, m, 2 * CMP_HIDDEN), lambda b, c, p, pt: (b, c, 0))
    out_shape = jax.ShapeDtypeStruct((B, ppb * rows_per_page, 2 * CMP_HIDDEN), F32)
    x_scratch = pltpu.VMEM((steps, pps * rows_per_page, kdim), F32)
    return pl.pallas_call(
        functools.partial(_cmp_partial_kernel, steps=steps, pps=pps),
        grid_spec=pltpu.PrefetchScalarGridSpec(
            num_scalar_prefetch=1,
            grid=(B, nch, steps),
            in_specs=page_specs + page_specs + [w_spec, w_spec],
            out_specs=[out_spec, out_spec],
            scratch_shapes=[x_scratch, x_scratch],
        ),
        out_shape=[out_shape, out_shape],
        compiler_params=_params(("parallel", "parallel", "arbitrary")),
        name="cmp_partials",
    )(table, *([pair_rows(k_pages)] * pps), *([pair_rows(v_pages)] * pps), wk_cat, wv_cat)


def _cmp_finish_kernel(ab_ref, pe_ref, w1_ref, w2_ref, g_ref, o_ref, out_ref, *, norm):
    n4 = ab_ref.shape[1]
    G = o_ref.shape[1]
    bias = _dot(jnp.broadcast_to(pe_ref[...], (8, pe_ref.shape[1])), w1_ref[...])[0:1, :]
    ch = min(n4, CMP_FINISH_ROWS)
    for c in range(n4 // ch):
        lo, hi = c * ch, (c + 1) * ch
        first = ab_ref[0, lo:hi, :CMP_HIDDEN]
        if hi + G <= n4:
            second = ab_ref[0, lo + G:hi + G, CMP_HIDDEN:]
        else:
            second = jnp.concatenate([ab_ref[0, lo + G:hi, CMP_HIDDEN:], ab_ref[0, hi - G:hi, CMP_HIDDEN:]], axis=0)
        pre = first + second + bias
        out = _dot((pre * _sigmoid(pre)).astype(BF16), w2_ref[...])
        if norm:
            out = _head_norm(out, g_ref[...])
        out_ref[lo:hi, :] = out
    for g in range(G):
        o_ref[0, g] = out_ref[pl.ds(g, n4 // G, stride=G), :].astype(o_ref.dtype)


def cmp_finish(ab, pe, w1, w2, gain, *, norm):
    B, n4, _ = ab.shape
    G = N_KV_HEADS
    full = lambda a: pl.BlockSpec(a.shape, lambda b: (0,) * a.ndim)
    args = (pe, w1, w2, gain)
    return pl.pallas_call(
        functools.partial(_cmp_finish_kernel, norm=norm),
        grid=(B,),
        in_specs=[pl.BlockSpec((1, n4, 2 * CMP_HIDDEN), lambda b: (b, 0, 0))] + [full(a) for a in args],
        out_specs=pl.BlockSpec((1, G, n4 // G, HEAD_DIM), lambda b: (b, 0, 0, 0)),
        out_shape=jax.ShapeDtypeStruct((B, G, n4 // G, HEAD_DIM), BF16),
        scratch_shapes=[pltpu.VMEM((n4, HEAD_DIM), F32)],
        compiler_params=_params(("parallel",)),
        name="cmp_finish",
    )(ab, *args)


def _select_blocks(score, blk, ns):
    rank = jnp.zeros(score.shape, jnp.int32)
    for j in range(ns):
        sj = score[j:j + 1, :]
        beats = (sj > score) | ((sj == score) & (j < blk))
        rank = rank + beats.astype(jnp.int32)
    return (rank < min(N_SEL, ns)) & (score > 0.5 * NEG)


def _attn_prompt_kernel(q_ref, qr_ref, ck_ref, cv_ref, ks_ref, vs_ref, kw_ref, vw_ref, gate_ref, mapT_ref,
                        expand_ref, o_ref, part_ref, sbias_ref, wbias_ref, m_ref, acc_ref, *, tq, kc, nc, ns, R):
    qi = pl.program_id(2)
    q0 = qi * tq
    pos = q0 + lax.broadcasted_iota(jnp.int32, (tq, 1), 0)
    ncp = ck_ref.shape[2]

    ck = ck_ref[0, 0]
    cv = cv_ref[0, 0]
    cidx = lax.broadcasted_iota(jnp.int32, (1, ncp), 1)
    ok_c = (cidx * CMP_STRIDE + CMP_BLOCK - 1 <= pos) & (cidx < nc)
    imp = jnp.zeros((tq, ncp), F32)
    for r in range(R):
        qh = q_ref[0, :, r * HEAD_DIM:(r + 1) * HEAD_DIM]
        s = jnp.where(ok_c, _dot_nt(qh, ck) * SCALE, NEG)
        e = jnp.exp(s - jnp.max(s, axis=1, keepdims=True))
        p = jnp.where(ok_c, e * (1.0 / jnp.sum(e, axis=1, keepdims=True)), 0.0)
        imp = imp + p
        part_ref[r] = gate_ref[0, :, r * N_BRANCH:r * N_BRANCH + 1] * _dot(p.astype(BF16), cv)

    nsp = mapT_ref.shape[0]
    p_slc = lax.dot_general(mapT_ref[...], imp, (((1,), (1,)), ((), ())), precision=lax.Precision.HIGHEST,
                            preferred_element_type=F32)
    blk = lax.broadcasted_iota(jnp.int32, (nsp, tq), 0)
    pos_l = q0 + lax.broadcasted_iota(jnp.int32, (nsp, tq), 1)
    cur = pos_l // SEL_BLOCK
    vis = blk * SEL_BLOCK <= pos_l
    forced = vis & ((blk == 0) | (blk == cur) | (blk == cur - 1))
    score = jnp.where(forced, FORCE, jnp.where(vis, p_slc, NEG))
    score = jnp.where(blk < ns, score, PAD_SCORE)
    sel = _select_blocks(score, blk, ns).astype(BF16)

    c_hi = (q0 + tq) // kc
    col = lax.broadcasted_iota(jnp.int32, (1, kc), 1)
    sel_keys = _dot_tn(sel, expand_ref[...])
    for c in range(sbias_ref.shape[0]):
        @pl.when(c < c_hi)
        def _(c=c):
            ok = (sel_keys[:, c * kc:(c + 1) * kc] > 0.5) & (c * kc + col <= pos)
            sbias_ref[c] = jnp.where(ok, 0.0, NEG)
    nwc = wbias_ref.shape[0]
    c_w0 = c_hi - nwc
    for d in range(nwc):
        kpos = (c_w0 + d) * kc + col
        wbias_ref[d] = jnp.where((kpos <= pos) & (kpos > pos - WINDOW), 0.0, NEG)

    def fold(t):
        return [t[:, i * LANE:(i + 1) * LANE] for i in range(kc // LANE)]

    ones_blk = jnp.ones((kc, LANE), BF16)

    def branch(k_ref, v_ref, c_lo, bias_of, gate_col):
        def logits(r, c, bias):
            k = k_ref[0, pl.ds(pl.multiple_of(c * kc, kc), kc), :]
            return _dot_nt(qr_ref[0, :, r * HEAD_DIM:(r + 1) * HEAD_DIM], k) * SCALE + bias

        m_ref[...] = jnp.full(m_ref.shape, NEG, F32)
        acc_ref[...] = jnp.zeros(acc_ref.shape, F32)

        def over_chunks(body):
            n = c_hi - c_lo

            def pair(i, carry):
                body(c_lo + 2 * i, carry)
                return body(c_lo + 2 * i + 1, carry)

            lax.fori_loop(0, n // 2, pair, 0)

            @pl.when(n % 2 == 1)
            def _():
                body(c_hi - 1, 0)

        def max_body(c, carry):
            bias = bias_of(c)
            for r in range(R):
                mx = m_ref[r]
                for part in fold(logits(r, c, bias)):
                    mx = jnp.maximum(mx, part)
                m_ref[r] = mx
            return carry

        over_chunks(max_body)
        for r in range(R):
            m_ref[r] = jnp.broadcast_to(jnp.max(m_ref[r], axis=1, keepdims=True), (tq, LANE))

        def sum_body(c, carry):
            bias = bias_of(c)
            v = jnp.concatenate([v_ref[0, pl.ds(pl.multiple_of(c * kc, kc), kc), :], ones_blk], axis=1)
            for r in range(R):
                t = logits(r, c, bias)
                m = m_ref[r]
                ps = [jnp.exp(part - m) for part in fold(t)]
                acc_ref[r] += _dot(jnp.concatenate(ps, axis=1).astype(BF16), v)
            return carry

        over_chunks(sum_body)
        for r in range(R):
            gate = gate_ref[0, :, r * N_BRANCH + gate_col:r * N_BRANCH + gate_col + 1]
            part_ref[r] += gate * (acc_ref[r, :, :HEAD_DIM] * (1.0 / acc_ref[r, :, HEAD_DIM:]))

    branch(ks_ref, vs_ref, 0, lambda c: sbias_ref[c], 1)
    branch(kw_ref, vw_ref, jnp.maximum(c_w0, 0), lambda c: wbias_ref[c - c_w0], 2)
    for r in range(R):
        o_ref[0, :, r * HEAD_DIM:(r + 1) * HEAD_DIM] = part_ref[r].astype(o_ref.dtype)


def _overlap_map(ncp, nsp, ns):
    ratio = CMP_BLOCK // CMP_STRIDE
    per_sel = SEL_BLOCK // CMP_STRIDE
    m = np.zeros((ncp, nsp), np.float32)
    for b in range(ns):
        for mm_ in range(per_sel):
            for n in range(ratio):
                j = per_sel * b + mm_ - n
                if 0 <= j < ncp:
                    m[j, b] += 1.0
    return m


def attn_prompt(q, qr, ck, cv, kvb, gates, *, nc):
    B, T, HD = q.shape
    G = N_KV_HEADS
    R = HD // HEAD_DIM // G
    tq = min(T, 256)
    kc = tq
    ns = T // SEL_BLOCK
    nsp = -(-ns // 8) * 8
    ncp = ck.shape[2]
    mapT = jnp.asarray(_overlap_map(ncp, nsp, ns).T)
    expand = jnp.asarray((np.arange(T)[None, :] // SEL_BLOCK == np.arange(nsp)[:, None]).astype(np.float32), BF16)
    q_spec = pl.BlockSpec((1, tq, R * HEAD_DIM), lambda b, g, i: (b, i, g))
    c_spec = pl.BlockSpec((1, 1, ncp, HEAD_DIM), lambda b, g, i: (b, g, 0, 0))
    kv_spec = lambda n: pl.BlockSpec((None, 1, T, HEAD_DIM), lambda b, g, i: (n, b, 0, g))
    return pl.pallas_call(
        functools.partial(_attn_prompt_kernel, tq=tq, kc=kc, nc=nc, ns=ns, R=R),
        grid=(B, G, T // tq),
        in_specs=[q_spec, q_spec, c_spec, c_spec, kv_spec(2), kv_spec(3), kv_spec(4), kv_spec(5),
                  pl.BlockSpec((1, tq, LANE), lambda b, g, i: (b, i, g)),
                  pl.BlockSpec((nsp, ncp), lambda b, g, i: (0, 0)),
                  pl.BlockSpec((nsp, T), lambda b, g, i: (0, 0))],
        out_specs=q_spec,
        out_shape=jax.ShapeDtypeStruct((B, T, HD), BF16),
        scratch_shapes=[pltpu.VMEM((R, tq, HEAD_DIM), F32), pltpu.VMEM((T // kc, tq, kc), F32),
                        pltpu.VMEM((min(WINDOW, T) // kc + tq // kc, tq, kc), F32)]
                       + [pltpu.VMEM((R, tq, HEAD_DIM), F32), pltpu.VMEM((R, tq, HEAD_DIM + LANE), F32)],
        compiler_params=_params(("parallel", "parallel", "arbitrary")),
        name="attn_prompt",
    )(q, qr, ck, cv, kvb, kvb, kvb, kvb, gates, mapT, expand)


def _attn_dec_dense_kernel(q_ref, qr_ref, ck_ref, cv_ref, kw_ref, vw_ref, map_ref, ocmp_ref, owin_ref, ids_ref,
                           *, nc, ns, pos):
    G, R = q_ref.shape[1], q_ref.shape[2]
    ncp = ck_ref.shape[2]
    nsl = map_ref.shape[1]
    cidx = lax.broadcasted_iota(jnp.int32, (1, ncp), 1)
    ok_c = (cidx * CMP_STRIDE + CMP_BLOCK - 1 <= pos) & (cidx < nc)
    blk_l = lax.broadcasted_iota(jnp.int32, (1, nsl), 1)
    cur = pos // SEL_BLOCK
    vis = blk_l * SEL_BLOCK <= pos
    forced = vis & ((blk_l == 0) | (blk_l == cur) | (blk_l == cur - 1))
    ii = lax.broadcasted_iota(jnp.int32, (nsl, nsl), 0)
    jj = lax.broadcasted_iota(jnp.int32, (nsl, nsl), 1)
    slot = lax.broadcasted_iota(jnp.int32, (nsl, LANE), 1).astype(F32)
    blk_s = lax.broadcasted_iota(jnp.int32, (nsl, LANE), 0).astype(F32)
    for g in range(G):
        s = jnp.where(ok_c, _dot_nt(q_ref[0, g], ck_ref[0, g]) * SCALE, NEG)
        e = jnp.exp(s - jnp.max(s, axis=1, keepdims=True))
        p = jnp.where(ok_c, e * (1.0 / jnp.sum(e, axis=1, keepdims=True)), 0.0)
        ocmp_ref[0, g] = _dot(p.astype(BF16), cv_ref[0, g])
        imp = jnp.broadcast_to(jnp.sum(p, axis=0, keepdims=True), (R, ncp))
        p_slc = jnp.dot(imp, map_ref[...], precision=lax.Precision.HIGHEST, preferred_element_type=F32)[0:1, :]
        score_l = jnp.where(forced, FORCE, jnp.where(vis, p_slc, NEG))
        score_l = jnp.where(blk_l < ns, score_l, PAD_SCORE)
        score_s = jnp.sum(jnp.where(ii == jj, score_l, 0.0), axis=1, keepdims=True)
        beats = (score_l > score_s) | ((score_l == score_s) & (jj < ii))
        rank = jnp.sum(beats.astype(F32), axis=1, keepdims=True)
        ids = jnp.sum(jnp.where(rank == slot, blk_s, 0.0), axis=0, keepdims=True)
        ids_ref[0, g] = ids[:, :N_SEL].astype(jnp.int32)
        kw = kw_ref[0, :, g, :].astype(BF16)
        vw = vw_ref[0, :, g, :].astype(BF16)
        s = _dot_nt(qr_ref[0, g], kw) * SCALE
        e = jnp.exp(s - jnp.max(s, axis=1, keepdims=True))
        p = e * (1.0 / jnp.sum(e, axis=1, keepdims=True))
        owin_ref[0, g] = _dot(p.astype(BF16), vw)


def attn_dec_dense(q, qr, ck, cv, kw, vw, *, nc, ns, pos):
    B, G, R, _ = q.shape
    ncp = ck.shape[2]
    wb = kw.shape[1]
    nsl = -(-ns // LANE) * LANE
    omap = jnp.asarray(_overlap_map(ncp, nsl, ns))
    q_spec = pl.BlockSpec((1, G, R, HEAD_DIM), lambda b: (b, 0, 0, 0))
    c_spec = pl.BlockSpec((1, G, ncp, HEAD_DIM), lambda b: (b, 0, 0, 0))
    w_spec = pl.BlockSpec((1, wb, G, HEAD_DIM), lambda b: (b, 0, 0, 0))
    return pl.pallas_call(
        functools.partial(_attn_dec_dense_kernel, nc=nc, ns=ns, pos=pos),
        grid=(B,),
        in_specs=[q_spec, q_spec, c_spec, c_spec, w_spec, w_spec, pl.BlockSpec((ncp, nsl), lambda b: (0, 0))],
        out_specs=[q_spec, q_spec, pl.BlockSpec((1, G, 1, N_SEL), lambda b: (b, 0, 0, 0))],
        out_shape=[jax.ShapeDtypeStruct((B, G, R, HEAD_DIM), F32), jax.ShapeDtypeStruct((B, G, R, HEAD_DIM), F32),
                   jax.ShapeDtypeStruct((B, G, 1, N_SEL), jnp.int32)],
        compiler_params=_params(("parallel",)),
        name="attn_dec_dense",
    )(q, qr, ck, cv, kw, vw, omap)


DEC_BLOCKS_PER_STEP = 4


def _attn_dec_sel_kernel(pt_ref, ids_ref, qr_ref, *refs, n_past, pos):
    del pt_ref
    G = qr_ref.shape[1]
    nb = DEC_BLOCKS_PER_STEP
    kc_refs, vc_refs = refs[:G * nb], refs[G * nb:2 * G * nb]
    kn_ref, vn_ref, ocmp_ref, owin_ref, gate_ref, o_ref, m_ref, l_ref, acc_ref = refs[2 * G * nb:]
    b, n = pl.program_id(0), pl.program_id(1)

    @pl.when(n == 0)
    def _():
        m_ref[...] = jnp.full(m_ref.shape, NEG, F32)
        l_ref[...] = jnp.zeros(l_ref.shape, F32)
        acc_ref[...] = jnp.zeros(acc_ref.shape, F32)

    row = lax.broadcasted_iota(jnp.int32, (SEL_BLOCK, 1), 0)
    lane = lax.broadcasted_iota(jnp.int32, (1, SEL_BLOCK), 1)
    key_row = jnp.where(lane < SEL_BLOCK // 2, 2 * lane, 2 * lane - (SEL_BLOCK - 1))
    head_rows = lambda ref, g: jnp.concatenate([ref[0, :, g, :], ref[0, :, G + g, :]], axis=0)
    for g, u in [(g, u) for g in range(G) for u in range(nb)]:
        bid = ids_ref[(b * G + g) * N_SEL + n * nb + u]
        is_new = bid >= n_past
        first = (row == 0) & (bid == n_past)
        sl = slice(g * HEAD_DIM, (g + 1) * HEAD_DIM)
        k = jnp.where(is_new, jnp.where(first, kn_ref[0, :, sl], 0.0),
                      head_rows(kc_refs[g * nb + u], g)).astype(BF16)
        v = jnp.where(is_new, jnp.where(first, vn_ref[0, :, sl], 0.0),
                      head_rows(vc_refs[g * nb + u], g)).astype(BF16)
        kpos = bid * SEL_BLOCK + key_row
        ok = kpos <= pos
        s = jnp.where(ok, _dot_nt(qr_ref[0, g], k) * SCALE, NEG)
        m_prev = m_ref[g]
        m_new = jnp.maximum(m_prev, jnp.max(s, axis=1, keepdims=True))
        alpha = jnp.exp(m_prev - m_new)
        p = jnp.where(ok, jnp.exp(s - m_new), 0.0)
        l_ref[g] = alpha * l_ref[g] + jnp.sum(p, axis=1, keepdims=True)
        acc_ref[g] = alpha * acc_ref[g] + _dot(p.astype(BF16), v)
        m_ref[g] = m_new

    @pl.when(n == N_SEL // nb - 1)
    def _():
        for g in range(G):
            gt = gate_ref[0, g]
            o_sel = acc_ref[g] * (1.0 / l_ref[g])
            o_ref[0, g] = gt[:, 0:1] * ocmp_ref[0, g] + gt[:, 1:2] * o_sel + gt[:, 2:3] * owin_ref[0, g]


def attn_dec_sel(table, ids, qr, k_cache, v_cache, k_new, v_new, ocmp, owin, gates, *, pos):
    B, G, R, _ = qr.shape
    page = k_cache.shape[1]
    bpp = page // SEL_BLOCK
    n_past = table.shape[1] * bpp
    kc = k_cache.reshape(k_cache.shape[0] * bpp, SEL_BLOCK // 2, 2 * G, HEAD_DIM)
    vc = v_cache.reshape(v_cache.shape[0] * bpp, SEL_BLOCK // 2, 2 * G, HEAD_DIM)

    nb = DEC_BLOCKS_PER_STEP

    def cache_spec(g, u):
        def index(b, n, pt, ids_):
            bid = jnp.minimum(ids_[(b * G + g) * N_SEL + n * nb + u], n_past - 1)
            return (pt[b, bid // bpp] * bpp + bid % bpp, 0, 0, 0)
        return pl.BlockSpec((1, SEL_BLOCK // 2, 2 * G, HEAD_DIM), index)

    q_spec = pl.BlockSpec((1, G, R, HEAD_DIM), lambda b, n, pt, ids_: (b, 0, 0, 0))
    n_spec = pl.BlockSpec((1, 1, G * HEAD_DIM), lambda b, n, pt, ids_: (b, 0, 0))
    g_spec = pl.BlockSpec((1, G, R, LANE), lambda b, n, pt, ids_: (b, 0, 0, 0))
    c_specs = [cache_spec(g, u) for g in range(G) for u in range(nb)]
    return pl.pallas_call(
        functools.partial(_attn_dec_sel_kernel, n_past=n_past, pos=pos),
        grid_spec=pltpu.PrefetchScalarGridSpec(
            num_scalar_prefetch=2,
            grid=(B, N_SEL // nb),
            in_specs=[q_spec] + c_specs + c_specs + [n_spec, n_spec, q_spec, q_spec, g_spec],
            out_specs=q_spec,
            scratch_shapes=[pltpu.VMEM((G, R, 1), F32), pltpu.VMEM((G, R, 1), F32),
                            pltpu.VMEM((G, R, HEAD_DIM), F32)],
        ),
        out_shape=jax.ShapeDtypeStruct((B, G, R, HEAD_DIM), F32),
        compiler_params=_params(("parallel", "arbitrary")),
        name="attn_dec_sel",
    )(table, ids.reshape(-1), qr, *([kc] * (G * nb)), *([vc] * (G * nb)), k_new, v_new, ocmp, owin, gates)


def _rope_tables(pos):
    half = HEAD_DIM // 2
    inv = ROPE_THETA ** (-jnp.arange(half, dtype=F32) / half)
    ang = pos.astype(F32)[:, None] * inv[None, :]
    cos, sin = jnp.cos(ang), jnp.sin(ang)
    return jnp.concatenate([cos, cos], axis=1), jnp.concatenate([-sin, sin], axis=1)


def _prep_weights(W):
    D = W['w_kv'].shape[0]
    H = D // HEAD_DIM
    R = H // N_KV_HEADS
    pg = D // len(POOL_WINDOWS)
    half = CMP_STRIDE * HEAD_DIM
    P = {}
    P['w_pool'] = W['w_pool'].astype(BF16).reshape(-1, len(POOL_WINDOWS) * pg, pg)
    kv_gain = jnp.ones((W['w_kv'].shape[1], KV_W), F32)
    kv_gain = kv_gain.at[2].set(jnp.tile(W['g_k_sel'], N_KV_HEADS)).at[4].set(jnp.tile(W['g_k_win'], N_KV_HEADS))
    P['kv_gain'] = kv_gain
    for t in ('k', 'v'):
        w1 = W['w_cmp_%s1' % t].astype(BF16)
        P['w_cmp_%s1' % t] = w1
        P['w_cmp_%scat' % t] = jnp.concatenate([w1[:half], w1[half:]], axis=1)
        P['w_cmp_%s2' % t] = W['w_cmp_%s2' % t].astype(BF16)
        P['pe_%s' % t] = W['pe_cmp_%s' % t].astype(BF16).reshape(1, -1)
    n_b = W['w_qg'].shape[0]
    wg = W['w_qg'][:, :, H * HEAD_DIM:].astype(BF16).reshape(n_b, D, N_KV_HEADS, R * N_BRANCH)
    wg = jnp.pad(wg, ((0, 0), (0, 0), (0, 0), (0, LANE - R * N_BRANCH)))
    P['w_gate'] = wg.reshape(n_b, D, N_KV_HEADS * LANE)
    P['w_ple'] = W['w_ple'].astype(BF16)
    P[('w_qg', 0)] = W['w_qg'][0, :, :H * HEAD_DIM].astype(BF16)
    return P


def _dense(x, wname, layer, W, P, *, N, epilogue, specs, extras=(), out_dtypes, side_cast=None, row_scale_in=None,
           norm_gains=None, name):
    M, K = x.shape
    shapes = [d if isinstance(d, jax.ShapeDtypeStruct) else jax.ShapeDtypeStruct((M, N), d) for d in out_dtypes]
    key = (wname, layer)
    tmx = 1024 if M >= 1024 else M
    use_mm = key in P
    if use_mm:
        grid, tm, tn, tk = _tiles(M, N, K, tmx, 1024, 2048 if (M >= 1024 and K > 4096) else 4096)
    else:
        grid, tm, tn, tk = _tiles(M, N, K, tmx, 512, K if K <= 4096 else 2048)
    extra_specs, out_specs = specs(tm, tn)
    extras = list(extras)
    if row_scale_in is not None:
        epilogue = _row_scaled(epilogue)
        extras.insert(0, row_scale_in)
        extra_specs = [pl.BlockSpec((tm, LANE), lambda i, j, k: (i, 0))] + list(extra_specs)
    if norm_gains:
        epilogue = _norm_producer(epilogue, len(extras))
        n_ex, n_ex_specs, n_shapes, n_specs = _norm_io(norm_gains, M, N, grid[1], tm, tn)
        extras, extra_specs = extras + n_ex, list(extra_specs) + n_ex_specs
        shapes, out_specs = shapes + n_shapes, list(out_specs) + n_specs
    side = None if side_cast is None else (W[side_cast[0]], side_cast[1])
    if use_mm:
        outs = list(mm(x, P[key], grid=grid, tm=tm, tn=tn, tk=tk, epilogue=epilogue, extras=extras,
                       extra_specs=extra_specs, out_shapes=shapes, out_specs=out_specs, side=side, name=name))
    else:
        w = W[wname]
        outs = list(mm_ws(x, w, layer=layer if w.ndim == 3 else None, side=side, grid=grid, tm=tm, tn=tn, tk=tk,
                          epilogue=epilogue, extras=extras, extra_specs=extra_specs, out_shapes=shapes,
                          out_specs=out_specs, name=name))
    if side is not None:
        P[side_cast] = outs.pop()
    if norm_gains:
        outs.append(row_scale(outs.pop(), N))
    return outs


def _ffn_ple(h, normed, p_l, layer, W, P, next_gains):
    M, D = h.shape
    F = W['w_up'].shape[2]
    host = M >= 1024
    next_up = ('w_up', layer + 1) if host and layer + 1 < W['w_up'].shape[0] else None
    mn = lambda tm, tn: ([], [_mn_spec(tm, tn)])
    res = lambda tm, tn: ([_mn_spec(tm, tn)], [_mn_spec(tm, tn)])
    y, rs = normed
    (u,) = _dense(y, 'w_up', layer, W, P, N=F, epilogue=lambda acc, rows: (jnp.square(jnp.maximum(acc, 0.0)),),
                  specs=mn, out_dtypes=[BF16], side_cast=('w_down', layer) if host else None, row_scale_in=rs,
                  name="ffn_up")
    h, y, rs = _dense(u, 'w_down', layer, W, P, N=D, epilogue=lambda acc, rows, r: (r[rows, :] + acc,), specs=res,
                      extras=[h], out_dtypes=[F32], norm_gains=[W['g_ple'][layer]], side_cast=next_up,
                      name="ffn_down")
    ple_dim = p_l.shape[1]
    ple_specs = lambda tm, tn: ([_mn_spec(tm, tn), pl.BlockSpec((tm, ple_dim), lambda i, j, k: (i, 0)),
                                 pl.BlockSpec((ple_dim, tn), lambda i, j, k: (0, j))], [_mn_spec(tm, tn)])
    return _dense(y, 'w_ple_gate', layer, W, P, N=D,
                  epilogue=lambda acc, rows, r, pp, wp: (r[rows, :] + _dot(pp[rows, :], wp[...]) * _sigmoid(acc),),
                  specs=ple_specs, extras=[h, p_l.astype(BF16), P['w_ple'][layer]], out_dtypes=[F32],
                  row_scale_in=rs, norm_gains=next_gains, name="ple")


def _kv_epilogue(acc, rows, gain_ref, cos_ref, sin_ref):
    j = pl.program_id(0)
    cosf, sinf = cos_ref[rows, :], sin_ref[rows, :]
    heads = []
    for hh in range(N_KV_HEADS):
        sl = slice(hh * HEAD_DIM, (hh + 1) * HEAD_DIM)
        heads.append(_rope(_head_norm(acc[:, sl], gain_ref[0, :, sl]), cosf, sinf))
    out = jnp.where((j == 2) | (j == 4), jnp.concatenate(heads, axis=1), acc)
    return out, out


def _q_epilogue(acc, rows, gq_ref, cos_ref, sin_ref):
    cosf, sinf = cos_ref[rows, :], sin_ref[rows, :]
    qs, qrs = [], []
    for hh in range(acc.shape[1] // HEAD_DIM):
        qn = _head_norm(acc[:, hh * HEAD_DIM:(hh + 1) * HEAD_DIM], gq_ref[...])
        qs.append(qn)
        qrs.append(_rope(qn, cosf, sinf))
    return jnp.concatenate(qs, axis=1), jnp.concatenate(qrs, axis=1)


def _trunk(x, p, pool_prefix, pos0, W, P, attend):
    B, T, D = x.shape
    M = B * T
    tmx = 1024 if M >= 1024 else M
    h = x.reshape(M, D)
    pg = D // len(POOL_WINDOWS)

    first_up = ('w_up', 0)
    d, pool_new, cast = pool_diff(x, W['g_mix'][0], pool_prefix[0], pos0,
                                  side=(W['w_up'], 0) if first_up not in P else None)
    if cast is not None:
        P[first_up] = cast
    d, pool_new = d.reshape(M, D), pool_new[None]
    grid, tm, tn, tk = _tiles(M, D, pg, tmx, pg, pg)
    n_ex, n_ex_specs, n_shapes, n_specs = _norm_io([W['g_ffn'][0]], M, D, grid[1], tm, tn)
    h, y, ssq = mm(d, P['w_pool'][0], grid=grid, tm=tm, tn=tn, tk=tk,
                   x_map=lambda i, j, k: (i, j), w_map=lambda i, j, k: (j, 0),
                   epilogue=_norm_producer(lambda acc, rows, sc, r: (r[rows, :] + acc * sc[...],), 2),
                   extras=[W['pool_scale'][0].reshape(1, D), h] + n_ex,
                   extra_specs=[pl.BlockSpec((1, tn), lambda i, j, k: (0, j)), _mn_spec(tm, tn)] + n_ex_specs,
                   out_shapes=[jax.ShapeDtypeStruct((M, D), F32)] + n_shapes,
                   out_specs=[_mn_spec(tm, tn)] + n_specs, name="pool_mix")
    h, hkv, a1, rs = _ffn_ple(h, (y, row_scale(ssq, D)), p[0].reshape(M, -1), 0, W, P,
                              [W['g_kv'], W['g_mix'][1]])

    pos = pos0 + jnp.tile(jnp.arange(T, dtype=jnp.int32), B)
    cosf, sinf = _rope_tables(pos)
    n_kv = W['w_kv'].shape[1]
    rope_spec = lambda tm: pl.BlockSpec((tm, HEAD_DIM), lambda i, j, k: (i, 0))
    kv_spec = lambda tm: pl.BlockSpec((1, tm, KV_W), lambda i, j, k: (j, i, 0))
    kv4_spec = lambda tm: pl.BlockSpec((1, tm, N_KV_HEADS, HEAD_DIM), lambda i, j, k: (j, i, 0, 0))
    kv_specs = lambda tm, tn: ([pl.BlockSpec((1, 1, KV_W), lambda i, j, k: (j, 0, 0)), rope_spec(tm), rope_spec(tm)],
                               [kv4_spec(tm), kv_spec(tm)])
    kv, kv_b = _dense(hkv, 'w_kv2d', None, W, P, N=n_kv * KV_W, epilogue=_kv_epilogue, specs=kv_specs,
                      extras=[P['kv_gain'].reshape(n_kv, 1, KV_W), cosf, sinf],
                      out_dtypes=[jax.ShapeDtypeStruct((n_kv, M, N_KV_HEADS, HEAD_DIM), F32),
                                  jax.ShapeDtypeStruct((n_kv, M, KV_W), BF16)], row_scale_in=rs, name="kv_proj")
    q_specs = lambda tm, tn: ([pl.BlockSpec((1, HEAD_DIM), lambda i, j, k: (0, 0)), rope_spec(tm), rope_spec(tm)],
                              [_mn_spec(tm, tn)] * 2)
    q, qr = _dense(a1, 'w_qg', 0, W, P, N=D, epilogue=_q_epilogue, specs=q_specs,
                   extras=[W['g_q'][0].reshape(1, HEAD_DIM), cosf, sinf], out_dtypes=[BF16, BF16], row_scale_in=rs,
                   name="q_proj")
    ng = N_KV_HEADS * LANE
    grid, tm, tn, tk = _tiles(M, ng, D, tmx, ng, D)
    (gates,) = mm(a1, P['w_gate'][0], grid=grid, tm=tm, tn=tn, tk=tk,
                  epilogue=_row_scaled(lambda acc, rows: (_sigmoid(acc),)), extras=[rs],
                  extra_specs=[pl.BlockSpec((tm, LANE), lambda i, j, k: (i, 0))],
                  out_shapes=[jax.ShapeDtypeStruct((M, ng), F32)], out_specs=[_mn_spec(tm, tn)], name="gate_proj")

    o, win_state = attend(kv, kv_b, q, qr, gates)

    h, y, rs = _dense(o, 'w_o', 0, W, P, N=D, epilogue=lambda acc, rows, r: (r[rows, :] + acc,), extras=[h],
                      specs=lambda tm, tn: ([_mn_spec(tm, tn)], [_mn_spec(tm, tn)]), out_dtypes=[F32],
                      norm_gains=[W['g_ffn'][1]], name="attn_out")
    (h,) = _ffn_ple(h, (y, rs), p[1].reshape(M, -1), 1, W, P, [])
    rows = tuple(kv[n].reshape(B, T, N_KV_HEADS, HEAD_DIM) for n in range(4))
    return h.reshape(B, T, D), pool_new, rows, win_state


def _compress(k_pages, v_pages, table, cp, W, P):
    abk, abv = cmp_partials(k_pages, v_pages, table, P['w_cmp_kcat'], P['w_cmp_vcat'], cp)
    gain = W['g_k_cmp'].reshape(1, HEAD_DIM)
    ck = cmp_finish(abk, P['pe_k'], P['w_cmp_k1'], P['w_cmp_k2'], gain, norm=True)
    cv = cmp_finish(abv, P['pe_v'], P['w_cmp_v1'], P['w_cmp_v2'], gain, norm=False)
    return ck, cv


def kernel(x_prompt, x_sample, state_pool, cache_k_cmp, cache_v_cmp, cache_k_sel, cache_v_sel, state_k_win, state_v_win, page_table, p_prompt, p_sample, g_mix, w_pool, pool_scale, g_kv, w_kv, g_k_cmp, g_k_sel, g_k_win, w_cmp_k1, w_cmp_k2, pe_cmp_k, w_cmp_v1, w_cmp_v2, pe_cmp_v, w_qg, g_q, w_o, g_ffn, w_up, w_down, g_ple, w_ple, w_ple_gate):
    W = dict(g_mix=g_mix, w_pool=w_pool, pool_scale=pool_scale, g_kv=g_kv, w_kv=w_kv, g_k_cmp=g_k_cmp,
             g_k_sel=g_k_sel, g_k_win=g_k_win, w_cmp_k1=w_cmp_k1, w_cmp_k2=w_cmp_k2, pe_cmp_k=pe_cmp_k,
             w_cmp_v1=w_cmp_v1, w_cmp_v2=w_cmp_v2, pe_cmp_v=pe_cmp_v, w_qg=w_qg, g_q=g_q, w_o=w_o,
             g_ffn=g_ffn, w_up=w_up, w_down=w_down, g_ple=g_ple, w_ple=w_ple, w_ple_gate=w_ple_gate)
    P = _prep_weights(W)
    W['w_kv2d'] = w_kv.reshape(w_kv.shape[0], -1)
    Bp, Tp, D = x_prompt.shape
    Bs, Ts, _ = x_sample.shape
    assert Ts == 1, "the decode path handles one new token per sequence"
    page = cache_k_cmp.shape[1]
    past_len = page_table.shape[1] * page
    R = D // HEAD_DIM // N_KV_HEADS
    assert Tp % page == 0 and past_len % SEL_BLOCK == 0

    def attend_prompt(kv, kv_b, q, qr, gates):
        ppb = Tp // page
        table = jnp.arange(Bp * ppb, dtype=jnp.int32).reshape(Bp, ppb)
        pages = lambda a: a.reshape(-1, page, N_KV_HEADS, HEAD_DIM)
        ck, cv = _compress(pages(kv[0]), pages(kv[1]), table, ppb, W, P)
        nc = Tp // CMP_STRIDE - CMP_BLOCK // CMP_STRIDE + 1
        seq = lambda a: a.reshape(Bp, Tp, -1)
        o = attn_prompt(seq(q), seq(qr), ck, cv, kv_b.reshape(-1, Bp, Tp, KV_W), seq(gates), nc=nc)
        nw = min(WINDOW, Tp)
        win = tuple(kv[n].reshape(Bp, Tp, N_KV_HEADS, HEAD_DIM)[:, -nw:] for n in (4, 5))
        return o.reshape(Bp * Tp, D), win

    def attend_sample(kv, kv_b, q, qr, gates):
        del kv_b
        ck, cv = _compress(cache_k_cmp, cache_v_cmp, page_table, min(32, page_table.shape[1]), W, P)
        nc = (past_len - (CMP_BLOCK - 1)) // CMP_STRIDE + 1
        ns = past_len // SEL_BLOCK + 1
        wb = state_k_win.shape[1]
        new_row = lambda a: a.reshape(Bs, 1, N_KV_HEADS, HEAD_DIM)
        kw = jnp.concatenate([state_k_win, new_row(kv[4])], axis=1)[:, -wb:]
        vw = jnp.concatenate([state_v_win, new_row(kv[5])], axis=1)[:, -wb:]
        heads = lambda a: a.reshape(Bs, N_KV_HEADS, R, HEAD_DIM)
        ocmp, owin, ids = attn_dec_dense(heads(q), heads(qr), ck, cv, kw, vw, nc=nc, ns=ns, pos=past_len)
        gt = gates.reshape(Bs, N_KV_HEADS, LANE)[:, :, :R * N_BRANCH].reshape(Bs, N_KV_HEADS, R, N_BRANCH)
        gt = jnp.pad(gt, ((0, 0), (0, 0), (0, 0), (0, LANE - N_BRANCH)))
        o = attn_dec_sel(page_table, ids, heads(qr), cache_k_sel, cache_v_sel,
                         kv[2].reshape(Bs, 1, KV_W), kv[3].reshape(Bs, 1, KV_W), ocmp, owin, gt, pos=past_len)
        return o.reshape(Bs, D).astype(BF16), (kw, vw)

    pool_zero = jnp.zeros((state_pool.shape[0], Bp, POOL_STATE, D), x_prompt.dtype)
    y_p, pool_p, rows_p, win_p = _trunk(x_prompt, p_prompt, pool_zero, 0, W, P, attend_prompt)
    y_s, pool_s, rows_s, win_s = _trunk(x_sample, p_sample, state_pool, past_len, W, P, attend_sample)
    return (y_p, y_s, pool_p, pool_s, rows_p[0], rows_p[1], rows_p[2], rows_p[3], win_p[0], win_p[1],
            rows_s[0], rows_s[1], rows_s[2], rows_s[3], win_s[0], win_s[1])
```

```python
import functools

import jax
import jax.numpy as jnp
import numpy as np
from jax import lax
from jax.experimental import pallas as pl
from jax.experimental.pallas import tpu as pltpu

F32 = jnp.float32
BF16 = jnp.bfloat16

POOL_WINDOWS = (2, 4, 8, 16)
POOL_STATE = max(POOL_WINDOWS) - 1
POOL_PAD = POOL_STATE + 1
HEAD_DIM = 128
N_KV_HEADS = 4
N_BRANCH = 3
CMP_BLOCK = 32
CMP_STRIDE = 16
CMP_HIDDEN = 2 * HEAD_DIM
SEL_BLOCK = 64
N_SEL = 16
WINDOW = 512
ROPE_THETA = 10000.0
EPS = 1e-6
SCALE = HEAD_DIM ** -0.5
NEG = -1e30
FORCE = 1e9
PAD_SCORE = -3e38
KV_W = N_KV_HEADS * HEAD_DIM
LANE = 128
VMEM_LIMIT = 56 * 1024 * 1024


def _params(sem):
    return pltpu.CompilerParams(dimension_semantics=sem, vmem_limit_bytes=VMEM_LIMIT)


def _sigmoid(x):
    return 1.0 / (1.0 + jnp.exp(-x))


def _dot(a, b):
    return jnp.dot(a, b, preferred_element_type=F32)


def _dot_nt(a, b):
    return lax.dot_general(a, b, (((1,), (1,)), ((), ())), preferred_element_type=F32)


def _dot_tn(a, b):
    return lax.dot_general(a, b, (((0,), (0,)), ((), ())), preferred_element_type=F32)


def _head_norm(x, g):
    return x * lax.rsqrt(jnp.mean(x * x, axis=-1, keepdims=True) + EPS) * g


def _rope(x, cosf, sinf):
    return x * cosf + pltpu.roll(x, HEAD_DIM // 2, 1) * sinf


def _fold_lanes(x):
    parts = [x[:, c * LANE:(c + 1) * LANE] for c in range(x.shape[1] // LANE)]
    return functools.reduce(lambda u, v: u + v, parts)


def _norm_producer(epilogue, n_base):
    def wrapped(acc, rows, *extras):
        (h,) = epilogue(acc, rows, *extras[:n_base])
        return (h, *[(h * g[...]).astype(BF16) for g in extras[n_base:]], _fold_lanes(h * h))
    return wrapped


def _row_scaled(epilogue):
    def wrapped(acc, rows, rs_ref, *extras):
        rs = rs_ref[rows, :]
        acc = jnp.concatenate([acc[:, c * LANE:(c + 1) * LANE] * rs for c in range(acc.shape[1] // LANE)], axis=1)
        return epilogue(acc, rows, *extras)
    return wrapped


def _row_scale_kernel(ssq_ref, o_ref, *, d):
    tot = jnp.sum(functools.reduce(lambda u, v: u + v, [ssq_ref[j] for j in range(ssq_ref.shape[0])]),
                  axis=1, keepdims=True)
    o_ref[...] = jnp.broadcast_to(lax.rsqrt(tot / d + EPS), o_ref.shape)


def row_scale(ssq, d):
    gn, M, _ = ssq.shape
    tm = min(M, 1024)
    return pl.pallas_call(
        functools.partial(_row_scale_kernel, d=d),
        grid=(M // tm,),
        in_specs=[pl.BlockSpec((gn, tm, LANE), lambda i: (0, i, 0))],
        out_specs=pl.BlockSpec((tm, LANE), lambda i: (i, 0)),
        out_shape=jax.ShapeDtypeStruct((M, LANE), F32),
        compiler_params=_params(("parallel",)),
        name="row_scale",
    )(ssq)


def _norm_io(gains, M, N, gn, tm, tn):
    extras = [g.reshape(1, N).astype(F32) for g in gains]
    extra_specs = [pl.BlockSpec((1, tn), lambda i, j, k: (0, j)) for _ in gains]
    shapes = [jax.ShapeDtypeStruct((M, N), BF16) for _ in gains] + [jax.ShapeDtypeStruct((gn, M, LANE), F32)]
    specs = [_mn_spec(tm, tn) for _ in gains] + [pl.BlockSpec((1, tm, LANE), lambda i, j, k: (j, i, 0))]
    return extras, extra_specs, shapes, specs


EPILOGUE_ROWS = 256


def _store_rows(o_ref, rows, r):
    if len(o_ref.shape) == 4:
        for hh in range(o_ref.shape[2]):
            o_ref[0, rows, hh, :] = r[:, hh * HEAD_DIM:(hh + 1) * HEAD_DIM].astype(o_ref.dtype)
    elif len(o_ref.shape) == 3:
        o_ref[0, rows, :] = r.astype(o_ref.dtype)
    else:
        o_ref[rows, :] = r.astype(o_ref.dtype)


def _finish_rows(acc_of, tm, outs, extras, epilogue):
    ch = min(tm, EPILOGUE_ROWS)
    for c in range(tm // ch):
        rows = slice(c * ch, (c + 1) * ch)
        for o_ref, r in zip(outs, epilogue(acc_of(rows), rows, *extras)):
            _store_rows(o_ref, rows, r)


def _k_steps(x_ref, w, acc_ref, nk, outs, extras, epilogue):
    k = pl.program_id(2)

    @pl.when(k == 0)
    def _():
        acc_ref[...] = _dot(x_ref[...], w())

    @pl.when((k > 0) & (k < nk - 1))
    def _():
        acc_ref[...] += _dot(x_ref[...], w())

    @pl.when(k == nk - 1)
    def _():
        _finish_rows(lambda rows: acc_ref[rows, :] + _dot(x_ref[rows, :], w()), x_ref.shape[0], outs, extras,
                     epilogue)


def _side_io(side, n_steps, step):
    s_arr, s_layer = side
    _, rows, cols = s_arr.shape
    rs = rows // n_steps
    assert rs * n_steps == rows and rs % 16 == 0
    in_spec = pl.BlockSpec((None, rs, cols), lambda *g: (s_layer, step(*g), 0))
    out_spec = pl.BlockSpec((rs, cols), lambda *g: (step(*g), 0))
    return s_arr, in_spec, jax.ShapeDtypeStruct((rows, cols), BF16), out_spec


def _mm_kernel(*refs, nk, n_extra, n_out, epilogue, side):
    x_ref, w_ref = refs[0], refs[1]
    extras = refs[2:2 + n_extra]
    n_in = 2 + n_extra + side
    outs = refs[n_in:n_in + n_out]
    tm = x_ref.shape[0]
    if side:
        refs[n_in + n_out][...] = refs[n_in - 1][...].astype(BF16)

    if nk == 1:
        _finish_rows(lambda rows: _dot(x_ref[rows, :], w_ref[...]), tm, outs, extras, epilogue)
    else:
        _k_steps(x_ref, lambda: w_ref[...], refs[-1], nk, outs, extras, epilogue)


def mm(x, w, *, grid, tm, tn, tk, epilogue, extras=(), extra_specs=(), out_shapes, out_specs,
       x_map=None, w_map=None, side=None, name):
    gm, gn, nk = grid
    x_map = x_map or (lambda i, j, k: (i, k))
    w_map = w_map or (lambda i, j, k: (k, j))
    ins, in_specs = [x, w, *extras], [pl.BlockSpec((tm, tk), x_map), pl.BlockSpec((tk, tn), w_map)]
    in_specs += list(extra_specs)
    n_out, out_shapes, out_specs = len(out_shapes), list(out_shapes), list(out_specs)
    if side is not None:
        s_arr, s_in, s_shape, s_out = _side_io(side, gm * gn * nk, lambda i, j, k: (i * gn + j) * nk + k)
        ins.append(s_arr), in_specs.append(s_in), out_shapes.append(s_shape), out_specs.append(s_out)
    kern = functools.partial(_mm_kernel, nk=nk, n_extra=len(extras), n_out=n_out, epilogue=epilogue,
                             side=side is not None)
    return pl.pallas_call(
        kern,
        grid=grid,
        in_specs=in_specs,
        out_specs=out_specs,
        out_shape=out_shapes,
        scratch_shapes=[pltpu.VMEM((tm, tn), F32)] if nk > 1 else [],
        compiler_params=_params(("parallel", "parallel", "arbitrary")),
        name=name,
    )(*ins)


def _mm_ws_kernel(*refs, nk, n_extra, n_out, epilogue, side):
    x_ref, w_ref = refs[0], refs[1]
    extras = refs[2:2 + n_extra]
    n_in = 2 + n_extra + side
    outs = refs[n_in:n_in + n_out]
    rest = refs[n_in + n_out:]
    wb_ref = rest[side]
    i, k = pl.program_id(1), pl.program_id(2)

    @pl.when(i == 0)
    def _():
        wb_ref[k] = w_ref[...].astype(BF16)

    if side:
        rest[0][...] = refs[n_in - 1][...].astype(BF16)

    tm = x_ref.shape[0]
    if nk == 1:
        _finish_rows(lambda rows: _dot(x_ref[rows, :], wb_ref[0]), tm, outs, extras, epilogue)
    else:
        _k_steps(x_ref, lambda: wb_ref[k], rest[-1], nk, outs, extras, epilogue)


def mm_ws(x, w, *, layer=None, side=None, grid, tm, tn, tk, epilogue, extras=(), extra_specs=(),
          out_shapes, out_specs, name):
    gm, gn, nk = grid
    swap = lambda f: (lambda j, i, k: f(i, j, k))
    respec = lambda s: pl.BlockSpec(s.block_shape, swap(s.index_map))
    k_once = lambda i, k: jnp.where(i == 0, k, nk - 1)
    if layer is None:
        w_spec = pl.BlockSpec((tk, tn), lambda j, i, k: (k_once(i, k), j))
    else:
        w_spec = pl.BlockSpec((None, tk, tn), lambda j, i, k: (layer, k_once(i, k), j))
    n_out = len(out_shapes)
    ins, in_specs = [x, w, *extras], [pl.BlockSpec((tm, tk), lambda j, i, k: (i, k)), w_spec]
    in_specs += [respec(s) for s in extra_specs]
    out_shapes, out_specs = list(out_shapes), [respec(s) for s in out_specs]
    if side is not None:
        s_arr, s_in, s_shape, s_out = _side_io(side, gm * gn * nk, lambda j, i, k: (j * gm + i) * nk + k)
        ins.append(s_arr), in_specs.append(s_in), out_shapes.append(s_shape), out_specs.append(s_out)
    kern = functools.partial(_mm_ws_kernel, nk=nk, n_extra=len(extras), n_out=n_out, epilogue=epilogue,
                             side=side is not None)
    return pl.pallas_call(
        kern,
        grid=(gn, gm, nk),
        in_specs=in_specs,
        out_specs=out_specs,
        out_shape=out_shapes,
        scratch_shapes=[pltpu.VMEM((nk, tk, tn), BF16)] + ([pltpu.VMEM((tm, tn), F32)] if nk > 1 else []),
        compiler_params=_params(("parallel", "arbitrary", "arbitrary")),
        name=name,
    )(*ins)


def _tiles(M, N, K, tm, tn, tk):
    tm, tn, tk = min(tm, M), min(tn, N), min(tk, K)
    return (M // tm, N // tn, K // tk), tm, tn, tk


def _mn_spec(tm, tn):
    return pl.BlockSpec((tm, tn), lambda i, j, k: (i, j))


def _pool_diff_kernel(*refs, tt, pos0, halo, side):
    x_ref, pre_ref, g_ref = refs[0], refs[1 + halo], refs[2 + halo]
    n_in = 3 + halo + side
    d_ref, st_ref, seq_ref = refs[n_in], refs[n_in + 1], refs[-1]
    if side:
        refs[n_in + 2][...] = refs[n_in - 1][...].astype(BF16)
    t = pl.program_id(1)
    norm = lambda x: x * lax.rsqrt(jnp.mean(x * x, axis=-1, keepdims=True) + EPS) * g_ref[...]
    a = norm(x_ref[0])
    seq_ref[POOL_PAD:POOL_PAD + tt, :] = a

    @pl.when(t == 0)
    def _():
        seq_ref[0:POOL_PAD, :] = pre_ref[0]

    if halo:
        @pl.when(t > 0)
        def _():
            seq_ref[0:POOL_PAD, :] = norm(refs[1][0])

    pos = pos0 + t * tt + lax.broadcasted_iota(jnp.int32, (tt, 1), 0)
    pg = a.shape[1] // len(POOL_WINDOWS)
    for g, w in enumerate(POOL_WINDOWS):
        cols = slice(g * pg, (g + 1) * pg)
        s = a[:, cols]
        for j in range(1, w):
            s = s + seq_ref[POOL_PAD - j:POOL_PAD - j + tt, cols]
        cnt = jnp.minimum(pos + 1, w).astype(F32)
        d_ref[0, :, cols] = (s / cnt - a[:, cols]).astype(d_ref.dtype)
    st_ref[0] = seq_ref[tt:tt + POOL_PAD, :]


def pool_diff(x, gain, prefix, pos0, side=None):
    B, T, D = x.shape
    tt = min(T, 256)
    halo = T > tt
    pre = jnp.concatenate([jnp.zeros((B, 1, D), F32), prefix], axis=1)
    hpt = tt // POOL_PAD
    in_specs = [pl.BlockSpec((1, tt, D), lambda b, t: (b, t, 0))]
    if halo:
        in_specs.append(pl.BlockSpec((1, POOL_PAD, D), lambda b, t: (b, jnp.maximum(t * hpt - 1, 0), 0)))
    in_specs += [pl.BlockSpec((1, POOL_PAD, D), lambda b, t: (b, 0, 0)), pl.BlockSpec((1, D), lambda b, t: (0, 0))]
    ins = [x, x] if halo else [x]
    ins += [pre, gain.reshape(1, D).astype(F32)]
    out_specs = [pl.BlockSpec((1, tt, D), lambda b, t: (b, t, 0)),
                 pl.BlockSpec((1, POOL_PAD, D), lambda b, t: (b, 0, 0))]
    out_shapes = [jax.ShapeDtypeStruct((B, T, D), BF16), jax.ShapeDtypeStruct((B, POOL_PAD, D), F32)]
    if side is not None:
        nt = T // tt
        s_arr, s_in, s_shape, s_out = _side_io(side, B * nt, lambda b, t: b * nt + t)
        ins.append(s_arr), in_specs.append(s_in), out_shapes.append(s_shape), out_specs.append(s_out)
    d, st, *cast = pl.pallas_call(
        functools.partial(_pool_diff_kernel, tt=tt, pos0=pos0, halo=halo, side=side is not None),
        grid=(B, T // tt),
        in_specs=in_specs,
        out_specs=out_specs,
        out_shape=out_shapes,
        scratch_shapes=[pltpu.VMEM((POOL_PAD + tt, D), F32)],
        compiler_params=_params(("parallel", "arbitrary")),
        name="pool_diff",
    )(*ins)
    return d, st[:, 1:], (cast[0] if cast else None)


PAGES_PER_STEP = 16
CMP_FINISH_ROWS = 512


def _cmp_partial_kernel(pt_ref, *refs, steps, pps):
    del pt_ref
    k_pages, v_pages = refs[:pps], refs[pps:2 * pps]
    wk_ref, wv_ref, abk_ref, abv_ref, xk_ref, xv_ref = refs[2 * pps:]
    p = pl.program_id(2)
    G = N_KV_HEADS
    half = CMP_STRIDE // 2
    rows_per_page = (k_pages[0].shape[1] // half) * G
    for pages, x_ref in ((k_pages, xk_ref), (v_pages, xv_ref)):
        for q, page_ref in enumerate(pages):
            for sb in range(page_ref.shape[1] // half):
                row = q * rows_per_page + sb * G
                for r in range(CMP_STRIDE):
                    x_ref[p, row:row + G, r * HEAD_DIM:(r + 1) * HEAD_DIM] = (
                        page_ref[0, sb * half + r // 2, (r % 2) * G:(r % 2 + 1) * G, :])

    @pl.when(p == steps - 1)
    def _():
        for x_ref, w_ref, ab_ref in ((xk_ref, wk_ref, abk_ref), (xv_ref, wv_ref, abv_ref)):
            x = x_ref[...].reshape(ab_ref.shape[1], x_ref.shape[2])
            ab_ref[0] = _dot(x.astype(BF16), w_ref[...])


def cmp_partials(k_pages, v_pages, table, wk_cat, wv_cat, cp):
    B, ppb = table.shape
    page = k_pages.shape[1]
    pps = PAGES_PER_STEP
    rows_per_page = page // CMP_STRIDE * N_KV_HEADS
    nch, steps = ppb // cp, cp // pps
    m = cp * rows_per_page
    kdim = CMP_STRIDE * HEAD_DIM
    pair_rows = lambda a: a.reshape(a.shape[0], page // 2, 2 * N_KV_HEADS, HEAD_DIM)

    def page_spec(q):
        return pl.BlockSpec((1, page // 2, 2 * N_KV_HEADS, HEAD_DIM),
                            lambda b, c, p, pt: (pt[b, c * cp + p * pps + q], 0, 0, 0))

    page_specs = [page_spec(q) for q in range(pps)]
    w_spec = pl.BlockSpec((kdim, 2 * CMP_HIDDEN), lambda b, c, p, pt: (0, 0))
    out_spec = pl.BlockSpec((1, m, 2 * CMP_HIDDEN), lambda b, c, p, pt: (b, c, 0))
    out_shape = jax.ShapeDtypeStruct((B, ppb * rows_per_page, 2 * CMP_HIDDEN), F32)
    x_scratch = pltpu.VMEM((steps, pps * rows_per_page, kdim), F32)
    return pl.pallas_call(
        functools.partial(_cmp_partial_kernel, steps=steps, pps=pps),
        grid_spec=pltpu.PrefetchScalarGridSpec(
            num_scalar_prefetch=1,
            grid=(B, nch, steps),
            in_specs=page_specs + page_specs + [w_spec, w_spec],
            out_specs=[out_spec, out_spec],
            scratch_shapes=[x_scratch, x_scratch],
        ),
        out_shape=[out_shape, out_shape],
        compiler_params=_params(("parallel", "parallel", "arbitrary")),
        name="cmp_partials",
    )(table, *([pair_rows(k_pages)] * pps), *([pair_rows(v_pages)] * pps), wk_cat, wv_cat)


def _cmp_finish_kernel(ab_ref, pe_ref, w1_ref, w2_ref, g_ref, o_ref, out_ref, *, norm):
    n4 = ab_ref.shape[1]
    G = o_ref.shape[1]
    bias = _dot(jnp.broadcast_to(pe_ref[...], (8, pe_ref.shape[1])), w1_ref[...])[0:1, :]
    ch = min(n4, CMP_FINISH_ROWS)
    for c in range(n4 // ch):
        lo, hi = c * ch, (c + 1) * ch
        first = ab_ref[0, lo:hi, :CMP_HIDDEN]
        if hi + G <= n4:
            second = ab_ref[0, lo + G:hi + G, CMP_HIDDEN:]
        else:
            second = jnp.concatenate([ab_ref[0, lo + G:hi, CMP_HIDDEN:], ab_ref[0, hi - G:hi, CMP_HIDDEN:]], axis=0)
        pre = first + second + bias
        out = _dot((pre * _sigmoid(pre)).astype(BF16), w2_ref[...])
        if norm:
            out = _head_norm(out, g_ref[...])
        out_ref[lo:hi, :] = out
    for g in range(G):
        o_ref[0, g] = out_ref[pl.ds(g, n4 // G, stride=G), :].astype(o_ref.dtype)


def cmp_finish(ab, pe, w1, w2, gain, *, norm):
    B, n4, _ = ab.shape
    G = N_KV_HEADS
    full = lambda a: pl.BlockSpec(a.shape, lambda b: (0,) * a.ndim)
    args = (pe, w1, w2, gain)
    return pl.pallas_call(
        functools.partial(_cmp_finish_kernel, norm=norm),
        grid=(B,),
        in_specs=[pl.BlockSpec((1, n4, 2 * CMP_HIDDEN), lambda b: (b, 0, 0))] + [full(a) for a in args],
        out_specs=pl.BlockSpec((1, G, n4 // G, HEAD_DIM), lambda b: (b, 0, 0, 0)),
        out_shape=jax.ShapeDtypeStruct((B, G, n4 // G, HEAD_DIM), BF16),
        scratch_shapes=[pltpu.VMEM((n4, HEAD_DIM), F32)],
        compiler_params=_params(("parallel",)),
        name="cmp_finish",
    )(ab, *args)


def _select_blocks(score, blk, ns):
    rank = jnp.zeros(score.shape, jnp.int32)
    for j in range(ns):
        sj = score[j:j + 1, :]
        beats = (sj > score) | ((sj == score) & (j < blk))
        rank = rank + beats.astype(jnp.int32)
    return (rank < min(N_SEL, ns)) & (score > 0.5 * NEG)


def _attn_prompt_kernel(q_ref, qr_ref, ck_ref, cv_ref, ks_ref, vs_ref, kw_ref, vw_ref, gate_ref, mapT_ref,
                        expand_ref, o_ref, part_ref, sbias_ref, wbias_ref, m_ref, acc_ref, *, tq, kc, nc, ns, R):
    qi = pl.program_id(2)
    q0 = qi * tq
    pos = q0 + lax.broadcasted_iota(jnp.int32, (tq, 1), 0)
    ncp = ck_ref.shape[2]

    ck = ck_ref[0, 0]
    cv = cv_ref[0, 0]
    cidx = lax.broadcasted_iota(jnp.int32, (1, ncp), 1)
    ok_c = (cidx * CMP_STRIDE + CMP_BLOCK - 1 <= pos) & (cidx < nc)
    imp = jnp.zeros((tq, ncp), F32)
    for r in range(R):
        qh = q_ref[0, :, r * HEAD_DIM:(r + 1) * HEAD_DIM]
        s = jnp.where(ok_c, _dot_nt(qh, ck) * SCALE, NEG)
        e = jnp.exp(s - jnp.max(s, axis=1, keepdims=True))
        p = jnp.where(ok_c, e * (1.0 / jnp.sum(e, axis=1, keepdims=True)), 0.0)
        imp = imp + p
        part_ref[r] = gate_ref[0, :, r * N_BRANCH:r * N_BRANCH + 1] * _dot(p.astype(BF16), cv)

    nsp = mapT_ref.shape[0]
    p_slc = lax.dot_general(mapT_ref[...], imp, (((1,), (1,)), ((), ())), precision=lax.Precision.HIGHEST,
                            preferred_element_type=F32)
    blk = lax.broadcasted_iota(jnp.int32, (nsp, tq), 0)
    pos_l = q0 + lax.broadcasted_iota(jnp.int32, (nsp, tq), 1)
    cur = pos_l // SEL_BLOCK
    vis = blk * SEL_BLOCK <= pos_l
    forced = vis & ((blk == 0) | (blk == cur) | (blk == cur - 1))
    score = jnp.where(forced, FORCE, jnp.where(vis, p_slc, NEG))
    score = jnp.where(blk < ns, score, PAD_SCORE)
    sel = _select_blocks(score, blk, ns).astype(BF16)

    c_hi = (q0 + tq) // kc
    col = lax.broadcasted_iota(jnp.int32, (1, kc), 1)
    sel_keys = _dot_tn(sel, expand_ref[...])
    for c in range(sbias_ref.shape[0]):
        @pl.when(c < c_hi)
        def _(c=c):
            ok = (sel_keys[:, c * kc:(c + 1) * kc] > 0.5) & (c * kc + col <= pos)
            sbias_ref[c] = jnp.where(ok, 0.0, NEG)
    nwc = wbias_ref.shape[0]
    c_w0 = c_hi - nwc
    for d in range(nwc):
        kpos = (c_w0 + d) * kc + col
        wbias_ref[d] = jnp.where((kpos <= pos) & (kpos > pos - WINDOW), 0.0, NEG)

    def fold(t):
        return [t[:, i * LANE:(i + 1) * LANE] for i in range(kc // LANE)]

    ones_blk = jnp.ones((kc, LANE), BF16)

    def branch(k_ref, v_ref, c_lo, bias_of, gate_col):
        def logits(r, c, bias):
            k = k_ref[0, pl.ds(pl.multiple_of(c * kc, kc), kc), :]
            return _dot_nt(qr_ref[0, :, r * HEAD_DIM:(r + 1) * HEAD_DIM], k) * SCALE + bias

        m_ref[...] = jnp.full(m_ref.shape, NEG, F32)
        acc_ref[...] = jnp.zeros(acc_ref.shape, F32)

        def over_chunks(body):
            n = c_hi - c_lo

            def pair(i, carry):
                body(c_lo + 2 * i, carry)
                return body(c_lo + 2 * i + 1, carry)

            lax.fori_loop(0, n // 2, pair, 0)

            @pl.when(n % 2 == 1)
            def _():
                body(c_hi - 1, 0)

        def max_body(c, carry):
            bias = bias_of(c)
            for r in range(R):
                mx = m_ref[r]
                for part in fold(logits(r, c, bias)):
                    mx = jnp.maximum(mx, part)
                m_ref[r] = mx
            return carry

        over_chunks(max_body)
        for r in range(R):
            m_ref[r] = jnp.broadcast_to(jnp.max(m_ref[r], axis=1, keepdims=True), (tq, LANE))

        def sum_body(c, carry):
            bias = bias_of(c)
            v = jnp.concatenate([v_ref[0, pl.ds(pl.multiple_of(c * kc, kc), kc), :], ones_blk], axis=1)
            for r in range(R):
                t = logits(r, c, bias)
                m = m_ref[r]
                ps = [jnp.exp(part - m) for part in fold(t)]
                acc_ref[r] += _dot(jnp.concatenate(ps, axis=1).astype(BF16), v)
            return carry

        over_chunks(sum_body)
        for r in range(R):
            gate = gate_ref[0, :, r * N_BRANCH + gate_col:r * N_BRANCH + gate_col + 1]
            part_ref[r] += gate * (acc_ref[r, :, :HEAD_DIM] * (1.0 / acc_ref[r, :, HEAD_DIM:]))

    branch(ks_ref, vs_ref, 0, lambda c: sbias_ref[c], 1)
    branch(kw_ref, vw_ref, jnp.maximum(c_w0, 0), lambda c: wbias_ref[c - c_w0], 2)
    for r in range(R):
        o_ref[0, :, r * HEAD_DIM:(r + 1) * HEAD_DIM] = part_ref[r].astype(o_ref.dtype)


def _overlap_map(ncp, nsp, ns):
    ratio = CMP_BLOCK // CMP_STRIDE
    per_sel = SEL_BLOCK // CMP_STRIDE
    m = np.zeros((ncp, nsp), np.float32)
    for b in range(ns):
        for mm_ in range(per_sel):
            for n in range(ratio):
                j = per_sel * b + mm_ - n
                if 0 <= j < ncp:
                    m[j, b] += 1.0
    return m


def attn_prompt(q, qr, ck, cv, kvb, gates, *, nc):
    B, T, HD = q.shape
    G = N_KV_HEADS
    R = HD // HEAD_DIM // G
    tq = min(T, 256)
    kc = tq
    ns = T // SEL_BLOCK
    nsp = -(-ns // 8) * 8
    ncp = ck.shape[2]
    mapT = jnp.asarray(_overlap_map(ncp, nsp, ns).T)
    expand = jnp.asarray((np.arange(T)[None, :] // SEL_BLOCK == np.arange(nsp)[:, None]).astype(np.float32), BF16)
    q_spec = pl.BlockSpec((1, tq, R * HEAD_DIM), lambda b, g, i: (b, i, g))
    c_spec = pl.BlockSpec((1, 1, ncp, HEAD_DIM), lambda b, g, i: (b, g, 0, 0))
    kv_spec = lambda n: pl.BlockSpec((None, 1, T, HEAD_DIM), lambda b, g, i: (n, b, 0, g))
    return pl.pallas_call(
        functools.partial(_attn_prompt_kernel, tq=tq, kc=kc, nc=nc, ns=ns, R=R),
        grid=(B, G, T // tq),
        in_specs=[q_spec, q_spec, c_spec, c_spec, kv_spec(2), kv_spec(3), kv_spec(4), kv_spec(5),
                  pl.BlockSpec((1, tq, LANE), lambda b, g, i: (b, i, g)),
                  pl.BlockSpec((nsp, ncp), lambda b, g, i: (0, 0)),
                  pl.BlockSpec((nsp, T), lambda b, g, i: (0, 0))],
        out_specs=q_spec,
        out_shape=jax.ShapeDtypeStruct((B, T, HD), BF16),
        scratch_shapes=[pltpu.VMEM((R, tq, HEAD_DIM), F32), pltpu.VMEM((T // kc, tq, kc), F32),
                        pltpu.VMEM((min(WINDOW, T) // kc + tq // kc, tq, kc), F32)]
                       + [pltpu.VMEM((R, tq, HEAD_DIM), F32), pltpu.VMEM((R, tq, HEAD_DIM + LANE), F32)],
        compiler_params=_params(("parallel", "parallel", "arbitrary")),
        name="attn_prompt",
    )(q, qr, ck, cv, kvb, kvb, kvb, kvb, gates, mapT, expand)


def _attn_dec_dense_kernel(q_ref, qr_ref, ck_ref, cv_ref, kw_ref, vw_ref, map_ref, ocmp_ref, owin_ref, ids_ref,
                           *, nc, ns, pos):
    G, R = q_ref.shape[1], q_ref.shape[2]
    ncp = ck_ref.shape[2]
    nsl = map_ref.shape[1]
    cidx = lax.broadcasted_iota(jnp.int32, (1, ncp), 1)
    ok_c = (cidx * CMP_STRIDE + CMP_BLOCK - 1 <= pos) & (cidx < nc)
    blk_l = lax.broadcasted_iota(jnp.int32, (1, nsl), 1)
    cur = pos // SEL_BLOCK
    vis = blk_l * SEL_BLOCK <= pos
    forced = vis & ((blk_l == 0) | (blk_l == cur) | (blk_l == cur - 1))
    ii = lax.broadcasted_iota(jnp.int32, (nsl, nsl), 0)
    jj = lax.broadcasted_iota(jnp.int32, (nsl, nsl), 1)
    slot = lax.broadcasted_iota(jnp.int32, (nsl, LANE), 1).astype(F32)
    blk_s = lax.broadcasted_iota(jnp.int32, (nsl, LANE), 0).astype(F32)
    for g in range(G):
        s = jnp.where(ok_c, _dot_nt(q_ref[0, g], ck_ref[0, g]) * SCALE, NEG)
        e = jnp.exp(s - jnp.max(s, axis=1, keepdims=True))
        p = jnp.where(ok_c, e * (1.0 / jnp.sum(e, axis=1, keepdims=True)), 0.0)
        ocmp_ref[0, g] = _dot(p.astype(BF16), cv_ref[0, g])
        imp = jnp.broadcast_to(jnp.sum(p, axis=0, keepdims=True), (R, ncp))
        p_slc = jnp.dot(imp, map_ref[...], precision=lax.Precision.HIGHEST, preferred_element_type=F32)[0:1, :]
        score_l = jnp.where(forced, FORCE, jnp.where(vis, p_slc, NEG))
        score_l = jnp.where(blk_l < ns, score_l, PAD_SCORE)
        score_s = jnp.sum(jnp.where(ii == jj, score_l, 0.0), axis=1, keepdims=True)
        beats = (score_l > score_s) | ((score_l == score_s) & (jj < ii))
        rank = jnp.sum(beats.astype(F32), axis=1, keepdims=True)
        ids = jnp.sum(jnp.where(rank == slot, blk_s, 0.0), axis=0, keepdims=True)
        ids_ref[0, g] = ids[:, :N_SEL].astype(jnp.int32)
        kw = kw_ref[0, :, g, :].astype(BF16)
        vw = vw_ref[0, :, g, :].astype(BF16)
        s = _dot_nt(qr_ref[0, g], kw) * SCALE
        e = jnp.exp(s - jnp.max(s, axis=1, keepdims=True))
        p = e * (1.0 / jnp.sum(e, axis=1, keepdims=True))
        owin_ref[0, g] = _dot(p.astype(BF16), vw)


def attn_dec_dense(q, qr, ck, cv, kw, vw, *, nc, ns, pos):
    B, G, R, _ = q.shape
    ncp = ck.shape[2]
    wb = kw.shape[1]
    nsl = -(-ns // LANE) * LANE
    omap = jnp.asarray(_overlap_map(ncp, nsl, ns))
    q_spec = pl.BlockSpec((1, G, R, HEAD_DIM), lambda b: (b, 0, 0, 0))
    c_spec = pl.BlockSpec((1, G, ncp, HEAD_DIM), lambda b: (b, 0, 0, 0))
    w_spec = pl.BlockSpec((1, wb, G, HEAD_DIM), lambda b: (b, 0, 0, 0))
    return pl.pallas_call(
        functools.partial(_attn_dec_dense_kernel, nc=nc, ns=ns, pos=pos),
        grid=(B,),
        in_specs=[q_spec, q_spec, c_spec, c_spec, w_spec, w_spec, pl.BlockSpec((ncp, nsl), lambda b: (0, 0))],
        out_specs=[q_spec, q_spec, pl.BlockSpec((1, G, 1, N_SEL), lambda b: (b, 0, 0, 0))],
        out_shape=[jax.ShapeDtypeStruct((B, G, R, HEAD_DIM), F32), jax.ShapeDtypeStruct((B, G, R, HEAD_DIM), F32),
                   jax.ShapeDtypeStruct((B, G, 1, N_SEL), jnp.int32)],
        compiler_params=_params(("parallel",)),
        name="attn_dec_dense",
    )(q, qr, ck, cv, kw, vw, omap)


def _attn_dec_sel_kernel(pt_ref, ids_ref, qr_ref, kn_ref, vn_ref, ocmp_ref, owin_ref, gate_ref, kc_hbm, vc_hbm,
                         o_ref, kbuf, vbuf, sem, *, n_past, pos, bpp):
    b = pl.program_id(0)
    G = qr_ref.shape[1]

    def block_copies(g, n):
        bid = jnp.minimum(ids_ref[(b * G + g) * N_SEL + n], n_past - 1)
        phys = pt_ref[b, bid // bpp] * bpp + bid % bpp
        slot = g * N_SEL + n
        return (pltpu.make_async_copy(kc_hbm.at[phys], kbuf.at[slot], sem.at[0]),
                pltpu.make_async_copy(vc_hbm.at[phys], vbuf.at[slot], sem.at[1]))

    pairs = [(g, n) for g in range(G) for n in range(N_SEL)]
    for g, n in pairs:
        for c in block_copies(g, n):
            c.start()
    for g, n in pairs:
        for c in block_copies(g, n):
            c.wait()

    row = lax.broadcasted_iota(jnp.int32, (SEL_BLOCK, 1), 0)
    lane = lax.broadcasted_iota(jnp.int32, (1, SEL_BLOCK), 1)
    key_row = jnp.where(lane < SEL_BLOCK // 2, 2 * lane, 2 * lane - (SEL_BLOCK - 1))
    head_rows = lambda buf, slot, g: jnp.concatenate([buf[slot, :, g, :], buf[slot, :, G + g, :]], axis=0)
    for g in range(G):
        m = jnp.full((qr_ref.shape[2], 1), NEG, F32)
        l = jnp.zeros((qr_ref.shape[2], 1), F32)
        acc = jnp.zeros((qr_ref.shape[2], HEAD_DIM), F32)
        sl = slice(g * HEAD_DIM, (g + 1) * HEAD_DIM)
        for n in range(N_SEL):
            bid = ids_ref[(b * G + g) * N_SEL + n]
            is_new = bid >= n_past
            first = (row == 0) & (bid == n_past)
            slot = g * N_SEL + n
            k = jnp.where(is_new, jnp.where(first, kn_ref[0, :, sl], 0.0), head_rows(kbuf, slot, g)).astype(BF16)
            v = jnp.where(is_new, jnp.where(first, vn_ref[0, :, sl], 0.0), head_rows(vbuf, slot, g)).astype(BF16)
            ok = bid * SEL_BLOCK + key_row <= pos
            s = jnp.where(ok, _dot_nt(qr_ref[0, g], k) * SCALE, NEG)
            m_new = jnp.maximum(m, jnp.max(s, axis=1, keepdims=True))
            alpha = jnp.exp(m - m_new)
            p = jnp.where(ok, jnp.exp(s - m_new), 0.0)
            l = alpha * l + jnp.sum(p, axis=1, keepdims=True)
            acc = alpha * acc + _dot(p.astype(BF16), v)
            m = m_new
        gt = gate_ref[0, g]
        o_ref[0, g] = gt[:, 0:1] * ocmp_ref[0, g] + gt[:, 1:2] * (acc * (1.0 / l)) + gt[:, 2:3] * owin_ref[0, g]


def attn_dec_sel(table, ids, qr, k_cache, v_cache, k_new, v_new, ocmp, owin, gates, *, pos):
    B, G, R, _ = qr.shape
    page = k_cache.shape[1]
    bpp = page // SEL_BLOCK
    n_past = table.shape[1] * bpp
    blk = (SEL_BLOCK // 2, 2 * G, HEAD_DIM)
    kc = k_cache.reshape((k_cache.shape[0] * bpp,) + blk)
    vc = v_cache.reshape((v_cache.shape[0] * bpp,) + blk)
    q_spec = pl.BlockSpec((1, G, R, HEAD_DIM), lambda b, pt, ids_: (b, 0, 0, 0))
    n_spec = pl.BlockSpec((1, 1, G * HEAD_DIM), lambda b, pt, ids_: (b, 0, 0))
    g_spec = pl.BlockSpec((1, G, R, LANE), lambda b, pt, ids_: (b, 0, 0, 0))
    hbm = pl.BlockSpec(memory_space=pl.ANY)
    return pl.pallas_call(
        functools.partial(_attn_dec_sel_kernel, n_past=n_past, pos=pos, bpp=bpp),
        grid_spec=pltpu.PrefetchScalarGridSpec(
            num_scalar_prefetch=2,
            grid=(B,),
            in_specs=[q_spec, n_spec, n_spec, q_spec, q_spec, g_spec, hbm, hbm],
            out_specs=q_spec,
            scratch_shapes=[pltpu.VMEM((G * N_SEL,) + blk, F32), pltpu.VMEM((G * N_SEL,) + blk, F32),
                            pltpu.SemaphoreType.DMA((2,))],
        ),
        out_shape=jax.ShapeDtypeStruct((B, G, R, HEAD_DIM), F32),
        compiler_params=_params(("arbitrary",)),
        name="attn_dec_sel",
    )(table, ids.reshape(-1), qr, k_new, v_new, ocmp, owin, gates, kc, vc)


def _rope_tables(pos):
    half = HEAD_DIM // 2
    inv = ROPE_THETA ** (-jnp.arange(half, dtype=F32) / half)
    ang = pos.astype(F32)[:, None] * inv[None, :]
    cos, sin = jnp.cos(ang), jnp.sin(ang)
    return jnp.concatenate([cos, cos], axis=1), jnp.concatenate([-sin, sin], axis=1)


def _prep_weights(W):
    D = W['w_kv'].shape[0]
    H = D // HEAD_DIM
    R = H // N_KV_HEADS
    pg = D // len(POOL_WINDOWS)
    half = CMP_STRIDE * HEAD_DIM
    P = {}
    P['w_pool'] = W['w_pool'].astype(BF16).reshape(-1, len(POOL_WINDOWS) * pg, pg)
    kv_gain = jnp.ones((W['w_kv'].shape[1], KV_W), F32)
    kv_gain = kv_gain.at[2].set(jnp.tile(W['g_k_sel'], N_KV_HEADS)).at[4].set(jnp.tile(W['g_k_win'], N_KV_HEADS))
    P['kv_gain'] = kv_gain
    for t in ('k', 'v'):
        w1 = W['w_cmp_%s1' % t].astype(BF16)
        P['w_cmp_%s1' % t] = w1
        P['w_cmp_%scat' % t] = jnp.concatenate([w1[:half], w1[half:]], axis=1)
        P['w_cmp_%s2' % t] = W['w_cmp_%s2' % t].astype(BF16)
        P['pe_%s' % t] = W['pe_cmp_%s' % t].astype(BF16).reshape(1, -1)
    n_b = W['w_qg'].shape[0]
    wg = W['w_qg'][:, :, H * HEAD_DIM:].astype(BF16).reshape(n_b, D, N_KV_HEADS, R * N_BRANCH)
    wg = jnp.pad(wg, ((0, 0), (0, 0), (0, 0), (0, LANE - R * N_BRANCH)))
    P['w_gate'] = wg.reshape(n_b, D, N_KV_HEADS * LANE)
    P['w_ple'] = W['w_ple'].astype(BF16)
    P[('w_qg', 0)] = W['w_qg'][0, :, :H * HEAD_DIM].astype(BF16)
    return P


def _dense(x, wname, layer, W, P, *, N, epilogue, specs, extras=(), out_dtypes, side_cast=None, row_scale_in=None,
           norm_gains=None, name):
    M, K = x.shape
    shapes = [d if isinstance(d, jax.ShapeDtypeStruct) else jax.ShapeDtypeStruct((M, N), d) for d in out_dtypes]
    key = (wname, layer)
    tmx = 1024 if M >= 1024 else M
    use_mm = key in P
    if use_mm:
        grid, tm, tn, tk = _tiles(M, N, K, tmx, 1024, 2048 if (M >= 1024 and K > 4096) else 4096)
    else:
        grid, tm, tn, tk = _tiles(M, N, K, tmx, 512, K if K <= 4096 else 2048)
    extra_specs, out_specs = specs(tm, tn)
    extras = list(extras)
    if row_scale_in is not None:
        epilogue = _row_scaled(epilogue)
        extras.insert(0, row_scale_in)
        extra_specs = [pl.BlockSpec((tm, LANE), lambda i, j, k: (i, 0))] + list(extra_specs)
    if norm_gains:
        epilogue = _norm_producer(epilogue, len(extras))
        n_ex, n_ex_specs, n_shapes, n_specs = _norm_io(norm_gains, M, N, grid[1], tm, tn)
        extras, extra_specs = extras + n_ex, list(extra_specs) + n_ex_specs
        shapes, out_specs = shapes + n_shapes, list(out_specs) + n_specs
    side = None if side_cast is None else (W[side_cast[0]], side_cast[1])
    if use_mm:
        outs = list(mm(x, P[key], grid=grid, tm=tm, tn=tn, tk=tk, epilogue=epilogue, extras=extras,
                       extra_specs=extra_specs, out_shapes=shapes, out_specs=out_specs, side=side, name=name))
    else:
        w = W[wname]
        outs = list(mm_ws(x, w, layer=layer if w.ndim == 3 else None, side=side, grid=grid, tm=tm, tn=tn, tk=tk,
                          epilogue=epilogue, extras=extras, extra_specs=extra_specs, out_shapes=shapes,
                          out_specs=out_specs, name=name))
    if side is not None:
        P[side_cast] = outs.pop()
    if norm_gains:
        outs.append(row_scale(outs.pop(), N))
    return outs


def _ffn_ple(h, normed, p_l, layer, W, P, next_gains):
    M, D = h.shape
    F = W['w_up'].shape[2]
    host = M >= 1024
    next_up = ('w_up', layer + 1) if host and layer + 1 < W['w_up'].shape[0] else None
    mn = lambda tm, tn: ([], [_mn_spec(tm, tn)])
    res = lambda tm, tn: ([_mn_spec(tm, tn)], [_mn_spec(tm, tn)])
    y, rs = normed
    (u,) = _dense(y, 'w_up', layer, W, P, N=F, epilogue=lambda acc, rows: (jnp.square(jnp.maximum(acc, 0.0)),),
                  specs=mn, out_dtypes=[BF16], side_cast=('w_down', layer) if host else None, row_scale_in=rs,
                  name="ffn_up")
    h, y, rs = _dense(u, 'w_down', layer, W, P, N=D, epilogue=lambda acc, rows, r: (r[rows, :] + acc,), specs=res,
                      extras=[h], out_dtypes=[F32], norm_gains=[W['g_ple'][layer]], side_cast=next_up,
                      name="ffn_down")
    ple_dim = p_l.shape[1]
    ple_specs = lambda tm, tn: ([_mn_spec(tm, tn), pl.BlockSpec((tm, ple_dim), lambda i, j, k: (i, 0)),
                                 pl.BlockSpec((ple_dim, tn), lambda i, j, k: (0, j))], [_mn_spec(tm, tn)])
    return _dense(y, 'w_ple_gate', layer, W, P, N=D,
                  epilogue=lambda acc, rows, r, pp, wp: (r[rows, :] + _dot(pp[rows, :], wp[...]) * _sigmoid(acc),),
                  specs=ple_specs, extras=[h, p_l.astype(BF16), P['w_ple'][layer]], out_dtypes=[F32],
                  row_scale_in=rs, norm_gains=next_gains, name="ple")


def _kv_epilogue(acc, rows, gain_ref, cos_ref, sin_ref):
    j = pl.program_id(0)
    cosf, sinf = cos_ref[rows, :], sin_ref[rows, :]
    heads = []
    for hh in range(N_KV_HEADS):
        sl = slice(hh * HEAD_DIM, (hh + 1) * HEAD_DIM)
        heads.append(_rope(_head_norm(acc[:, sl], gain_ref[0, :, sl]), cosf, sinf))
    out = jnp.where((j == 2) | (j == 4), jnp.concatenate(heads, axis=1), acc)
    return out, out


def _q_epilogue(acc, rows, gq_ref, cos_ref, sin_ref):
    cosf, sinf = cos_ref[rows, :], sin_ref[rows, :]
    qs, qrs = [], []
    for hh in range(acc.shape[1] // HEAD_DIM):
        qn = _head_norm(acc[:, hh * HEAD_DIM:(hh + 1) * HEAD_DIM], gq_ref[...])
        qs.append(qn)
        qrs.append(_rope(qn, cosf, sinf))
    return jnp.concatenate(qs, axis=1), jnp.concatenate(qrs, axis=1)


def _trunk(x, p, pool_prefix, pos0, W, P, attend):
    B, T, D = x.shape
    M = B * T
    tmx = 1024 if M >= 1024 else M
    h = x.reshape(M, D)
    pg = D // len(POOL_WINDOWS)

    first_up = ('w_up', 0)
    d, pool_new, cast = pool_diff(x, W['g_mix'][0], pool_prefix[0], pos0,
                                  side=(W['w_up'], 0) if first_up not in P else None)
    if cast is not None:
        P[first_up] = cast
    d, pool_new = d.reshape(M, D), pool_new[None]
    grid, tm, tn, tk = _tiles(M, D, pg, tmx, pg, pg)
    n_ex, n_ex_specs, n_shapes, n_specs = _norm_io([W['g_ffn'][0]], M, D, grid[1], tm, tn)
    h, y, ssq = mm(d, P['w_pool'][0], grid=grid, tm=tm, tn=tn, tk=tk,
                   x_map=lambda i, j, k: (i, j), w_map=lambda i, j, k: (j, 0),
                   epilogue=_norm_producer(lambda acc, rows, sc, r: (r[rows, :] + acc * sc[...],), 2),
                   extras=[W['pool_scale'][0].reshape(1, D), h] + n_ex,
                   extra_specs=[pl.BlockSpec((1, tn), lambda i, j, k: (0, j)), _mn_spec(tm, tn)] + n_ex_specs,
                   out_shapes=[jax.ShapeDtypeStruct((M, D), F32)] + n_shapes,
                   out_specs=[_mn_spec(tm, tn)] + n_specs, name="pool_mix")
    h, hkv, a1, rs = _ffn_ple(h, (y, row_scale(ssq, D)), p[0].reshape(M, -1), 0, W, P,
                              [W['g_kv'], W['g_mix'][1]])

    pos = pos0 + jnp.tile(jnp.arange(T, dtype=jnp.int32), B)
    cosf, sinf = _rope_tables(pos)
    n_kv = W['w_kv'].shape[1]
    rope_spec = lambda tm: pl.BlockSpec((tm, HEAD_DIM), lambda i, j, k: (i, 0))
    kv_spec = lambda tm: pl.BlockSpec((1, tm, KV_W), lambda i, j, k: (j, i, 0))
    kv4_spec = lambda tm: pl.BlockSpec((1, tm, N_KV_HEADS, HEAD_DIM), lambda i, j, k: (j, i, 0, 0))
    kv_specs = lambda tm, tn: ([pl.BlockSpec((1, 1, KV_W), lambda i, j, k: (j, 0, 0)), rope_spec(tm), rope_spec(tm)],
                               [kv4_spec(tm), kv_spec(tm)])
    kv, kv_b = _dense(hkv, 'w_kv2d', None, W, P, N=n_kv * KV_W, epilogue=_kv_epilogue, specs=kv_specs,
                      extras=[P['kv_gain'].reshape(n_kv, 1, KV_W), cosf, sinf],
                      out_dtypes=[jax.ShapeDtypeStruct((n_kv, M, N_KV_HEADS, HEAD_DIM), F32),
                                  jax.ShapeDtypeStruct((n_kv, M, KV_W), BF16)], row_scale_in=rs, name="kv_proj")
    q_specs = lambda tm, tn: ([pl.BlockSpec((1, HEAD_DIM), lambda i, j, k: (0, 0)), rope_spec(tm), rope_spec(tm)],
                              [_mn_spec(tm, tn)] * 2)
    q, qr = _dense(a1, 'w_qg', 0, W, P, N=D, epilogue=_q_epilogue, specs=q_specs,
                   extras=[W['g_q'][0].reshape(1, HEAD_DIM), cosf, sinf], out_dtypes=[BF16, BF16], row_scale_in=rs,
                   name="q_proj")
    ng = N_KV_HEADS * LANE
    grid, tm, tn, tk = _tiles(M, ng, D, tmx, ng, D)
    (gates,) = mm(a1, P['w_gate'][0], grid=grid, tm=tm, tn=tn, tk=tk,
                  epilogue=_row_scaled(lambda acc, rows: (_sigmoid(acc),)), extras=[rs],
                  extra_specs=[pl.BlockSpec((tm, LANE), lambda i, j, k: (i, 0))],
                  out_shapes=[jax.ShapeDtypeStruct((M, ng), F32)], out_specs=[_mn_spec(tm, tn)], name="gate_proj")

    o, win_state = attend(kv, kv_b, q, qr, gates)

    h, y, rs = _dense(o, 'w_o', 0, W, P, N=D, epilogue=lambda acc, rows, r: (r[rows, :] + acc,), extras=[h],
                      specs=lambda tm, tn: ([_mn_spec(tm, tn)], [_mn_spec(tm, tn)]), out_dtypes=[F32],
                      norm_gains=[W['g_ffn'][1]], name="attn_out")
    (h,) = _ffn_ple(h, (y, rs), p[1].reshape(M, -1), 1, W, P, [])
    rows = tuple(kv[n].reshape(B, T, N_KV_HEADS, HEAD_DIM) for n in range(4))
    return h.reshape(B, T, D), pool_new, rows, win_state


def _compress(k_pages, v_pages, table, cp, W, P):
    abk, abv = cmp_partials(k_pages, v_pages, table, P['w_cmp_kcat'], P['w_cmp_vcat'], cp)
    gain = W['g_k_cmp'].reshape(1, HEAD_DIM)
    ck = cmp_finish(abk, P['pe_k'], P['w_cmp_k1'], P['w_cmp_k2'], gain, norm=True)
    cv = cmp_finish(abv, P['pe_v'], P['w_cmp_v1'], P['w_cmp_v2'], gain, norm=False)
    return ck, cv


def kernel(x_prompt, x_sample, state_pool, cache_k_cmp, cache_v_cmp, cache_k_sel, cache_v_sel, state_k_win, state_v_win, page_table, p_prompt, p_sample, g_mix, w_pool, pool_scale, g_kv, w_kv, g_k_cmp, g_k_sel, g_k_win, w_cmp_k1, w_cmp_k2, pe_cmp_k, w_cmp_v1, w_cmp_v2, pe_cmp_v, w_qg, g_q, w_o, g_ffn, w_up, w_down, g_ple, w_ple, w_ple_gate):
    W = dict(g_mix=g_mix, w_pool=w_pool, pool_scale=pool_scale, g_kv=g_kv, w_kv=w_kv, g_k_cmp=g_k_cmp,
             g_k_sel=g_k_sel, g_k_win=g_k_win, w_cmp_k1=w_cmp_k1, w_cmp_k2=w_cmp_k2, pe_cmp_k=pe_cmp_k,
             w_cmp_v1=w_cmp_v1, w_cmp_v2=w_cmp_v2, pe_cmp_v=pe_cmp_v, w_qg=w_qg, g_q=g_q, w_o=w_o,
             g_ffn=g_ffn, w_up=w_up, w_down=w_down, g_ple=g_ple, w_ple=w_ple, w_ple_gate=w_ple_gate)
    P = _prep_weights(W)
    W['w_kv2d'] = w_kv.reshape(w_kv.shape[0], -1)
    Bp, Tp, D = x_prompt.shape
    Bs, Ts, _ = x_sample.shape
    assert Ts == 1, "the decode path handles one new token per sequence"
    page = cache_k_cmp.shape[1]
    past_len = page_table.shape[1] * page
    R = D // HEAD_DIM // N_KV_HEADS
    assert Tp % page == 0 and past_len % SEL_BLOCK == 0

    def attend_prompt(kv, kv_b, q, qr, gates):
        ppb = Tp // page
        table = jnp.arange(Bp * ppb, dtype=jnp.int32).reshape(Bp, ppb)
        pages = lambda a: a.reshape(-1, page, N_KV_HEADS, HEAD_DIM)
        ck, cv = _compress(pages(kv[0]), pages(kv[1]), table, ppb, W, P)
        nc = Tp // CMP_STRIDE - CMP_BLOCK // CMP_STRIDE + 1
        seq = lambda a: a.reshape(Bp, Tp, -1)
        o = attn_prompt(seq(q), seq(qr), ck, cv, kv_b.reshape(-1, Bp, Tp, KV_W), seq(gates), nc=nc)
        nw = min(WINDOW, Tp)
        win = tuple(kv[n].reshape(Bp, Tp, N_KV_HEADS, HEAD_DIM)[:, -nw:] for n in (4, 5))
        return o.reshape(Bp * Tp, D), win

    def attend_sample(kv, kv_b, q, qr, gates):
        del kv_b
        ck, cv = _compress(cache_k_cmp, cache_v_cmp, page_table, min(32, page_table.shape[1]), W, P)
        nc = (past_len - (CMP_BLOCK - 1)) // CMP_STRIDE + 1
        ns = past_len // SEL_BLOCK + 1
        wb = state_k_win.shape[1]
        new_row = lambda a: a.reshape(Bs, 1, N_KV_HEADS, HEAD_DIM)
        kw = jnp.concatenate([state_k_win, new_row(kv[4])], axis=1)[:, -wb:]
        vw = jnp.concatenate([state_v_win, new_row(kv[5])], axis=1)[:, -wb:]
        heads = lambda a: a.reshape(Bs, N_KV_HEADS, R, HEAD_DIM)
        ocmp, owin, ids = attn_dec_dense(heads(q), heads(qr), ck, cv, kw, vw, nc=nc, ns=ns, pos=past_len)
        gt = gates.reshape(Bs, N_KV_HEADS, LANE)[:, :, :R * N_BRANCH].reshape(Bs, N_KV_HEADS, R, N_BRANCH)
        gt = jnp.pad(gt, ((0, 0), (0, 0), (0, 0), (0, LANE - N_BRANCH)))
        o = attn_dec_sel(page_table, ids, heads(qr), cache_k_sel, cache_v_sel,
                         kv[2].reshape(Bs, 1, KV_W), kv[3].reshape(Bs, 1, KV_W), ocmp, owin, gt, pos=past_len)
        return o.reshape(Bs, D).astype(BF16), (kw, vw)

    pool_zero = jnp.zeros((state_pool.shape[0], Bp, POOL_STATE, D), x_prompt.dtype)
    y_p, pool_p, rows_p, win_p = _trunk(x_prompt, p_prompt, pool_zero, 0, W, P, attend_prompt)
    y_s, pool_s, rows_s, win_s = _trunk(x_sample, p_sample, state_pool, past_len, W, P, attend_sample)
    return (y_p, y_s, pool_p, pool_s, rows_p[0], rows_p[1], rows_p[2], rows_p[3], win_p[0], win_p[1],
            rows_s[0], rows_s[1], rows_s[2], rows_s[3], win_s[0], win_s[1])
```

```python
import functools

import jax
import jax.numpy as jnp
import numpy as np
from jax import lax
from jax.experimental import pallas as pl
from jax.experimental.pallas import tpu as pltpu

F32 = jnp.float32
BF16 = jnp.bfloat16

POOL_WINDOWS = (2, 4, 8, 16)
POOL_STATE = max(POOL_WINDOWS) - 1
POOL_PAD = POOL_STATE + 1
HEAD_DIM = 128
N_KV_HEADS = 4
N_BRANCH = 3
CMP_BLOCK = 32
CMP_STRIDE = 16
CMP_HIDDEN = 2 * HEAD_DIM
SEL_BLOCK = 64
N_SEL = 16
WINDOW = 512
ROPE_THETA = 10000.0
EPS = 1e-6
SCALE = HEAD_DIM ** -0.5
NEG = -1e30
FORCE = 1e9
PAD_SCORE = -3e38
KV_W = N_KV_HEADS * HEAD_DIM
LANE = 128
VMEM_LIMIT = 56 * 1024 * 1024


def _params(sem):
    return pltpu.CompilerParams(dimension_semantics=sem, vmem_limit_bytes=VMEM_LIMIT)


def _sigmoid(x):
    return 1.0 / (1.0 + jnp.exp(-x))


def _dot(a, b):
    return jnp.dot(a, b, preferred_element_type=F32)


def _dot_nt(a, b):
    return lax.dot_general(a, b, (((1,), (1,)), ((), ())), preferred_element_type=F32)


def _dot_tn(a, b):
    return lax.dot_general(a, b, (((0,), (0,)), ((), ())), preferred_element_type=F32)


def _head_norm(x, g):
    return x * lax.rsqrt(jnp.mean(x * x, axis=-1, keepdims=True) + EPS) * g


def _rope(x, cosf, sinf):
    return x * cosf + pltpu.roll(x, HEAD_DIM // 2, 1) * sinf


def _fold_lanes(x):
    parts = [x[:, c * LANE:(c + 1) * LANE] for c in range(x.shape[1] // LANE)]
    return functools.reduce(lambda u, v: u + v, parts)


def _norm_producer(epilogue, n_base):
    def wrapped(acc, rows, *extras):
        (h,) = epilogue(acc, rows, *extras[:n_base])
        return (h, *[(h * g[...]).astype(BF16) for g in extras[n_base:]], _fold_lanes(h * h))
    return wrapped


def _row_scaled(epilogue):
    def wrapped(acc, rows, rs_ref, *extras):
        rs = rs_ref[rows, :]
        acc = jnp.concatenate([acc[:, c * LANE:(c + 1) * LANE] * rs for c in range(acc.shape[1] // LANE)], axis=1)
        return epilogue(acc, rows, *extras)
    return wrapped


def _row_scale_kernel(ssq_ref, o_ref, *, d):
    tot = jnp.sum(functools.reduce(lambda u, v: u + v, [ssq_ref[j] for j in range(ssq_ref.shape[0])]),
                  axis=1, keepdims=True)
    o_ref[...] = jnp.broadcast_to(lax.rsqrt(tot / d + EPS), o_ref.shape)


def row_scale(ssq, d):
    gn, M, _ = ssq.shape
    tm = min(M, 1024)
    return pl.pallas_call(
        functools.partial(_row_scale_kernel, d=d),
        grid=(M // tm,),
        in_specs=[pl.BlockSpec((gn, tm, LANE), lambda i: (0, i, 0))],
        out_specs=pl.BlockSpec((tm, LANE), lambda i: (i, 0)),
        out_shape=jax.ShapeDtypeStruct((M, LANE), F32),
        compiler_params=_params(("parallel",)),
        name="row_scale",
    )(ssq)


def _norm_io(gains, M, N, gn, tm, tn):
    extras = [g.reshape(1, N).astype(F32) for g in gains]
    extra_specs = [pl.BlockSpec((1, tn), lambda i, j, k: (0, j)) for _ in gains]
    shapes = [jax.ShapeDtypeStruct((M, N), BF16) for _ in gains] + [jax.ShapeDtypeStruct((gn, M, LANE), F32)]
    specs = [_mn_spec(tm, tn) for _ in gains] + [pl.BlockSpec((1, tm, LANE), lambda i, j, k: (j, i, 0))]
    return extras, extra_specs, shapes, specs


EPILOGUE_ROWS = 256


def _store_rows(o_ref, rows, r):
    if len(o_ref.shape) == 4:
        for hh in range(o_ref.shape[2]):
            o_ref[0, rows, hh, :] = r[:, hh * HEAD_DIM:(hh + 1) * HEAD_DIM].astype(o_ref.dtype)
    elif len(o_ref.shape) == 3:
        o_ref[0, rows, :] = r.astype(o_ref.dtype)
    else:
        o_ref[rows, :] = r.astype(o_ref.dtype)


def _finish_rows(acc_of, tm, outs, extras, epilogue):
    ch = min(tm, EPILOGUE_ROWS)
    for c in range(tm // ch):
        rows = slice(c * ch, (c + 1) * ch)
        for o_ref, r in zip(outs, epilogue(acc_of(rows), rows, *extras)):
            _store_rows(o_ref, rows, r)


def _k_steps(x_ref, w, acc_ref, nk, outs, extras, epilogue):
    k = pl.program_id(2)

    @pl.when(k == 0)
    def _():
        acc_ref[...] = _dot(x_ref[...], w())

    @pl.when((k > 0) & (k < nk - 1))
    def _():
        acc_ref[...] += _dot(x_ref[...], w())

    @pl.when(k == nk - 1)
    def _():
        _finish_rows(lambda rows: acc_ref[rows, :] + _dot(x_ref[rows, :], w()), x_ref.shape[0], outs, extras,
                     epilogue)


def _side_io(side, n_steps, step):
    s_arr, s_layer = side
    _, rows, cols = s_arr.shape
    rs = rows // n_steps
    assert rs * n_steps == rows and rs % 16 == 0
    in_spec = pl.BlockSpec((None, rs, cols), lambda *g: (s_layer, step(*g), 0))
    out_spec = pl.BlockSpec((rs, cols), lambda *g: (step(*g), 0))
    return s_arr, in_spec, jax.ShapeDtypeStruct((rows, cols), BF16), out_spec


def _mm_kernel(*refs, nk, n_extra, n_out, epilogue, side):
    x_ref, w_ref = refs[0], refs[1]
    extras = refs[2:2 + n_extra]
    n_in = 2 + n_extra + side
    outs = refs[n_in:n_in + n_out]
    tm = x_ref.shape[0]
    if side:
        refs[n_in + n_out][...] = refs[n_in - 1][...].astype(BF16)

    if nk == 1:
        _finish_rows(lambda rows: _dot(x_ref[rows, :], w_ref[...]), tm, outs, extras, epilogue)
    else:
        _k_steps(x_ref, lambda: w_ref[...], refs[-1], nk, outs, extras, epilogue)


def mm(x, w, *, grid, tm, tn, tk, epilogue, extras=(), extra_specs=(), out_shapes, out_specs,
       x_map=None, w_map=None, side=None, name):
    gm, gn, nk = grid
    x_map = x_map or (lambda i, j, k: (i, k))
    w_map = w_map or (lambda i, j, k: (k, j))
    ins, in_specs = [x, w, *extras], [pl.BlockSpec((tm, tk), x_map), pl.BlockSpec((tk, tn), w_map)]
    in_specs += list(extra_specs)
    n_out, out_shapes, out_specs = len(out_shapes), list(out_shapes), list(out_specs)
    if side is not None:
        s_arr, s_in, s_shape, s_out = _side_io(side, gm * gn * nk, lambda i, j, k: (i * gn + j) * nk + k)
        ins.append(s_arr), in_specs.append(s_in), out_shapes.append(s_shape), out_specs.append(s_out)
    kern = functools.partial(_mm_kernel, nk=nk, n_extra=len(extras), n_out=n_out, epilogue=epilogue,
                             side=side is not None)
    return pl.pallas_call(
        kern,
        grid=grid,
        in_specs=in_specs,
        out_specs=out_specs,
        out_shape=out_shapes,
        scratch_shapes=[pltpu.VMEM((tm, tn), F32)] if nk > 1 else [],
        compiler_params=_params(("parallel", "parallel", "arbitrary")),
        name=name,
    )(*ins)


def _mm_ws_kernel(*refs, nk, n_extra, n_out, epilogue, side):
    x_ref, w_ref = refs[0], refs[1]
    extras = refs[2:2 + n_extra]
    n_in = 2 + n_extra + side
    outs = refs[n_in:n_in + n_out]
    rest = refs[n_in + n_out:]
    wb_ref = rest[side]
    i, k = pl.program_id(1), pl.program_id(2)

    @pl.when(i == 0)
    def _():
        wb_ref[k] = w_ref[...].astype(BF16)

    if side:
        rest[0][...] = refs[n_in - 1][...].astype(BF16)

    tm = x_ref.shape[0]
    if nk == 1:
        _finish_rows(lambda rows: _dot(x_ref[rows, :], wb_ref[0]), tm, outs, extras, epilogue)
    else:
        _k_steps(x_ref, lambda: wb_ref[k], rest[-1], nk, outs, extras, epilogue)


def mm_ws(x, w, *, layer=None, side=None, grid, tm, tn, tk, epilogue, extras=(), extra_specs=(),
          out_shapes, out_specs, name):
    gm, gn, nk = grid
    swap = lambda f: (lambda j, i, k: f(i, j, k))
    respec = lambda s: pl.BlockSpec(s.block_shape, swap(s.index_map))
    k_once = lambda i, k: jnp.where(i == 0, k, nk - 1)
    if layer is None:
        w_spec = pl.BlockSpec((tk, tn), lambda j, i, k: (k_once(i, k), j))
    else:
        w_spec = pl.BlockSpec((None, tk, tn), lambda j, i, k: (layer, k_once(i, k), j))
    n_out = len(out_shapes)
    ins, in_specs = [x, w, *extras], [pl.BlockSpec((tm, tk), lambda j, i, k: (i, k)), w_spec]
    in_specs += [respec(s) for s in extra_specs]
    out_shapes, out_specs = list(out_shapes), [respec(s) for s in out_specs]
    if side is not None:
        s_arr, s_in, s_shape, s_out = _side_io(side, gm * gn * nk, lambda j, i, k: (j * gm + i) * nk + k)
        ins.append(s_arr), in_specs.append(s_in), out_shapes.append(s_shape), out_specs.append(s_out)
    kern = functools.partial(_mm_ws_kernel, nk=nk, n_extra=len(extras), n_out=n_out, epilogue=epilogue,
                             side=side is not None)
    return pl.pallas_call(
        kern,
        grid=(gn, gm, nk),
        in_specs=in_specs,
        out_specs=out_specs,
        out_shape=out_shapes,
        scratch_shapes=[pltpu.VMEM((nk, tk, tn), BF16)] + ([pltpu.VMEM((tm, tn), F32)] if nk > 1 else []),
        compiler_params=_params(("parallel", "arbitrary", "arbitrary")),
        name=name,
    )(*ins)


def _tiles(M, N, K, tm, tn, tk):
    tm, tn, tk = min(tm, M), min(tn, N), min(tk, K)
    return (M // tm, N // tn, K // tk), tm, tn, tk


def _mn_spec(tm, tn):
    return pl.BlockSpec((tm, tn), lambda i, j, k: (i, j))


def _pool_diff_kernel(*refs, tt, pos0, halo, side):
    x_ref, pre_ref, g_ref = refs[0], refs[1 + halo], refs[2 + halo]
    n_in = 3 + halo + side
    d_ref, st_ref, seq_ref = refs[n_in], refs[n_in + 1], refs[-1]
    if side:
        refs[n_in + 2][...] = refs[n_in - 1][...].astype(BF16)
    t = pl.program_id(1)
    norm = lambda x: x * lax.rsqrt(jnp.mean(x * x, axis=-1, keepdims=True) + EPS) * g_ref[...]
    a = norm(x_ref[0])
    seq_ref[POOL_PAD:POOL_PAD + tt, :] = a

    @pl.when(t == 0)
    def _():
        seq_ref[0:POOL_PAD, :] = pre_ref[0]

    if halo:
        @pl.when(t > 0)
        def _():
            seq_ref[0:POOL_PAD, :] = norm(refs[1][0])

    pos = pos0 + t * tt + lax.broadcasted_iota(jnp.int32, (tt, 1), 0)
    pg = a.shape[1] // len(POOL_WINDOWS)
    for g, w in enumerate(POOL_WINDOWS):
        cols = slice(g * pg, (g + 1) * pg)
        s = a[:, cols]
        for j in range(1, w):
            s = s + seq_ref[POOL_PAD - j:POOL_PAD - j + tt, cols]
        cnt = jnp.minimum(pos + 1, w).astype(F32)
        d_ref[0, :, cols] = (s / cnt - a[:, cols]).astype(d_ref.dtype)
    st_ref[0] = seq_ref[tt:tt + POOL_PAD, :]


def pool_diff(x, gain, prefix, pos0, side=None):
    B, T, D = x.shape
    tt = min(T, 256)
    halo = T > tt
    pre = jnp.concatenate([jnp.zeros((B, 1, D), F32), prefix], axis=1)
    hpt = tt // POOL_PAD
    in_specs = [pl.BlockSpec((1, tt, D), lambda b, t: (b, t, 0))]
    if halo:
        in_specs.append(pl.BlockSpec((1, POOL_PAD, D), lambda b, t: (b, jnp.maximum(t * hpt - 1, 0), 0)))
    in_specs += [pl.BlockSpec((1, POOL_PAD, D), lambda b, t: (b, 0, 0)), pl.BlockSpec((1, D), lambda b, t: (0, 0))]
    ins = [x, x] if halo else [x]
    ins += [pre, gain.reshape(1, D).astype(F32)]
    out_specs = [pl.BlockSpec((1, tt, D), lambda b, t: (b, t, 0)),
                 pl.BlockSpec((1, POOL_PAD, D), lambda b, t: (b, 0, 0))]
    out_shapes = [jax.ShapeDtypeStruct((B, T, D), BF16), jax.ShapeDtypeStruct((B, POOL_PAD, D), F32)]
    if side is not None:
        nt = T // tt
        s_arr, s_in, s_shape, s_out = _side_io(side, B * nt, lambda b, t: b * nt + t)
        ins.append(s_arr), in_specs.append(s_in), out_shapes.append(s_shape), out_specs.append(s_out)
    d, st, *cast = pl.pallas_call(
        functools.partial(_pool_diff_kernel, tt=tt, pos0=pos0, halo=halo, side=side is not None),
        grid=(B, T // tt),
        in_specs=in_specs,
        out_specs=out_specs,
        out_shape=out_shapes,
        scratch_shapes=[pltpu.VMEM((POOL_PAD + tt, D), F32)],
        compiler_params=_params(("parallel", "arbitrary")),
        name="pool_diff",
    )(*ins)
    return d, st[:, 1:], (cast[0] if cast else None)


PAGES_PER_STEP = 16
CMP_FINISH_ROWS = 512


def _cmp_partial_kernel(pt_ref, *refs, steps, pps):
    del pt_ref
    k_pages, v_pages = refs[:pps], refs[pps:2 * pps]
    wk_ref, wv_ref, abk_ref, abv_ref, xk_ref, xv_ref = refs[2 * pps:]
    p = pl.program_id(2)
    G = N_KV_HEADS
    half = CMP_STRIDE // 2
    rows_per_page = (k_pages[0].shape[1] // half) * G
    for pages, x_ref in ((k_pages, xk_ref), (v_pages, xv_ref)):
        for q, page_ref in enumerate(pages):
            for sb in range(page_ref.shape[1] // half):
                row = q * rows_per_page + sb * G
                for r in range(CMP_STRIDE):
                    x_ref[p, row:row + G, r * HEAD_DIM:(r + 1) * HEAD_DIM] = (
                        page_ref[0, sb * half + r // 2, (r % 2) * G:(r % 2 + 1) * G, :])

    @pl.when(p == steps - 1)
    def _():
        for x_ref, w_ref, ab_ref in ((xk_ref, wk_ref, abk_ref), (xv_ref, wv_ref, abv_ref)):
            x = x_ref[...].reshape(ab_ref.shape[1], x_ref.shape[2])
            ab_ref[0] = _dot(x.astype(BF16), w_ref[...])


def cmp_partials(k_pages, v_pages, table, wk_cat, wv_cat, cp):
    B, ppb = table.shape
    page = k_pages.shape[1]
    pps = PAGES_PER_STEP
    rows_per_page = page // CMP_STRIDE * N_KV_HEADS
    nch, steps = ppb // cp, cp // pps
    m = cp * rows_per_page
    kdim = CMP_STRIDE * HEAD_DIM
    pair_rows = lambda a: a.reshape(a.shape[0], page // 2, 2 * N_KV_HEADS, HEAD_DIM)

    def page_spec(q):
        return pl.BlockSpec((1, page // 2, 2 * N_KV_HEADS, HEAD_DIM),
                            lambda b, c, p, pt: (pt[b, c * cp + p * pps + q], 0, 0, 0))

    page_specs = [page_spec(q) for q in range(pps)]
    w_spec = pl.BlockSpec((kdim, 2 * CMP_HIDDEN), lambda b, c, p, pt: (0, 0))
    out_spec = pl.BlockSpec((1, m, 2 * CMP_HIDDEN), lambda b, c, p, pt: (b, c, 0))
    out_shape = jax.ShapeDtypeStruct((B, ppb * rows_per_page, 2 * CMP_HIDDEN), F32)
    x_scratch = pltpu.VMEM((steps, pps * rows_per_page, kdim), F32)
    return pl.pallas_call(
        functools.partial(_cmp_partial_kernel, steps=steps, pps=pps),
        grid_spec=pltpu.PrefetchScalarGridSpec(
            num_scalar_prefetch=1,
            grid=(B, nch, steps),
            in_specs=page_specs + page_specs + [w_spec, w_spec],
            out_specs=[out_spec, out_spec],
            scratch_shapes=[x_scratch, x_scratch],
        ),
        out_shape=[out_shape, out_shape],
        compiler_params=_params(("parallel", "parallel", "arbitrary")),
        name="cmp_partials",
    )(table, *([pair_rows(k_pages)] * pps), *([pair_rows(v_pages)] * pps), wk_cat, wv_cat)


def _cmp_finish_kernel(ab_ref, pe_ref, w1_ref, w2_ref, g_ref, o_ref, out_ref, *, norm):
    n4 = ab_ref.shape[1]
    G = o_ref.shape[1]
    bias = _dot(jnp.broadcast_to(pe_ref[...], (8, pe_ref.shape[1])), w1_ref[...])[0:1, :]
    ch = min(n4, CMP_FINISH_ROWS)
    for c in range(n4 // ch):
        lo, hi = c * ch, (c + 1) * ch
        first = ab_ref[0, lo:hi, :CMP_HIDDEN]
        if hi + G <= n4:
            second = ab_ref[0, lo + G:hi + G, CMP_HIDDEN:]
        else:
            second = jnp.concatenate([ab_ref[0, lo + G:hi, CMP_HIDDEN:], ab_ref[0, hi - G:hi, CMP_HIDDEN:]], axis=0)
        pre = first + second + bias
        out = _dot((pre * _sigmoid(pre)).astype(BF16), w2_ref[...])
        if norm:
            out = _head_norm(out, g_ref[...])
        out_ref[lo:hi, :] = out
    for g in range(G):
        o_ref[0, g] = out_ref[pl.ds(g, n4 // G, stride=G), :].astype(o_ref.dtype)


def cmp_finish(ab, pe, w1, w2, gain, *, norm):
    B, n4, _ = ab.shape
    G = N_KV_HEADS
    full = lambda a: pl.BlockSpec(a.shape, lambda b: (0,) * a.ndim)
    args = (pe, w1, w2, gain)
    return pl.pallas_call(
        functools.partial(_cmp_finish_kernel, norm=norm),
        grid=(B,),
        in_specs=[pl.BlockSpec((1, n4, 2 * CMP_HIDDEN), lambda b: (b, 0, 0))] + [full(a) for a in args],
        out_specs=pl.BlockSpec((1, G, n4 // G, HEAD_DIM), lambda b: (b, 0, 0, 0)),
        out_shape=jax.ShapeDtypeStruct((B, G, n4 // G, HEAD_DIM), BF16),
        scratch_shapes=[pltpu.VMEM((n4, HEAD_DIM), F32)],
        compiler_params=_params(("parallel",)),
        name="cmp_finish",
    )(ab, *args)


def _select_blocks(score, blk, ns):
    rank = jnp.zeros(score.shape, jnp.int32)
    for j in range(ns):
        sj = score[j:j + 1, :]
        beats = (sj > score) | ((sj == score) & (j < blk))
        rank = rank + beats.astype(jnp.int32)
    return (rank < min(N_SEL, ns)) & (score > 0.5 * NEG)


def _attn_prompt_kernel(q_ref, qr_ref, ck_ref, cv_ref, ks_ref, vs_ref, kw_ref, vw_ref, gate_ref, mapT_ref,
                        expand_ref, o_ref, part_ref, sbias_ref, wbias_ref, m_ref, acc_ref, *, tq, kc, nc, ns, R):
    qi = pl.program_id(2)
    q0 = qi * tq
    pos = q0 + lax.broadcasted_iota(jnp.int32, (tq, 1), 0)
    ncp = ck_ref.shape[2]

    ck = ck_ref[0, 0]
    cv = cv_ref[0, 0]
    cidx = lax.broadcasted_iota(jnp.int32, (1, ncp), 1)
    ok_c = (cidx * CMP_STRIDE + CMP_BLOCK - 1 <= pos) & (cidx < nc)
    imp = jnp.zeros((tq, ncp), F32)
    for r in range(R):
        qh = q_ref[0, :, r * HEAD_DIM:(r + 1) * HEAD_DIM]
        s = jnp.where(ok_c, _dot_nt(qh, ck) * SCALE, NEG)
        e = jnp.exp(s - jnp.max(s, axis=1, keepdims=True))
        p = jnp.where(ok_c, e * (1.0 / jnp.sum(e, axis=1, keepdims=True)), 0.0)
        imp = imp + p
        part_ref[r] = gate_ref[0, :, r * N_BRANCH:r * N_BRANCH + 1] * _dot(p.astype(BF16), cv)

    nsp = mapT_ref.shape[0]
    p_slc = lax.dot_general(mapT_ref[...], imp, (((1,), (1,)), ((), ())), precision=lax.Precision.HIGHEST,
                            preferred_element_type=F32)
    blk = lax.broadcasted_iota(jnp.int32, (nsp, tq), 0)
    pos_l = q0 + lax.broadcasted_iota(jnp.int32, (nsp, tq), 1)
    cur = pos_l // SEL_BLOCK
    vis = blk * SEL_BLOCK <= pos_l
    forced = vis & ((blk == 0) | (blk == cur) | (blk == cur - 1))
    score = jnp.where(forced, FORCE, jnp.where(vis, p_slc, NEG))
    score = jnp.where(blk < ns, score, PAD_SCORE)
    sel = _select_blocks(score, blk, ns).astype(BF16)

    c_hi = (q0 + tq) // kc
    col = lax.broadcasted_iota(jnp.int32, (1, kc), 1)
    sel_keys = _dot_tn(sel, expand_ref[...])
    for c in range(sbias_ref.shape[0]):
        @pl.when(c < c_hi)
        def _(c=c):
            ok = (sel_keys[:, c * kc:(c + 1) * kc] > 0.5) & (c * kc + col <= pos)
            sbias_ref[c] = jnp.where(ok, 0.0, NEG)
    nwc = wbias_ref.shape[0]
    c_w0 = c_hi - nwc
    for d in range(nwc):
        kpos = (c_w0 + d) * kc + col
        wbias_ref[d] = jnp.where((kpos <= pos) & (kpos > pos - WINDOW), 0.0, NEG)

    def fold(t):
        return [t[:, i * LANE:(i + 1) * LANE] for i in range(kc // LANE)]

    ones_blk = jnp.ones((kc, LANE), BF16)

    def branch(k_ref, v_ref, c_lo, bias_of, gate_col):
        def logits(r, c, bias):
            k = k_ref[0, pl.ds(pl.multiple_of(c * kc, kc), kc), :]
            return _dot_nt(qr_ref[0, :, r * HEAD_DIM:(r + 1) * HEAD_DIM], k) * SCALE + bias

        m_ref[...] = jnp.full(m_ref.shape, NEG, F32)
        acc_ref[...] = jnp.zeros(acc_ref.shape, F32)

        def over_chunks(body):
            n = c_hi - c_lo

            def pair(i, carry):
                body(c_lo + 2 * i, carry)
                return body(c_lo + 2 * i + 1, carry)

            lax.fori_loop(0, n // 2, pair, 0)

            @pl.when(n % 2 == 1)
            def _():
                body(c_hi - 1, 0)

        def max_body(c, carry):
            bias = bias_of(c)
            for r in range(R):
                mx = m_ref[r]
                for part in fold(logits(r, c, bias)):
                    mx = jnp.maximum(mx, part)
                m_ref[r] = mx
            return carry

        over_chunks(max_body)
        for r in range(R):
            m_ref[r] = jnp.broadcast_to(jnp.max(m_ref[r], axis=1, keepdims=True), (tq, LANE))

        def sum_body(c, carry):
            bias = bias_of(c)
            v = jnp.concatenate([v_ref[0, pl.ds(pl.multiple_of(c * kc, kc), kc), :], ones_blk], axis=1)
            for r in range(R):
                t = logits(r, c, bias)
                m = m_ref[r]
                ps = [jnp.exp(part - m) for part in fold(t)]
                acc_ref[r] += _dot(jnp.concatenate(ps, axis=1).astype(BF16), v)
            return carry

        over_chunks(sum_body)
        for r in range(R):
            gate = gate_ref[0, :, r * N_BRANCH + gate_col:r * N_BRANCH + gate_col + 1]
            part_ref[r] += gate * (acc_ref[r, :, :HEAD_DIM] * (1.0 / acc_ref[r, :, HEAD_DIM:]))

    branch(ks_ref, vs_ref, 0, lambda c: sbias_ref[c], 1)
    branch(kw_ref, vw_ref, jnp.maximum(c_w0, 0), lambda c: wbias_ref[c - c_w0], 2)
    for r in range(R):
        o_ref[0, :, r * HEAD_DIM:(r + 1) * HEAD_DIM] = part_ref[r].astype(o_ref.dtype)


def _overlap_map(ncp, nsp, ns):
    ratio = CMP_BLOCK // CMP_STRIDE
    per_sel = SEL_BLOCK // CMP_STRIDE
    m = np.zeros((ncp, nsp), np.float32)
    for b in range(ns):
        for mm_ in range(per_sel):
            for n in range(ratio):
                j = per_sel * b + mm_ - n
                if 0 <= j < ncp:
                    m[j, b] += 1.0
    return m


def attn_prompt(q, qr, ck, cv, kvb, gates, *, nc):
    B, T, HD = q.shape
    G = N_KV_HEADS
    R = HD // HEAD_DIM // G
    tq = min(T, 256)
    kc = tq
    ns = T // SEL_BLOCK
    nsp = -(-ns // 8) * 8
    ncp = ck.shape[2]
    mapT = jnp.asarray(_overlap_map(ncp, nsp, ns).T)
    expand = jnp.asarray((np.arange(T)[None, :] // SEL_BLOCK == np.arange(nsp)[:, None]).astype(np.float32), BF16)
    q_spec = pl.BlockSpec((1, tq, R * HEAD_DIM), lambda b, g, i: (b, i, g))
    c_spec = pl.BlockSpec((1, 1, ncp, HEAD_DIM), lambda b, g, i: (b, g, 0, 0))
    kv_spec = lambda n: pl.BlockSpec((None, 1, T, HEAD_DIM), lambda b, g, i: (n, b, 0, g))
    return pl.pallas_call(
        functools.partial(_attn_prompt_kernel, tq=tq, kc=kc, nc=nc, ns=ns, R=R),
        grid=(B, G, T // tq),
        in_specs=[q_spec, q_spec, c_spec, c_spec, kv_spec(2), kv_spec(3), kv_spec(4), kv_spec(5),
                  pl.BlockSpec((1, tq, LANE), lambda b, g, i: (b, i, g)),
                  pl.BlockSpec((nsp, ncp), lambda b, g, i: (0, 0)),
                  pl.BlockSpec((nsp, T), lambda b, g, i: (0, 0))],
        out_specs=q_spec,
        out_shape=jax.ShapeDtypeStruct((B, T, HD), BF16),
        scratch_shapes=[pltpu.VMEM((R, tq, HEAD_DIM), F32), pltpu.VMEM((T // kc, tq, kc), F32),
                        pltpu.VMEM((min(WINDOW, T) // kc + tq // kc, tq, kc), F32)]
                       + [pltpu.VMEM((R, tq, HEAD_DIM), F32), pltpu.VMEM((R, tq, HEAD_DIM + LANE), F32)],
        compiler_params=_params(("parallel", "parallel", "arbitrary")),
        name="attn_prompt",
    )(q, qr, ck, cv, kvb, kvb, kvb, kvb, gates, mapT, expand)


def _attn_dec_dense_kernel(q_ref, qr_ref, ck_ref, cv_ref, kw_ref, vw_ref, map_ref, ocmp_ref, owin_ref, ids_ref,
                           *, nc, ns, pos):
    G, R = q_ref.shape[1], q_ref.shape[2]
    ncp = ck_ref.shape[2]
    nsl = map_ref.shape[1]
    cidx = lax.broadcasted_iota(jnp.int32, (1, ncp), 1)
    ok_c = (cidx * CMP_STRIDE + CMP_BLOCK - 1 <= pos) & (cidx < nc)
    blk_l = lax.broadcasted_iota(jnp.int32, (1, nsl), 1)
    cur = pos // SEL_BLOCK
    vis = blk_l * SEL_BLOCK <= pos
    forced = vis & ((blk_l == 0) | (blk_l == cur) | (blk_l == cur - 1))
    ii = lax.broadcasted_iota(jnp.int32, (nsl, nsl), 0)
    jj = lax.broadcasted_iota(jnp.int32, (nsl, nsl), 1)
    slot = lax.broadcasted_iota(jnp.int32, (nsl, LANE), 1).astype(F32)
    blk_s = lax.broadcasted_iota(jnp.int32, (nsl, LANE), 0).astype(F32)
    for g in range(G):
        s = jnp.where(ok_c, _dot_nt(q_ref[0, g], ck_ref[0, g]) * SCALE, NEG)
        e = jnp.exp(s - jnp.max(s, axis=1, keepdims=True))
        p = jnp.where(ok_c, e * (1.0 / jnp.sum(e, axis=1, keepdims=True)), 0.0)
        ocmp_ref[0, g] = _dot(p.astype(BF16), cv_ref[0, g])
        imp = jnp.broadcast_to(jnp.sum(p, axis=0, keepdims=True), (R, ncp))
        p_slc = jnp.dot(imp, map_ref[...], precision=lax.Precision.HIGHEST, preferred_element_type=F32)[0:1, :]
        score_l = jnp.where(forced, FORCE, jnp.where(vis, p_slc, NEG))
        score_l = jnp.where(blk_l < ns, score_l, PAD_SCORE)
        score_s = jnp.sum(jnp.where(ii == jj, score_l, 0.0), axis=1, keepdims=True)
        beats = (score_l > score_s) | ((score_l == score_s) & (jj < ii))
        rank = jnp.sum(beats.astype(F32), axis=1, keepdims=True)
        ids = jnp.sum(jnp.where(rank == slot, blk_s, 0.0), axis=0, keepdims=True)
        ids_ref[0, g] = ids[:, :N_SEL].astype(jnp.int32)
        kw = kw_ref[0, :, g, :].astype(BF16)
        vw = vw_ref[0, :, g, :].astype(BF16)
        s = _dot_nt(qr_ref[0, g], kw) * SCALE
        e = jnp.exp(s - jnp.max(s, axis=1, keepdims=True))
        p = e * (1.0 / jnp.sum(e, axis=1, keepdims=True))
        owin_ref[0, g] = _dot(p.astype(BF16), vw)


def attn_dec_dense(q, qr, ck, cv, kw, vw, *, nc, ns, pos):
    B, G, R, _ = q.shape
    ncp = ck.shape[2]
    wb = kw.shape[1]
    nsl = -(-ns // LANE) * LANE
    omap = jnp.asarray(_overlap_map(ncp, nsl, ns))
    q_spec = pl.BlockSpec((1, G, R, HEAD_DIM), lambda b: (b, 0, 0, 0))
    c_spec = pl.BlockSpec((1, G, ncp, HEAD_DIM), lambda b: (b, 0, 0, 0))
    w_spec = pl.BlockSpec((1, wb, G, HEAD_DIM), lambda b: (b, 0, 0, 0))
    return pl.pallas_call(
        functools.partial(_attn_dec_dense_kernel, nc=nc, ns=ns, pos=pos),
        grid=(B,),
        in_specs=[q_spec, q_spec, c_spec, c_spec, w_spec, w_spec, pl.BlockSpec((ncp, nsl), lambda b: (0, 0))],
        out_specs=[q_spec, q_spec, pl.BlockSpec((1, G, 1, N_SEL), lambda b: (b, 0, 0, 0))],
        out_shape=[jax.ShapeDtypeStruct((B, G, R, HEAD_DIM), F32), jax.ShapeDtypeStruct((B, G, R, HEAD_DIM), F32),
                   jax.ShapeDtypeStruct((B, G, 1, N_SEL), jnp.int32)],
        compiler_params=_params(("parallel",)),
        name="attn_dec_dense",
    )(q, qr, ck, cv, kw, vw, omap)


def _attn_dec_sel_kernel(pt_ref, ids_ref, qr_ref, kn_ref, vn_ref, ocmp_ref, owin_ref, gate_ref, kc_hbm, vc_hbm,
                         o_ref, kbuf, vbuf, sem, *, n_past, pos, bpp):
    b = pl.program_id(0)
    G = qr_ref.shape[1]

    def block_copies(g, n):
        bid = jnp.minimum(ids_ref[(b * G + g) * N_SEL + n], n_past - 1)
        phys = pt_ref[b, bid // bpp] * bpp + bid % bpp
        slot = g * N_SEL + n
        return (pltpu.make_async_copy(kc_hbm.at[phys], kbuf.at[slot], sem.at[0]),
                pltpu.make_async_copy(vc_hbm.at[phys], vbuf.at[slot], sem.at[1]))

    pairs = [(g, n) for g in range(G) for n in range(N_SEL)]
    for i, (g, n) in enumerate(pairs):
        for c in block_copies(g, n):
            c.start(priority=i % 2)
    for g, n in pairs:
        for c in block_copies(g, n):
            c.wait()

    row = lax.broadcasted_iota(jnp.int32, (SEL_BLOCK, 1), 0)
    lane = lax.broadcasted_iota(jnp.int32, (1, SEL_BLOCK), 1)
    key_row = jnp.where(lane < SEL_BLOCK // 2, 2 * lane, 2 * lane - (SEL_BLOCK - 1))
    head_rows = lambda buf, slot, g: jnp.concatenate([buf[slot, :, g, :], buf[slot, :, G + g, :]], axis=0)
    for g in range(G):
        m = jnp.full((qr_ref.shape[2], 1), NEG, F32)
        l = jnp.zeros((qr_ref.shape[2], 1), F32)
        acc = jnp.zeros((qr_ref.shape[2], HEAD_DIM), F32)
        sl = slice(g * HEAD_DIM, (g + 1) * HEAD_DIM)
        for n in range(N_SEL):
            bid = ids_ref[(b * G + g) * N_SEL + n]
            is_new = bid >= n_past
            first = (row == 0) & (bid == n_past)
            slot = g * N_SEL + n
            k = jnp.where(is_new, jnp.where(first, kn_ref[0, :, sl], 0.0), head_rows(kbuf, slot, g)).astype(BF16)
            v = jnp.where(is_new, jnp.where(first, vn_ref[0, :, sl], 0.0), head_rows(vbuf, slot, g)).astype(BF16)
            ok = bid * SEL_BLOCK + key_row <= pos
            s = jnp.where(ok, _dot_nt(qr_ref[0, g], k) * SCALE, NEG)
            m_new = jnp.maximum(m, jnp.max(s, axis=1, keepdims=True))
            alpha = jnp.exp(m - m_new)
            p = jnp.where(ok, jnp.exp(s - m_new), 0.0)
            l = alpha * l + jnp.sum(p, axis=1, keepdims=True)
            acc = alpha * acc + _dot(p.astype(BF16), v)
            m = m_new
        gt = gate_ref[0, g]
        o_ref[0, g] = gt[:, 0:1] * ocmp_ref[0, g] + gt[:, 1:2] * (acc * (1.0 / l)) + gt[:, 2:3] * owin_ref[0, g]


def attn_dec_sel(table, ids, qr, k_cache, v_cache, k_new, v_new, ocmp, owin, gates, *, pos):
    B, G, R, _ = qr.shape
    page = k_cache.shape[1]
    bpp = page // SEL_BLOCK
    n_past = table.shape[1] * bpp
    blk = (SEL_BLOCK // 2, 2 * G, HEAD_DIM)
    kc = k_cache.reshape((k_cache.shape[0] * bpp,) + blk)
    vc = v_cache.reshape((v_cache.shape[0] * bpp,) + blk)
    q_spec = pl.BlockSpec((1, G, R, HEAD_DIM), lambda b, pt, ids_: (b, 0, 0, 0))
    n_spec = pl.BlockSpec((1, 1, G * HEAD_DIM), lambda b, pt, ids_: (b, 0, 0))
    g_spec = pl.BlockSpec((1, G, R, LANE), lambda b, pt, ids_: (b, 0, 0, 0))
    hbm = pl.BlockSpec(memory_space=pl.ANY)
    return pl.pallas_call(
        functools.partial(_attn_dec_sel_kernel, n_past=n_past, pos=pos, bpp=bpp),
        grid_spec=pltpu.PrefetchScalarGridSpec(
            num_scalar_prefetch=2,
            grid=(B,),
            in_specs=[q_spec, n_spec, n_spec, q_spec, q_spec, g_spec, hbm, hbm],
            out_specs=q_spec,
            scratch_shapes=[pltpu.VMEM((G * N_SEL,) + blk, F32), pltpu.VMEM((G * N_SEL,) + blk, F32),
                            pltpu.SemaphoreType.DMA((2,))],
        ),
        out_shape=jax.ShapeDtypeStruct((B, G, R, HEAD_DIM), F32),
        compiler_params=_params(("arbitrary",)),
        name="attn_dec_sel",
    )(table, ids.reshape(-1), qr, k_new, v_new, ocmp, owin, gates, kc, vc)


def _rope_tables(pos):
    half = HEAD_DIM // 2
    inv = ROPE_THETA ** (-jnp.arange(half, dtype=F32) / half)
    ang = pos.astype(F32)[:, None] * inv[None, :]
    cos, sin = jnp.cos(ang), jnp.sin(ang)
    return jnp.concatenate([cos, cos], axis=1), jnp.concatenate([-sin, sin], axis=1)


def _prep_weights(W):
    D = W['w_kv'].shape[0]
    H = D // HEAD_DIM
    R = H // N_KV_HEADS
    pg = D // len(POOL_WINDOWS)
    half = CMP_STRIDE * HEAD_DIM
    P = {}
    P['w_pool'] = W['w_pool'].astype(BF16).reshape(-1, len(POOL_WINDOWS) * pg, pg)
    kv_gain = jnp.ones((W['w_kv'].shape[1], KV_W), F32)
    kv_gain = kv_gain.at[2].set(jnp.tile(W['g_k_sel'], N_KV_HEADS)).at[4].set(jnp.tile(W['g_k_win'], N_KV_HEADS))
    P['kv_gain'] = kv_gain
    for t in ('k', 'v'):
        w1 = W['w_cmp_%s1' % t].astype(BF16)
        P['w_cmp_%s1' % t] = w1
        P['w_cmp_%scat' % t] = jnp.concatenate([w1[:half], w1[half:]], axis=1)
        P['w_cmp_%s2' % t] = W['w_cmp_%s2' % t].astype(BF16)
        P['pe_%s' % t] = W['pe_cmp_%s' % t].astype(BF16).reshape(1, -1)
    n_b = W['w_qg'].shape[0]
    wg = W['w_qg'][:, :, H * HEAD_DIM:].astype(BF16).reshape(n_b, D, N_KV_HEADS, R * N_BRANCH)
    wg = jnp.pad(wg, ((0, 0), (0, 0), (0, 0), (0, LANE - R * N_BRANCH)))
    P['w_gate'] = wg.reshape(n_b, D, N_KV_HEADS * LANE)
    P['w_ple'] = W['w_ple'].astype(BF16)
    P[('w_qg', 0)] = W['w_qg'][0, :, :H * HEAD_DIM].astype(BF16)
    return P


def _dense(x, wname, layer, W, P, *, N, epilogue, specs, extras=(), out_dtypes, side_cast=None, row_scale_in=None,
           norm_gains=None, name):
    M, K = x.shape
    shapes = [d if isinstance(d, jax.ShapeDtypeStruct) else jax.ShapeDtypeStruct((M, N), d) for d in out_dtypes]
    key = (wname, layer)
    tmx = 1024 if M >= 1024 else M
    use_mm = key in P
    if use_mm:
        grid, tm, tn, tk = _tiles(M, N, K, tmx, 1024, 2048 if (M >= 1024 and K > 4096) else 4096)
    else:
        grid, tm, tn, tk = _tiles(M, N, K, tmx, 512, K if K <= 4096 else 2048)
    extra_specs, out_specs = specs(tm, tn)
    extras = list(extras)
    if row_scale_in is not None:
        epilogue = _row_scaled(epilogue)
        extras.insert(0, row_scale_in)
        extra_specs = [pl.BlockSpec((tm, LANE), lambda i, j, k: (i, 0))] + list(extra_specs)
    if norm_gains:
        epilogue = _norm_producer(epilogue, len(extras))
        n_ex, n_ex_specs, n_shapes, n_specs = _norm_io(norm_gains, M, N, grid[1], tm, tn)
        extras, extra_specs = extras + n_ex, list(extra_specs) + n_ex_specs
        shapes, out_specs = shapes + n_shapes, list(out_specs) + n_specs
    side = None if side_cast is None else (W[side_cast[0]], side_cast[1])
    if use_mm:
        outs = list(mm(x, P[key], grid=grid, tm=tm, tn=tn, tk=tk, epilogue=epilogue, extras=extras,
                       extra_specs=extra_specs, out_shapes=shapes, out_specs=out_specs, side=side, name=name))
    else:
        w = W[wname]
        outs = list(mm_ws(x, w, layer=layer if w.ndim == 3 else None, side=side, grid=grid, tm=tm, tn=tn, tk=tk,
                          epilogue=epilogue, extras=extras, extra_specs=extra_specs, out_shapes=shapes,
                          out_specs=out_specs, name=name))
    if side is not None:
        P[side_cast] = outs.pop()
    if norm_gains:
        outs.append(row_scale(outs.pop(), N))
    return outs


def _ffn_ple(h, normed, p_l, layer, W, P, next_gains):
    M, D = h.shape
    F = W['w_up'].shape[2]
    host = M >= 1024
    next_up = ('w_up', layer + 1) if host and layer + 1 < W['w_up'].shape[0] else None
    mn = lambda tm, tn: ([], [_mn_spec(tm, tn)])
    res = lambda tm, tn: ([_mn_spec(tm, tn)], [_mn_spec(tm, tn)])
    y, rs = normed
    (u,) = _dense(y, 'w_up', layer, W, P, N=F, epilogue=lambda acc, rows: (jnp.square(jnp.maximum(acc, 0.0)),),
                  specs=mn, out_dtypes=[BF16], side_cast=('w_down', layer) if host else None, row_scale_in=rs,
                  name="ffn_up")
    h, y, rs = _dense(u, 'w_down', layer, W, P, N=D, epilogue=lambda acc, rows, r: (r[rows, :] + acc,), specs=res,
                      extras=[h], out_dtypes=[F32], norm_gains=[W['g_ple'][layer]], side_cast=next_up,
                      name="ffn_down")
    ple_dim = p_l.shape[1]
    ple_specs = lambda tm, tn: ([_mn_spec(tm, tn), pl.BlockSpec((tm, ple_dim), lambda i, j, k: (i, 0)),
                                 pl.BlockSpec((ple_dim, tn), lambda i, j, k: (0, j))], [_mn_spec(tm, tn)])
    return _dense(y, 'w_ple_gate', layer, W, P, N=D,
                  epilogue=lambda acc, rows, r, pp, wp: (r[rows, :] + _dot(pp[rows, :], wp[...]) * _sigmoid(acc),),
                  specs=ple_specs, extras=[h, p_l.astype(BF16), P['w_ple'][layer]], out_dtypes=[F32],
                  row_scale_in=rs, norm_gains=next_gains, name="ple")


def _kv_epilogue(acc, rows, gain_ref, cos_ref, sin_ref):
    j = pl.program_id(0)
    cosf, sinf = cos_ref[rows, :], sin_ref[rows, :]
    heads = []
    for hh in range(N_KV_HEADS):
        sl = slice(hh * HEAD_DIM, (hh + 1) * HEAD_DIM)
        heads.append(_rope(_head_norm(acc[:, sl], gain_ref[0, :, sl]), cosf, sinf))
    out = jnp.where((j == 2) | (j == 4), jnp.concatenate(heads, axis=1), acc)
    return out, out


def _q_epilogue(acc, rows, gq_ref, cos_ref, sin_ref):
    cosf, sinf = cos_ref[rows, :], sin_ref[rows, :]
    qs, qrs = [], []
    for hh in range(acc.shape[1] // HEAD_DIM):
        qn = _head_norm(acc[:, hh * HEAD_DIM:(hh + 1) * HEAD_DIM], gq_ref[...])
        qs.append(qn)
        qrs.append(_rope(qn, cosf, sinf))
    return jnp.concatenate(qs, axis=1), jnp.concatenate(qrs, axis=1)


def _trunk(x, p, pool_prefix, pos0, W, P, attend):
    B, T, D = x.shape
    M = B * T
    tmx = 1024 if M >= 1024 else M
    h = x.reshape(M, D)
    pg = D // len(POOL_WINDOWS)

    first_up = ('w_up', 0)
    d, pool_new, cast = pool_diff(x, W['g_mix'][0], pool_prefix[0], pos0,
                                  side=(W['w_up'], 0) if first_up not in P else None)
    if cast is not None:
        P[first_up] = cast
    d, pool_new = d.reshape(M, D), pool_new[None]
    grid, tm, tn, tk = _tiles(M, D, pg, tmx, pg, pg)
    n_ex, n_ex_specs, n_shapes, n_specs = _norm_io([W['g_ffn'][0]], M, D, grid[1], tm, tn)
    h, y, ssq = mm(d, P['w_pool'][0], grid=grid, tm=tm, tn=tn, tk=tk,
                   x_map=lambda i, j, k: (i, j), w_map=lambda i, j, k: (j, 0),
                   epilogue=_norm_producer(lambda acc, rows, sc, r: (r[rows, :] + acc * sc[...],), 2),
                   extras=[W['pool_scale'][0].reshape(1, D), h] + n_ex,
                   extra_specs=[pl.BlockSpec((1, tn), lambda i, j, k: (0, j)), _mn_spec(tm, tn)] + n_ex_specs,
                   out_shapes=[jax.ShapeDtypeStruct((M, D), F32)] + n_shapes,
                   out_specs=[_mn_spec(tm, tn)] + n_specs, name="pool_mix")
    h, hkv, a1, rs = _ffn_ple(h, (y, row_scale(ssq, D)), p[0].reshape(M, -1), 0, W, P,
                              [W['g_kv'], W['g_mix'][1]])

    pos = pos0 + jnp.tile(jnp.arange(T, dtype=jnp.int32), B)
    cosf, sinf = _rope_tables(pos)
    n_kv = W['w_kv'].shape[1]
    rope_spec = lambda tm: pl.BlockSpec((tm, HEAD_DIM), lambda i, j, k: (i, 0))
    kv_spec = lambda tm: pl.BlockSpec((1, tm, KV_W), lambda i, j, k: (j, i, 0))
    kv4_spec = lambda tm: pl.BlockSpec((1, tm, N_KV_HEADS, HEAD_DIM), lambda i, j, k: (j, i, 0, 0))
    kv_specs = lambda tm, tn: ([pl.BlockSpec((1, 1, KV_W), lambda i, j, k: (j, 0, 0)), rope_spec(tm), rope_spec(tm)],
                               [kv4_spec(tm), kv_spec(tm)])
    kv, kv_b = _dense(hkv, 'w_kv2d', None, W, P, N=n_kv * KV_W, epilogue=_kv_epilogue, specs=kv_specs,
                      extras=[P['kv_gain'].reshape(n_kv, 1, KV_W), cosf, sinf],
                      out_dtypes=[jax.ShapeDtypeStruct((n_kv, M, N_KV_HEADS, HEAD_DIM), F32),
                                  jax.ShapeDtypeStruct((n_kv, M, KV_W), BF16)], row_scale_in=rs, name="kv_proj")
    q_specs = lambda tm, tn: ([pl.BlockSpec((1, HEAD_DIM), lambda i, j, k: (0, 0)), rope_spec(tm), rope_spec(tm)],
                              [_mn_spec(tm, tn)] * 2)
    q, qr = _dense(a1, 'w_qg', 0, W, P, N=D, epilogue=_q_epilogue, specs=q_specs,
                   extras=[W['g_q'][0].reshape(1, HEAD_DIM), cosf, sinf], out_dtypes=[BF16, BF16], row_scale_in=rs,
                   name="q_proj")
    ng = N_KV_HEADS * LANE
    grid, tm, tn, tk = _tiles(M, ng, D, tmx, ng, D)
    (gates,) = mm(a1, P['w_gate'][0], grid=grid, tm=tm, tn=tn, tk=tk,
                  epilogue=_row_scaled(lambda acc, rows: (_sigmoid(acc),)), extras=[rs],
                  extra_specs=[pl.BlockSpec((tm, LANE), lambda i, j, k: (i, 0))],
                  out_shapes=[jax.ShapeDtypeStruct((M, ng), F32)], out_specs=[_mn_spec(tm, tn)], name="gate_proj")

    o, win_state = attend(kv, kv_b, q, qr, gates)

    h, y, rs = _dense(o, 'w_o', 0, W, P, N=D, epilogue=lambda acc, rows, r: (r[rows, :] + acc,), extras=[h],
                      specs=lambda tm, tn: ([_mn_spec(tm, tn)], [_mn_spec(tm, tn)]), out_dtypes=[F32],
                      norm_gains=[W['g_ffn'][1]], name="attn_out")
    (h,) = _ffn_ple(h, (y, rs), p[1].reshape(M, -1), 1, W, P, [])
    rows = tuple(kv[n].reshape(B, T, N_KV_HEADS, HEAD_DIM) for n in range(4))
    return h.reshape(B, T, D), pool_new, rows, win_state


def _compress(k_pages, v_pages, table, cp, W, P):
    abk, abv = cmp_partials(k_pages, v_pages, table, P['w_cmp_kcat'], P['w_cmp_vcat'], cp)
    gain = W['g_k_cmp'].reshape(1, HEAD_DIM)
    ck = cmp_finish(abk, P['pe_k'], P['w_cmp_k1'], P['w_cmp_k2'], gain, norm=True)
    cv = cmp_finish(abv, P['pe_v'], P['w_cmp_v1'], P['w_cmp_v2'], gain, norm=False)
    return ck, cv


def kernel(x_prompt, x_sample, state_pool, cache_k_cmp, cache_v_cmp, cache_k_sel, cache_v_sel, state_k_win, state_v_win, page_table, p_prompt, p_sample, g_mix, w_pool, pool_scale, g_kv, w_kv, g_k_cmp, g_k_sel, g_k_win, w_cmp_k1, w_cmp_k2, pe_cmp_k, w_cmp_v1, w_cmp_v2, pe_cmp_v, w_qg, g_q, w_o, g_ffn, w_up, w_down, g_ple, w_ple, w_ple_gate):
    W = dict(g_mix=g_mix, w_pool=w_pool, pool_scale=pool_scale, g_kv=g_kv, w_kv=w_kv, g_k_cmp=g_k_cmp,
             g_k_sel=g_k_sel, g_k_win=g_k_win, w_cmp_k1=w_cmp_k1, w_cmp_k2=w_cmp_k2, pe_cmp_k=pe_cmp_k,
             w_cmp_v1=w_cmp_v1, w_cmp_v2=w_cmp_v2, pe_cmp_v=pe_cmp_v, w_qg=w_qg, g_q=g_q, w_o=w_o,
             g_ffn=g_ffn, w_up=w_up, w_down=w_down, g_ple=g_ple, w_ple=w_ple, w_ple_gate=w_ple_gate)
    P = _prep_weights(W)
    W['w_kv2d'] = w_kv.reshape(w_kv.shape[0], -1)
    Bp, Tp, D = x_prompt.shape
    Bs, Ts, _ = x_sample.shape
    assert Ts == 1, "the decode path handles one new token per sequence"
    page = cache_k_cmp.shape[1]
    past_len = page_table.shape[1] * page
    R = D // HEAD_DIM // N_KV_HEADS
    assert Tp % page == 0 and past_len % SEL_BLOCK == 0

    def attend_prompt(kv, kv_b, q, qr, gates):
        ppb = Tp // page
        table = jnp.arange(Bp * ppb, dtype=jnp.int32).reshape(Bp, ppb)
        pages = lambda a: a.reshape(-1, page, N_KV_HEADS, HEAD_DIM)
        ck, cv = _compress(pages(kv[0]), pages(kv[1]), table, ppb, W, P)
        nc = Tp // CMP_STRIDE - CMP_BLOCK // CMP_STRIDE + 1
        seq = lambda a: a.reshape(Bp, Tp, -1)
        o = attn_prompt(seq(q), seq(qr), ck, cv, kv_b.reshape(-1, Bp, Tp, KV_W), seq(gates), nc=nc)
        nw = min(WINDOW, Tp)
        win = tuple(kv[n].reshape(Bp, Tp, N_KV_HEADS, HEAD_DIM)[:, -nw:] for n in (4, 5))
        return o.reshape(Bp * Tp, D), win

    def attend_sample(kv, kv_b, q, qr, gates):
        del kv_b
        ck, cv = _compress(cache_k_cmp, cache_v_cmp, page_table, min(32, page_table.shape[1]), W, P)
        nc = (past_len - (CMP_BLOCK - 1)) // CMP_STRIDE + 1
        ns = past_len // SEL_BLOCK + 1
        wb = state_k_win.shape[1]
        new_row = lambda a: a.reshape(Bs, 1, N_KV_HEADS, HEAD_DIM)
        kw = jnp.concatenate([state_k_win, new_row(kv[4])], axis=1)[:, -wb:]
        vw = jnp.concatenate([state_v_win, new_row(kv[5])], axis=1)[:, -wb:]
        heads = lambda a: a.reshape(Bs, N_KV_HEADS, R, HEAD_DIM)
        ocmp, owin, ids = attn_dec_dense(heads(q), heads(qr), ck, cv, kw, vw, nc=nc, ns=ns, pos=past_len)
        gt = gates.reshape(Bs, N_KV_HEADS, LANE)[:, :, :R * N_BRANCH].reshape(Bs, N_KV_HEADS, R, N_BRANCH)
        gt = jnp.pad(gt, ((0, 0), (0, 0), (0, 0), (0, LANE - N_BRANCH)))
        o = attn_dec_sel(page_table, ids, heads(qr), cache_k_sel, cache_v_sel,
                         kv[2].reshape(Bs, 1, KV_W), kv[3].reshape(Bs, 1, KV_W), ocmp, owin, gt, pos=past_len)
        return o.reshape(Bs, D).astype(BF16), (kw, vw)

    pool_zero = jnp.zeros((state_pool.shape[0], Bp, POOL_STATE, D), x_prompt.dtype)
    y_p, pool_p, rows_p, win_p = _trunk(x_prompt, p_prompt, pool_zero, 0, W, P, attend_prompt)
    y_s, pool_s, rows_s, win_s = _trunk(x_sample, p_sample, state_pool, past_len, W, P, attend_sample)
    return (y_p, y_s, pool_p, pool_s, rows_p[0], rows_p[1], rows_p[2], rows_p[3], win_p[0], win_p[1],
            rows_s[0], rows_s[1], rows_s[2], rows_s[3], win_s[0], win_s[1])
```
